```python
import math
import jax, jax.numpy as jnp
from jax import lax
import numpy as np

D_MODEL = 1024
BATCH = 8
SEQ = 4096
DEPTH = 1

MIX_WIDTH = D_MODEL
RWKV_WIDTH = MIX_WIDTH // 2
RWKV_HEAD_DIM = 64
RWKV_HEADS = RWKV_WIDTH // RWKV_HEAD_DIM
DECAY_LORA = 64
ICLR_LORA = 64
GATE_LORA = 128
RWKV_COLS = 3 * RWKV_WIDTH + DECAY_LORA + ICLR_LORA + GATE_LORA
DIFF_WIDTH = MIX_WIDTH - RWKV_WIDTH
DIFF_HEAD_DIM = 64
DIFF_HEADS = DIFF_WIDTH // (2 * DIFF_HEAD_DIM)
DIFF_COLS = 3 * DIFF_WIDTH
IN_COLS = RWKV_COLS + DIFF_COLS
Q_BLOCK = 128
N_EXPERTS = 32
TOP_K = 4
D_EXPERT = D_MODEL
SWIGLU_LIMIT = 7.0
SWIGLU_ALPHA = 1.702
MOE_BLOCK = 512
NORM_EPS = 1e-5
GN_EPS = 64e-5

kernel_name = "hymba_rwkv7_diffattn_alibi_moe_adaln"


def rms_norm(x, gain, eps=NORM_EPS):
    xf = x.astype(jnp.float32)
    y = xf * lax.rsqrt(jnp.mean(xf * xf, axis=-1, keepdims=True) + eps)
    return (y * gain.astype(jnp.float32)).astype(x.dtype)


def token_shift_lerp(y, mu):
    prev = jnp.pad(y, ((0, 0), (1, 0), (0, 0)))[:, :-1]
    return y + mu * (prev - y)


def alibi_slopes(n_heads):
    return jnp.asarray([2.0 ** (-8.0 * (i + 1) / n_heads) for i in range(n_heads)], jnp.float32)


def rwkv7_mixer(p, shift_mu, w0, w_up, a0, a_up, g_up, k_k, k_a, r_k, gn_gain, gn_bias):
    f32 = jnp.float32
    B, S, _ = p.shape
    H, Dh = RWKV_HEADS, RWKV_HEAD_DIM
    p = token_shift_lerp(p.astype(f32), shift_mu.astype(f32))
    splits = np.cumsum([RWKV_WIDTH, RWKV_WIDTH, RWKV_WIDTH, DECAY_LORA, ICLR_LORA]).tolist()
    r, k, v, w_lo, a_lo, g_lo = jnp.split(p, splits, axis=-1)
    w = -jax.nn.softplus(-(w0.astype(f32) + jnp.tanh(w_lo) @ w_up.astype(f32))) - 0.5
    decay = jnp.exp(-jnp.exp(w))
    a = jax.nn.sigmoid(a0.astype(f32) + a_lo @ a_up.astype(f32))
    g = jax.nn.sigmoid(g_lo) @ g_up.astype(f32)
    kk = (k * k_k.astype(f32)).reshape(B, S, H, Dh)
    kk = kk / jnp.maximum(jnp.linalg.norm(kk, axis=-1, keepdims=True), 1e-12)
    k = k * (1.0 + (a - 1.0) * k_a.astype(f32))
    rh, kh, vh = (t.reshape(B, S, H, Dh) for t in (r, k, v))
    wh = decay.reshape(B, S, H, Dh)
    ah = a.reshape(B, S, H, Dh)
    a_vec = -kk
    b_vec = kk * ah

    def step(state, inp):
        r_t, k_t, v_t, w_t, a_t, b_t = inp
        sa = jnp.einsum('bhvk,bhk->bhv', state, a_t)
        state = (state * w_t[:, :, None, :] + sa[..., None] * b_t[:, :, None, :]
                 + v_t[..., None] * k_t[:, :, None, :])
        return state, jnp.einsum('bhvk,bhk->bhv', state, r_t)

    xs = tuple(jnp.moveaxis(t, 1, 0) for t in (rh, kh, vh, wh, a_vec, b_vec))
    state0 = jnp.zeros((B, H, Dh, Dh), f32)
    _, ys = lax.scan(step, state0, xs)
    y = jnp.moveaxis(ys, 0, 1)
    mean = jnp.mean(y, axis=-1, keepdims=True)
    var = jnp.mean(jnp.square(y - mean), axis=-1, keepdims=True)
    y = ((y - mean) * lax.rsqrt(var + GN_EPS)).reshape(B, S, RWKV_WIDTH)
    y = y * gn_gain.astype(f32) + gn_bias.astype(f32)
    bonus = jnp.sum(rh * kh * r_k.astype(f32), axis=-1, keepdims=True) * vh
    y = (y + bonus.reshape(B, S, RWKV_WIDTH)) * g
    return y


def diff_attention(p, lambda_init, lambda_q1, lambda_k1, lambda_q2, lambda_k2, subln_gain):
    f32 = jnp.float32
    B, S, _ = p.shape
    H, Dh = DIFF_HEADS, DIFF_HEAD_DIM
    nb = S // Q_BLOCK
    q, k, v = jnp.split(p, 3, axis=-1)
    q = q.reshape(B, S, H, 2, Dh)
    k = k.reshape(B, S, H, 2, Dh)
    v = v.reshape(B, S, H, 2 * Dh)
    scale = 1.0 / math.sqrt(Dh)
    lam = (jnp.exp(jnp.sum(lambda_q1.astype(f32) * lambda_k1.astype(f32)))
           - jnp.exp(jnp.sum(lambda_q2.astype(f32) * lambda_k2.astype(f32))) + lambda_init)
    slopes = alibi_slopes(H)
    qb = q.reshape(B, nb, Q_BLOCK, H, 2, Dh).transpose(1, 0, 3, 4, 2, 5)
    kt = k.transpose(0, 2, 3, 1, 4)
    vt = v.transpose(0, 2, 1, 3)
    key_pos = jnp.arange(S, dtype=jnp.int32)
    starts = jnp.arange(nb, dtype=jnp.int32) * Q_BLOCK

    def attend(args):
        q_blk, start = args
        q_pos = start + jnp.arange(Q_BLOCK, dtype=jnp.int32)
        dist = (q_pos[:, None] - key_pos[None, :]).astype(f32)
        s = jnp.einsum('bhcqd,bhcsd->bhcqs', q_blk, kt).astype(f32) * scale
        s = s - slopes[None, :, None, None, None] * dist
        s = jnp.where(dist >= 0, s, -jnp.inf)
        pr = jax.nn.softmax(s, axis=-1)
        attn = pr[:, :, 0] - lam * pr[:, :, 1]
        return jnp.einsum('bhqs,bhsd->bhqd', attn.astype(vt.dtype), vt)

    out = lax.map(attend, (qb, starts))
    out = out.transpose(1, 0, 3, 2, 4).reshape(B, S, H, 2 * Dh)
    out = rms_norm(out, subln_gain) * (1.0 - lambda_init)
    return out.reshape(B, S, DIFF_WIDTH).astype(p.dtype)


def moe_ffn(h, router_w, router_b, w_gate_up, b_gate_up, w_down, b_down):
    B, S, D = h.shape
    T = B * S
    A = T * TOP_K
    hf = h.reshape(T, D)
    logits = (hf @ router_w + router_b).astype(jnp.float32)
    top_logits, top_idx = lax.top_k(logits, TOP_K)
    top_w = jax.nn.softmax(top_logits, axis=-1)
    e_flat = top_idx.reshape(A).astype(jnp.int32)
    w_flat = top_w.reshape(A)
    tok_flat = jnp.arange(A, dtype=jnp.int32) // TOP_K
    order = jnp.argsort(e_flat)
    e_s, tok_s, w_s = e_flat[order], tok_flat[order], w_flat[order]
    counts = jnp.bincount(e_flat, length=N_EXPERTS)
    starts = jnp.cumsum(counts) - counts
    padded = ((counts + MOE_BLOCK - 1) // MOE_BLOCK) * MOE_BLOCK
    pad_ends = jnp.cumsum(padded)
    pad_starts = pad_ends - padded
    slot = pad_starts[e_s] + (jnp.arange(A, dtype=jnp.int32) - starts[e_s])
    n_blocks = -(-(A + N_EXPERTS * (MOE_BLOCK - 1)) // MOE_BLOCK)
    P = n_blocks * MOE_BLOCK
    slot_tok = jnp.zeros((P,), jnp.int32).at[slot].set(tok_s)
    slot_w = jnp.zeros((P,), jnp.float32).at[slot].set(w_s)
    block_e = jnp.minimum(
        jnp.searchsorted(pad_ends, jnp.arange(n_blocks, dtype=pad_ends.dtype) * MOE_BLOCK, side='right'),
        N_EXPERTS - 1).astype(jnp.int32)

    def expert_block(args):
        tok, wt, e = args
        xb = hf[tok]
        gu = xb @ w_gate_up[e] + b_gate_up[e]
        gate, up = gu[:, :D_EXPERT], gu[:, D_EXPERT:]
        gate = jnp.minimum(gate, SWIGLU_LIMIT)
        up = jnp.clip(up, -SWIGLU_LIMIT, SWIGLU_LIMIT)
        act = (up + 1.0) * (gate * jax.nn.sigmoid(SWIGLU_ALPHA * gate))
        y = act @ w_down[e] + b_down[e]
        return y * wt[:, None].astype(y.dtype)

    ys = lax.map(expert_block, (slot_tok.reshape(n_blocks, MOE_BLOCK),
                                slot_w.reshape(n_blocks, MOE_BLOCK), block_e))
    out = jnp.zeros((T, D), ys.dtype).at[slot_tok].add(ys.reshape(P, D))
    return out.reshape(B, S, D).astype(h.dtype)


def hybrid_layer(x, c, lambda_init, mod_w, mod_b, norm1_gain, w_in, rwkv_shift_mu, rwkv_w0,
                 rwkv_w_up, rwkv_a0, rwkv_a_up, rwkv_g_up, rwkv_k_k, rwkv_k_a, rwkv_r_k,
                 rwkv_gn_gain, rwkv_gn_bias, diff_lambda_q1, diff_lambda_k1, diff_lambda_q2,
                 diff_lambda_k2, diff_subln_gain, w_out, norm2_gain, router_w, router_b,
                 w_gate_up, b_gate_up, w_down, b_down):
    mod = jax.nn.silu(c) @ mod_w + mod_b
    shift1, scale1, gate1, shift2, scale2, gate2 = jnp.split(mod[:, None, :], 6, axis=-1)
    h = rms_norm(x, norm1_gain) * (1.0 + scale1) + shift1
    proj = h @ w_in
    y_rwkv = rwkv7_mixer(proj[..., :RWKV_COLS], rwkv_shift_mu, rwkv_w0, rwkv_w_up, rwkv_a0,
                         rwkv_a_up, rwkv_g_up, rwkv_k_k, rwkv_k_a, rwkv_r_k, rwkv_gn_gain,
                         rwkv_gn_bias).astype(x.dtype)
    y_diff = diff_attention(proj[..., RWKV_COLS:], lambda_init, diff_lambda_q1, diff_lambda_k1,
                            diff_lambda_q2, diff_lambda_k2, diff_subln_gain)
    mix = jnp.concatenate([y_rwkv, y_diff], axis=-1) @ w_out
    x = x + gate1 * mix
    h = rms_norm(x, norm2_gain) * (1.0 + scale2) + shift2
    x = x + gate2 * moe_ffn(h, router_w, router_b, w_gate_up, b_gate_up, w_down, b_down)
    return x


def setup_inputs(seed: int = 0) -> dict:
    key = jax.random.key(seed)
    ks = jax.random.split(key, 40)
    L, D = DEPTH, D_MODEL
    nrm = lambda k, shape, s: jax.random.normal(k, shape, jnp.float32) * s
    return {
        "x": nrm(ks[0], (BATCH, SEQ, D), 1.0),
        "c": nrm(ks[1], (BATCH, D), 1.0),
        "mod_w": nrm(ks[2], (L, D, 6 * D), 0.5 * D ** -0.5),
        "mod_b": nrm(ks[3], (L, 6 * D), 0.02),
        "norm1_gain": 1.0 + nrm(ks[4], (L, D), 0.01),
        "w_in": nrm(ks[5], (L, D, IN_COLS), D ** -0.5),
        "rwkv_shift_mu": jax.random.uniform(ks[6], (L, RWKV_COLS), jnp.float32),
        "rwkv_w0": nrm(ks[7], (L, RWKV_WIDTH), 0.5),
        "rwkv_w_up": nrm(ks[8], (L, DECAY_LORA, RWKV_WIDTH), 0.5 * DECAY_LORA ** -0.5),
        "rwkv_a0": nrm(ks[9], (L, RWKV_WIDTH), 0.5),
        "rwkv_a_up": nrm(ks[10], (L, ICLR_LORA, RWKV_WIDTH), 0.5 * ICLR_LORA ** -0.5),
        "rwkv_g_up": nrm(ks[11], (L, GATE_LORA, RWKV_WIDTH), GATE_LORA ** -0.5),
        "rwkv_k_k": 0.85 + nrm(ks[12], (L, RWKV_WIDTH), 0.05),
        "rwkv_k_a": 1.0 + nrm(ks[13], (L, RWKV_WIDTH), 0.05),
        "rwkv_r_k": nrm(ks[14], (L, RWKV_HEADS, RWKV_HEAD_DIM), 0.1),
        "rwkv_gn_gain": 1.0 + nrm(ks[15], (L, RWKV_WIDTH), 0.01),
        "rwkv_gn_bias": nrm(ks[16], (L, RWKV_WIDTH), 0.01),
        "diff_lambda_q1": nrm(ks[17], (L, DIFF_HEAD_DIM), 0.1),
        "diff_lambda_k1": nrm(ks[18], (L, DIFF_HEAD_DIM), 0.1),
        "diff_lambda_q2": nrm(ks[19], (L, DIFF_HEAD_DIM), 0.1),
        "diff_lambda_k2": nrm(ks[20], (L, DIFF_HEAD_DIM), 0.1),
        "diff_subln_gain": 1.0 + nrm(ks[21], (L, 2 * DIFF_HEAD_DIM), 0.01),
        "w_out": nrm(ks[22], (L, MIX_WIDTH, D), MIX_WIDTH ** -0.5),
        "norm2_gain": 1.0 + nrm(ks[23], (L, D), 0.01),
        "router_w": nrm(ks[24], (L, D, N_EXPERTS), D ** -0.5),
        "router_b": nrm(ks[25], (L, N_EXPERTS), 0.01),
        "w_gate_up": nrm(ks[26], (L, N_EXPERTS, D, 2 * D_EXPERT), D ** -0.5),
        "b_gate_up": nrm(ks[27], (L, N_EXPERTS, 2 * D_EXPERT), 0.01),
        "w_down": nrm(ks[28], (L, N_EXPERTS, D_EXPERT, D), D_EXPERT ** -0.5),
        "b_down": nrm(ks[29], (L, N_EXPERTS, D), 0.01),
        "final_gain": 1.0 + nrm(ks[30], (D,), 0.01),
    }


def reference(x, c, mod_w, mod_b, norm1_gain, w_in, rwkv_shift_mu, rwkv_w0, rwkv_w_up, rwkv_a0,
              rwkv_a_up, rwkv_g_up, rwkv_k_k, rwkv_k_a, rwkv_r_k, rwkv_gn_gain, rwkv_gn_bias,
              diff_lambda_q1, diff_lambda_k1, diff_lambda_q2, diff_lambda_k2, diff_subln_gain,
              w_out, norm2_gain, router_w, router_b, w_gate_up, b_gate_up, w_down, b_down,
              final_gain):
    for l in range(DEPTH):
        lambda_init = 0.8 - 0.6 * math.exp(-0.3 * l)
        x = hybrid_layer(
            x, c, lambda_init, mod_w[l], mod_b[l], norm1_gain[l], w_in[l], rwkv_shift_mu[l],
            rwkv_w0[l], rwkv_w_up[l], rwkv_a0[l], rwkv_a_up[l], rwkv_g_up[l], rwkv_k_k[l],
            rwkv_k_a[l], rwkv_r_k[l], rwkv_gn_gain[l], rwkv_gn_bias[l], diff_lambda_q1[l],
            diff_lambda_k1[l], diff_lambda_q2[l], diff_lambda_k2[l], diff_subln_gain[l],
            w_out[l], norm2_gain[l], router_w[l], router_b[l], w_gate_up[l], b_gate_up[l],
            w_down[l], b_down[l])
    return rms_norm(x, final_gain)
```

```python
import functools
import math

import jax
import jax.numpy as jnp
from jax import lax
from jax.experimental import pallas as pl
from jax.experimental.pallas import tpu as pltpu

F32 = jnp.float32
BF16 = jnp.bfloat16
I32 = jnp.int32
U32 = jnp.uint32

D_MODEL = 1024
RWKV_WIDTH = 512
RWKV_HEADS = 8
HEAD_DIM = 64
RWKV_COLS = 3 * RWKV_WIDTH + 64 + 64 + 128
DIFF_WIDTH = 512
DIFF_HEADS = 4
DIFF_COLS = 3 * DIFF_WIDTH
N_EXPERTS = 32
TOP_K = 4
D_EXPERT = 1024
SWIGLU_LIMIT = 7.0
SWIGLU_ALPHA = 1.702
MOE_BLOCK = 512
NORM_EPS = 1e-5
GN_EPS = 64e-5
LAMBDA_INIT = 0.8 - 0.6 * math.exp(-0.3 * 0)

CHUNK = 64
GROUP = 4
GW = GROUP * HEAD_DIM
VMEM_LIMIT = 56 * 1024 * 1024


def _dot(a, b):
    return jnp.dot(a, b, preferred_element_type=F32)


def _dot_nt(a, b):
    return lax.dot_general(a, b, (((1,), (1,)), ((), ())), preferred_element_type=F32)


def _dot_tn(a, b):
    return lax.dot_general(a, b, (((0,), (0,)), ((), ())), preferred_element_type=F32)


def _split_dot(x, w_bf16):
    hi = x.astype(BF16)
    lo = (x - hi.astype(F32)).astype(BF16)
    return _dot(hi, w_bf16) + _dot(lo, w_bf16)


def _mod_kernel(c_ref, w_ref, b_ref, o_ref):
    c = c_ref[...]
    s = c * jax.nn.sigmoid(c)
    o_ref[...] = _dot(s, w_ref[...]) + b_ref[...]


def _mod_call(c, mod_w, mod_b):
    B = c.shape[0]
    n = mod_w.shape[1]
    tn = 1536
    return pl.pallas_call(
        _mod_kernel,
        grid=(n // tn,),
        in_specs=[pl.BlockSpec((B, D_MODEL), lambda j: (0, 0)),
                  pl.BlockSpec((D_MODEL, tn), lambda j: (0, j)),
                  pl.BlockSpec((1, tn), lambda j: (0, j))],
        out_specs=pl.BlockSpec((B, tn), lambda j: (0, j)),
        out_shape=jax.ShapeDtypeStruct((B, n), F32),
        compiler_params=pltpu.CompilerParams(
            dimension_semantics=("arbitrary",), vmem_limit_bytes=VMEM_LIMIT),
        name="mod",
    )(c, mod_w, mod_b.reshape(1, n))


def _inproj_kernel(x_ref, mod_ref, g1_ref, wrw_ref, wat_ref, mu_ref, w0_ref, wup_ref,
                   a0_ref, aup_ref, gup_ref, kk_ref, ka_ref, hsum_ref,
                   r_out, k_out, v_out, a_out, b_out, g_out, lw_out,
                   q_out, kat_out, vt_out, carry_ref, *, tm):
    s = pl.program_id(1)
    x = x_ref[0]
    ms = jnp.mean(x * x, axis=-1, keepdims=True)
    shift1 = mod_ref[0, 0:1, :]
    scale1 = mod_ref[0, 1:2, :]
    h = x * lax.rsqrt(ms + NORM_EPS) * g1_ref[...] * (1.0 + scale1) + shift1
    hb = h.astype(BF16)

    p = _dot(hb, wrw_ref[...])

    @pl.when(s == 0)
    def _():
        carry_ref[...] = jnp.zeros_like(carry_ref)

    rolled = pltpu.roll(p, shift=1, axis=0)
    row = lax.broadcasted_iota(I32, p.shape, 0)
    prev = jnp.where(row == 0, carry_ref[...], rolled)
    carry_ref[...] = p[tm - 1:tm, :]
    ps = p + mu_ref[...] * (prev - p)

    r = ps[:, 0:512]
    k = ps[:, 512:1024]
    v = ps[:, 1024:1536]
    lo2 = ps[:, 1536:1664]
    g_lo = ps[:, 1664:1792]
    z = w0_ref[...] + _dot(jnp.tanh(lo2).astype(BF16), wup_ref[...])
    nz = -z
    softplus = jnp.maximum(nz, 0.0) + jnp.log(1.0 + jnp.exp(-jnp.abs(nz)))
    w = -softplus - 0.5
    lw_out[0] = -jnp.exp(w)
    a = jax.nn.sigmoid(a0_ref[...] + _dot(lo2.astype(BF16), aup_ref[...]))
    g = _dot(jax.nn.sigmoid(g_lo).astype(BF16), gup_ref[...])
    kk = k * kk_ref[...]
    ssq = _split_dot(kk * kk, hsum_ref[...])
    kk = kk / jnp.maximum(jnp.sqrt(ssq), 1e-12)
    k = k * (1.0 + (a - 1.0) * ka_ref[...])
    r_out[0] = r.astype(BF16)
    k_out[0] = k.astype(BF16)
    v_out[0] = v.astype(BF16)
    a_out[0] = (-kk).astype(BF16)
    b_out[0] = (kk * a).astype(BF16)
    g_out[0] = g.astype(BF16)

    pa = _dot(hb, wat_ref[...])
    q_out[0] = (pa[:, 0:512] * (1.0 / math.sqrt(HEAD_DIM))).astype(BF16)
    kat_out[0] = pa[:, 512:1024].astype(BF16)
    for hh in range(DIFF_HEADS):
        vh = pa[:, 1024 + hh * 128:1024 + (hh + 1) * 128]
        vt_out[0, hh, 0] = vh.T.astype(BF16)


def _head_sum_matrix(width):
    i = jnp.arange(width) // HEAD_DIM
    return (i[:, None] == i[None, :]).astype(BF16)


def _inproj_call(x, mod6, norm1_gain, w_in, mu, w0, w_up, a0, a_up, g_up, k_k, k_a, tm):
    B, S, _ = x.shape
    ns = S // tm
    w_rw = w_in[:, :RWKV_COLS].astype(BF16)
    w_at = w_in[:, RWKV_COLS:].astype(BF16)
    zeros = jnp.zeros((64, RWKV_WIDTH), F32)
    wup_p = jnp.concatenate([w_up, zeros], axis=0).astype(BF16)
    aup_p = jnp.concatenate([zeros, a_up], axis=0).astype(BF16)
    row = lambda v: v.reshape(1, -1)
    full = lambda shape: pl.BlockSpec(shape, lambda b, s: (0,) * len(shape))
    tok = lambda w: pl.BlockSpec((1, tm, w), lambda b, s: (b, s, 0))
    rw_shape = jax.ShapeDtypeStruct((B, S, RWKV_WIDTH), BF16)
    out_shape = [rw_shape] * 6 + [
        jax.ShapeDtypeStruct((B, S, RWKV_WIDTH), F32),
        jax.ShapeDtypeStruct((B, S, DIFF_WIDTH), BF16),
        jax.ShapeDtypeStruct((B, S, DIFF_WIDTH), BF16),
        jax.ShapeDtypeStruct((B, DIFF_HEADS, ns, 128, tm), BF16)]
    out_specs = [tok(RWKV_WIDTH)] * 7 + [tok(DIFF_WIDTH)] * 2 + [
        pl.BlockSpec((1, DIFF_HEADS, 1, 128, tm), lambda b, s: (b, 0, s, 0, 0))]
    return pl.pallas_call(
        functools.partial(_inproj_kernel, tm=tm),
        grid=(B, ns),
        in_specs=[tok(D_MODEL),
                  pl.BlockSpec((1, 6, D_MODEL), lambda b, s: (b, 0, 0)),
                  full((1, D_MODEL)),
                  full((D_MODEL, RWKV_COLS)), full((D_MODEL, DIFF_COLS)),
                  full((1, RWKV_COLS)), full((1, RWKV_WIDTH)), full((128, RWKV_WIDTH)),
                  full((1, RWKV_WIDTH)), full((128, RWKV_WIDTH)), full((128, RWKV_WIDTH)),
                  full((1, RWKV_WIDTH)), full((1, RWKV_WIDTH)),
                  full((RWKV_WIDTH, RWKV_WIDTH))],
        out_specs=out_specs,
        out_shape=out_shape,
        scratch_shapes=[pltpu.VMEM((1, RWKV_COLS), F32)],
        compiler_params=pltpu.CompilerParams(
            dimension_semantics=("arbitrary", "arbitrary"), vmem_limit_bytes=VMEM_LIMIT),
        name="inproj",
    )(x, mod6, row(norm1_gain), w_rw, w_at, row(mu), row(w0), wup_p, row(a0), aup_p,
      g_up.astype(BF16), row(k_k), row(k_a), _head_sum_matrix(RWKV_WIDTH))


def _rwkv_kernel(r_ref, k_ref, v_ref, a_ref, b_ref, g_ref, lw_ref, rk_ref, gng_ref, gnb_ref,
                 hsum_ref, y_ref, s_ref):
    C = CHUNK

    @pl.when(pl.program_id(1) == 0)
    def _():
        s_ref[...] = jnp.zeros_like(s_ref)

    ti = lax.broadcasted_iota(I32, (C, GW), 0)
    si = lax.broadcasted_iota(I32, (C, GW), 1) % C
    incl = si <= ti
    strict = si < ti
    eye = (si == ti).astype(F32)
    bdmask = (lax.broadcasted_iota(I32, (GW, GW), 0) // HEAD_DIM ==
              lax.broadcasted_iota(I32, (GW, GW), 1) // HEAD_DIM)
    tri = (lax.broadcasted_iota(I32, (C, C), 0) >=
           lax.broadcasted_iota(I32, (C, C), 1)).astype(BF16)
    hsum = hsum_ref[...]

    def bd(xb):
        return jnp.where(bdmask, jnp.concatenate([xb] * GROUP, axis=0), jnp.zeros((), xb.dtype))

    for grp in range(RWKV_HEADS // GROUP):
        sl = slice(grp * GW, (grp + 1) * GW)
        lw = lw_ref[0, :, sl]
        L = _split_dot_left(tri, lw)
        Lx = L - lw
        Lc = L[C - 1:C, :]
        rho = L[C // 2 - 1:C // 2, :]
        r = r_ref[0, :, sl].astype(F32)
        k = k_ref[0, :, sl].astype(F32)
        vb = v_ref[0, :, sl]
        a = a_ref[0, :, sl].astype(F32)
        b = b_ref[0, :, sl].astype(F32)
        e_k = jnp.exp(rho - L)
        rt = (r * jnp.exp(L - rho)).astype(BF16)
        at = (a * jnp.exp(Lx - rho)).astype(BF16)
        kt = (k * e_k).astype(BF16)
        bt = (b * e_k).astype(BF16)
        r0 = (r * jnp.exp(L)).astype(BF16)
        a0 = (a * jnp.exp(Lx)).astype(BF16)
        e_o = jnp.exp(Lc - L)
        kh = (k * e_o).astype(BF16)
        bh = (b * e_o).astype(BF16)

        lhs = jnp.concatenate([rt, at], axis=0)
        ak = _dot_nt(lhs, bd(kt))
        ab = _dot_nt(lhs, bd(bt))
        a_rk = jnp.where(incl, ak[:C], 0.0)
        a_ak = jnp.where(strict, ak[C:], 0.0)
        a_rb = jnp.where(incl, ab[:C], 0.0)
        n = jnp.where(strict, ab[C:], 0.0)

        p = eye + n
        xb = n.astype(BF16)
        x = _dot(xb, bd(xb))
        for _ in range(4):
            xb = x.astype(BF16)
            o = _dot(jnp.concatenate([xb, p.astype(BF16)], axis=0), bd(xb))
            x = o[:C]
            p = p + o[C:]
        p = p + _dot(p.astype(BF16), bd(x.astype(BF16)))
        tb = p.astype(BF16)

        av = _dot(jnp.concatenate([a_ak.astype(BF16), a_rk.astype(BF16)], axis=0), bd(vb))
        akv = av[:C]
        arkv = av[C:]
        w1 = _dot(tb, bd(a0))
        w2 = _dot(tb, bd(akv.astype(BF16)))

        sv = s_ref[grp]
        rws = _dot_nt(jnp.concatenate([r0, w1.astype(BF16)], axis=0), sv.astype(BF16))
        u = rws[C:] + w2
        ub = u.astype(BF16)
        y = rws[:C] + _dot(a_rb.astype(BF16), bd(ub)) + arkv
        upd = _dot_tn(jnp.concatenate([ub, vb], axis=0), jnp.concatenate([bh, kh], axis=0))
        s_ref[grp] = sv * jnp.exp(Lc) + jnp.where(bdmask, upd, 0.0)

        mean = _split_dot(y, hsum) * (1.0 / HEAD_DIM)
        d = y - mean
        var = _split_dot(d * d, hsum) * (1.0 / HEAD_DIM)
        yn = d * lax.rsqrt(var + GN_EPS) * gng_ref[:, sl] + gnb_ref[:, sl]
        v = vb.astype(F32)
        bonus = _split_dot(r * k * rk_ref[:, sl], hsum) * v
        y_ref[0, :, sl] = ((yn + bonus) * g_ref[0, :, sl].astype(F32)).astype(BF16)


def _split_dot_left(w_bf16, x):
    hi = x.astype(BF16)
    lo = (x - hi.astype(F32)).astype(BF16)
    return _dot(w_bf16, hi) + _dot(w_bf16, lo)


def _rwkv_call(r, k, v, a, b, g, lw, r_k, gn_gain, gn_bias):
    B, S, _ = r.shape
    nc = S // CHUNK
    tok = pl.BlockSpec((1, CHUNK, RWKV_WIDTH), lambda bb, c: (bb, c, 0))
    par = pl.BlockSpec((1, RWKV_WIDTH), lambda bb, c: (0, 0))
    return pl.pallas_call(
        _rwkv_kernel,
        grid=(B, nc),
        in_specs=[tok] * 7 + [par] * 3 + [pl.BlockSpec((GW, GW), lambda bb, c: (0, 0))],
        out_specs=tok,
        out_shape=jax.ShapeDtypeStruct((B, S, RWKV_WIDTH), BF16),
        scratch_shapes=[pltpu.VMEM((RWKV_HEADS // GROUP, GW, GW), F32)],
        compiler_params=pltpu.CompilerParams(
            dimension_semantics=("arbitrary", "arbitrary"), vmem_limit_bytes=VMEM_LIMIT),
        name="rwkv",
    )(r, k, v, a, b, g, lw, r_k.reshape(1, -1), gn_gain.reshape(1, -1), gn_bias.reshape(1, -1),
      _head_sum_matrix(GW))


def _attn_kernel(slopes_ref, q_ref, k_ref, vt_ref, lq1_ref, lk1_ref, lq2_ref, lk2_ref,
                 gain_ref, o_ref, acc_ref, m_ref, l_ref, *, tq, tk):
    hh = pl.program_id(1)
    i = pl.program_id(2)
    slope = slopes_ref[hh]
    q = q_ref[0]
    lane = lax.broadcasted_iota(I32, q.shape, 1)
    zero = jnp.zeros((), q.dtype)
    qs = (jnp.where(lane < HEAD_DIM, q, zero), jnp.where(lane >= HEAD_DIM, q, zero))
    rc = (lax.broadcasted_iota(I32, (tk, tq), 0) -
          lax.broadcasted_iota(I32, (tk, tq), 1)).astype(F32)

    m_ref[...] = jnp.full_like(m_ref, -jnp.inf)
    l_ref[...] = jnp.zeros_like(l_ref)
    acc_ref[...] = jnp.zeros_like(acc_ref)

    def block(j, masked):
        kb = k_ref[0, pl.ds(pl.multiple_of(j * tk, tk), tk), :]
        vt = vt_ref[0, 0, j]
        rel = rc + (j * tk - i * tq).astype(F32)
        bias = slope * rel
        for c in range(2):
            s = _dot_nt(kb, qs[c]) + bias
            if masked:
                s = jnp.where(rel <= 0.0, s, -jnp.inf)
            m_old = m_ref[c]
            m_new = jnp.maximum(m_old, jnp.max(s, axis=0, keepdims=True))
            pr = jnp.exp(s - m_new)
            alpha = jnp.exp(m_old - m_new)
            l_ref[c] = alpha * l_ref[c] + jnp.sum(pr, axis=0, keepdims=True)
            acc_ref[c] = alpha * acc_ref[c] + _dot(vt, pr.astype(BF16))
            m_ref[c] = m_new

    n_full = (i * tq) // tk
    n_all = (i * tq + tq - 1) // tk + 1

    def full_body(j, carry):
        block(j, False)
        return carry

    def masked_body(j, carry):
        block(j, True)
        return carry

    lax.fori_loop(0, n_full, full_body, 0)
    lax.fori_loop(n_full, n_all, masked_body, 0)

    lam = (jnp.exp(jnp.sum(lq1_ref[...] * lk1_ref[...], axis=-1, keepdims=True))
           - jnp.exp(jnp.sum(lq2_ref[...] * lk2_ref[...], axis=-1, keepdims=True))
           + LAMBDA_INIT)
    o = acc_ref[0] / l_ref[0] - lam * (acc_ref[1] / l_ref[1])
    ot = o.T
    ms = jnp.mean(ot * ot, axis=-1, keepdims=True)
    y = ot * lax.rsqrt(ms + NORM_EPS) * gain_ref[...] * (1.0 - LAMBDA_INIT)
    o_ref[0] = y.astype(BF16)


def _attn_call(q, k, vt, lq1, lk1, lq2, lk2, subln_gain, tq):
    B, S, _ = q.shape
    ns, tk = vt.shape[2], vt.shape[4]
    slopes = jnp.asarray([2.0 ** (-8.0 * (i + 1) / DIFF_HEADS) for i in range(DIFF_HEADS)], F32)
    vec = lambda n: pl.BlockSpec((1, n), lambda b, h, i, sl: (0, 0))
    grid_spec = pltpu.PrefetchScalarGridSpec(
        num_scalar_prefetch=1,
        grid=(B, DIFF_HEADS, S // tq),
        in_specs=[pl.BlockSpec((1, tq, 128), lambda b, h, i, sl: (b, i, h)),
                  pl.BlockSpec((1, S, 128), lambda b, h, i, sl: (b, 0, h)),
                  pl.BlockSpec((1, 1, ns, 128, tk), lambda b, h, i, sl: (b, h, 0, 0, 0)),
                  vec(HEAD_DIM), vec(HEAD_DIM), vec(HEAD_DIM), vec(HEAD_DIM), vec(128)],
        out_specs=pl.BlockSpec((1, tq, 128), lambda b, h, i, sl: (b, i, h)),
        scratch_shapes=[pltpu.VMEM((2, 128, tq), F32), pltpu.VMEM((2, 1, tq), F32),
                        pltpu.VMEM((2, 1, tq), F32)])
    return pl.pallas_call(
        functools.partial(_attn_kernel, tq=tq, tk=tk),
        grid_spec=grid_spec,
        out_shape=jax.ShapeDtypeStruct((B, S, DIFF_WIDTH), BF16),
        compiler_params=pltpu.CompilerParams(
            dimension_semantics=("arbitrary", "arbitrary", "arbitrary"),
            vmem_limit_bytes=VMEM_LIMIT),
        name="attn",
    )(slopes, q, k, vt, lq1.reshape(1, -1), lk1.reshape(1, -1), lq2.reshape(1, -1),
      lk2.reshape(1, -1), subln_gain.reshape(1, -1))


def _pack_bf16_pairs(x):
    w = x.shape[1] // 2
    lo = pltpu.bitcast(x[:, :w].astype(BF16).astype(F32), U32)
    hi = pltpu.bitcast(x[:, w:].astype(BF16).astype(F32), U32)
    return (lo >> 16) | (hi & jnp.uint32(0xFFFF0000))


def _unpack_bf16_pairs(p):
    lo = pltpu.bitcast(p << 16, F32)
    hi = pltpu.bitcast(p & jnp.uint32(0xFFFF0000), F32)
    return lo, hi


ROW_SUB = (D_MODEL // 2) // 128


def _store_rows(ref, x2d):
    for s in range(ROW_SUB):
        ref[:, s, :] = x2d[:, s * 128:(s + 1) * 128]


def _load_rows(ref):
    return jnp.concatenate([ref[:, s, :] for s in range(ROW_SUB)], axis=1)


def _outproj_kernel(yr_ref, yd_ref, x_ref, mod_ref, wo_r_ref, wo_d_ref, g2_ref, rw_hi_ref,
                    rw_lo_ref, rb_ref, x1_out, hp_out, idx_out, w_out, rank_out, cnt_out,
                    carry_ref, *, tm):
    t = pl.program_id(0)

    @pl.when(t == 0)
    def _():
        carry_ref[...] = jnp.zeros_like(carry_ref)

    mix = _dot(yr_ref[...], wo_r_ref[...]) + _dot(yd_ref[...], wo_d_ref[...])
    gate1 = mod_ref[0, 2:3, :]
    shift2 = mod_ref[0, 3:4, :]
    scale2 = mod_ref[0, 4:5, :]
    x1 = x_ref[...] + gate1 * mix
    x1_out[...] = x1
    ms = jnp.mean(x1 * x1, axis=-1, keepdims=True)
    h = x1 * lax.rsqrt(ms + NORM_EPS) * g2_ref[...] * (1.0 + scale2) + shift2
    _store_rows(hp_out, _pack_bf16_pairs(h))

    hi = h.astype(BF16)
    lo = (h - hi.astype(F32)).astype(BF16)
    logits = (_dot(hi, rw_hi_ref[...]) + _dot(hi, rw_lo_ref[...]) + _dot(lo, rw_hi_ref[...])
              + rb_ref[...])

    eidx = lax.broadcasted_iota(I32, logits.shape, 1)
    col4 = lax.broadcasted_iota(I32, (tm, TOP_K), 1)
    lg = logits
    vals, idxs = [], []
    onehot = jnp.zeros(logits.shape, F32)
    for _ in range(TOP_K):
        m = jnp.max(lg, axis=-1, keepdims=True)
        ix = jnp.min(jnp.where(lg == m, eidx, N_EXPERTS), axis=-1, keepdims=True)
        sel = eidx == ix
        vals.append(m)
        idxs.append(ix)
        onehot = onehot + sel.astype(F32)
        lg = jnp.where(sel, -jnp.inf, lg)
    es = [jnp.exp(v - vals[0]) for v in vals]
    den = es[0] + es[1] + es[2] + es[3]

    tri = (lax.broadcasted_iota(I32, (tm, tm), 0) >
           lax.broadcasted_iota(I32, (tm, tm), 1)).astype(BF16)
    prefix = _dot(tri, onehot.astype(BF16)) + carry_ref[...]
    idx4 = jnp.zeros((tm, TOP_K), I32)
    w4 = jnp.zeros((tm, TOP_K), F32)
    rank4 = jnp.zeros((tm, TOP_K), I32)
    for kk in range(TOP_K):
        rk = jnp.sum(jnp.where(eidx == idxs[kk], prefix, 0.0), axis=-1, keepdims=True)
        idx4 = jnp.where(col4 == kk, idxs[kk], idx4)
        w4 = jnp.where(col4 == kk, es[kk] / den, w4)
        rank4 = jnp.where(col4 == kk, rk.astype(I32), rank4)
    idx_out[...] = idx4
    w_out[...] = w4
    rank_out[...] = rank4
    carry_ref[...] = carry_ref[...] + jnp.sum(onehot, axis=0, keepdims=True)
    cnt_out[...] = carry_ref[...].astype(I32)


def _outproj_call(y_rwkv, y_diff, x, mod6, w_out, norm2_gain, router_w, router_b, S, tm):
    T = x.shape[0]
    tiles_per_seq = S // tm
    rw_hi = router_w.astype(BF16)
    rw_lo = (router_w - rw_hi.astype(F32)).astype(BF16)
    tok = lambda w: pl.BlockSpec((tm, w), lambda t: (t, 0))
    full = lambda shape: pl.BlockSpec(shape, lambda t: (0,) * len(shape))
    return pl.pallas_call(
        functools.partial(_outproj_kernel, tm=tm),
        grid=(T // tm,),
        in_specs=[tok(RWKV_WIDTH), tok(DIFF_WIDTH), tok(D_MODEL),
                  pl.BlockSpec((1, 6, D_MODEL), lambda t: (t // tiles_per_seq, 0, 0)),
                  full((RWKV_WIDTH, D_MODEL)), full((DIFF_WIDTH, D_MODEL)), full((1, D_MODEL)),
                  full((D_MODEL, N_EXPERTS)), full((D_MODEL, N_EXPERTS)), full((1, N_EXPERTS))],
        out_specs=[tok(D_MODEL), pl.BlockSpec((tm, ROW_SUB, 128), lambda t: (t, 0, 0)),
                   tok(TOP_K), tok(TOP_K), tok(TOP_K), full((1, N_EXPERTS))],
        out_shape=[jax.ShapeDtypeStruct((T, D_MODEL), F32),
                   jax.ShapeDtypeStruct((T, ROW_SUB, 128), U32),
                   jax.ShapeDtypeStruct((T, TOP_K), I32),
                   jax.ShapeDtypeStruct((T, TOP_K), F32),
                   jax.ShapeDtypeStruct((T, TOP_K), I32),
                   jax.ShapeDtypeStruct((1, N_EXPERTS), I32)],
        scratch_shapes=[pltpu.VMEM((1, N_EXPERTS), F32)],
        compiler_params=pltpu.CompilerParams(
            dimension_semantics=("arbitrary",), vmem_limit_bytes=VMEM_LIMIT),
        name="outproj",
    )(y_rwkv, y_diff, x, mod6, w_out[:RWKV_WIDTH].astype(BF16), w_out[RWKV_WIDTH:].astype(BF16),
      norm2_gain.reshape(1, -1), rw_hi, rw_lo, router_b.reshape(1, -1))


def _dispatch_kernel(last_ref, cnt_ref, nu_ref, slots_hbm, hp_ref, xs_hbm, slot_smem, zero_ref, sem,
                     zsem, *, tm, n_blocks):
    t = pl.program_id(0)

    @pl.when(t == 0)
    def _():
        zero_ref[...] = jnp.zeros_like(zero_ref)

        def zero_copy(row):
            return pltpu.make_async_copy(zero_ref, xs_hbm.at[pl.ds(row, MOE_BLOCK)], zsem)

        def start_block(bk, carry):
            zero_copy(bk * MOE_BLOCK).start()
            return carry

        def wait_block(bk, carry):
            zero_copy(0).wait()
            return carry

        for e in range(N_EXPERTS):
            @pl.when(cnt_ref[e] > 0)
            def _():
                zero_copy(last_ref[e]).start()
        lax.fori_loop(nu_ref[0], n_blocks, start_block, 0)
        for e in range(N_EXPERTS):
            @pl.when(cnt_ref[e] > 0)
            def _():
                zero_copy(0).wait()
        lax.fori_loop(nu_ref[0], n_blocks, wait_block, 0)

    cp = pltpu.make_async_copy(slots_hbm.at[pl.ds(t * (tm * TOP_K), tm * TOP_K)], slot_smem, sem.at[0])
    cp.start()
    cp.wait()

    def body(i, carry):
        for kk in range(TOP_K):
            slot = slot_smem[i * TOP_K + kk]
            pltpu.make_async_copy(hp_ref.at[i], xs_hbm.at[slot], sem.at[1]).start()
        return carry

    lax.fori_loop(0, tm, body, 0)
    for kk in range(TOP_K):
        pltpu.make_async_copy(hp_ref, xs_hbm.at[pl.ds(0, tm)], sem.at[1]).wait()


def _dispatch_call(hp, slots_flat, last_block_row, counts, n_used, n_blocks, tm):
    T = hp.shape[0]
    n_rows = n_blocks * MOE_BLOCK
    grid_spec = pltpu.PrefetchScalarGridSpec(
        num_scalar_prefetch=3,
        grid=(T // tm,),
        in_specs=[pl.BlockSpec(memory_space=pl.ANY),
                  pl.BlockSpec((tm, ROW_SUB, 128), lambda t, last, cnt, nu: (t, 0, 0))],
        out_specs=pl.BlockSpec(memory_space=pl.ANY),
        scratch_shapes=[pltpu.SMEM((tm * TOP_K,), I32),
                        pltpu.VMEM((MOE_BLOCK, ROW_SUB, 128), U32),
                        pltpu.SemaphoreType.DMA((2,)),
                        pltpu.SemaphoreType.DMA(())])
    return pl.pallas_call(
        functools.partial(_dispatch_kernel, tm=tm, n_blocks=n_blocks),
        grid_spec=grid_spec,
        out_shape=jax.ShapeDtypeStruct((n_rows, ROW_SUB, 128), U32),
        compiler_params=pltpu.CompilerParams(
            dimension_semantics=("arbitrary",), vmem_limit_bytes=VMEM_LIMIT),
        name="dispatch",
    )(last_block_row, counts, n_used, slots_flat, hp)


def _experts_kernel(be_ref, nu_ref, xs_ref, wgu_a_ref, wgu_b_ref, bgu_ref, wd_ref, bd_ref, ys_ref):
    i = pl.program_id(0)

    @pl.when(i < nu_ref[0])
    def _():
        xa, xb = _unpack_bf16_pairs(_load_rows(xs_ref))
        gu = (_dot(xa.astype(BF16), wgu_a_ref[0]) + _dot(xb.astype(BF16), wgu_b_ref[0])
              + bgu_ref[0])
        gate = jnp.minimum(gu[:, :D_EXPERT], SWIGLU_LIMIT)
        up = jnp.clip(gu[:, D_EXPERT:], -SWIGLU_LIMIT, SWIGLU_LIMIT)
        act = (up + 1.0) * (gate * jax.nn.sigmoid(SWIGLU_ALPHA * gate))
        y = _dot(act.astype(BF16), wd_ref[0]) + bd_ref[0]
        _store_rows(ys_ref, _pack_bf16_pairs(y))

    @pl.when(i >= nu_ref[0])
    def _():
        ys_ref[...] = jnp.zeros_like(ys_ref)


def _experts_call(xs, block_e, n_used, w_gate_up, b_gate_up, w_down, b_down, n_blocks):
    half = D_MODEL // 2
    wgu = w_gate_up.astype(BF16)
    wd = w_down.astype(BF16)
    blk = lambda i, be, nu: jnp.minimum(i, nu[0] - 1)
    rows = pl.BlockSpec((MOE_BLOCK, ROW_SUB, 128), lambda i, be, nu: (blk(i, be, nu), 0, 0))
    grid_spec = pltpu.PrefetchScalarGridSpec(
        num_scalar_prefetch=2,
        grid=(n_blocks,),
        in_specs=[rows,
                  pl.BlockSpec((1, half, 2 * D_EXPERT), lambda i, be, nu: (be[i], 0, 0)),
                  pl.BlockSpec((1, half, 2 * D_EXPERT), lambda i, be, nu: (be[i], 1, 0)),
                  pl.BlockSpec((1, 1, 2 * D_EXPERT), lambda i, be, nu: (be[i], 0, 0)),
                  pl.BlockSpec((1, D_EXPERT, D_MODEL), lambda i, be, nu: (be[i], 0, 0)),
                  pl.BlockSpec((1, 1, D_MODEL), lambda i, be, nu: (be[i], 0, 0))],
        out_specs=pl.BlockSpec((MOE_BLOCK, ROW_SUB, 128), lambda i, be, nu: (i, 0, 0)))
    return pl.pallas_call(
        _experts_kernel,
        grid_spec=grid_spec,
        out_shape=jax.ShapeDtypeStruct((n_blocks * MOE_BLOCK, ROW_SUB, 128), U32),
        compiler_params=pltpu.CompilerParams(
            dimension_semantics=("arbitrary",), vmem_limit_bytes=VMEM_LIMIT),
        name="experts",
    )(block_e, n_used, xs, wgu, wgu, b_gate_up.reshape(N_EXPERTS, 1, -1), wd,
      b_down.reshape(N_EXPERTS, 1, -1))


def _combine_kernel(slots_hbm, ys_hbm, x1_ref, w_ref, mod_ref, fg_ref, o_ref,
                    slot_smem, buf_ref, sem, *, tm):
    t = pl.program_id(0)
    cp = pltpu.make_async_copy(slots_hbm.at[pl.ds(t * (tm * TOP_K), tm * TOP_K)], slot_smem, sem.at[0])
    cp.start()
    cp.wait()

    def body(i, carry):
        for kk in range(TOP_K):
            slot = slot_smem[i * TOP_K + kk]
            pltpu.make_async_copy(ys_hbm.at[slot], buf_ref.at[kk, i], sem.at[1]).start()
        return carry

    lax.fori_loop(0, tm, body, 0)
    for kk in range(TOP_K):
        pltpu.make_async_copy(ys_hbm.at[pl.ds(0, tm)], buf_ref.at[kk], sem.at[1]).wait()

    w = w_ref[...]
    acc_lo = jnp.zeros((tm, D_MODEL // 2), F32)
    acc_hi = jnp.zeros((tm, D_MODEL // 2), F32)
    for kk in range(TOP_K):
        lo, hi = _unpack_bf16_pairs(_load_rows(buf_ref.at[kk]))
        wk = w[:, kk:kk + 1]
        acc_lo = acc_lo + wk * lo
        acc_hi = acc_hi + wk * hi
    moe = jnp.concatenate([acc_lo, acc_hi], axis=1)
    gate2 = mod_ref[0, 5:6, :]
    x2 = x1_ref[...] + gate2 * moe
    ms = jnp.mean(x2 * x2, axis=-1, keepdims=True)
    o_ref[...] = x2 * lax.rsqrt(ms + NORM_EPS) * fg_ref[...]


def _combine_call(ys, slots_flat, x1, top_w, mod6, final_gain, S, tm):
    T = x1.shape[0]
    tiles_per_seq = S // tm
    return pl.pallas_call(
        functools.partial(_combine_kernel, tm=tm),
        grid=(T // tm,),
        in_specs=[pl.BlockSpec(memory_space=pl.ANY),
                  pl.BlockSpec(memory_space=pl.ANY),
                  pl.BlockSpec((tm, D_MODEL), lambda t: (t, 0)),
                  pl.BlockSpec((tm, TOP_K), lambda t: (t, 0)),
                  pl.BlockSpec((1, 6, D_MODEL), lambda t: (t // tiles_per_seq, 0, 0)),
                  pl.BlockSpec((1, D_MODEL), lambda t: (0, 0))],
        out_specs=pl.BlockSpec((tm, D_MODEL), lambda t: (t, 0)),
        out_shape=jax.ShapeDtypeStruct((T, D_MODEL), F32),
        scratch_shapes=[pltpu.SMEM((tm * TOP_K,), I32),
                        pltpu.VMEM((TOP_K, tm, ROW_SUB, 128), U32),
                        pltpu.SemaphoreType.DMA((2,))],
        compiler_params=pltpu.CompilerParams(
            dimension_semantics=("arbitrary",), vmem_limit_bytes=VMEM_LIMIT),
        name="combine",
    )(slots_flat, ys, x1, top_w, mod6, final_gain.reshape(1, -1))


def _forward(x, c, mod_w, mod_b, norm1_gain, w_in, rwkv_shift_mu, rwkv_w0, rwkv_w_up, rwkv_a0,
             rwkv_a_up, rwkv_g_up, rwkv_k_k, rwkv_k_a, rwkv_r_k, rwkv_gn_gain, rwkv_gn_bias,
             diff_lambda_q1, diff_lambda_k1, diff_lambda_q2, diff_lambda_k2, diff_subln_gain,
             w_out, norm2_gain, router_w, router_b, w_gate_up, b_gate_up, w_down, b_down,
             final_gain):
    B, S, D = x.shape
    T = B * S
    tm_in = min(512, S)
    tq = min(256, S)
    tm = min(256, S)

    mod6 = _mod_call(c, mod_w, mod_b).reshape(B, 6, D)
    (r, k, v, a, b, g, lw, q, kat, vt) = _inproj_call(
        x, mod6, norm1_gain, w_in, rwkv_shift_mu, rwkv_w0, rwkv_w_up, rwkv_a0, rwkv_a_up,
        rwkv_g_up, rwkv_k_k, rwkv_k_a, tm_in)
    y_rwkv = _rwkv_call(r, k, v, a, b, g, lw, rwkv_r_k, rwkv_gn_gain, rwkv_gn_bias)
    y_diff = _attn_call(q, kat, vt, diff_lambda_q1, diff_lambda_k1, diff_lambda_q2,
                        diff_lambda_k2, diff_subln_gain, tq)

    x1, hp, top_idx, top_w, rank, counts = _outproj_call(
        y_rwkv.reshape(T, -1), y_diff.reshape(T, -1), x.reshape(T, D), mod6, w_out, norm2_gain,
        router_w, router_b, S, tm)

    counts = counts.reshape(N_EXPERTS)
    padded = ((counts + MOE_BLOCK - 1) // MOE_BLOCK) * MOE_BLOCK
    pad_ends = jnp.cumsum(padded)
    pad_starts = pad_ends - padded
    n_blocks = -(-(T * TOP_K + N_EXPERTS * (MOE_BLOCK - 1)) // MOE_BLOCK)
    n_used = (pad_ends[-1] // MOE_BLOCK).astype(I32).reshape(1)
    block_start = jnp.minimum(jnp.arange(n_blocks, dtype=I32), n_used[0] - 1) * MOE_BLOCK
    block_e = jnp.minimum(jnp.searchsorted(pad_ends, block_start, side='right'),
                          N_EXPERTS - 1).astype(I32)
    slots = (pad_starts.astype(I32)[top_idx] + rank).reshape(T * TOP_K)
    last_block_row = (pad_ends - MOE_BLOCK).astype(I32)

    xs = _dispatch_call(hp, slots, last_block_row, counts.astype(I32), n_used, n_blocks, tm)
    ys = _experts_call(xs, block_e, n_used, w_gate_up, b_gate_up, w_down, b_down, n_blocks)
    out = _combine_call(ys, slots, x1, top_w, mod6, final_gain, S, tm)
    return out.reshape(B, S, D)


def kernel(x, c, mod_w, mod_b, norm1_gain, w_in, rwkv_shift_mu, rwkv_w0, rwkv_w_up, rwkv_a0, rwkv_a_up, rwkv_g_up, rwkv_k_k, rwkv_k_a, rwkv_r_k, rwkv_gn_gain, rwkv_gn_bias, diff_lambda_q1, diff_lambda_k1, diff_lambda_q2, diff_lambda_k2, diff_subln_gain, w_out, norm2_gain, router_w, router_b, w_gate_up, b_gate_up, w_down, b_down, final_gain):
    return _forward(x, c, mod_w[0], mod_b[0], norm1_gain[0], w_in[0], rwkv_shift_mu[0], rwkv_w0[0],
                    rwkv_w_up[0], rwkv_a0[0], rwkv_a_up[0], rwkv_g_up[0], rwkv_k_k[0], rwkv_k_a[0],
                    rwkv_r_k[0].reshape(-1), rwkv_gn_gain[0], rwkv_gn_bias[0], diff_lambda_q1[0],
                    diff_lambda_k1[0], diff_lambda_q2[0], diff_lambda_k2[0], diff_subln_gain[0],
                    w_out[0], norm2_gain[0], router_w[0], router_b[0], w_gate_up[0], b_gate_up[0],
                    w_down[0], b_down[0], final_gain)
```

```python
import functools
import math

import jax
import jax.numpy as jnp
from jax import lax
from jax.experimental import pallas as pl
from jax.experimental.pallas import tpu as pltpu

F32 = jnp.float32
BF16 = jnp.bfloat16
I32 = jnp.int32
U32 = jnp.uint32

D_MODEL = 1024
RWKV_WIDTH = 512
RWKV_HEADS = 8
HEAD_DIM = 64
RWKV_COLS = 3 * RWKV_WIDTH + 64 + 64 + 128
DIFF_WIDTH = 512
DIFF_HEADS = 4
DIFF_COLS = 3 * DIFF_WIDTH
N_EXPERTS = 32
TOP_K = 4
D_EXPERT = 1024
SWIGLU_LIMIT = 7.0
SWIGLU_ALPHA = 1.702
MOE_BLOCK = 512
NORM_EPS = 1e-5
GN_EPS = 64e-5
LAMBDA_INIT = 0.8 - 0.6 * math.exp(-0.3 * 0)

CHUNK = 64
GROUP = 4
GW = GROUP * HEAD_DIM
VMEM_LIMIT = 56 * 1024 * 1024


def _dot(a, b):
    return jnp.dot(a, b, preferred_element_type=F32)


def _dot_nt(a, b):
    return lax.dot_general(a, b, (((1,), (1,)), ((), ())), preferred_element_type=F32)


def _dot_tn(a, b):
    return lax.dot_general(a, b, (((0,), (0,)), ((), ())), preferred_element_type=F32)


def _split_dot(x, w_bf16):
    hi = x.astype(BF16)
    lo = (x - hi.astype(F32)).astype(BF16)
    return _dot(hi, w_bf16) + _dot(lo, w_bf16)


def _mod_kernel(c_ref, w_ref, b_ref, o_ref):
    c = c_ref[...]
    s = c * jax.nn.sigmoid(c)
    o_ref[...] = _dot(s, w_ref[...]) + b_ref[...]


def _mod_call(c, mod_w, mod_b):
    B = c.shape[0]
    n = mod_w.shape[1]
    tn = 1536
    return pl.pallas_call(
        _mod_kernel,
        grid=(n // tn,),
        in_specs=[pl.BlockSpec((B, D_MODEL), lambda j: (0, 0)),
                  pl.BlockSpec((D_MODEL, tn), lambda j: (0, j)),
                  pl.BlockSpec((1, tn), lambda j: (0, j))],
        out_specs=pl.BlockSpec((B, tn), lambda j: (0, j)),
        out_shape=jax.ShapeDtypeStruct((B, n), F32),
        compiler_params=pltpu.CompilerParams(
            dimension_semantics=("arbitrary",), vmem_limit_bytes=VMEM_LIMIT),
        name="mod",
    )(c, mod_w, mod_b.reshape(1, n))


def _inproj_kernel(x_ref, mod_ref, g1_ref, wrw_ref, wat_ref, mu_ref, w0_ref, wup_ref,
                   a0_ref, aup_ref, gup_ref, kk_ref, ka_ref, hsum_ref,
                   r_out, k_out, v_out, a_out, b_out, g_out, lw_out,
                   q_out, kat_out, vt_out, carry_ref, *, tm):
    s = pl.program_id(1)
    x = x_ref[0]
    ms = jnp.mean(x * x, axis=-1, keepdims=True)
    shift1 = mod_ref[0, 0:1, :]
    scale1 = mod_ref[0, 1:2, :]
    h = x * lax.rsqrt(ms + NORM_EPS) * g1_ref[...] * (1.0 + scale1) + shift1
    hb = h.astype(BF16)

    p = _dot(hb, wrw_ref[...])

    @pl.when(s == 0)
    def _():
        carry_ref[...] = jnp.zeros_like(carry_ref)

    rolled = pltpu.roll(p, shift=1, axis=0)
    row = lax.broadcasted_iota(I32, p.shape, 0)
    prev = jnp.where(row == 0, carry_ref[...], rolled)
    carry_ref[...] = p[tm - 1:tm, :]
    ps = p + mu_ref[...] * (prev - p)

    r = ps[:, 0:512]
    k = ps[:, 512:1024]
    v = ps[:, 1024:1536]
    lo2 = ps[:, 1536:1664]
    g_lo = ps[:, 1664:1792]
    z = w0_ref[...] + _dot(jnp.tanh(lo2).astype(BF16), wup_ref[...])
    nz = -z
    softplus = jnp.maximum(nz, 0.0) + jnp.log(1.0 + jnp.exp(-jnp.abs(nz)))
    w = -softplus - 0.5
    lw_out[0] = -jnp.exp(w)
    a = jax.nn.sigmoid(a0_ref[...] + _dot(lo2.astype(BF16), aup_ref[...]))
    g = _dot(jax.nn.sigmoid(g_lo).astype(BF16), gup_ref[...])
    kk = k * kk_ref[...]
    ssq = _split_dot(kk * kk, hsum_ref[...])
    kk = kk / jnp.maximum(jnp.sqrt(ssq), 1e-12)
    k = k * (1.0 + (a - 1.0) * ka_ref[...])
    r_out[0] = r.astype(BF16)
    k_out[0] = k.astype(BF16)
    v_out[0] = v.astype(BF16)
    a_out[0] = (-kk).astype(BF16)
    b_out[0] = (kk * a).astype(BF16)
    g_out[0] = g.astype(BF16)

    pa = _dot(hb, wat_ref[...])
    q_out[0] = (pa[:, 0:512] * (1.0 / math.sqrt(HEAD_DIM))).astype(BF16)
    kat_out[0] = pa[:, 512:1024].astype(BF16)
    for hh in range(DIFF_HEADS):
        vh = pa[:, 1024 + hh * 128:1024 + (hh + 1) * 128]
        vt_out[0, hh, 0] = vh.T.astype(BF16)


def _head_sum_matrix(width):
    i = jnp.arange(width) // HEAD_DIM
    return (i[:, None] == i[None, :]).astype(BF16)


def _inproj_call(x, mod6, norm1_gain, w_in, mu, w0, w_up, a0, a_up, g_up, k_k, k_a, tm):
    B, S, _ = x.shape
    ns = S // tm
    w_rw = w_in[:, :RWKV_COLS].astype(BF16)
    w_at = w_in[:, RWKV_COLS:].astype(BF16)
    zeros = jnp.zeros((64, RWKV_WIDTH), F32)
    wup_p = jnp.concatenate([w_up, zeros], axis=0).astype(BF16)
    aup_p = jnp.concatenate([zeros, a_up], axis=0).astype(BF16)
    row = lambda v: v.reshape(1, -1)
    full = lambda shape: pl.BlockSpec(shape, lambda b, s: (0,) * len(shape))
    tok = lambda w: pl.BlockSpec((1, tm, w), lambda b, s: (b, s, 0))
    rw_shape = jax.ShapeDtypeStruct((B, S, RWKV_WIDTH), BF16)
    out_shape = [rw_shape] * 6 + [
        jax.ShapeDtypeStruct((B, S, RWKV_WIDTH), F32),
        jax.ShapeDtypeStruct((B, S, DIFF_WIDTH), BF16),
        jax.ShapeDtypeStruct((B, S, DIFF_WIDTH), BF16),
        jax.ShapeDtypeStruct((B, DIFF_HEADS, ns, 128, tm), BF16)]
    out_specs = [tok(RWKV_WIDTH)] * 7 + [tok(DIFF_WIDTH)] * 2 + [
        pl.BlockSpec((1, DIFF_HEADS, 1, 128, tm), lambda b, s: (b, 0, s, 0, 0))]
    return pl.pallas_call(
        functools.partial(_inproj_kernel, tm=tm),
        grid=(B, ns),
        in_specs=[tok(D_MODEL),
                  pl.BlockSpec((1, 6, D_MODEL), lambda b, s: (b, 0, 0)),
                  full((1, D_MODEL)),
                  full((D_MODEL, RWKV_COLS)), full((D_MODEL, DIFF_COLS)),
                  full((1, RWKV_COLS)), full((1, RWKV_WIDTH)), full((128, RWKV_WIDTH)),
                  full((1, RWKV_WIDTH)), full((128, RWKV_WIDTH)), full((128, RWKV_WIDTH)),
                  full((1, RWKV_WIDTH)), full((1, RWKV_WIDTH)),
                  full((RWKV_WIDTH, RWKV_WIDTH))],
        out_specs=out_specs,
        out_shape=out_shape,
        scratch_shapes=[pltpu.VMEM((1, RWKV_COLS), F32)],
        compiler_params=pltpu.CompilerParams(
            dimension_semantics=("arbitrary", "arbitrary"), vmem_limit_bytes=VMEM_LIMIT),
        name="inproj",
    )(x, mod6, row(norm1_gain), w_rw, w_at, row(mu), row(w0), wup_p, row(a0), aup_p,
      g_up.astype(BF16), row(k_k), row(k_a), _head_sum_matrix(RWKV_WIDTH))


def _rwkv_kernel(r_ref, k_ref, v_ref, a_ref, b_ref, g_ref, lw_ref, rk_ref, gng_ref, gnb_ref,
                 hsum_ref, tri_ref, y_ref, s0_ref, s1_ref, *, nck):
    C = CHUNK
    n_grp = RWKV_HEADS // GROUP
    s_refs = (s0_ref, s1_ref)

    @pl.when(pl.program_id(1) == 0)
    def _():
        s0_ref[...] = jnp.zeros_like(s0_ref)
        s1_ref[...] = jnp.zeros_like(s1_ref)

    ti = lax.broadcasted_iota(I32, (C, GW), 0)
    si = lax.broadcasted_iota(I32, (C, GW), 1) % C
    incl = si <= ti
    strict = si < ti
    eye = (si == ti).astype(F32)
    bdmask = (lax.broadcasted_iota(I32, (GW, GW), 0) // HEAD_DIM ==
              lax.broadcasted_iota(I32, (GW, GW), 1) // HEAD_DIM)
    hsum = hsum_ref[...]
    chains = [(ck, g) for ck in range(nck) for g in range(n_grp)]

    def bd(xb):
        return jnp.where(bdmask, jnp.concatenate([xb] * GROUP, axis=0), jnp.zeros((), xb.dtype))

    def cut(x, ch):
        ck, g = ch
        return x[ck * C:(ck + 1) * C, g * GW:(g + 1) * GW]

    def rows_of_chunk(x, row):
        return jnp.concatenate(
            [jnp.broadcast_to(x[ck * C + row:ck * C + row + 1, :], (C, x.shape[1]))
             for ck in range(nck)], axis=0)

    lw = lw_ref[0]
    L = _split_dot_left(tri_ref[...], lw)
    Lx = L - lw
    Lc = rows_of_chunk(L, C - 1)
    rho = rows_of_chunk(L, C // 2 - 1)
    r = r_ref[0].astype(F32)
    k = k_ref[0].astype(F32)
    vb = v_ref[0]
    a = a_ref[0].astype(F32)
    b = b_ref[0].astype(F32)
    e_k = jnp.exp(rho - L)
    rt = (r * jnp.exp(L - rho)).astype(BF16)
    at = (a * jnp.exp(Lx - rho)).astype(BF16)
    kt = (k * e_k).astype(BF16)
    bt = (b * e_k).astype(BF16)
    r0 = (r * jnp.exp(L)).astype(BF16)
    a0 = (a * jnp.exp(Lx)).astype(BF16)
    e_o = jnp.exp(Lc - L)
    kh = (k * e_o).astype(BF16)
    bh = (b * e_o).astype(BF16)
    gc = jnp.exp(Lc)

    lhs = {ch: jnp.concatenate([cut(rt, ch), cut(at, ch)], axis=0) for ch in chains}
    ak = {ch: _dot_nt(lhs[ch], bd(cut(kt, ch))) for ch in chains}
    ab = {ch: _dot_nt(lhs[ch], bd(cut(bt, ch))) for ch in chains}
    a_rk = {ch: jnp.where(incl, ak[ch][:C], 0.0).astype(BF16) for ch in chains}
    a_ak = {ch: jnp.where(strict, ak[ch][C:], 0.0).astype(BF16) for ch in chains}
    a_rb = {ch: jnp.where(incl, ab[ch][:C], 0.0).astype(BF16) for ch in chains}
    n = {ch: jnp.where(strict, ab[ch][C:], 0.0) for ch in chains}

    p = {ch: eye + n[ch] for ch in chains}
    xb = {ch: n[ch].astype(BF16) for ch in chains}
    x = {ch: _dot(xb[ch], bd(xb[ch])) for ch in chains}
    for _ in range(4):
        xb = {ch: x[ch].astype(BF16) for ch in chains}
        o = {ch: _dot(jnp.concatenate([xb[ch], p[ch].astype(BF16)], axis=0), bd(xb[ch]))
             for ch in chains}
        x = {ch: o[ch][:C] for ch in chains}
        p = {ch: p[ch] + o[ch][C:] for ch in chains}
    o = {ch: _dot(p[ch].astype(BF16), bd(x[ch].astype(BF16))) for ch in chains}
    tb = {ch: (p[ch] + o[ch]).astype(BF16) for ch in chains}

    av = {ch: _dot(jnp.concatenate([a_ak[ch], a_rk[ch]], axis=0), bd(cut(vb, ch))) for ch in chains}
    w1 = {ch: _dot(tb[ch], bd(cut(a0, ch))).astype(BF16) for ch in chains}
    w2 = {ch: _dot(tb[ch], bd(av[ch][:C].astype(BF16))) for ch in chains}

    sv = [s_refs[g][...] for g in range(n_grp)]
    ys = {}
    for ck in range(nck):
        grp = [(ck, g) for g in range(n_grp)]
        rws = [_dot_nt(jnp.concatenate([cut(r0, ch), w1[ch]], axis=0), sv[ch[1]].astype(BF16))
               for ch in grp]
        ub = [(rws[g][C:] + w2[(ck, g)]).astype(BF16) for g in range(n_grp)]
        upd = [_dot_tn(jnp.concatenate([ub[g], cut(vb, (ck, g))], axis=0),
                       jnp.concatenate([cut(bh, (ck, g)), cut(kh, (ck, g))], axis=0))
               for g in range(n_grp)]
        for g in range(n_grp):
            ch = (ck, g)
            ys[ch] = rws[g][:C] + _dot(a_rb[ch], bd(ub[g])) + av[ch][C:]
        sv = [sv[g] * cut(gc, (ck, g))[0:1, :] + jnp.where(bdmask, upd[g], 0.0)
              for g in range(n_grp)]
    for g in range(n_grp):
        s_refs[g][...] = sv[g]

    for g in range(n_grp):
        sl = slice(g * GW, (g + 1) * GW)
        y = jnp.concatenate([ys[(ck, g)] for ck in range(nck)], axis=0)
        mean = _split_dot(y, hsum) * (1.0 / HEAD_DIM)
        d = y - mean
        var = _split_dot(d * d, hsum) * (1.0 / HEAD_DIM)
        yn = d * lax.rsqrt(var + GN_EPS) * gng_ref[:, sl] + gnb_ref[:, sl]
        bonus = _split_dot(r[:, sl] * k[:, sl] * rk_ref[:, sl], hsum) * vb[:, sl].astype(F32)
        y_ref[0, :, sl] = ((yn + bonus) * g_ref[0, :, sl].astype(F32)).astype(BF16)


def _split_dot_left(w_bf16, x):
    hi = x.astype(BF16)
    lo = (x - hi.astype(F32)).astype(BF16)
    return _dot(w_bf16, hi) + _dot(w_bf16, lo)


def _rwkv_call(r, k, v, a, b, g, lw, r_k, gn_gain, gn_bias, nck):
    B, S, _ = r.shape
    tb = nck * CHUNK
    idx = jnp.arange(tb)
    tri = ((idx[:, None] // CHUNK == idx[None, :] // CHUNK) &
           (idx[:, None] >= idx[None, :])).astype(BF16)
    tok = pl.BlockSpec((1, tb, RWKV_WIDTH), lambda bb, c: (bb, c, 0))
    par = pl.BlockSpec((1, RWKV_WIDTH), lambda bb, c: (0, 0))
    return pl.pallas_call(
        functools.partial(_rwkv_kernel, nck=nck),
        grid=(B, S // tb),
        in_specs=[tok] * 7 + [par] * 3 + [pl.BlockSpec((GW, GW), lambda bb, c: (0, 0)),
                                          pl.BlockSpec((tb, tb), lambda bb, c: (0, 0))],
        out_specs=tok,
        out_shape=jax.ShapeDtypeStruct((B, S, RWKV_WIDTH), BF16),
        scratch_shapes=[pltpu.VMEM((GW, GW), F32)] * (RWKV_HEADS // GROUP),
        compiler_params=pltpu.CompilerParams(
            dimension_semantics=("arbitrary", "arbitrary"), vmem_limit_bytes=VMEM_LIMIT),
        name="rwkv",
    )(r, k, v, a, b, g, lw, r_k.reshape(1, -1), gn_gain.reshape(1, -1), gn_bias.reshape(1, -1),
      _head_sum_matrix(GW), tri)


def _tree_reduce(op, x):
    while x.shape[0] > 8:
        h = x.shape[0] // 2
        x = op(x[:h], x[h:])
    return x


def _attn_kernel(slopes_ref, q_ref, k_ref, vt_ref, lq1_ref, lk1_ref, lq2_ref, lk2_ref,
                 gain_ref, o_ref, sa_ref, sb_ref, acc_ref, m_ref, l_ref, *, tq, tk):
    hh = pl.program_id(1)
    i = pl.program_id(2)
    slope = slopes_ref[hh]

    q = q_ref[0].astype(F32)
    lane = lax.broadcasted_iota(I32, (tq, 256), 1)
    qrow = lax.broadcasted_iota(I32, (tq, 256), 0)
    qa = (qrow >> 5).astype(F32) * (32.0 * slope)
    qb = (qrow & 31).astype(F32) * slope
    qbias = jnp.where(lane < 130, 1.0, jnp.where(lane == 130, -qa, jnp.where(lane == 131, -qb, 0.0)))
    qpad = jnp.concatenate([q, jnp.zeros_like(q)], axis=1)
    q_aug = []
    for c in range(2):
        in_comp = (lane >= c * HEAD_DIM) & (lane < (c + 1) * HEAD_DIM)
        q_aug.append(jnp.where(in_comp, qpad, jnp.where(lane >= 128, qbias, 0.0)))
    klane = lax.broadcasted_iota(I32, (tk, 128), 1)
    krow = lax.broadcasted_iota(I32, (tk, 128), 0)
    ka = (krow >> 5).astype(F32) * (32.0 * slope)
    kb_ = (krow & 31).astype(F32) * slope
    kbias = jnp.where(klane == 0, ka, jnp.where(klane == 1, kb_, jnp.where(klane < 4, 1.0, 0.0))
                      ).astype(BF16)

    qt_both = jnp.concatenate([q_aug[0].T, q_aug[1].T], axis=1).astype(BF16)

    m_ref[...] = jnp.full_like(m_ref, -jnp.inf)
    l_ref[...] = jnp.zeros_like(l_ref)
    acc_ref[...] = jnp.zeros_like(acc_ref)

    def scores(j):
        kb = k_ref[0, pl.ds(pl.multiple_of(j * tk, tk), tk), :]
        return _dot(jnp.concatenate([kb, kbias], axis=1), qt_both)

    def absorb(s_ref, j, masked):
        s = s_ref[...]
        off = j * tk - i * tq
        cj = slope * off.astype(F32)
        if masked:
            keep = (lax.broadcasted_iota(I32, (tk, tq), 0) -
                    lax.broadcasted_iota(I32, (tk, tq), 1) + off) <= 0
            s = jnp.where(jnp.concatenate([keep, keep], axis=1), s, -jnp.inf)
        m_loc = jnp.max(_tree_reduce(jnp.maximum, s), axis=0, keepdims=True)
        m_old = m_ref[...]
        m_new = jnp.maximum(m_old, m_loc + cj)
        pr = jnp.exp(s - (m_new - cj))
        alpha = jnp.exp(m_old - m_new)
        l_ref[...] = alpha * l_ref[...] + jnp.sum(_tree_reduce(jnp.add, pr), axis=0, keepdims=True)
        acc_ref[...] = alpha * acc_ref[...] + _dot(vt_ref[0, 0, j], pr.astype(BF16))
        m_ref[...] = m_new

    n_full = (i * tq) // tk
    sa_ref[...] = scores(0)

    def pair_body(jj, carry):
        j0 = 2 * jj
        sb_ref[...] = scores(j0 + 1)
        absorb(sa_ref, j0, False)
        sa_ref[...] = scores(j0 + 2)
        absorb(sb_ref, j0 + 1, False)
        return carry

    lax.fori_loop(0, n_full // 2, pair_body, 0)

    @pl.when(n_full % 2 == 1)
    def _():
        sb_ref[...] = scores(n_full)
        absorb(sa_ref, n_full - 1, False)
        absorb(sb_ref, n_full, True)

    @pl.when(n_full % 2 == 0)
    def _():
        absorb(sa_ref, n_full, True)

    lam = (jnp.exp(jnp.sum(lq1_ref[...] * lk1_ref[...], axis=-1, keepdims=True))
           - jnp.exp(jnp.sum(lq2_ref[...] * lk2_ref[...], axis=-1, keepdims=True))
           + LAMBDA_INIT)
    o2 = acc_ref[...] / l_ref[...]
    o = o2[:, :tq] - lam * o2[:, tq:]
    ot = o.T
    ms = jnp.mean(ot * ot, axis=-1, keepdims=True)
    y = ot * lax.rsqrt(ms + NORM_EPS) * gain_ref[...] * (1.0 - LAMBDA_INIT)
    o_ref[0] = y.astype(BF16)


def _attn_call(q, k, vt, lq1, lk1, lq2, lk2, subln_gain, tq):
    B, S, _ = q.shape
    ns, tk = vt.shape[2], vt.shape[4]
    slopes = jnp.asarray([2.0 ** (-8.0 * (i + 1) / DIFF_HEADS) for i in range(DIFF_HEADS)], F32)
    vec = lambda n: pl.BlockSpec((1, n), lambda b, h, i, sl: (0, 0))
    grid_spec = pltpu.PrefetchScalarGridSpec(
        num_scalar_prefetch=1,
        grid=(B, DIFF_HEADS, S // tq),
        in_specs=[pl.BlockSpec((1, tq, 128), lambda b, h, i, sl: (b, i, h)),
                  pl.BlockSpec((1, S, 128), lambda b, h, i, sl: (b, 0, h)),
                  pl.BlockSpec((1, 1, ns, 128, tk), lambda b, h, i, sl: (b, h, 0, 0, 0)),
                  vec(HEAD_DIM), vec(HEAD_DIM), vec(HEAD_DIM), vec(HEAD_DIM), vec(128)],
        out_specs=pl.BlockSpec((1, tq, 128), lambda b, h, i, sl: (b, i, h)),
        scratch_shapes=[pltpu.VMEM((tk, 2 * tq), F32), pltpu.VMEM((tk, 2 * tq), F32),
                        pltpu.VMEM((128, 2 * tq), F32), pltpu.VMEM((1, 2 * tq), F32),
                        pltpu.VMEM((1, 2 * tq), F32)])
    return pl.pallas_call(
        functools.partial(_attn_kernel, tq=tq, tk=tk),
        grid_spec=grid_spec,
        out_shape=jax.ShapeDtypeStruct((B, S, DIFF_WIDTH), BF16),
        compiler_params=pltpu.CompilerParams(
            dimension_semantics=("arbitrary", "arbitrary", "arbitrary"),
            vmem_limit_bytes=VMEM_LIMIT),
        name="attn",
    )(slopes, q, k, vt, lq1.reshape(1, -1), lk1.reshape(1, -1), lq2.reshape(1, -1),
      lk2.reshape(1, -1), subln_gain.reshape(1, -1))


def _pack_bf16_pairs(x):
    w = x.shape[1] // 2
    lo = pltpu.bitcast(x[:, :w].astype(BF16).astype(F32), U32)
    hi = pltpu.bitcast(x[:, w:].astype(BF16).astype(F32), U32)
    return (lo >> 16) | (hi & jnp.uint32(0xFFFF0000))


def _unpack_bf16_pairs(p):
    lo = pltpu.bitcast(p << 16, F32)
    hi = pltpu.bitcast(p & jnp.uint32(0xFFFF0000), F32)
    return lo, hi


ROW_SUB = (D_MODEL // 2) // 128


def _store_rows(ref, x2d):
    for s in range(ROW_SUB):
        ref[:, s, :] = x2d[:, s * 128:(s + 1) * 128]


def _load_rows(ref):
    return jnp.concatenate([ref[:, s, :] for s in range(ROW_SUB)], axis=1)


def _outproj_kernel(yr_ref, yd_ref, x_ref, mod_ref, wo_r_ref, wo_d_ref, g2_ref, rw_hi_ref,
                    rw_lo_ref, rb_ref, x1_out, hp_out, idx_out, w_out, rank_out, cnt_out,
                    carry_ref, *, tm):
    t = pl.program_id(0)

    @pl.when(t == 0)
    def _():
        carry_ref[...] = jnp.zeros_like(carry_ref)

    mix = _dot(yr_ref[...], wo_r_ref[...]) + _dot(yd_ref[...], wo_d_ref[...])
    gate1 = mod_ref[0, 2:3, :]
    shift2 = mod_ref[0, 3:4, :]
    scale2 = mod_ref[0, 4:5, :]
    x1 = x_ref[...] + gate1 * mix
    x1_out[...] = x1
    ms = jnp.mean(x1 * x1, axis=-1, keepdims=True)
    h = x1 * lax.rsqrt(ms + NORM_EPS) * g2_ref[...] * (1.0 + scale2) + shift2
    _store_rows(hp_out, _pack_bf16_pairs(h))

    hi = h.astype(BF16)
    lo = (h - hi.astype(F32)).astype(BF16)
    logits = (_dot(hi, rw_hi_ref[...]) + _dot(hi, rw_lo_ref[...]) + _dot(lo, rw_hi_ref[...])
              + rb_ref[...])

    eidx = lax.broadcasted_iota(I32, logits.shape, 1)
    col4 = lax.broadcasted_iota(I32, (tm, TOP_K), 1)
    lg = logits
    vals, idxs = [], []
    onehot = jnp.zeros(logits.shape, F32)
    for _ in range(TOP_K):
        m = jnp.max(lg, axis=-1, keepdims=True)
        ix = jnp.min(jnp.where(lg == m, eidx, N_EXPERTS), axis=-1, keepdims=True)
        sel = eidx == ix
        vals.append(m)
        idxs.append(ix)
        onehot = onehot + sel.astype(F32)
        lg = jnp.where(sel, -jnp.inf, lg)
    es = [jnp.exp(v - vals[0]) for v in vals]
    den = es[0] + es[1] + es[2] + es[3]

    tri = (lax.broadcasted_iota(I32, (tm, tm), 0) >
           lax.broadcasted_iota(I32, (tm, tm), 1)).astype(BF16)
    prefix = _dot(tri, onehot.astype(BF16)) + carry_ref[...]
    idx4 = jnp.zeros((tm, TOP_K), I32)
    w4 = jnp.zeros((tm, TOP_K), F32)
    rank4 = jnp.zeros((tm, TOP_K), I32)
    for kk in range(TOP_K):
        rk = jnp.sum(jnp.where(eidx == idxs[kk], prefix, 0.0), axis=-1, keepdims=True)
        idx4 = jnp.where(col4 == kk, idxs[kk], idx4)
        w4 = jnp.where(col4 == kk, es[kk] / den, w4)
        rank4 = jnp.where(col4 == kk, rk.astype(I32), rank4)
    idx_out[...] = idx4
    w_out[...] = w4
    rank_out[...] = rank4
    carry_ref[...] = carry_ref[...] + jnp.sum(onehot, axis=0, keepdims=True)
    cnt_out[...] = carry_ref[...].astype(I32)


def _outproj_call(y_rwkv, y_diff, x, mod6, w_out, norm2_gain, router_w, router_b, S, tm):
    T = x.shape[0]
    tiles_per_seq = S // tm
    rw_hi = router_w.astype(BF16)
    rw_lo = (router_w - rw_hi.astype(F32)).astype(BF16)
    tok = lambda w: pl.BlockSpec((tm, w), lambda t: (t, 0))
    full = lambda shape: pl.BlockSpec(shape, lambda t: (0,) * len(shape))
    return pl.pallas_call(
        functools.partial(_outproj_kernel, tm=tm),
        grid=(T // tm,),
        in_specs=[tok(RWKV_WIDTH), tok(DIFF_WIDTH), tok(D_MODEL),
                  pl.BlockSpec((1, 6, D_MODEL), lambda t: (t // tiles_per_seq, 0, 0)),
                  full((RWKV_WIDTH, D_MODEL)), full((DIFF_WIDTH, D_MODEL)), full((1, D_MODEL)),
                  full((D_MODEL, N_EXPERTS)), full((D_MODEL, N_EXPERTS)), full((1, N_EXPERTS))],
        out_specs=[tok(D_MODEL), pl.BlockSpec((tm, ROW_SUB, 128), lambda t: (t, 0, 0)),
                   tok(TOP_K), tok(TOP_K), tok(TOP_K), full((1, N_EXPERTS))],
        out_shape=[jax.ShapeDtypeStruct((T, D_MODEL), F32),
                   jax.ShapeDtypeStruct((T, ROW_SUB, 128), U32),
                   jax.ShapeDtypeStruct((T, TOP_K), I32),
                   jax.ShapeDtypeStruct((T, TOP_K), F32),
                   jax.ShapeDtypeStruct((T, TOP_K), I32),
                   jax.ShapeDtypeStruct((1, N_EXPERTS), I32)],
        scratch_shapes=[pltpu.VMEM((1, N_EXPERTS), F32)],
        compiler_params=pltpu.CompilerParams(
            dimension_semantics=("arbitrary",), vmem_limit_bytes=VMEM_LIMIT),
        name="outproj",
    )(y_rwkv, y_diff, x, mod6, w_out[:RWKV_WIDTH].astype(BF16), w_out[RWKV_WIDTH:].astype(BF16),
      norm2_gain.reshape(1, -1), rw_hi, rw_lo, router_b.reshape(1, -1))


def _dispatch_kernel(last_ref, cnt_ref, nu_ref, slots_hbm, hp_ref, xs_hbm, slot_smem, zero_ref, sem,
                     zsem, *, tm, n_blocks):
    t = pl.program_id(0)

    @pl.when(t == 0)
    def _():
        zero_ref[...] = jnp.zeros_like(zero_ref)

        def zero_copy(row):
            return pltpu.make_async_copy(zero_ref, xs_hbm.at[pl.ds(row, MOE_BLOCK)], zsem)

        def start_block(bk, carry):
            zero_copy(bk * MOE_BLOCK).start()
            return carry

        def wait_block(bk, carry):
            zero_copy(0).wait()
            return carry

        for e in range(N_EXPERTS):
            @pl.when(cnt_ref[e] > 0)
            def _():
                zero_copy(last_ref[e]).start()
        lax.fori_loop(nu_ref[0], n_blocks, start_block, 0)
        for e in range(N_EXPERTS):
            @pl.when(cnt_ref[e] > 0)
            def _():
                zero_copy(0).wait()
        lax.fori_loop(nu_ref[0], n_blocks, wait_block, 0)

    cp = pltpu.make_async_copy(slots_hbm.at[pl.ds(t * (tm * TOP_K), tm * TOP_K)], slot_smem, sem.at[0])
    cp.start()
    cp.wait()

    def body(i, carry):
        for kk in range(TOP_K):
            slot = slot_smem[i * TOP_K + kk]
            pltpu.make_async_copy(hp_ref.at[i], xs_hbm.at[slot], sem.at[1]).start()
        return carry

    lax.fori_loop(0, tm, body, 0)
    for kk in range(TOP_K):
        pltpu.make_async_copy(hp_ref, xs_hbm.at[pl.ds(0, tm)], sem.at[1]).wait()


def _dispatch_call(hp, slots_flat, last_block_row, counts, n_used, n_blocks, tm):
    T = hp.shape[0]
    n_rows = n_blocks * MOE_BLOCK
    grid_spec = pltpu.PrefetchScalarGridSpec(
        num_scalar_prefetch=3,
        grid=(T // tm,),
        in_specs=[pl.BlockSpec(memory_space=pl.ANY),
                  pl.BlockSpec((tm, ROW_SUB, 128), lambda t, last, cnt, nu: (t, 0, 0))],
        out_specs=pl.BlockSpec(memory_space=pl.ANY),
        scratch_shapes=[pltpu.SMEM((tm * TOP_K,), I32),
                        pltpu.VMEM((MOE_BLOCK, ROW_SUB, 128), U32),
                        pltpu.SemaphoreType.DMA((2,)),
                        pltpu.SemaphoreType.DMA(())])
    return pl.pallas_call(
        functools.partial(_dispatch_kernel, tm=tm, n_blocks=n_blocks),
        grid_spec=grid_spec,
        out_shape=jax.ShapeDtypeStruct((n_rows, ROW_SUB, 128), U32),
        compiler_params=pltpu.CompilerParams(
            dimension_semantics=("arbitrary",), vmem_limit_bytes=VMEM_LIMIT),
        name="dispatch",
    )(last_block_row, counts, n_used, slots_flat, hp)


def _experts_kernel(be_ref, nu_ref, xs_ref, wgu_a_ref, wgu_b_ref, bgu_ref, wd_ref, bd_ref, ys_ref):
    i = pl.program_id(0)

    @pl.when(i < nu_ref[0])
    def _():
        xa, xb = _unpack_bf16_pairs(_load_rows(xs_ref))
        gu = (_dot(xa.astype(BF16), wgu_a_ref[0]) + _dot(xb.astype(BF16), wgu_b_ref[0])
              + bgu_ref[0])
        gate = jnp.minimum(gu[:, :D_EXPERT], SWIGLU_LIMIT)
        up = jnp.clip(gu[:, D_EXPERT:], -SWIGLU_LIMIT, SWIGLU_LIMIT)
        act = (up + 1.0) * (gate * jax.nn.sigmoid(SWIGLU_ALPHA * gate))
        y = _dot(act.astype(BF16), wd_ref[0]) + bd_ref[0]
        _store_rows(ys_ref, _pack_bf16_pairs(y))

    @pl.when(i >= nu_ref[0])
    def _():
        ys_ref[...] = jnp.zeros_like(ys_ref)


def _experts_call(xs, block_e, n_used, w_gate_up, b_gate_up, w_down, b_down, n_blocks):
    half = D_MODEL // 2
    wgu = w_gate_up.astype(BF16)
    wd = w_down.astype(BF16)
    blk = lambda i, be, nu: jnp.minimum(i, nu[0] - 1)
    rows = pl.BlockSpec((MOE_BLOCK, ROW_SUB, 128), lambda i, be, nu: (blk(i, be, nu), 0, 0))
    grid_spec = pltpu.PrefetchScalarGridSpec(
        num_scalar_prefetch=2,
        grid=(n_blocks,),
        in_specs=[rows,
                  pl.BlockSpec((1, half, 2 * D_EXPERT), lambda i, be, nu: (be[i], 0, 0)),
                  pl.BlockSpec((1, half, 2 * D_EXPERT), lambda i, be, nu: (be[i], 1, 0)),
                  pl.BlockSpec((1, 1, 2 * D_EXPERT), lambda i, be, nu: (be[i], 0, 0)),
                  pl.BlockSpec((1, D_EXPERT, D_MODEL), lambda i, be, nu: (be[i], 0, 0)),
                  pl.BlockSpec((1, 1, D_MODEL), lambda i, be, nu: (be[i], 0, 0))],
        out_specs=pl.BlockSpec((MOE_BLOCK, ROW_SUB, 128), lambda i, be, nu: (i, 0, 0)))
    return pl.pallas_call(
        _experts_kernel,
        grid_spec=grid_spec,
        out_shape=jax.ShapeDtypeStruct((n_blocks * MOE_BLOCK, ROW_SUB, 128), U32),
        compiler_params=pltpu.CompilerParams(
            dimension_semantics=("arbitrary",), vmem_limit_bytes=VMEM_LIMIT),
        name="experts",
    )(block_e, n_used, xs, wgu, wgu, b_gate_up.reshape(N_EXPERTS, 1, -1), wd,
      b_down.reshape(N_EXPERTS, 1, -1))


def _combine_kernel(slots_hbm, ys_hbm, x1_ref, w_ref, mod_ref, fg_ref, o_ref,
                    slot_smem, buf_ref, sem, *, tm):
    t = pl.program_id(0)
    cp = pltpu.make_async_copy(slots_hbm.at[pl.ds(t * (tm * TOP_K), tm * TOP_K)], slot_smem, sem.at[0])
    cp.start()
    cp.wait()

    def body(i, carry):
        for kk in range(TOP_K):
            slot = slot_smem[i * TOP_K + kk]
            pltpu.make_async_copy(ys_hbm.at[slot], buf_ref.at[kk, i], sem.at[1]).start()
        return carry

    lax.fori_loop(0, tm, body, 0)
    for kk in range(TOP_K):
        pltpu.make_async_copy(ys_hbm.at[pl.ds(0, tm)], buf_ref.at[kk], sem.at[1]).wait()

    w = w_ref[...]
    acc_lo = jnp.zeros((tm, D_MODEL // 2), F32)
    acc_hi = jnp.zeros((tm, D_MODEL // 2), F32)
    for kk in range(TOP_K):
        lo, hi = _unpack_bf16_pairs(_load_rows(buf_ref.at[kk]))
        wk = w[:, kk:kk + 1]
        acc_lo = acc_lo + wk * lo
        acc_hi = acc_hi + wk * hi
    moe = jnp.concatenate([acc_lo, acc_hi], axis=1)
    gate2 = mod_ref[0, 5:6, :]
    x2 = x1_ref[...] + gate2 * moe
    ms = jnp.mean(x2 * x2, axis=-1, keepdims=True)
    o_ref[...] = x2 * lax.rsqrt(ms + NORM_EPS) * fg_ref[...]


def _combine_call(ys, slots_flat, x1, top_w, mod6, final_gain, S, tm):
    T = x1.shape[0]
    tiles_per_seq = S // tm
    return pl.pallas_call(
        functools.partial(_combine_kernel, tm=tm),
        grid=(T // tm,),
        in_specs=[pl.BlockSpec(memory_space=pl.ANY),
                  pl.BlockSpec(memory_space=pl.ANY),
                  pl.BlockSpec((tm, D_MODEL), lambda t: (t, 0)),
                  pl.BlockSpec((tm, TOP_K), lambda t: (t, 0)),
                  pl.BlockSpec((1, 6, D_MODEL), lambda t: (t // tiles_per_seq, 0, 0)),
                  pl.BlockSpec((1, D_MODEL), lambda t: (0, 0))],
        out_specs=pl.BlockSpec((tm, D_MODEL), lambda t: (t, 0)),
        out_shape=jax.ShapeDtypeStruct((T, D_MODEL), F32),
        scratch_shapes=[pltpu.SMEM((tm * TOP_K,), I32),
                        pltpu.VMEM((TOP_K, tm, ROW_SUB, 128), U32),
                        pltpu.SemaphoreType.DMA((2,))],
        compiler_params=pltpu.CompilerParams(
            dimension_semantics=("arbitrary",), vmem_limit_bytes=VMEM_LIMIT),
        name="combine",
    )(slots_flat, ys, x1, top_w, mod6, final_gain.reshape(1, -1))


def _forward(x, c, mod_w, mod_b, norm1_gain, w_in, rwkv_shift_mu, rwkv_w0, rwkv_w_up, rwkv_a0,
             rwkv_a_up, rwkv_g_up, rwkv_k_k, rwkv_k_a, rwkv_r_k, rwkv_gn_gain, rwkv_gn_bias,
             diff_lambda_q1, diff_lambda_k1, diff_lambda_q2, diff_lambda_k2, diff_subln_gain,
             w_out, norm2_gain, router_w, router_b, w_gate_up, b_gate_up, w_down, b_down,
             final_gain):
    B, S, D = x.shape
    T = B * S
    tm_in = min(512, S)
    tq = min(256, S)
    tm = min(256, S)

    mod6 = _mod_call(c, mod_w, mod_b).reshape(B, 6, D)
    (r, k, v, a, b, g, lw, q, kat, vt) = _inproj_call(
        x, mod6, norm1_gain, w_in, rwkv_shift_mu, rwkv_w0, rwkv_w_up, rwkv_a0, rwkv_a_up,
        rwkv_g_up, rwkv_k_k, rwkv_k_a, tm_in)
    y_rwkv = _rwkv_call(r, k, v, a, b, g, lw, rwkv_r_k, rwkv_gn_gain, rwkv_gn_bias,
                        min(4, S // CHUNK))
    y_diff = _attn_call(q, kat, vt, diff_lambda_q1, diff_lambda_k1, diff_lambda_q2,
                        diff_lambda_k2, diff_subln_gain, tq)

    x1, hp, top_idx, top_w, rank, counts = _outproj_call(
        y_rwkv.reshape(T, -1), y_diff.reshape(T, -1), x.reshape(T, D), mod6, w_out, norm2_gain,
        router_w, router_b, S, tm)

    counts = counts.reshape(N_EXPERTS)
    padded = ((counts + MOE_BLOCK - 1) // MOE_BLOCK) * MOE_BLOCK
    pad_ends = jnp.cumsum(padded)
    pad_starts = pad_ends - padded
    n_blocks = -(-(T * TOP_K + N_EXPERTS * (MOE_BLOCK - 1)) // MOE_BLOCK)
    n_used = (pad_ends[-1] // MOE_BLOCK).astype(I32).reshape(1)
    block_start = jnp.minimum(jnp.arange(n_blocks, dtype=I32), n_used[0] - 1) * MOE_BLOCK
    block_e = jnp.minimum(jnp.searchsorted(pad_ends, block_start, side='right'),
                          N_EXPERTS - 1).astype(I32)
    slots = (pad_starts.astype(I32)[top_idx] + rank).reshape(T * TOP_K)
    last_block_row = (pad_ends - MOE_BLOCK).astype(I32)

    xs = _dispatch_call(hp, slots, last_block_row, counts.astype(I32), n_used, n_blocks, tm)
    ys = _experts_call(xs, block_e, n_used, w_gate_up, b_gate_up, w_down, b_down, n_blocks)
    out = _combine_call(ys, slots, x1, top_w, mod6, final_gain, S, tm)
    return out.reshape(B, S, D)


def kernel(x, c, mod_w, mod_b, norm1_gain, w_in, rwkv_shift_mu, rwkv_w0, rwkv_w_up, rwkv_a0, rwkv_a_up, rwkv_g_up, rwkv_k_k, rwkv_k_a, rwkv_r_k, rwkv_gn_gain, rwkv_gn_bias, diff_lambda_q1, diff_lambda_k1, diff_lambda_q2, diff_lambda_k2, diff_subln_gain, w_out, norm2_gain, router_w, router_b, w_gate_up, b_gate_up, w_down, b_down, final_gain):
    return _forward(x, c, mod_w[0], mod_b[0], norm1_gain[0], w_in[0], rwkv_shift_mu[0], rwkv_w0[0],
                    rwkv_w_up[0], rwkv_a0[0], rwkv_a_up[0], rwkv_g_up[0], rwkv_k_k[0], rwkv_k_a[0],
                    rwkv_r_k[0].reshape(-1), rwkv_gn_gain[0], rwkv_gn_bias[0], diff_lambda_q1[0],
                    diff_lambda_k1[0], diff_lambda_q2[0], diff_lambda_k2[0], diff_subln_gain[0],
                    w_out[0], norm2_gain[0], router_w[0], router_b[0], w_gate_up[0], b_gate_up[0],
                    w_down[0], b_down[0], final_gain)
```

```python
import functools
import math

import jax
import jax.numpy as jnp
from jax import lax
from jax.experimental import pallas as pl
from jax.experimental.pallas import tpu as pltpu

F32 = jnp.float32
BF16 = jnp.bfloat16
I32 = jnp.int32
U32 = jnp.uint32

D_MODEL = 1024
RWKV_WIDTH = 512
RWKV_HEADS = 8
HEAD_DIM = 64
RWKV_COLS = 3 * RWKV_WIDTH + 64 + 64 + 128
DIFF_WIDTH = 512
DIFF_HEADS = 4
DIFF_COLS = 3 * DIFF_WIDTH
N_EXPERTS = 32
TOP_K = 4
D_EXPERT = 1024
SWIGLU_LIMIT = 7.0
SWIGLU_ALPHA = 1.702
MOE_BLOCK = 512
NORM_EPS = 1e-5
GN_EPS = 64e-5
LAMBDA_INIT = 0.8 - 0.6 * math.exp(-0.3 * 0)

CHUNK = 64
GROUP = 4
GW = GROUP * HEAD_DIM
VMEM_LIMIT = 56 * 1024 * 1024


def _dot(a, b):
    return jnp.dot(a, b, preferred_element_type=F32)


def _dot_nt(a, b):
    return lax.dot_general(a, b, (((1,), (1,)), ((), ())), preferred_element_type=F32)


def _dot_tn(a, b):
    return lax.dot_general(a, b, (((0,), (0,)), ((), ())), preferred_element_type=F32)


def _split_dot(x, w_bf16):
    hi = x.astype(BF16)
    lo = (x - hi.astype(F32)).astype(BF16)
    return _dot(hi, w_bf16) + _dot(lo, w_bf16)


def _mod_kernel(c_ref, w_ref, b_ref, o_ref):
    c = c_ref[...]
    s = c * jax.nn.sigmoid(c)
    o_ref[...] = _dot(s, w_ref[...]) + b_ref[...]


def _mod_call(c, mod_w, mod_b):
    B = c.shape[0]
    n = mod_w.shape[1]
    tn = 1536
    return pl.pallas_call(
        _mod_kernel,
        grid=(n // tn,),
        in_specs=[pl.BlockSpec((B, D_MODEL), lambda j: (0, 0)),
                  pl.BlockSpec((D_MODEL, tn), lambda j: (0, j)),
                  pl.BlockSpec((1, tn), lambda j: (0, j))],
        out_specs=pl.BlockSpec((B, tn), lambda j: (0, j)),
        out_shape=jax.ShapeDtypeStruct((B, n), F32),
        compiler_params=pltpu.CompilerParams(
            dimension_semantics=("arbitrary",), vmem_limit_bytes=VMEM_LIMIT),
        name="mod",
    )(c, mod_w, mod_b.reshape(1, n))


def _inproj_kernel(x_ref, mod_ref, g1_ref, wrw_ref, wat_ref, mu_ref, w0_ref, wup_ref,
                   a0_ref, aup_ref, gup_ref, kk_ref, ka_ref, hsum_ref,
                   r_out, k_out, v_out, a_out, b_out, g_out, lw_out,
                   q_out, kat_out, vt_out, carry_ref, *, tm):
    s = pl.program_id(1)
    x = x_ref[0]
    ms = jnp.mean(x * x, axis=-1, keepdims=True)
    shift1 = mod_ref[0, 0:1, :]
    scale1 = mod_ref[0, 1:2, :]
    h = x * lax.rsqrt(ms + NORM_EPS) * g1_ref[...] * (1.0 + scale1) + shift1
    hb = h.astype(BF16)

    p = _dot(hb, wrw_ref[...])

    @pl.when(s == 0)
    def _():
        carry_ref[...] = jnp.zeros_like(carry_ref)

    rolled = pltpu.roll(p, shift=1, axis=0)
    row = lax.broadcasted_iota(I32, p.shape, 0)
    prev = jnp.where(row == 0, carry_ref[...], rolled)
    carry_ref[...] = p[tm - 1:tm, :]
    ps = p + mu_ref[...] * (prev - p)

    r = ps[:, 0:512]
    k = ps[:, 512:1024]
    v = ps[:, 1024:1536]
    lo2 = ps[:, 1536:1664]
    g_lo = ps[:, 1664:1792]
    z = w0_ref[...] + _dot(jnp.tanh(lo2).astype(BF16), wup_ref[...])
    nz = -z
    softplus = jnp.maximum(nz, 0.0) + jnp.log(1.0 + jnp.exp(-jnp.abs(nz)))
    w = -softplus - 0.5
    lw_out[0] = -jnp.exp(w)
    a = jax.nn.sigmoid(a0_ref[...] + _dot(lo2.astype(BF16), aup_ref[...]))
    g = _dot(jax.nn.sigmoid(g_lo).astype(BF16), gup_ref[...])
    kk = k * kk_ref[...]
    ssq = _split_dot(kk * kk, hsum_ref[...])
    kk = kk / jnp.maximum(jnp.sqrt(ssq), 1e-12)
    k = k * (1.0 + (a - 1.0) * ka_ref[...])
    r_out[0] = r.astype(BF16)
    k_out[0] = k.astype(BF16)
    v_out[0] = v.astype(BF16)
    a_out[0] = (-kk).astype(BF16)
    b_out[0] = (kk * a).astype(BF16)
    g_out[0] = g.astype(BF16)

    pa = _dot(hb, wat_ref[...])
    q_out[0] = (pa[:, 0:512] * (1.0 / math.sqrt(HEAD_DIM))).astype(BF16)
    kat_out[0] = pa[:, 512:1024].astype(BF16)
    for hh in range(DIFF_HEADS):
        vh = pa[:, 1024 + hh * 128:1024 + (hh + 1) * 128]
        vt_out[0, hh, 0] = vh.T.astype(BF16)


def _head_sum_matrix(width):
    i = jnp.arange(width) // HEAD_DIM
    return (i[:, None] == i[None, :]).astype(BF16)


def _inproj_call(x, mod6, norm1_gain, w_in, mu, w0, w_up, a0, a_up, g_up, k_k, k_a, tm):
    B, S, _ = x.shape
    ns = S // tm
    w_rw = w_in[:, :RWKV_COLS].astype(BF16)
    w_at = w_in[:, RWKV_COLS:].astype(BF16)
    zeros = jnp.zeros((64, RWKV_WIDTH), F32)
    wup_p = jnp.concatenate([w_up, zeros], axis=0).astype(BF16)
    aup_p = jnp.concatenate([zeros, a_up], axis=0).astype(BF16)
    row = lambda v: v.reshape(1, -1)
    full = lambda shape: pl.BlockSpec(shape, lambda b, s: (0,) * len(shape))
    tok = lambda w: pl.BlockSpec((1, tm, w), lambda b, s: (b, s, 0))
    rw_shape = jax.ShapeDtypeStruct((B, S, RWKV_WIDTH), BF16)
    out_shape = [rw_shape] * 6 + [
        jax.ShapeDtypeStruct((B, S, RWKV_WIDTH), F32),
        jax.ShapeDtypeStruct((B, S, DIFF_WIDTH), BF16),
        jax.ShapeDtypeStruct((B, S, DIFF_WIDTH), BF16),
        jax.ShapeDtypeStruct((B, DIFF_HEADS, ns, 128, tm), BF16)]
    out_specs = [tok(RWKV_WIDTH)] * 7 + [tok(DIFF_WIDTH)] * 2 + [
        pl.BlockSpec((1, DIFF_HEADS, 1, 128, tm), lambda b, s: (b, 0, s, 0, 0))]
    return pl.pallas_call(
        functools.partial(_inproj_kernel, tm=tm),
        grid=(B, ns),
        in_specs=[tok(D_MODEL),
                  pl.BlockSpec((1, 6, D_MODEL), lambda b, s: (b, 0, 0)),
                  full((1, D_MODEL)),
                  full((D_MODEL, RWKV_COLS)), full((D_MODEL, DIFF_COLS)),
                  full((1, RWKV_COLS)), full((1, RWKV_WIDTH)), full((128, RWKV_WIDTH)),
                  full((1, RWKV_WIDTH)), full((128, RWKV_WIDTH)), full((128, RWKV_WIDTH)),
                  full((1, RWKV_WIDTH)), full((1, RWKV_WIDTH)),
                  full((RWKV_WIDTH, RWKV_WIDTH))],
        out_specs=out_specs,
        out_shape=out_shape,
        scratch_shapes=[pltpu.VMEM((1, RWKV_COLS), F32)],
        compiler_params=pltpu.CompilerParams(
            dimension_semantics=("arbitrary", "arbitrary"), vmem_limit_bytes=VMEM_LIMIT),
        name="inproj",
    )(x, mod6, row(norm1_gain), w_rw, w_at, row(mu), row(w0), wup_p, row(a0), aup_p,
      g_up.astype(BF16), row(k_k), row(k_a), _head_sum_matrix(RWKV_WIDTH))


def _rwkv_kernel(r_ref, k_ref, v_ref, a_ref, b_ref, g_ref, lw_ref, rk_ref, gng_ref, gnb_ref,
                 hsum_ref, tri_ref, y_ref, s0_ref, s1_ref, *, nck):
    C = CHUNK
    n_grp = RWKV_HEADS // GROUP
    s_refs = (s0_ref, s1_ref)

    @pl.when(pl.program_id(1) == 0)
    def _():
        s0_ref[...] = jnp.zeros_like(s0_ref)
        s1_ref[...] = jnp.zeros_like(s1_ref)

    ti = lax.broadcasted_iota(I32, (C, GW), 0)
    si = lax.broadcasted_iota(I32, (C, GW), 1) % C
    incl = si <= ti
    strict = si < ti
    eye = (si == ti).astype(F32)
    bdmask = (lax.broadcasted_iota(I32, (GW, GW), 0) // HEAD_DIM ==
              lax.broadcasted_iota(I32, (GW, GW), 1) // HEAD_DIM)
    hsum = hsum_ref[...]
    chains = [(ck, g) for ck in range(nck) for g in range(n_grp)]

    def bd(xb):
        return jnp.where(bdmask, jnp.concatenate([xb] * GROUP, axis=0), jnp.zeros((), xb.dtype))

    def cut(x, ch):
        ck, g = ch
        return x[ck * C:(ck + 1) * C, g * GW:(g + 1) * GW]

    def rows_of_chunk(x, row):
        return jnp.concatenate(
            [jnp.broadcast_to(x[ck * C + row:ck * C + row + 1, :], (C, x.shape[1]))
             for ck in range(nck)], axis=0)

    lw = lw_ref[0]
    L = _split_dot_left(tri_ref[...], lw)
    Lx = L - lw
    Lc = rows_of_chunk(L, C - 1)
    rho = rows_of_chunk(L, C // 2 - 1)
    r = r_ref[0].astype(F32)
    k = k_ref[0].astype(F32)
    vb = v_ref[0]
    a = a_ref[0].astype(F32)
    b = b_ref[0].astype(F32)
    e_k = jnp.exp(rho - L)
    rt = (r * jnp.exp(L - rho)).astype(BF16)
    at = (a * jnp.exp(Lx - rho)).astype(BF16)
    kt = (k * e_k).astype(BF16)
    bt = (b * e_k).astype(BF16)
    r0 = (r * jnp.exp(L)).astype(BF16)
    a0 = (a * jnp.exp(Lx)).astype(BF16)
    e_o = jnp.exp(Lc - L)
    kh = (k * e_o).astype(BF16)
    bh = (b * e_o).astype(BF16)
    gc = jnp.exp(Lc)

    lhs = {ch: jnp.concatenate([cut(rt, ch), cut(at, ch)], axis=0) for ch in chains}
    ak = {ch: _dot_nt(lhs[ch], bd(cut(kt, ch))) for ch in chains}
    ab = {ch: _dot_nt(lhs[ch], bd(cut(bt, ch))) for ch in chains}
    a_rk = {ch: jnp.where(incl, ak[ch][:C], 0.0).astype(BF16) for ch in chains}
    a_ak = {ch: jnp.where(strict, ak[ch][C:], 0.0).astype(BF16) for ch in chains}
    a_rb = {ch: jnp.where(incl, ab[ch][:C], 0.0).astype(BF16) for ch in chains}
    n = {ch: jnp.where(strict, ab[ch][C:], 0.0) for ch in chains}

    p = {ch: eye + n[ch] for ch in chains}
    xb = {ch: n[ch].astype(BF16) for ch in chains}
    x = {ch: _dot(xb[ch], bd(xb[ch])) for ch in chains}
    for _ in range(4):
        xb = {ch: x[ch].astype(BF16) for ch in chains}
        o = {ch: _dot(jnp.concatenate([xb[ch], p[ch].astype(BF16)], axis=0), bd(xb[ch]))
             for ch in chains}
        x = {ch: o[ch][:C] for ch in chains}
        p = {ch: p[ch] + o[ch][C:] for ch in chains}
    o = {ch: _dot(p[ch].astype(BF16), bd(x[ch].astype(BF16))) for ch in chains}
    tb = {ch: (p[ch] + o[ch]).astype(BF16) for ch in chains}

    av = {ch: _dot(jnp.concatenate([a_ak[ch], a_rk[ch]], axis=0), bd(cut(vb, ch))) for ch in chains}
    w1 = {ch: _dot(tb[ch], bd(cut(a0, ch))).astype(BF16) for ch in chains}
    w2 = {ch: _dot(tb[ch], bd(av[ch][:C].astype(BF16))) for ch in chains}

    sv = [s_refs[g][...] for g in range(n_grp)]
    ys = {}
    for ck in range(nck):
        grp = [(ck, g) for g in range(n_grp)]
        rws = [_dot_nt(jnp.concatenate([cut(r0, ch), w1[ch]], axis=0), sv[ch[1]].astype(BF16))
               for ch in grp]
        ub = [(rws[g][C:] + w2[(ck, g)]).astype(BF16) for g in range(n_grp)]
        upd = [_dot_tn(jnp.concatenate([ub[g], cut(vb, (ck, g))], axis=0),
                       jnp.concatenate([cut(bh, (ck, g)), cut(kh, (ck, g))], axis=0))
               for g in range(n_grp)]
        for g in range(n_grp):
            ch = (ck, g)
            ys[ch] = rws[g][:C] + _dot(a_rb[ch], bd(ub[g])) + av[ch][C:]
        sv = [sv[g] * cut(gc, (ck, g))[0:1, :] + jnp.where(bdmask, upd[g], 0.0)
              for g in range(n_grp)]
    for g in range(n_grp):
        s_refs[g][...] = sv[g]

    for g in range(n_grp):
        sl = slice(g * GW, (g + 1) * GW)
        y = jnp.concatenate([ys[(ck, g)] for ck in range(nck)], axis=0)
        mean = _split_dot(y, hsum) * (1.0 / HEAD_DIM)
        d = y - mean
        var = _split_dot(d * d, hsum) * (1.0 / HEAD_DIM)
        yn = d * lax.rsqrt(var + GN_EPS) * gng_ref[:, sl] + gnb_ref[:, sl]
        bonus = _split_dot(r[:, sl] * k[:, sl] * rk_ref[:, sl], hsum) * vb[:, sl].astype(F32)
        y_ref[0, :, sl] = ((yn + bonus) * g_ref[0, :, sl].astype(F32)).astype(BF16)


def _split_dot_left(w_bf16, x):
    hi = x.astype(BF16)
    lo = (x - hi.astype(F32)).astype(BF16)
    return _dot(w_bf16, hi) + _dot(w_bf16, lo)


def _rwkv_call(r, k, v, a, b, g, lw, r_k, gn_gain, gn_bias, nck):
    B, S, _ = r.shape
    tb = nck * CHUNK
    idx = jnp.arange(tb)
    tri = ((idx[:, None] // CHUNK == idx[None, :] // CHUNK) &
           (idx[:, None] >= idx[None, :])).astype(BF16)
    tok = pl.BlockSpec((1, tb, RWKV_WIDTH), lambda bb, c: (bb, c, 0))
    par = pl.BlockSpec((1, RWKV_WIDTH), lambda bb, c: (0, 0))
    return pl.pallas_call(
        functools.partial(_rwkv_kernel, nck=nck),
        grid=(B, S // tb),
        in_specs=[tok] * 7 + [par] * 3 + [pl.BlockSpec((GW, GW), lambda bb, c: (0, 0)),
                                          pl.BlockSpec((tb, tb), lambda bb, c: (0, 0))],
        out_specs=tok,
        out_shape=jax.ShapeDtypeStruct((B, S, RWKV_WIDTH), BF16),
        scratch_shapes=[pltpu.VMEM((GW, GW), F32)] * (RWKV_HEADS // GROUP),
        compiler_params=pltpu.CompilerParams(
            dimension_semantics=("arbitrary", "arbitrary"), vmem_limit_bytes=VMEM_LIMIT),
        name="rwkv",
    )(r, k, v, a, b, g, lw, r_k.reshape(1, -1), gn_gain.reshape(1, -1), gn_bias.reshape(1, -1),
      _head_sum_matrix(GW), tri)


def _tree_reduce(op, x):
    while x.shape[0] > 8:
        h = x.shape[0] // 2
        x = op(x[:h], x[h:])
    return x


def _attn_kernel(slopes_ref, q_ref, k_ref, vt_ref, lq1_ref, lk1_ref, lq2_ref, lk2_ref,
                 gain_ref, o_ref, sa_ref, sb_ref, acc_ref, m_ref, l_ref, *, tq, tk):
    hh = pl.program_id(1)
    i = pl.program_id(2)
    slope = slopes_ref[hh]

    q = q_ref[0].astype(F32)
    lane = lax.broadcasted_iota(I32, (tq, 256), 1)
    qrow = lax.broadcasted_iota(I32, (tq, 256), 0)
    qa = (qrow >> 5).astype(F32) * (32.0 * slope)
    qb = (qrow & 31).astype(F32) * slope
    qbias = jnp.where(lane < 130, 1.0, jnp.where(lane == 130, -qa, jnp.where(lane == 131, -qb, 0.0)))
    qpad = jnp.concatenate([q, jnp.zeros_like(q)], axis=1)
    q_aug = []
    for c in range(2):
        in_comp = (lane >= c * HEAD_DIM) & (lane < (c + 1) * HEAD_DIM)
        q_aug.append(jnp.where(in_comp, qpad, jnp.where(lane >= 128, qbias, 0.0)))
    klane = lax.broadcasted_iota(I32, (tk, 128), 1)
    krow = lax.broadcasted_iota(I32, (tk, 128), 0)
    ka = (krow >> 5).astype(F32) * (32.0 * slope)
    kb_ = (krow & 31).astype(F32) * slope
    kbias = jnp.where(klane == 0, ka, jnp.where(klane == 1, kb_, jnp.where(klane < 4, 1.0, 0.0))
                      ).astype(BF16)

    qt_both = jnp.concatenate([q_aug[0].T, q_aug[1].T], axis=1).astype(BF16)

    m_ref[...] = jnp.full_like(m_ref, -jnp.inf)
    l_ref[...] = jnp.zeros_like(l_ref)
    acc_ref[...] = jnp.zeros_like(acc_ref)

    def scores(j):
        kb = k_ref[0, pl.ds(pl.multiple_of(j * tk, tk), tk), :]
        return _dot(jnp.concatenate([kb, kbias], axis=1), qt_both)

    def absorb(s_ref, j, masked):
        s = s_ref[...]
        off = j * tk - i * tq
        cj = slope * off.astype(F32)
        if masked:
            keep = (lax.broadcasted_iota(I32, (tk, tq), 0) -
                    lax.broadcasted_iota(I32, (tk, tq), 1) + off) <= 0
            s = jnp.where(jnp.concatenate([keep, keep], axis=1), s, -jnp.inf)
        m_loc = jnp.max(_tree_reduce(jnp.maximum, s), axis=0, keepdims=True)
        m_old = m_ref[...]
        m_new = jnp.maximum(m_old, m_loc + cj)
        pr = jnp.exp(s - (m_new - cj))
        alpha = jnp.exp(m_old - m_new)
        l_ref[...] = alpha * l_ref[...] + jnp.sum(_tree_reduce(jnp.add, pr), axis=0, keepdims=True)
        acc_ref[...] = alpha * acc_ref[...] + _dot(vt_ref[0, 0, j], pr.astype(BF16))
        m_ref[...] = m_new

    n_full = (i * tq) // tk
    sa_ref[...] = scores(0)

    def pair_body(jj, carry):
        j0 = 2 * jj
        sb_ref[...] = scores(j0 + 1)
        absorb(sa_ref, j0, False)
        sa_ref[...] = scores(j0 + 2)
        absorb(sb_ref, j0 + 1, False)
        return carry

    lax.fori_loop(0, n_full // 2, pair_body, 0)

    @pl.when(n_full % 2 == 1)
    def _():
        sb_ref[...] = scores(n_full)
        absorb(sa_ref, n_full - 1, False)
        absorb(sb_ref, n_full, True)

    @pl.when(n_full % 2 == 0)
    def _():
        absorb(sa_ref, n_full, True)

    lam = (jnp.exp(jnp.sum(lq1_ref[...] * lk1_ref[...], axis=-1, keepdims=True))
           - jnp.exp(jnp.sum(lq2_ref[...] * lk2_ref[...], axis=-1, keepdims=True))
           + LAMBDA_INIT)
    o2 = acc_ref[...] / l_ref[...]
    o = o2[:, :tq] - lam * o2[:, tq:]
    ot = o.T
    ms = jnp.mean(ot * ot, axis=-1, keepdims=True)
    y = ot * lax.rsqrt(ms + NORM_EPS) * gain_ref[...] * (1.0 - LAMBDA_INIT)
    o_ref[0] = y.astype(BF16)


def _attn_call(q, k, vt, lq1, lk1, lq2, lk2, subln_gain, tq):
    B, S, _ = q.shape
    ns, tk = vt.shape[2], vt.shape[4]
    slopes = jnp.asarray([2.0 ** (-8.0 * (i + 1) / DIFF_HEADS) for i in range(DIFF_HEADS)], F32)
    vec = lambda n: pl.BlockSpec((1, n), lambda b, h, i, sl: (0, 0))
    grid_spec = pltpu.PrefetchScalarGridSpec(
        num_scalar_prefetch=1,
        grid=(B, DIFF_HEADS, S // tq),
        in_specs=[pl.BlockSpec((1, tq, 128), lambda b, h, i, sl: (b, i, h)),
                  pl.BlockSpec((1, S, 128), lambda b, h, i, sl: (b, 0, h)),
                  pl.BlockSpec((1, 1, ns, 128, tk), lambda b, h, i, sl: (b, h, 0, 0, 0)),
                  vec(HEAD_DIM), vec(HEAD_DIM), vec(HEAD_DIM), vec(HEAD_DIM), vec(128)],
        out_specs=pl.BlockSpec((1, tq, 128), lambda b, h, i, sl: (b, i, h)),
        scratch_shapes=[pltpu.VMEM((tk, 2 * tq), F32), pltpu.VMEM((tk, 2 * tq), F32),
                        pltpu.VMEM((128, 2 * tq), F32), pltpu.VMEM((1, 2 * tq), F32),
                        pltpu.VMEM((1, 2 * tq), F32)])
    return pl.pallas_call(
        functools.partial(_attn_kernel, tq=tq, tk=tk),
        grid_spec=grid_spec,
        out_shape=jax.ShapeDtypeStruct((B, S, DIFF_WIDTH), BF16),
        compiler_params=pltpu.CompilerParams(
            dimension_semantics=("arbitrary", "arbitrary", "arbitrary"),
            vmem_limit_bytes=VMEM_LIMIT),
        name="attn",
    )(slopes, q, k, vt, lq1.reshape(1, -1), lk1.reshape(1, -1), lq2.reshape(1, -1),
      lk2.reshape(1, -1), subln_gain.reshape(1, -1))


def _pack_bf16_pairs(x):
    w = x.shape[1] // 2
    lo = pltpu.bitcast(x[:, :w].astype(BF16).astype(F32), U32)
    hi = pltpu.bitcast(x[:, w:].astype(BF16).astype(F32), U32)
    return (lo >> 16) | (hi & jnp.uint32(0xFFFF0000))


def _unpack_bf16_pairs(p):
    lo = pltpu.bitcast(p << 16, F32)
    hi = pltpu.bitcast(p & jnp.uint32(0xFFFF0000), F32)
    return lo, hi


ROW_SUB = (D_MODEL // 2) // 128


def _store_rows(ref, x2d):
    for s in range(ROW_SUB):
        ref[:, s, :] = x2d[:, s * 128:(s + 1) * 128]


def _load_rows(ref):
    return jnp.concatenate([ref[:, s, :] for s in range(ROW_SUB)], axis=1)


def _outproj_kernel(yr_ref, yd_ref, x_ref, mod_ref, wo_r_ref, wo_d_ref, g2_ref, rw_hi_ref,
                    rw_lo_ref, rb_ref, x1_out, hp_out, idx_out, w_out, rank_out, cnt_out,
                    carry_ref, *, tm):
    t = pl.program_id(0)

    @pl.when(t == 0)
    def _():
        carry_ref[...] = jnp.zeros_like(carry_ref)

    mix = _dot(yr_ref[...], wo_r_ref[...]) + _dot(yd_ref[...], wo_d_ref[...])
    gate1 = mod_ref[0, 2:3, :]
    shift2 = mod_ref[0, 3:4, :]
    scale2 = mod_ref[0, 4:5, :]
    x1 = x_ref[...] + gate1 * mix
    x1_out[...] = x1
    ms = jnp.mean(x1 * x1, axis=-1, keepdims=True)
    h = x1 * lax.rsqrt(ms + NORM_EPS) * g2_ref[...] * (1.0 + scale2) + shift2
    _store_rows(hp_out, _pack_bf16_pairs(h))

    hi = h.astype(BF16)
    lo = (h - hi.astype(F32)).astype(BF16)
    logits = (_dot(hi, rw_hi_ref[...]) + _dot(hi, rw_lo_ref[...]) + _dot(lo, rw_hi_ref[...])
              + rb_ref[...])

    eidx = lax.broadcasted_iota(I32, logits.shape, 1)
    col4 = lax.broadcasted_iota(I32, (tm, TOP_K), 1)
    lg = logits
    vals, idxs = [], []
    onehot = jnp.zeros(logits.shape, F32)
    for _ in range(TOP_K):
        m = jnp.max(lg, axis=-1, keepdims=True)
        ix = jnp.min(jnp.where(lg == m, eidx, N_EXPERTS), axis=-1, keepdims=True)
        sel = eidx == ix
        vals.append(m)
        idxs.append(ix)
        onehot = onehot + sel.astype(F32)
        lg = jnp.where(sel, -jnp.inf, lg)
    es = [jnp.exp(v - vals[0]) for v in vals]
    den = es[0] + es[1] + es[2] + es[3]

    tri = (lax.broadcasted_iota(I32, (tm, tm), 0) >
           lax.broadcasted_iota(I32, (tm, tm), 1)).astype(BF16)
    prefix = _dot(tri, onehot.astype(BF16)) + carry_ref[...]
    idx4 = jnp.zeros((tm, TOP_K), I32)
    w4 = jnp.zeros((tm, TOP_K), F32)
    rank4 = jnp.zeros((tm, TOP_K), I32)
    for kk in range(TOP_K):
        rk = jnp.sum(jnp.where(eidx == idxs[kk], prefix, 0.0), axis=-1, keepdims=True)
        idx4 = jnp.where(col4 == kk, idxs[kk], idx4)
        w4 = jnp.where(col4 == kk, es[kk] / den, w4)
        rank4 = jnp.where(col4 == kk, rk.astype(I32), rank4)
    idx_out[...] = idx4
    w_out[...] = w4
    rank_out[...] = rank4
    carry_ref[...] = carry_ref[...] + jnp.sum(onehot, axis=0, keepdims=True)
    cnt_out[...] = carry_ref[...].astype(I32)


def _outproj_call(y_rwkv, y_diff, x, mod6, w_out, norm2_gain, router_w, router_b, S, tm):
    T = x.shape[0]
    tiles_per_seq = S // tm
    rw_hi = router_w.astype(BF16)
    rw_lo = (router_w - rw_hi.astype(F32)).astype(BF16)
    tok = lambda w: pl.BlockSpec((tm, w), lambda t: (t, 0))
    full = lambda shape: pl.BlockSpec(shape, lambda t: (0,) * len(shape))
    return pl.pallas_call(
        functools.partial(_outproj_kernel, tm=tm),
        grid=(T // tm,),
        in_specs=[tok(RWKV_WIDTH), tok(DIFF_WIDTH), tok(D_MODEL),
                  pl.BlockSpec((1, 6, D_MODEL), lambda t: (t // tiles_per_seq, 0, 0)),
                  full((RWKV_WIDTH, D_MODEL)), full((DIFF_WIDTH, D_MODEL)), full((1, D_MODEL)),
                  full((D_MODEL, N_EXPERTS)), full((D_MODEL, N_EXPERTS)), full((1, N_EXPERTS))],
        out_specs=[tok(D_MODEL), pl.BlockSpec((tm, ROW_SUB, 128), lambda t: (t, 0, 0)),
                   tok(TOP_K), tok(TOP_K), tok(TOP_K), full((1, N_EXPERTS))],
        out_shape=[jax.ShapeDtypeStruct((T, D_MODEL), F32),
                   jax.ShapeDtypeStruct((T, ROW_SUB, 128), U32),
                   jax.ShapeDtypeStruct((T, TOP_K), I32),
                   jax.ShapeDtypeStruct((T, TOP_K), F32),
                   jax.ShapeDtypeStruct((T, TOP_K), I32),
                   jax.ShapeDtypeStruct((1, N_EXPERTS), I32)],
        scratch_shapes=[pltpu.VMEM((1, N_EXPERTS), F32)],
        compiler_params=pltpu.CompilerParams(
            dimension_semantics=("arbitrary",), vmem_limit_bytes=VMEM_LIMIT),
        name="outproj",
    )(y_rwkv, y_diff, x, mod6, w_out[:RWKV_WIDTH].astype(BF16), w_out[RWKV_WIDTH:].astype(BF16),
      norm2_gain.reshape(1, -1), rw_hi, rw_lo, router_b.reshape(1, -1))


def _dispatch_kernel(last_ref, cnt_ref, nu_ref, slots_hbm, hp_ref, xs_hbm, slot_smem, own_ref,
                     zero_ref, ssem, rsem, zsem, *, tm, n_blocks, n_tiles):
    t = pl.program_id(0)

    @pl.when(t == 0)
    def _():
        zero_ref[...] = jnp.zeros_like(zero_ref)

        def zero_copy(row):
            return pltpu.make_async_copy(zero_ref, xs_hbm.at[pl.ds(row, MOE_BLOCK)], zsem)

        def start_block(bk, carry):
            zero_copy(bk * MOE_BLOCK).start()
            return carry

        def wait_block(bk, carry):
            zero_copy(0).wait()
            return carry

        for e in range(N_EXPERTS):
            @pl.when(cnt_ref[e] > 0)
            def _():
                zero_copy(last_ref[e]).start()
        lax.fori_loop(nu_ref[0], n_blocks, start_block, 0)
        for e in range(N_EXPERTS):
            @pl.when(cnt_ref[e] > 0)
            def _():
                zero_copy(0).wait()
        lax.fori_loop(nu_ref[0], n_blocks, wait_block, 0)

    cur = t % 2
    n_idx = tm * TOP_K

    def slot_copy(tile, buf):
        return pltpu.make_async_copy(slots_hbm.at[pl.ds(tile * n_idx, n_idx)], slot_smem.at[buf],
                                     ssem.at[buf])

    def wait_rows(buf):
        for _ in range(TOP_K):
            pltpu.make_async_copy(own_ref.at[buf], xs_hbm.at[pl.ds(0, tm)], rsem.at[buf]).wait()

    @pl.when(t == 0)
    def _():
        slot_copy(0, 0).start()

    @pl.when(t + 1 < n_tiles)
    def _():
        slot_copy(t + 1, 1 - cur).start()

    @pl.when(t >= 2)
    def _():
        wait_rows(cur)

    own_ref[cur] = hp_ref[...]
    slot_copy(t, cur).wait()

    def body(i, carry):
        for kk in range(TOP_K):
            slot = slot_smem[cur, i * TOP_K + kk]
            pltpu.make_async_copy(own_ref.at[cur, i], xs_hbm.at[slot], rsem.at[cur]).start(
                priority=kk % 2)
        return carry

    lax.fori_loop(0, tm, body, 0)

    @pl.when(t == n_tiles - 1)
    def _():
        wait_rows(cur)
        if n_tiles >= 2:
            wait_rows(1 - cur)


def _dispatch_call(hp, slots_flat, last_block_row, counts, n_used, n_blocks, tm):
    T = hp.shape[0]
    n_rows = n_blocks * MOE_BLOCK
    grid_spec = pltpu.PrefetchScalarGridSpec(
        num_scalar_prefetch=3,
        grid=(T // tm,),
        in_specs=[pl.BlockSpec(memory_space=pl.ANY),
                  pl.BlockSpec((tm, ROW_SUB, 128), lambda t, last, cnt, nu: (t, 0, 0))],
        out_specs=pl.BlockSpec(memory_space=pl.ANY),
        scratch_shapes=[pltpu.SMEM((2, tm * TOP_K), I32),
                        pltpu.VMEM((2, tm, ROW_SUB, 128), U32),
                        pltpu.VMEM((MOE_BLOCK, ROW_SUB, 128), U32),
                        pltpu.SemaphoreType.DMA((2,)),
                        pltpu.SemaphoreType.DMA((2,)),
                        pltpu.SemaphoreType.DMA(())])
    return pl.pallas_call(
        functools.partial(_dispatch_kernel, tm=tm, n_blocks=n_blocks, n_tiles=T // tm),
        grid_spec=grid_spec,
        out_shape=jax.ShapeDtypeStruct((n_rows, ROW_SUB, 128), U32),
        compiler_params=pltpu.CompilerParams(
            dimension_semantics=("arbitrary",), vmem_limit_bytes=VMEM_LIMIT),
        name="dispatch",
    )(last_block_row, counts, n_used, slots_flat, hp)


def _experts_kernel(be_ref, nu_ref, xs_ref, wgu_ref, bgu_ref, wd_ref, bd_ref, ys_ref,
                    wgu_bf, wd_bf):
    i = pl.program_id(0)

    @pl.when((i == 0) | (be_ref[i] != be_ref[jnp.maximum(i - 1, 0)]))
    def _():
        wgu_bf[...] = wgu_ref[0].astype(BF16)
        wd_bf[...] = wd_ref[0].astype(BF16)

    @pl.when(i < nu_ref[0])
    def _():
        xa, xb = _unpack_bf16_pairs(_load_rows(xs_ref))
        x = jnp.concatenate([xa.astype(BF16), xb.astype(BF16)], axis=1)
        gu = _dot(x, wgu_bf[...]) + bgu_ref[0]
        gate = jnp.minimum(gu[:, :D_EXPERT], SWIGLU_LIMIT)
        up = jnp.clip(gu[:, D_EXPERT:], -SWIGLU_LIMIT, SWIGLU_LIMIT)
        act = (up + 1.0) * (gate * jax.nn.sigmoid(SWIGLU_ALPHA * gate))
        y = _dot(act.astype(BF16), wd_bf[...]) + bd_ref[0]
        _store_rows(ys_ref, _pack_bf16_pairs(y))

    @pl.when(i >= nu_ref[0])
    def _():
        ys_ref[...] = jnp.zeros_like(ys_ref)


def _experts_call(xs, block_e, n_used, w_gate_up, b_gate_up, w_down, b_down, n_blocks):
    blk = lambda i, be, nu: jnp.minimum(i, nu[0] - 1)
    rows = pl.BlockSpec((MOE_BLOCK, ROW_SUB, 128), lambda i, be, nu: (blk(i, be, nu), 0, 0))
    grid_spec = pltpu.PrefetchScalarGridSpec(
        num_scalar_prefetch=2,
        grid=(n_blocks,),
        in_specs=[rows,
                  pl.BlockSpec((1, D_MODEL, 2 * D_EXPERT), lambda i, be, nu: (be[i], 0, 0)),
                  pl.BlockSpec((1, 1, 2 * D_EXPERT), lambda i, be, nu: (be[i], 0, 0)),
                  pl.BlockSpec((1, D_EXPERT, D_MODEL), lambda i, be, nu: (be[i], 0, 0)),
                  pl.BlockSpec((1, 1, D_MODEL), lambda i, be, nu: (be[i], 0, 0))],
        out_specs=pl.BlockSpec((MOE_BLOCK, ROW_SUB, 128), lambda i, be, nu: (i, 0, 0)),
        scratch_shapes=[pltpu.VMEM((D_MODEL, 2 * D_EXPERT), BF16),
                        pltpu.VMEM((D_EXPERT, D_MODEL), BF16)])
    return pl.pallas_call(
        _experts_kernel,
        grid_spec=grid_spec,
        out_shape=jax.ShapeDtypeStruct((n_blocks * MOE_BLOCK, ROW_SUB, 128), U32),
        compiler_params=pltpu.CompilerParams(
            dimension_semantics=("arbitrary",), vmem_limit_bytes=VMEM_LIMIT),
        name="experts",
    )(block_e, n_used, xs, w_gate_up, b_gate_up.reshape(N_EXPERTS, 1, -1), w_down,
      b_down.reshape(N_EXPERTS, 1, -1))


def _combine_kernel(slots_hbm, ys_hbm, x1_ref, w_ref, mod_ref, fg_ref, o_ref,
                    slot_smem, buf_ref, ssem, rsem, *, tm, n_tiles):
    t = pl.program_id(0)
    cur = t % 2
    n_idx = tm * TOP_K

    def slot_copy(tile, buf):
        return pltpu.make_async_copy(slots_hbm.at[pl.ds(tile * n_idx, n_idx)], slot_smem.at[buf],
                                     ssem.at[buf])

    def gather(buf):
        def body(i, carry):
            for kk in range(TOP_K):
                slot = slot_smem[buf, i * TOP_K + kk]
                pltpu.make_async_copy(ys_hbm.at[slot], buf_ref.at[buf, kk, i], rsem.at[buf]).start(
                    priority=kk % 2)
            return carry

        lax.fori_loop(0, tm, body, 0)

    @pl.when(t == 0)
    def _():
        first = slot_copy(0, 0)
        first.start()
        first.wait()
        gather(0)
        if n_tiles >= 2:
            slot_copy(1, 1).start()

    @pl.when(t + 1 < n_tiles)
    def _():
        slot_copy(t + 1, 1 - cur).wait()
        gather(1 - cur)

    @pl.when(t + 2 < n_tiles)
    def _():
        slot_copy(t + 2, cur).start()

    for kk in range(TOP_K):
        pltpu.make_async_copy(ys_hbm.at[pl.ds(0, tm)], buf_ref.at[cur, kk], rsem.at[cur]).wait()

    w = w_ref[...]
    acc_lo = jnp.zeros((tm, D_MODEL // 2), F32)
    acc_hi = jnp.zeros((tm, D_MODEL // 2), F32)
    for kk in range(TOP_K):
        lo, hi = _unpack_bf16_pairs(_load_rows(buf_ref.at[cur, kk]))
        wk = w[:, kk:kk + 1]
        acc_lo = acc_lo + wk * lo
        acc_hi = acc_hi + wk * hi
    moe = jnp.concatenate([acc_lo, acc_hi], axis=1)
    gate2 = mod_ref[0, 5:6, :]
    x2 = x1_ref[...] + gate2 * moe
    ms = jnp.mean(x2 * x2, axis=-1, keepdims=True)
    o_ref[...] = x2 * lax.rsqrt(ms + NORM_EPS) * fg_ref[...]


def _combine_call(ys, slots_flat, x1, top_w, mod6, final_gain, S, tm):
    T = x1.shape[0]
    tiles_per_seq = S // tm
    return pl.pallas_call(
        functools.partial(_combine_kernel, tm=tm, n_tiles=T // tm),
        grid=(T // tm,),
        in_specs=[pl.BlockSpec(memory_space=pl.ANY),
                  pl.BlockSpec(memory_space=pl.ANY),
                  pl.BlockSpec((tm, D_MODEL), lambda t: (t, 0)),
                  pl.BlockSpec((tm, TOP_K), lambda t: (t, 0)),
                  pl.BlockSpec((1, 6, D_MODEL), lambda t: (t // tiles_per_seq, 0, 0)),
                  pl.BlockSpec((1, D_MODEL), lambda t: (0, 0))],
        out_specs=pl.BlockSpec((tm, D_MODEL), lambda t: (t, 0)),
        out_shape=jax.ShapeDtypeStruct((T, D_MODEL), F32),
        scratch_shapes=[pltpu.SMEM((2, tm * TOP_K), I32),
                        pltpu.VMEM((2, TOP_K, tm, ROW_SUB, 128), U32),
                        pltpu.SemaphoreType.DMA((2,)),
                        pltpu.SemaphoreType.DMA((2,))],
        compiler_params=pltpu.CompilerParams(
            dimension_semantics=("arbitrary",), vmem_limit_bytes=VMEM_LIMIT),
        name="combine",
    )(slots_flat, ys, x1, top_w, mod6, final_gain.reshape(1, -1))


def _forward(x, c, mod_w, mod_b, norm1_gain, w_in, rwkv_shift_mu, rwkv_w0, rwkv_w_up, rwkv_a0,
             rwkv_a_up, rwkv_g_up, rwkv_k_k, rwkv_k_a, rwkv_r_k, rwkv_gn_gain, rwkv_gn_bias,
             diff_lambda_q1, diff_lambda_k1, diff_lambda_q2, diff_lambda_k2, diff_subln_gain,
             w_out, norm2_gain, router_w, router_b, w_gate_up, b_gate_up, w_down, b_down,
             final_gain):
    B, S, D = x.shape
    T = B * S
    tm_in = min(512, S)
    tq = min(256, S)
    tm = min(256, S)

    mod6 = _mod_call(c, mod_w, mod_b).reshape(B, 6, D)
    (r, k, v, a, b, g, lw, q, kat, vt) = _inproj_call(
        x, mod6, norm1_gain, w_in, rwkv_shift_mu, rwkv_w0, rwkv_w_up, rwkv_a0, rwkv_a_up,
        rwkv_g_up, rwkv_k_k, rwkv_k_a, tm_in)
    y_rwkv = _rwkv_call(r, k, v, a, b, g, lw, rwkv_r_k, rwkv_gn_gain, rwkv_gn_bias,
                        min(4, S // CHUNK))
    y_diff = _attn_call(q, kat, vt, diff_lambda_q1, diff_lambda_k1, diff_lambda_q2,
                        diff_lambda_k2, diff_subln_gain, tq)

    x1, hp, top_idx, top_w, rank, counts = _outproj_call(
        y_rwkv.reshape(T, -1), y_diff.reshape(T, -1), x.reshape(T, D), mod6, w_out, norm2_gain,
        router_w, router_b, S, tm)

    counts = counts.reshape(N_EXPERTS)
    padded = ((counts + MOE_BLOCK - 1) // MOE_BLOCK) * MOE_BLOCK
    pad_ends = jnp.cumsum(padded)
    pad_starts = pad_ends - padded
    n_blocks = -(-(T * TOP_K + N_EXPERTS * (MOE_BLOCK - 1)) // MOE_BLOCK)
    n_used = (pad_ends[-1] // MOE_BLOCK).astype(I32).reshape(1)
    block_start = jnp.minimum(jnp.arange(n_blocks, dtype=I32), n_used[0] - 1) * MOE_BLOCK
    block_e = jnp.minimum(jnp.sum(pad_ends[None, :] <= block_start[:, None], axis=1),
                          N_EXPERTS - 1).astype(I32)
    expert_ids = jnp.arange(N_EXPERTS, dtype=I32)
    start_of = jnp.sum(jnp.where(top_idx[..., None] == expert_ids, pad_starts.astype(I32), 0), axis=-1)
    slots = (start_of + rank).reshape(T * TOP_K)
    last_block_row = (pad_ends - MOE_BLOCK).astype(I32)

    xs = _dispatch_call(hp, slots, last_block_row, counts.astype(I32), n_used, n_blocks, tm)
    ys = _experts_call(xs, block_e, n_used, w_gate_up, b_gate_up, w_down, b_down, n_blocks)
    out = _combine_call(ys, slots, x1, top_w, mod6, final_gain, S, tm)
    return out.reshape(B, S, D)


def kernel(x, c, mod_w, mod_b, norm1_gain, w_in, rwkv_shift_mu, rwkv_w0, rwkv_w_up, rwkv_a0, rwkv_a_up, rwkv_g_up, rwkv_k_k, rwkv_k_a, rwkv_r_k, rwkv_gn_gain, rwkv_gn_bias, diff_lambda_q1, diff_lambda_k1, diff_lambda_q2, diff_lambda_k2, diff_subln_gain, w_out, norm2_gain, router_w, router_b, w_gate_up, b_gate_up, w_down, b_down, final_gain):
    return _forward(x, c, mod_w[0], mod_b[0], norm1_gain[0], w_in[0], rwkv_shift_mu[0], rwkv_w0[0],
                    rwkv_w_up[0], rwkv_a0[0], rwkv_a_up[0], rwkv_g_up[0], rwkv_k_k[0], rwkv_k_a[0],
                    rwkv_r_k[0].reshape(-1), rwkv_gn_gain[0], rwkv_gn_bias[0], diff_lambda_q1[0],
                    diff_lambda_k1[0], diff_lambda_q2[0], diff_lambda_k2[0], diff_subln_gain[0],
                    w_out[0], norm2_gain[0], router_w[0], router_b[0], w_gate_up[0], b_gate_up[0],
                    w_down[0], b_down[0], final_gain)
```

```python
import functools
import math

import jax
import jax.numpy as jnp
from jax import lax
from jax.experimental import pallas as pl
from jax.experimental.pallas import tpu as pltpu

F32 = jnp.float32
BF16 = jnp.bfloat16
I32 = jnp.int32
U32 = jnp.uint32

D_MODEL = 1024
RWKV_WIDTH = 512
RWKV_HEADS = 8
HEAD_DIM = 64
RWKV_COLS = 3 * RWKV_WIDTH + 64 + 64 + 128
DIFF_WIDTH = 512
DIFF_HEADS = 4
DIFF_COLS = 3 * DIFF_WIDTH
N_EXPERTS = 32
TOP_K = 4
D_EXPERT = 1024
SWIGLU_LIMIT = 7.0
SWIGLU_ALPHA = 1.702
MOE_BLOCK = 512
EXPERT_SLICE = 256
NORM_EPS = 1e-5
GN_EPS = 64e-5
LAMBDA_INIT = 0.8 - 0.6 * math.exp(-0.3 * 0)

CHUNK = 64
GROUP = 4
GW = GROUP * HEAD_DIM
VMEM_LIMIT = 56 * 1024 * 1024


def _dot(a, b):
    return jnp.dot(a, b, preferred_element_type=F32)


def _dot_nt(a, b):
    return lax.dot_general(a, b, (((1,), (1,)), ((), ())), preferred_element_type=F32)


def _dot_tn(a, b):
    return lax.dot_general(a, b, (((0,), (0,)), ((), ())), preferred_element_type=F32)


def _split_dot(x, w_bf16):
    hi = x.astype(BF16)
    lo = (x - hi.astype(F32)).astype(BF16)
    return _dot(hi, w_bf16) + _dot(lo, w_bf16)


def _mod_kernel(c_ref, w_ref, b_ref, o_ref):
    c = c_ref[...]
    s = c * jax.nn.sigmoid(c)
    o_ref[...] = _dot(s, w_ref[...]) + b_ref[...]


def _mod_call(c, mod_w, mod_b):
    B = c.shape[0]
    n = mod_w.shape[1]
    tn = 1536
    return pl.pallas_call(
        _mod_kernel,
        grid=(n // tn,),
        in_specs=[pl.BlockSpec((B, D_MODEL), lambda j: (0, 0)),
                  pl.BlockSpec((D_MODEL, tn), lambda j: (0, j)),
                  pl.BlockSpec((1, tn), lambda j: (0, j))],
        out_specs=pl.BlockSpec((B, tn), lambda j: (0, j)),
        out_shape=jax.ShapeDtypeStruct((B, n), F32),
        compiler_params=pltpu.CompilerParams(
            dimension_semantics=("arbitrary",), vmem_limit_bytes=VMEM_LIMIT),
        name="mod",
    )(c, mod_w, mod_b.reshape(1, n))


def _inproj_kernel(x_ref, mod_ref, g1_ref, wrw_ref, wat_ref, mu_ref, w0_ref, wup_ref,
                   a0_ref, aup_ref, gup_ref, kk_ref, ka_ref, hsum_ref,
                   r_out, k_out, v_out, a_out, b_out, g_out, lw_out,
                   q_out, kat_out, vt_out, carry_ref, *, tm):
    s = pl.program_id(1)
    x = x_ref[0]
    ms = jnp.mean(x * x, axis=-1, keepdims=True)
    shift1 = mod_ref[0, 0:1, :]
    scale1 = mod_ref[0, 1:2, :]
    h = x * lax.rsqrt(ms + NORM_EPS) * g1_ref[...] * (1.0 + scale1) + shift1
    hb = h.astype(BF16)

    p = _dot(hb, wrw_ref[...])
    pa = _dot(hb, wat_ref[...])

    @pl.when(s == 0)
    def _():
        carry_ref[...] = jnp.zeros_like(carry_ref)

    rolled = pltpu.roll(p, shift=1, axis=0)
    row = lax.broadcasted_iota(I32, p.shape, 0)
    prev = jnp.where(row == 0, carry_ref[...], rolled)
    carry_ref[...] = p[tm - 1:tm, :]
    ps = p + mu_ref[...] * (prev - p)

    r = ps[:, 0:512]
    k = ps[:, 512:1024]
    v = ps[:, 1024:1536]
    lo2 = ps[:, 1536:1664]
    g_lo = ps[:, 1664:1792]
    z = w0_ref[...] + _dot(jnp.tanh(lo2).astype(BF16), wup_ref[...])
    nz = -z
    softplus = jnp.maximum(nz, 0.0) + jnp.log(1.0 + jnp.exp(-jnp.abs(nz)))
    w = -softplus - 0.5
    lw_out[0] = -jnp.exp(w)
    a = jax.nn.sigmoid(a0_ref[...] + _dot(lo2.astype(BF16), aup_ref[...]))
    g = _dot(jax.nn.sigmoid(g_lo).astype(BF16), gup_ref[...])
    kk = k * kk_ref[...]
    ssq = _split_dot(kk * kk, hsum_ref[...])
    kk = kk / jnp.maximum(jnp.sqrt(ssq), 1e-12)
    k = k * (1.0 + (a - 1.0) * ka_ref[...])
    r_out[0] = r.astype(BF16)
    k_out[0] = k.astype(BF16)
    v_out[0] = v.astype(BF16)
    a_out[0] = (-kk).astype(BF16)
    b_out[0] = (kk * a).astype(BF16)
    g_out[0] = g.astype(BF16)

    q_out[0] = (pa[:, 0:512] * (1.0 / math.sqrt(HEAD_DIM))).astype(BF16)
    kat_out[0] = pa[:, 512:1024].astype(BF16)
    for hh in range(DIFF_HEADS):
        vh = pa[:, 1024 + hh * 128:1024 + (hh + 1) * 128]
        vt_out[0, hh, 0] = vh.T.astype(BF16)


def _head_sum_matrix(width):
    i = jnp.arange(width) // HEAD_DIM
    return (i[:, None] == i[None, :]).astype(BF16)


def _inproj_call(x, mod6, norm1_gain, w_in, mu, w0, w_up, a0, a_up, g_up, k_k, k_a, tm):
    B, S, _ = x.shape
    ns = S // tm
    w_rw = w_in[:, :RWKV_COLS].astype(BF16)
    w_at = w_in[:, RWKV_COLS:].astype(BF16)
    zeros = jnp.zeros((64, RWKV_WIDTH), F32)
    wup_p = jnp.concatenate([w_up, zeros], axis=0).astype(BF16)
    aup_p = jnp.concatenate([zeros, a_up], axis=0).astype(BF16)
    row = lambda v: v.reshape(1, -1)
    full = lambda shape: pl.BlockSpec(shape, lambda b, s: (0,) * len(shape))
    tok = lambda w: pl.BlockSpec((1, tm, w), lambda b, s: (b, s, 0))
    rw_shape = jax.ShapeDtypeStruct((B, S, RWKV_WIDTH), BF16)
    out_shape = [rw_shape] * 6 + [
        jax.ShapeDtypeStruct((B, S, RWKV_WIDTH), F32),
        jax.ShapeDtypeStruct((B, S, DIFF_WIDTH), BF16),
        jax.ShapeDtypeStruct((B, S, DIFF_WIDTH), BF16),
        jax.ShapeDtypeStruct((B, DIFF_HEADS, ns, 128, tm), BF16)]
    out_specs = [tok(RWKV_WIDTH)] * 7 + [tok(DIFF_WIDTH)] * 2 + [
        pl.BlockSpec((1, DIFF_HEADS, 1, 128, tm), lambda b, s: (b, 0, s, 0, 0))]
    return pl.pallas_call(
        functools.partial(_inproj_kernel, tm=tm),
        grid=(B, ns),
        in_specs=[tok(D_MODEL),
                  pl.BlockSpec((1, 6, D_MODEL), lambda b, s: (b, 0, 0)),
                  full((1, D_MODEL)),
                  full((D_MODEL, RWKV_COLS)), full((D_MODEL, DIFF_COLS)),
                  full((1, RWKV_COLS)), full((1, RWKV_WIDTH)), full((128, RWKV_WIDTH)),
                  full((1, RWKV_WIDTH)), full((128, RWKV_WIDTH)), full((128, RWKV_WIDTH)),
                  full((1, RWKV_WIDTH)), full((1, RWKV_WIDTH)),
                  full((RWKV_WIDTH, RWKV_WIDTH))],
        out_specs=out_specs,
        out_shape=out_shape,
        scratch_shapes=[pltpu.VMEM((1, RWKV_COLS), F32)],
        compiler_params=pltpu.CompilerParams(
            dimension_semantics=("arbitrary", "arbitrary"), vmem_limit_bytes=VMEM_LIMIT),
        name="inproj",
    )(x, mod6, row(norm1_gain), w_rw, w_at, row(mu), row(w0), wup_p, row(a0), aup_p,
      g_up.astype(BF16), row(k_k), row(k_a), _head_sum_matrix(RWKV_WIDTH))


def _rwkv_kernel(r_ref, k_ref, v_ref, a_ref, b_ref, g_ref, lw_ref, rk_ref, gng_ref, gnb_ref,
                 hsum_ref, tri_ref, y_ref, s0_ref, s1_ref, *, nck):
    C = CHUNK
    n_grp = RWKV_HEADS // GROUP
    s_refs = (s0_ref, s1_ref)

    @pl.when(pl.program_id(1) == 0)
    def _():
        s0_ref[...] = jnp.zeros_like(s0_ref)
        s1_ref[...] = jnp.zeros_like(s1_ref)

    ti = lax.broadcasted_iota(I32, (C, GW), 0)
    si = lax.broadcasted_iota(I32, (C, GW), 1) % C
    incl = si <= ti
    strict = si < ti
    eye = (si == ti).astype(F32)
    bdmask = (lax.broadcasted_iota(I32, (GW, GW), 0) // HEAD_DIM ==
              lax.broadcasted_iota(I32, (GW, GW), 1) // HEAD_DIM)
    hsum = hsum_ref[...]
    chains = [(ck, g) for ck in range(nck) for g in range(n_grp)]

    def bd(xb):
        return jnp.where(bdmask, jnp.concatenate([xb] * GROUP, axis=0), jnp.zeros((), xb.dtype))

    def cut(x, ch):
        ck, g = ch
        return x[ck * C:(ck + 1) * C, g * GW:(g + 1) * GW]

    def rows_of_chunk(x, row):
        return jnp.concatenate(
            [jnp.broadcast_to(x[ck * C + row:ck * C + row + 1, :], (C, x.shape[1]))
             for ck in range(nck)], axis=0)

    lw = lw_ref[0]
    L = _split_dot_left(tri_ref[...], lw)
    Lx = L - lw
    Lc = rows_of_chunk(L, C - 1)
    rho = rows_of_chunk(L, C // 2 - 1)
    r = r_ref[0].astype(F32)
    k = k_ref[0].astype(F32)
    vb = v_ref[0]
    a = a_ref[0].astype(F32)
    b = b_ref[0].astype(F32)
    e_k = jnp.exp(rho - L)
    rt = (r * jnp.exp(L - rho)).astype(BF16)
    at = (a * jnp.exp(Lx - rho)).astype(BF16)
    kt = (k * e_k).astype(BF16)
    bt = (b * e_k).astype(BF16)
    r0 = (r * jnp.exp(L)).astype(BF16)
    a0 = (a * jnp.exp(Lx)).astype(BF16)
    e_o = jnp.exp(Lc - L)
    kh = (k * e_o).astype(BF16)
    bh = (b * e_o).astype(BF16)
    gc = jnp.exp(Lc)

    lhs = {ch: jnp.concatenate([cut(rt, ch), cut(at, ch)], axis=0) for ch in chains}
    ak = {ch: _dot_nt(lhs[ch], bd(cut(kt, ch))) for ch in chains}
    ab = {ch: _dot_nt(lhs[ch], bd(cut(bt, ch))) for ch in chains}
    a_rk = {ch: jnp.where(incl, ak[ch][:C], 0.0).astype(BF16) for ch in chains}
    a_ak = {ch: jnp.where(strict, ak[ch][C:], 0.0).astype(BF16) for ch in chains}
    a_rb = {ch: jnp.where(incl, ab[ch][:C], 0.0).astype(BF16) for ch in chains}
    n = {ch: jnp.where(strict, ab[ch][C:], 0.0) for ch in chains}

    p = {ch: eye + n[ch] for ch in chains}
    xb = {ch: n[ch].astype(BF16) for ch in chains}
    x = {ch: _dot(xb[ch], bd(xb[ch])) for ch in chains}
    for _ in range(4):
        xb = {ch: x[ch].astype(BF16) for ch in chains}
        o = {ch: _dot(jnp.concatenate([xb[ch], p[ch].astype(BF16)], axis=0), bd(xb[ch]))
             for ch in chains}
        x = {ch: o[ch][:C] for ch in chains}
        p = {ch: p[ch] + o[ch][C:] for ch in chains}
    o = {ch: _dot(p[ch].astype(BF16), bd(x[ch].astype(BF16))) for ch in chains}
    tb = {ch: (p[ch] + o[ch]).astype(BF16) for ch in chains}

    av = {ch: _dot(jnp.concatenate([a_ak[ch], a_rk[ch]], axis=0), bd(cut(vb, ch))) for ch in chains}
    w1 = {ch: _dot(tb[ch], bd(cut(a0, ch))).astype(BF16) for ch in chains}
    w2 = {ch: _dot(tb[ch], bd(av[ch][:C].astype(BF16))) for ch in chains}

    sv = [s_refs[g][...] for g in range(n_grp)]
    ys = {}
    for ck in range(nck):
        grp = [(ck, g) for g in range(n_grp)]
        rws = [_dot_nt(jnp.concatenate([cut(r0, ch), w1[ch]], axis=0), sv[ch[1]].astype(BF16))
               for ch in grp]
        ub = [(rws[g][C:] + w2[(ck, g)]).astype(BF16) for g in range(n_grp)]
        upd = [_dot_tn(jnp.concatenate([ub[g], cut(vb, (ck, g))], axis=0),
                       jnp.concatenate([cut(bh, (ck, g)), cut(kh, (ck, g))], axis=0))
               for g in range(n_grp)]
        for g in range(n_grp):
            ch = (ck, g)
            ys[ch] = rws[g][:C] + _dot(a_rb[ch], bd(ub[g])) + av[ch][C:]
        sv = [sv[g] * cut(gc, (ck, g))[0:1, :] + jnp.where(bdmask, upd[g], 0.0)
              for g in range(n_grp)]
    for g in range(n_grp):
        s_refs[g][...] = sv[g]

    for g in range(n_grp):
        sl = slice(g * GW, (g + 1) * GW)
        y = jnp.concatenate([ys[(ck, g)] for ck in range(nck)], axis=0)
        mean = _split_dot(y, hsum) * (1.0 / HEAD_DIM)
        d = y - mean
        var = _split_dot(d * d, hsum) * (1.0 / HEAD_DIM)
        yn = d * lax.rsqrt(var + GN_EPS) * gng_ref[:, sl] + gnb_ref[:, sl]
        bonus = _split_dot(r[:, sl] * k[:, sl] * rk_ref[:, sl], hsum) * vb[:, sl].astype(F32)
        y_ref[0, :, sl] = ((yn + bonus) * g_ref[0, :, sl].astype(F32)).astype(BF16)


def _split_dot_left(w_bf16, x):
    hi = x.astype(BF16)
    lo = (x - hi.astype(F32)).astype(BF16)
    return _dot(w_bf16, hi) + _dot(w_bf16, lo)


def _rwkv_call(r, k, v, a, b, g, lw, r_k, gn_gain, gn_bias, nck):
    B, S, _ = r.shape
    tb = nck * CHUNK
    idx = jnp.arange(tb)
    tri = ((idx[:, None] // CHUNK == idx[None, :] // CHUNK) &
           (idx[:, None] >= idx[None, :])).astype(BF16)
    tok = pl.BlockSpec((1, tb, RWKV_WIDTH), lambda bb, c: (bb, c, 0))
    par = pl.BlockSpec((1, RWKV_WIDTH), lambda bb, c: (0, 0))
    return pl.pallas_call(
        functools.partial(_rwkv_kernel, nck=nck),
        grid=(B, S // tb),
        in_specs=[tok] * 7 + [par] * 3 + [pl.BlockSpec((GW, GW), lambda bb, c: (0, 0)),
                                          pl.BlockSpec((tb, tb), lambda bb, c: (0, 0))],
        out_specs=tok,
        out_shape=jax.ShapeDtypeStruct((B, S, RWKV_WIDTH), BF16),
        scratch_shapes=[pltpu.VMEM((GW, GW), F32)] * (RWKV_HEADS // GROUP),
        compiler_params=pltpu.CompilerParams(
            dimension_semantics=("arbitrary", "arbitrary"), vmem_limit_bytes=VMEM_LIMIT),
        name="rwkv",
    )(r, k, v, a, b, g, lw, r_k.reshape(1, -1), gn_gain.reshape(1, -1), gn_bias.reshape(1, -1),
      _head_sum_matrix(GW), tri)


def _tree_reduce(op, x):
    while x.shape[0] > 8:
        h = x.shape[0] // 2
        x = op(x[:h], x[h:])
    return x


def _attn_kernel(slopes_ref, q_ref, k_ref, vt_ref, lq1_ref, lk1_ref, lq2_ref, lk2_ref,
                 gain_ref, o_ref, sa_ref, sb_ref, acc_ref, m_ref, l_ref, *, tq, tk):
    hh = pl.program_id(1)
    i = pl.program_id(2)
    slope = slopes_ref[hh]

    q = q_ref[0].astype(F32)
    lane = lax.broadcasted_iota(I32, (tq, 256), 1)
    qrow = lax.broadcasted_iota(I32, (tq, 256), 0)
    qa = (qrow >> 5).astype(F32) * (32.0 * slope)
    qb = (qrow & 31).astype(F32) * slope
    qbias = jnp.where(lane < 130, 1.0, jnp.where(lane == 130, -qa, jnp.where(lane == 131, -qb, 0.0)))
    qpad = jnp.concatenate([q, jnp.zeros_like(q)], axis=1)
    q_aug = []
    for c in range(2):
        in_comp = (lane >= c * HEAD_DIM) & (lane < (c + 1) * HEAD_DIM)
        q_aug.append(jnp.where(in_comp, qpad, jnp.where(lane >= 128, qbias, 0.0)))
    klane = lax.broadcasted_iota(I32, (tk, 128), 1)
    krow = lax.broadcasted_iota(I32, (tk, 128), 0)
    ka = (krow >> 5).astype(F32) * (32.0 * slope)
    kb_ = (krow & 31).astype(F32) * slope
    kbias = jnp.where(klane == 0, ka, jnp.where(klane == 1, kb_, jnp.where(klane < 4, 1.0, 0.0))
                      ).astype(BF16)

    qt_both = jnp.concatenate([q_aug[0].T, q_aug[1].T], axis=1).astype(BF16)

    m_ref[...] = jnp.full_like(m_ref, -jnp.inf)
    l_ref[...] = jnp.zeros_like(l_ref)
    acc_ref[...] = jnp.zeros_like(acc_ref)

    def scores(j):
        kb = k_ref[0, pl.ds(pl.multiple_of(j * tk, tk), tk), :]
        return _dot(jnp.concatenate([kb, kbias], axis=1), qt_both)

    def absorb(s_ref, j, masked):
        s = s_ref[...]
        off = j * tk - i * tq
        cj = slope * off.astype(F32)
        if masked:
            keep = (lax.broadcasted_iota(I32, (tk, tq), 0) -
                    lax.broadcasted_iota(I32, (tk, tq), 1) + off) <= 0
            s = jnp.where(jnp.concatenate([keep, keep], axis=1), s, -jnp.inf)
        m_loc = jnp.max(_tree_reduce(jnp.maximum, s), axis=0, keepdims=True)
        m_old = m_ref[...]
        m_new = jnp.maximum(m_old, m_loc + cj)
        pr = jnp.exp(s - (m_new - cj))
        alpha = jnp.exp(m_old - m_new)
        l_ref[...] = alpha * l_ref[...] + jnp.sum(_tree_reduce(jnp.add, pr), axis=0, keepdims=True)
        acc_ref[...] = alpha * acc_ref[...] + _dot(vt_ref[0, 0, j], pr.astype(BF16))
        m_ref[...] = m_new

    n_full = (i * tq) // tk
    sa_ref[...] = scores(0)

    def pair_body(jj, carry):
        j0 = 2 * jj
        sb_ref[...] = scores(j0 + 1)
        absorb(sa_ref, j0, False)
        sa_ref[...] = scores(j0 + 2)
        absorb(sb_ref, j0 + 1, False)
        return carry

    lax.fori_loop(0, n_full // 2, pair_body, 0)

    @pl.when(n_full % 2 == 1)
    def _():
        sb_ref[...] = scores(n_full)
        absorb(sa_ref, n_full - 1, False)
        absorb(sb_ref, n_full, True)

    @pl.when(n_full % 2 == 0)
    def _():
        absorb(sa_ref, n_full, True)

    lam = (jnp.exp(jnp.sum(lq1_ref[...] * lk1_ref[...], axis=-1, keepdims=True))
           - jnp.exp(jnp.sum(lq2_ref[...] * lk2_ref[...], axis=-1, keepdims=True))
           + LAMBDA_INIT)
    o2 = acc_ref[...] / l_ref[...]
    o = o2[:, :tq] - lam * o2[:, tq:]
    ot = o.T
    ms = jnp.mean(ot * ot, axis=-1, keepdims=True)
    y = ot * lax.rsqrt(ms + NORM_EPS) * gain_ref[...] * (1.0 - LAMBDA_INIT)
    o_ref[0] = y.astype(BF16)


def _attn_call(q, k, vt, lq1, lk1, lq2, lk2, subln_gain, tq):
    B, S, _ = q.shape
    ns, tk = vt.shape[2], vt.shape[4]
    slopes = jnp.asarray([2.0 ** (-8.0 * (i + 1) / DIFF_HEADS) for i in range(DIFF_HEADS)], F32)
    vec = lambda n: pl.BlockSpec((1, n), lambda b, h, i, sl: (0, 0))
    grid_spec = pltpu.PrefetchScalarGridSpec(
        num_scalar_prefetch=1,
        grid=(B, DIFF_HEADS, S // tq),
        in_specs=[pl.BlockSpec((1, tq, 128), lambda b, h, i, sl: (b, i, h)),
                  pl.BlockSpec((1, S, 128), lambda b, h, i, sl: (b, 0, h)),
                  pl.BlockSpec((1, 1, ns, 128, tk), lambda b, h, i, sl: (b, h, 0, 0, 0)),
                  vec(HEAD_DIM), vec(HEAD_DIM), vec(HEAD_DIM), vec(HEAD_DIM), vec(128)],
        out_specs=pl.BlockSpec((1, tq, 128), lambda b, h, i, sl: (b, i, h)),
        scratch_shapes=[pltpu.VMEM((tk, 2 * tq), F32), pltpu.VMEM((tk, 2 * tq), F32),
                        pltpu.VMEM((128, 2 * tq), F32), pltpu.VMEM((1, 2 * tq), F32),
                        pltpu.VMEM((1, 2 * tq), F32)])
    return pl.pallas_call(
        functools.partial(_attn_kernel, tq=tq, tk=tk),
        grid_spec=grid_spec,
        out_shape=jax.ShapeDtypeStruct((B, S, DIFF_WIDTH), BF16),
        compiler_params=pltpu.CompilerParams(
            dimension_semantics=("arbitrary", "arbitrary", "arbitrary"),
            vmem_limit_bytes=VMEM_LIMIT),
        name="attn",
    )(slopes, q, k, vt, lq1.reshape(1, -1), lk1.reshape(1, -1), lq2.reshape(1, -1),
      lk2.reshape(1, -1), subln_gain.reshape(1, -1))


def _pack_bf16_pairs(x):
    w = x.shape[1] // 2
    lo = pltpu.bitcast(x[:, :w].astype(BF16).astype(F32), U32)
    hi = pltpu.bitcast(x[:, w:].astype(BF16).astype(F32), U32)
    return (lo >> 16) | (hi & jnp.uint32(0xFFFF0000))


def _unpack_bf16_pairs(p):
    lo = pltpu.bitcast(p << 16, F32)
    hi = pltpu.bitcast(p & jnp.uint32(0xFFFF0000), F32)
    return lo, hi


ROW_SUB = (D_MODEL // 2) // 128


def _store_rows(ref, x2d):
    for s in range(ROW_SUB):
        ref[:, s, :] = x2d[:, s * 128:(s + 1) * 128]


def _load_rows(ref):
    return jnp.concatenate([ref[:, s, :] for s in range(ROW_SUB)], axis=1)


def _outproj_kernel(yr_ref, yd_ref, x_ref, mod_ref, wo_r_ref, wo_d_ref, g2_ref, rw_hi_ref,
                    rw_lo_ref, rb_ref, x1_out, hp_out, idx_out, w_out, rank_out, cnt_out,
                    carry_ref, *, tm):
    t = pl.program_id(0)

    @pl.when(t == 0)
    def _():
        carry_ref[...] = jnp.zeros_like(carry_ref)

    mix = _dot(yr_ref[...], wo_r_ref[...]) + _dot(yd_ref[...], wo_d_ref[...])
    gate1 = mod_ref[0, 2:3, :]
    shift2 = mod_ref[0, 3:4, :]
    scale2 = mod_ref[0, 4:5, :]
    x1 = x_ref[...] + gate1 * mix
    x1_out[...] = x1
    ms = jnp.mean(x1 * x1, axis=-1, keepdims=True)
    h = x1 * lax.rsqrt(ms + NORM_EPS) * g2_ref[...] * (1.0 + scale2) + shift2
    _store_rows(hp_out, _pack_bf16_pairs(h))

    hi = h.astype(BF16)
    lo = (h - hi.astype(F32)).astype(BF16)
    logits = (_dot(hi, rw_hi_ref[...]) + _dot(hi, rw_lo_ref[...]) + _dot(lo, rw_hi_ref[...])
              + rb_ref[...])

    eidx = lax.broadcasted_iota(I32, logits.shape, 1)
    col4 = lax.broadcasted_iota(I32, (tm, TOP_K), 1)
    lg = logits
    vals, idxs = [], []
    onehot = jnp.zeros(logits.shape, F32)
    for _ in range(TOP_K):
        m = jnp.max(lg, axis=-1, keepdims=True)
        ix = jnp.min(jnp.where(lg == m, eidx, N_EXPERTS), axis=-1, keepdims=True)
        sel = eidx == ix
        vals.append(m)
        idxs.append(ix)
        onehot = onehot + sel.astype(F32)
        lg = jnp.where(sel, -jnp.inf, lg)
    es = [jnp.exp(v - vals[0]) for v in vals]
    den = es[0] + es[1] + es[2] + es[3]

    tri = (lax.broadcasted_iota(I32, (tm, tm), 0) >
           lax.broadcasted_iota(I32, (tm, tm), 1)).astype(BF16)
    prefix = _dot(tri, onehot.astype(BF16)) + carry_ref[...]
    idx4 = jnp.zeros((tm, TOP_K), I32)
    w4 = jnp.zeros((tm, TOP_K), F32)
    rank4 = jnp.zeros((tm, TOP_K), I32)
    for kk in range(TOP_K):
        rk = jnp.sum(jnp.where(eidx == idxs[kk], prefix, 0.0), axis=-1, keepdims=True)
        idx4 = jnp.where(col4 == kk, idxs[kk], idx4)
        w4 = jnp.where(col4 == kk, es[kk] / den, w4)
        rank4 = jnp.where(col4 == kk, rk.astype(I32), rank4)
    idx_out[...] = idx4
    w_out[...] = w4
    rank_out[...] = rank4
    carry_ref[...] = carry_ref[...] + jnp.sum(onehot, axis=0, keepdims=True)
    cnt_out[...] = carry_ref[...].astype(I32)


def _outproj_call(y_rwkv, y_diff, x, mod6, w_out, norm2_gain, router_w, router_b, S, tm):
    T = x.shape[0]
    tiles_per_seq = S // tm
    rw_hi = router_w.astype(BF16)
    rw_lo = (router_w - rw_hi.astype(F32)).astype(BF16)
    tok = lambda w: pl.BlockSpec((tm, w), lambda t: (t, 0))
    full = lambda shape: pl.BlockSpec(shape, lambda t: (0,) * len(shape))
    return pl.pallas_call(
        functools.partial(_outproj_kernel, tm=tm),
        grid=(T // tm,),
        in_specs=[tok(RWKV_WIDTH), tok(DIFF_WIDTH), tok(D_MODEL),
                  pl.BlockSpec((1, 6, D_MODEL), lambda t: (t // tiles_per_seq, 0, 0)),
                  full((RWKV_WIDTH, D_MODEL)), full((DIFF_WIDTH, D_MODEL)), full((1, D_MODEL)),
                  full((D_MODEL, N_EXPERTS)), full((D_MODEL, N_EXPERTS)), full((1, N_EXPERTS))],
        out_specs=[tok(D_MODEL), pl.BlockSpec((tm, ROW_SUB, 128), lambda t: (t, 0, 0)),
                   tok(TOP_K), tok(TOP_K), tok(TOP_K), full((1, N_EXPERTS))],
        out_shape=[jax.ShapeDtypeStruct((T, D_MODEL), F32),
                   jax.ShapeDtypeStruct((T, ROW_SUB, 128), U32),
                   jax.ShapeDtypeStruct((T, TOP_K), I32),
                   jax.ShapeDtypeStruct((T, TOP_K), F32),
                   jax.ShapeDtypeStruct((T, TOP_K), I32),
                   jax.ShapeDtypeStruct((1, N_EXPERTS), I32)],
        scratch_shapes=[pltpu.VMEM((1, N_EXPERTS), F32)],
        compiler_params=pltpu.CompilerParams(
            dimension_semantics=("arbitrary",), vmem_limit_bytes=VMEM_LIMIT),
        name="outproj",
    )(y_rwkv, y_diff, x, mod6, w_out[:RWKV_WIDTH].astype(BF16), w_out[RWKV_WIDTH:].astype(BF16),
      norm2_gain.reshape(1, -1), rw_hi, rw_lo, router_b.reshape(1, -1))


def _dispatch_kernel(last_ref, cnt_ref, nu_ref, slots_hbm, hp_ref, xs_hbm, slot_smem, own_ref,
                     zero_ref, ssem, rsem, zsem, *, tm, n_blocks, n_tiles):
    t = pl.program_id(0)

    @pl.when(t == 0)
    def _():
        zero_ref[...] = jnp.zeros_like(zero_ref)

        def zero_copy(row):
            return pltpu.make_async_copy(zero_ref, xs_hbm.at[pl.ds(row, MOE_BLOCK)], zsem)

        def start_block(bk, carry):
            zero_copy(bk * MOE_BLOCK).start()
            return carry

        def wait_block(bk, carry):
            zero_copy(0).wait()
            return carry

        for e in range(N_EXPERTS):
            @pl.when(cnt_ref[e] > 0)
            def _():
                zero_copy(last_ref[e]).start()
        lax.fori_loop(nu_ref[0], n_blocks, start_block, 0)
        for e in range(N_EXPERTS):
            @pl.when(cnt_ref[e] > 0)
            def _():
                zero_copy(0).wait()
        lax.fori_loop(nu_ref[0], n_blocks, wait_block, 0)

    cur = t % 2
    n_idx = tm * TOP_K

    def slot_copy(tile, buf):
        return pltpu.make_async_copy(slots_hbm.at[pl.ds(tile * n_idx, n_idx)], slot_smem.at[buf],
                                     ssem.at[buf])

    def wait_rows(buf):
        for _ in range(TOP_K):
            pltpu.make_async_copy(own_ref.at[buf], xs_hbm.at[pl.ds(0, tm)], rsem.at[buf]).wait()

    @pl.when(t == 0)
    def _():
        slot_copy(0, 0).start()

    @pl.when(t + 1 < n_tiles)
    def _():
        slot_copy(t + 1, 1 - cur).start()

    @pl.when(t >= 2)
    def _():
        wait_rows(cur)

    own_ref[cur] = hp_ref[...]
    slot_copy(t, cur).wait()

    def body(i, carry):
        for kk in range(TOP_K):
            slot = slot_smem[cur, i * TOP_K + kk]
            pltpu.make_async_copy(own_ref.at[cur, i], xs_hbm.at[slot], rsem.at[cur]).start(
                priority=kk % 2)
        return carry

    lax.fori_loop(0, tm, body, 0)

    @pl.when(t == n_tiles - 1)
    def _():
        wait_rows(cur)
        if n_tiles >= 2:
            wait_rows(1 - cur)


def _dispatch_call(hp, slots_flat, last_block_row, counts, n_used, n_blocks, tm):
    T = hp.shape[0]
    n_rows = n_blocks * MOE_BLOCK
    grid_spec = pltpu.PrefetchScalarGridSpec(
        num_scalar_prefetch=3,
        grid=(T // tm,),
        in_specs=[pl.BlockSpec(memory_space=pl.ANY),
                  pl.BlockSpec((tm, ROW_SUB, 128), lambda t, last, cnt, nu: (t, 0, 0))],
        out_specs=pl.BlockSpec(memory_space=pl.ANY),
        scratch_shapes=[pltpu.SMEM((2, tm * TOP_K), I32),
                        pltpu.VMEM((2, tm, ROW_SUB, 128), U32),
                        pltpu.VMEM((MOE_BLOCK, ROW_SUB, 128), U32),
                        pltpu.SemaphoreType.DMA((2,)),
                        pltpu.SemaphoreType.DMA((2,)),
                        pltpu.SemaphoreType.DMA(())])
    return pl.pallas_call(
        functools.partial(_dispatch_kernel, tm=tm, n_blocks=n_blocks, n_tiles=T // tm),
        grid_spec=grid_spec,
        out_shape=jax.ShapeDtypeStruct((n_rows, ROW_SUB, 128), U32),
        compiler_params=pltpu.CompilerParams(
            dimension_semantics=("arbitrary",), vmem_limit_bytes=VMEM_LIMIT),
        name="dispatch",
    )(last_block_row, counts, n_used, slots_flat, hp)


def _experts_kernel(be_ref, nu_ref, xs_ref, wgu_ref, bgu_ref, wd_ref, bd_ref, ys_ref,
                    wgu_bf, wd_bf):
    i = pl.program_id(0)

    @pl.when((i == 0) | (be_ref[i] != be_ref[jnp.maximum(i - 1, 0)]))
    def _():
        wgu_bf[...] = wgu_ref[0].astype(BF16)
        wd_bf[...] = wd_ref[0].astype(BF16)

    @pl.when(i < nu_ref[0])
    def _():
        xa, xb = _unpack_bf16_pairs(_load_rows(xs_ref))
        x = jnp.concatenate([xa.astype(BF16), xb.astype(BF16)], axis=1)
        n_sl = D_EXPERT // EXPERT_SLICE

        def gate_up(c):
            lo = c * EXPERT_SLICE
            g = _dot(x, wgu_bf[:, lo:lo + EXPERT_SLICE]) + bgu_ref[0, :, lo:lo + EXPERT_SLICE]
            u = (_dot(x, wgu_bf[:, D_EXPERT + lo:D_EXPERT + lo + EXPERT_SLICE])
                 + bgu_ref[0, :, D_EXPERT + lo:D_EXPERT + lo + EXPERT_SLICE])
            return g, u

        y = bd_ref[0] + jnp.zeros((MOE_BLOCK, D_MODEL), F32)
        nxt = gate_up(0)
        for c in range(n_sl):
            g, u = nxt
            if c + 1 < n_sl:
                nxt = gate_up(c + 1)
            gate = jnp.minimum(g, SWIGLU_LIMIT)
            up = jnp.clip(u, -SWIGLU_LIMIT, SWIGLU_LIMIT)
            act = (up + 1.0) * (gate * jax.nn.sigmoid(SWIGLU_ALPHA * gate))
            y = y + _dot(act.astype(BF16), wd_bf[c * EXPERT_SLICE:(c + 1) * EXPERT_SLICE, :])
        _store_rows(ys_ref, _pack_bf16_pairs(y))

    @pl.when(i >= nu_ref[0])
    def _():
        ys_ref[...] = jnp.zeros_like(ys_ref)


def _experts_call(xs, block_e, n_used, w_gate_up, b_gate_up, w_down, b_down, n_blocks):
    blk = lambda i, be, nu: jnp.minimum(i, nu[0] - 1)
    rows = pl.BlockSpec((MOE_BLOCK, ROW_SUB, 128), lambda i, be, nu: (blk(i, be, nu), 0, 0))
    grid_spec = pltpu.PrefetchScalarGridSpec(
        num_scalar_prefetch=2,
        grid=(n_blocks,),
        in_specs=[rows,
                  pl.BlockSpec((1, D_MODEL, 2 * D_EXPERT), lambda i, be, nu: (be[i], 0, 0)),
                  pl.BlockSpec((1, 1, 2 * D_EXPERT), lambda i, be, nu: (be[i], 0, 0)),
                  pl.BlockSpec((1, D_EXPERT, D_MODEL), lambda i, be, nu: (be[i], 0, 0)),
                  pl.BlockSpec((1, 1, D_MODEL), lambda i, be, nu: (be[i], 0, 0))],
        out_specs=pl.BlockSpec((MOE_BLOCK, ROW_SUB, 128), lambda i, be, nu: (i, 0, 0)),
        scratch_shapes=[pltpu.VMEM((D_MODEL, 2 * D_EXPERT), BF16),
                        pltpu.VMEM((D_EXPERT, D_MODEL), BF16)])
    return pl.pallas_call(
        _experts_kernel,
        grid_spec=grid_spec,
        out_shape=jax.ShapeDtypeStruct((n_blocks * MOE_BLOCK, ROW_SUB, 128), U32),
        compiler_params=pltpu.CompilerParams(
            dimension_semantics=("arbitrary",), vmem_limit_bytes=VMEM_LIMIT),
        name="experts",
    )(block_e, n_used, xs, w_gate_up, b_gate_up.reshape(N_EXPERTS, 1, -1), w_down,
      b_down.reshape(N_EXPERTS, 1, -1))


def _combine_kernel(slots_hbm, ys_hbm, x1_ref, w_ref, mod_ref, fg_ref, o_ref,
                    slot_smem, buf_ref, ssem, rsem, *, tm, n_tiles):
    t = pl.program_id(0)
    cur = t % 2
    n_idx = tm * TOP_K

    def slot_copy(tile, buf):
        return pltpu.make_async_copy(slots_hbm.at[pl.ds(tile * n_idx, n_idx)], slot_smem.at[buf],
                                     ssem.at[buf])

    def gather(buf):
        def body(i, carry):
            for kk in range(TOP_K):
                slot = slot_smem[buf, i * TOP_K + kk]
                pltpu.make_async_copy(ys_hbm.at[slot], buf_ref.at[buf, kk, i], rsem.at[buf]).start(
                    priority=kk % 2)
            return carry

        lax.fori_loop(0, tm, body, 0)

    @pl.when(t == 0)
    def _():
        first = slot_copy(0, 0)
        first.start()
        first.wait()
        gather(0)
        if n_tiles >= 2:
            slot_copy(1, 1).start()

    @pl.when(t + 1 < n_tiles)
    def _():
        slot_copy(t + 1, 1 - cur).wait()
        gather(1 - cur)

    @pl.when(t + 2 < n_tiles)
    def _():
        slot_copy(t + 2, cur).start()

    for kk in range(TOP_K):
        pltpu.make_async_copy(ys_hbm.at[pl.ds(0, tm)], buf_ref.at[cur, kk], rsem.at[cur]).wait()

    w = w_ref[...]
    acc_lo = jnp.zeros((tm, D_MODEL // 2), F32)
    acc_hi = jnp.zeros((tm, D_MODEL // 2), F32)
    for kk in range(TOP_K):
        lo, hi = _unpack_bf16_pairs(_load_rows(buf_ref.at[cur, kk]))
        wk = w[:, kk:kk + 1]
        acc_lo = acc_lo + wk * lo
        acc_hi = acc_hi + wk * hi
    moe = jnp.concatenate([acc_lo, acc_hi], axis=1)
    gate2 = mod_ref[0, 5:6, :]
    x2 = x1_ref[...] + gate2 * moe
    ms = jnp.mean(x2 * x2, axis=-1, keepdims=True)
    o_ref[...] = x2 * lax.rsqrt(ms + NORM_EPS) * fg_ref[...]


def _combine_call(ys, slots_flat, x1, top_w, mod6, final_gain, S, tm):
    T = x1.shape[0]
    tiles_per_seq = S // tm
    return pl.pallas_call(
        functools.partial(_combine_kernel, tm=tm, n_tiles=T // tm),
        grid=(T // tm,),
        in_specs=[pl.BlockSpec(memory_space=pl.ANY),
                  pl.BlockSpec(memory_space=pl.ANY),
                  pl.BlockSpec((tm, D_MODEL), lambda t: (t, 0)),
                  pl.BlockSpec((tm, TOP_K), lambda t: (t, 0)),
                  pl.BlockSpec((1, 6, D_MODEL), lambda t: (t // tiles_per_seq, 0, 0)),
                  pl.BlockSpec((1, D_MODEL), lambda t: (0, 0))],
        out_specs=pl.BlockSpec((tm, D_MODEL), lambda t: (t, 0)),
        out_shape=jax.ShapeDtypeStruct((T, D_MODEL), F32),
        scratch_shapes=[pltpu.SMEM((2, tm * TOP_K), I32),
                        pltpu.VMEM((2, TOP_K, tm, ROW_SUB, 128), U32),
                        pltpu.SemaphoreType.DMA((2,)),
                        pltpu.SemaphoreType.DMA((2,))],
        compiler_params=pltpu.CompilerParams(
            dimension_semantics=("arbitrary",), vmem_limit_bytes=VMEM_LIMIT),
        name="combine",
    )(slots_flat, ys, x1, top_w, mod6, final_gain.reshape(1, -1))


def _forward(x, c, mod_w, mod_b, norm1_gain, w_in, rwkv_shift_mu, rwkv_w0, rwkv_w_up, rwkv_a0,
             rwkv_a_up, rwkv_g_up, rwkv_k_k, rwkv_k_a, rwkv_r_k, rwkv_gn_gain, rwkv_gn_bias,
             diff_lambda_q1, diff_lambda_k1, diff_lambda_q2, diff_lambda_k2, diff_subln_gain,
             w_out, norm2_gain, router_w, router_b, w_gate_up, b_gate_up, w_down, b_down,
             final_gain):
    B, S, D = x.shape
    T = B * S
    tm_in = min(512, S)
    tq = min(512, S)
    tm = min(256, S)

    mod6 = _mod_call(c, mod_w, mod_b).reshape(B, 6, D)
    (r, k, v, a, b, g, lw, q, kat, vt) = _inproj_call(
        x, mod6, norm1_gain, w_in, rwkv_shift_mu, rwkv_w0, rwkv_w_up, rwkv_a0, rwkv_a_up,
        rwkv_g_up, rwkv_k_k, rwkv_k_a, tm_in)
    y_rwkv = _rwkv_call(r, k, v, a, b, g, lw, rwkv_r_k, rwkv_gn_gain, rwkv_gn_bias,
                        min(4, S // CHUNK))
    y_diff = _attn_call(q, kat, vt, diff_lambda_q1, diff_lambda_k1, diff_lambda_q2,
                        diff_lambda_k2, diff_subln_gain, tq)

    x1, hp, top_idx, top_w, rank, counts = _outproj_call(
        y_rwkv.reshape(T, -1), y_diff.reshape(T, -1), x.reshape(T, D), mod6, w_out, norm2_gain,
        router_w, router_b, S, tm)

    counts = counts.reshape(N_EXPERTS)
    padded = ((counts + MOE_BLOCK - 1) // MOE_BLOCK) * MOE_BLOCK
    pad_ends = jnp.cumsum(padded)
    pad_starts = pad_ends - padded
    n_blocks = -(-(T * TOP_K + N_EXPERTS * (MOE_BLOCK - 1)) // MOE_BLOCK)
    n_used = (pad_ends[-1] // MOE_BLOCK).astype(I32).reshape(1)
    block_start = jnp.minimum(jnp.arange(n_blocks, dtype=I32), n_used[0] - 1) * MOE_BLOCK
    block_e = jnp.minimum(jnp.sum(pad_ends[None, :] <= block_start[:, None], axis=1),
                          N_EXPERTS - 1).astype(I32)
    expert_ids = jnp.arange(N_EXPERTS, dtype=I32)
    start_of = jnp.sum(jnp.where(top_idx[..., None] == expert_ids, pad_starts.astype(I32), 0), axis=-1)
    slots = (start_of + rank).reshape(T * TOP_K)
    last_block_row = (pad_ends - MOE_BLOCK).astype(I32)

    xs = _dispatch_call(hp, slots, last_block_row, counts.astype(I32), n_used, n_blocks, tm)
    ys = _experts_call(xs, block_e, n_used, w_gate_up, b_gate_up, w_down, b_down, n_blocks)
    out = _combine_call(ys, slots, x1, top_w, mod6, final_gain, S, tm)
    return out.reshape(B, S, D)


def kernel(x, c, mod_w, mod_b, norm1_gain, w_in, rwkv_shift_mu, rwkv_w0, rwkv_w_up, rwkv_a0, rwkv_a_up, rwkv_g_up, rwkv_k_k, rwkv_k_a, rwkv_r_k, rwkv_gn_gain, rwkv_gn_bias, diff_lambda_q1, diff_lambda_k1, diff_lambda_q2, diff_lambda_k2, diff_subln_gain, w_out, norm2_gain, router_w, router_b, w_gate_up, b_gate_up, w_down, b_down, final_gain):
    return _forward(x, c, mod_w[0], mod_b[0], norm1_gain[0], w_in[0], rwkv_shift_mu[0], rwkv_w0[0],
                    rwkv_w_up[0], rwkv_a0[0], rwkv_a_up[0], rwkv_g_up[0], rwkv_k_k[0], rwkv_k_a[0],
                    rwkv_r_k[0].reshape(-1), rwkv_gn_gain[0], rwkv_gn_bias[0], diff_lambda_q1[0],
                    diff_lambda_k1[0], diff_lambda_q2[0], diff_lambda_k2[0], diff_subln_gain[0],
                    w_out[0], norm2_gain[0], router_w[0], router_b[0], w_gate_up[0], b_gate_up[0],
                    w_down[0], b_down[0], final_gain)
```

```python
import functools
import math

import jax
import jax.numpy as jnp
from jax import lax
from jax.experimental import pallas as pl
from jax.experimental.pallas import tpu as pltpu
from jax.experimental.pallas import tpu_sc as plsc

F32 = jnp.float32
BF16 = jnp.bfloat16
I32 = jnp.int32
U32 = jnp.uint32

D_MODEL = 1024
RWKV_WIDTH = 512
RWKV_HEADS = 8
HEAD_DIM = 64
RWKV_COLS = 3 * RWKV_WIDTH + 64 + 64 + 128
DIFF_WIDTH = 512
DIFF_HEADS = 4
DIFF_COLS = 3 * DIFF_WIDTH
N_EXPERTS = 32
TOP_K = 4
D_EXPERT = 1024
SWIGLU_LIMIT = 7.0
SWIGLU_ALPHA = 1.702
MOE_BLOCK = 512
NORM_EPS = 1e-5
GN_EPS = 64e-5
LAMBDA_INIT = 0.8 - 0.6 * math.exp(-0.3 * 0)

CHUNK = 64
GROUP = 4
GW = GROUP * HEAD_DIM
VMEM_LIMIT = 56 * 1024 * 1024


def _dot(a, b):
    return jnp.dot(a, b, preferred_element_type=F32)


def _dot_nt(a, b):
    return lax.dot_general(a, b, (((1,), (1,)), ((), ())), preferred_element_type=F32)


def _dot_tn(a, b):
    return lax.dot_general(a, b, (((0,), (0,)), ((), ())), preferred_element_type=F32)


def _split_dot(x, w_bf16):
    hi = x.astype(BF16)
    lo = (x - hi.astype(F32)).astype(BF16)
    return _dot(hi, w_bf16) + _dot(lo, w_bf16)


def _mod_kernel(c_ref, w_ref, b_ref, o_ref):
    c = c_ref[...]
    s = c * jax.nn.sigmoid(c)
    o_ref[...] = _dot(s, w_ref[...]) + b_ref[...]


def _mod_call(c, mod_w, mod_b):
    B = c.shape[0]
    n = mod_w.shape[1]
    tn = 1536
    return pl.pallas_call(
        _mod_kernel,
        grid=(n // tn,),
        in_specs=[pl.BlockSpec((B, D_MODEL), lambda j: (0, 0)),
                  pl.BlockSpec((D_MODEL, tn), lambda j: (0, j)),
                  pl.BlockSpec((1, tn), lambda j: (0, j))],
        out_specs=pl.BlockSpec((B, tn), lambda j: (0, j)),
        out_shape=jax.ShapeDtypeStruct((B, n), F32),
        compiler_params=pltpu.CompilerParams(
            dimension_semantics=("arbitrary",), vmem_limit_bytes=VMEM_LIMIT),
        name="mod",
    )(c, mod_w, mod_b.reshape(1, n))


def _inproj_kernel(x_ref, mod_ref, g1_ref, wrw_ref, wat_ref, mu_ref, w0_ref, wup_ref,
                   a0_ref, aup_ref, gup_ref, kk_ref, ka_ref, hsum_ref,
                   r_out, k_out, v_out, a_out, b_out, g_out, lw_out,
                   q_out, kat_out, vt_out, carry_ref, *, tm):
    s = pl.program_id(1)
    x = x_ref[0]
    ms = jnp.mean(x * x, axis=-1, keepdims=True)
    shift1 = mod_ref[0, 0:1, :]
    scale1 = mod_ref[0, 1:2, :]
    h = x * lax.rsqrt(ms + NORM_EPS) * g1_ref[...] * (1.0 + scale1) + shift1
    hb = h.astype(BF16)

    p = _dot(hb, wrw_ref[...])
    pa = _dot(hb, wat_ref[...])

    @pl.when(s == 0)
    def _():
        carry_ref[...] = jnp.zeros_like(carry_ref)

    rolled = pltpu.roll(p, shift=1, axis=0)
    row = lax.broadcasted_iota(I32, p.shape, 0)
    prev = jnp.where(row == 0, carry_ref[...], rolled)
    carry_ref[...] = p[tm - 1:tm, :]
    ps = p + mu_ref[...] * (prev - p)

    r = ps[:, 0:512]
    k = ps[:, 512:1024]
    v = ps[:, 1024:1536]
    lo2 = ps[:, 1536:1664]
    g_lo = ps[:, 1664:1792]
    z = w0_ref[...] + _dot(jnp.tanh(lo2).astype(BF16), wup_ref[...])
    nz = -z
    softplus = jnp.maximum(nz, 0.0) + jnp.log(1.0 + jnp.exp(-jnp.abs(nz)))
    w = -softplus - 0.5
    lw_out[0] = -jnp.exp(w)
    a = jax.nn.sigmoid(a0_ref[...] + _dot(lo2.astype(BF16), aup_ref[...]))
    g = _dot(jax.nn.sigmoid(g_lo).astype(BF16), gup_ref[...])
    kk = k * kk_ref[...]
    ssq = _split_dot(kk * kk, hsum_ref[...])
    kk = kk / jnp.maximum(jnp.sqrt(ssq), 1e-12)
    k = k * (1.0 + (a - 1.0) * ka_ref[...])
    r_out[0] = r.astype(BF16)
    k_out[0] = k.astype(BF16)
    v_out[0] = v.astype(BF16)
    a_out[0] = (-kk).astype(BF16)
    b_out[0] = (kk * a).astype(BF16)
    g_out[0] = g.astype(BF16)

    q_out[0] = (pa[:, 0:512] * (1.0 / math.sqrt(HEAD_DIM))).astype(BF16)
    kat_out[0] = pa[:, 512:1024].astype(BF16)
    for hh in range(DIFF_HEADS):
        vh = pa[:, 1024 + hh * 128:1024 + (hh + 1) * 128]
        vt_out[0, hh, 0] = vh.T.astype(BF16)


def _head_sum_matrix(width):
    i = jnp.arange(width) // HEAD_DIM
    return (i[:, None] == i[None, :]).astype(BF16)


def _inproj_call(x, mod6, norm1_gain, w_in, mu, w0, w_up, a0, a_up, g_up, k_k, k_a, tm):
    B, S, _ = x.shape
    ns = S // tm
    w_rw = w_in[:, :RWKV_COLS].astype(BF16)
    w_at = w_in[:, RWKV_COLS:].astype(BF16)
    zeros = jnp.zeros((64, RWKV_WIDTH), F32)
    wup_p = jnp.concatenate([w_up, zeros], axis=0).astype(BF16)
    aup_p = jnp.concatenate([zeros, a_up], axis=0).astype(BF16)
    row = lambda v: v.reshape(1, -1)
    full = lambda shape: pl.BlockSpec(shape, lambda b, s: (0,) * len(shape))
    tok = lambda w: pl.BlockSpec((1, tm, w), lambda b, s: (b, s, 0))
    rw_shape = jax.ShapeDtypeStruct((B, S, RWKV_WIDTH), BF16)
    out_shape = [rw_shape] * 6 + [
        jax.ShapeDtypeStruct((B, S, RWKV_WIDTH), F32),
        jax.ShapeDtypeStruct((B, S, DIFF_WIDTH), BF16),
        jax.ShapeDtypeStruct((B, S, DIFF_WIDTH), BF16),
        jax.ShapeDtypeStruct((B, DIFF_HEADS, ns, 128, tm), BF16)]
    out_specs = [tok(RWKV_WIDTH)] * 7 + [tok(DIFF_WIDTH)] * 2 + [
        pl.BlockSpec((1, DIFF_HEADS, 1, 128, tm), lambda b, s: (b, 0, s, 0, 0))]
    return pl.pallas_call(
        functools.partial(_inproj_kernel, tm=tm),
        grid=(B, ns),
        in_specs=[tok(D_MODEL),
                  pl.BlockSpec((1, 6, D_MODEL), lambda b, s: (b, 0, 0)),
                  full((1, D_MODEL)),
                  full((D_MODEL, RWKV_COLS)), full((D_MODEL, DIFF_COLS)),
                  full((1, RWKV_COLS)), full((1, RWKV_WIDTH)), full((128, RWKV_WIDTH)),
                  full((1, RWKV_WIDTH)), full((128, RWKV_WIDTH)), full((128, RWKV_WIDTH)),
                  full((1, RWKV_WIDTH)), full((1, RWKV_WIDTH)),
                  full((RWKV_WIDTH, RWKV_WIDTH))],
        out_specs=out_specs,
        out_shape=out_shape,
        scratch_shapes=[pltpu.VMEM((1, RWKV_COLS), F32)],
        compiler_params=pltpu.CompilerParams(
            dimension_semantics=("arbitrary", "arbitrary"), vmem_limit_bytes=VMEM_LIMIT),
        name="inproj",
    )(x, mod6, row(norm1_gain), w_rw, w_at, row(mu), row(w0), wup_p, row(a0), aup_p,
      g_up.astype(BF16), row(k_k), row(k_a), _head_sum_matrix(RWKV_WIDTH))


def _rwkv_kernel(r_ref, k_ref, v_ref, a_ref, b_ref, g_ref, lw_ref, rk_ref, gng_ref, gnb_ref,
                 hsum_ref, tri_ref, y_ref, s0_ref, s1_ref, *, nck):
    C = CHUNK
    n_grp = RWKV_HEADS // GROUP
    s_refs = (s0_ref, s1_ref)

    @pl.when(pl.program_id(1) == 0)
    def _():
        s0_ref[...] = jnp.zeros_like(s0_ref)
        s1_ref[...] = jnp.zeros_like(s1_ref)

    ti = lax.broadcasted_iota(I32, (C, GW), 0)
    si = lax.broadcasted_iota(I32, (C, GW), 1) % C
    incl = si <= ti
    strict = si < ti
    eye = (si == ti).astype(F32)
    bdmask = (lax.broadcasted_iota(I32, (GW, GW), 0) // HEAD_DIM ==
              lax.broadcasted_iota(I32, (GW, GW), 1) // HEAD_DIM)
    hsum = hsum_ref[...]
    chains = [(ck, g) for ck in range(nck) for g in range(n_grp)]

    def bd(xb):
        return jnp.where(bdmask, jnp.concatenate([xb] * GROUP, axis=0), jnp.zeros((), xb.dtype))

    def cut(x, ch):
        ck, g = ch
        return x[ck * C:(ck + 1) * C, g * GW:(g + 1) * GW]

    def rows_of_chunk(x, row):
        return jnp.concatenate(
            [jnp.broadcast_to(x[ck * C + row:ck * C + row + 1, :], (C, x.shape[1]))
             for ck in range(nck)], axis=0)

    lw = lw_ref[0]
    L = _split_dot_left(tri_ref[...], lw)
    Lx = L - lw
    Lc = rows_of_chunk(L, C - 1)
    rho = rows_of_chunk(L, C // 2 - 1)
    r = r_ref[0].astype(F32)
    k = k_ref[0].astype(F32)
    vb = v_ref[0]
    a = a_ref[0].astype(F32)
    b = b_ref[0].astype(F32)
    e_k = jnp.exp(rho - L)
    rt = (r * jnp.exp(L - rho)).astype(BF16)
    at = (a * jnp.exp(Lx - rho)).astype(BF16)
    kt = (k * e_k).astype(BF16)
    bt = (b * e_k).astype(BF16)
    r0 = (r * jnp.exp(L)).astype(BF16)
    a0 = (a * jnp.exp(Lx)).astype(BF16)
    e_o = jnp.exp(Lc - L)
    kh = (k * e_o).astype(BF16)
    bh = (b * e_o).astype(BF16)
    gc = jnp.exp(Lc)

    lhs = {ch: jnp.concatenate([cut(rt, ch), cut(at, ch)], axis=0) for ch in chains}
    ak = {ch: _dot_nt(lhs[ch], bd(cut(kt, ch))) for ch in chains}
    ab = {ch: _dot_nt(lhs[ch], bd(cut(bt, ch))) for ch in chains}
    a_rk = {ch: jnp.where(incl, ak[ch][:C], 0.0).astype(BF16) for ch in chains}
    a_ak = {ch: jnp.where(strict, ak[ch][C:], 0.0).astype(BF16) for ch in chains}
    a_rb = {ch: jnp.where(incl, ab[ch][:C], 0.0).astype(BF16) for ch in chains}
    n = {ch: jnp.where(strict, ab[ch][C:], 0.0) for ch in chains}

    p = {ch: eye + n[ch] for ch in chains}
    xb = {ch: n[ch].astype(BF16) for ch in chains}
    x = {ch: _dot(xb[ch], bd(xb[ch])) for ch in chains}
    for _ in range(4):
        xb = {ch: x[ch].astype(BF16) for ch in chains}
        o = {ch: _dot(jnp.concatenate([xb[ch], p[ch].astype(BF16)], axis=0), bd(xb[ch]))
             for ch in chains}
        x = {ch: o[ch][:C] for ch in chains}
        p = {ch: p[ch] + o[ch][C:] for ch in chains}
    o = {ch: _dot(p[ch].astype(BF16), bd(x[ch].astype(BF16))) for ch in chains}
    tb = {ch: (p[ch] + o[ch]).astype(BF16) for ch in chains}

    av = {ch: _dot(jnp.concatenate([a_ak[ch], a_rk[ch]], axis=0), bd(cut(vb, ch))) for ch in chains}
    w1 = {ch: _dot(tb[ch], bd(cut(a0, ch))).astype(BF16) for ch in chains}
    w2 = {ch: _dot(tb[ch], bd(av[ch][:C].astype(BF16))) for ch in chains}

    sv = [s_refs[g][...] for g in range(n_grp)]
    ys = {}
    for ck in range(nck):
        grp = [(ck, g) for g in range(n_grp)]
        rws = [_dot_nt(jnp.concatenate([cut(r0, ch), w1[ch]], axis=0), sv[ch[1]].astype(BF16))
               for ch in grp]
        ub = [(rws[g][C:] + w2[(ck, g)]).astype(BF16) for g in range(n_grp)]
        upd = [_dot_tn(jnp.concatenate([ub[g], cut(vb, (ck, g))], axis=0),
                       jnp.concatenate([cut(bh, (ck, g)), cut(kh, (ck, g))], axis=0))
               for g in range(n_grp)]
        for g in range(n_grp):
            ch = (ck, g)
            ys[ch] = rws[g][:C] + _dot(a_rb[ch], bd(ub[g])) + av[ch][C:]
        sv = [sv[g] * cut(gc, (ck, g))[0:1, :] + jnp.where(bdmask, upd[g], 0.0)
              for g in range(n_grp)]
    for g in range(n_grp):
        s_refs[g][...] = sv[g]

    for g in range(n_grp):
        sl = slice(g * GW, (g + 1) * GW)
        y = jnp.concatenate([ys[(ck, g)] for ck in range(nck)], axis=0)
        mean = _split_dot(y, hsum) * (1.0 / HEAD_DIM)
        d = y - mean
        var = _split_dot(d * d, hsum) * (1.0 / HEAD_DIM)
        yn = d * lax.rsqrt(var + GN_EPS) * gng_ref[:, sl] + gnb_ref[:, sl]
        bonus = _split_dot(r[:, sl] * k[:, sl] * rk_ref[:, sl], hsum) * vb[:, sl].astype(F32)
        y_ref[0, :, sl] = ((yn + bonus) * g_ref[0, :, sl].astype(F32)).astype(BF16)


def _split_dot_left(w_bf16, x):
    hi = x.astype(BF16)
    lo = (x - hi.astype(F32)).astype(BF16)
    return _dot(w_bf16, hi) + _dot(w_bf16, lo)


def _rwkv_call(r, k, v, a, b, g, lw, r_k, gn_gain, gn_bias, nck):
    B, S, _ = r.shape
    tb = nck * CHUNK
    idx = jnp.arange(tb)
    tri = ((idx[:, None] // CHUNK == idx[None, :] // CHUNK) &
           (idx[:, None] >= idx[None, :])).astype(BF16)
    tok = pl.BlockSpec((1, tb, RWKV_WIDTH), lambda bb, c: (bb, c, 0))
    par = pl.BlockSpec((1, RWKV_WIDTH), lambda bb, c: (0, 0))
    return pl.pallas_call(
        functools.partial(_rwkv_kernel, nck=nck),
        grid=(B, S // tb),
        in_specs=[tok] * 7 + [par] * 3 + [pl.BlockSpec((GW, GW), lambda bb, c: (0, 0)),
                                          pl.BlockSpec((tb, tb), lambda bb, c: (0, 0))],
        out_specs=tok,
        out_shape=jax.ShapeDtypeStruct((B, S, RWKV_WIDTH), BF16),
        scratch_shapes=[pltpu.VMEM((GW, GW), F32)] * (RWKV_HEADS // GROUP),
        compiler_params=pltpu.CompilerParams(
            dimension_semantics=("arbitrary", "arbitrary"), vmem_limit_bytes=VMEM_LIMIT),
        name="rwkv",
    )(r, k, v, a, b, g, lw, r_k.reshape(1, -1), gn_gain.reshape(1, -1), gn_bias.reshape(1, -1),
      _head_sum_matrix(GW), tri)


def _tree_reduce(op, x):
    while x.shape[0] > 8:
        h = x.shape[0] // 2
        x = op(x[:h], x[h:])
    return x


def _attn_kernel(slopes_ref, q_ref, k_ref, vt_ref, lq1_ref, lk1_ref, lq2_ref, lk2_ref,
                 gain_ref, o_ref, sa_ref, sb_ref, acc_ref, m_ref, l_ref, *, tq, tk):
    hh = pl.program_id(1)
    i = pl.program_id(2)
    slope = slopes_ref[hh]

    q = q_ref[0].astype(F32)
    lane = lax.broadcasted_iota(I32, (tq, 256), 1)
    qrow = lax.broadcasted_iota(I32, (tq, 256), 0)
    qa = (qrow >> 5).astype(F32) * (32.0 * slope)
    qb = (qrow & 31).astype(F32) * slope
    qbias = jnp.where(lane < 130, 1.0, jnp.where(lane == 130, -qa, jnp.where(lane == 131, -qb, 0.0)))
    qpad = jnp.concatenate([q, jnp.zeros_like(q)], axis=1)
    q_aug = []
    for c in range(2):
        in_comp = (lane >= c * HEAD_DIM) & (lane < (c + 1) * HEAD_DIM)
        q_aug.append(jnp.where(in_comp, qpad, jnp.where(lane >= 128, qbias, 0.0)))
    klane = lax.broadcasted_iota(I32, (tk, 128), 1)
    krow = lax.broadcasted_iota(I32, (tk, 128), 0)
    ka = (krow >> 5).astype(F32) * (32.0 * slope)
    kb_ = (krow & 31).astype(F32) * slope
    kbias = jnp.where(klane == 0, ka, jnp.where(klane == 1, kb_, jnp.where(klane < 4, 1.0, 0.0))
                      ).astype(BF16)

    qt_both = jnp.concatenate([q_aug[0].T, q_aug[1].T], axis=1).astype(BF16)

    m_ref[...] = jnp.full_like(m_ref, -jnp.inf)
    l_ref[...] = jnp.zeros_like(l_ref)
    acc_ref[...] = jnp.zeros_like(acc_ref)

    def scores(j):
        kb = k_ref[0, pl.ds(pl.multiple_of(j * tk, tk), tk), :]
        return _dot(jnp.concatenate([kb, kbias], axis=1), qt_both)

    def absorb(s_ref, j, masked):
        s = s_ref[...]
        off = j * tk - i * tq
        cj = slope * off.astype(F32)
        if masked:
            keep = (lax.broadcasted_iota(I32, (tk, tq), 0) -
                    lax.broadcasted_iota(I32, (tk, tq), 1) + off) <= 0
            s = jnp.where(jnp.concatenate([keep, keep], axis=1), s, -jnp.inf)
        m_loc = jnp.max(_tree_reduce(jnp.maximum, s), axis=0, keepdims=True)
        m_old = m_ref[...]
        m_new = jnp.maximum(m_old, m_loc + cj)
        pr = jnp.exp(s - (m_new - cj))
        alpha = jnp.exp(m_old - m_new)
        l_ref[...] = alpha * l_ref[...] + jnp.sum(_tree_reduce(jnp.add, pr), axis=0, keepdims=True)
        acc_ref[...] = alpha * acc_ref[...] + _dot(vt_ref[0, 0, j], pr.astype(BF16))
        m_ref[...] = m_new

    n_full = (i * tq) // tk
    sa_ref[...] = scores(0)

    def pair_body(jj, carry):
        j0 = 2 * jj
        sb_ref[...] = scores(j0 + 1)
        absorb(sa_ref, j0, False)
        sa_ref[...] = scores(j0 + 2)
        absorb(sb_ref, j0 + 1, False)
        return carry

    lax.fori_loop(0, n_full // 2, pair_body, 0)

    @pl.when(n_full % 2 == 1)
    def _():
        sb_ref[...] = scores(n_full)
        absorb(sa_ref, n_full - 1, False)
        absorb(sb_ref, n_full, True)

    @pl.when(n_full % 2 == 0)
    def _():
        absorb(sa_ref, n_full, True)

    lam = (jnp.exp(jnp.sum(lq1_ref[...] * lk1_ref[...], axis=-1, keepdims=True))
           - jnp.exp(jnp.sum(lq2_ref[...] * lk2_ref[...], axis=-1, keepdims=True))
           + LAMBDA_INIT)
    o2 = acc_ref[...] / l_ref[...]
    o = o2[:, :tq] - lam * o2[:, tq:]
    ot = o.T
    ms = jnp.mean(ot * ot, axis=-1, keepdims=True)
    y = ot * lax.rsqrt(ms + NORM_EPS) * gain_ref[...] * (1.0 - LAMBDA_INIT)
    o_ref[0] = y.astype(BF16)


def _attn_call(q, k, vt, lq1, lk1, lq2, lk2, subln_gain, tq):
    B, S, _ = q.shape
    ns, tk = vt.shape[2], vt.shape[4]
    slopes = jnp.asarray([2.0 ** (-8.0 * (i + 1) / DIFF_HEADS) for i in range(DIFF_HEADS)], F32)
    vec = lambda n: pl.BlockSpec((1, n), lambda b, h, i, sl: (0, 0))
    grid_spec = pltpu.PrefetchScalarGridSpec(
        num_scalar_prefetch=1,
        grid=(B, DIFF_HEADS, S // tq),
        in_specs=[pl.BlockSpec((1, tq, 128), lambda b, h, i, sl: (b, i, h)),
                  pl.BlockSpec((1, S, 128), lambda b, h, i, sl: (b, 0, h)),
                  pl.BlockSpec((1, 1, ns, 128, tk), lambda b, h, i, sl: (b, h, 0, 0, 0)),
                  vec(HEAD_DIM), vec(HEAD_DIM), vec(HEAD_DIM), vec(HEAD_DIM), vec(128)],
        out_specs=pl.BlockSpec((1, tq, 128), lambda b, h, i, sl: (b, i, h)),
        scratch_shapes=[pltpu.VMEM((tk, 2 * tq), F32), pltpu.VMEM((tk, 2 * tq), F32),
                        pltpu.VMEM((128, 2 * tq), F32), pltpu.VMEM((1, 2 * tq), F32),
                        pltpu.VMEM((1, 2 * tq), F32)])
    return pl.pallas_call(
        functools.partial(_attn_kernel, tq=tq, tk=tk),
        grid_spec=grid_spec,
        out_shape=jax.ShapeDtypeStruct((B, S, DIFF_WIDTH), BF16),
        compiler_params=pltpu.CompilerParams(
            dimension_semantics=("arbitrary", "arbitrary", "arbitrary"),
            vmem_limit_bytes=VMEM_LIMIT),
        name="attn",
    )(slopes, q, k, vt, lq1.reshape(1, -1), lk1.reshape(1, -1), lq2.reshape(1, -1),
      lk2.reshape(1, -1), subln_gain.reshape(1, -1))


def _pack_bf16_pairs(x):
    w = x.shape[1] // 2
    lo = pltpu.bitcast(x[:, :w].astype(BF16).astype(F32), U32)
    hi = pltpu.bitcast(x[:, w:].astype(BF16).astype(F32), U32)
    return (lo >> 16) | (hi & jnp.uint32(0xFFFF0000))


def _unpack_bf16_pairs(p):
    lo = pltpu.bitcast(p << 16, F32)
    hi = pltpu.bitcast(p & jnp.uint32(0xFFFF0000), F32)
    return lo, hi


ROW_SUB = (D_MODEL // 2) // 128


def _store_rows(ref, x2d):
    for s in range(ROW_SUB):
        ref[:, s, :] = x2d[:, s * 128:(s + 1) * 128]


def _load_rows(ref):
    return jnp.concatenate([ref[:, s, :] for s in range(ROW_SUB)], axis=1)


def _outproj_kernel(yr_ref, yd_ref, x_ref, mod_ref, wo_r_ref, wo_d_ref, g2_ref, rw_hi_ref,
                    rw_lo_ref, rb_ref, x1_out, hp_out, idx_out, w_out, rank_out, cnt_out,
                    carry_ref, *, tm):
    t = pl.program_id(0)

    @pl.when(t == 0)
    def _():
        carry_ref[...] = jnp.zeros_like(carry_ref)

    mix = _dot(yr_ref[...], wo_r_ref[...]) + _dot(yd_ref[...], wo_d_ref[...])
    gate1 = mod_ref[0, 2:3, :]
    shift2 = mod_ref[0, 3:4, :]
    scale2 = mod_ref[0, 4:5, :]
    x1 = x_ref[...] + gate1 * mix
    x1_out[...] = x1
    ms = jnp.mean(x1 * x1, axis=-1, keepdims=True)
    h = x1 * lax.rsqrt(ms + NORM_EPS) * g2_ref[...] * (1.0 + scale2) + shift2
    _store_rows(hp_out, _pack_bf16_pairs(h))

    hi = h.astype(BF16)
    lo = (h - hi.astype(F32)).astype(BF16)
    logits = (_dot(hi, rw_hi_ref[...]) + _dot(hi, rw_lo_ref[...]) + _dot(lo, rw_hi_ref[...])
              + rb_ref[...])

    eidx = lax.broadcasted_iota(I32, logits.shape, 1)
    col4 = lax.broadcasted_iota(I32, (tm, TOP_K), 1)
    lg = logits
    vals, idxs = [], []
    onehot = jnp.zeros(logits.shape, F32)
    for _ in range(TOP_K):
        m = jnp.max(lg, axis=-1, keepdims=True)
        ix = jnp.min(jnp.where(lg == m, eidx, N_EXPERTS), axis=-1, keepdims=True)
        sel = eidx == ix
        vals.append(m)
        idxs.append(ix)
        onehot = onehot + sel.astype(F32)
        lg = jnp.where(sel, -jnp.inf, lg)
    es = [jnp.exp(v - vals[0]) for v in vals]
    den = es[0] + es[1] + es[2] + es[3]

    tri = (lax.broadcasted_iota(I32, (tm, tm), 0) >
           lax.broadcasted_iota(I32, (tm, tm), 1)).astype(BF16)
    prefix = _dot(tri, onehot.astype(BF16)) + carry_ref[...]
    idx4 = jnp.zeros((tm, TOP_K), I32)
    w4 = jnp.zeros((tm, TOP_K), F32)
    rank4 = jnp.zeros((tm, TOP_K), I32)
    for kk in range(TOP_K):
        rk = jnp.sum(jnp.where(eidx == idxs[kk], prefix, 0.0), axis=-1, keepdims=True)
        idx4 = jnp.where(col4 == kk, idxs[kk], idx4)
        w4 = jnp.where(col4 == kk, es[kk] / den, w4)
        rank4 = jnp.where(col4 == kk, rk.astype(I32), rank4)
    idx_out[...] = idx4
    w_out[...] = w4
    rank_out[...] = rank4
    carry_ref[...] = carry_ref[...] + jnp.sum(onehot, axis=0, keepdims=True)
    cnt_out[...] = carry_ref[...].astype(I32)


def _outproj_call(y_rwkv, y_diff, x, mod6, w_out, norm2_gain, router_w, router_b, S, tm):
    T = x.shape[0]
    tiles_per_seq = S // tm
    rw_hi = router_w.astype(BF16)
    rw_lo = (router_w - rw_hi.astype(F32)).astype(BF16)
    tok = lambda w: pl.BlockSpec((tm, w), lambda t: (t, 0))
    full = lambda shape: pl.BlockSpec(shape, lambda t: (0,) * len(shape))
    return pl.pallas_call(
        functools.partial(_outproj_kernel, tm=tm),
        grid=(T // tm,),
        in_specs=[tok(RWKV_WIDTH), tok(DIFF_WIDTH), tok(D_MODEL),
                  pl.BlockSpec((1, 6, D_MODEL), lambda t: (t // tiles_per_seq, 0, 0)),
                  full((RWKV_WIDTH, D_MODEL)), full((DIFF_WIDTH, D_MODEL)), full((1, D_MODEL)),
                  full((D_MODEL, N_EXPERTS)), full((D_MODEL, N_EXPERTS)), full((1, N_EXPERTS))],
        out_specs=[tok(D_MODEL), pl.BlockSpec((tm, ROW_SUB, 128), lambda t: (t, 0, 0)),
                   tok(TOP_K), tok(TOP_K), tok(TOP_K), full((1, N_EXPERTS))],
        out_shape=[jax.ShapeDtypeStruct((T, D_MODEL), F32),
                   jax.ShapeDtypeStruct((T, ROW_SUB, 128), U32),
                   jax.ShapeDtypeStruct((T, TOP_K), I32),
                   jax.ShapeDtypeStruct((T, TOP_K), F32),
                   jax.ShapeDtypeStruct((T, TOP_K), I32),
                   jax.ShapeDtypeStruct((1, N_EXPERTS), I32)],
        scratch_shapes=[pltpu.VMEM((1, N_EXPERTS), F32)],
        compiler_params=pltpu.CompilerParams(
            dimension_semantics=("arbitrary",), vmem_limit_bytes=VMEM_LIMIT),
        name="outproj",
    )(y_rwkv, y_diff, x, mod6, w_out[:RWKV_WIDTH].astype(BF16), w_out[RWKV_WIDTH:].astype(BF16),
      norm2_gain.reshape(1, -1), rw_hi, rw_lo, router_b.reshape(1, -1))


def _dispatch_kernel(last_ref, cnt_ref, nu_ref, slots_hbm, hp_ref, xs_hbm, slot_smem, own_ref,
                     zero_ref, ssem, rsem, zsem, *, tm, n_blocks, n_tiles):
    t = pl.program_id(0)

    @pl.when(t == 0)
    def _():
        zero_ref[...] = jnp.zeros_like(zero_ref)

        def zero_copy(row):
            return pltpu.make_async_copy(zero_ref, xs_hbm.at[pl.ds(row, MOE_BLOCK)], zsem)

        def start_block(bk, carry):
            zero_copy(bk * MOE_BLOCK).start()
            return carry

        def wait_block(bk, carry):
            zero_copy(0).wait()
            return carry

        for e in range(N_EXPERTS):
            @pl.when(cnt_ref[e] > 0)
            def _():
                zero_copy(last_ref[e]).start()
        lax.fori_loop(nu_ref[0], n_blocks, start_block, 0)
        for e in range(N_EXPERTS):
            @pl.when(cnt_ref[e] > 0)
            def _():
                zero_copy(0).wait()
        lax.fori_loop(nu_ref[0], n_blocks, wait_block, 0)

    cur = t % 2
    n_idx = tm * TOP_K

    def slot_copy(tile, buf):
        return pltpu.make_async_copy(slots_hbm.at[pl.ds(tile * n_idx, n_idx)], slot_smem.at[buf],
                                     ssem.at[buf])

    def wait_rows(buf):
        for _ in range(TOP_K):
            pltpu.make_async_copy(own_ref.at[buf], xs_hbm.at[pl.ds(0, tm)], rsem.at[buf]).wait()

    @pl.when(t == 0)
    def _():
        slot_copy(0, 0).start()

    @pl.when(t + 1 < n_tiles)
    def _():
        slot_copy(t + 1, 1 - cur).start()

    @pl.when(t >= 2)
    def _():
        wait_rows(cur)

    own_ref[cur] = hp_ref[...]
    slot_copy(t, cur).wait()

    def body(i, carry):
        for kk in range(TOP_K):
            slot = slot_smem[cur, i * TOP_K + kk]
            pltpu.make_async_copy(own_ref.at[cur, i], xs_hbm.at[slot], rsem.at[cur]).start(
                priority=kk % 2)
        return carry

    lax.fori_loop(0, tm, body, 0)

    @pl.when(t == n_tiles - 1)
    def _():
        wait_rows(cur)
        if n_tiles >= 2:
            wait_rows(1 - cur)


def _dispatch_call(hp, slots_flat, last_block_row, counts, n_used, n_blocks, tm):
    T = hp.shape[0]
    n_rows = n_blocks * MOE_BLOCK
    grid_spec = pltpu.PrefetchScalarGridSpec(
        num_scalar_prefetch=3,
        grid=(T // tm,),
        in_specs=[pl.BlockSpec(memory_space=pl.ANY),
                  pl.BlockSpec((tm, ROW_SUB, 128), lambda t, last, cnt, nu: (t, 0, 0))],
        out_specs=pl.BlockSpec(memory_space=pl.ANY),
        scratch_shapes=[pltpu.SMEM((2, tm * TOP_K), I32),
                        pltpu.VMEM((2, tm, ROW_SUB, 128), U32),
                        pltpu.VMEM((MOE_BLOCK, ROW_SUB, 128), U32),
                        pltpu.SemaphoreType.DMA((2,)),
                        pltpu.SemaphoreType.DMA((2,)),
                        pltpu.SemaphoreType.DMA(())])
    return pl.pallas_call(
        functools.partial(_dispatch_kernel, tm=tm, n_blocks=n_blocks, n_tiles=T // tm),
        grid_spec=grid_spec,
        out_shape=jax.ShapeDtypeStruct((n_rows, ROW_SUB, 128), U32),
        compiler_params=pltpu.CompilerParams(
            dimension_semantics=("arbitrary",), vmem_limit_bytes=VMEM_LIMIT),
        name="dispatch",
    )(last_block_row, counts, n_used, slots_flat, hp)


def _experts_kernel(be_ref, nu_ref, valid_ref, xs_ref, wgu_ref, bgu_ref, wd_ref, bd_ref, ys_ref,
                    wgu_bf, wd_bf):
    i = pl.program_id(0)

    @pl.when((i == 0) | (be_ref[i] != be_ref[jnp.maximum(i - 1, 0)]))
    def _():
        wgu_bf[...] = wgu_ref[0].astype(BF16)
        wd_bf[...] = wd_ref[0].astype(BF16)

    @pl.when(i < nu_ref[0])
    def _():
        packed = _load_rows(xs_ref)
        row = lax.broadcasted_iota(I32, packed.shape, 0)
        packed = jnp.where(row < valid_ref[i], packed, jnp.uint32(0))
        xa, xb = _unpack_bf16_pairs(packed)
        x = jnp.concatenate([xa.astype(BF16), xb.astype(BF16)], axis=1)
        gu = _dot(x, wgu_bf[...]) + bgu_ref[0]
        gate = jnp.minimum(gu[:, :D_EXPERT], SWIGLU_LIMIT)
        up = jnp.clip(gu[:, D_EXPERT:], -SWIGLU_LIMIT, SWIGLU_LIMIT)
        act = (up + 1.0) * (gate * jax.nn.sigmoid(SWIGLU_ALPHA * gate))
        y = _dot(act.astype(BF16), wd_bf[...]) + bd_ref[0]
        _store_rows(ys_ref, _pack_bf16_pairs(y))

    @pl.when(i >= nu_ref[0])
    def _():
        ys_ref[...] = jnp.zeros_like(ys_ref)


def _experts_call(xs, block_e, n_used, valid_rows, w_gate_up, b_gate_up, w_down, b_down, n_blocks):
    per_expert = lambda shape: pl.BlockSpec(shape, lambda i, be, nu, vr: (be[i], 0, 0))
    grid_spec = pltpu.PrefetchScalarGridSpec(
        num_scalar_prefetch=3,
        grid=(n_blocks,),
        in_specs=[pl.BlockSpec((MOE_BLOCK, ROW_SUB, 128),
                               lambda i, be, nu, vr: (jnp.minimum(i, nu[0] - 1), 0, 0)),
                  per_expert((1, D_MODEL, 2 * D_EXPERT)), per_expert((1, 1, 2 * D_EXPERT)),
                  per_expert((1, D_EXPERT, D_MODEL)), per_expert((1, 1, D_MODEL))],
        out_specs=pl.BlockSpec((MOE_BLOCK, ROW_SUB, 128), lambda i, be, nu, vr: (i, 0, 0)),
        scratch_shapes=[pltpu.VMEM((D_MODEL, 2 * D_EXPERT), BF16),
                        pltpu.VMEM((D_EXPERT, D_MODEL), BF16)])
    return pl.pallas_call(
        _experts_kernel,
        grid_spec=grid_spec,
        out_shape=jax.ShapeDtypeStruct((n_blocks * MOE_BLOCK, ROW_SUB, 128), U32),
        compiler_params=pltpu.CompilerParams(
            dimension_semantics=("arbitrary",), vmem_limit_bytes=VMEM_LIMIT),
        name="experts",
    )(block_e, n_used, valid_rows, xs, w_gate_up, b_gate_up.reshape(N_EXPERTS, 1, -1), w_down,
      b_down.reshape(N_EXPERTS, 1, -1))


def _combine_kernel(slots_hbm, ys_hbm, x1_ref, w_ref, mod_ref, fg_ref, o_ref,
                    slot_smem, buf_ref, ssem, rsem, *, tm, n_tiles):
    t = pl.program_id(0)
    cur = t % 2
    n_idx = tm * TOP_K

    def slot_copy(tile, buf):
        return pltpu.make_async_copy(slots_hbm.at[pl.ds(tile * n_idx, n_idx)], slot_smem.at[buf],
                                     ssem.at[buf])

    def gather(buf):
        def body(i, carry):
            for kk in range(TOP_K):
                slot = slot_smem[buf, i * TOP_K + kk]
                pltpu.make_async_copy(ys_hbm.at[slot], buf_ref.at[buf, kk, i], rsem.at[buf]).start(
                    priority=kk % 2)
            return carry

        lax.fori_loop(0, tm, body, 0)

    @pl.when(t == 0)
    def _():
        first = slot_copy(0, 0)
        first.start()
        first.wait()
        gather(0)
        if n_tiles >= 2:
            slot_copy(1, 1).start()

    @pl.when(t + 1 < n_tiles)
    def _():
        slot_copy(t + 1, 1 - cur).wait()
        gather(1 - cur)

    @pl.when(t + 2 < n_tiles)
    def _():
        slot_copy(t + 2, cur).start()

    for kk in range(TOP_K):
        pltpu.make_async_copy(ys_hbm.at[pl.ds(0, tm)], buf_ref.at[cur, kk], rsem.at[cur]).wait()

    w = w_ref[...]
    acc_lo = jnp.zeros((tm, D_MODEL // 2), F32)
    acc_hi = jnp.zeros((tm, D_MODEL // 2), F32)
    for kk in range(TOP_K):
        lo, hi = _unpack_bf16_pairs(_load_rows(buf_ref.at[cur, kk]))
        wk = w[:, kk:kk + 1]
        acc_lo = acc_lo + wk * lo
        acc_hi = acc_hi + wk * hi
    moe = jnp.concatenate([acc_lo, acc_hi], axis=1)
    gate2 = mod_ref[0, 5:6, :]
    x2 = x1_ref[...] + gate2 * moe
    ms = jnp.mean(x2 * x2, axis=-1, keepdims=True)
    o_ref[...] = x2 * lax.rsqrt(ms + NORM_EPS) * fg_ref[...]


def _combine_call(ys, slots_flat, x1, top_w, mod6, final_gain, S, tm):
    T = x1.shape[0]
    tiles_per_seq = S // tm
    return pl.pallas_call(
        functools.partial(_combine_kernel, tm=tm, n_tiles=T // tm),
        grid=(T // tm,),
        in_specs=[pl.BlockSpec(memory_space=pl.ANY),
                  pl.BlockSpec(memory_space=pl.ANY),
                  pl.BlockSpec((tm, D_MODEL), lambda t: (t, 0)),
                  pl.BlockSpec((tm, TOP_K), lambda t: (t, 0)),
                  pl.BlockSpec((1, 6, D_MODEL), lambda t: (t // tiles_per_seq, 0, 0)),
                  pl.BlockSpec((1, D_MODEL), lambda t: (0, 0))],
        out_specs=pl.BlockSpec((tm, D_MODEL), lambda t: (t, 0)),
        out_shape=jax.ShapeDtypeStruct((T, D_MODEL), F32),
        scratch_shapes=[pltpu.SMEM((2, tm * TOP_K), I32),
                        pltpu.VMEM((2, TOP_K, tm, ROW_SUB, 128), U32),
                        pltpu.SemaphoreType.DMA((2,)),
                        pltpu.SemaphoreType.DMA((2,))],
        compiler_params=pltpu.CompilerParams(
            dimension_semantics=("arbitrary",), vmem_limit_bytes=VMEM_LIMIT),
        name="combine",
    )(slots_flat, ys, x1, top_w, mod6, final_gain.reshape(1, -1))


SC_WINDOW = 128


def _sc_mesh():
    return plsc.VectorSubcoreMesh(core_axis_name="c", subcore_axis_name="s")


def _sc_worker_chunks(n_chunks):
    info = plsc.get_sparse_core_info()
    n_workers = info.num_cores * info.num_subcores
    assert n_chunks % n_workers == 0
    return info.num_cores, n_chunks // n_workers


def _sc_scatter_call(rows, slot_chunks, n_out_rows):
    T = rows.shape[0]
    n_chunks = slot_chunks.shape[0]
    per_pass = T // SC_WINDOW
    n_cores, per_worker = _sc_worker_chunks(n_chunks)

    @functools.partial(
        pl.kernel, mesh=_sc_mesh(),
        out_type=jax.ShapeDtypeStruct((n_out_rows, ROW_SUB, 128), U32),
        scratch_types=[pltpu.VMEM((SC_WINDOW,), I32), pltpu.VMEM((SC_WINDOW, ROW_SUB, 128), U32)],
        name="sc_dispatch")
    def run(rows_hbm, idx_hbm, out_hbm, idx_v, rows_v):
        wid = lax.axis_index("s") * n_cores + lax.axis_index("c")

        @pl.loop(0, per_worker)
        def _(j):
            chunk = wid * per_worker + j
            src = lax.rem(chunk, per_pass) * SC_WINDOW
            pltpu.sync_copy(idx_hbm.at[chunk], idx_v)
            pltpu.sync_copy(rows_hbm.at[pl.ds(src, SC_WINDOW)], rows_v)
            pltpu.sync_copy(rows_v, out_hbm.at[idx_v])

    return run(rows, slot_chunks)


def _sc_gather_call(table, slot_chunks):
    n_chunks = slot_chunks.shape[0]
    n_cores, per_worker = _sc_worker_chunks(n_chunks)

    @functools.partial(
        pl.kernel, mesh=_sc_mesh(),
        out_type=jax.ShapeDtypeStruct((n_chunks * SC_WINDOW, ROW_SUB, 128), U32),
        scratch_types=[pltpu.VMEM((SC_WINDOW,), I32), pltpu.VMEM((SC_WINDOW, ROW_SUB, 128), U32)],
        name="sc_collect")
    def run(table_hbm, idx_hbm, out_hbm, idx_v, rows_v):
        wid = lax.axis_index("s") * n_cores + lax.axis_index("c")

        @pl.loop(0, per_worker)
        def _(j):
            chunk = wid * per_worker + j
            pltpu.sync_copy(idx_hbm.at[chunk], idx_v)
            pltpu.sync_copy(table_hbm.at[idx_v], rows_v)
            pltpu.sync_copy(rows_v, out_hbm.at[pl.ds(chunk * SC_WINDOW, SC_WINDOW)])

    return run(table, slot_chunks)


def _combine_dense_kernel(g0_ref, g1_ref, g2_ref, g3_ref, x1_ref, w_ref, mod_ref, fg_ref, o_ref,
                          *, tm):
    w = w_ref[...]
    acc_lo = jnp.zeros((tm, D_MODEL // 2), F32)
    acc_hi = jnp.zeros((tm, D_MODEL // 2), F32)
    for kk, g_ref in enumerate((g0_ref, g1_ref, g2_ref, g3_ref)):
        lo, hi = _unpack_bf16_pairs(_load_rows(g_ref))
        wk = w[:, kk:kk + 1]
        acc_lo = acc_lo + wk * lo
        acc_hi = acc_hi + wk * hi
    moe = jnp.concatenate([acc_lo, acc_hi], axis=1)
    gate2 = mod_ref[0, 5:6, :]
    x2 = x1_ref[...] + gate2 * moe
    ms = jnp.mean(x2 * x2, axis=-1, keepdims=True)
    o_ref[...] = x2 * lax.rsqrt(ms + NORM_EPS) * fg_ref[...]


def _combine_dense_call(gathered, x1, top_w, mod6, final_gain, S, tm):
    T = x1.shape[0]
    nt = T // tm
    tiles_per_seq = S // tm
    rows = lambda kk: pl.BlockSpec((tm, ROW_SUB, 128), lambda t: (kk * nt + t, 0, 0))
    return pl.pallas_call(
        functools.partial(_combine_dense_kernel, tm=tm),
        grid=(nt,),
        in_specs=[rows(0), rows(1), rows(2), rows(3),
                  pl.BlockSpec((tm, D_MODEL), lambda t: (t, 0)),
                  pl.BlockSpec((tm, TOP_K), lambda t: (t, 0)),
                  pl.BlockSpec((1, 6, D_MODEL), lambda t: (t // tiles_per_seq, 0, 0)),
                  pl.BlockSpec((1, D_MODEL), lambda t: (0, 0))],
        out_specs=pl.BlockSpec((tm, D_MODEL), lambda t: (t, 0)),
        out_shape=jax.ShapeDtypeStruct((T, D_MODEL), F32),
        compiler_params=pltpu.CompilerParams(
            dimension_semantics=("arbitrary",), vmem_limit_bytes=VMEM_LIMIT),
        name="combine",
    )(gathered, gathered, gathered, gathered, x1, top_w, mod6, final_gain.reshape(1, -1))


def _forward(x, c, mod_w, mod_b, norm1_gain, w_in, rwkv_shift_mu, rwkv_w0, rwkv_w_up, rwkv_a0,
             rwkv_a_up, rwkv_g_up, rwkv_k_k, rwkv_k_a, rwkv_r_k, rwkv_gn_gain, rwkv_gn_bias,
             diff_lambda_q1, diff_lambda_k1, diff_lambda_q2, diff_lambda_k2, diff_subln_gain,
             w_out, norm2_gain, router_w, router_b, w_gate_up, b_gate_up, w_down, b_down,
             final_gain):
    B, S, D = x.shape
    T = B * S
    tm_in = min(512, S)
    tq = min(512, S)
    tm = min(256, S)

    mod6 = _mod_call(c, mod_w, mod_b).reshape(B, 6, D)
    (r, k, v, a, b, g, lw, q, kat, vt) = _inproj_call(
        x, mod6, norm1_gain, w_in, rwkv_shift_mu, rwkv_w0, rwkv_w_up, rwkv_a0, rwkv_a_up,
        rwkv_g_up, rwkv_k_k, rwkv_k_a, tm_in)
    y_rwkv = _rwkv_call(r, k, v, a, b, g, lw, rwkv_r_k, rwkv_gn_gain, rwkv_gn_bias,
                        min(4, S // CHUNK))
    y_diff = _attn_call(q, kat, vt, diff_lambda_q1, diff_lambda_k1, diff_lambda_q2,
                        diff_lambda_k2, diff_subln_gain, tq)

    x1, hp, top_idx, top_w, rank, counts = _outproj_call(
        y_rwkv.reshape(T, -1), y_diff.reshape(T, -1), x.reshape(T, D), mod6, w_out, norm2_gain,
        router_w, router_b, S, tm)

    counts = counts.reshape(N_EXPERTS)
    padded = ((counts + MOE_BLOCK - 1) // MOE_BLOCK) * MOE_BLOCK
    pad_ends = jnp.cumsum(padded)
    pad_starts = pad_ends - padded
    n_blocks = -(-(T * TOP_K + N_EXPERTS * (MOE_BLOCK - 1)) // MOE_BLOCK)
    n_used = (pad_ends[-1] // MOE_BLOCK).astype(I32).reshape(1)
    block_start = jnp.minimum(jnp.arange(n_blocks, dtype=I32), n_used[0] - 1) * MOE_BLOCK
    block_e = jnp.minimum(jnp.sum(pad_ends[None, :] <= block_start[:, None], axis=1),
                          N_EXPERTS - 1).astype(I32)
    expert_ids = jnp.arange(N_EXPERTS, dtype=I32)
    start_of = jnp.sum(jnp.where(top_idx[..., None] == expert_ids, pad_starts.astype(I32), 0), axis=-1)
    slots = start_of + rank
    slot_chunks = slots.T.reshape(TOP_K * T // SC_WINDOW, SC_WINDOW)
    block_row = jnp.arange(n_blocks, dtype=I32) * MOE_BLOCK
    valid_rows = jnp.clip(counts.astype(I32)[block_e] - (block_row - pad_starts.astype(I32)[block_e]),
                          0, MOE_BLOCK).astype(I32)

    xs = _sc_scatter_call(hp, slot_chunks, n_blocks * MOE_BLOCK)
    ys = _experts_call(xs, block_e, n_used, valid_rows, w_gate_up, b_gate_up, w_down, b_down,
                       n_blocks)
    gathered = _sc_gather_call(ys, slot_chunks)
    out = _combine_dense_call(gathered, x1, top_w, mod6, final_gain, S, tm)
    return out.reshape(B, S, D)


def kernel(x, c, mod_w, mod_b, norm1_gain, w_in, rwkv_shift_mu, rwkv_w0, rwkv_w_up, rwkv_a0, rwkv_a_up, rwkv_g_up, rwkv_k_k, rwkv_k_a, rwkv_r_k, rwkv_gn_gain, rwkv_gn_bias, diff_lambda_q1, diff_lambda_k1, diff_lambda_q2, diff_lambda_k2, diff_subln_gain, w_out, norm2_gain, router_w, router_b, w_gate_up, b_gate_up, w_down, b_down, final_gain):
    return _forward(x, c, mod_w[0], mod_b[0], norm1_gain[0], w_in[0], rwkv_shift_mu[0], rwkv_w0[0],
                    rwkv_w_up[0], rwkv_a0[0], rwkv_a_up[0], rwkv_g_up[0], rwkv_k_k[0], rwkv_k_a[0],
                    rwkv_r_k[0].reshape(-1), rwkv_gn_gain[0], rwkv_gn_bias[0], diff_lambda_q1[0],
                    diff_lambda_k1[0], diff_lambda_q2[0], diff_lambda_k2[0], diff_subln_gain[0],
                    w_out[0], norm2_gain[0], router_w[0], router_b[0], w_gate_up[0], b_gate_up[0],
                    w_down[0], b_down[0], final_gain)
```

```python
import functools
import math

import jax
import jax.numpy as jnp
from jax import lax
from jax.experimental import pallas as pl
from jax.experimental.pallas import tpu as pltpu
from jax.experimental.pallas import tpu_sc as plsc

F32 = jnp.float32
BF16 = jnp.bfloat16
I32 = jnp.int32
U32 = jnp.uint32

D_MODEL = 1024
RWKV_WIDTH = 512
RWKV_HEADS = 8
HEAD_DIM = 64
RWKV_COLS = 3 * RWKV_WIDTH + 64 + 64 + 128
DIFF_WIDTH = 512
DIFF_HEADS = 4
DIFF_COLS = 3 * DIFF_WIDTH
N_EXPERTS = 32
TOP_K = 4
D_EXPERT = 1024
SWIGLU_LIMIT = 7.0
SWIGLU_ALPHA = 1.702
MOE_BLOCK = 512
NORM_EPS = 1e-5
GN_EPS = 64e-5
LAMBDA_INIT = 0.8 - 0.6 * math.exp(-0.3 * 0)

CHUNK = 64
GROUP = 4
GW = GROUP * HEAD_DIM
VMEM_LIMIT = 56 * 1024 * 1024


def _dot(a, b):
    return jnp.dot(a, b, preferred_element_type=F32)


def _dot_nt(a, b):
    return lax.dot_general(a, b, (((1,), (1,)), ((), ())), preferred_element_type=F32)


def _dot_tn(a, b):
    return lax.dot_general(a, b, (((0,), (0,)), ((), ())), preferred_element_type=F32)


def _split_dot(x, w_bf16):
    hi = x.astype(BF16)
    lo = (x - hi.astype(F32)).astype(BF16)
    return _dot(hi, w_bf16) + _dot(lo, w_bf16)


def _mod_kernel(c_ref, w_ref, b_ref, o_ref):
    c = c_ref[...]
    s = c * jax.nn.sigmoid(c)
    o_ref[...] = _dot(s, w_ref[...]) + b_ref[...]


def _mod_call(c, mod_w, mod_b):
    B = c.shape[0]
    n = mod_w.shape[1]
    tn = 1536
    return pl.pallas_call(
        _mod_kernel,
        grid=(n // tn,),
        in_specs=[pl.BlockSpec((B, D_MODEL), lambda j: (0, 0)),
                  pl.BlockSpec((D_MODEL, tn), lambda j: (0, j)),
                  pl.BlockSpec((1, tn), lambda j: (0, j))],
        out_specs=pl.BlockSpec((B, tn), lambda j: (0, j)),
        out_shape=jax.ShapeDtypeStruct((B, n), F32),
        compiler_params=pltpu.CompilerParams(
            dimension_semantics=("arbitrary",), vmem_limit_bytes=VMEM_LIMIT),
        name="mod",
    )(c, mod_w, mod_b.reshape(1, n))


def _inproj_kernel(x_ref, mod_ref, g1_ref, wrw_ref, wat_ref, mu_ref, w0_ref, wup_ref,
                   a0_ref, aup_ref, gup_ref, kk_ref, ka_ref, hsum_ref,
                   r_out, k_out, v_out, a_out, b_out, g_out, lw_out,
                   q_out, kat_out, vt_out, carry_ref, *, tm):
    s = pl.program_id(1)
    x = x_ref[0]
    ms = jnp.mean(x * x, axis=-1, keepdims=True)
    shift1 = mod_ref[0, 0:1, :]
    scale1 = mod_ref[0, 1:2, :]
    h = x * lax.rsqrt(ms + NORM_EPS) * g1_ref[...] * (1.0 + scale1) + shift1
    hb = h.astype(BF16)

    p = _dot(hb, wrw_ref[...])
    pa = _dot(hb, wat_ref[...])

    @pl.when(s == 0)
    def _():
        carry_ref[...] = jnp.zeros_like(carry_ref)

    rolled = pltpu.roll(p, shift=1, axis=0)
    row = lax.broadcasted_iota(I32, p.shape, 0)
    prev = jnp.where(row == 0, carry_ref[...], rolled)
    carry_ref[...] = p[tm - 1:tm, :]
    ps = p + mu_ref[...] * (prev - p)

    r = ps[:, 0:512]
    k = ps[:, 512:1024]
    v = ps[:, 1024:1536]
    lo2 = ps[:, 1536:1664]
    g_lo = ps[:, 1664:1792]
    z = w0_ref[...] + _dot(jnp.tanh(lo2).astype(BF16), wup_ref[...])
    nz = -z
    softplus = jnp.maximum(nz, 0.0) + jnp.log(1.0 + jnp.exp(-jnp.abs(nz)))
    w = -softplus - 0.5
    lw_out[0] = -jnp.exp(w)
    a = jax.nn.sigmoid(a0_ref[...] + _dot(lo2.astype(BF16), aup_ref[...]))
    g = _dot(jax.nn.sigmoid(g_lo).astype(BF16), gup_ref[...])
    kk = k * kk_ref[...]
    ssq = _split_dot(kk * kk, hsum_ref[...])
    kk = kk / jnp.maximum(jnp.sqrt(ssq), 1e-12)
    k = k * (1.0 + (a - 1.0) * ka_ref[...])
    r_out[0] = r.astype(BF16)
    k_out[0] = k.astype(BF16)
    v_out[0] = v.astype(BF16)
    a_out[0] = (-kk).astype(BF16)
    b_out[0] = (kk * a).astype(BF16)
    g_out[0] = g.astype(BF16)

    q_out[0] = (pa[:, 0:512] * (1.0 / math.sqrt(HEAD_DIM))).astype(BF16)
    kat_out[0] = pa[:, 512:1024].astype(BF16)
    for hh in range(DIFF_HEADS):
        vh = pa[:, 1024 + hh * 128:1024 + (hh + 1) * 128]
        vt_out[0, hh, 0] = vh.T.astype(BF16)


def _head_sum_matrix(width):
    i = jnp.arange(width) // HEAD_DIM
    return (i[:, None] == i[None, :]).astype(BF16)


def _inproj_call(x, mod6, norm1_gain, w_in, mu, w0, w_up, a0, a_up, g_up, k_k, k_a, tm):
    B, S, _ = x.shape
    ns = S // tm
    w_rw = w_in[:, :RWKV_COLS].astype(BF16)
    w_at = w_in[:, RWKV_COLS:].astype(BF16)
    zeros = jnp.zeros((64, RWKV_WIDTH), F32)
    wup_p = jnp.concatenate([w_up, zeros], axis=0).astype(BF16)
    aup_p = jnp.concatenate([zeros, a_up], axis=0).astype(BF16)
    row = lambda v: v.reshape(1, -1)
    full = lambda shape: pl.BlockSpec(shape, lambda b, s: (0,) * len(shape))
    tok = lambda w: pl.BlockSpec((1, tm, w), lambda b, s: (b, s, 0))
    rw_shape = jax.ShapeDtypeStruct((B, S, RWKV_WIDTH), BF16)
    out_shape = [rw_shape] * 6 + [
        jax.ShapeDtypeStruct((B, S, RWKV_WIDTH), F32),
        jax.ShapeDtypeStruct((B, S, DIFF_WIDTH), BF16),
        jax.ShapeDtypeStruct((B, S, DIFF_WIDTH), BF16),
        jax.ShapeDtypeStruct((B, DIFF_HEADS, ns, 128, tm), BF16)]
    out_specs = [tok(RWKV_WIDTH)] * 7 + [tok(DIFF_WIDTH)] * 2 + [
        pl.BlockSpec((1, DIFF_HEADS, 1, 128, tm), lambda b, s: (b, 0, s, 0, 0))]
    return pl.pallas_call(
        functools.partial(_inproj_kernel, tm=tm),
        grid=(B, ns),
        in_specs=[tok(D_MODEL),
                  pl.BlockSpec((1, 6, D_MODEL), lambda b, s: (b, 0, 0)),
                  full((1, D_MODEL)),
                  full((D_MODEL, RWKV_COLS)), full((D_MODEL, DIFF_COLS)),
                  full((1, RWKV_COLS)), full((1, RWKV_WIDTH)), full((128, RWKV_WIDTH)),
                  full((1, RWKV_WIDTH)), full((128, RWKV_WIDTH)), full((128, RWKV_WIDTH)),
                  full((1, RWKV_WIDTH)), full((1, RWKV_WIDTH)),
                  full((RWKV_WIDTH, RWKV_WIDTH))],
        out_specs=out_specs,
        out_shape=out_shape,
        scratch_shapes=[pltpu.VMEM((1, RWKV_COLS), F32)],
        compiler_params=pltpu.CompilerParams(
            dimension_semantics=("arbitrary", "arbitrary"), vmem_limit_bytes=VMEM_LIMIT),
        name="inproj",
    )(x, mod6, row(norm1_gain), w_rw, w_at, row(mu), row(w0), wup_p, row(a0), aup_p,
      g_up.astype(BF16), row(k_k), row(k_a), _head_sum_matrix(RWKV_WIDTH))


def _rwkv_kernel(r_ref, k_ref, v_ref, a_ref, b_ref, g_ref, lw_ref, rk_ref, gng_ref, gnb_ref,
                 hsum_ref, tri_ref, y_ref, s0_ref, s1_ref, *, nck):
    C = CHUNK
    n_grp = RWKV_HEADS // GROUP
    s_refs = (s0_ref, s1_ref)

    @pl.when(pl.program_id(1) == 0)
    def _():
        s0_ref[...] = jnp.zeros_like(s0_ref)
        s1_ref[...] = jnp.zeros_like(s1_ref)

    ti = lax.broadcasted_iota(I32, (C, GW), 0)
    si = lax.broadcasted_iota(I32, (C, GW), 1) % C
    incl = si <= ti
    strict = si < ti
    eye = (si == ti).astype(F32)
    bdmask = (lax.broadcasted_iota(I32, (GW, GW), 0) // HEAD_DIM ==
              lax.broadcasted_iota(I32, (GW, GW), 1) // HEAD_DIM)
    hsum = hsum_ref[...]
    chains = [(ck, g) for ck in range(nck) for g in range(n_grp)]

    def bd(xb):
        return jnp.where(bdmask, jnp.concatenate([xb] * GROUP, axis=0), jnp.zeros((), xb.dtype))

    def cut(x, ch):
        ck, g = ch
        return x[ck * C:(ck + 1) * C, g * GW:(g + 1) * GW]

    def rows_of_chunk(x, row):
        return jnp.concatenate(
            [jnp.broadcast_to(x[ck * C + row:ck * C + row + 1, :], (C, x.shape[1]))
             for ck in range(nck)], axis=0)

    lw = lw_ref[0]
    L = _split_dot_left(tri_ref[...], lw)
    Lx = L - lw
    Lc = rows_of_chunk(L, C - 1)
    rho = rows_of_chunk(L, C // 2 - 1)
    r = r_ref[0].astype(F32)
    k = k_ref[0].astype(F32)
    vb = v_ref[0]
    a = a_ref[0].astype(F32)
    b = b_ref[0].astype(F32)
    e_k = jnp.exp(rho - L)
    rt = (r * jnp.exp(L - rho)).astype(BF16)
    at = (a * jnp.exp(Lx - rho)).astype(BF16)
    kt = (k * e_k).astype(BF16)
    bt = (b * e_k).astype(BF16)
    r0 = (r * jnp.exp(L)).astype(BF16)
    a0 = (a * jnp.exp(Lx)).astype(BF16)
    e_o = jnp.exp(Lc - L)
    kh = (k * e_o).astype(BF16)
    bh = (b * e_o).astype(BF16)
    gc = jnp.exp(Lc)

    lhs = {ch: jnp.concatenate([cut(rt, ch), cut(at, ch)], axis=0) for ch in chains}
    ak = {ch: _dot_nt(lhs[ch], bd(cut(kt, ch))) for ch in chains}
    ab = {ch: _dot_nt(lhs[ch], bd(cut(bt, ch))) for ch in chains}
    a_rk = {ch: jnp.where(incl, ak[ch][:C], 0.0).astype(BF16) for ch in chains}
    a_ak = {ch: jnp.where(strict, ak[ch][C:], 0.0).astype(BF16) for ch in chains}
    a_rb = {ch: jnp.where(incl, ab[ch][:C], 0.0).astype(BF16) for ch in chains}
    n = {ch: jnp.where(strict, ab[ch][C:], 0.0) for ch in chains}

    p = {ch: eye + n[ch] for ch in chains}
    xb = {ch: n[ch].astype(BF16) for ch in chains}
    x = {ch: _dot(xb[ch], bd(xb[ch])) for ch in chains}
    for _ in range(4):
        xb = {ch: x[ch].astype(BF16) for ch in chains}
        o = {ch: _dot(jnp.concatenate([xb[ch], p[ch].astype(BF16)], axis=0), bd(xb[ch]))
             for ch in chains}
        x = {ch: o[ch][:C] for ch in chains}
        p = {ch: p[ch] + o[ch][C:] for ch in chains}
    o = {ch: _dot(p[ch].astype(BF16), bd(x[ch].astype(BF16))) for ch in chains}
    tb = {ch: (p[ch] + o[ch]).astype(BF16) for ch in chains}

    av = {ch: _dot(jnp.concatenate([a_ak[ch], a_rk[ch]], axis=0), bd(cut(vb, ch))) for ch in chains}
    w1 = {ch: _dot(tb[ch], bd(cut(a0, ch))).astype(BF16) for ch in chains}
    w2 = {ch: _dot(tb[ch], bd(av[ch][:C].astype(BF16))) for ch in chains}

    sv = [s_refs[g][...] for g in range(n_grp)]
    ys = {}
    for ck in range(nck):
        grp = [(ck, g) for g in range(n_grp)]
        rws = [_dot_nt(jnp.concatenate([cut(r0, ch), w1[ch]], axis=0), sv[ch[1]].astype(BF16))
               for ch in grp]
        ub = [(rws[g][C:] + w2[(ck, g)]).astype(BF16) for g in range(n_grp)]
        upd = [_dot_tn(jnp.concatenate([ub[g], cut(vb, (ck, g))], axis=0),
                       jnp.concatenate([cut(bh, (ck, g)), cut(kh, (ck, g))], axis=0))
               for g in range(n_grp)]
        for g in range(n_grp):
            ch = (ck, g)
            ys[ch] = rws[g][:C] + _dot(a_rb[ch], bd(ub[g])) + av[ch][C:]
        sv = [sv[g] * cut(gc, (ck, g))[0:1, :] + jnp.where(bdmask, upd[g], 0.0)
              for g in range(n_grp)]
    for g in range(n_grp):
        s_refs[g][...] = sv[g]

    for g in range(n_grp):
        sl = slice(g * GW, (g + 1) * GW)
        y = jnp.concatenate([ys[(ck, g)] for ck in range(nck)], axis=0)
        mean = _split_dot(y, hsum) * (1.0 / HEAD_DIM)
        d = y - mean
        var = _split_dot(d * d, hsum) * (1.0 / HEAD_DIM)
        yn = d * lax.rsqrt(var + GN_EPS) * gng_ref[:, sl] + gnb_ref[:, sl]
        bonus = _split_dot(r[:, sl] * k[:, sl] * rk_ref[:, sl], hsum) * vb[:, sl].astype(F32)
        y_ref[0, :, sl] = ((yn + bonus) * g_ref[0, :, sl].astype(F32)).astype(BF16)


def _split_dot_left(w_bf16, x):
    hi = x.astype(BF16)
    lo = (x - hi.astype(F32)).astype(BF16)
    return _dot(w_bf16, hi) + _dot(w_bf16, lo)


def _rwkv_call(r, k, v, a, b, g, lw, r_k, gn_gain, gn_bias, nck):
    B, S, _ = r.shape
    tb = nck * CHUNK
    idx = jnp.arange(tb)
    tri = ((idx[:, None] // CHUNK == idx[None, :] // CHUNK) &
           (idx[:, None] >= idx[None, :])).astype(BF16)
    tok = pl.BlockSpec((1, tb, RWKV_WIDTH), lambda bb, c: (bb, c, 0))
    par = pl.BlockSpec((1, RWKV_WIDTH), lambda bb, c: (0, 0))
    return pl.pallas_call(
        functools.partial(_rwkv_kernel, nck=nck),
        grid=(B, S // tb),
        in_specs=[tok] * 7 + [par] * 3 + [pl.BlockSpec((GW, GW), lambda bb, c: (0, 0)),
                                          pl.BlockSpec((tb, tb), lambda bb, c: (0, 0))],
        out_specs=tok,
        out_shape=jax.ShapeDtypeStruct((B, S, RWKV_WIDTH), BF16),
        scratch_shapes=[pltpu.VMEM((GW, GW), F32)] * (RWKV_HEADS // GROUP),
        compiler_params=pltpu.CompilerParams(
            dimension_semantics=("arbitrary", "arbitrary"), vmem_limit_bytes=VMEM_LIMIT),
        name="rwkv",
    )(r, k, v, a, b, g, lw, r_k.reshape(1, -1), gn_gain.reshape(1, -1), gn_bias.reshape(1, -1),
      _head_sum_matrix(GW), tri)


def _tree_reduce(op, x):
    while x.shape[0] > 8:
        h = x.shape[0] // 2
        x = op(x[:h], x[h:])
    return x


def _attn_kernel(slopes_ref, q_ref, k_ref, vt_ref, lq1_ref, lk1_ref, lq2_ref, lk2_ref,
                 gain_ref, o_ref, sa_ref, sb_ref, acc_ref, m_ref, l_ref, *, tq, tk):
    hh = pl.program_id(1)
    i = pl.program_id(2)
    slope = slopes_ref[hh]

    q = q_ref[0].astype(F32)
    lane = lax.broadcasted_iota(I32, (tq, 256), 1)
    qrow = lax.broadcasted_iota(I32, (tq, 256), 0)
    qa = (qrow >> 5).astype(F32) * (32.0 * slope)
    qb = (qrow & 31).astype(F32) * slope
    qbias = jnp.where(lane < 130, 1.0, jnp.where(lane == 130, -qa, jnp.where(lane == 131, -qb, 0.0)))
    qpad = jnp.concatenate([q, jnp.zeros_like(q)], axis=1)
    q_aug = []
    for c in range(2):
        in_comp = (lane >= c * HEAD_DIM) & (lane < (c + 1) * HEAD_DIM)
        q_aug.append(jnp.where(in_comp, qpad, jnp.where(lane >= 128, qbias, 0.0)))
    klane = lax.broadcasted_iota(I32, (tk, 128), 1)
    krow = lax.broadcasted_iota(I32, (tk, 128), 0)
    ka = (krow >> 5).astype(F32) * (32.0 * slope)
    kb_ = (krow & 31).astype(F32) * slope
    kbias = jnp.where(klane == 0, ka, jnp.where(klane == 1, kb_, jnp.where(klane < 4, 1.0, 0.0))
                      ).astype(BF16)

    qt_both = jnp.concatenate([q_aug[0].T, q_aug[1].T], axis=1).astype(BF16)

    m_ref[...] = jnp.full_like(m_ref, -jnp.inf)
    l_ref[...] = jnp.zeros_like(l_ref)
    acc_ref[...] = jnp.zeros_like(acc_ref)

    def scores(j):
        kb = k_ref[0, pl.ds(pl.multiple_of(j * tk, tk), tk), :]
        return _dot(jnp.concatenate([kb, kbias], axis=1), qt_both)

    def absorb(s_ref, j, masked):
        s = s_ref[...]
        off = j * tk - i * tq
        cj = slope * off.astype(F32)
        if masked:
            keep = (lax.broadcasted_iota(I32, (tk, tq), 0) -
                    lax.broadcasted_iota(I32, (tk, tq), 1) + off) <= 0
            s = jnp.where(jnp.concatenate([keep, keep], axis=1), s, -jnp.inf)
        m_loc = jnp.max(_tree_reduce(jnp.maximum, s), axis=0, keepdims=True)
        m_old = m_ref[...]
        m_new = jnp.maximum(m_old, m_loc + cj)
        pr = jnp.exp(s - (m_new - cj))
        alpha = jnp.exp(m_old - m_new)
        l_ref[...] = alpha * l_ref[...] + jnp.sum(_tree_reduce(jnp.add, pr), axis=0, keepdims=True)
        acc_ref[...] = alpha * acc_ref[...] + _dot(vt_ref[0, 0, j], pr.astype(BF16))
        m_ref[...] = m_new

    n_full = (i * tq) // tk
    sa_ref[...] = scores(0)

    def pair_body(jj, carry):
        j0 = 2 * jj
        sb_ref[...] = scores(j0 + 1)
        absorb(sa_ref, j0, False)
        sa_ref[...] = scores(j0 + 2)
        absorb(sb_ref, j0 + 1, False)
        return carry

    lax.fori_loop(0, n_full // 2, pair_body, 0)

    @pl.when(n_full % 2 == 1)
    def _():
        sb_ref[...] = scores(n_full)
        absorb(sa_ref, n_full - 1, False)
        absorb(sb_ref, n_full, True)

    @pl.when(n_full % 2 == 0)
    def _():
        absorb(sa_ref, n_full, True)

    lam = (jnp.exp(jnp.sum(lq1_ref[...] * lk1_ref[...], axis=-1, keepdims=True))
           - jnp.exp(jnp.sum(lq2_ref[...] * lk2_ref[...], axis=-1, keepdims=True))
           + LAMBDA_INIT)
    o2 = acc_ref[...] / l_ref[...]
    o = o2[:, :tq] - lam * o2[:, tq:]
    ot = o.T
    ms = jnp.mean(ot * ot, axis=-1, keepdims=True)
    y = ot * lax.rsqrt(ms + NORM_EPS) * gain_ref[...] * (1.0 - LAMBDA_INIT)
    o_ref[0] = y.astype(BF16)


def _attn_call(q, k, vt, lq1, lk1, lq2, lk2, subln_gain, tq):
    B, S, _ = q.shape
    ns, tk = vt.shape[2], vt.shape[4]
    slopes = jnp.asarray([2.0 ** (-8.0 * (i + 1) / DIFF_HEADS) for i in range(DIFF_HEADS)], F32)
    vec = lambda n: pl.BlockSpec((1, n), lambda b, h, i, sl: (0, 0))
    grid_spec = pltpu.PrefetchScalarGridSpec(
        num_scalar_prefetch=1,
        grid=(B, DIFF_HEADS, S // tq),
        in_specs=[pl.BlockSpec((1, tq, 128), lambda b, h, i, sl: (b, i, h)),
                  pl.BlockSpec((1, S, 128), lambda b, h, i, sl: (b, 0, h)),
                  pl.BlockSpec((1, 1, ns, 128, tk), lambda b, h, i, sl: (b, h, 0, 0, 0)),
                  vec(HEAD_DIM), vec(HEAD_DIM), vec(HEAD_DIM), vec(HEAD_DIM), vec(128)],
        out_specs=pl.BlockSpec((1, tq, 128), lambda b, h, i, sl: (b, i, h)),
        scratch_shapes=[pltpu.VMEM((tk, 2 * tq), F32), pltpu.VMEM((tk, 2 * tq), F32),
                        pltpu.VMEM((128, 2 * tq), F32), pltpu.VMEM((1, 2 * tq), F32),
                        pltpu.VMEM((1, 2 * tq), F32)])
    return pl.pallas_call(
        functools.partial(_attn_kernel, tq=tq, tk=tk),
        grid_spec=grid_spec,
        out_shape=jax.ShapeDtypeStruct((B, S, DIFF_WIDTH), BF16),
        compiler_params=pltpu.CompilerParams(
            dimension_semantics=("arbitrary", "arbitrary", "arbitrary"),
            vmem_limit_bytes=VMEM_LIMIT),
        name="attn",
    )(slopes, q, k, vt, lq1.reshape(1, -1), lk1.reshape(1, -1), lq2.reshape(1, -1),
      lk2.reshape(1, -1), subln_gain.reshape(1, -1))


def _pack_bf16_pairs(x):
    w = x.shape[1] // 2
    lo = pltpu.bitcast(x[:, :w].astype(BF16).astype(F32), U32)
    hi = pltpu.bitcast(x[:, w:].astype(BF16).astype(F32), U32)
    return (lo >> 16) | (hi & jnp.uint32(0xFFFF0000))


def _unpack_bf16_pairs(p):
    lo = pltpu.bitcast(p << 16, F32)
    hi = pltpu.bitcast(p & jnp.uint32(0xFFFF0000), F32)
    return lo, hi


ROW_SUB = (D_MODEL // 2) // 128


def _store_rows(ref, x2d):
    n = x2d.shape[0]
    for s in range(ROW_SUB):
        ref[pl.ds(s, n, stride=ROW_SUB), :] = x2d[:, s * 128:(s + 1) * 128]


def _load_rows(ref):
    n = ref.shape[0] // ROW_SUB
    return jnp.concatenate([ref[pl.ds(s, n, stride=ROW_SUB), :] for s in range(ROW_SUB)], axis=1)


def _outproj_kernel(yr_ref, yd_ref, x_ref, mod_ref, wo_r_ref, wo_d_ref, g2_ref, rw_hi_ref,
                    rw_lo_ref, rb_ref, x1_out, hp_out, idx_out, w_out, rank_out, cnt_out,
                    carry_ref, *, tm):
    t = pl.program_id(0)

    @pl.when(t == 0)
    def _():
        carry_ref[...] = jnp.zeros_like(carry_ref)

    mix = _dot(yr_ref[...], wo_r_ref[...]) + _dot(yd_ref[...], wo_d_ref[...])
    gate1 = mod_ref[0, 2:3, :]
    shift2 = mod_ref[0, 3:4, :]
    scale2 = mod_ref[0, 4:5, :]
    x1 = x_ref[...] + gate1 * mix
    x1_out[...] = x1
    ms = jnp.mean(x1 * x1, axis=-1, keepdims=True)
    h = x1 * lax.rsqrt(ms + NORM_EPS) * g2_ref[...] * (1.0 + scale2) + shift2
    _store_rows(hp_out, _pack_bf16_pairs(h))

    hi = h.astype(BF16)
    lo = (h - hi.astype(F32)).astype(BF16)
    logits = (_dot(hi, rw_hi_ref[...]) + _dot(hi, rw_lo_ref[...]) + _dot(lo, rw_hi_ref[...])
              + rb_ref[...])

    eidx = lax.broadcasted_iota(I32, logits.shape, 1)
    col4 = lax.broadcasted_iota(I32, (tm, TOP_K), 1)
    lg = logits
    vals, idxs = [], []
    onehot = jnp.zeros(logits.shape, F32)
    for _ in range(TOP_K):
        m = jnp.max(lg, axis=-1, keepdims=True)
        ix = jnp.min(jnp.where(lg == m, eidx, N_EXPERTS), axis=-1, keepdims=True)
        sel = eidx == ix
        vals.append(m)
        idxs.append(ix)
        onehot = onehot + sel.astype(F32)
        lg = jnp.where(sel, -jnp.inf, lg)
    es = [jnp.exp(v - vals[0]) for v in vals]
    den = es[0] + es[1] + es[2] + es[3]

    tri = (lax.broadcasted_iota(I32, (tm, tm), 0) >
           lax.broadcasted_iota(I32, (tm, tm), 1)).astype(BF16)
    prefix = _dot(tri, onehot.astype(BF16)) + carry_ref[...]
    idx4 = jnp.zeros((tm, TOP_K), I32)
    w4 = jnp.zeros((tm, TOP_K), F32)
    rank4 = jnp.zeros((tm, TOP_K), I32)
    for kk in range(TOP_K):
        rk = jnp.sum(jnp.where(eidx == idxs[kk], prefix, 0.0), axis=-1, keepdims=True)
        idx4 = jnp.where(col4 == kk, idxs[kk], idx4)
        w4 = jnp.where(col4 == kk, es[kk] / den, w4)
        rank4 = jnp.where(col4 == kk, rk.astype(I32), rank4)
    idx_out[...] = idx4
    w_out[...] = w4
    rank_out[...] = rank4
    carry_ref[...] = carry_ref[...] + jnp.sum(onehot, axis=0, keepdims=True)
    cnt_out[...] = carry_ref[...].astype(I32)


def _outproj_call(y_rwkv, y_diff, x, mod6, w_out, norm2_gain, router_w, router_b, S, tm):
    T = x.shape[0]
    tiles_per_seq = S // tm
    rw_hi = router_w.astype(BF16)
    rw_lo = (router_w - rw_hi.astype(F32)).astype(BF16)
    tok = lambda w: pl.BlockSpec((tm, w), lambda t: (t, 0))
    full = lambda shape: pl.BlockSpec(shape, lambda t: (0,) * len(shape))
    return pl.pallas_call(
        functools.partial(_outproj_kernel, tm=tm),
        grid=(T // tm,),
        in_specs=[tok(RWKV_WIDTH), tok(DIFF_WIDTH), tok(D_MODEL),
                  pl.BlockSpec((1, 6, D_MODEL), lambda t: (t // tiles_per_seq, 0, 0)),
                  full((RWKV_WIDTH, D_MODEL)), full((DIFF_WIDTH, D_MODEL)), full((1, D_MODEL)),
                  full((D_MODEL, N_EXPERTS)), full((D_MODEL, N_EXPERTS)), full((1, N_EXPERTS))],
        out_specs=[tok(D_MODEL), pl.BlockSpec((tm * ROW_SUB, 128), lambda t: (t, 0)),
                   tok(TOP_K), tok(TOP_K), tok(TOP_K), full((1, N_EXPERTS))],
        out_shape=[jax.ShapeDtypeStruct((T, D_MODEL), F32),
                   jax.ShapeDtypeStruct((T * ROW_SUB, 128), U32),
                   jax.ShapeDtypeStruct((T, TOP_K), I32),
                   jax.ShapeDtypeStruct((T, TOP_K), F32),
                   jax.ShapeDtypeStruct((T, TOP_K), I32),
                   jax.ShapeDtypeStruct((1, N_EXPERTS), I32)],
        scratch_shapes=[pltpu.VMEM((1, N_EXPERTS), F32)],
        compiler_params=pltpu.CompilerParams(
            dimension_semantics=("arbitrary",), vmem_limit_bytes=VMEM_LIMIT),
        name="outproj",
    )(y_rwkv, y_diff, x, mod6, w_out[:RWKV_WIDTH].astype(BF16), w_out[RWKV_WIDTH:].astype(BF16),
      norm2_gain.reshape(1, -1), rw_hi, rw_lo, router_b.reshape(1, -1))


def _experts_kernel(be_ref, nu_ref, valid_ref, xs_ref, wgu_ref, bgu_ref, wd_ref, bd_ref, ys_ref,
                    wgu_bf, wd_bf):
    i = pl.program_id(0)

    @pl.when((i == 0) | (be_ref[i] != be_ref[jnp.maximum(i - 1, 0)]))
    def _():
        wgu_bf[...] = wgu_ref[0].astype(BF16)
        wd_bf[...] = wd_ref[0].astype(BF16)

    @pl.when(i < nu_ref[0])
    def _():
        packed = _load_rows(xs_ref)
        row = lax.broadcasted_iota(I32, packed.shape, 0)
        packed = jnp.where(row < valid_ref[i], packed, jnp.uint32(0))
        xa, xb = _unpack_bf16_pairs(packed)
        x = jnp.concatenate([xa.astype(BF16), xb.astype(BF16)], axis=1)
        gu = _dot(x, wgu_bf[...]) + bgu_ref[0]
        gate = jnp.minimum(gu[:, :D_EXPERT], SWIGLU_LIMIT)
        up = jnp.clip(gu[:, D_EXPERT:], -SWIGLU_LIMIT, SWIGLU_LIMIT)
        act = (up + 1.0) * (gate * jax.nn.sigmoid(SWIGLU_ALPHA * gate))
        y = _dot(act.astype(BF16), wd_bf[...]) + bd_ref[0]
        _store_rows(ys_ref, _pack_bf16_pairs(y))

    @pl.when(i >= nu_ref[0])
    def _():
        ys_ref[...] = jnp.zeros_like(ys_ref)


def _experts_call(xs, block_e, n_used, valid_rows, w_gate_up, b_gate_up, w_down, b_down, n_blocks):
    per_expert = lambda shape: pl.BlockSpec(shape, lambda i, be, nu, vr: (be[i], 0, 0))
    grid_spec = pltpu.PrefetchScalarGridSpec(
        num_scalar_prefetch=3,
        grid=(n_blocks,),
        in_specs=[pl.BlockSpec((MOE_BLOCK * ROW_SUB, 128),
                               lambda i, be, nu, vr: (jnp.minimum(i, nu[0] - 1), 0)),
                  per_expert((1, D_MODEL, 2 * D_EXPERT)), per_expert((1, 1, 2 * D_EXPERT)),
                  per_expert((1, D_EXPERT, D_MODEL)), per_expert((1, 1, D_MODEL))],
        out_specs=pl.BlockSpec((MOE_BLOCK * ROW_SUB, 128), lambda i, be, nu, vr: (i, 0)),
        scratch_shapes=[pltpu.VMEM((D_MODEL, 2 * D_EXPERT), BF16),
                        pltpu.VMEM((D_EXPERT, D_MODEL), BF16)])
    return pl.pallas_call(
        _experts_kernel,
        grid_spec=grid_spec,
        out_shape=jax.ShapeDtypeStruct((n_blocks * MOE_BLOCK * ROW_SUB, 128), U32),
        compiler_params=pltpu.CompilerParams(
            dimension_semantics=("arbitrary",), vmem_limit_bytes=VMEM_LIMIT),
        name="experts",
    )(block_e, n_used, valid_rows, xs, w_gate_up, b_gate_up.reshape(N_EXPERTS, 1, -1), w_down,
      b_down.reshape(N_EXPERTS, 1, -1))


SC_WINDOW = 128


def _sc_mesh():
    return plsc.VectorSubcoreMesh(core_axis_name="c", subcore_axis_name="s")


def _sc_worker_chunks(n_chunks):
    info = plsc.get_sparse_core_info()
    n_workers = info.num_cores * info.num_subcores
    assert n_chunks % n_workers == 0
    return info.num_cores, n_chunks // n_workers


def _sc_scatter_call(rows, slot_chunks, n_out_rows):
    T = rows.shape[0]
    n_chunks = slot_chunks.shape[0]
    per_pass = T // SC_WINDOW
    n_cores, per_worker = _sc_worker_chunks(n_chunks)

    @functools.partial(
        pl.kernel, mesh=_sc_mesh(),
        out_type=jax.ShapeDtypeStruct((n_out_rows, ROW_SUB, 128), U32),
        scratch_types=[pltpu.VMEM((SC_WINDOW,), I32), pltpu.VMEM((SC_WINDOW, ROW_SUB, 128), U32)],
        name="sc_dispatch")
    def run(rows_hbm, idx_hbm, out_hbm, idx_v, rows_v):
        wid = lax.axis_index("s") * n_cores + lax.axis_index("c")

        @pl.loop(0, per_worker)
        def _(j):
            chunk = wid * per_worker + j
            src = lax.rem(chunk, per_pass) * SC_WINDOW
            pltpu.sync_copy(idx_hbm.at[chunk], idx_v)
            pltpu.sync_copy(rows_hbm.at[pl.ds(src, SC_WINDOW)], rows_v)
            pltpu.sync_copy(rows_v, out_hbm.at[idx_v])

    return run(rows, slot_chunks)


def _sc_gather_call(table, slot_chunks):
    n_chunks = slot_chunks.shape[0]
    n_cores, per_worker = _sc_worker_chunks(n_chunks)

    @functools.partial(
        pl.kernel, mesh=_sc_mesh(),
        out_type=jax.ShapeDtypeStruct((n_chunks * SC_WINDOW, ROW_SUB, 128), U32),
        scratch_types=[pltpu.VMEM((SC_WINDOW,), I32), pltpu.VMEM((SC_WINDOW, ROW_SUB, 128), U32)],
        name="sc_collect")
    def run(table_hbm, idx_hbm, out_hbm, idx_v, rows_v):
        wid = lax.axis_index("s") * n_cores + lax.axis_index("c")

        @pl.loop(0, per_worker)
        def _(j):
            chunk = wid * per_worker + j
            pltpu.sync_copy(idx_hbm.at[chunk], idx_v)
            pltpu.sync_copy(table_hbm.at[idx_v], rows_v)
            pltpu.sync_copy(rows_v, out_hbm.at[pl.ds(chunk * SC_WINDOW, SC_WINDOW)])

    return run(table, slot_chunks)


def _combine_dense_kernel(g0_ref, g1_ref, g2_ref, g3_ref, x1_ref, w_ref, mod_ref, fg_ref, o_ref,
                          *, tm):
    w = w_ref[...]
    acc_lo = jnp.zeros((tm, D_MODEL // 2), F32)
    acc_hi = jnp.zeros((tm, D_MODEL // 2), F32)
    for kk, g_ref in enumerate((g0_ref, g1_ref, g2_ref, g3_ref)):
        lo, hi = _unpack_bf16_pairs(_load_rows(g_ref))
        wk = w[:, kk:kk + 1]
        acc_lo = acc_lo + wk * lo
        acc_hi = acc_hi + wk * hi
    moe = jnp.concatenate([acc_lo, acc_hi], axis=1)
    gate2 = mod_ref[0, 5:6, :]
    x2 = x1_ref[...] + gate2 * moe
    ms = jnp.mean(x2 * x2, axis=-1, keepdims=True)
    o_ref[...] = x2 * lax.rsqrt(ms + NORM_EPS) * fg_ref[...]


def _combine_dense_call(gathered, x1, top_w, mod6, final_gain, S, tm):
    T = x1.shape[0]
    nt = T // tm
    tiles_per_seq = S // tm
    rows = lambda kk: pl.BlockSpec((tm * ROW_SUB, 128), lambda t: (kk * nt + t, 0))
    return pl.pallas_call(
        functools.partial(_combine_dense_kernel, tm=tm),
        grid=(nt,),
        in_specs=[rows(0), rows(1), rows(2), rows(3),
                  pl.BlockSpec((tm, D_MODEL), lambda t: (t, 0)),
                  pl.BlockSpec((tm, TOP_K), lambda t: (t, 0)),
                  pl.BlockSpec((1, 6, D_MODEL), lambda t: (t // tiles_per_seq, 0, 0)),
                  pl.BlockSpec((1, D_MODEL), lambda t: (0, 0))],
        out_specs=pl.BlockSpec((tm, D_MODEL), lambda t: (t, 0)),
        out_shape=jax.ShapeDtypeStruct((T, D_MODEL), F32),
        compiler_params=pltpu.CompilerParams(
            dimension_semantics=("arbitrary",), vmem_limit_bytes=VMEM_LIMIT),
        name="combine",
    )(gathered, gathered, gathered, gathered, x1, top_w, mod6, final_gain.reshape(1, -1))


def _forward(x, c, mod_w, mod_b, norm1_gain, w_in, rwkv_shift_mu, rwkv_w0, rwkv_w_up, rwkv_a0,
             rwkv_a_up, rwkv_g_up, rwkv_k_k, rwkv_k_a, rwkv_r_k, rwkv_gn_gain, rwkv_gn_bias,
             diff_lambda_q1, diff_lambda_k1, diff_lambda_q2, diff_lambda_k2, diff_subln_gain,
             w_out, norm2_gain, router_w, router_b, w_gate_up, b_gate_up, w_down, b_down,
             final_gain):
    B, S, D = x.shape
    T = B * S
    tm_in = min(512, S)
    tq = min(512, S)
    tm = min(256, S)

    mod6 = _mod_call(c, mod_w, mod_b).reshape(B, 6, D)
    (r, k, v, a, b, g, lw, q, kat, vt) = _inproj_call(
        x, mod6, norm1_gain, w_in, rwkv_shift_mu, rwkv_w0, rwkv_w_up, rwkv_a0, rwkv_a_up,
        rwkv_g_up, rwkv_k_k, rwkv_k_a, tm_in)
    y_rwkv = _rwkv_call(r, k, v, a, b, g, lw, rwkv_r_k, rwkv_gn_gain, rwkv_gn_bias,
                        min(4, S // CHUNK))
    y_diff = _attn_call(q, kat, vt, diff_lambda_q1, diff_lambda_k1, diff_lambda_q2,
                        diff_lambda_k2, diff_subln_gain, tq)

    x1, hp, top_idx, top_w, rank, counts = _outproj_call(
        y_rwkv.reshape(T, -1), y_diff.reshape(T, -1), x.reshape(T, D), mod6, w_out, norm2_gain,
        router_w, router_b, S, tm)

    counts = counts.reshape(N_EXPERTS)
    padded = ((counts + MOE_BLOCK - 1) // MOE_BLOCK) * MOE_BLOCK
    pad_ends = jnp.cumsum(padded)
    pad_starts = pad_ends - padded
    n_blocks = -(-(T * TOP_K + N_EXPERTS * (MOE_BLOCK - 1)) // MOE_BLOCK)
    n_used = (pad_ends[-1] // MOE_BLOCK).astype(I32).reshape(1)
    block_start = jnp.minimum(jnp.arange(n_blocks, dtype=I32), n_used[0] - 1) * MOE_BLOCK
    block_e = jnp.minimum(jnp.sum(pad_ends[None, :] <= block_start[:, None], axis=1),
                          N_EXPERTS - 1).astype(I32)
    expert_ids = jnp.arange(N_EXPERTS, dtype=I32)
    start_of = jnp.sum(jnp.where(top_idx[..., None] == expert_ids, pad_starts.astype(I32), 0), axis=-1)
    slots = start_of + rank
    slot_chunks = slots.T.reshape(TOP_K * T // SC_WINDOW, SC_WINDOW)
    block_row = jnp.arange(n_blocks, dtype=I32) * MOE_BLOCK
    valid_rows = jnp.clip(counts.astype(I32)[block_e] - (block_row - pad_starts.astype(I32)[block_e]),
                          0, MOE_BLOCK).astype(I32)

    as_tokens = lambda z: z.reshape(-1, ROW_SUB, 128)
    as_lines = lambda z: z.reshape(-1, 128)
    xs = as_lines(_sc_scatter_call(as_tokens(hp), slot_chunks, n_blocks * MOE_BLOCK))
    ys = _experts_call(xs, block_e, n_used, valid_rows, w_gate_up, b_gate_up, w_down, b_down,
                       n_blocks)
    gathered = as_lines(_sc_gather_call(as_tokens(ys), slot_chunks))
    out = _combine_dense_call(gathered, x1, top_w, mod6, final_gain, S, tm)
    return out.reshape(B, S, D)


def kernel(x, c, mod_w, mod_b, norm1_gain, w_in, rwkv_shift_mu, rwkv_w0, rwkv_w_up, rwkv_a0, rwkv_a_up, rwkv_g_up, rwkv_k_k, rwkv_k_a, rwkv_r_k, rwkv_gn_gain, rwkv_gn_bias, diff_lambda_q1, diff_lambda_k1, diff_lambda_q2, diff_lambda_k2, diff_subln_gain, w_out, norm2_gain, router_w, router_b, w_gate_up, b_gate_up, w_down, b_down, final_gain):
    return _forward(x, c, mod_w[0], mod_b[0], norm1_gain[0], w_in[0], rwkv_shift_mu[0], rwkv_w0[0],
                    rwkv_w_up[0], rwkv_a0[0], rwkv_a_up[0], rwkv_g_up[0], rwkv_k_k[0], rwkv_k_a[0],
                    rwkv_r_k[0].reshape(-1), rwkv_gn_gain[0], rwkv_gn_bias[0], diff_lambda_q1[0],
                    diff_lambda_k1[0], diff_lambda_q2[0], diff_lambda_k2[0], diff_subln_gain[0],
                    w_out[0], norm2_gain[0], router_w[0], router_b[0], w_gate_up[0], b_gate_up[0],
                    w_down[0], b_down[0], final_gain)
```

```python
import functools
import math

import jax
import jax.numpy as jnp
from jax import lax
from jax.experimental import pallas as pl
from jax.experimental.pallas import tpu as pltpu
from jax.experimental.pallas import tpu_sc as plsc

F32 = jnp.float32
BF16 = jnp.bfloat16
I32 = jnp.int32
U32 = jnp.uint32

D_MODEL = 1024
RWKV_WIDTH = 512
RWKV_HEADS = 8
HEAD_DIM = 64
RWKV_COLS = 3 * RWKV_WIDTH + 64 + 64 + 128
DIFF_WIDTH = 512
DIFF_HEADS = 4
DIFF_COLS = 3 * DIFF_WIDTH
N_EXPERTS = 32
TOP_K = 4
D_EXPERT = 1024
SWIGLU_LIMIT = 7.0
SWIGLU_ALPHA = 1.702
MOE_BLOCK = 512
NORM_EPS = 1e-5
GN_EPS = 64e-5
LAMBDA_INIT = 0.8 - 0.6 * math.exp(-0.3 * 0)

CHUNK = 64
GROUP = 4
GW = GROUP * HEAD_DIM
VT_ROWS = 128 + 16
VMEM_LIMIT = 56 * 1024 * 1024


def _dot(a, b):
    return jnp.dot(a, b, preferred_element_type=F32)


def _dot_nt(a, b):
    return lax.dot_general(a, b, (((1,), (1,)), ((), ())), preferred_element_type=F32)


def _dot_tn(a, b):
    return lax.dot_general(a, b, (((0,), (0,)), ((), ())), preferred_element_type=F32)


def _split_dot(x, w_bf16):
    hi = x.astype(BF16)
    lo = (x - hi.astype(F32)).astype(BF16)
    return _dot(hi, w_bf16) + _dot(lo, w_bf16)


def _mod_kernel(c_ref, w_ref, b_ref, o_ref):
    c = c_ref[...]
    s = c * jax.nn.sigmoid(c)
    o_ref[...] = _dot(s, w_ref[...]) + b_ref[...]


def _mod_call(c, mod_w, mod_b):
    B = c.shape[0]
    n = mod_w.shape[1]
    tn = 1536
    return pl.pallas_call(
        _mod_kernel,
        grid=(n // tn,),
        in_specs=[pl.BlockSpec((B, D_MODEL), lambda j: (0, 0)),
                  pl.BlockSpec((D_MODEL, tn), lambda j: (0, j)),
                  pl.BlockSpec((1, tn), lambda j: (0, j))],
        out_specs=pl.BlockSpec((B, tn), lambda j: (0, j)),
        out_shape=jax.ShapeDtypeStruct((B, n), F32),
        compiler_params=pltpu.CompilerParams(
            dimension_semantics=("arbitrary",), vmem_limit_bytes=VMEM_LIMIT),
        name="mod",
    )(c, mod_w, mod_b.reshape(1, n))


def _inproj_kernel(x_ref, mod_ref, g1_ref, wrw_ref, wat_ref, mu_ref, w0_ref, wup_ref,
                   a0_ref, aup_ref, gup_ref, kk_ref, ka_ref, hsum_ref,
                   r_out, k_out, v_out, a_out, b_out, g_out, lw_out,
                   q_out, kat_out, vt_out, carry_ref, *, tm):
    s = pl.program_id(1)
    x = x_ref[0]
    ms = jnp.mean(x * x, axis=-1, keepdims=True)
    shift1 = mod_ref[0, 0:1, :]
    scale1 = mod_ref[0, 1:2, :]
    h = x * lax.rsqrt(ms + NORM_EPS) * g1_ref[...] * (1.0 + scale1) + shift1
    hb = h.astype(BF16)

    p = _dot(hb, wrw_ref[...])
    pa = _dot(hb, wat_ref[...])

    @pl.when(s == 0)
    def _():
        carry_ref[...] = jnp.zeros_like(carry_ref)

    rolled = pltpu.roll(p, shift=1, axis=0)
    row = lax.broadcasted_iota(I32, p.shape, 0)
    prev = jnp.where(row == 0, carry_ref[...], rolled)
    carry_ref[...] = p[tm - 1:tm, :]
    ps = p + mu_ref[...] * (prev - p)

    r = ps[:, 0:512]
    k = ps[:, 512:1024]
    v = ps[:, 1024:1536]
    lo2 = ps[:, 1536:1664]
    g_lo = ps[:, 1664:1792]
    z = w0_ref[...] + _dot(jnp.tanh(lo2).astype(BF16), wup_ref[...])
    nz = -z
    softplus = jnp.maximum(nz, 0.0) + jnp.log(1.0 + jnp.exp(-jnp.abs(nz)))
    w = -softplus - 0.5
    lw_out[0] = -jnp.exp(w)
    a = jax.nn.sigmoid(a0_ref[...] + _dot(lo2.astype(BF16), aup_ref[...]))
    g = _dot(jax.nn.sigmoid(g_lo).astype(BF16), gup_ref[...])
    kk = k * kk_ref[...]
    ssq = _split_dot(kk * kk, hsum_ref[...])
    kk = kk / jnp.maximum(jnp.sqrt(ssq), 1e-12)
    k = k * (1.0 + (a - 1.0) * ka_ref[...])
    r_out[0] = r.astype(BF16)
    k_out[0] = k.astype(BF16)
    v_out[0] = v.astype(BF16)
    a_out[0] = (-kk).astype(BF16)
    b_out[0] = (kk * a).astype(BF16)
    g_out[0] = g.astype(BF16)

    q_out[0] = (pa[:, 0:512] * (1.0 / math.sqrt(HEAD_DIM))).astype(BF16)
    kat_out[0] = pa[:, 512:1024].astype(BF16)
    for hh in range(DIFF_HEADS):
        vh = pa[:, 1024 + hh * 128:1024 + (hh + 1) * 128]
        vt_out[0, hh, 0, 0:128, :] = vh.T.astype(BF16)
        vt_out[0, hh, 0, 128:VT_ROWS, :] = jnp.ones((VT_ROWS - 128, tm), BF16)


def _head_sum_matrix(width):
    i = jnp.arange(width) // HEAD_DIM
    return (i[:, None] == i[None, :]).astype(BF16)


def _inproj_call(x, mod6, norm1_gain, w_in, mu, w0, w_up, a0, a_up, g_up, k_k, k_a, tm):
    B, S, _ = x.shape
    ns = S // tm
    w_rw = w_in[:, :RWKV_COLS].astype(BF16)
    w_at = w_in[:, RWKV_COLS:].astype(BF16)
    zeros = jnp.zeros((64, RWKV_WIDTH), F32)
    wup_p = jnp.concatenate([w_up, zeros], axis=0).astype(BF16)
    aup_p = jnp.concatenate([zeros, a_up], axis=0).astype(BF16)
    row = lambda v: v.reshape(1, -1)
    full = lambda shape: pl.BlockSpec(shape, lambda b, s: (0,) * len(shape))
    tok = lambda w: pl.BlockSpec((1, tm, w), lambda b, s: (b, s, 0))
    rw_shape = jax.ShapeDtypeStruct((B, S, RWKV_WIDTH), BF16)
    out_shape = [rw_shape] * 6 + [
        jax.ShapeDtypeStruct((B, S, RWKV_WIDTH), F32),
        jax.ShapeDtypeStruct((B, S, DIFF_WIDTH), BF16),
        jax.ShapeDtypeStruct((B, S, DIFF_WIDTH), BF16),
        jax.ShapeDtypeStruct((B, DIFF_HEADS, ns, VT_ROWS, tm), BF16)]
    out_specs = [tok(RWKV_WIDTH)] * 7 + [tok(DIFF_WIDTH)] * 2 + [
        pl.BlockSpec((1, DIFF_HEADS, 1, VT_ROWS, tm), lambda b, s: (b, 0, s, 0, 0))]
    return pl.pallas_call(
        functools.partial(_inproj_kernel, tm=tm),
        grid=(B, ns),
        in_specs=[tok(D_MODEL),
                  pl.BlockSpec((1, 6, D_MODEL), lambda b, s: (b, 0, 0)),
                  full((1, D_MODEL)),
                  full((D_MODEL, RWKV_COLS)), full((D_MODEL, DIFF_COLS)),
                  full((1, RWKV_COLS)), full((1, RWKV_WIDTH)), full((128, RWKV_WIDTH)),
                  full((1, RWKV_WIDTH)), full((128, RWKV_WIDTH)), full((128, RWKV_WIDTH)),
                  full((1, RWKV_WIDTH)), full((1, RWKV_WIDTH)),
                  full((RWKV_WIDTH, RWKV_WIDTH))],
        out_specs=out_specs,
        out_shape=out_shape,
        scratch_shapes=[pltpu.VMEM((1, RWKV_COLS), F32)],
        compiler_params=pltpu.CompilerParams(
            dimension_semantics=("arbitrary", "arbitrary"), vmem_limit_bytes=VMEM_LIMIT),
        name="inproj",
    )(x, mod6, row(norm1_gain), w_rw, w_at, row(mu), row(w0), wup_p, row(a0), aup_p,
      g_up.astype(BF16), row(k_k), row(k_a), _head_sum_matrix(RWKV_WIDTH))


def _rwkv_kernel(r_ref, k_ref, v_ref, a_ref, b_ref, g_ref, lw_ref, rk_ref, gng_ref, gnb_ref,
                 hsum_ref, y_ref, s0_ref, s1_ref, *, nck):
    C = CHUNK
    n_grp = RWKV_HEADS // GROUP
    s_refs = (s0_ref, s1_ref)

    @pl.when(pl.program_id(1) == 0)
    def _():
        s0_ref[...] = jnp.zeros_like(s0_ref)
        s1_ref[...] = jnp.zeros_like(s1_ref)

    ti = lax.broadcasted_iota(I32, (C, GW), 0)
    si = lax.broadcasted_iota(I32, (C, GW), 1) % C
    incl = si <= ti
    strict = si < ti
    eye = (si == ti).astype(F32)
    bdmask = (lax.broadcasted_iota(I32, (GW, GW), 0) // HEAD_DIM ==
              lax.broadcasted_iota(I32, (GW, GW), 1) // HEAD_DIM)
    hsum = hsum_ref[...]
    chains = [(ck, g) for ck in range(nck) for g in range(n_grp)]

    bdmask_bf = bdmask.astype(BF16)

    def bd(xb):
        return jnp.concatenate([xb] * GROUP, axis=0) * bdmask_bf

    tri =(lax.broadcasted_iota(I32, (C, C), 0) >=
           lax.broadcasted_iota(I32, (C, C), 1)).astype(BF16)
    done = {}
    sv = [s_refs[g][...] for g in range(n_grp)]

    def chain(ch):
        ck, g = ch
        rows = slice(ck * C, (ck + 1) * C)
        cols = slice(g * GW, (g + 1) * GW)
        lw = lw_ref[0, rows, cols]
        L = _split_dot_left(tri, lw)
        yield
        Lx = L - lw
        Lc = L[C - 1:C, :]
        rho = L[C // 2 - 1:C // 2, :]
        r = r_ref[0, rows, cols].astype(F32)
        k = k_ref[0, rows, cols].astype(F32)
        vb = v_ref[0, rows, cols]
        a = a_ref[0, rows, cols].astype(F32)
        b = b_ref[0, rows, cols].astype(F32)
        e_k = jnp.exp(rho - L)
        lhs = jnp.concatenate([(r * jnp.exp(L - rho)).astype(BF16),
                               (a * jnp.exp(Lx - rho)).astype(BF16)], axis=0)
        kbd = bd((k * e_k).astype(BF16))
        bbd = bd((b * e_k).astype(BF16))
        r0 = (r * jnp.exp(L)).astype(BF16)
        a0bd = bd((a * jnp.exp(Lx)).astype(BF16))
        e_o = jnp.exp(Lc - L)
        bk = jnp.concatenate([(b * e_o).astype(BF16), (k * e_o).astype(BF16)], axis=0)
        vbd = bd(vb)
        yield
        ak = _dot_nt(lhs, kbd)
        ab = _dot_nt(lhs, bbd)
        yield
        a_rk = jnp.where(incl, ak[:C], 0.0).astype(BF16)
        a_ak = jnp.where(strict, ak[C:], 0.0).astype(BF16)
        a_rb = jnp.where(incl, ab[:C], 0.0).astype(BF16)
        n = jnp.where(strict, ab[C:], 0.0)
        p = eye + n
        xb = n.astype(BF16)
        xbd = bd(xb)
        yield
        x = _dot(xb, xbd)
        yield
        for _ in range(4):
            xb = x.astype(BF16)
            stack = jnp.concatenate([xb, p.astype(BF16)], axis=0)
            xbd = bd(xb)
            yield
            o = _dot(stack, xbd)
            yield
            x = o[:C]
            p = p + o[C:]
        pb = p.astype(BF16)
        xbd = bd(x.astype(BF16))
        yield
        o = _dot(pb, xbd)
        yield
        tb = (p + o).astype(BF16)
        yield
        av = _dot(jnp.concatenate([a_ak, a_rk], axis=0), vbd)
        w1 = _dot(tb, a0bd)
        yield
        akvbd = bd(av[:C].astype(BF16))
        rw_lhs = jnp.concatenate([r0, w1.astype(BF16)], axis=0)
        yield
        w2 = _dot(tb, akvbd)
        yield
        done[ch] = dict(rw_lhs=rw_lhs, w2=w2, a_rb=a_rb, arkv=av[C:], vb=vb, bk=bk,
                        gc=jnp.exp(Lc), rk=r * k)

    def sequential(ck):
        d = [done[(ck, g)] for g in range(n_grp)]
        rws = [_dot_nt(d[g]["rw_lhs"], sv[g].astype(BF16)) for g in range(n_grp)]
        yield
        ub = [(rws[g][C:] + d[g]["w2"]).astype(BF16) for g in range(n_grp)]
        uv = [jnp.concatenate([ub[g], d[g]["vb"]], axis=0) for g in range(n_grp)]
        ubd = [bd(ub[g]) for g in range(n_grp)]
        yield
        upd = [_dot_tn(uv[g], d[g]["bk"]) for g in range(n_grp)]
        yield
        for g in range(n_grp):
            sv[g] = sv[g] * d[g]["gc"] + jnp.where(bdmask, upd[g], 0.0)
        done[("state", ck)] = True
        y = [rws[g][:C] + _dot(d[g]["a_rb"], ubd[g]) + d[g]["arkv"] for g in range(n_grp)]
        yield
        mean = [_split_dot(y[g], hsum) * (1.0 / HEAD_DIM) for g in range(n_grp)]
        yield
        dev = [y[g] - mean[g] for g in range(n_grp)]
        var = [_split_dot(dev[g] * dev[g], hsum) * (1.0 / HEAD_DIM) for g in range(n_grp)]
        bonus = [_split_dot(d[g]["rk"] * rk_ref[:, g * GW:(g + 1) * GW], hsum) for g in range(n_grp)]
        yield
        rows = slice(ck * C, (ck + 1) * C)
        for g in range(n_grp):
            cols = slice(g * GW, (g + 1) * GW)
            yn = dev[g] * lax.rsqrt(var[g] + GN_EPS) * gng_ref[:, cols] + gnb_ref[:, cols]
            out = (yn + bonus[g] * d[g]["vb"].astype(F32)) * g_ref[0, rows, cols].astype(F32)
            y_ref[0, rows, cols] = out.astype(BF16)

    waiting = [chain(ch) for ch in chains]
    seq_next = 0
    running = []
    while waiting or running or seq_next < nck:
        ready = (seq_next < nck and all((seq_next, g) in done for g in range(n_grp))
                 and (seq_next == 0 or ("state", seq_next - 1) in done))
        if ready:
            running.append(sequential(seq_next))
            seq_next += 1
        elif waiting:
            running.append(waiting.pop(0))
        for gen in list(running):
            try:
                next(gen)
            except StopIteration:
                running.remove(gen)
    for g in range(n_grp):
        s_refs[g][...] = sv[g]


def _split_dot_left(w_bf16, x):
    hi = x.astype(BF16)
    lo = (x - hi.astype(F32)).astype(BF16)
    return _dot(w_bf16, hi) + _dot(w_bf16, lo)


def _rwkv_call(r, k, v, a, b, g, lw, r_k, gn_gain, gn_bias, nck):
    B, S, _ = r.shape
    tb = nck * CHUNK
    tok = pl.BlockSpec((1, tb, RWKV_WIDTH), lambda bb, c: (bb, c, 0))
    par = pl.BlockSpec((1, RWKV_WIDTH), lambda bb, c: (0, 0))
    return pl.pallas_call(
        functools.partial(_rwkv_kernel, nck=nck),
        grid=(B, S // tb),
        in_specs=[tok] * 7 + [par] * 3 + [pl.BlockSpec((GW, GW), lambda bb, c: (0, 0))],
        out_specs=tok,
        out_shape=jax.ShapeDtypeStruct((B, S, RWKV_WIDTH), BF16),
        scratch_shapes=[pltpu.VMEM((GW, GW), F32)] * (RWKV_HEADS // GROUP),
        compiler_params=pltpu.CompilerParams(
            dimension_semantics=("arbitrary", "arbitrary"), vmem_limit_bytes=VMEM_LIMIT),
        name="rwkv",
    )(r, k, v, a, b, g, lw, r_k.reshape(1, -1), gn_gain.reshape(1, -1), gn_bias.reshape(1, -1),
      _head_sum_matrix(GW))


def _tree_reduce(op, x):
    while x.shape[0] > 8:
        h = x.shape[0] // 2
        x = op(x[:h], x[h:])
    return x


def _attn_kernel(slopes_ref, q_ref, k_ref, vt_ref, lq1_ref, lk1_ref, lq2_ref, lk2_ref,
                 gain_ref, o_ref, sa_ref, sb_ref, acc_ref, m_ref, *, tq, tk):
    hh = pl.program_id(1)
    i = pl.program_id(2)
    slope = slopes_ref[hh]

    q = q_ref[0].astype(F32)
    lane = lax.broadcasted_iota(I32, (tq, 256), 1)
    qrow = lax.broadcasted_iota(I32, (tq, 256), 0)
    qa = (qrow >> 5).astype(F32) * (32.0 * slope)
    qb = (qrow & 31).astype(F32) * slope
    qbias = jnp.where(lane < 130, 1.0, jnp.where(lane == 130, -qa, jnp.where(lane == 131, -qb, 0.0)))
    qpad = jnp.concatenate([q, jnp.zeros_like(q)], axis=1)
    q_aug = []
    for c in range(2):
        in_comp = (lane >= c * HEAD_DIM) & (lane < (c + 1) * HEAD_DIM)
        q_aug.append(jnp.where(in_comp, qpad, jnp.where(lane >= 128, qbias, 0.0)))
    klane = lax.broadcasted_iota(I32, (tk, 128), 1)
    krow = lax.broadcasted_iota(I32, (tk, 128), 0)
    ka = (krow >> 5).astype(F32) * (32.0 * slope)
    kb_ = (krow & 31).astype(F32) * slope
    kbias = jnp.where(klane == 0, ka, jnp.where(klane == 1, kb_, jnp.where(klane < 4, 1.0, 0.0))
                      ).astype(BF16)

    qt_both = jnp.concatenate([q_aug[0].T, q_aug[1].T], axis=1).astype(BF16)

    m_ref[...] = jnp.full_like(m_ref, -jnp.inf)
    acc_ref[...] = jnp.zeros_like(acc_ref)

    def scores(j):
        kb = k_ref[0, pl.ds(pl.multiple_of(j * tk, tk), tk), :]
        return _dot(jnp.concatenate([kb, kbias], axis=1), qt_both)

    def absorb(s_ref, j, masked):
        s = s_ref[...]
        off = j * tk - i * tq
        cj = slope * off.astype(F32)
        if masked:
            keep = (lax.broadcasted_iota(I32, (tk, tq), 0) -
                    lax.broadcasted_iota(I32, (tk, tq), 1) + off) <= 0
            s = jnp.where(jnp.concatenate([keep, keep], axis=1), s, -jnp.inf)
        m_loc = jnp.max(_tree_reduce(jnp.maximum, s), axis=0, keepdims=True)
        m_old = m_ref[...]
        m_new = jnp.maximum(m_old, m_loc + cj)
        pr = jnp.exp((s - (m_new - cj)).astype(BF16))
        alpha = jnp.exp(m_old - m_new)
        acc_ref[...] = alpha * acc_ref[...] + _dot(vt_ref[0, 0, j], pr)
        m_ref[...] = m_new

    n_full = (i * tq) // tk
    sa_ref[...] = scores(0)

    def pair_body(jj, carry):
        j0 = 2 * jj
        sb_ref[...] = scores(j0 + 1)
        absorb(sa_ref, j0, False)
        sa_ref[...] = scores(j0 + 2)
        absorb(sb_ref, j0 + 1, False)
        return carry

    lax.fori_loop(0, n_full // 2, pair_body, 0)

    @pl.when(n_full % 2 == 1)
    def _():
        sb_ref[...] = scores(n_full)
        absorb(sa_ref, n_full - 1, False)
        absorb(sb_ref, n_full, True)

    @pl.when(n_full % 2 == 0)
    def _():
        absorb(sa_ref, n_full, True)

    lam = (jnp.exp(jnp.sum(lq1_ref[...] * lk1_ref[...], axis=-1, keepdims=True))
           - jnp.exp(jnp.sum(lq2_ref[...] * lk2_ref[...], axis=-1, keepdims=True))
           + LAMBDA_INIT)
    o2 = acc_ref[0:128, :] / acc_ref[128:129, :]
    o = o2[:, :tq] - lam * o2[:, tq:]
    ot = o.T
    ms = jnp.mean(ot * ot, axis=-1, keepdims=True)
    y = ot * lax.rsqrt(ms + NORM_EPS) * gain_ref[...] * (1.0 - LAMBDA_INIT)
    o_ref[0] = y.astype(BF16)


def _attn_call(q, k, vt, lq1, lk1, lq2, lk2, subln_gain, tq):
    B, S, _ = q.shape
    ns, tk = vt.shape[2], vt.shape[4]
    slopes = jnp.asarray([2.0 ** (-8.0 * (i + 1) / DIFF_HEADS) for i in range(DIFF_HEADS)], F32)
    vec = lambda n: pl.BlockSpec((1, n), lambda b, h, i, sl: (0, 0))
    grid_spec = pltpu.PrefetchScalarGridSpec(
        num_scalar_prefetch=1,
        grid=(B, DIFF_HEADS, S // tq),
        in_specs=[pl.BlockSpec((1, tq, 128), lambda b, h, i, sl: (b, i, h)),
                  pl.BlockSpec((1, S, 128), lambda b, h, i, sl: (b, 0, h)),
                  pl.BlockSpec((1, 1, ns, VT_ROWS, tk), lambda b, h, i, sl: (b, h, 0, 0, 0)),
                  vec(HEAD_DIM), vec(HEAD_DIM), vec(HEAD_DIM), vec(HEAD_DIM), vec(128)],
        out_specs=pl.BlockSpec((1, tq, 128), lambda b, h, i, sl: (b, i, h)),
        scratch_shapes=[pltpu.VMEM((tk, 2 * tq), F32), pltpu.VMEM((tk, 2 * tq), F32),
                        pltpu.VMEM((VT_ROWS, 2 * tq), F32), pltpu.VMEM((1, 2 * tq), F32)])
    return pl.pallas_call(
        functools.partial(_attn_kernel, tq=tq, tk=tk),
        grid_spec=grid_spec,
        out_shape=jax.ShapeDtypeStruct((B, S, DIFF_WIDTH), BF16),
        compiler_params=pltpu.CompilerParams(
            dimension_semantics=("arbitrary", "arbitrary", "arbitrary"),
            vmem_limit_bytes=VMEM_LIMIT),
        name="attn",
    )(slopes, q, k, vt, lq1.reshape(1, -1), lk1.reshape(1, -1), lq2.reshape(1, -1),
      lk2.reshape(1, -1), subln_gain.reshape(1, -1))


def _pack_bf16_pairs(x):
    w = x.shape[1] // 2
    lo = pltpu.bitcast(x[:, :w].astype(BF16).astype(F32), U32)
    hi = pltpu.bitcast(x[:, w:].astype(BF16).astype(F32), U32)
    return (lo >> 16) | (hi & jnp.uint32(0xFFFF0000))


def _unpack_bf16_pairs(p):
    lo = pltpu.bitcast(p << 16, F32)
    hi = pltpu.bitcast(p & jnp.uint32(0xFFFF0000), F32)
    return lo, hi


ROW_SUB = (D_MODEL // 2) // 128


def _store_rows(ref, x2d):
    n = x2d.shape[0]
    for s in range(ROW_SUB):
        ref[pl.ds(s, n, stride=ROW_SUB), :] = x2d[:, s * 128:(s + 1) * 128]


def _load_rows(ref):
    n = ref.shape[0] // ROW_SUB
    return jnp.concatenate([ref[pl.ds(s, n, stride=ROW_SUB), :] for s in range(ROW_SUB)], axis=1)


def _outproj_kernel(yr_ref, yd_ref, x_ref, mod_ref, wo_r_ref, wo_d_ref, g2_ref, rw_hi_ref,
                    rw_lo_ref, rb_ref, x1_out, hp_out, idx_out, w_out, rank_out, cnt_out,
                    carry_ref, *, tm):
    t = pl.program_id(0)

    @pl.when(t == 0)
    def _():
        carry_ref[...] = jnp.zeros_like(carry_ref)

    mix = _dot(yr_ref[...], wo_r_ref[...]) + _dot(yd_ref[...], wo_d_ref[...])
    gate1 = mod_ref[0, 2:3, :]
    shift2 = mod_ref[0, 3:4, :]
    scale2 = mod_ref[0, 4:5, :]
    x1 = x_ref[...] + gate1 * mix
    x1_out[...] = x1
    ms = jnp.mean(x1 * x1, axis=-1, keepdims=True)
    h = x1 * lax.rsqrt(ms + NORM_EPS) * g2_ref[...] * (1.0 + scale2) + shift2
    _store_rows(hp_out, _pack_bf16_pairs(h))

    hi = h.astype(BF16)
    lo = (h - hi.astype(F32)).astype(BF16)
    logits = (_dot(hi, rw_hi_ref[...]) + _dot(hi, rw_lo_ref[...]) + _dot(lo, rw_hi_ref[...])
              + rb_ref[...])

    eidx = lax.broadcasted_iota(I32, logits.shape, 1)
    col4 = lax.broadcasted_iota(I32, (tm, TOP_K), 1)
    lg = logits
    vals, idxs = [], []
    onehot = jnp.zeros(logits.shape, F32)
    for _ in range(TOP_K):
        m = jnp.max(lg, axis=-1, keepdims=True)
        ix = jnp.min(jnp.where(lg == m, eidx, N_EXPERTS), axis=-1, keepdims=True)
        sel = eidx == ix
        vals.append(m)
        idxs.append(ix)
        onehot = onehot + sel.astype(F32)
        lg = jnp.where(sel, -jnp.inf, lg)
    es = [jnp.exp(v - vals[0]) for v in vals]
    den = es[0] + es[1] + es[2] + es[3]

    tri = (lax.broadcasted_iota(I32, (tm, tm), 0) >
           lax.broadcasted_iota(I32, (tm, tm), 1)).astype(BF16)
    prefix = _dot(tri, onehot.astype(BF16)) + carry_ref[...]
    idx4 = jnp.zeros((tm, TOP_K), I32)
    w4 = jnp.zeros((tm, TOP_K), F32)
    rank4 = jnp.zeros((tm, TOP_K), I32)
    for kk in range(TOP_K):
        rk = jnp.sum(jnp.where(eidx == idxs[kk], prefix, 0.0), axis=-1, keepdims=True)
        idx4 = jnp.where(col4 == kk, idxs[kk], idx4)
        w4 = jnp.where(col4 == kk, es[kk] / den, w4)
        rank4 = jnp.where(col4 == kk, rk.astype(I32), rank4)
    idx_out[...] = idx4
    w_out[...] = w4
    rank_out[...] = rank4
    carry_ref[...] = carry_ref[...] + jnp.sum(onehot, axis=0, keepdims=True)
    cnt_out[...] = carry_ref[...].astype(I32)


def _outproj_call(y_rwkv, y_diff, x, mod6, w_out, norm2_gain, router_w, router_b, S, tm):
    T = x.shape[0]
    tiles_per_seq = S // tm
    rw_hi = router_w.astype(BF16)
    rw_lo = (router_w - rw_hi.astype(F32)).astype(BF16)
    tok = lambda w: pl.BlockSpec((tm, w), lambda t: (t, 0))
    full = lambda shape: pl.BlockSpec(shape, lambda t: (0,) * len(shape))
    return pl.pallas_call(
        functools.partial(_outproj_kernel, tm=tm),
        grid=(T // tm,),
        in_specs=[tok(RWKV_WIDTH), tok(DIFF_WIDTH), tok(D_MODEL),
                  pl.BlockSpec((1, 6, D_MODEL), lambda t: (t // tiles_per_seq, 0, 0)),
                  full((RWKV_WIDTH, D_MODEL)), full((DIFF_WIDTH, D_MODEL)), full((1, D_MODEL)),
                  full((D_MODEL, N_EXPERTS)), full((D_MODEL, N_EXPERTS)), full((1, N_EXPERTS))],
        out_specs=[tok(D_MODEL), pl.BlockSpec((tm * ROW_SUB, 128), lambda t: (t, 0)),
                   tok(TOP_K), tok(TOP_K), tok(TOP_K), full((1, N_EXPERTS))],
        out_shape=[jax.ShapeDtypeStruct((T, D_MODEL), F32),
                   jax.ShapeDtypeStruct((T * ROW_SUB, 128), U32),
                   jax.ShapeDtypeStruct((T, TOP_K), I32),
                   jax.ShapeDtypeStruct((T, TOP_K), F32),
                   jax.ShapeDtypeStruct((T, TOP_K), I32),
                   jax.ShapeDtypeStruct((1, N_EXPERTS), I32)],
        scratch_shapes=[pltpu.VMEM((1, N_EXPERTS), F32)],
        compiler_params=pltpu.CompilerParams(
            dimension_semantics=("arbitrary",), vmem_limit_bytes=VMEM_LIMIT),
        name="outproj",
    )(y_rwkv, y_diff, x, mod6, w_out[:RWKV_WIDTH].astype(BF16), w_out[RWKV_WIDTH:].astype(BF16),
      norm2_gain.reshape(1, -1), rw_hi, rw_lo, router_b.reshape(1, -1))


def _experts_kernel(be_ref, nu_ref, valid_ref, xs_ref, wgu_ref, bgu_ref, wd_ref, bd_ref, ys_ref,
                    wgu_bf, wd_bf):
    i = pl.program_id(0)

    @pl.when((i == 0) | (be_ref[i] != be_ref[jnp.maximum(i - 1, 0)]))
    def _():
        wgu_bf[...] = wgu_ref[0].astype(BF16)
        wd_bf[...] = wd_ref[0].astype(BF16)

    @pl.when(i < nu_ref[0])
    def _():
        packed = _load_rows(xs_ref)
        row = lax.broadcasted_iota(I32, packed.shape, 0)
        packed = jnp.where(row < valid_ref[i], packed, jnp.uint32(0))
        xa, xb = _unpack_bf16_pairs(packed)
        x = jnp.concatenate([xa.astype(BF16), xb.astype(BF16)], axis=1)
        gu = _dot(x, wgu_bf[...]) + bgu_ref[0]
        gate = jnp.minimum(gu[:, :D_EXPERT], SWIGLU_LIMIT)
        up = jnp.clip(gu[:, D_EXPERT:], -SWIGLU_LIMIT, SWIGLU_LIMIT)
        act = (up + 1.0) * (gate * jax.nn.sigmoid(SWIGLU_ALPHA * gate))
        y = _dot(act.astype(BF16), wd_bf[...]) + bd_ref[0]
        _store_rows(ys_ref, _pack_bf16_pairs(y))

    @pl.when(i >= nu_ref[0])
    def _():
        ys_ref[...] = jnp.zeros_like(ys_ref)


def _experts_call(xs, block_e, n_used, valid_rows, w_gate_up, b_gate_up, w_down, b_down, n_blocks):
    per_expert = lambda shape: pl.BlockSpec(shape, lambda i, be, nu, vr: (be[i], 0, 0))
    grid_spec = pltpu.PrefetchScalarGridSpec(
        num_scalar_prefetch=3,
        grid=(n_blocks,),
        in_specs=[pl.BlockSpec((MOE_BLOCK * ROW_SUB, 128),
                               lambda i, be, nu, vr: (jnp.minimum(i, nu[0] - 1), 0)),
                  per_expert((1, D_MODEL, 2 * D_EXPERT)), per_expert((1, 1, 2 * D_EXPERT)),
                  per_expert((1, D_EXPERT, D_MODEL)), per_expert((1, 1, D_MODEL))],
        out_specs=pl.BlockSpec((MOE_BLOCK * ROW_SUB, 128), lambda i, be, nu, vr: (i, 0)),
        scratch_shapes=[pltpu.VMEM((D_MODEL, 2 * D_EXPERT), BF16),
                        pltpu.VMEM((D_EXPERT, D_MODEL), BF16)])
    return pl.pallas_call(
        _experts_kernel,
        grid_spec=grid_spec,
        out_shape=jax.ShapeDtypeStruct((n_blocks * MOE_BLOCK * ROW_SUB, 128), U32),
        compiler_params=pltpu.CompilerParams(
            dimension_semantics=("arbitrary",), vmem_limit_bytes=VMEM_LIMIT),
        name="experts",
    )(block_e, n_used, valid_rows, xs, w_gate_up, b_gate_up.reshape(N_EXPERTS, 1, -1), w_down,
      b_down.reshape(N_EXPERTS, 1, -1))


SC_WINDOW = 128


def _sc_mesh():
    return plsc.VectorSubcoreMesh(core_axis_name="c", subcore_axis_name="s")


def _sc_worker_chunks(n_chunks):
    info = plsc.get_sparse_core_info()
    n_workers = info.num_cores * info.num_subcores
    assert n_chunks % n_workers == 0
    return info.num_cores, n_chunks // n_workers


def _sc_scatter_call(rows, slot_chunks, n_out_rows):
    T = rows.shape[0]
    n_chunks = slot_chunks.shape[0]
    per_pass = T // SC_WINDOW
    n_cores, per_worker = _sc_worker_chunks(n_chunks)

    @functools.partial(
        pl.kernel, mesh=_sc_mesh(),
        out_type=jax.ShapeDtypeStruct((n_out_rows, ROW_SUB, 128), U32),
        scratch_types=[pltpu.VMEM((SC_WINDOW,), I32), pltpu.VMEM((SC_WINDOW, ROW_SUB, 128), U32)],
        name="sc_dispatch")
    def run(rows_hbm, idx_hbm, out_hbm, idx_v, rows_v):
        wid = lax.axis_index("s") * n_cores + lax.axis_index("c")

        @pl.loop(0, per_worker)
        def _(j):
            chunk = wid * per_worker + j
            src = lax.rem(chunk, per_pass) * SC_WINDOW
            pltpu.sync_copy(idx_hbm.at[chunk], idx_v)
            pltpu.sync_copy(rows_hbm.at[pl.ds(src, SC_WINDOW)], rows_v)
            pltpu.sync_copy(rows_v, out_hbm.at[idx_v])

    return run(rows, slot_chunks)


def _sc_gather_call(table, slot_chunks):
    n_chunks = slot_chunks.shape[0]
    n_cores, per_worker = _sc_worker_chunks(n_chunks)

    @functools.partial(
        pl.kernel, mesh=_sc_mesh(),
        out_type=jax.ShapeDtypeStruct((n_chunks * SC_WINDOW, ROW_SUB, 128), U32),
        scratch_types=[pltpu.VMEM((SC_WINDOW,), I32), pltpu.VMEM((SC_WINDOW, ROW_SUB, 128), U32)],
        name="sc_collect")
    def run(table_hbm, idx_hbm, out_hbm, idx_v, rows_v):
        wid = lax.axis_index("s") * n_cores + lax.axis_index("c")

        @pl.loop(0, per_worker)
        def _(j):
            chunk = wid * per_worker + j
            pltpu.sync_copy(idx_hbm.at[chunk], idx_v)
            pltpu.sync_copy(table_hbm.at[idx_v], rows_v)
            pltpu.sync_copy(rows_v, out_hbm.at[pl.ds(chunk * SC_WINDOW, SC_WINDOW)])

    return run(table, slot_chunks)


def _combine_dense_kernel(g0_ref, g1_ref, g2_ref, g3_ref, x1_ref, w_ref, mod_ref, fg_ref, o_ref,
                          *, tm):
    w = w_ref[...]
    acc_lo = jnp.zeros((tm, D_MODEL // 2), F32)
    acc_hi = jnp.zeros((tm, D_MODEL // 2), F32)
    for kk, g_ref in enumerate((g0_ref, g1_ref, g2_ref, g3_ref)):
        lo, hi = _unpack_bf16_pairs(_load_rows(g_ref))
        wk = w[:, kk:kk + 1]
        acc_lo = acc_lo + wk * lo
        acc_hi = acc_hi + wk * hi
    moe = jnp.concatenate([acc_lo, acc_hi], axis=1)
    gate2 = mod_ref[0, 5:6, :]
    x2 = x1_ref[...] + gate2 * moe
    ms = jnp.mean(x2 * x2, axis=-1, keepdims=True)
    o_ref[...] = x2 * lax.rsqrt(ms + NORM_EPS) * fg_ref[...]


def _combine_dense_call(gathered, x1, top_w, mod6, final_gain, S, tm):
    T = x1.shape[0]
    nt = T // tm
    tiles_per_seq = S // tm
    rows = lambda kk: pl.BlockSpec((tm * ROW_SUB, 128), lambda t: (kk * nt + t, 0))
    return pl.pallas_call(
        functools.partial(_combine_dense_kernel, tm=tm),
        grid=(nt,),
        in_specs=[rows(0), rows(1), rows(2), rows(3),
                  pl.BlockSpec((tm, D_MODEL), lambda t: (t, 0)),
                  pl.BlockSpec((tm, TOP_K), lambda t: (t, 0)),
                  pl.BlockSpec((1, 6, D_MODEL), lambda t: (t // tiles_per_seq, 0, 0)),
                  pl.BlockSpec((1, D_MODEL), lambda t: (0, 0))],
        out_specs=pl.BlockSpec((tm, D_MODEL), lambda t: (t, 0)),
        out_shape=jax.ShapeDtypeStruct((T, D_MODEL), F32),
        compiler_params=pltpu.CompilerParams(
            dimension_semantics=("arbitrary",), vmem_limit_bytes=VMEM_LIMIT),
        name="combine",
    )(gathered, gathered, gathered, gathered, x1, top_w, mod6, final_gain.reshape(1, -1))


def _forward(x, c, mod_w, mod_b, norm1_gain, w_in, rwkv_shift_mu, rwkv_w0, rwkv_w_up, rwkv_a0,
             rwkv_a_up, rwkv_g_up, rwkv_k_k, rwkv_k_a, rwkv_r_k, rwkv_gn_gain, rwkv_gn_bias,
             diff_lambda_q1, diff_lambda_k1, diff_lambda_q2, diff_lambda_k2, diff_subln_gain,
             w_out, norm2_gain, router_w, router_b, w_gate_up, b_gate_up, w_down, b_down,
             final_gain):
    B, S, D = x.shape
    T = B * S
    tm_in = min(512, S)
    tq = min(512, S)
    tm = min(256, S)

    mod6 = _mod_call(c, mod_w, mod_b).reshape(B, 6, D)
    (r, k, v, a, b, g, lw, q, kat, vt) = _inproj_call(
        x, mod6, norm1_gain, w_in, rwkv_shift_mu, rwkv_w0, rwkv_w_up, rwkv_a0, rwkv_a_up,
        rwkv_g_up, rwkv_k_k, rwkv_k_a, tm_in)
    y_rwkv = _rwkv_call(r, k, v, a, b, g, lw, rwkv_r_k, rwkv_gn_gain, rwkv_gn_bias,
                        min(8, S // CHUNK))
    y_diff = _attn_call(q, kat, vt, diff_lambda_q1, diff_lambda_k1, diff_lambda_q2,
                        diff_lambda_k2, diff_subln_gain, tq)

    x1, hp, top_idx, top_w, rank, counts = _outproj_call(
        y_rwkv.reshape(T, -1), y_diff.reshape(T, -1), x.reshape(T, D), mod6, w_out, norm2_gain,
        router_w, router_b, S, tm)

    counts = counts.reshape(N_EXPERTS)
    padded = ((counts + MOE_BLOCK - 1) // MOE_BLOCK) * MOE_BLOCK
    pad_ends = jnp.cumsum(padded)
    pad_starts = pad_ends - padded
    n_blocks = -(-(T * TOP_K + N_EXPERTS * (MOE_BLOCK - 1)) // MOE_BLOCK)
    n_used = (pad_ends[-1] // MOE_BLOCK).astype(I32).reshape(1)
    block_start = jnp.minimum(jnp.arange(n_blocks, dtype=I32), n_used[0] - 1) * MOE_BLOCK
    block_e = jnp.minimum(jnp.sum(pad_ends[None, :] <= block_start[:, None], axis=1),
                          N_EXPERTS - 1).astype(I32)
    expert_ids = jnp.arange(N_EXPERTS, dtype=I32)
    start_of = jnp.sum(jnp.where(top_idx[..., None] == expert_ids, pad_starts.astype(I32), 0), axis=-1)
    slots = start_of + rank
    slot_chunks = slots.T.reshape(TOP_K * T // SC_WINDOW, SC_WINDOW)
    block_row = jnp.arange(n_blocks, dtype=I32) * MOE_BLOCK
    valid_rows = jnp.clip(counts.astype(I32)[block_e] - (block_row - pad_starts.astype(I32)[block_e]),
                          0, MOE_BLOCK).astype(I32)

    as_tokens = lambda z: z.reshape(-1, ROW_SUB, 128)
    as_lines = lambda z: z.reshape(-1, 128)
    xs = as_lines(_sc_scatter_call(as_tokens(hp), slot_chunks, n_blocks * MOE_BLOCK))
    ys = _experts_call(xs, block_e, n_used, valid_rows, w_gate_up, b_gate_up, w_down, b_down,
                       n_blocks)
    gathered = as_lines(_sc_gather_call(as_tokens(ys), slot_chunks))
    out = _combine_dense_call(gathered, x1, top_w, mod6, final_gain, S, tm)
    return out.reshape(B, S, D)


def kernel(x, c, mod_w, mod_b, norm1_gain, w_in, rwkv_shift_mu, rwkv_w0, rwkv_w_up, rwkv_a0, rwkv_a_up, rwkv_g_up, rwkv_k_k, rwkv_k_a, rwkv_r_k, rwkv_gn_gain, rwkv_gn_bias, diff_lambda_q1, diff_lambda_k1, diff_lambda_q2, diff_lambda_k2, diff_subln_gain, w_out, norm2_gain, router_w, router_b, w_gate_up, b_gate_up, w_down, b_down, final_gain):
    return _forward(x, c, mod_w[0], mod_b[0], norm1_gain[0], w_in[0], rwkv_shift_mu[0], rwkv_w0[0],
                    rwkv_w_up[0], rwkv_a0[0], rwkv_a_up[0], rwkv_g_up[0], rwkv_k_k[0], rwkv_k_a[0],
                    rwkv_r_k[0].reshape(-1), rwkv_gn_gain[0], rwkv_gn_bias[0], diff_lambda_q1[0],
                    diff_lambda_k1[0], diff_lambda_q2[0], diff_lambda_k2[0], diff_subln_gain[0],
                    w_out[0], norm2_gain[0], router_w[0], router_b[0], w_gate_up[0], b_gate_up[0],
                    w_down[0], b_down[0], final_gain)
```

```python
import functools
import math

import jax
import jax.numpy as jnp
from jax import lax
from jax.experimental import pallas as pl
from jax.experimental.pallas import tpu as pltpu
from jax.experimental.pallas import tpu_sc as plsc

F32 = jnp.float32
BF16 = jnp.bfloat16
I32 = jnp.int32
U32 = jnp.uint32

D_MODEL = 1024
RWKV_WIDTH = 512
RWKV_HEADS = 8
HEAD_DIM = 64
RWKV_COLS = 3 * RWKV_WIDTH + 64 + 64 + 128
DIFF_WIDTH = 512
DIFF_HEADS = 4
DIFF_COLS = 3 * DIFF_WIDTH
N_EXPERTS = 32
TOP_K = 4
D_EXPERT = 1024
SWIGLU_LIMIT = 7.0
SWIGLU_ALPHA = 1.702
MOE_BLOCK = 512
NORM_EPS = 1e-5
GN_EPS = 64e-5
LAMBDA_INIT = 0.8 - 0.6 * math.exp(-0.3 * 0)

CHUNK = 64
GROUP = 4
GW = GROUP * HEAD_DIM
VT_ROWS = 128 + 16
VMEM_LIMIT = 56 * 1024 * 1024


def _dot(a, b):
    return jnp.dot(a, b, preferred_element_type=F32)


def _dot_nt(a, b):
    return lax.dot_general(a, b, (((1,), (1,)), ((), ())), preferred_element_type=F32)


def _dot_tn(a, b):
    return lax.dot_general(a, b, (((0,), (0,)), ((), ())), preferred_element_type=F32)


def _round_robin(stages):
    waiting = list(stages)
    running = []
    while waiting or running:
        if waiting:
            running.append(waiting.pop(0))
        for gen in list(running):
            try:
                next(gen)
            except StopIteration:
                running.remove(gen)


def _split_dot(x, w_bf16):
    hi = x.astype(BF16)
    lo = (x - hi.astype(F32)).astype(BF16)
    return _dot(hi, w_bf16) + _dot(lo, w_bf16)


def _mod_kernel(c_ref, w_ref, b_ref, o_ref):
    c = c_ref[...]
    s = c * jax.nn.sigmoid(c)
    o_ref[...] = _dot(s, w_ref[...]) + b_ref[...]


def _mod_call(c, mod_w, mod_b):
    B = c.shape[0]
    n = mod_w.shape[1]
    tn = 1536
    return pl.pallas_call(
        _mod_kernel,
        grid=(n // tn,),
        in_specs=[pl.BlockSpec((B, D_MODEL), lambda j: (0, 0)),
                  pl.BlockSpec((D_MODEL, tn), lambda j: (0, j)),
                  pl.BlockSpec((1, tn), lambda j: (0, j))],
        out_specs=pl.BlockSpec((B, tn), lambda j: (0, j)),
        out_shape=jax.ShapeDtypeStruct((B, n), F32),
        compiler_params=pltpu.CompilerParams(
            dimension_semantics=("arbitrary",), vmem_limit_bytes=VMEM_LIMIT),
        name="mod",
    )(c, mod_w, mod_b.reshape(1, n))


def _inproj_kernel(x_ref, mod_ref, g1_ref, wrw_ref, wat_ref, mu_ref, w0_ref, wup_ref,
                   a0_ref, aup_ref, gup_ref, kk_ref, ka_ref, hsum_ref,
                   r_out, k_out, v_out, a_out, b_out, g_out, lw_out,
                   q_out, kat_out, vt_out, carry_ref, *, tm):
    s = pl.program_id(1)
    shift1 = mod_ref[0, 0:1, :]
    scale1 = mod_ref[0, 1:2, :]

    @pl.when(s == 0)
    def _():
        carry_ref[...] = jnp.zeros_like(carry_ref)

    n_part = 2 if tm % 32 == 0 else 1
    rows_per = tm // n_part
    last_row = [carry_ref[...]]

    def part(i):
        rs = slice(i * rows_per, (i + 1) * rows_per)
        x = x_ref[0, rs, :]
        ms = jnp.mean(x * x, axis=-1, keepdims=True)
        h = x * lax.rsqrt(ms + NORM_EPS) * g1_ref[...] * (1.0 + scale1) + shift1
        hb = h.astype(BF16)
        yield
        p = _dot(hb, wrw_ref[...])
        last_row.append(p[rows_per - 1:rows_per, :])
        yield
        pa = _dot(hb, wat_ref[...])
        yield
        rolled = pltpu.roll(p, shift=1, axis=0)
        row = lax.broadcasted_iota(I32, p.shape, 0)
        prev = jnp.where(row == 0, last_row[i], rolled)
        ps = p + mu_ref[...] * (prev - p)
        r = ps[:, 0:512]
        k = ps[:, 512:1024]
        v = ps[:, 1024:1536]
        lo2 = ps[:, 1536:1664]
        g_lo = ps[:, 1664:1792]
        kk = k * kk_ref[...]
        yield
        z = w0_ref[...] + _dot(jnp.tanh(lo2).astype(BF16), wup_ref[...])
        a_pre = a0_ref[...] + _dot(lo2.astype(BF16), aup_ref[...])
        g = _dot(jax.nn.sigmoid(g_lo).astype(BF16), gup_ref[...])
        ssq = _split_dot(kk * kk, hsum_ref[...])
        yield
        nz = -z
        softplus = jnp.maximum(nz, 0.0) + jnp.log(1.0 + jnp.exp(-jnp.abs(nz)))
        w = -softplus - 0.5
        lw_out[0, rs, :] = -jnp.exp(w)
        a = jax.nn.sigmoid(a_pre)
        kk = kk / jnp.maximum(jnp.sqrt(ssq), 1e-12)
        k = k * (1.0 + (a - 1.0) * ka_ref[...])
        r_out[0, rs, :] = r.astype(BF16)
        k_out[0, rs, :] = k.astype(BF16)
        v_out[0, rs, :] = v.astype(BF16)
        a_out[0, rs, :] = (-kk).astype(BF16)
        b_out[0, rs, :] = (kk * a).astype(BF16)
        g_out[0, rs, :] = g.astype(BF16)
        yield
        q_out[0, rs, :] = (pa[:, 0:512] * (1.0 / math.sqrt(HEAD_DIM))).astype(BF16)
        kat_out[0, rs, :] = pa[:, 512:1024].astype(BF16)
        for hh in range(DIFF_HEADS):
            vh = pa[:, 1024 + hh * 128:1024 + (hh + 1) * 128]
            vt_out[0, hh, 0, 0:128, rs] = vh.T.astype(BF16)

    _round_robin([part(i) for i in range(n_part)])
    carry_ref[...] = last_row[n_part]
    for hh in range(DIFF_HEADS):
        vt_out[0, hh, 0, 128:VT_ROWS, :] = jnp.ones((VT_ROWS - 128, tm), BF16)


def _head_sum_matrix(width):
    i = jnp.arange(width) // HEAD_DIM
    return (i[:, None] == i[None, :]).astype(BF16)


def _inproj_call(x, mod6, norm1_gain, w_in, mu, w0, w_up, a0, a_up, g_up, k_k, k_a, tm):
    B, S, _ = x.shape
    ns = S // tm
    w_rw = w_in[:, :RWKV_COLS].astype(BF16)
    w_at = w_in[:, RWKV_COLS:].astype(BF16)
    zeros = jnp.zeros((64, RWKV_WIDTH), F32)
    wup_p = jnp.concatenate([w_up, zeros], axis=0).astype(BF16)
    aup_p = jnp.concatenate([zeros, a_up], axis=0).astype(BF16)
    row = lambda v: v.reshape(1, -1)
    full = lambda shape: pl.BlockSpec(shape, lambda b, s: (0,) * len(shape))
    tok = lambda w: pl.BlockSpec((1, tm, w), lambda b, s: (b, s, 0))
    rw_shape = jax.ShapeDtypeStruct((B, S, RWKV_WIDTH), BF16)
    out_shape = [rw_shape] * 6 + [
        jax.ShapeDtypeStruct((B, S, RWKV_WIDTH), F32),
        jax.ShapeDtypeStruct((B, S, DIFF_WIDTH), BF16),
        jax.ShapeDtypeStruct((B, S, DIFF_WIDTH), BF16),
        jax.ShapeDtypeStruct((B, DIFF_HEADS, ns, VT_ROWS, tm), BF16)]
    out_specs = [tok(RWKV_WIDTH)] * 7 + [tok(DIFF_WIDTH)] * 2 + [
        pl.BlockSpec((1, DIFF_HEADS, 1, VT_ROWS, tm), lambda b, s: (b, 0, s, 0, 0))]
    return pl.pallas_call(
        functools.partial(_inproj_kernel, tm=tm),
        grid=(B, ns),
        in_specs=[tok(D_MODEL),
                  pl.BlockSpec((1, 6, D_MODEL), lambda b, s: (b, 0, 0)),
                  full((1, D_MODEL)),
                  full((D_MODEL, RWKV_COLS)), full((D_MODEL, DIFF_COLS)),
                  full((1, RWKV_COLS)), full((1, RWKV_WIDTH)), full((128, RWKV_WIDTH)),
                  full((1, RWKV_WIDTH)), full((128, RWKV_WIDTH)), full((128, RWKV_WIDTH)),
                  full((1, RWKV_WIDTH)), full((1, RWKV_WIDTH)),
                  full((RWKV_WIDTH, RWKV_WIDTH))],
        out_specs=out_specs,
        out_shape=out_shape,
        scratch_shapes=[pltpu.VMEM((1, RWKV_COLS), F32)],
        compiler_params=pltpu.CompilerParams(
            dimension_semantics=("arbitrary", "arbitrary"), vmem_limit_bytes=VMEM_LIMIT),
        name="inproj",
    )(x, mod6, row(norm1_gain), w_rw, w_at, row(mu), row(w0), wup_p, row(a0), aup_p,
      g_up.astype(BF16), row(k_k), row(k_a), _head_sum_matrix(RWKV_WIDTH))


def _rwkv_kernel(r_ref, k_ref, v_ref, a_ref, b_ref, g_ref, lw_ref, rk_ref, gng_ref, gnb_ref,
                 hsum_ref, y_ref, s0_ref, s1_ref, *, nck):
    C = CHUNK
    n_grp = RWKV_HEADS // GROUP
    s_refs = (s0_ref, s1_ref)

    @pl.when(pl.program_id(1) == 0)
    def _():
        s0_ref[...] = jnp.zeros_like(s0_ref)
        s1_ref[...] = jnp.zeros_like(s1_ref)

    ti = lax.broadcasted_iota(I32, (C, GW), 0)
    si = lax.broadcasted_iota(I32, (C, GW), 1) % C
    incl = si <= ti
    strict = si < ti
    eye = (si == ti).astype(F32)
    bdmask = (lax.broadcasted_iota(I32, (GW, GW), 0) // HEAD_DIM ==
              lax.broadcasted_iota(I32, (GW, GW), 1) // HEAD_DIM)
    hsum = hsum_ref[...]
    chains = [(ck, g) for ck in range(nck) for g in range(n_grp)]

    bdmask_bf = bdmask.astype(BF16)

    def bd(xb):
        return jnp.concatenate([xb] * GROUP, axis=0) * bdmask_bf

    tri =(lax.broadcasted_iota(I32, (C, C), 0) >=
           lax.broadcasted_iota(I32, (C, C), 1)).astype(BF16)
    done = {}
    sv = [s_refs[g][...] for g in range(n_grp)]

    def chain(ch):
        ck, g = ch
        rows = slice(ck * C, (ck + 1) * C)
        cols = slice(g * GW, (g + 1) * GW)
        lw = lw_ref[0, rows, cols]
        L = _split_dot_left(tri, lw)
        yield
        Lx = L - lw
        Lc = L[C - 1:C, :]
        rho = L[C // 2 - 1:C // 2, :]
        r = r_ref[0, rows, cols].astype(F32)
        k = k_ref[0, rows, cols].astype(F32)
        vb = v_ref[0, rows, cols]
        a = a_ref[0, rows, cols].astype(F32)
        b = b_ref[0, rows, cols].astype(F32)
        e_k = jnp.exp(rho - L)
        lhs = jnp.concatenate([(r * jnp.exp(L - rho)).astype(BF16),
                               (a * jnp.exp(Lx - rho)).astype(BF16)], axis=0)
        kbd = bd((k * e_k).astype(BF16))
        bbd = bd((b * e_k).astype(BF16))
        r0 = (r * jnp.exp(L)).astype(BF16)
        a0bd = bd((a * jnp.exp(Lx)).astype(BF16))
        e_o = jnp.exp(Lc - L)
        bk = jnp.concatenate([(b * e_o).astype(BF16), (k * e_o).astype(BF16)], axis=0)
        vbd = bd(vb)
        yield
        ak = _dot_nt(lhs, kbd)
        ab = _dot_nt(lhs, bbd)
        yield
        a_rk = jnp.where(incl, ak[:C], 0.0).astype(BF16)
        a_ak = jnp.where(strict, ak[C:], 0.0).astype(BF16)
        a_rb = jnp.where(incl, ab[:C], 0.0).astype(BF16)
        n = jnp.where(strict, ab[C:], 0.0)
        p = eye + n
        xb = n.astype(BF16)
        xbd = bd(xb)
        yield
        x = _dot(xb, xbd)
        yield
        for _ in range(4):
            xb = x.astype(BF16)
            stack = jnp.concatenate([xb, p.astype(BF16)], axis=0)
            xbd = bd(xb)
            yield
            o = _dot(stack, xbd)
            yield
            x = o[:C]
            p = p + o[C:]
        pb = p.astype(BF16)
        xbd = bd(x.astype(BF16))
        yield
        o = _dot(pb, xbd)
        yield
        tb = (p + o).astype(BF16)
        yield
        av = _dot(jnp.concatenate([a_ak, a_rk], axis=0), vbd)
        w1 = _dot(tb, a0bd)
        yield
        akvbd = bd(av[:C].astype(BF16))
        rw_lhs = jnp.concatenate([r0, w1.astype(BF16)], axis=0)
        yield
        w2 = _dot(tb, akvbd)
        yield
        done[ch] = dict(rw_lhs=rw_lhs, w2=w2, a_rb=a_rb, arkv=av[C:], vb=vb, bk=bk,
                        gc=jnp.exp(Lc), rk=r * k)

    def sequential(ck):
        d = [done[(ck, g)] for g in range(n_grp)]
        rws = [_dot_nt(d[g]["rw_lhs"], sv[g].astype(BF16)) for g in range(n_grp)]
        yield
        ub = [(rws[g][C:] + d[g]["w2"]).astype(BF16) for g in range(n_grp)]
        uv = [jnp.concatenate([ub[g], d[g]["vb"]], axis=0) for g in range(n_grp)]
        ubd = [bd(ub[g]) for g in range(n_grp)]
        yield
        upd = [_dot_tn(uv[g], d[g]["bk"]) for g in range(n_grp)]
        yield
        for g in range(n_grp):
            sv[g] = sv[g] * d[g]["gc"] + jnp.where(bdmask, upd[g], 0.0)
        done[("state", ck)] = True
        y = [rws[g][:C] + _dot(d[g]["a_rb"], ubd[g]) + d[g]["arkv"] for g in range(n_grp)]
        yield
        mean = [_split_dot(y[g], hsum) * (1.0 / HEAD_DIM) for g in range(n_grp)]
        yield
        dev = [y[g] - mean[g] for g in range(n_grp)]
        var = [_split_dot(dev[g] * dev[g], hsum) * (1.0 / HEAD_DIM) for g in range(n_grp)]
        bonus = [_split_dot(d[g]["rk"] * rk_ref[:, g * GW:(g + 1) * GW], hsum) for g in range(n_grp)]
        yield
        rows = slice(ck * C, (ck + 1) * C)
        for g in range(n_grp):
            cols = slice(g * GW, (g + 1) * GW)
            yn = dev[g] * lax.rsqrt(var[g] + GN_EPS) * gng_ref[:, cols] + gnb_ref[:, cols]
            out = (yn + bonus[g] * d[g]["vb"].astype(F32)) * g_ref[0, rows, cols].astype(F32)
            y_ref[0, rows, cols] = out.astype(BF16)

    waiting = [chain(ch) for ch in chains]
    seq_next = 0
    running = []
    while waiting or running or seq_next < nck:
        ready = (seq_next < nck and all((seq_next, g) in done for g in range(n_grp))
                 and (seq_next == 0 or ("state", seq_next - 1) in done))
        if ready:
            running.append(sequential(seq_next))
            seq_next += 1
        elif waiting:
            running.append(waiting.pop(0))
        for gen in list(running):
            try:
                next(gen)
            except StopIteration:
                running.remove(gen)
    for g in range(n_grp):
        s_refs[g][...] = sv[g]


def _split_dot_left(w_bf16, x):
    hi = x.astype(BF16)
    lo = (x - hi.astype(F32)).astype(BF16)
    return _dot(w_bf16, hi) + _dot(w_bf16, lo)


def _rwkv_call(r, k, v, a, b, g, lw, r_k, gn_gain, gn_bias, nck):
    B, S, _ = r.shape
    tb = nck * CHUNK
    tok = pl.BlockSpec((1, tb, RWKV_WIDTH), lambda bb, c: (bb, c, 0))
    par = pl.BlockSpec((1, RWKV_WIDTH), lambda bb, c: (0, 0))
    return pl.pallas_call(
        functools.partial(_rwkv_kernel, nck=nck),
        grid=(B, S // tb),
        in_specs=[tok] * 7 + [par] * 3 + [pl.BlockSpec((GW, GW), lambda bb, c: (0, 0))],
        out_specs=tok,
        out_shape=jax.ShapeDtypeStruct((B, S, RWKV_WIDTH), BF16),
        scratch_shapes=[pltpu.VMEM((GW, GW), F32)] * (RWKV_HEADS // GROUP),
        compiler_params=pltpu.CompilerParams(
            dimension_semantics=("arbitrary", "arbitrary"), vmem_limit_bytes=VMEM_LIMIT),
        name="rwkv",
    )(r, k, v, a, b, g, lw, r_k.reshape(1, -1), gn_gain.reshape(1, -1), gn_bias.reshape(1, -1),
      _head_sum_matrix(GW))


def _tree_reduce(op, x):
    while x.shape[0] > 8:
        h = x.shape[0] // 2
        x = op(x[:h], x[h:])
    return x


def _attn_kernel(slopes_ref, q_ref, k_ref, vt_ref, lq1_ref, lk1_ref, lq2_ref, lk2_ref,
                 gain_ref, o_ref, sa_ref, sb_ref, acc_ref, m_ref, *, tq, tk):
    hh = pl.program_id(1)
    i = pl.program_id(2)
    slope = slopes_ref[hh]

    q = q_ref[0].astype(F32)
    lane = lax.broadcasted_iota(I32, (tq, 256), 1)
    qrow = lax.broadcasted_iota(I32, (tq, 256), 0)
    qa = (qrow >> 5).astype(F32) * (32.0 * slope)
    qb = (qrow & 31).astype(F32) * slope
    qbias = jnp.where(lane < 130, 1.0, jnp.where(lane == 130, -qa, jnp.where(lane == 131, -qb, 0.0)))
    qpad = jnp.concatenate([q, jnp.zeros_like(q)], axis=1)
    q_aug = []
    for c in range(2):
        in_comp = (lane >= c * HEAD_DIM) & (lane < (c + 1) * HEAD_DIM)
        q_aug.append(jnp.where(in_comp, qpad, jnp.where(lane >= 128, qbias, 0.0)))
    klane = lax.broadcasted_iota(I32, (tk, 128), 1)
    krow = lax.broadcasted_iota(I32, (tk, 128), 0)
    ka = (krow >> 5).astype(F32) * (32.0 * slope)
    kb_ = (krow & 31).astype(F32) * slope
    kbias = jnp.where(klane == 0, ka, jnp.where(klane == 1, kb_, jnp.where(klane < 4, 1.0, 0.0))
                      ).astype(BF16)

    qt_both = jnp.concatenate([q_aug[0].T, q_aug[1].T], axis=1).astype(BF16)

    m_ref[...] = jnp.full_like(m_ref, -jnp.inf)
    acc_ref[...] = jnp.zeros_like(acc_ref)

    def scores(j):
        kb = k_ref[0, pl.ds(pl.multiple_of(j * tk, tk), tk), :]
        return _dot(jnp.concatenate([kb, kbias], axis=1), qt_both)

    def absorb(s_ref, j, masked):
        s = s_ref[...]
        off = j * tk - i * tq
        cj = slope * off.astype(F32)
        if masked:
            keep = (lax.broadcasted_iota(I32, (tk, tq), 0) -
                    lax.broadcasted_iota(I32, (tk, tq), 1) + off) <= 0
            s = jnp.where(jnp.concatenate([keep, keep], axis=1), s, -jnp.inf)
        m_loc = jnp.max(_tree_reduce(jnp.maximum, s), axis=0, keepdims=True)
        m_old = m_ref[...]
        m_new = jnp.maximum(m_old, m_loc + cj)
        pr = jnp.exp((s - (m_new - cj)).astype(BF16))
        alpha = jnp.exp(m_old - m_new)
        acc_ref[...] = alpha * acc_ref[...] + _dot(vt_ref[0, 0, j], pr)
        m_ref[...] = m_new

    n_full = (i * tq) // tk
    sa_ref[...] = scores(0)

    def pair_body(jj, carry):
        j0 = 2 * jj
        sb_ref[...] = scores(j0 + 1)
        absorb(sa_ref, j0, False)
        sa_ref[...] = scores(j0 + 2)
        absorb(sb_ref, j0 + 1, False)
        return carry

    lax.fori_loop(0, n_full // 2, pair_body, 0)

    @pl.when(n_full % 2 == 1)
    def _():
        sb_ref[...] = scores(n_full)
        absorb(sa_ref, n_full - 1, False)
        absorb(sb_ref, n_full, True)

    @pl.when(n_full % 2 == 0)
    def _():
        absorb(sa_ref, n_full, True)

    lam = (jnp.exp(jnp.sum(lq1_ref[...] * lk1_ref[...], axis=-1, keepdims=True))
           - jnp.exp(jnp.sum(lq2_ref[...] * lk2_ref[...], axis=-1, keepdims=True))
           + LAMBDA_INIT)
    o2 = acc_ref[0:128, :] / acc_ref[128:129, :]
    o = o2[:, :tq] - lam * o2[:, tq:]
    ot = o.T
    ms = jnp.mean(ot * ot, axis=-1, keepdims=True)
    y = ot * lax.rsqrt(ms + NORM_EPS) * gain_ref[...] * (1.0 - LAMBDA_INIT)
    o_ref[0] = y.astype(BF16)


def _attn_call(q, k, vt, lq1, lk1, lq2, lk2, subln_gain, tq):
    B, S, _ = q.shape
    ns, tk = vt.shape[2], vt.shape[4]
    slopes = jnp.asarray([2.0 ** (-8.0 * (i + 1) / DIFF_HEADS) for i in range(DIFF_HEADS)], F32)
    vec = lambda n: pl.BlockSpec((1, n), lambda b, h, i, sl: (0, 0))
    grid_spec = pltpu.PrefetchScalarGridSpec(
        num_scalar_prefetch=1,
        grid=(B, DIFF_HEADS, S // tq),
        in_specs=[pl.BlockSpec((1, tq, 128), lambda b, h, i, sl: (b, i, h)),
                  pl.BlockSpec((1, S, 128), lambda b, h, i, sl: (b, 0, h)),
                  pl.BlockSpec((1, 1, ns, VT_ROWS, tk), lambda b, h, i, sl: (b, h, 0, 0, 0)),
                  vec(HEAD_DIM), vec(HEAD_DIM), vec(HEAD_DIM), vec(HEAD_DIM), vec(128)],
        out_specs=pl.BlockSpec((1, tq, 128), lambda b, h, i, sl: (b, i, h)),
        scratch_shapes=[pltpu.VMEM((tk, 2 * tq), F32), pltpu.VMEM((tk, 2 * tq), F32),
                        pltpu.VMEM((VT_ROWS, 2 * tq), F32), pltpu.VMEM((1, 2 * tq), F32)])
    return pl.pallas_call(
        functools.partial(_attn_kernel, tq=tq, tk=tk),
        grid_spec=grid_spec,
        out_shape=jax.ShapeDtypeStruct((B, S, DIFF_WIDTH), BF16),
        compiler_params=pltpu.CompilerParams(
            dimension_semantics=("arbitrary", "arbitrary", "arbitrary"),
            vmem_limit_bytes=VMEM_LIMIT),
        name="attn",
    )(slopes, q, k, vt, lq1.reshape(1, -1), lk1.reshape(1, -1), lq2.reshape(1, -1),
      lk2.reshape(1, -1), subln_gain.reshape(1, -1))


def _pack_bf16_pairs(x):
    w = x.shape[1] // 2
    lo = pltpu.bitcast(x[:, :w].astype(BF16).astype(F32), U32)
    hi = pltpu.bitcast(x[:, w:].astype(BF16).astype(F32), U32)
    return (lo >> 16) | (hi & jnp.uint32(0xFFFF0000))


def _unpack_bf16_pairs(p):
    lo = pltpu.bitcast(p << 16, F32)
    hi = pltpu.bitcast(p & jnp.uint32(0xFFFF0000), F32)
    return lo, hi


ROW_SUB = (D_MODEL // 2) // 128


def _store_rows(ref, x2d):
    n = x2d.shape[0]
    for s in range(ROW_SUB):
        ref[pl.ds(s, n, stride=ROW_SUB), :] = x2d[:, s * 128:(s + 1) * 128]


def _load_rows(ref):
    n = ref.shape[0] // ROW_SUB
    return jnp.concatenate([ref[pl.ds(s, n, stride=ROW_SUB), :] for s in range(ROW_SUB)], axis=1)


def _outproj_kernel(yr_ref, yd_ref, x_ref, mod_ref, wo_r_ref, wo_d_ref, g2_ref, rw_hi_ref,
                    rw_lo_ref, rb_ref, x1_out, hp_out, idx_out, w_out, rank_out, cnt_out,
                    carry_ref, *, tm):
    t = pl.program_id(0)

    @pl.when(t == 0)
    def _():
        carry_ref[...] = jnp.zeros_like(carry_ref)

    gate1 = mod_ref[0, 2:3, :]
    shift2 = mod_ref[0, 3:4, :]
    scale2 = mod_ref[0, 4:5, :]
    n_part = 2 if tm % 32 == 0 else 1
    rows_per = tm // n_part
    eidx = lax.broadcasted_iota(I32, (rows_per, N_EXPERTS), 1)
    col4 = lax.broadcasted_iota(I32, (rows_per, TOP_K), 1)
    tri = (lax.broadcasted_iota(I32, (rows_per, rows_per), 0) >
           lax.broadcasted_iota(I32, (rows_per, rows_per), 1)).astype(BF16)
    before = [carry_ref[...]]

    def part(i):
        rs = slice(i * rows_per, (i + 1) * rows_per)
        mix = _dot(yr_ref[rs, :], wo_r_ref[...]) + _dot(yd_ref[rs, :], wo_d_ref[...])
        yield
        x1 = x_ref[rs, :] + gate1 * mix
        x1_out[rs, :] = x1
        ms = jnp.mean(x1 * x1, axis=-1, keepdims=True)
        h = x1 * lax.rsqrt(ms + NORM_EPS) * g2_ref[...] * (1.0 + scale2) + shift2
        _store_rows(hp_out.at[pl.ds(i * rows_per * ROW_SUB, rows_per * ROW_SUB), :],
                    _pack_bf16_pairs(h))
        hi = h.astype(BF16)
        lo = (h - hi.astype(F32)).astype(BF16)
        yield
        logits = (_dot(hi, rw_hi_ref[...]) + _dot(hi, rw_lo_ref[...]) + _dot(lo, rw_hi_ref[...])
                  + rb_ref[...])
        yield
        lg = logits
        vals, idxs = [], []
        onehot = jnp.zeros(logits.shape, F32)
        for _ in range(TOP_K):
            m = jnp.max(lg, axis=-1, keepdims=True)
            ix = jnp.min(jnp.where(lg == m, eidx, N_EXPERTS), axis=-1, keepdims=True)
            sel = eidx == ix
            vals.append(m)
            idxs.append(ix)
            onehot = onehot + sel.astype(F32)
            lg = jnp.where(sel, -jnp.inf, lg)
        es = [jnp.exp(v - vals[0]) for v in vals]
        den = es[0] + es[1] + es[2] + es[3]
        before.append(before[i] + jnp.sum(onehot, axis=0, keepdims=True))
        yield
        prefix = _dot(tri, onehot.astype(BF16)) + before[i]
        yield
        idx4 = jnp.zeros((rows_per, TOP_K), I32)
        w4 = jnp.zeros((rows_per, TOP_K), F32)
        rank4 = jnp.zeros((rows_per, TOP_K), I32)
        for kk in range(TOP_K):
            rk = jnp.sum(jnp.where(eidx == idxs[kk], prefix, 0.0), axis=-1, keepdims=True)
            idx4 = jnp.where(col4 == kk, idxs[kk], idx4)
            w4 = jnp.where(col4 == kk, es[kk] / den, w4)
            rank4 = jnp.where(col4 == kk, rk.astype(I32), rank4)
        idx_out[rs, :] = idx4
        w_out[rs, :] = w4
        rank_out[rs, :] = rank4

    _round_robin([part(i) for i in range(n_part)])
    carry_ref[...] = before[n_part]
    cnt_out[...] = before[n_part].astype(I32)


def _outproj_call(y_rwkv, y_diff, x, mod6, w_out, norm2_gain, router_w, router_b, S, tm):
    T = x.shape[0]
    tiles_per_seq = S // tm
    rw_hi = router_w.astype(BF16)
    rw_lo = (router_w - rw_hi.astype(F32)).astype(BF16)
    tok = lambda w: pl.BlockSpec((tm, w), lambda t: (t, 0))
    full = lambda shape: pl.BlockSpec(shape, lambda t: (0,) * len(shape))
    return pl.pallas_call(
        functools.partial(_outproj_kernel, tm=tm),
        grid=(T // tm,),
        in_specs=[tok(RWKV_WIDTH), tok(DIFF_WIDTH), tok(D_MODEL),
                  pl.BlockSpec((1, 6, D_MODEL), lambda t: (t // tiles_per_seq, 0, 0)),
                  full((RWKV_WIDTH, D_MODEL)), full((DIFF_WIDTH, D_MODEL)), full((1, D_MODEL)),
                  full((D_MODEL, N_EXPERTS)), full((D_MODEL, N_EXPERTS)), full((1, N_EXPERTS))],
        out_specs=[tok(D_MODEL), pl.BlockSpec((tm * ROW_SUB, 128), lambda t: (t, 0)),
                   tok(TOP_K), tok(TOP_K), tok(TOP_K), full((1, N_EXPERTS))],
        out_shape=[jax.ShapeDtypeStruct((T, D_MODEL), F32),
                   jax.ShapeDtypeStruct((T * ROW_SUB, 128), U32),
                   jax.ShapeDtypeStruct((T, TOP_K), I32),
                   jax.ShapeDtypeStruct((T, TOP_K), F32),
                   jax.ShapeDtypeStruct((T, TOP_K), I32),
                   jax.ShapeDtypeStruct((1, N_EXPERTS), I32)],
        scratch_shapes=[pltpu.VMEM((1, N_EXPERTS), F32)],
        compiler_params=pltpu.CompilerParams(
            dimension_semantics=("arbitrary",), vmem_limit_bytes=VMEM_LIMIT),
        name="outproj",
    )(y_rwkv, y_diff, x, mod6, w_out[:RWKV_WIDTH].astype(BF16), w_out[RWKV_WIDTH:].astype(BF16),
      norm2_gain.reshape(1, -1), rw_hi, rw_lo, router_b.reshape(1, -1))


def _experts_kernel(be_ref, nu_ref, valid_ref, xs_ref, wgu_ref, bgu_ref, wd_ref, bd_ref, ys_ref,
                    wgu_bf, wd_bf):
    i = pl.program_id(0)

    @pl.when((i == 0) | (be_ref[i] != be_ref[jnp.maximum(i - 1, 0)]))
    def _():
        wgu_bf[...] = wgu_ref[0].astype(BF16)
        wd_bf[...] = wd_ref[0].astype(BF16)

    @pl.when(i < nu_ref[0])
    def _():
        n_part = 2
        rows_per = MOE_BLOCK // n_part

        def part(j):
            lines = xs_ref.at[pl.ds(j * rows_per * ROW_SUB, rows_per * ROW_SUB), :]
            packed = _load_rows(lines)
            row = lax.broadcasted_iota(I32, packed.shape, 0) + j * rows_per
            packed = jnp.where(row < valid_ref[i], packed, jnp.uint32(0))
            xa, xb = _unpack_bf16_pairs(packed)
            x = jnp.concatenate([xa.astype(BF16), xb.astype(BF16)], axis=1)
            yield
            gu = _dot(x, wgu_bf[...]) + bgu_ref[0]
            yield
            gate = jnp.minimum(gu[:, :D_EXPERT], SWIGLU_LIMIT)
            up = jnp.clip(gu[:, D_EXPERT:], -SWIGLU_LIMIT, SWIGLU_LIMIT)
            act = ((up + 1.0) * (gate * jax.nn.sigmoid(SWIGLU_ALPHA * gate))).astype(BF16)
            yield
            y = _dot(act, wd_bf[...]) + bd_ref[0]
            yield
            _store_rows(ys_ref.at[pl.ds(j * rows_per * ROW_SUB, rows_per * ROW_SUB), :],
                        _pack_bf16_pairs(y))

        _round_robin([part(j) for j in range(n_part)])

    @pl.when(i >= nu_ref[0])
    def _():
        ys_ref[...] = jnp.zeros_like(ys_ref)


def _experts_call(xs, block_e, n_used, valid_rows, w_gate_up, b_gate_up, w_down, b_down, n_blocks):
    per_expert = lambda shape: pl.BlockSpec(shape, lambda i, be, nu, vr: (be[i], 0, 0))
    grid_spec = pltpu.PrefetchScalarGridSpec(
        num_scalar_prefetch=3,
        grid=(n_blocks,),
        in_specs=[pl.BlockSpec((MOE_BLOCK * ROW_SUB, 128),
                               lambda i, be, nu, vr: (jnp.minimum(i, nu[0] - 1), 0)),
                  per_expert((1, D_MODEL, 2 * D_EXPERT)), per_expert((1, 1, 2 * D_EXPERT)),
                  per_expert((1, D_EXPERT, D_MODEL)), per_expert((1, 1, D_MODEL))],
        out_specs=pl.BlockSpec((MOE_BLOCK * ROW_SUB, 128), lambda i, be, nu, vr: (i, 0)),
        scratch_shapes=[pltpu.VMEM((D_MODEL, 2 * D_EXPERT), BF16),
                        pltpu.VMEM((D_EXPERT, D_MODEL), BF16)])
    return pl.pallas_call(
        _experts_kernel,
        grid_spec=grid_spec,
        out_shape=jax.ShapeDtypeStruct((n_blocks * MOE_BLOCK * ROW_SUB, 128), U32),
        compiler_params=pltpu.CompilerParams(
            dimension_semantics=("arbitrary",), vmem_limit_bytes=VMEM_LIMIT),
        name="experts",
    )(block_e, n_used, valid_rows, xs, w_gate_up, b_gate_up.reshape(N_EXPERTS, 1, -1), w_down,
      b_down.reshape(N_EXPERTS, 1, -1))


SC_WINDOW = 128


def _sc_mesh():
    return plsc.VectorSubcoreMesh(core_axis_name="c", subcore_axis_name="s")


def _sc_worker_chunks(n_chunks):
    info = plsc.get_sparse_core_info()
    n_workers = info.num_cores * info.num_subcores
    assert n_chunks % n_workers == 0
    return info.num_cores, n_chunks // n_workers


def _sc_scatter_call(rows, slot_chunks, n_out_rows):
    T = rows.shape[0]
    per_pass = T // SC_WINDOW
    n_cores, per_worker = _sc_worker_chunks(per_pass)

    @functools.partial(
        pl.kernel, mesh=_sc_mesh(),
        out_type=jax.ShapeDtypeStruct((n_out_rows, ROW_SUB, 128), U32),
        scratch_types=[pltpu.VMEM((SC_WINDOW,), I32), pltpu.VMEM((SC_WINDOW, ROW_SUB, 128), U32)],
        name="sc_dispatch")
    def run(rows_hbm, idx_hbm, out_hbm, idx_v, rows_v):
        wid = lax.axis_index("s") * n_cores + lax.axis_index("c")

        @pl.loop(0, per_worker)
        def _(j):
            chunk = wid * per_worker + j
            pltpu.sync_copy(rows_hbm.at[pl.ds(chunk * SC_WINDOW, SC_WINDOW)], rows_v)
            for kk in range(TOP_K):
                pltpu.sync_copy(idx_hbm.at[kk * per_pass + chunk], idx_v)
                pltpu.sync_copy(rows_v, out_hbm.at[idx_v])

    return run(rows, slot_chunks)


def _sc_gather_call(table, slot_chunks):
    n_chunks = slot_chunks.shape[0]
    n_cores, per_worker = _sc_worker_chunks(n_chunks)

    @functools.partial(
        pl.kernel, mesh=_sc_mesh(),
        out_type=jax.ShapeDtypeStruct((n_chunks * SC_WINDOW, ROW_SUB, 128), U32),
        scratch_types=[pltpu.VMEM((SC_WINDOW,), I32), pltpu.VMEM((SC_WINDOW, ROW_SUB, 128), U32)],
        name="sc_collect")
    def run(table_hbm, idx_hbm, out_hbm, idx_v, rows_v):
        wid = lax.axis_index("s") * n_cores + lax.axis_index("c")

        @pl.loop(0, per_worker)
        def _(j):
            chunk = wid * per_worker + j
            pltpu.sync_copy(idx_hbm.at[chunk], idx_v)
            pltpu.sync_copy(table_hbm.at[idx_v], rows_v)
            pltpu.sync_copy(rows_v, out_hbm.at[pl.ds(chunk * SC_WINDOW, SC_WINDOW)])

    return run(table, slot_chunks)


def _combine_dense_kernel(g0_ref, g1_ref, g2_ref, g3_ref, x1_ref, w_ref, mod_ref, fg_ref, o_ref,
                          *, tm):
    w = w_ref[...]
    acc_lo = jnp.zeros((tm, D_MODEL // 2), F32)
    acc_hi = jnp.zeros((tm, D_MODEL // 2), F32)
    for kk, g_ref in enumerate((g0_ref, g1_ref, g2_ref, g3_ref)):
        lo, hi = _unpack_bf16_pairs(_load_rows(g_ref))
        wk = w[:, kk:kk + 1]
        acc_lo = acc_lo + wk * lo
        acc_hi = acc_hi + wk * hi
    moe = jnp.concatenate([acc_lo, acc_hi], axis=1)
    gate2 = mod_ref[0, 5:6, :]
    x2 = x1_ref[...] + gate2 * moe
    ms = jnp.mean(x2 * x2, axis=-1, keepdims=True)
    o_ref[...] = x2 * lax.rsqrt(ms + NORM_EPS) * fg_ref[...]


def _combine_dense_call(gathered, x1, top_w, mod6, final_gain, S, tm):
    T = x1.shape[0]
    nt = T // tm
    tiles_per_seq = S // tm
    rows = lambda kk: pl.BlockSpec((tm * ROW_SUB, 128), lambda t: (kk * nt + t, 0))
    return pl.pallas_call(
        functools.partial(_combine_dense_kernel, tm=tm),
        grid=(nt,),
        in_specs=[rows(0), rows(1), rows(2), rows(3),
                  pl.BlockSpec((tm, D_MODEL), lambda t: (t, 0)),
                  pl.BlockSpec((tm, TOP_K), lambda t: (t, 0)),
                  pl.BlockSpec((1, 6, D_MODEL), lambda t: (t // tiles_per_seq, 0, 0)),
                  pl.BlockSpec((1, D_MODEL), lambda t: (0, 0))],
        out_specs=pl.BlockSpec((tm, D_MODEL), lambda t: (t, 0)),
        out_shape=jax.ShapeDtypeStruct((T, D_MODEL), F32),
        compiler_params=pltpu.CompilerParams(
            dimension_semantics=("arbitrary",), vmem_limit_bytes=VMEM_LIMIT),
        name="combine",
    )(gathered, gathered, gathered, gathered, x1, top_w, mod6, final_gain.reshape(1, -1))


def _forward(x, c, mod_w, mod_b, norm1_gain, w_in, rwkv_shift_mu, rwkv_w0, rwkv_w_up, rwkv_a0,
             rwkv_a_up, rwkv_g_up, rwkv_k_k, rwkv_k_a, rwkv_r_k, rwkv_gn_gain, rwkv_gn_bias,
             diff_lambda_q1, diff_lambda_k1, diff_lambda_q2, diff_lambda_k2, diff_subln_gain,
             w_out, norm2_gain, router_w, router_b, w_gate_up, b_gate_up, w_down, b_down,
             final_gain):
    B, S, D = x.shape
    T = B * S
    tm_in = min(512, S)
    tq = min(512, S)
    tm_out = min(512, S)
    tm = min(256, S)

    mod6 = _mod_call(c, mod_w, mod_b).reshape(B, 6, D)
    (r, k, v, a, b, g, lw, q, kat, vt) = _inproj_call(
        x, mod6, norm1_gain, w_in, rwkv_shift_mu, rwkv_w0, rwkv_w_up, rwkv_a0, rwkv_a_up,
        rwkv_g_up, rwkv_k_k, rwkv_k_a, tm_in)
    y_rwkv = _rwkv_call(r, k, v, a, b, g, lw, rwkv_r_k, rwkv_gn_gain, rwkv_gn_bias,
                        min(8, S // CHUNK))
    y_diff = _attn_call(q, kat, vt, diff_lambda_q1, diff_lambda_k1, diff_lambda_q2,
                        diff_lambda_k2, diff_subln_gain, tq)

    x1, hp, top_idx, top_w, rank, counts = _outproj_call(
        y_rwkv.reshape(T, -1), y_diff.reshape(T, -1), x.reshape(T, D), mod6, w_out, norm2_gain,
        router_w, router_b, S, tm_out)

    counts = counts.reshape(N_EXPERTS)
    padded = ((counts + MOE_BLOCK - 1) // MOE_BLOCK) * MOE_BLOCK
    pad_ends = jnp.cumsum(padded)
    pad_starts = pad_ends - padded
    n_blocks = -(-(T * TOP_K + N_EXPERTS * (MOE_BLOCK - 1)) // MOE_BLOCK)
    n_used = (pad_ends[-1] // MOE_BLOCK).astype(I32).reshape(1)
    block_start = jnp.minimum(jnp.arange(n_blocks, dtype=I32), n_used[0] - 1) * MOE_BLOCK
    block_e = jnp.minimum(jnp.sum(pad_ends[None, :] <= block_start[:, None], axis=1),
                          N_EXPERTS - 1).astype(I32)
    expert_ids = jnp.arange(N_EXPERTS, dtype=I32)
    start_of = jnp.sum(jnp.where(top_idx[..., None] == expert_ids, pad_starts.astype(I32), 0), axis=-1)
    slots = start_of + rank
    slot_chunks = slots.T.reshape(TOP_K * T // SC_WINDOW, SC_WINDOW)
    block_row = jnp.arange(n_blocks, dtype=I32) * MOE_BLOCK
    valid_rows = jnp.clip(counts.astype(I32)[block_e] - (block_row - pad_starts.astype(I32)[block_e]),
                          0, MOE_BLOCK).astype(I32)

    as_tokens = lambda z: z.reshape(-1, ROW_SUB, 128)
    as_lines = lambda z: z.reshape(-1, 128)
    xs = as_lines(_sc_scatter_call(as_tokens(hp), slot_chunks, n_blocks * MOE_BLOCK))
    ys = _experts_call(xs, block_e, n_used, valid_rows, w_gate_up, b_gate_up, w_down, b_down,
                       n_blocks)
    gathered = as_lines(_sc_gather_call(as_tokens(ys), slot_chunks))
    out = _combine_dense_call(gathered, x1, top_w, mod6, final_gain, S, tm)
    return out.reshape(B, S, D)


def kernel(x, c, mod_w, mod_b, norm1_gain, w_in, rwkv_shift_mu, rwkv_w0, rwkv_w_up, rwkv_a0, rwkv_a_up, rwkv_g_up, rwkv_k_k, rwkv_k_a, rwkv_r_k, rwkv_gn_gain, rwkv_gn_bias, diff_lambda_q1, diff_lambda_k1, diff_lambda_q2, diff_lambda_k2, diff_subln_gain, w_out, norm2_gain, router_w, router_b, w_gate_up, b_gate_up, w_down, b_down, final_gain):
    return _forward(x, c, mod_w[0], mod_b[0], norm1_gain[0], w_in[0], rwkv_shift_mu[0], rwkv_w0[0],
                    rwkv_w_up[0], rwkv_a0[0], rwkv_a_up[0], rwkv_g_up[0], rwkv_k_k[0], rwkv_k_a[0],
                    rwkv_r_k[0].reshape(-1), rwkv_gn_gain[0], rwkv_gn_bias[0], diff_lambda_q1[0],
                    diff_lambda_k1[0], diff_lambda_q2[0], diff_lambda_k2[0], diff_subln_gain[0],
                    w_out[0], norm2_gain[0], router_w[0], router_b[0], w_gate_up[0], b_gate_up[0],
                    w_down[0], b_down[0], final_gain)
```

```python
import functools
import math

import jax
import jax.numpy as jnp
from jax import lax
from jax.experimental import pallas as pl
from jax.experimental.pallas import tpu as pltpu
from jax.experimental.pallas import tpu_sc as plsc

F32 = jnp.float32
BF16 = jnp.bfloat16
I32 = jnp.int32
U32 = jnp.uint32

D_MODEL = 1024
RWKV_WIDTH = 512
RWKV_HEADS = 8
HEAD_DIM = 64
RWKV_COLS = 3 * RWKV_WIDTH + 64 + 64 + 128
DIFF_WIDTH = 512
DIFF_HEADS = 4
DIFF_COLS = 3 * DIFF_WIDTH
N_EXPERTS = 32
TOP_K = 4
D_EXPERT = 1024
SWIGLU_LIMIT = 7.0
SWIGLU_ALPHA = 1.702
MOE_BLOCK = 512
NORM_EPS = 1e-5
GN_EPS = 64e-5
LAMBDA_INIT = 0.8 - 0.6 * math.exp(-0.3 * 0)

CHUNK = 64
GROUP = 4
GW = GROUP * HEAD_DIM
VT_ROWS = 128 + 16
VMEM_LIMIT = 56 * 1024 * 1024


def _dot(a, b):
    return jnp.dot(a, b, preferred_element_type=F32)


def _dot_nt(a, b):
    return lax.dot_general(a, b, (((1,), (1,)), ((), ())), preferred_element_type=F32)


def _dot_tn(a, b):
    return lax.dot_general(a, b, (((0,), (0,)), ((), ())), preferred_element_type=F32)


def _round_robin(stages):
    waiting = list(stages)
    running = []
    while waiting or running:
        if waiting:
            running.append(waiting.pop(0))
        for gen in list(running):
            try:
                next(gen)
            except StopIteration:
                running.remove(gen)


def _split_dot(x, w_bf16):
    hi = x.astype(BF16)
    lo = (x - hi.astype(F32)).astype(BF16)
    return _dot(hi, w_bf16) + _dot(lo, w_bf16)


def _mod_kernel(c_ref, w_ref, b_ref, o_ref):
    c = c_ref[...]
    s = c * jax.nn.sigmoid(c)
    o_ref[...] = _dot(s, w_ref[...]) + b_ref[...]


def _mod_call(c, mod_w, mod_b):
    B = c.shape[0]
    n = mod_w.shape[1]
    tn = 1536
    return pl.pallas_call(
        _mod_kernel,
        grid=(n // tn,),
        in_specs=[pl.BlockSpec((B, D_MODEL), lambda j: (0, 0)),
                  pl.BlockSpec((D_MODEL, tn), lambda j: (0, j)),
                  pl.BlockSpec((1, tn), lambda j: (0, j))],
        out_specs=pl.BlockSpec((B, tn), lambda j: (0, j)),
        out_shape=jax.ShapeDtypeStruct((B, n), F32),
        compiler_params=pltpu.CompilerParams(
            dimension_semantics=("arbitrary",), vmem_limit_bytes=VMEM_LIMIT),
        name="mod",
    )(c, mod_w, mod_b.reshape(1, n))


def _inproj_kernel(x_ref, mod_ref, g1_ref, wrw_ref, wat_ref, mu_ref, w0_ref, wup_ref,
                   a0_ref, aup_ref, gup_ref, kk_ref, ka_ref, hsum_ref,
                   r_out, k_out, v_out, a_out, b_out, g_out, lw_out,
                   q_out, kat_out, vt_out, carry_ref, *, tm):
    s = pl.program_id(1)
    shift1 = mod_ref[0, 0:1, :]
    scale1 = mod_ref[0, 1:2, :]

    @pl.when(s == 0)
    def _():
        carry_ref[...] = jnp.zeros_like(carry_ref)

    n_part = 2 if tm % 32 == 0 else 1
    rows_per = tm // n_part
    last_row = [carry_ref[...]]

    def part(i):
        rs = slice(i * rows_per, (i + 1) * rows_per)
        x = x_ref[0, rs, :]
        ms = jnp.mean(x * x, axis=-1, keepdims=True)
        h = x * lax.rsqrt(ms + NORM_EPS) * g1_ref[...] * (1.0 + scale1) + shift1
        hb = h.astype(BF16)
        yield
        p = _dot(hb, wrw_ref[...])
        last_row.append(p[rows_per - 1:rows_per, :])
        yield
        pa = _dot(hb, wat_ref[...])
        yield
        rolled = pltpu.roll(p, shift=1, axis=0)
        row = lax.broadcasted_iota(I32, p.shape, 0)
        prev = jnp.where(row == 0, last_row[i], rolled)
        ps = p + mu_ref[...] * (prev - p)
        r = ps[:, 0:512]
        k = ps[:, 512:1024]
        v = ps[:, 1024:1536]
        lo2 = ps[:, 1536:1664]
        g_lo = ps[:, 1664:1792]
        kk = k * kk_ref[...]
        yield
        z = w0_ref[...] + _dot(jnp.tanh(lo2).astype(BF16), wup_ref[...])
        a_pre = a0_ref[...] + _dot(lo2.astype(BF16), aup_ref[...])
        g = _dot(jax.nn.sigmoid(g_lo).astype(BF16), gup_ref[...])
        ssq = _split_dot(kk * kk, hsum_ref[...])
        yield
        nz = -z
        softplus = jnp.maximum(nz, 0.0) + jnp.log(1.0 + jnp.exp(-jnp.abs(nz)))
        w = -softplus - 0.5
        lw_out[0, rs, :] = -jnp.exp(w)
        a = jax.nn.sigmoid(a_pre)
        kk = kk / jnp.maximum(jnp.sqrt(ssq), 1e-12)
        k = k * (1.0 + (a - 1.0) * ka_ref[...])
        r_out[0, rs, :] = r.astype(BF16)
        k_out[0, rs, :] = k.astype(BF16)
        v_out[0, rs, :] = v.astype(BF16)
        a_out[0, rs, :] = (-kk).astype(BF16)
        b_out[0, rs, :] = (kk * a).astype(BF16)
        g_out[0, rs, :] = g.astype(BF16)
        yield
        q_out[0, rs, :] = (pa[:, 0:512] * (1.0 / math.sqrt(HEAD_DIM))).astype(BF16)
        kat_out[0, rs, :] = pa[:, 512:1024].astype(BF16)
        for hh in range(DIFF_HEADS):
            vh = pa[:, 1024 + hh * 128:1024 + (hh + 1) * 128]
            vt_out[0, hh, 0, 0:128, rs] = vh.T.astype(BF16)

    _round_robin([part(i) for i in range(n_part)])
    carry_ref[...] = last_row[n_part]
    for hh in range(DIFF_HEADS):
        vt_out[0, hh, 0, 128:VT_ROWS, :] = jnp.ones((VT_ROWS - 128, tm), BF16)


def _head_sum_matrix(width):
    i = jnp.arange(width) // HEAD_DIM
    return (i[:, None] == i[None, :]).astype(BF16)


def _inproj_call(x, mod6, norm1_gain, w_in, mu, w0, w_up, a0, a_up, g_up, k_k, k_a, tm):
    B, S, _ = x.shape
    ns = S // tm
    w_rw = w_in[:, :RWKV_COLS].astype(BF16)
    w_at = w_in[:, RWKV_COLS:].astype(BF16)
    zeros = jnp.zeros((64, RWKV_WIDTH), F32)
    wup_p = jnp.concatenate([w_up, zeros], axis=0).astype(BF16)
    aup_p = jnp.concatenate([zeros, a_up], axis=0).astype(BF16)
    row = lambda v: v.reshape(1, -1)
    full = lambda shape: pl.BlockSpec(shape, lambda b, s: (0,) * len(shape))
    tok = lambda w: pl.BlockSpec((1, tm, w), lambda b, s: (b, s, 0))
    rw_shape = jax.ShapeDtypeStruct((B, S, RWKV_WIDTH), BF16)
    out_shape = [rw_shape] * 6 + [
        jax.ShapeDtypeStruct((B, S, RWKV_WIDTH), F32),
        jax.ShapeDtypeStruct((B, S, DIFF_WIDTH), BF16),
        jax.ShapeDtypeStruct((B, S, DIFF_WIDTH), BF16),
        jax.ShapeDtypeStruct((B, DIFF_HEADS, ns, VT_ROWS, tm), BF16)]
    out_specs = [tok(RWKV_WIDTH)] * 7 + [tok(DIFF_WIDTH)] * 2 + [
        pl.BlockSpec((1, DIFF_HEADS, 1, VT_ROWS, tm), lambda b, s: (b, 0, s, 0, 0))]
    return pl.pallas_call(
        functools.partial(_inproj_kernel, tm=tm),
        grid=(B, ns),
        in_specs=[tok(D_MODEL),
                  pl.BlockSpec((1, 6, D_MODEL), lambda b, s: (b, 0, 0)),
                  full((1, D_MODEL)),
                  full((D_MODEL, RWKV_COLS)), full((D_MODEL, DIFF_COLS)),
                  full((1, RWKV_COLS)), full((1, RWKV_WIDTH)), full((128, RWKV_WIDTH)),
                  full((1, RWKV_WIDTH)), full((128, RWKV_WIDTH)), full((128, RWKV_WIDTH)),
                  full((1, RWKV_WIDTH)), full((1, RWKV_WIDTH)),
                  full((RWKV_WIDTH, RWKV_WIDTH))],
        out_specs=out_specs,
        out_shape=out_shape,
        scratch_shapes=[pltpu.VMEM((1, RWKV_COLS), F32)],
        compiler_params=pltpu.CompilerParams(
            dimension_semantics=("arbitrary", "arbitrary"), vmem_limit_bytes=VMEM_LIMIT),
        name="inproj",
    )(x, mod6, row(norm1_gain), w_rw, w_at, row(mu), row(w0), wup_p, row(a0), aup_p,
      g_up.astype(BF16), row(k_k), row(k_a), _head_sum_matrix(RWKV_WIDTH))


def _rwkv_kernel(r_ref, k_ref, v_ref, a_ref, b_ref, g_ref, lw_ref, rk_ref, gng_ref, gnb_ref,
                 hsum_ref, y_ref, s0_ref, s1_ref, *, nck):
    C = CHUNK
    n_grp = RWKV_HEADS // GROUP
    s_refs = (s0_ref, s1_ref)

    @pl.when(pl.program_id(1) == 0)
    def _():
        s0_ref[...] = jnp.zeros_like(s0_ref)
        s1_ref[...] = jnp.zeros_like(s1_ref)

    ti = lax.broadcasted_iota(I32, (C, GW), 0)
    si = lax.broadcasted_iota(I32, (C, GW), 1) % C
    incl = si <= ti
    strict = si < ti
    eye = (si == ti).astype(F32)
    bdmask = (lax.broadcasted_iota(I32, (GW, GW), 0) // HEAD_DIM ==
              lax.broadcasted_iota(I32, (GW, GW), 1) // HEAD_DIM)
    hsum = hsum_ref[...]
    chains = [(ck, g) for ck in range(nck) for g in range(n_grp)]

    bdmask_bf = bdmask.astype(BF16)

    def bd(xb):
        return jnp.concatenate([xb] * GROUP, axis=0) * bdmask_bf

    tri =(lax.broadcasted_iota(I32, (C, C), 0) >=
           lax.broadcasted_iota(I32, (C, C), 1)).astype(BF16)
    done = {}
    sv = [s_refs[g][...] for g in range(n_grp)]

    def chain(ch):
        ck, g = ch
        rows = slice(ck * C, (ck + 1) * C)
        cols = slice(g * GW, (g + 1) * GW)
        lw = lw_ref[0, rows, cols]
        L = _split_dot_left(tri, lw)
        yield
        Lx = L - lw
        Lc = L[C - 1:C, :]
        rho = L[C // 2 - 1:C // 2, :]
        r = r_ref[0, rows, cols].astype(F32)
        k = k_ref[0, rows, cols].astype(F32)
        vb = v_ref[0, rows, cols]
        a = a_ref[0, rows, cols].astype(F32)
        b = b_ref[0, rows, cols].astype(F32)
        e_k = jnp.exp(rho - L)
        lhs = jnp.concatenate([(r * jnp.exp(L - rho)).astype(BF16),
                               (a * jnp.exp(Lx - rho)).astype(BF16)], axis=0)
        kbd = bd((k * e_k).astype(BF16))
        bbd = bd((b * e_k).astype(BF16))
        r0 = (r * jnp.exp(L)).astype(BF16)
        a0bd = bd((a * jnp.exp(Lx)).astype(BF16))
        e_o = jnp.exp(Lc - L)
        bk = jnp.concatenate([(b * e_o).astype(BF16), (k * e_o).astype(BF16)], axis=0)
        vbd = bd(vb)
        yield
        ak = _dot_nt(lhs, kbd)
        ab = _dot_nt(lhs, bbd)
        yield
        a_rk = jnp.where(incl, ak[:C], 0.0).astype(BF16)
        a_ak = jnp.where(strict, ak[C:], 0.0).astype(BF16)
        a_rb = jnp.where(incl, ab[:C], 0.0).astype(BF16)
        n = jnp.where(strict, ab[C:], 0.0)
        p = eye + n
        xb = n.astype(BF16)
        xbd = bd(xb)
        yield
        x = _dot(xb, xbd)
        yield
        for _ in range(4):
            xb = x.astype(BF16)
            stack = jnp.concatenate([xb, p.astype(BF16)], axis=0)
            xbd = bd(xb)
            yield
            o = _dot(stack, xbd)
            yield
            x = o[:C]
            p = p + o[C:]
        pb = p.astype(BF16)
        xbd = bd(x.astype(BF16))
        yield
        o = _dot(pb, xbd)
        yield
        tb = (p + o).astype(BF16)
        yield
        av = _dot(jnp.concatenate([a_ak, a_rk], axis=0), vbd)
        w1 = _dot(tb, a0bd)
        yield
        akvbd = bd(av[:C].astype(BF16))
        rw_lhs = jnp.concatenate([r0, w1.astype(BF16)], axis=0)
        yield
        w2 = _dot(tb, akvbd)
        yield
        done[ch] = dict(rw_lhs=rw_lhs, w2=w2, a_rb=a_rb, arkv=av[C:], vb=vb, bk=bk,
                        gc=jnp.exp(Lc), rk=r * k)

    def sequential(ck):
        d = [done[(ck, g)] for g in range(n_grp)]
        rws = [_dot_nt(d[g]["rw_lhs"], sv[g].astype(BF16)) for g in range(n_grp)]
        yield
        ub = [(rws[g][C:] + d[g]["w2"]).astype(BF16) for g in range(n_grp)]
        uv = [jnp.concatenate([ub[g], d[g]["vb"]], axis=0) for g in range(n_grp)]
        ubd = [bd(ub[g]) for g in range(n_grp)]
        yield
        upd = [_dot_tn(uv[g], d[g]["bk"]) for g in range(n_grp)]
        yield
        for g in range(n_grp):
            sv[g] = sv[g] * d[g]["gc"] + jnp.where(bdmask, upd[g], 0.0)
        done[("state", ck)] = True
        y = [rws[g][:C] + _dot(d[g]["a_rb"], ubd[g]) + d[g]["arkv"] for g in range(n_grp)]
        yield
        mean = [_split_dot(y[g], hsum) * (1.0 / HEAD_DIM) for g in range(n_grp)]
        yield
        dev = [y[g] - mean[g] for g in range(n_grp)]
        var = [_split_dot(dev[g] * dev[g], hsum) * (1.0 / HEAD_DIM) for g in range(n_grp)]
        bonus = [_split_dot(d[g]["rk"] * rk_ref[:, g * GW:(g + 1) * GW], hsum) for g in range(n_grp)]
        yield
        rows = slice(ck * C, (ck + 1) * C)
        for g in range(n_grp):
            cols = slice(g * GW, (g + 1) * GW)
            yn = dev[g] * lax.rsqrt(var[g] + GN_EPS) * gng_ref[:, cols] + gnb_ref[:, cols]
            out = (yn + bonus[g] * d[g]["vb"].astype(F32)) * g_ref[0, rows, cols].astype(F32)
            y_ref[0, rows, cols] = out.astype(BF16)

    waiting = [chain(ch) for ch in chains]
    seq_next = 0
    running = []
    while waiting or running or seq_next < nck:
        ready = (seq_next < nck and all((seq_next, g) in done for g in range(n_grp))
                 and (seq_next == 0 or ("state", seq_next - 1) in done))
        if ready:
            running.append(sequential(seq_next))
            seq_next += 1
        elif waiting:
            running.append(waiting.pop(0))
        for gen in list(running):
            try:
                next(gen)
            except StopIteration:
                running.remove(gen)
    for g in range(n_grp):
        s_refs[g][...] = sv[g]


def _split_dot_left(w_bf16, x):
    hi = x.astype(BF16)
    lo = (x - hi.astype(F32)).astype(BF16)
    return _dot(w_bf16, hi) + _dot(w_bf16, lo)


def _rwkv_call(r, k, v, a, b, g, lw, r_k, gn_gain, gn_bias, nck):
    B, S, _ = r.shape
    tb = nck * CHUNK
    tok = pl.BlockSpec((1, tb, RWKV_WIDTH), lambda bb, c: (bb, c, 0))
    par = pl.BlockSpec((1, RWKV_WIDTH), lambda bb, c: (0, 0))
    return pl.pallas_call(
        functools.partial(_rwkv_kernel, nck=nck),
        grid=(B, S // tb),
        in_specs=[tok] * 7 + [par] * 3 + [pl.BlockSpec((GW, GW), lambda bb, c: (0, 0))],
        out_specs=tok,
        out_shape=jax.ShapeDtypeStruct((B, S, RWKV_WIDTH), BF16),
        scratch_shapes=[pltpu.VMEM((GW, GW), F32)] * (RWKV_HEADS // GROUP),
        compiler_params=pltpu.CompilerParams(
            dimension_semantics=("arbitrary", "arbitrary"), vmem_limit_bytes=VMEM_LIMIT),
        name="rwkv",
    )(r, k, v, a, b, g, lw, r_k.reshape(1, -1), gn_gain.reshape(1, -1), gn_bias.reshape(1, -1),
      _head_sum_matrix(GW))


def _tree_reduce(op, x):
    while x.shape[0] > 8:
        h = x.shape[0] // 2
        x = op(x[:h], x[h:])
    return x


def _attn_kernel(slopes_ref, q_ref, k_ref, vt_ref, lq1_ref, lk1_ref, lq2_ref, lk2_ref,
                 gain_ref, o_ref, sa_ref, sb_ref, acc_ref, m_ref, *, tq, tk):
    hh = pl.program_id(1)
    i = pl.program_id(2)
    slope = slopes_ref[hh]

    q = q_ref[0].astype(F32)
    lane = lax.broadcasted_iota(I32, (tq, 256), 1)
    qrow = lax.broadcasted_iota(I32, (tq, 256), 0)
    qa = (qrow >> 5).astype(F32) * (32.0 * slope)
    qb = (qrow & 31).astype(F32) * slope
    qbias = jnp.where(lane < 130, 1.0, jnp.where(lane == 130, -qa, jnp.where(lane == 131, -qb, 0.0)))
    qpad = jnp.concatenate([q, jnp.zeros_like(q)], axis=1)
    q_aug = []
    for c in range(2):
        in_comp = (lane >= c * HEAD_DIM) & (lane < (c + 1) * HEAD_DIM)
        q_aug.append(jnp.where(in_comp, qpad, jnp.where(lane >= 128, qbias, 0.0)))
    klane = lax.broadcasted_iota(I32, (tk, 128), 1)
    krow = lax.broadcasted_iota(I32, (tk, 128), 0)
    ka = (krow >> 5).astype(F32) * (32.0 * slope)
    kb_ = (krow & 31).astype(F32) * slope
    kbias = jnp.where(klane == 0, ka, jnp.where(klane == 1, kb_, jnp.where(klane < 4, 1.0, 0.0))
                      ).astype(BF16)

    qt_both = jnp.concatenate([q_aug[0].T, q_aug[1].T], axis=1).astype(BF16)

    m_ref[...] = jnp.full_like(m_ref, -jnp.inf)
    acc_ref[...] = jnp.zeros_like(acc_ref)

    def scores(j):
        kb = k_ref[0, pl.ds(pl.multiple_of(j * tk, tk), tk), :]
        return _dot(jnp.concatenate([kb, kbias], axis=1), qt_both)

    def absorb(s_ref, j, masked):
        s = s_ref[...]
        off = j * tk - i * tq
        cj = slope * off.astype(F32)
        if masked:
            keep = (lax.broadcasted_iota(I32, (tk, tq), 0) -
                    lax.broadcasted_iota(I32, (tk, tq), 1) + off) <= 0
            s = jnp.where(jnp.concatenate([keep, keep], axis=1), s, -jnp.inf)
        m_loc = jnp.max(_tree_reduce(jnp.maximum, s), axis=0, keepdims=True)
        m_old = m_ref[...]
        m_new = jnp.maximum(m_old, m_loc + cj)
        pr = jnp.exp((s - (m_new - cj)).astype(BF16))
        alpha = jnp.exp(m_old - m_new)
        acc_ref[...] = alpha * acc_ref[...] + _dot(vt_ref[0, 0, j], pr)
        m_ref[...] = m_new

    n_full = (i * tq) // tk
    sa_ref[...] = scores(0)

    def pair_body(jj, carry):
        j0 = 2 * jj
        sb_ref[...] = scores(j0 + 1)
        absorb(sa_ref, j0, False)
        sa_ref[...] = scores(j0 + 2)
        absorb(sb_ref, j0 + 1, False)
        return carry

    lax.fori_loop(0, n_full // 2, pair_body, 0)

    @pl.when(n_full % 2 == 1)
    def _():
        sb_ref[...] = scores(n_full)
        absorb(sa_ref, n_full - 1, False)
        absorb(sb_ref, n_full, True)

    @pl.when(n_full % 2 == 0)
    def _():
        absorb(sa_ref, n_full, True)

    lam = (jnp.exp(jnp.sum(lq1_ref[...] * lk1_ref[...], axis=-1, keepdims=True))
           - jnp.exp(jnp.sum(lq2_ref[...] * lk2_ref[...], axis=-1, keepdims=True))
           + LAMBDA_INIT)
    o2 = acc_ref[0:128, :] / acc_ref[128:129, :]
    o = o2[:, :tq] - lam * o2[:, tq:]
    ot = o.T
    ms = jnp.mean(ot * ot, axis=-1, keepdims=True)
    y = ot * lax.rsqrt(ms + NORM_EPS) * gain_ref[...] * (1.0 - LAMBDA_INIT)
    o_ref[0] = y.astype(BF16)


def _attn_call(q, k, vt, lq1, lk1, lq2, lk2, subln_gain, tq):
    B, S, _ = q.shape
    ns, tk = vt.shape[2], vt.shape[4]
    slopes = jnp.asarray([2.0 ** (-8.0 * (i + 1) / DIFF_HEADS) for i in range(DIFF_HEADS)], F32)
    vec = lambda n: pl.BlockSpec((1, n), lambda b, h, i, sl: (0, 0))
    grid_spec = pltpu.PrefetchScalarGridSpec(
        num_scalar_prefetch=1,
        grid=(B, DIFF_HEADS, S // tq),
        in_specs=[pl.BlockSpec((1, tq, 128), lambda b, h, i, sl: (b, i, h)),
                  pl.BlockSpec((1, S, 128), lambda b, h, i, sl: (b, 0, h)),
                  pl.BlockSpec((1, 1, ns, VT_ROWS, tk), lambda b, h, i, sl: (b, h, 0, 0, 0)),
                  vec(HEAD_DIM), vec(HEAD_DIM), vec(HEAD_DIM), vec(HEAD_DIM), vec(128)],
        out_specs=pl.BlockSpec((1, tq, 128), lambda b, h, i, sl: (b, i, h)),
        scratch_shapes=[pltpu.VMEM((tk, 2 * tq), F32), pltpu.VMEM((tk, 2 * tq), F32),
                        pltpu.VMEM((VT_ROWS, 2 * tq), F32), pltpu.VMEM((1, 2 * tq), F32)])
    return pl.pallas_call(
        functools.partial(_attn_kernel, tq=tq, tk=tk),
        grid_spec=grid_spec,
        out_shape=jax.ShapeDtypeStruct((B, S, DIFF_WIDTH), BF16),
        compiler_params=pltpu.CompilerParams(
            dimension_semantics=("arbitrary", "arbitrary", "arbitrary"),
            vmem_limit_bytes=VMEM_LIMIT),
        name="attn",
    )(slopes, q, k, vt, lq1.reshape(1, -1), lk1.reshape(1, -1), lq2.reshape(1, -1),
      lk2.reshape(1, -1), subln_gain.reshape(1, -1))


def _pack_bf16_pairs(x):
    w = x.shape[1] // 2
    lo = pltpu.bitcast(x[:, :w].astype(BF16).astype(F32), U32)
    hi = pltpu.bitcast(x[:, w:].astype(BF16).astype(F32), U32)
    return (lo >> 16) | (hi & jnp.uint32(0xFFFF0000))


def _unpack_bf16_pairs(p):
    lo = pltpu.bitcast(p << 16, F32)
    hi = pltpu.bitcast(p & jnp.uint32(0xFFFF0000), F32)
    return lo, hi


ROW_SUB = (D_MODEL // 2) // 128


def _store_rows(ref, x2d):
    n = x2d.shape[0]
    for s in range(ROW_SUB):
        ref[pl.ds(s, n, stride=ROW_SUB), :] = x2d[:, s * 128:(s + 1) * 128]


def _load_rows(ref):
    n = ref.shape[0] // ROW_SUB
    return jnp.concatenate([ref[pl.ds(s, n, stride=ROW_SUB), :] for s in range(ROW_SUB)], axis=1)


def _outproj_kernel(yr_ref, yd_ref, x_ref, mod_ref, wo_r_ref, wo_d_ref, g2_ref, rw_hi_ref,
                    rw_lo_ref, rb_ref, x1_out, hp_out, idx_out, w_out, rank_out, cnt_out,
                    carry_ref, *, tm):
    t = pl.program_id(0)

    @pl.when(t == 0)
    def _():
        carry_ref[...] = jnp.zeros_like(carry_ref)

    gate1 = mod_ref[0, 2:3, :]
    shift2 = mod_ref[0, 3:4, :]
    scale2 = mod_ref[0, 4:5, :]
    n_part = 2 if tm % 32 == 0 else 1
    rows_per = tm // n_part
    eidx = lax.broadcasted_iota(I32, (rows_per, N_EXPERTS), 1)
    col4 = lax.broadcasted_iota(I32, (rows_per, TOP_K), 1)
    tri = (lax.broadcasted_iota(I32, (rows_per, rows_per), 0) >
           lax.broadcasted_iota(I32, (rows_per, rows_per), 1)).astype(BF16)
    before = [carry_ref[...]]

    def part(i):
        rs = slice(i * rows_per, (i + 1) * rows_per)
        mix = _dot(yr_ref[rs, :], wo_r_ref[...]) + _dot(yd_ref[rs, :], wo_d_ref[...])
        yield
        x1 = x_ref[rs, :] + gate1 * mix
        x1_out[rs, :] = x1
        ms = jnp.mean(x1 * x1, axis=-1, keepdims=True)
        h = x1 * lax.rsqrt(ms + NORM_EPS) * g2_ref[...] * (1.0 + scale2) + shift2
        _store_rows(hp_out.at[pl.ds(i * rows_per * ROW_SUB, rows_per * ROW_SUB), :],
                    _pack_bf16_pairs(h))
        hi = h.astype(BF16)
        lo = (h - hi.astype(F32)).astype(BF16)
        yield
        logits = (_dot(hi, rw_hi_ref[...]) + _dot(hi, rw_lo_ref[...]) + _dot(lo, rw_hi_ref[...])
                  + rb_ref[...])
        yield
        lg = logits
        vals, idxs = [], []
        onehot = jnp.zeros(logits.shape, F32)
        for _ in range(TOP_K):
            m = jnp.max(lg, axis=-1, keepdims=True)
            ix = jnp.min(jnp.where(lg == m, eidx, N_EXPERTS), axis=-1, keepdims=True)
            sel = eidx == ix
            vals.append(m)
            idxs.append(ix)
            onehot = onehot + sel.astype(F32)
            lg = jnp.where(sel, -jnp.inf, lg)
        es = [jnp.exp(v - vals[0]) for v in vals]
        den = es[0] + es[1] + es[2] + es[3]
        before.append(before[i] + jnp.sum(onehot, axis=0, keepdims=True))
        yield
        prefix = _dot(tri, onehot.astype(BF16)) + before[i]
        yield
        idx4 = jnp.zeros((rows_per, TOP_K), I32)
        w4 = jnp.zeros((rows_per, TOP_K), F32)
        rank4 = jnp.zeros((rows_per, TOP_K), I32)
        for kk in range(TOP_K):
            rk = jnp.sum(jnp.where(eidx == idxs[kk], prefix, 0.0), axis=-1, keepdims=True)
            idx4 = jnp.where(col4 == kk, idxs[kk], idx4)
            w4 = jnp.where(col4 == kk, es[kk] / den, w4)
            rank4 = jnp.where(col4 == kk, rk.astype(I32), rank4)
        idx_out[rs, :] = idx4
        w_out[rs, :] = w4
        rank_out[rs, :] = rank4

    _round_robin([part(i) for i in range(n_part)])
    carry_ref[...] = before[n_part]
    cnt_out[...] = before[n_part].astype(I32)


def _outproj_call(y_rwkv, y_diff, x, mod6, w_out, norm2_gain, router_w, router_b, S, tm):
    T = x.shape[0]
    tiles_per_seq = S // tm
    rw_hi = router_w.astype(BF16)
    rw_lo = (router_w - rw_hi.astype(F32)).astype(BF16)
    tok = lambda w: pl.BlockSpec((tm, w), lambda t: (t, 0))
    full = lambda shape: pl.BlockSpec(shape, lambda t: (0,) * len(shape))
    return pl.pallas_call(
        functools.partial(_outproj_kernel, tm=tm),
        grid=(T // tm,),
        in_specs=[tok(RWKV_WIDTH), tok(DIFF_WIDTH), tok(D_MODEL),
                  pl.BlockSpec((1, 6, D_MODEL), lambda t: (t // tiles_per_seq, 0, 0)),
                  full((RWKV_WIDTH, D_MODEL)), full((DIFF_WIDTH, D_MODEL)), full((1, D_MODEL)),
                  full((D_MODEL, N_EXPERTS)), full((D_MODEL, N_EXPERTS)), full((1, N_EXPERTS))],
        out_specs=[tok(D_MODEL), pl.BlockSpec((tm * ROW_SUB, 128), lambda t: (t, 0)),
                   tok(TOP_K), tok(TOP_K), tok(TOP_K), full((1, N_EXPERTS))],
        out_shape=[jax.ShapeDtypeStruct((T, D_MODEL), F32),
                   jax.ShapeDtypeStruct((T * ROW_SUB, 128), U32),
                   jax.ShapeDtypeStruct((T, TOP_K), I32),
                   jax.ShapeDtypeStruct((T, TOP_K), F32),
                   jax.ShapeDtypeStruct((T, TOP_K), I32),
                   jax.ShapeDtypeStruct((1, N_EXPERTS), I32)],
        scratch_shapes=[pltpu.VMEM((1, N_EXPERTS), F32)],
        compiler_params=pltpu.CompilerParams(
            dimension_semantics=("arbitrary",), vmem_limit_bytes=VMEM_LIMIT),
        name="outproj",
    )(y_rwkv, y_diff, x, mod6, w_out[:RWKV_WIDTH].astype(BF16), w_out[RWKV_WIDTH:].astype(BF16),
      norm2_gain.reshape(1, -1), rw_hi, rw_lo, router_b.reshape(1, -1))


def _experts_kernel(be_ref, nu_ref, valid_ref, xs_ref, wgu_ref, bgu_ref, wd_ref, bd_ref, ys_ref,
                    wgu_bf, wd_bf):
    i = pl.program_id(0)

    @pl.when((i == 0) | (be_ref[i] != be_ref[jnp.maximum(i - 1, 0)]))
    def _():
        wgu_bf[...] = wgu_ref[0].astype(BF16)
        wd_bf[...] = wd_ref[0].astype(BF16)

    @pl.when(i < nu_ref[0])
    def _():
        packed = _load_rows(xs_ref)
        row = lax.broadcasted_iota(I32, packed.shape, 0)
        packed = jnp.where(row < valid_ref[i], packed, jnp.uint32(0))
        xa, xb = _unpack_bf16_pairs(packed)
        x = jnp.concatenate([xa.astype(BF16), xb.astype(BF16)], axis=1)
        gu = _dot(x, wgu_bf[...]) + bgu_ref[0]
        gate = jnp.minimum(gu[:, :D_EXPERT], SWIGLU_LIMIT)
        up = jnp.clip(gu[:, D_EXPERT:], -SWIGLU_LIMIT, SWIGLU_LIMIT)
        act = (up + 1.0) * (gate * jax.nn.sigmoid(SWIGLU_ALPHA * gate))
        y = _dot(act.astype(BF16), wd_bf[...]) + bd_ref[0]
        _store_rows(ys_ref, _pack_bf16_pairs(y))

    @pl.when(i >= nu_ref[0])
    def _():
        ys_ref[...] = jnp.zeros_like(ys_ref)


def _experts_call(xs, block_e, n_used, valid_rows, w_gate_up, b_gate_up, w_down, b_down, n_blocks):
    per_expert = lambda shape: pl.BlockSpec(shape, lambda i, be, nu, vr: (be[i], 0, 0))
    grid_spec = pltpu.PrefetchScalarGridSpec(
        num_scalar_prefetch=3,
        grid=(n_blocks,),
        in_specs=[pl.BlockSpec((MOE_BLOCK * ROW_SUB, 128),
                               lambda i, be, nu, vr: (jnp.minimum(i, nu[0] - 1), 0)),
                  per_expert((1, D_MODEL, 2 * D_EXPERT)), per_expert((1, 1, 2 * D_EXPERT)),
                  per_expert((1, D_EXPERT, D_MODEL)), per_expert((1, 1, D_MODEL))],
        out_specs=pl.BlockSpec((MOE_BLOCK * ROW_SUB, 128), lambda i, be, nu, vr: (i, 0)),
        scratch_shapes=[pltpu.VMEM((D_MODEL, 2 * D_EXPERT), BF16),
                        pltpu.VMEM((D_EXPERT, D_MODEL), BF16)])
    return pl.pallas_call(
        _experts_kernel,
        grid_spec=grid_spec,
        out_shape=jax.ShapeDtypeStruct((n_blocks * MOE_BLOCK * ROW_SUB, 128), U32),
        compiler_params=pltpu.CompilerParams(
            dimension_semantics=("arbitrary",), vmem_limit_bytes=VMEM_LIMIT),
        name="experts",
    )(block_e, n_used, valid_rows, xs, w_gate_up, b_gate_up.reshape(N_EXPERTS, 1, -1), w_down,
      b_down.reshape(N_EXPERTS, 1, -1))


SC_WINDOW = 128
COMBINE_PARTS = 4


def _sc_mesh():
    return plsc.VectorSubcoreMesh(core_axis_name="c", subcore_axis_name="s")


def _sc_worker_chunks(n_chunks):
    info = plsc.get_sparse_core_info()
    n_workers = info.num_cores * info.num_subcores
    assert n_chunks % n_workers == 0
    return info.num_cores, n_chunks // n_workers


def _sc_scatter_call(rows, slot_chunks, n_out_rows):
    T = rows.shape[0]
    per_pass = T // SC_WINDOW
    n_cores, per_worker = _sc_worker_chunks(per_pass)

    @functools.partial(
        pl.kernel, mesh=_sc_mesh(),
        out_type=jax.ShapeDtypeStruct((n_out_rows, ROW_SUB, 128), U32),
        scratch_types=[pltpu.VMEM((SC_WINDOW,), I32), pltpu.VMEM((SC_WINDOW, ROW_SUB, 128), U32)],
        name="sc_dispatch")
    def run(rows_hbm, idx_hbm, out_hbm, idx_v, rows_v):
        wid = lax.axis_index("s") * n_cores + lax.axis_index("c")

        @pl.loop(0, per_worker)
        def _(j):
            chunk = wid * per_worker + j
            pltpu.sync_copy(rows_hbm.at[pl.ds(chunk * SC_WINDOW, SC_WINDOW)], rows_v)
            for kk in range(TOP_K):
                pltpu.sync_copy(idx_hbm.at[kk * per_pass + chunk], idx_v)
                pltpu.sync_copy(rows_v, out_hbm.at[idx_v])

    return run(rows, slot_chunks)


def _sc_gather_call(table, slot_chunks):
    n_chunks = slot_chunks.shape[0]
    n_cores, per_worker = _sc_worker_chunks(n_chunks)

    @functools.partial(
        pl.kernel, mesh=_sc_mesh(),
        out_type=jax.ShapeDtypeStruct((n_chunks * SC_WINDOW, ROW_SUB, 128), U32),
        scratch_types=[pltpu.VMEM((SC_WINDOW,), I32), pltpu.VMEM((SC_WINDOW, ROW_SUB, 128), U32)],
        name="sc_collect")
    def run(table_hbm, idx_hbm, out_hbm, idx_v, rows_v):
        wid = lax.axis_index("s") * n_cores + lax.axis_index("c")

        @pl.loop(0, per_worker)
        def _(j):
            chunk = wid * per_worker + j
            pltpu.sync_copy(idx_hbm.at[chunk], idx_v)
            pltpu.sync_copy(table_hbm.at[idx_v], rows_v)
            pltpu.sync_copy(rows_v, out_hbm.at[pl.ds(chunk * SC_WINDOW, SC_WINDOW)])

    return run(table, slot_chunks)


def _combine_dense_kernel(g0_ref, g1_ref, g2_ref, g3_ref, x1_ref, w_ref, mod_ref, fg_ref, o_ref,
                          *, tm):
    w = w_ref[...]
    acc_lo = jnp.zeros((tm, D_MODEL // 2), F32)
    acc_hi = jnp.zeros((tm, D_MODEL // 2), F32)
    for kk, g_ref in enumerate((g0_ref, g1_ref, g2_ref, g3_ref)):
        lo, hi = _unpack_bf16_pairs(_load_rows(g_ref))
        wk = w[:, kk:kk + 1]
        acc_lo = acc_lo + wk * lo
        acc_hi = acc_hi + wk * hi
    moe = jnp.concatenate([acc_lo, acc_hi], axis=1)
    gate2 = mod_ref[0, 5:6, :]
    x2 = x1_ref[...] + gate2 * moe
    ms = jnp.mean(x2 * x2, axis=-1, keepdims=True)
    o_ref[...] = x2 * lax.rsqrt(ms + NORM_EPS) * fg_ref[...]


def _combine_dense_call(gathered, x1, top_w, mod6, final_gain, S, tm, part, n_parts):
    T = x1.shape[0]
    assert T % (tm * n_parts) == 0
    nt = T // tm // n_parts
    first = part * nt
    tiles_per_seq = S // tm
    rows = lambda kk: pl.BlockSpec((tm * ROW_SUB, 128), lambda t: (kk * nt + t, 0))
    return pl.pallas_call(
        functools.partial(_combine_dense_kernel, tm=tm),
        grid=(nt,),
        in_specs=[rows(0), rows(1), rows(2), rows(3),
                  pl.BlockSpec((tm, D_MODEL), lambda t: (first + t, 0)),
                  pl.BlockSpec((tm, TOP_K), lambda t: (first + t, 0)),
                  pl.BlockSpec((1, 6, D_MODEL), lambda t: ((first + t) // tiles_per_seq, 0, 0)),
                  pl.BlockSpec((1, D_MODEL), lambda t: (0, 0))],
        out_specs=pl.BlockSpec((tm, D_MODEL), lambda t: (first + t, 0)),
        out_shape=jax.ShapeDtypeStruct((T, D_MODEL), F32),
        input_output_aliases={4: 0},
        compiler_params=pltpu.CompilerParams(
            dimension_semantics=("arbitrary",), vmem_limit_bytes=VMEM_LIMIT),
        name="combine",
    )(gathered, gathered, gathered, gathered, x1, top_w, mod6, final_gain.reshape(1, -1))


def _forward(x, c, mod_w, mod_b, norm1_gain, w_in, rwkv_shift_mu, rwkv_w0, rwkv_w_up, rwkv_a0,
             rwkv_a_up, rwkv_g_up, rwkv_k_k, rwkv_k_a, rwkv_r_k, rwkv_gn_gain, rwkv_gn_bias,
             diff_lambda_q1, diff_lambda_k1, diff_lambda_q2, diff_lambda_k2, diff_subln_gain,
             w_out, norm2_gain, router_w, router_b, w_gate_up, b_gate_up, w_down, b_down,
             final_gain):
    B, S, D = x.shape
    T = B * S
    tm_in = min(512, S)
    tq = min(512, S)
    tm_out = min(512, S)
    tm = min(256, S)

    mod6 = _mod_call(c, mod_w, mod_b).reshape(B, 6, D)
    (r, k, v, a, b, g, lw, q, kat, vt) = _inproj_call(
        x, mod6, norm1_gain, w_in, rwkv_shift_mu, rwkv_w0, rwkv_w_up, rwkv_a0, rwkv_a_up,
        rwkv_g_up, rwkv_k_k, rwkv_k_a, tm_in)
    y_rwkv = _rwkv_call(r, k, v, a, b, g, lw, rwkv_r_k, rwkv_gn_gain, rwkv_gn_bias,
                        min(8, S // CHUNK))
    y_diff = _attn_call(q, kat, vt, diff_lambda_q1, diff_lambda_k1, diff_lambda_q2,
                        diff_lambda_k2, diff_subln_gain, tq)

    x1, hp, top_idx, top_w, rank, counts = _outproj_call(
        y_rwkv.reshape(T, -1), y_diff.reshape(T, -1), x.reshape(T, D), mod6, w_out, norm2_gain,
        router_w, router_b, S, tm_out)

    counts = counts.reshape(N_EXPERTS)
    padded = ((counts + MOE_BLOCK - 1) // MOE_BLOCK) * MOE_BLOCK
    pad_ends = jnp.cumsum(padded)
    pad_starts = pad_ends - padded
    n_blocks = -(-(T * TOP_K + N_EXPERTS * (MOE_BLOCK - 1)) // MOE_BLOCK)
    n_used = (pad_ends[-1] // MOE_BLOCK).astype(I32).reshape(1)
    block_start = jnp.minimum(jnp.arange(n_blocks, dtype=I32), n_used[0] - 1) * MOE_BLOCK
    block_e = jnp.minimum(jnp.sum(pad_ends[None, :] <= block_start[:, None], axis=1),
                          N_EXPERTS - 1).astype(I32)
    expert_ids = jnp.arange(N_EXPERTS, dtype=I32)
    start_of = jnp.sum(jnp.where(top_idx[..., None] == expert_ids, pad_starts.astype(I32), 0), axis=-1)
    slots = start_of + rank
    slot_chunks = slots.T.reshape(TOP_K * T // SC_WINDOW, SC_WINDOW)
    block_row = jnp.arange(n_blocks, dtype=I32) * MOE_BLOCK
    valid_rows = jnp.clip(counts.astype(I32)[block_e] - (block_row - pad_starts.astype(I32)[block_e]),
                          0, MOE_BLOCK).astype(I32)

    as_tokens = lambda z: z.reshape(-1, ROW_SUB, 128)
    as_lines = lambda z: z.reshape(-1, 128)
    xs = as_lines(_sc_scatter_call(as_tokens(hp), slot_chunks, n_blocks * MOE_BLOCK))
    ys = _experts_call(xs, block_e, n_used, valid_rows, w_gate_up, b_gate_up, w_down, b_down,
                       n_blocks)
    ys_tokens = as_tokens(ys)
    out = x1
    per_part = T // COMBINE_PARTS
    for part in range(COMBINE_PARTS):
        part_slots = slots[part * per_part:(part + 1) * per_part]
        part_chunks = part_slots.T.reshape(TOP_K * per_part // SC_WINDOW, SC_WINDOW)
        gathered = as_lines(_sc_gather_call(ys_tokens, part_chunks))
        out = _combine_dense_call(gathered, out, top_w, mod6, final_gain, S, tm, part, COMBINE_PARTS)
    return out.reshape(B, S, D)


def kernel(x, c, mod_w, mod_b, norm1_gain, w_in, rwkv_shift_mu, rwkv_w0, rwkv_w_up, rwkv_a0, rwkv_a_up, rwkv_g_up, rwkv_k_k, rwkv_k_a, rwkv_r_k, rwkv_gn_gain, rwkv_gn_bias, diff_lambda_q1, diff_lambda_k1, diff_lambda_q2, diff_lambda_k2, diff_subln_gain, w_out, norm2_gain, router_w, router_b, w_gate_up, b_gate_up, w_down, b_down, final_gain):
    return _forward(x, c, mod_w[0], mod_b[0], norm1_gain[0], w_in[0], rwkv_shift_mu[0], rwkv_w0[0],
                    rwkv_w_up[0], rwkv_a0[0], rwkv_a_up[0], rwkv_g_up[0], rwkv_k_k[0], rwkv_k_a[0],
                    rwkv_r_k[0].reshape(-1), rwkv_gn_gain[0], rwkv_gn_bias[0], diff_lambda_q1[0],
                    diff_lambda_k1[0], diff_lambda_q2[0], diff_lambda_k2[0], diff_subln_gain[0],
                    w_out[0], norm2_gain[0], router_w[0], router_b[0], w_gate_up[0], b_gate_up[0],
                    w_down[0], b_down[0], final_gain)
```

```python
import functools
import math

import jax
import jax.numpy as jnp
from jax import lax
from jax.experimental import pallas as pl
from jax.experimental.pallas import tpu as pltpu
from jax.experimental.pallas import tpu_sc as plsc

F32 = jnp.float32
BF16 = jnp.bfloat16
I32 = jnp.int32
U32 = jnp.uint32

D_MODEL = 1024
RWKV_WIDTH = 512
RWKV_HEADS = 8
HEAD_DIM = 64
RWKV_COLS = 3 * RWKV_WIDTH + 64 + 64 + 128
DIFF_WIDTH = 512
DIFF_HEADS = 4
DIFF_COLS = 3 * DIFF_WIDTH
N_EXPERTS = 32
TOP_K = 4
D_EXPERT = 1024
SWIGLU_LIMIT = 7.0
SWIGLU_ALPHA = 1.702
MOE_BLOCK = 512
NORM_EPS = 1e-5
GN_EPS = 64e-5
LAMBDA_INIT = 0.8 - 0.6 * math.exp(-0.3 * 0)

CHUNK = 64
GROUP = 4
GW = GROUP * HEAD_DIM
VT_ROWS = 128 + 16
VMEM_LIMIT = 56 * 1024 * 1024


def _dot(a, b):
    return jnp.dot(a, b, preferred_element_type=F32)


def _dot_nt(a, b):
    return lax.dot_general(a, b, (((1,), (1,)), ((), ())), preferred_element_type=F32)


def _dot_tn(a, b):
    return lax.dot_general(a, b, (((0,), (0,)), ((), ())), preferred_element_type=F32)


def _round_robin(stages):
    waiting = list(stages)
    running = []
    while waiting or running:
        if waiting:
            running.append(waiting.pop(0))
        for gen in list(running):
            try:
                next(gen)
            except StopIteration:
                running.remove(gen)


def _split_dot(x, w_bf16):
    hi = x.astype(BF16)
    lo = (x - hi.astype(F32)).astype(BF16)
    return _dot(hi, w_bf16) + _dot(lo, w_bf16)


def _mod_kernel(c_ref, w_ref, b_ref, o_ref):
    c = c_ref[...]
    s = c * jax.nn.sigmoid(c)
    o_ref[...] = _dot(s, w_ref[...]) + b_ref[...]


def _mod_call(c, mod_w, mod_b):
    B = c.shape[0]
    n = mod_w.shape[1]
    tn = 1536
    return pl.pallas_call(
        _mod_kernel,
        grid=(n // tn,),
        in_specs=[pl.BlockSpec((B, D_MODEL), lambda j: (0, 0)),
                  pl.BlockSpec((D_MODEL, tn), lambda j: (0, j)),
                  pl.BlockSpec((1, tn), lambda j: (0, j))],
        out_specs=pl.BlockSpec((B, tn), lambda j: (0, j)),
        out_shape=jax.ShapeDtypeStruct((B, n), F32),
        compiler_params=pltpu.CompilerParams(
            dimension_semantics=("arbitrary",), vmem_limit_bytes=VMEM_LIMIT),
        name="mod",
    )(c, mod_w, mod_b.reshape(1, n))


def _inproj_kernel(x_ref, mod_ref, g1_ref, wrw_ref, wat_ref, mu_ref, w0_ref, wup_ref,
                   a0_ref, aup_ref, gup_ref, kk_ref, ka_ref, hsum_ref,
                   r_out, k_out, v_out, a_out, b_out, g_out, lw_out,
                   q_out, kat_out, vt_out, carry_ref, *, tm):
    s = pl.program_id(1)
    shift1 = mod_ref[0, 0:1, :]
    scale1 = mod_ref[0, 1:2, :]

    @pl.when(s == 0)
    def _():
        carry_ref[...] = jnp.zeros_like(carry_ref)

    n_part = 2 if tm % 32 == 0 else 1
    rows_per = tm // n_part
    last_row = [carry_ref[...]]

    def part(i):
        rs = slice(i * rows_per, (i + 1) * rows_per)
        x = x_ref[0, rs, :]
        ms = jnp.mean(x * x, axis=-1, keepdims=True)
        h = x * lax.rsqrt(ms + NORM_EPS) * g1_ref[...] * (1.0 + scale1) + shift1
        hb = h.astype(BF16)
        yield
        p = _dot(hb, wrw_ref[...])
        last_row.append(p[rows_per - 1:rows_per, :])
        yield
        pa = _dot(hb, wat_ref[...])
        yield
        rolled = pltpu.roll(p, shift=1, axis=0)
        row = lax.broadcasted_iota(I32, p.shape, 0)
        prev = jnp.where(row == 0, last_row[i], rolled)
        ps = p + mu_ref[...] * (prev - p)
        r = ps[:, 0:512]
        k = ps[:, 512:1024]
        v = ps[:, 1024:1536]
        lo2 = ps[:, 1536:1664]
        g_lo = ps[:, 1664:1792]
        kk = k * kk_ref[...]
        yield
        z = w0_ref[...] + _dot(jnp.tanh(lo2).astype(BF16), wup_ref[...])
        a_pre = a0_ref[...] + _dot(lo2.astype(BF16), aup_ref[...])
        g = _dot(jax.nn.sigmoid(g_lo).astype(BF16), gup_ref[...])
        ssq = _split_dot(kk * kk, hsum_ref[...])
        yield
        nz = -z
        softplus = jnp.maximum(nz, 0.0) + jnp.log(1.0 + jnp.exp(-jnp.abs(nz)))
        w = -softplus - 0.5
        lw_out[0, rs, :] = -jnp.exp(w)
        a = jax.nn.sigmoid(a_pre)
        kk = kk / jnp.maximum(jnp.sqrt(ssq), 1e-12)
        k = k * (1.0 + (a - 1.0) * ka_ref[...])
        r_out[0, rs, :] = r.astype(BF16)
        k_out[0, rs, :] = k.astype(BF16)
        v_out[0, rs, :] = v.astype(BF16)
        a_out[0, rs, :] = (-kk).astype(BF16)
        b_out[0, rs, :] = (kk * a).astype(BF16)
        g_out[0, rs, :] = g.astype(BF16)
        yield
        q_out[0, rs, :] = (pa[:, 0:512] * (1.0 / math.sqrt(HEAD_DIM))).astype(BF16)
        kat_out[0, rs, :] = pa[:, 512:1024].astype(BF16)
        for hh in range(DIFF_HEADS):
            vh = pa[:, 1024 + hh * 128:1024 + (hh + 1) * 128]
            vt_out[0, hh, 0, 0:128, rs] = vh.T.astype(BF16)

    _round_robin([part(i) for i in range(n_part)])
    carry_ref[...] = last_row[n_part]
    for hh in range(DIFF_HEADS):
        vt_out[0, hh, 0, 128:VT_ROWS, :] = jnp.ones((VT_ROWS - 128, tm), BF16)


def _head_sum_matrix(width):
    i = jnp.arange(width) // HEAD_DIM
    return (i[:, None] == i[None, :]).astype(BF16)


def _inproj_call(x, mod6, norm1_gain, w_in, mu, w0, w_up, a0, a_up, g_up, k_k, k_a, tm):
    B, S, _ = x.shape
    ns = S // tm
    w_rw = w_in[:, :RWKV_COLS].astype(BF16)
    w_at = w_in[:, RWKV_COLS:].astype(BF16)
    zeros = jnp.zeros((64, RWKV_WIDTH), F32)
    wup_p = jnp.concatenate([w_up, zeros], axis=0).astype(BF16)
    aup_p = jnp.concatenate([zeros, a_up], axis=0).astype(BF16)
    row = lambda v: v.reshape(1, -1)
    full = lambda shape: pl.BlockSpec(shape, lambda b, s: (0,) * len(shape))
    tok = lambda w: pl.BlockSpec((1, tm, w), lambda b, s: (b, s, 0))
    rw_shape = jax.ShapeDtypeStruct((B, S, RWKV_WIDTH), BF16)
    out_shape = [rw_shape] * 6 + [
        jax.ShapeDtypeStruct((B, S, RWKV_WIDTH), F32),
        jax.ShapeDtypeStruct((B, S, DIFF_WIDTH), BF16),
        jax.ShapeDtypeStruct((B, S, DIFF_WIDTH), BF16),
        jax.ShapeDtypeStruct((B, DIFF_HEADS, ns, VT_ROWS, tm), BF16)]
    out_specs = [tok(RWKV_WIDTH)] * 7 + [tok(DIFF_WIDTH)] * 2 + [
        pl.BlockSpec((1, DIFF_HEADS, 1, VT_ROWS, tm), lambda b, s: (b, 0, s, 0, 0))]
    return pl.pallas_call(
        functools.partial(_inproj_kernel, tm=tm),
        grid=(B, ns),
        in_specs=[tok(D_MODEL),
                  pl.BlockSpec((1, 6, D_MODEL), lambda b, s: (b, 0, 0)),
                  full((1, D_MODEL)),
                  full((D_MODEL, RWKV_COLS)), full((D_MODEL, DIFF_COLS)),
                  full((1, RWKV_COLS)), full((1, RWKV_WIDTH)), full((128, RWKV_WIDTH)),
                  full((1, RWKV_WIDTH)), full((128, RWKV_WIDTH)), full((128, RWKV_WIDTH)),
                  full((1, RWKV_WIDTH)), full((1, RWKV_WIDTH)),
                  full((RWKV_WIDTH, RWKV_WIDTH))],
        out_specs=out_specs,
        out_shape=out_shape,
        scratch_shapes=[pltpu.VMEM((1, RWKV_COLS), F32)],
        compiler_params=pltpu.CompilerParams(
            dimension_semantics=("arbitrary", "arbitrary"), vmem_limit_bytes=VMEM_LIMIT),
        name="inproj",
    )(x, mod6, row(norm1_gain), w_rw, w_at, row(mu), row(w0), wup_p, row(a0), aup_p,
      g_up.astype(BF16), row(k_k), row(k_a), _head_sum_matrix(RWKV_WIDTH))


def _rwkv_kernel(r_ref, k_ref, v_ref, a_ref, b_ref, g_ref, lw_ref, rk_ref, gng_ref, gnb_ref,
                 hsum_ref, y_ref, s0_ref, s1_ref, *, nck):
    C = CHUNK
    n_grp = RWKV_HEADS // GROUP
    s_refs = (s0_ref, s1_ref)

    @pl.when(pl.program_id(1) == 0)
    def _():
        s0_ref[...] = jnp.zeros_like(s0_ref)
        s1_ref[...] = jnp.zeros_like(s1_ref)

    ti = lax.broadcasted_iota(I32, (C, GW), 0)
    si = lax.broadcasted_iota(I32, (C, GW), 1) % C
    incl = si <= ti
    strict = si < ti
    eye = (si == ti).astype(F32)
    bdmask = (lax.broadcasted_iota(I32, (GW, GW), 0) // HEAD_DIM ==
              lax.broadcasted_iota(I32, (GW, GW), 1) // HEAD_DIM)
    hsum = hsum_ref[...]
    chains = [(ck, g) for ck in range(nck) for g in range(n_grp)]

    bdmask_bf = bdmask.astype(BF16)

    def bd(xb):
        return jnp.concatenate([xb] * GROUP, axis=0) * bdmask_bf

    tri =(lax.broadcasted_iota(I32, (C, C), 0) >=
           lax.broadcasted_iota(I32, (C, C), 1)).astype(BF16)
    done = {}
    sv = [s_refs[g][...] for g in range(n_grp)]

    def chain(ch):
        ck, g = ch
        rows = slice(ck * C, (ck + 1) * C)
        cols = slice(g * GW, (g + 1) * GW)
        lw = lw_ref[0, rows, cols]
        L = _split_dot_left(tri, lw)
        yield
        Lx = L - lw
        Lc = L[C - 1:C, :]
        rho = L[C // 2 - 1:C // 2, :]
        r = r_ref[0, rows, cols].astype(F32)
        k = k_ref[0, rows, cols].astype(F32)
        vb = v_ref[0, rows, cols]
        a = a_ref[0, rows, cols].astype(F32)
        b = b_ref[0, rows, cols].astype(F32)
        e_k = jnp.exp(rho - L)
        lhs = jnp.concatenate([(r * jnp.exp(L - rho)).astype(BF16),
                               (a * jnp.exp(Lx - rho)).astype(BF16)], axis=0)
        kbd = bd((k * e_k).astype(BF16))
        bbd = bd((b * e_k).astype(BF16))
        r0 = (r * jnp.exp(L)).astype(BF16)
        a0bd = bd((a * jnp.exp(Lx)).astype(BF16))
        e_o = jnp.exp(Lc - L)
        bk = jnp.concatenate([(b * e_o).astype(BF16), (k * e_o).astype(BF16)], axis=0)
        vbd = bd(vb)
        yield
        ak = _dot_nt(lhs, kbd)
        ab = _dot_nt(lhs, bbd)
        yield
        a_rk = jnp.where(incl, ak[:C], 0.0).astype(BF16)
        a_ak = jnp.where(strict, ak[C:], 0.0).astype(BF16)
        a_rb = jnp.where(incl, ab[:C], 0.0).astype(BF16)
        n = jnp.where(strict, ab[C:], 0.0)
        p = eye + n
        xb = n.astype(BF16)
        xbd = bd(xb)
        yield
        x = _dot(xb, xbd)
        yield
        for _ in range(4):
            xb = x.astype(BF16)
            stack = jnp.concatenate([xb, p.astype(BF16)], axis=0)
            xbd = bd(xb)
            yield
            o = _dot(stack, xbd)
            yield
            x = o[:C]
            p = p + o[C:]
        pb = p.astype(BF16)
        xbd = bd(x.astype(BF16))
        yield
        o = _dot(pb, xbd)
        yield
        tb = (p + o).astype(BF16)
        yield
        av = _dot(jnp.concatenate([a_ak, a_rk], axis=0), vbd)
        w1 = _dot(tb, a0bd)
        yield
        akvbd = bd(av[:C].astype(BF16))
        rw_lhs = jnp.concatenate([r0, w1.astype(BF16)], axis=0)
        yield
        w2 = _dot(tb, akvbd)
        yield
        done[ch] = dict(rw_lhs=rw_lhs, w2=w2, a_rb=a_rb, arkv=av[C:], vb=vb, bk=bk,
                        gc=jnp.exp(Lc), rk=r * k)

    def sequential(ck):
        d = [done[(ck, g)] for g in range(n_grp)]
        rws = [_dot_nt(d[g]["rw_lhs"], sv[g].astype(BF16)) for g in range(n_grp)]
        yield
        ub = [(rws[g][C:] + d[g]["w2"]).astype(BF16) for g in range(n_grp)]
        uv = [jnp.concatenate([ub[g], d[g]["vb"]], axis=0) for g in range(n_grp)]
        ubd = [bd(ub[g]) for g in range(n_grp)]
        yield
        upd = [_dot_tn(uv[g], d[g]["bk"]) for g in range(n_grp)]
        yield
        for g in range(n_grp):
            sv[g] = sv[g] * d[g]["gc"] + jnp.where(bdmask, upd[g], 0.0)
        done[("state", ck)] = True
        y = [rws[g][:C] + _dot(d[g]["a_rb"], ubd[g]) + d[g]["arkv"] for g in range(n_grp)]
        yield
        mean = [_split_dot(y[g], hsum) * (1.0 / HEAD_DIM) for g in range(n_grp)]
        yield
        dev = [y[g] - mean[g] for g in range(n_grp)]
        var = [_split_dot(dev[g] * dev[g], hsum) * (1.0 / HEAD_DIM) for g in range(n_grp)]
        bonus = [_split_dot(d[g]["rk"] * rk_ref[:, g * GW:(g + 1) * GW], hsum) for g in range(n_grp)]
        yield
        rows = slice(ck * C, (ck + 1) * C)
        for g in range(n_grp):
            cols = slice(g * GW, (g + 1) * GW)
            yn = dev[g] * lax.rsqrt(var[g] + GN_EPS) * gng_ref[:, cols] + gnb_ref[:, cols]
            out = (yn + bonus[g] * d[g]["vb"].astype(F32)) * g_ref[0, rows, cols].astype(F32)
            y_ref[0, rows, cols] = out.astype(BF16)

    waiting = [chain(ch) for ch in chains]
    seq_next = 0
    running = []
    while waiting or running or seq_next < nck:
        ready = (seq_next < nck and all((seq_next, g) in done for g in range(n_grp))
                 and (seq_next == 0 or ("state", seq_next - 1) in done))
        if ready:
            running.append(sequential(seq_next))
            seq_next += 1
        elif waiting:
            running.append(waiting.pop(0))
        for gen in list(running):
            try:
                next(gen)
            except StopIteration:
                running.remove(gen)
    for g in range(n_grp):
        s_refs[g][...] = sv[g]


def _split_dot_left(w_bf16, x):
    hi = x.astype(BF16)
    lo = (x - hi.astype(F32)).astype(BF16)
    return _dot(w_bf16, hi) + _dot(w_bf16, lo)


def _rwkv_call(r, k, v, a, b, g, lw, r_k, gn_gain, gn_bias, nck):
    B, S, _ = r.shape
    tb = nck * CHUNK
    tok = pl.BlockSpec((1, tb, RWKV_WIDTH), lambda bb, c: (bb, c, 0))
    par = pl.BlockSpec((1, RWKV_WIDTH), lambda bb, c: (0, 0))
    return pl.pallas_call(
        functools.partial(_rwkv_kernel, nck=nck),
        grid=(B, S // tb),
        in_specs=[tok] * 7 + [par] * 3 + [pl.BlockSpec((GW, GW), lambda bb, c: (0, 0))],
        out_specs=tok,
        out_shape=jax.ShapeDtypeStruct((B, S, RWKV_WIDTH), BF16),
        scratch_shapes=[pltpu.VMEM((GW, GW), F32)] * (RWKV_HEADS // GROUP),
        compiler_params=pltpu.CompilerParams(
            dimension_semantics=("arbitrary", "arbitrary"), vmem_limit_bytes=VMEM_LIMIT),
        name="rwkv",
    )(r, k, v, a, b, g, lw, r_k.reshape(1, -1), gn_gain.reshape(1, -1), gn_bias.reshape(1, -1),
      _head_sum_matrix(GW))


def _tree_reduce(op, x):
    while x.shape[0] > 8:
        h = x.shape[0] // 2
        x = op(x[:h], x[h:])
    return x


def _attn_kernel(slopes_ref, q_ref, k_ref, vt_ref, lq1_ref, lk1_ref, lq2_ref, lk2_ref,
                 gain_ref, o_ref, sa_ref, sb_ref, acc_ref, m_ref, *, tq, tk):
    hh = pl.program_id(1)
    i = pl.program_id(2)
    slope = slopes_ref[hh]

    q = q_ref[0].astype(F32)
    lane = lax.broadcasted_iota(I32, (tq, 256), 1)
    qrow = lax.broadcasted_iota(I32, (tq, 256), 0)
    qa = (qrow >> 5).astype(F32) * (32.0 * slope)
    qb = (qrow & 31).astype(F32) * slope
    qbias = jnp.where(lane < 130, 1.0, jnp.where(lane == 130, -qa, jnp.where(lane == 131, -qb, 0.0)))
    qpad = jnp.concatenate([q, jnp.zeros_like(q)], axis=1)
    q_aug = []
    for c in range(2):
        in_comp = (lane >= c * HEAD_DIM) & (lane < (c + 1) * HEAD_DIM)
        q_aug.append(jnp.where(in_comp, qpad, jnp.where(lane >= 128, qbias, 0.0)))
    klane = lax.broadcasted_iota(I32, (tk, 128), 1)
    krow = lax.broadcasted_iota(I32, (tk, 128), 0)
    ka = (krow >> 5).astype(F32) * (32.0 * slope)
    kb_ = (krow & 31).astype(F32) * slope
    kbias = jnp.where(klane == 0, ka, jnp.where(klane == 1, kb_, jnp.where(klane < 4, 1.0, 0.0))
                      ).astype(BF16)

    qt_both = jnp.concatenate([q_aug[0].T, q_aug[1].T], axis=1).astype(BF16)

    m_ref[...] = jnp.full_like(m_ref, -jnp.inf)
    acc_ref[...] = jnp.zeros_like(acc_ref)

    def scores(j):
        kb = k_ref[0, pl.ds(pl.multiple_of(j * tk, tk), tk), :]
        return _dot(jnp.concatenate([kb, kbias], axis=1), qt_both)

    def absorb(s_ref, j, masked):
        s = s_ref[...]
        off = j * tk - i * tq
        cj = slope * off.astype(F32)
        if masked:
            keep = (lax.broadcasted_iota(I32, (tk, tq), 0) -
                    lax.broadcasted_iota(I32, (tk, tq), 1) + off) <= 0
            s = jnp.where(jnp.concatenate([keep, keep], axis=1), s, -jnp.inf)
        m_loc = jnp.max(_tree_reduce(jnp.maximum, s), axis=0, keepdims=True)
        m_old = m_ref[...]
        m_new = jnp.maximum(m_old, m_loc + cj)
        pr = jnp.exp((s - (m_new - cj)).astype(BF16))
        alpha = jnp.exp(m_old - m_new)
        acc_ref[...] = alpha * acc_ref[...] + _dot(vt_ref[0, 0, j], pr)
        m_ref[...] = m_new

    n_full = (i * tq) // tk
    sa_ref[...] = scores(0)

    def pair_body(jj, carry):
        j0 = 2 * jj
        sb_ref[...] = scores(j0 + 1)
        absorb(sa_ref, j0, False)
        sa_ref[...] = scores(j0 + 2)
        absorb(sb_ref, j0 + 1, False)
        return carry

    lax.fori_loop(0, n_full // 2, pair_body, 0)

    @pl.when(n_full % 2 == 1)
    def _():
        sb_ref[...] = scores(n_full)
        absorb(sa_ref, n_full - 1, False)
        absorb(sb_ref, n_full, True)

    @pl.when(n_full % 2 == 0)
    def _():
        absorb(sa_ref, n_full, True)

    lam = (jnp.exp(jnp.sum(lq1_ref[...] * lk1_ref[...], axis=-1, keepdims=True))
           - jnp.exp(jnp.sum(lq2_ref[...] * lk2_ref[...], axis=-1, keepdims=True))
           + LAMBDA_INIT)
    o2 = acc_ref[0:128, :] / acc_ref[128:129, :]
    o = o2[:, :tq] - lam * o2[:, tq:]
    ot = o.T
    ms = jnp.mean(ot * ot, axis=-1, keepdims=True)
    y = ot * lax.rsqrt(ms + NORM_EPS) * gain_ref[...] * (1.0 - LAMBDA_INIT)
    o_ref[0] = y.astype(BF16)


def _attn_call(q, k, vt, lq1, lk1, lq2, lk2, subln_gain, tq):
    B, S, _ = q.shape
    ns, tk = vt.shape[2], vt.shape[4]
    slopes = jnp.asarray([2.0 ** (-8.0 * (i + 1) / DIFF_HEADS) for i in range(DIFF_HEADS)], F32)
    vec = lambda n: pl.BlockSpec((1, n), lambda b, h, i, sl: (0, 0))
    grid_spec = pltpu.PrefetchScalarGridSpec(
        num_scalar_prefetch=1,
        grid=(B, DIFF_HEADS, S // tq),
        in_specs=[pl.BlockSpec((1, tq, 128), lambda b, h, i, sl: (b, i, h)),
                  pl.BlockSpec((1, S, 128), lambda b, h, i, sl: (b, 0, h)),
                  pl.BlockSpec((1, 1, ns, VT_ROWS, tk), lambda b, h, i, sl: (b, h, 0, 0, 0)),
                  vec(HEAD_DIM), vec(HEAD_DIM), vec(HEAD_DIM), vec(HEAD_DIM), vec(128)],
        out_specs=pl.BlockSpec((1, tq, 128), lambda b, h, i, sl: (b, i, h)),
        scratch_shapes=[pltpu.VMEM((tk, 2 * tq), F32), pltpu.VMEM((tk, 2 * tq), F32),
                        pltpu.VMEM((VT_ROWS, 2 * tq), F32), pltpu.VMEM((1, 2 * tq), F32)])
    return pl.pallas_call(
        functools.partial(_attn_kernel, tq=tq, tk=tk),
        grid_spec=grid_spec,
        out_shape=jax.ShapeDtypeStruct((B, S, DIFF_WIDTH), BF16),
        compiler_params=pltpu.CompilerParams(
            dimension_semantics=("arbitrary", "arbitrary", "arbitrary"),
            vmem_limit_bytes=VMEM_LIMIT),
        name="attn",
    )(slopes, q, k, vt, lq1.reshape(1, -1), lk1.reshape(1, -1), lq2.reshape(1, -1),
      lk2.reshape(1, -1), subln_gain.reshape(1, -1))


def _pack_bf16_pairs(x):
    w = x.shape[1] // 2
    lo = pltpu.bitcast(x[:, :w].astype(BF16).astype(F32), U32)
    hi = pltpu.bitcast(x[:, w:].astype(BF16).astype(F32), U32)
    return (lo >> 16) | (hi & jnp.uint32(0xFFFF0000))


def _unpack_bf16_pairs(p):
    lo = pltpu.bitcast(p << 16, F32)
    hi = pltpu.bitcast(p & jnp.uint32(0xFFFF0000), F32)
    return lo, hi


ROW_SUB = (D_MODEL // 2) // 128


def _store_rows(ref, x2d):
    n = x2d.shape[0]
    for s in range(ROW_SUB):
        ref[pl.ds(s, n, stride=ROW_SUB), :] = x2d[:, s * 128:(s + 1) * 128]


def _load_rows(ref):
    n = ref.shape[0] // ROW_SUB
    return jnp.concatenate([ref[pl.ds(s, n, stride=ROW_SUB), :] for s in range(ROW_SUB)], axis=1)


def _outproj_kernel(yr_ref, yd_ref, x_ref, mod_ref, wo_r_ref, wo_d_ref, g2_ref, rw_hi_ref,
                    rw_lo_ref, rb_ref, x1_out, hp_out, idx_out, w_out, rank_out, cnt_out,
                    carry_ref, *, tm):
    t = pl.program_id(0)

    @pl.when(t == 0)
    def _():
        carry_ref[...] = jnp.zeros_like(carry_ref)

    gate1 = mod_ref[0, 2:3, :]
    shift2 = mod_ref[0, 3:4, :]
    scale2 = mod_ref[0, 4:5, :]
    n_part = max(1, tm // 256)
    rows_per = tm // n_part
    eidx = lax.broadcasted_iota(I32, (rows_per, N_EXPERTS), 1)
    col4 = lax.broadcasted_iota(I32, (rows_per, TOP_K), 1)
    tri = (lax.broadcasted_iota(I32, (rows_per, rows_per), 0) >
           lax.broadcasted_iota(I32, (rows_per, rows_per), 1)).astype(BF16)
    before = [carry_ref[...]]

    def part(i):
        rs = slice(i * rows_per, (i + 1) * rows_per)
        mix = _dot(yr_ref[rs, :], wo_r_ref[...]) + _dot(yd_ref[rs, :], wo_d_ref[...])
        yield
        x1 = x_ref[rs, :] + gate1 * mix
        x1_out[rs, :] = x1
        ms = jnp.mean(x1 * x1, axis=-1, keepdims=True)
        h = x1 * lax.rsqrt(ms + NORM_EPS) * g2_ref[...] * (1.0 + scale2) + shift2
        _store_rows(hp_out.at[pl.ds(i * rows_per * ROW_SUB, rows_per * ROW_SUB), :],
                    _pack_bf16_pairs(h))
        hi = h.astype(BF16)
        lo = (h - hi.astype(F32)).astype(BF16)
        yield
        logits = (_dot(hi, rw_hi_ref[...]) + _dot(hi, rw_lo_ref[...]) + _dot(lo, rw_hi_ref[...])
                  + rb_ref[...])
        yield
        lg = logits
        vals, idxs = [], []
        onehot = jnp.zeros(logits.shape, F32)
        for _ in range(TOP_K):
            m = jnp.max(lg, axis=-1, keepdims=True)
            ix = jnp.min(jnp.where(lg == m, eidx, N_EXPERTS), axis=-1, keepdims=True)
            sel = eidx == ix
            vals.append(m)
            idxs.append(ix)
            onehot = onehot + sel.astype(F32)
            lg = jnp.where(sel, -jnp.inf, lg)
        es = [jnp.exp(v - vals[0]) for v in vals]
        den = es[0] + es[1] + es[2] + es[3]
        before.append(before[i] + jnp.sum(onehot, axis=0, keepdims=True))
        yield
        prefix = _dot(tri, onehot.astype(BF16)) + before[i]
        yield
        idx4 = jnp.zeros((rows_per, TOP_K), I32)
        w4 = jnp.zeros((rows_per, TOP_K), F32)
        rank4 = jnp.zeros((rows_per, TOP_K), I32)
        for kk in range(TOP_K):
            rk = jnp.sum(jnp.where(eidx == idxs[kk], prefix, 0.0), axis=-1, keepdims=True)
            idx4 = jnp.where(col4 == kk, idxs[kk], idx4)
            w4 = jnp.where(col4 == kk, es[kk] / den, w4)
            rank4 = jnp.where(col4 == kk, rk.astype(I32), rank4)
        idx_out[rs, :] = idx4
        w_out[rs, :] = w4
        rank_out[rs, :] = rank4

    _round_robin([part(i) for i in range(n_part)])
    carry_ref[...] = before[n_part]
    cnt_out[...] = before[n_part].astype(I32)


def _outproj_call(y_rwkv, y_diff, x, mod6, w_out, norm2_gain, router_w, router_b, S, tm):
    T = x.shape[0]
    tiles_per_seq = S // tm
    rw_hi = router_w.astype(BF16)
    rw_lo = (router_w - rw_hi.astype(F32)).astype(BF16)
    tok = lambda w: pl.BlockSpec((tm, w), lambda t: (t, 0))
    full = lambda shape: pl.BlockSpec(shape, lambda t: (0,) * len(shape))
    return pl.pallas_call(
        functools.partial(_outproj_kernel, tm=tm),
        grid=(T // tm,),
        in_specs=[tok(RWKV_WIDTH), tok(DIFF_WIDTH), tok(D_MODEL),
                  pl.BlockSpec((1, 6, D_MODEL), lambda t: (t // tiles_per_seq, 0, 0)),
                  full((RWKV_WIDTH, D_MODEL)), full((DIFF_WIDTH, D_MODEL)), full((1, D_MODEL)),
                  full((D_MODEL, N_EXPERTS)), full((D_MODEL, N_EXPERTS)), full((1, N_EXPERTS))],
        out_specs=[tok(D_MODEL), pl.BlockSpec((tm * ROW_SUB, 128), lambda t: (t, 0)),
                   tok(TOP_K), tok(TOP_K), tok(TOP_K), full((1, N_EXPERTS))],
        out_shape=[jax.ShapeDtypeStruct((T, D_MODEL), F32),
                   jax.ShapeDtypeStruct((T * ROW_SUB, 128), U32),
                   jax.ShapeDtypeStruct((T, TOP_K), I32),
                   jax.ShapeDtypeStruct((T, TOP_K), F32),
                   jax.ShapeDtypeStruct((T, TOP_K), I32),
                   jax.ShapeDtypeStruct((1, N_EXPERTS), I32)],
        scratch_shapes=[pltpu.VMEM((1, N_EXPERTS), F32)],
        compiler_params=pltpu.CompilerParams(
            dimension_semantics=("arbitrary",), vmem_limit_bytes=VMEM_LIMIT),
        name="outproj",
    )(y_rwkv, y_diff, x, mod6, w_out[:RWKV_WIDTH].astype(BF16), w_out[RWKV_WIDTH:].astype(BF16),
      norm2_gain.reshape(1, -1), rw_hi, rw_lo, router_b.reshape(1, -1))


def _experts_kernel(be_ref, nu_ref, valid_ref, xs_ref, wgu_ref, bgu_ref, wd_ref, bd_ref, ys_ref,
                    wgu_bf, wd_bf):
    i = pl.program_id(0)

    @pl.when((i == 0) | (be_ref[i] != be_ref[jnp.maximum(i - 1, 0)]))
    def _():
        wgu_bf[...] = wgu_ref[0].astype(BF16)
        wd_bf[...] = wd_ref[0].astype(BF16)

    @pl.when(i < nu_ref[0])
    def _():
        packed = _load_rows(xs_ref)
        row = lax.broadcasted_iota(I32, packed.shape, 0)
        packed = jnp.where(row < valid_ref[i], packed, jnp.uint32(0))
        xa, xb = _unpack_bf16_pairs(packed)
        x = jnp.concatenate([xa.astype(BF16), xb.astype(BF16)], axis=1)
        gu = _dot(x, wgu_bf[...]) + bgu_ref[0]
        gate = jnp.minimum(gu[:, :D_EXPERT], SWIGLU_LIMIT)
        up = jnp.clip(gu[:, D_EXPERT:], -SWIGLU_LIMIT, SWIGLU_LIMIT)
        act = (up + 1.0) * (gate * jax.nn.sigmoid(SWIGLU_ALPHA * gate))
        y = _dot(act.astype(BF16), wd_bf[...]) + bd_ref[0]
        _store_rows(ys_ref, _pack_bf16_pairs(y))

    @pl.when(i >= nu_ref[0])
    def _():
        ys_ref[...] = jnp.zeros_like(ys_ref)


def _experts_call(xs, block_e, n_used, valid_rows, w_gate_up, b_gate_up, w_down, b_down, n_blocks):
    per_expert = lambda shape: pl.BlockSpec(shape, lambda i, be, nu, vr: (be[i], 0, 0))
    grid_spec = pltpu.PrefetchScalarGridSpec(
        num_scalar_prefetch=3,
        grid=(n_blocks,),
        in_specs=[pl.BlockSpec((MOE_BLOCK * ROW_SUB, 128),
                               lambda i, be, nu, vr: (jnp.minimum(i, nu[0] - 1), 0)),
                  per_expert((1, D_MODEL, 2 * D_EXPERT)), per_expert((1, 1, 2 * D_EXPERT)),
                  per_expert((1, D_EXPERT, D_MODEL)), per_expert((1, 1, D_MODEL))],
        out_specs=pl.BlockSpec((MOE_BLOCK * ROW_SUB, 128), lambda i, be, nu, vr: (i, 0)),
        scratch_shapes=[pltpu.VMEM((D_MODEL, 2 * D_EXPERT), BF16),
                        pltpu.VMEM((D_EXPERT, D_MODEL), BF16)])
    return pl.pallas_call(
        _experts_kernel,
        grid_spec=grid_spec,
        out_shape=jax.ShapeDtypeStruct((n_blocks * MOE_BLOCK * ROW_SUB, 128), U32),
        compiler_params=pltpu.CompilerParams(
            dimension_semantics=("arbitrary",), vmem_limit_bytes=VMEM_LIMIT),
        name="experts",
    )(block_e, n_used, valid_rows, xs, w_gate_up, b_gate_up.reshape(N_EXPERTS, 1, -1), w_down,
      b_down.reshape(N_EXPERTS, 1, -1))


SC_WINDOW = 128
COMBINE_PARTS = 4


def _sc_mesh():
    return plsc.VectorSubcoreMesh(core_axis_name="c", subcore_axis_name="s")


def _sc_worker_chunks(n_chunks):
    info = plsc.get_sparse_core_info()
    n_workers = info.num_cores * info.num_subcores
    assert n_chunks % n_workers == 0
    return info.num_cores, n_chunks // n_workers


def _sc_scatter_call(rows, slot_chunks, n_out_rows):
    T = rows.shape[0]
    per_pass = T // SC_WINDOW
    n_cores, per_worker = _sc_worker_chunks(per_pass)

    @functools.partial(
        pl.kernel, mesh=_sc_mesh(),
        out_type=jax.ShapeDtypeStruct((n_out_rows, ROW_SUB, 128), U32),
        scratch_types=[pltpu.VMEM((SC_WINDOW,), I32), pltpu.VMEM((SC_WINDOW, ROW_SUB, 128), U32)],
        name="sc_dispatch")
    def run(rows_hbm, idx_hbm, out_hbm, idx_v, rows_v):
        wid = lax.axis_index("s") * n_cores + lax.axis_index("c")

        @pl.loop(0, per_worker)
        def _(j):
            chunk = wid * per_worker + j
            pltpu.sync_copy(rows_hbm.at[pl.ds(chunk * SC_WINDOW, SC_WINDOW)], rows_v)
            for kk in range(TOP_K):
                pltpu.sync_copy(idx_hbm.at[kk * per_pass + chunk], idx_v)
                pltpu.sync_copy(rows_v, out_hbm.at[idx_v])

    return run(rows, slot_chunks)


def _sc_gather_call(table, slot_chunks):
    n_chunks = slot_chunks.shape[0]
    n_cores, per_worker = _sc_worker_chunks(n_chunks)

    @functools.partial(
        pl.kernel, mesh=_sc_mesh(),
        out_type=jax.ShapeDtypeStruct((n_chunks * SC_WINDOW, ROW_SUB, 128), U32),
        scratch_types=[pltpu.VMEM((SC_WINDOW,), I32), pltpu.VMEM((SC_WINDOW, ROW_SUB, 128), U32)],
        name="sc_collect")
    def run(table_hbm, idx_hbm, out_hbm, idx_v, rows_v):
        wid = lax.axis_index("s") * n_cores + lax.axis_index("c")

        @pl.loop(0, per_worker)
        def _(j):
            chunk = wid * per_worker + j
            pltpu.sync_copy(idx_hbm.at[chunk], idx_v)
            pltpu.sync_copy(table_hbm.at[idx_v], rows_v)
            pltpu.sync_copy(rows_v, out_hbm.at[pl.ds(chunk * SC_WINDOW, SC_WINDOW)])

    return run(table, slot_chunks)


def _combine_dense_kernel(g0_ref, g1_ref, g2_ref, g3_ref, x1_ref, w_ref, mod_ref, fg_ref, o_ref,
                          *, tm):
    w = w_ref[...]
    acc_lo = jnp.zeros((tm, D_MODEL // 2), F32)
    acc_hi = jnp.zeros((tm, D_MODEL // 2), F32)
    for kk, g_ref in enumerate((g0_ref, g1_ref, g2_ref, g3_ref)):
        lo, hi = _unpack_bf16_pairs(_load_rows(g_ref))
        wk = w[:, kk:kk + 1]
        acc_lo = acc_lo + wk * lo
        acc_hi = acc_hi + wk * hi
    moe = jnp.concatenate([acc_lo, acc_hi], axis=1)
    gate2 = mod_ref[0, 5:6, :]
    x2 = x1_ref[...] + gate2 * moe
    ms = jnp.mean(x2 * x2, axis=-1, keepdims=True)
    o_ref[...] = x2 * lax.rsqrt(ms + NORM_EPS) * fg_ref[...]


def _combine_dense_call(gathered, x1, top_w, mod6, final_gain, S, tm, part, n_parts):
    T = x1.shape[0]
    assert T % (tm * n_parts) == 0
    nt = T // tm // n_parts
    first = part * nt
    tiles_per_seq = S // tm
    rows = lambda kk: pl.BlockSpec((tm * ROW_SUB, 128), lambda t: (kk * nt + t, 0))
    return pl.pallas_call(
        functools.partial(_combine_dense_kernel, tm=tm),
        grid=(nt,),
        in_specs=[rows(0), rows(1), rows(2), rows(3),
                  pl.BlockSpec((tm, D_MODEL), lambda t: (first + t, 0)),
                  pl.BlockSpec((tm, TOP_K), lambda t: (first + t, 0)),
                  pl.BlockSpec((1, 6, D_MODEL), lambda t: ((first + t) // tiles_per_seq, 0, 0)),
                  pl.BlockSpec((1, D_MODEL), lambda t: (0, 0))],
        out_specs=pl.BlockSpec((tm, D_MODEL), lambda t: (first + t, 0)),
        out_shape=jax.ShapeDtypeStruct((T, D_MODEL), F32),
        input_output_aliases={4: 0},
        compiler_params=pltpu.CompilerParams(
            dimension_semantics=("arbitrary",), vmem_limit_bytes=VMEM_LIMIT),
        name="combine",
    )(gathered, gathered, gathered, gathered, x1, top_w, mod6, final_gain.reshape(1, -1))


def _forward(x, c, mod_w, mod_b, norm1_gain, w_in, rwkv_shift_mu, rwkv_w0, rwkv_w_up, rwkv_a0,
             rwkv_a_up, rwkv_g_up, rwkv_k_k, rwkv_k_a, rwkv_r_k, rwkv_gn_gain, rwkv_gn_bias,
             diff_lambda_q1, diff_lambda_k1, diff_lambda_q2, diff_lambda_k2, diff_subln_gain,
             w_out, norm2_gain, router_w, router_b, w_gate_up, b_gate_up, w_down, b_down,
             final_gain):
    B, S, D = x.shape
    T = B * S
    tm_in = min(512, S)
    tq = min(512, S)
    tm_out = min(1024, S)
    tm = min(256, S)

    mod6 = _mod_call(c, mod_w, mod_b).reshape(B, 6, D)
    (r, k, v, a, b, g, lw, q, kat, vt) = _inproj_call(
        x, mod6, norm1_gain, w_in, rwkv_shift_mu, rwkv_w0, rwkv_w_up, rwkv_a0, rwkv_a_up,
        rwkv_g_up, rwkv_k_k, rwkv_k_a, tm_in)
    y_rwkv = _rwkv_call(r, k, v, a, b, g, lw, rwkv_r_k, rwkv_gn_gain, rwkv_gn_bias,
                        min(16, S // CHUNK))
    y_diff = _attn_call(q, kat, vt, diff_lambda_q1, diff_lambda_k1, diff_lambda_q2,
                        diff_lambda_k2, diff_subln_gain, tq)

    x1, hp, top_idx, top_w, rank, counts = _outproj_call(
        y_rwkv.reshape(T, -1), y_diff.reshape(T, -1), x.reshape(T, D), mod6, w_out, norm2_gain,
        router_w, router_b, S, tm_out)

    counts = counts.reshape(N_EXPERTS)
    padded = ((counts + MOE_BLOCK - 1) // MOE_BLOCK) * MOE_BLOCK
    pad_ends = jnp.cumsum(padded)
    pad_starts = pad_ends - padded
    n_blocks = -(-(T * TOP_K + N_EXPERTS * (MOE_BLOCK - 1)) // MOE_BLOCK)
    n_used = (pad_ends[-1] // MOE_BLOCK).astype(I32).reshape(1)
    block_start = jnp.minimum(jnp.arange(n_blocks, dtype=I32), n_used[0] - 1) * MOE_BLOCK
    block_e = jnp.minimum(jnp.sum(pad_ends[None, :] <= block_start[:, None], axis=1),
                          N_EXPERTS - 1).astype(I32)
    expert_ids = jnp.arange(N_EXPERTS, dtype=I32)
    start_of = jnp.sum(jnp.where(top_idx[..., None] == expert_ids, pad_starts.astype(I32), 0), axis=-1)
    slots = start_of + rank
    slot_chunks = slots.T.reshape(TOP_K * T // SC_WINDOW, SC_WINDOW)
    block_row = jnp.arange(n_blocks, dtype=I32) * MOE_BLOCK
    valid_rows = jnp.clip(counts.astype(I32)[block_e] - (block_row - pad_starts.astype(I32)[block_e]),
                          0, MOE_BLOCK).astype(I32)

    as_tokens = lambda z: z.reshape(-1, ROW_SUB, 128)
    as_lines = lambda z: z.reshape(-1, 128)
    xs = as_lines(_sc_scatter_call(as_tokens(hp), slot_chunks, n_blocks * MOE_BLOCK))
    ys = _experts_call(xs, block_e, n_used, valid_rows, w_gate_up, b_gate_up, w_down, b_down,
                       n_blocks)
    ys_tokens = as_tokens(ys)
    out = x1
    per_part = T // COMBINE_PARTS
    for part in range(COMBINE_PARTS):
        part_slots = slots[part * per_part:(part + 1) * per_part]
        part_chunks = part_slots.T.reshape(TOP_K * per_part // SC_WINDOW, SC_WINDOW)
        gathered = as_lines(_sc_gather_call(ys_tokens, part_chunks))
        out = _combine_dense_call(gathered, out, top_w, mod6, final_gain, S, tm, part, COMBINE_PARTS)
    return out.reshape(B, S, D)


def kernel(x, c, mod_w, mod_b, norm1_gain, w_in, rwkv_shift_mu, rwkv_w0, rwkv_w_up, rwkv_a0, rwkv_a_up, rwkv_g_up, rwkv_k_k, rwkv_k_a, rwkv_r_k, rwkv_gn_gain, rwkv_gn_bias, diff_lambda_q1, diff_lambda_k1, diff_lambda_q2, diff_lambda_k2, diff_subln_gain, w_out, norm2_gain, router_w, router_b, w_gate_up, b_gate_up, w_down, b_down, final_gain):
    return _forward(x, c, mod_w[0], mod_b[0], norm1_gain[0], w_in[0], rwkv_shift_mu[0], rwkv_w0[0],
                    rwkv_w_up[0], rwkv_a0[0], rwkv_a_up[0], rwkv_g_up[0], rwkv_k_k[0], rwkv_k_a[0],
                    rwkv_r_k[0].reshape(-1), rwkv_gn_gain[0], rwkv_gn_bias[0], diff_lambda_q1[0],
                    diff_lambda_k1[0], diff_lambda_q2[0], diff_lambda_k2[0], diff_subln_gain[0],
                    w_out[0], norm2_gain[0], router_w[0], router_b[0], w_gate_up[0], b_gate_up[0],
                    w_down[0], b_down[0], final_gain)
```

```python
import functools
import math

import jax
import jax.numpy as jnp
from jax import lax
from jax.experimental import pallas as pl
from jax.experimental.pallas import tpu as pltpu
from jax.experimental.pallas import tpu_sc as plsc

F32 = jnp.float32
BF16 = jnp.bfloat16
I32 = jnp.int32
U32 = jnp.uint32

D_MODEL = 1024
RWKV_WIDTH = 512
RWKV_HEADS = 8
HEAD_DIM = 64
RWKV_COLS = 3 * RWKV_WIDTH + 64 + 64 + 128
DIFF_WIDTH = 512
DIFF_HEADS = 4
DIFF_COLS = 3 * DIFF_WIDTH
N_EXPERTS = 32
TOP_K = 4
D_EXPERT = 1024
SWIGLU_LIMIT = 7.0
SWIGLU_ALPHA = 1.702
MOE_BLOCK = 512
NORM_EPS = 1e-5
GN_EPS = 64e-5
LAMBDA_INIT = 0.8 - 0.6 * math.exp(-0.3 * 0)

CHUNK = 64
GROUP = 4
GW = GROUP * HEAD_DIM
VT_ROWS = 128 + 16
VMEM_LIMIT = 56 * 1024 * 1024


def _dot(a, b):
    return jnp.dot(a, b, preferred_element_type=F32)


def _dot_nt(a, b):
    return lax.dot_general(a, b, (((1,), (1,)), ((), ())), preferred_element_type=F32)


def _dot_tn(a, b):
    return lax.dot_general(a, b, (((0,), (0,)), ((), ())), preferred_element_type=F32)


def _round_robin(stages):
    waiting = list(stages)
    running = []
    while waiting or running:
        if waiting:
            running.append(waiting.pop(0))
        for gen in list(running):
            try:
                next(gen)
            except StopIteration:
                running.remove(gen)


def _split_dot(x, w_bf16):
    hi = x.astype(BF16)
    lo = (x - hi.astype(F32)).astype(BF16)
    return _dot(hi, w_bf16) + _dot(lo, w_bf16)


def _mod_kernel(c_ref, w_ref, b_ref, o_ref):
    c = c_ref[...]
    s = c * jax.nn.sigmoid(c)
    o_ref[...] = _dot(s, w_ref[...]) + b_ref[...]


def _mod_call(c, mod_w, mod_b):
    B = c.shape[0]
    n = mod_w.shape[1]
    tn = 1536
    return pl.pallas_call(
        _mod_kernel,
        grid=(n // tn,),
        in_specs=[pl.BlockSpec((B, D_MODEL), lambda j: (0, 0)),
                  pl.BlockSpec((D_MODEL, tn), lambda j: (0, j)),
                  pl.BlockSpec((1, tn), lambda j: (0, j))],
        out_specs=pl.BlockSpec((B, tn), lambda j: (0, j)),
        out_shape=jax.ShapeDtypeStruct((B, n), F32),
        compiler_params=pltpu.CompilerParams(
            dimension_semantics=("arbitrary",), vmem_limit_bytes=VMEM_LIMIT),
        name="mod",
    )(c, mod_w, mod_b.reshape(1, n))


def _inproj_kernel(x_ref, mod_ref, g1_ref, wrw_ref, wat_ref, mu_ref, w0_ref, wup_ref,
                   a0_ref, aup_ref, gup_ref, kk_ref, ka_ref, hsum_ref,
                   r_out, k_out, v_out, a_out, b_out, g_out, lw_out,
                   q_out, kat_out, vt_out, carry_ref, *, tm):
    s = pl.program_id(1)
    shift1 = mod_ref[0, 0:1, :]
    scale1 = mod_ref[0, 1:2, :]

    @pl.when(s == 0)
    def _():
        carry_ref[...] = jnp.zeros_like(carry_ref)

    n_part = max(1, tm // 128)
    rows_per = tm // n_part
    last_row = [carry_ref[...]]

    def part(i):
        rs = slice(i * rows_per, (i + 1) * rows_per)
        x = x_ref[0, rs, :]
        ms = jnp.mean(x * x, axis=-1, keepdims=True)
        h = x * lax.rsqrt(ms + NORM_EPS) * g1_ref[...] * (1.0 + scale1) + shift1
        hb = h.astype(BF16)
        yield
        p = _dot(hb, wrw_ref[...])
        last_row.append(p[rows_per - 1:rows_per, :])
        yield
        pa = _dot(hb, wat_ref[...])
        yield
        rolled = pltpu.roll(p, shift=1, axis=0)
        row = lax.broadcasted_iota(I32, p.shape, 0)
        prev = jnp.where(row == 0, last_row[i], rolled)
        ps = p + mu_ref[...] * (prev - p)
        r = ps[:, 0:512]
        k = ps[:, 512:1024]
        v = ps[:, 1024:1536]
        lo2 = ps[:, 1536:1664]
        g_lo = ps[:, 1664:1792]
        kk = k * kk_ref[...]
        yield
        z = w0_ref[...] + _dot(jnp.tanh(lo2).astype(BF16), wup_ref[...])
        a_pre = a0_ref[...] + _dot(lo2.astype(BF16), aup_ref[...])
        g = _dot(jax.nn.sigmoid(g_lo).astype(BF16), gup_ref[...])
        ssq = _split_dot(kk * kk, hsum_ref[...])
        yield
        nz = -z
        softplus = jnp.maximum(nz, 0.0) + jnp.log(1.0 + jnp.exp(-jnp.abs(nz)))
        w = -softplus - 0.5
        lw_out[0, rs, :] = -jnp.exp(w)
        a = jax.nn.sigmoid(a_pre)
        kk = kk / jnp.maximum(jnp.sqrt(ssq), 1e-12)
        k = k * (1.0 + (a - 1.0) * ka_ref[...])
        r_out[0, rs, :] = r.astype(BF16)
        k_out[0, rs, :] = k.astype(BF16)
        v_out[0, rs, :] = v.astype(BF16)
        a_out[0, rs, :] = (-kk).astype(BF16)
        b_out[0, rs, :] = (kk * a).astype(BF16)
        g_out[0, rs, :] = g.astype(BF16)
        yield
        q_out[0, rs, :] = (pa[:, 0:512] * (1.0 / math.sqrt(HEAD_DIM))).astype(BF16)
        kat_out[0, rs, :] = pa[:, 512:1024].astype(BF16)
        for hh in range(DIFF_HEADS):
            vh = pa[:, 1024 + hh * 128:1024 + (hh + 1) * 128]
            vt_out[0, hh, 0, 0:128, rs] = vh.T.astype(BF16)

    _round_robin([part(i) for i in range(n_part)])
    carry_ref[...] = last_row[n_part]
    for hh in range(DIFF_HEADS):
        vt_out[0, hh, 0, 128:VT_ROWS, :] = jnp.ones((VT_ROWS - 128, tm), BF16)


def _head_sum_matrix(width):
    i = jnp.arange(width) // HEAD_DIM
    return (i[:, None] == i[None, :]).astype(BF16)


def _inproj_call(x, mod6, norm1_gain, w_in, mu, w0, w_up, a0, a_up, g_up, k_k, k_a, tm):
    B, S, _ = x.shape
    ns = S // tm
    w_rw = w_in[:, :RWKV_COLS].astype(BF16)
    w_at = w_in[:, RWKV_COLS:].astype(BF16)
    zeros = jnp.zeros((64, RWKV_WIDTH), F32)
    wup_p = jnp.concatenate([w_up, zeros], axis=0).astype(BF16)
    aup_p = jnp.concatenate([zeros, a_up], axis=0).astype(BF16)
    row = lambda v: v.reshape(1, -1)
    full = lambda shape: pl.BlockSpec(shape, lambda b, s: (0,) * len(shape))
    tok = lambda w: pl.BlockSpec((1, tm, w), lambda b, s: (b, s, 0))
    rw_shape = jax.ShapeDtypeStruct((B, S, RWKV_WIDTH), BF16)
    out_shape = [rw_shape] * 6 + [
        jax.ShapeDtypeStruct((B, S, RWKV_WIDTH), F32),
        jax.ShapeDtypeStruct((B, S, DIFF_WIDTH), BF16),
        jax.ShapeDtypeStruct((B, S, DIFF_WIDTH), BF16),
        jax.ShapeDtypeStruct((B, DIFF_HEADS, ns, VT_ROWS, tm), BF16)]
    out_specs = [tok(RWKV_WIDTH)] * 7 + [tok(DIFF_WIDTH)] * 2 + [
        pl.BlockSpec((1, DIFF_HEADS, 1, VT_ROWS, tm), lambda b, s: (b, 0, s, 0, 0))]
    return pl.pallas_call(
        functools.partial(_inproj_kernel, tm=tm),
        grid=(B, ns),
        in_specs=[tok(D_MODEL),
                  pl.BlockSpec((1, 6, D_MODEL), lambda b, s: (b, 0, 0)),
                  full((1, D_MODEL)),
                  full((D_MODEL, RWKV_COLS)), full((D_MODEL, DIFF_COLS)),
                  full((1, RWKV_COLS)), full((1, RWKV_WIDTH)), full((128, RWKV_WIDTH)),
                  full((1, RWKV_WIDTH)), full((128, RWKV_WIDTH)), full((128, RWKV_WIDTH)),
                  full((1, RWKV_WIDTH)), full((1, RWKV_WIDTH)),
                  full((RWKV_WIDTH, RWKV_WIDTH))],
        out_specs=out_specs,
        out_shape=out_shape,
        scratch_shapes=[pltpu.VMEM((1, RWKV_COLS), F32)],
        compiler_params=pltpu.CompilerParams(
            dimension_semantics=("arbitrary", "arbitrary"), vmem_limit_bytes=VMEM_LIMIT),
        name="inproj",
    )(x, mod6, row(norm1_gain), w_rw, w_at, row(mu), row(w0), wup_p, row(a0), aup_p,
      g_up.astype(BF16), row(k_k), row(k_a), _head_sum_matrix(RWKV_WIDTH))


def _rwkv_kernel(r_ref, k_ref, v_ref, a_ref, b_ref, g_ref, lw_ref, rk_ref, gng_ref, gnb_ref,
                 hsum_ref, y_ref, s0_ref, s1_ref, *, nck):
    C = CHUNK
    n_grp = RWKV_HEADS // GROUP
    s_refs = (s0_ref, s1_ref)

    @pl.when(pl.program_id(1) == 0)
    def _():
        s0_ref[...] = jnp.zeros_like(s0_ref)
        s1_ref[...] = jnp.zeros_like(s1_ref)

    ti = lax.broadcasted_iota(I32, (C, GW), 0)
    si = lax.broadcasted_iota(I32, (C, GW), 1) % C
    incl = si <= ti
    strict = si < ti
    eye = (si == ti).astype(F32)
    bdmask = (lax.broadcasted_iota(I32, (GW, GW), 0) // HEAD_DIM ==
              lax.broadcasted_iota(I32, (GW, GW), 1) // HEAD_DIM)
    hsum = hsum_ref[...]
    chains = [(ck, g) for ck in range(nck) for g in range(n_grp)]

    bdmask_bf = bdmask.astype(BF16)

    def bd(xb):
        return jnp.concatenate([xb] * GROUP, axis=0) * bdmask_bf

    tri =(lax.broadcasted_iota(I32, (C, C), 0) >=
           lax.broadcasted_iota(I32, (C, C), 1)).astype(BF16)
    done = {}
    sv = [s_refs[g][...] for g in range(n_grp)]

    def chain(ch):
        ck, g = ch
        rows = slice(ck * C, (ck + 1) * C)
        cols = slice(g * GW, (g + 1) * GW)
        lw = lw_ref[0, rows, cols]
        L = _split_dot_left(tri, lw)
        yield
        Lx = L - lw
        Lc = L[C - 1:C, :]
        rho = L[C // 2 - 1:C // 2, :]
        r = r_ref[0, rows, cols].astype(F32)
        k = k_ref[0, rows, cols].astype(F32)
        vb = v_ref[0, rows, cols]
        a = a_ref[0, rows, cols].astype(F32)
        b = b_ref[0, rows, cols].astype(F32)
        e_k = jnp.exp(rho - L)
        lhs = jnp.concatenate([(r * jnp.exp(L - rho)).astype(BF16),
                               (a * jnp.exp(Lx - rho)).astype(BF16)], axis=0)
        kbd = bd((k * e_k).astype(BF16))
        bbd = bd((b * e_k).astype(BF16))
        r0 = (r * jnp.exp(L)).astype(BF16)
        a0bd = bd((a * jnp.exp(Lx)).astype(BF16))
        e_o = jnp.exp(Lc - L)
        bk = jnp.concatenate([(b * e_o).astype(BF16), (k * e_o).astype(BF16)], axis=0)
        vbd = bd(vb)
        yield
        ak = _dot_nt(lhs, kbd)
        ab = _dot_nt(lhs, bbd)
        yield
        a_rk = jnp.where(incl, ak[:C], 0.0).astype(BF16)
        a_ak = jnp.where(strict, ak[C:], 0.0).astype(BF16)
        a_rb = jnp.where(incl, ab[:C], 0.0).astype(BF16)
        n = jnp.where(strict, ab[C:], 0.0)
        p = eye + n
        xb = n.astype(BF16)
        xbd = bd(xb)
        yield
        x = _dot(xb, xbd)
        yield
        for _ in range(4):
            xb = x.astype(BF16)
            stack = jnp.concatenate([xb, p.astype(BF16)], axis=0)
            xbd = bd(xb)
            yield
            o = _dot(stack, xbd)
            yield
            x = o[:C]
            p = p + o[C:]
        pb = p.astype(BF16)
        xbd = bd(x.astype(BF16))
        yield
        o = _dot(pb, xbd)
        yield
        tb = (p + o).astype(BF16)
        yield
        av = _dot(jnp.concatenate([a_ak, a_rk], axis=0), vbd)
        w1 = _dot(tb, a0bd)
        yield
        akvbd = bd(av[:C].astype(BF16))
        rw_lhs = jnp.concatenate([r0, w1.astype(BF16)], axis=0)
        yield
        w2 = _dot(tb, akvbd)
        yield
        done[ch] = dict(rw_lhs=rw_lhs, w2=w2, a_rb=a_rb, arkv=av[C:], vb=vb, bk=bk,
                        gc=jnp.exp(Lc), rk=r * k)

    def sequential(ck):
        d = [done[(ck, g)] for g in range(n_grp)]
        rws = [_dot_nt(d[g]["rw_lhs"], sv[g].astype(BF16)) for g in range(n_grp)]
        yield
        ub = [(rws[g][C:] + d[g]["w2"]).astype(BF16) for g in range(n_grp)]
        uv = [jnp.concatenate([ub[g], d[g]["vb"]], axis=0) for g in range(n_grp)]
        ubd = [bd(ub[g]) for g in range(n_grp)]
        yield
        upd = [_dot_tn(uv[g], d[g]["bk"]) for g in range(n_grp)]
        yield
        for g in range(n_grp):
            sv[g] = sv[g] * d[g]["gc"] + jnp.where(bdmask, upd[g], 0.0)
        done[("state", ck)] = True
        y = [rws[g][:C] + _dot(d[g]["a_rb"], ubd[g]) + d[g]["arkv"] for g in range(n_grp)]
        yield
        mean = [_split_dot(y[g], hsum) * (1.0 / HEAD_DIM) for g in range(n_grp)]
        yield
        dev = [y[g] - mean[g] for g in range(n_grp)]
        var = [_split_dot(dev[g] * dev[g], hsum) * (1.0 / HEAD_DIM) for g in range(n_grp)]
        bonus = [_split_dot(d[g]["rk"] * rk_ref[:, g * GW:(g + 1) * GW], hsum) for g in range(n_grp)]
        yield
        rows = slice(ck * C, (ck + 1) * C)
        for g in range(n_grp):
            cols = slice(g * GW, (g + 1) * GW)
            yn = dev[g] * lax.rsqrt(var[g] + GN_EPS) * gng_ref[:, cols] + gnb_ref[:, cols]
            out = (yn + bonus[g] * d[g]["vb"].astype(F32)) * g_ref[0, rows, cols].astype(F32)
            y_ref[0, rows, cols] = out.astype(BF16)

    waiting = [chain(ch) for ch in chains]
    seq_next = 0
    running = []
    while waiting or running or seq_next < nck:
        ready = (seq_next < nck and all((seq_next, g) in done for g in range(n_grp))
                 and (seq_next == 0 or ("state", seq_next - 1) in done))
        if ready:
            running.append(sequential(seq_next))
            seq_next += 1
        elif waiting:
            running.append(waiting.pop(0))
        for gen in list(running):
            try:
                next(gen)
            except StopIteration:
                running.remove(gen)
    for g in range(n_grp):
        s_refs[g][...] = sv[g]


def _split_dot_left(w_bf16, x):
    hi = x.astype(BF16)
    lo = (x - hi.astype(F32)).astype(BF16)
    return _dot(w_bf16, hi) + _dot(w_bf16, lo)


def _rwkv_call(r, k, v, a, b, g, lw, r_k, gn_gain, gn_bias, nck):
    B, S, _ = r.shape
    tb = nck * CHUNK
    tok = pl.BlockSpec((1, tb, RWKV_WIDTH), lambda bb, c: (bb, c, 0))
    par = pl.BlockSpec((1, RWKV_WIDTH), lambda bb, c: (0, 0))
    return pl.pallas_call(
        functools.partial(_rwkv_kernel, nck=nck),
        grid=(B, S // tb),
        in_specs=[tok] * 7 + [par] * 3 + [pl.BlockSpec((GW, GW), lambda bb, c: (0, 0))],
        out_specs=tok,
        out_shape=jax.ShapeDtypeStruct((B, S, RWKV_WIDTH), BF16),
        scratch_shapes=[pltpu.VMEM((GW, GW), F32)] * (RWKV_HEADS // GROUP),
        compiler_params=pltpu.CompilerParams(
            dimension_semantics=("arbitrary", "arbitrary"), vmem_limit_bytes=VMEM_LIMIT),
        name="rwkv",
    )(r, k, v, a, b, g, lw, r_k.reshape(1, -1), gn_gain.reshape(1, -1), gn_bias.reshape(1, -1),
      _head_sum_matrix(GW))


def _tree_reduce(op, x):
    while x.shape[0] > 8:
        h = x.shape[0] // 2
        x = op(x[:h], x[h:])
    return x


def _attn_kernel(slopes_ref, q_ref, k_ref, vt_ref, lq1_ref, lk1_ref, lq2_ref, lk2_ref,
                 gain_ref, o_ref, sa_ref, sb_ref, acc_ref, m_ref, *, tq, tk):
    hh = pl.program_id(1)
    i = pl.program_id(2)
    slope = slopes_ref[hh]

    q = q_ref[0].astype(F32)
    lane = lax.broadcasted_iota(I32, (tq, 256), 1)
    qrow = lax.broadcasted_iota(I32, (tq, 256), 0)
    qa = (qrow >> 5).astype(F32) * (32.0 * slope)
    qb = (qrow & 31).astype(F32) * slope
    qbias = jnp.where(lane < 130, 1.0, jnp.where(lane == 130, -qa, jnp.where(lane == 131, -qb, 0.0)))
    qpad = jnp.concatenate([q, jnp.zeros_like(q)], axis=1)
    q_aug = []
    for c in range(2):
        in_comp = (lane >= c * HEAD_DIM) & (lane < (c + 1) * HEAD_DIM)
        q_aug.append(jnp.where(in_comp, qpad, jnp.where(lane >= 128, qbias, 0.0)))
    klane = lax.broadcasted_iota(I32, (tk, 128), 1)
    krow = lax.broadcasted_iota(I32, (tk, 128), 0)
    ka = (krow >> 5).astype(F32) * (32.0 * slope)
    kb_ = (krow & 31).astype(F32) * slope
    kbias = jnp.where(klane == 0, ka, jnp.where(klane == 1, kb_, jnp.where(klane < 4, 1.0, 0.0))
                      ).astype(BF16)

    qt_both = jnp.concatenate([q_aug[0].T, q_aug[1].T], axis=1).astype(BF16)

    m_ref[...] = jnp.full_like(m_ref, -jnp.inf)
    acc_ref[...] = jnp.zeros_like(acc_ref)

    def scores(j):
        kb = k_ref[0, pl.ds(pl.multiple_of(j * tk, tk), tk), :]
        return _dot(jnp.concatenate([kb, kbias], axis=1), qt_both)

    def absorb(s_ref, j, masked):
        s = s_ref[...]
        off = j * tk - i * tq
        cj = slope * off.astype(F32)
        if masked:
            keep = (lax.broadcasted_iota(I32, (tk, tq), 0) -
                    lax.broadcasted_iota(I32, (tk, tq), 1) + off) <= 0
            s = jnp.where(jnp.concatenate([keep, keep], axis=1), s, -jnp.inf)
        m_loc = jnp.max(_tree_reduce(jnp.maximum, s), axis=0, keepdims=True)
        m_old = m_ref[...]
        m_new = jnp.maximum(m_old, m_loc + cj)
        pr = jnp.exp((s - (m_new - cj)).astype(BF16))
        alpha = jnp.exp(m_old - m_new)
        acc_ref[...] = alpha * acc_ref[...] + _dot(vt_ref[0, 0, j], pr)
        m_ref[...] = m_new

    n_full = (i * tq) // tk
    sa_ref[...] = scores(0)

    def pair_body(jj, carry):
        j0 = 2 * jj
        sb_ref[...] = scores(j0 + 1)
        absorb(sa_ref, j0, False)
        sa_ref[...] = scores(j0 + 2)
        absorb(sb_ref, j0 + 1, False)
        return carry

    lax.fori_loop(0, n_full // 2, pair_body, 0)

    @pl.when(n_full % 2 == 1)
    def _():
        sb_ref[...] = scores(n_full)
        absorb(sa_ref, n_full - 1, False)
        absorb(sb_ref, n_full, True)

    @pl.when(n_full % 2 == 0)
    def _():
        absorb(sa_ref, n_full, True)

    lam = (jnp.exp(jnp.sum(lq1_ref[...] * lk1_ref[...], axis=-1, keepdims=True))
           - jnp.exp(jnp.sum(lq2_ref[...] * lk2_ref[...], axis=-1, keepdims=True))
           + LAMBDA_INIT)
    o2 = acc_ref[0:128, :] * (1.0 / acc_ref[128:129, :])
    o = o2[:, :tq] - lam * o2[:, tq:]
    ot = o.T
    ms = jnp.mean(ot * ot, axis=-1, keepdims=True)
    y = ot * lax.rsqrt(ms + NORM_EPS) * gain_ref[...] * (1.0 - LAMBDA_INIT)
    o_ref[0] = y.astype(BF16)


def _attn_call(q, k, vt, lq1, lk1, lq2, lk2, subln_gain, tq):
    B, S, _ = q.shape
    ns, tk = vt.shape[2], vt.shape[4]
    slopes = jnp.asarray([2.0 ** (-8.0 * (i + 1) / DIFF_HEADS) for i in range(DIFF_HEADS)], F32)
    vec = lambda n: pl.BlockSpec((1, n), lambda b, h, i, sl: (0, 0))
    grid_spec = pltpu.PrefetchScalarGridSpec(
        num_scalar_prefetch=1,
        grid=(B, DIFF_HEADS, S // tq),
        in_specs=[pl.BlockSpec((1, tq, 128), lambda b, h, i, sl: (b, i, h)),
                  pl.BlockSpec((1, S, 128), lambda b, h, i, sl: (b, 0, h)),
                  pl.BlockSpec((1, 1, ns, VT_ROWS, tk), lambda b, h, i, sl: (b, h, 0, 0, 0)),
                  vec(HEAD_DIM), vec(HEAD_DIM), vec(HEAD_DIM), vec(HEAD_DIM), vec(128)],
        out_specs=pl.BlockSpec((1, tq, 128), lambda b, h, i, sl: (b, i, h)),
        scratch_shapes=[pltpu.VMEM((tk, 2 * tq), F32), pltpu.VMEM((tk, 2 * tq), F32),
                        pltpu.VMEM((VT_ROWS, 2 * tq), F32), pltpu.VMEM((1, 2 * tq), F32)])
    return pl.pallas_call(
        functools.partial(_attn_kernel, tq=tq, tk=tk),
        grid_spec=grid_spec,
        out_shape=jax.ShapeDtypeStruct((B, S, DIFF_WIDTH), BF16),
        compiler_params=pltpu.CompilerParams(
            dimension_semantics=("arbitrary", "arbitrary", "arbitrary"),
            vmem_limit_bytes=VMEM_LIMIT),
        name="attn",
    )(slopes, q, k, vt, lq1.reshape(1, -1), lk1.reshape(1, -1), lq2.reshape(1, -1),
      lk2.reshape(1, -1), subln_gain.reshape(1, -1))


def _pack_bf16_pairs(x):
    w = x.shape[1] // 2
    lo = pltpu.bitcast(x[:, :w].astype(BF16).astype(F32), U32)
    hi = pltpu.bitcast(x[:, w:].astype(BF16).astype(F32), U32)
    return (lo >> 16) | (hi & jnp.uint32(0xFFFF0000))


def _unpack_bf16_pairs(p):
    lo = pltpu.bitcast(p << 16, F32)
    hi = pltpu.bitcast(p & jnp.uint32(0xFFFF0000), F32)
    return lo, hi


ROW_SUB = (D_MODEL // 2) // 128


def _store_rows(ref, x2d):
    n = x2d.shape[0]
    for s in range(ROW_SUB):
        ref[pl.ds(s, n, stride=ROW_SUB), :] = x2d[:, s * 128:(s + 1) * 128]


def _load_rows(ref):
    n = ref.shape[0] // ROW_SUB
    return jnp.concatenate([ref[pl.ds(s, n, stride=ROW_SUB), :] for s in range(ROW_SUB)], axis=1)


def _outproj_kernel(yr_ref, yd_ref, x_ref, mod_ref, wo_r_ref, wo_d_ref, g2_ref, rw_hi_ref,
                    rw_lo_ref, rb_ref, x1_out, hp_out, idx_out, w_out, rank_out, cnt_out,
                    carry_ref, *, tm):
    t = pl.program_id(0)

    @pl.when(t == 0)
    def _():
        carry_ref[...] = jnp.zeros_like(carry_ref)

    gate1 = mod_ref[0, 2:3, :]
    shift2 = mod_ref[0, 3:4, :]
    scale2 = mod_ref[0, 4:5, :]
    n_part = max(1, tm // 256)
    rows_per = tm // n_part
    eidx = lax.broadcasted_iota(I32, (rows_per, N_EXPERTS), 1)
    col4 = lax.broadcasted_iota(I32, (rows_per, TOP_K), 1)
    tri = (lax.broadcasted_iota(I32, (rows_per, rows_per), 0) >
           lax.broadcasted_iota(I32, (rows_per, rows_per), 1)).astype(BF16)
    before = [carry_ref[...]]

    def part(i):
        rs = slice(i * rows_per, (i + 1) * rows_per)
        mix = _dot(yr_ref[rs, :], wo_r_ref[...]) + _dot(yd_ref[rs, :], wo_d_ref[...])
        yield
        x1 = x_ref[rs, :] + gate1 * mix
        x1_out[rs, :] = x1
        ms = jnp.mean(x1 * x1, axis=-1, keepdims=True)
        h = x1 * lax.rsqrt(ms + NORM_EPS) * g2_ref[...] * (1.0 + scale2) + shift2
        _store_rows(hp_out.at[pl.ds(i * rows_per * ROW_SUB, rows_per * ROW_SUB), :],
                    _pack_bf16_pairs(h))
        hi = h.astype(BF16)
        lo = (h - hi.astype(F32)).astype(BF16)
        yield
        logits = (_dot(hi, rw_hi_ref[...]) + _dot(hi, rw_lo_ref[...]) + _dot(lo, rw_hi_ref[...])
                  + rb_ref[...])
        yield
        lg = logits
        vals, idxs = [], []
        onehot = jnp.zeros(logits.shape, F32)
        for _ in range(TOP_K):
            m = jnp.max(lg, axis=-1, keepdims=True)
            ix = jnp.min(jnp.where(lg == m, eidx, N_EXPERTS), axis=-1, keepdims=True)
            sel = eidx == ix
            vals.append(m)
            idxs.append(ix)
            onehot = onehot + sel.astype(F32)
            lg = jnp.where(sel, -jnp.inf, lg)
        es = [jnp.exp(v - vals[0]) for v in vals]
        den = es[0] + es[1] + es[2] + es[3]
        before.append(before[i] + jnp.sum(onehot, axis=0, keepdims=True))
        yield
        prefix = _dot(tri, onehot.astype(BF16)) + before[i]
        yield
        idx4 = jnp.zeros((rows_per, TOP_K), I32)
        w4 = jnp.zeros((rows_per, TOP_K), F32)
        rank4 = jnp.zeros((rows_per, TOP_K), I32)
        for kk in range(TOP_K):
            rk = jnp.sum(jnp.where(eidx == idxs[kk], prefix, 0.0), axis=-1, keepdims=True)
            idx4 = jnp.where(col4 == kk, idxs[kk], idx4)
            w4 = jnp.where(col4 == kk, es[kk] / den, w4)
            rank4 = jnp.where(col4 == kk, rk.astype(I32), rank4)
        idx_out[rs, :] = idx4
        w_out[rs, :] = w4
        rank_out[rs, :] = rank4

    _round_robin([part(i) for i in range(n_part)])
    carry_ref[...] = before[n_part]
    cnt_out[...] = before[n_part].astype(I32)


def _outproj_call(y_rwkv, y_diff, x, mod6, w_out, norm2_gain, router_w, router_b, S, tm):
    T = x.shape[0]
    tiles_per_seq = S // tm
    rw_hi = router_w.astype(BF16)
    rw_lo = (router_w - rw_hi.astype(F32)).astype(BF16)
    tok = lambda w: pl.BlockSpec((tm, w), lambda t: (t, 0))
    full = lambda shape: pl.BlockSpec(shape, lambda t: (0,) * len(shape))
    return pl.pallas_call(
        functools.partial(_outproj_kernel, tm=tm),
        grid=(T // tm,),
        in_specs=[tok(RWKV_WIDTH), tok(DIFF_WIDTH), tok(D_MODEL),
                  pl.BlockSpec((1, 6, D_MODEL), lambda t: (t // tiles_per_seq, 0, 0)),
                  full((RWKV_WIDTH, D_MODEL)), full((DIFF_WIDTH, D_MODEL)), full((1, D_MODEL)),
                  full((D_MODEL, N_EXPERTS)), full((D_MODEL, N_EXPERTS)), full((1, N_EXPERTS))],
        out_specs=[tok(D_MODEL), pl.BlockSpec((tm * ROW_SUB, 128), lambda t: (t, 0)),
                   tok(TOP_K), tok(TOP_K), tok(TOP_K), full((1, N_EXPERTS))],
        out_shape=[jax.ShapeDtypeStruct((T, D_MODEL), F32),
                   jax.ShapeDtypeStruct((T * ROW_SUB, 128), U32),
                   jax.ShapeDtypeStruct((T, TOP_K), I32),
                   jax.ShapeDtypeStruct((T, TOP_K), F32),
                   jax.ShapeDtypeStruct((T, TOP_K), I32),
                   jax.ShapeDtypeStruct((1, N_EXPERTS), I32)],
        scratch_shapes=[pltpu.VMEM((1, N_EXPERTS), F32)],
        compiler_params=pltpu.CompilerParams(
            dimension_semantics=("arbitrary",), vmem_limit_bytes=VMEM_LIMIT),
        name="outproj",
    )(y_rwkv, y_diff, x, mod6, w_out[:RWKV_WIDTH].astype(BF16), w_out[RWKV_WIDTH:].astype(BF16),
      norm2_gain.reshape(1, -1), rw_hi, rw_lo, router_b.reshape(1, -1))


def _experts_kernel(be_ref, nu_ref, valid_ref, xs_ref, wgu_ref, bgu_ref, wd_ref, bd_ref, ys_ref,
                    wgu_bf, wd_bf):
    i = pl.program_id(0)

    @pl.when((i == 0) | (be_ref[i] != be_ref[jnp.maximum(i - 1, 0)]))
    def _():
        wgu_bf[...] = wgu_ref[0].astype(BF16)
        wd_bf[...] = wd_ref[0].astype(BF16)

    @pl.when(i < nu_ref[0])
    def _():
        packed = _load_rows(xs_ref)
        row = lax.broadcasted_iota(I32, packed.shape, 0)
        packed = jnp.where(row < valid_ref[i], packed, jnp.uint32(0))
        xa, xb = _unpack_bf16_pairs(packed)
        x = jnp.concatenate([xa.astype(BF16), xb.astype(BF16)], axis=1)
        gu = _dot(x, wgu_bf[...]) + bgu_ref[0]
        gate = jnp.minimum(gu[:, :D_EXPERT], SWIGLU_LIMIT)
        up = jnp.clip(gu[:, D_EXPERT:], -SWIGLU_LIMIT, SWIGLU_LIMIT)
        act = (up + 1.0) * (gate * jax.nn.sigmoid(SWIGLU_ALPHA * gate))
        y = _dot(act.astype(BF16), wd_bf[...]) + bd_ref[0]
        _store_rows(ys_ref, _pack_bf16_pairs(y))

    @pl.when(i >= nu_ref[0])
    def _():
        ys_ref[...] = jnp.zeros_like(ys_ref)


def _experts_call(xs, block_e, n_used, valid_rows, w_gate_up, b_gate_up, w_down, b_down, n_blocks):
    per_expert = lambda shape: pl.BlockSpec(shape, lambda i, be, nu, vr: (be[i], 0, 0))
    grid_spec = pltpu.PrefetchScalarGridSpec(
        num_scalar_prefetch=3,
        grid=(n_blocks,),
        in_specs=[pl.BlockSpec((MOE_BLOCK * ROW_SUB, 128),
                               lambda i, be, nu, vr: (jnp.minimum(i, nu[0] - 1), 0)),
                  per_expert((1, D_MODEL, 2 * D_EXPERT)), per_expert((1, 1, 2 * D_EXPERT)),
                  per_expert((1, D_EXPERT, D_MODEL)), per_expert((1, 1, D_MODEL))],
        out_specs=pl.BlockSpec((MOE_BLOCK * ROW_SUB, 128), lambda i, be, nu, vr: (i, 0)),
        scratch_shapes=[pltpu.VMEM((D_MODEL, 2 * D_EXPERT), BF16),
                        pltpu.VMEM((D_EXPERT, D_MODEL), BF16)])
    return pl.pallas_call(
        _experts_kernel,
        grid_spec=grid_spec,
        out_shape=jax.ShapeDtypeStruct((n_blocks * MOE_BLOCK * ROW_SUB, 128), U32),
        compiler_params=pltpu.CompilerParams(
            dimension_semantics=("arbitrary",), vmem_limit_bytes=VMEM_LIMIT),
        name="experts",
    )(block_e, n_used, valid_rows, xs, w_gate_up, b_gate_up.reshape(N_EXPERTS, 1, -1), w_down,
      b_down.reshape(N_EXPERTS, 1, -1))


SC_WINDOW = 128
COMBINE_PARTS = 4


def _sc_mesh():
    return plsc.VectorSubcoreMesh(core_axis_name="c", subcore_axis_name="s")


def _sc_worker_chunks(n_chunks):
    info = plsc.get_sparse_core_info()
    n_workers = info.num_cores * info.num_subcores
    assert n_chunks % n_workers == 0
    return info.num_cores, n_chunks // n_workers


def _sc_scatter_call(rows, slot_chunks, n_out_rows):
    T = rows.shape[0]
    per_pass = T // SC_WINDOW
    n_cores, per_worker = _sc_worker_chunks(per_pass)

    @functools.partial(
        pl.kernel, mesh=_sc_mesh(),
        out_type=jax.ShapeDtypeStruct((n_out_rows, ROW_SUB, 128), U32),
        scratch_types=[pltpu.VMEM((SC_WINDOW,), I32), pltpu.VMEM((SC_WINDOW, ROW_SUB, 128), U32)],
        name="sc_dispatch")
    def run(rows_hbm, idx_hbm, out_hbm, idx_v, rows_v):
        wid = lax.axis_index("s") * n_cores + lax.axis_index("c")

        @pl.loop(0, per_worker)
        def _(j):
            chunk = wid * per_worker + j
            pltpu.sync_copy(rows_hbm.at[pl.ds(chunk * SC_WINDOW, SC_WINDOW)], rows_v)
            for kk in range(TOP_K):
                pltpu.sync_copy(idx_hbm.at[kk * per_pass + chunk], idx_v)
                pltpu.sync_copy(rows_v, out_hbm.at[idx_v])

    return run(rows, slot_chunks)


def _sc_gather_call(table, slot_chunks):
    n_chunks = slot_chunks.shape[0]
    n_cores, per_worker = _sc_worker_chunks(n_chunks)

    @functools.partial(
        pl.kernel, mesh=_sc_mesh(),
        out_type=jax.ShapeDtypeStruct((n_chunks * SC_WINDOW, ROW_SUB, 128), U32),
        scratch_types=[pltpu.VMEM((SC_WINDOW,), I32), pltpu.VMEM((SC_WINDOW, ROW_SUB, 128), U32)],
        name="sc_collect")
    def run(table_hbm, idx_hbm, out_hbm, idx_v, rows_v):
        wid = lax.axis_index("s") * n_cores + lax.axis_index("c")

        @pl.loop(0, per_worker)
        def _(j):
            chunk = wid * per_worker + j
            pltpu.sync_copy(idx_hbm.at[chunk], idx_v)
            pltpu.sync_copy(table_hbm.at[idx_v], rows_v)
            pltpu.sync_copy(rows_v, out_hbm.at[pl.ds(chunk * SC_WINDOW, SC_WINDOW)])

    return run(table, slot_chunks)


def _combine_dense_kernel(g0_ref, g1_ref, g2_ref, g3_ref, x1_ref, w_ref, mod_ref, fg_ref, o_ref,
                          *, tm):
    w = w_ref[...]
    acc_lo = jnp.zeros((tm, D_MODEL // 2), F32)
    acc_hi = jnp.zeros((tm, D_MODEL // 2), F32)
    for kk, g_ref in enumerate((g0_ref, g1_ref, g2_ref, g3_ref)):
        lo, hi = _unpack_bf16_pairs(_load_rows(g_ref))
        wk = w[:, kk:kk + 1]
        acc_lo = acc_lo + wk * lo
        acc_hi = acc_hi + wk * hi
    moe = jnp.concatenate([acc_lo, acc_hi], axis=1)
    gate2 = mod_ref[0, 5:6, :]
    x2 = x1_ref[...] + gate2 * moe
    ms = jnp.mean(x2 * x2, axis=-1, keepdims=True)
    o_ref[...] = x2 * lax.rsqrt(ms + NORM_EPS) * fg_ref[...]


def _combine_dense_call(gathered, x1, top_w, mod6, final_gain, S, tm, part, n_parts):
    T = x1.shape[0]
    assert T % (tm * n_parts) == 0
    nt = T // tm // n_parts
    first = part * nt
    tiles_per_seq = S // tm
    rows = lambda kk: pl.BlockSpec((tm * ROW_SUB, 128), lambda t: (kk * nt + t, 0))
    return pl.pallas_call(
        functools.partial(_combine_dense_kernel, tm=tm),
        grid=(nt,),
        in_specs=[rows(0), rows(1), rows(2), rows(3),
                  pl.BlockSpec((tm, D_MODEL), lambda t: (first + t, 0)),
                  pl.BlockSpec((tm, TOP_K), lambda t: (first + t, 0)),
                  pl.BlockSpec((1, 6, D_MODEL), lambda t: ((first + t) // tiles_per_seq, 0, 0)),
                  pl.BlockSpec((1, D_MODEL), lambda t: (0, 0))],
        out_specs=pl.BlockSpec((tm, D_MODEL), lambda t: (first + t, 0)),
        out_shape=jax.ShapeDtypeStruct((T, D_MODEL), F32),
        input_output_aliases={4: 0},
        compiler_params=pltpu.CompilerParams(
            dimension_semantics=("arbitrary",), vmem_limit_bytes=VMEM_LIMIT),
        name="combine",
    )(gathered, gathered, gathered, gathered, x1, top_w, mod6, final_gain.reshape(1, -1))


def _forward(x, c, mod_w, mod_b, norm1_gain, w_in, rwkv_shift_mu, rwkv_w0, rwkv_w_up, rwkv_a0,
             rwkv_a_up, rwkv_g_up, rwkv_k_k, rwkv_k_a, rwkv_r_k, rwkv_gn_gain, rwkv_gn_bias,
             diff_lambda_q1, diff_lambda_k1, diff_lambda_q2, diff_lambda_k2, diff_subln_gain,
             w_out, norm2_gain, router_w, router_b, w_gate_up, b_gate_up, w_down, b_down,
             final_gain):
    B, S, D = x.shape
    T = B * S
    tm_in = min(512, S)
    tq = min(512, S)
    tm_out = min(1024, S)
    tm = min(256, S)

    mod6 = _mod_call(c, mod_w, mod_b).reshape(B, 6, D)
    (r, k, v, a, b, g, lw, q, kat, vt) = _inproj_call(
        x, mod6, norm1_gain, w_in, rwkv_shift_mu, rwkv_w0, rwkv_w_up, rwkv_a0, rwkv_a_up,
        rwkv_g_up, rwkv_k_k, rwkv_k_a, tm_in)
    y_rwkv = _rwkv_call(r, k, v, a, b, g, lw, rwkv_r_k, rwkv_gn_gain, rwkv_gn_bias,
                        min(16, S // CHUNK))
    y_diff = _attn_call(q, kat, vt, diff_lambda_q1, diff_lambda_k1, diff_lambda_q2,
                        diff_lambda_k2, diff_subln_gain, tq)

    x1, hp, top_idx, top_w, rank, counts = _outproj_call(
        y_rwkv.reshape(T, -1), y_diff.reshape(T, -1), x.reshape(T, D), mod6, w_out, norm2_gain,
        router_w, router_b, S, tm_out)

    counts = counts.reshape(N_EXPERTS)
    padded = ((counts + MOE_BLOCK - 1) // MOE_BLOCK) * MOE_BLOCK
    pad_ends = jnp.cumsum(padded)
    pad_starts = pad_ends - padded
    n_blocks = -(-(T * TOP_K + N_EXPERTS * (MOE_BLOCK - 1)) // MOE_BLOCK)
    n_used = (pad_ends[-1] // MOE_BLOCK).astype(I32).reshape(1)
    block_start = jnp.minimum(jnp.arange(n_blocks, dtype=I32), n_used[0] - 1) * MOE_BLOCK
    block_e = jnp.minimum(jnp.sum(pad_ends[None, :] <= block_start[:, None], axis=1),
                          N_EXPERTS - 1).astype(I32)
    idx_t = top_idx.T
    expert_ids = jnp.arange(N_EXPERTS, dtype=I32)[:, None, None]
    start_of = jnp.sum(jnp.where(idx_t[None] == expert_ids, pad_starts.astype(I32)[:, None, None], 0),
                       axis=0)
    slots = start_of + rank.T
    slot_chunks = slots.reshape(TOP_K * T // SC_WINDOW, SC_WINDOW)
    block_row = jnp.arange(n_blocks, dtype=I32) * MOE_BLOCK
    valid_rows = jnp.clip(counts.astype(I32)[block_e] - (block_row - pad_starts.astype(I32)[block_e]),
                          0, MOE_BLOCK).astype(I32)

    as_tokens = lambda z: z.reshape(-1, ROW_SUB, 128)
    as_lines = lambda z: z.reshape(-1, 128)
    xs = as_lines(_sc_scatter_call(as_tokens(hp), slot_chunks, n_blocks * MOE_BLOCK))
    ys = _experts_call(xs, block_e, n_used, valid_rows, w_gate_up, b_gate_up, w_down, b_down,
                       n_blocks)
    ys_tokens = as_tokens(ys)
    out = x1
    per_part = T // COMBINE_PARTS
    for part in range(COMBINE_PARTS):
        part_slots = slots[:, part * per_part:(part + 1) * per_part]
        part_chunks = part_slots.reshape(TOP_K * per_part // SC_WINDOW, SC_WINDOW)
        gathered = as_lines(_sc_gather_call(ys_tokens, part_chunks))
        out = _combine_dense_call(gathered, out, top_w, mod6, final_gain, S, tm, part, COMBINE_PARTS)
    return out.reshape(B, S, D)


def kernel(x, c, mod_w, mod_b, norm1_gain, w_in, rwkv_shift_mu, rwkv_w0, rwkv_w_up, rwkv_a0, rwkv_a_up, rwkv_g_up, rwkv_k_k, rwkv_k_a, rwkv_r_k, rwkv_gn_gain, rwkv_gn_bias, diff_lambda_q1, diff_lambda_k1, diff_lambda_q2, diff_lambda_k2, diff_subln_gain, w_out, norm2_gain, router_w, router_b, w_gate_up, b_gate_up, w_down, b_down, final_gain):
    return _forward(x, c, mod_w[0], mod_b[0], norm1_gain[0], w_in[0], rwkv_shift_mu[0], rwkv_w0[0],
                    rwkv_w_up[0], rwkv_a0[0], rwkv_a_up[0], rwkv_g_up[0], rwkv_k_k[0], rwkv_k_a[0],
                    rwkv_r_k[0].reshape(-1), rwkv_gn_gain[0], rwkv_gn_bias[0], diff_lambda_q1[0],
                    diff_lambda_k1[0], diff_lambda_q2[0], diff_lambda_k2[0], diff_subln_gain[0],
                    w_out[0], norm2_gain[0], router_w[0], router_b[0], w_gate_up[0], b_gate_up[0],
                    w_down[0], b_down[0], final_gain)
```

```python
import functools
import math

import jax
import jax.numpy as jnp
from jax import lax
from jax.experimental import pallas as pl
from jax.experimental.pallas import tpu as pltpu
from jax.experimental.pallas import tpu_sc as plsc

F32 = jnp.float32
BF16 = jnp.bfloat16
I32 = jnp.int32
U32 = jnp.uint32

D_MODEL = 1024
RWKV_WIDTH = 512
RWKV_HEADS = 8
HEAD_DIM = 64
RWKV_COLS = 3 * RWKV_WIDTH + 64 + 64 + 128
DIFF_WIDTH = 512
DIFF_HEADS = 4
DIFF_COLS = 3 * DIFF_WIDTH
N_EXPERTS = 32
TOP_K = 4
D_EXPERT = 1024
SWIGLU_LIMIT = 7.0
SWIGLU_ALPHA = 1.702
MOE_BLOCK = 512
NORM_EPS = 1e-5
GN_EPS = 64e-5
LAMBDA_INIT = 0.8 - 0.6 * math.exp(-0.3 * 0)

CHUNK = 64
GROUP = 4
GW = GROUP * HEAD_DIM
VT_ROWS = 128 + 16
KEY_SLABS = 2
VMEM_LIMIT = 56 * 1024 * 1024


def _dot(a, b):
    return jnp.dot(a, b, preferred_element_type=F32)


def _dot_nt(a, b):
    return lax.dot_general(a, b, (((1,), (1,)), ((), ())), preferred_element_type=F32)


def _dot_tn(a, b):
    return lax.dot_general(a, b, (((0,), (0,)), ((), ())), preferred_element_type=F32)


def _round_robin(stages):
    waiting = list(stages)
    running = []
    while waiting or running:
        if waiting:
            running.append(waiting.pop(0))
        for gen in list(running):
            try:
                next(gen)
            except StopIteration:
                running.remove(gen)


def _split_dot(x, w_bf16):
    hi = x.astype(BF16)
    lo = (x - hi.astype(F32)).astype(BF16)
    return _dot(hi, w_bf16) + _dot(lo, w_bf16)


def _mod_kernel(c_ref, w_ref, b_ref, o_ref):
    c = c_ref[...]
    s = c * jax.nn.sigmoid(c)
    o_ref[...] = _dot(s, w_ref[...]) + b_ref[...]


def _mod_call(c, mod_w, mod_b):
    B = c.shape[0]
    n = mod_w.shape[1]
    tn = 1536
    return pl.pallas_call(
        _mod_kernel,
        grid=(n // tn,),
        in_specs=[pl.BlockSpec((B, D_MODEL), lambda j: (0, 0)),
                  pl.BlockSpec((D_MODEL, tn), lambda j: (0, j)),
                  pl.BlockSpec((1, tn), lambda j: (0, j))],
        out_specs=pl.BlockSpec((B, tn), lambda j: (0, j)),
        out_shape=jax.ShapeDtypeStruct((B, n), F32),
        compiler_params=pltpu.CompilerParams(
            dimension_semantics=("arbitrary",), vmem_limit_bytes=VMEM_LIMIT),
        name="mod",
    )(c, mod_w, mod_b.reshape(1, n))


def _inproj_kernel(x_ref, mod_ref, g1_ref, wrw_ref, wat_ref, mu_ref, w0_ref, wup_ref,
                   a0_ref, aup_ref, gup_ref, kk_ref, ka_ref, hsum_ref,
                   r_out, k_out, v_out, a_out, b_out, g_out, lw_out,
                   q_out, kat_out, vt_out, carry_ref, *, tm):
    s = pl.program_id(1)
    shift1 = mod_ref[0, 0:1, :]
    scale1 = mod_ref[0, 1:2, :]

    @pl.when(s == 0)
    def _():
        carry_ref[...] = jnp.zeros_like(carry_ref)

    n_part = max(1, tm // 128)
    rows_per = tm // n_part
    last_row = [carry_ref[...]]

    def part(i):
        rs = slice(i * rows_per, (i + 1) * rows_per)
        x = x_ref[0, rs, :]
        ms = jnp.mean(x * x, axis=-1, keepdims=True)
        h = x * lax.rsqrt(ms + NORM_EPS) * g1_ref[...] * (1.0 + scale1) + shift1
        hb = h.astype(BF16)
        yield
        p = _dot(hb, wrw_ref[...])
        last_row.append(p[rows_per - 1:rows_per, :])
        yield
        pa = _dot(hb, wat_ref[...])
        yield
        rolled = pltpu.roll(p, shift=1, axis=0)
        row = lax.broadcasted_iota(I32, p.shape, 0)
        prev = jnp.where(row == 0, last_row[i], rolled)
        ps = p + mu_ref[...] * (prev - p)
        r = ps[:, 0:512]
        k = ps[:, 512:1024]
        v = ps[:, 1024:1536]
        lo2 = ps[:, 1536:1664]
        g_lo = ps[:, 1664:1792]
        kk = k * kk_ref[...]
        yield
        z = w0_ref[...] + _dot(jnp.tanh(lo2).astype(BF16), wup_ref[...])
        a_pre = a0_ref[...] + _dot(lo2.astype(BF16), aup_ref[...])
        g = _dot(jax.nn.sigmoid(g_lo).astype(BF16), gup_ref[...])
        ssq = _split_dot(kk * kk, hsum_ref[...])
        yield
        nz = -z
        softplus = jnp.maximum(nz, 0.0) + jnp.log(1.0 + jnp.exp(-jnp.abs(nz)))
        w = -softplus - 0.5
        lw_out[0, rs, :] = -jnp.exp(w)
        a = jax.nn.sigmoid(a_pre)
        kk = kk / jnp.maximum(jnp.sqrt(ssq), 1e-12)
        k = k * (1.0 + (a - 1.0) * ka_ref[...])
        r_out[0, rs, :] = r.astype(BF16)
        k_out[0, rs, :] = k.astype(BF16)
        v_out[0, rs, :] = v.astype(BF16)
        a_out[0, rs, :] = (-kk).astype(BF16)
        b_out[0, rs, :] = (kk * a).astype(BF16)
        g_out[0, rs, :] = g.astype(BF16)
        yield
        q_out[0, rs, :] = (pa[:, 0:512] * (1.0 / math.sqrt(HEAD_DIM))).astype(BF16)
        kat_out[0, rs, :] = pa[:, 512:1024].astype(BF16)
        for hh in range(DIFF_HEADS):
            vh = pa[:, 1024 + hh * 128:1024 + (hh + 1) * 128]
            vt_out[0, hh, 0, 0:128, rs] = vh.T.astype(BF16)

    _round_robin([part(i) for i in range(n_part)])
    carry_ref[...] = last_row[n_part]
    for hh in range(DIFF_HEADS):
        vt_out[0, hh, 0, 128:VT_ROWS, :] = jnp.ones((VT_ROWS - 128, tm), BF16)


def _head_sum_matrix(width):
    i = jnp.arange(width) // HEAD_DIM
    return (i[:, None] == i[None, :]).astype(BF16)


def _inproj_call(x, mod6, norm1_gain, w_in, mu, w0, w_up, a0, a_up, g_up, k_k, k_a, tm):
    B, S, _ = x.shape
    ns = S // tm
    w_rw = w_in[:, :RWKV_COLS].astype(BF16)
    w_at = w_in[:, RWKV_COLS:].astype(BF16)
    zeros = jnp.zeros((64, RWKV_WIDTH), F32)
    wup_p = jnp.concatenate([w_up, zeros], axis=0).astype(BF16)
    aup_p = jnp.concatenate([zeros, a_up], axis=0).astype(BF16)
    row = lambda v: v.reshape(1, -1)
    full = lambda shape: pl.BlockSpec(shape, lambda b, s: (0,) * len(shape))
    tok = lambda w: pl.BlockSpec((1, tm, w), lambda b, s: (b, s, 0))
    rw_shape = jax.ShapeDtypeStruct((B, S, RWKV_WIDTH), BF16)
    out_shape = [rw_shape] * 6 + [
        jax.ShapeDtypeStruct((B, S, RWKV_WIDTH), F32),
        jax.ShapeDtypeStruct((B, S, DIFF_WIDTH), BF16),
        jax.ShapeDtypeStruct((B, S, DIFF_WIDTH), BF16),
        jax.ShapeDtypeStruct((B, DIFF_HEADS, ns, VT_ROWS, tm), BF16)]
    out_specs = [tok(RWKV_WIDTH)] * 7 + [tok(DIFF_WIDTH)] * 2 + [
        pl.BlockSpec((1, DIFF_HEADS, 1, VT_ROWS, tm), lambda b, s: (b, 0, s, 0, 0))]
    return pl.pallas_call(
        functools.partial(_inproj_kernel, tm=tm),
        grid=(B, ns),
        in_specs=[tok(D_MODEL),
                  pl.BlockSpec((1, 6, D_MODEL), lambda b, s: (b, 0, 0)),
                  full((1, D_MODEL)),
                  full((D_MODEL, RWKV_COLS)), full((D_MODEL, DIFF_COLS)),
                  full((1, RWKV_COLS)), full((1, RWKV_WIDTH)), full((128, RWKV_WIDTH)),
                  full((1, RWKV_WIDTH)), full((128, RWKV_WIDTH)), full((128, RWKV_WIDTH)),
                  full((1, RWKV_WIDTH)), full((1, RWKV_WIDTH)),
                  full((RWKV_WIDTH, RWKV_WIDTH))],
        out_specs=out_specs,
        out_shape=out_shape,
        scratch_shapes=[pltpu.VMEM((1, RWKV_COLS), F32)],
        compiler_params=pltpu.CompilerParams(
            dimension_semantics=("arbitrary", "arbitrary"), vmem_limit_bytes=VMEM_LIMIT),
        name="inproj",
    )(x, mod6, row(norm1_gain), w_rw, w_at, row(mu), row(w0), wup_p, row(a0), aup_p,
      g_up.astype(BF16), row(k_k), row(k_a), _head_sum_matrix(RWKV_WIDTH))


def _rwkv_kernel(r_ref, k_ref, v_ref, a_ref, b_ref, g_ref, lw_ref, rk_ref, gng_ref, gnb_ref,
                 hsum_ref, y_ref, s0_ref, s1_ref, *, nck):
    C = CHUNK
    n_grp = RWKV_HEADS // GROUP
    s_refs = (s0_ref, s1_ref)

    @pl.when(pl.program_id(1) == 0)
    def _():
        s0_ref[...] = jnp.zeros_like(s0_ref)
        s1_ref[...] = jnp.zeros_like(s1_ref)

    ti = lax.broadcasted_iota(I32, (C, GW), 0)
    si = lax.broadcasted_iota(I32, (C, GW), 1) % C
    incl = si <= ti
    strict = si < ti
    eye = (si == ti).astype(F32)
    bdmask = (lax.broadcasted_iota(I32, (GW, GW), 0) // HEAD_DIM ==
              lax.broadcasted_iota(I32, (GW, GW), 1) // HEAD_DIM)
    hsum = hsum_ref[...]
    chains = [(ck, g) for ck in range(nck) for g in range(n_grp)]

    bdmask_bf = bdmask.astype(BF16)

    def bd(xb):
        return jnp.concatenate([xb] * GROUP, axis=0) * bdmask_bf

    tri =(lax.broadcasted_iota(I32, (C, C), 0) >=
           lax.broadcasted_iota(I32, (C, C), 1)).astype(BF16)
    done = {}
    sv = [s_refs[g][...] for g in range(n_grp)]

    def chain(ch):
        ck, g = ch
        rows = slice(ck * C, (ck + 1) * C)
        cols = slice(g * GW, (g + 1) * GW)
        lw = lw_ref[0, rows, cols]
        L = _split_dot_left(tri, lw)
        yield
        Lx = L - lw
        Lc = L[C - 1:C, :]
        rho = L[C // 2 - 1:C // 2, :]
        r = r_ref[0, rows, cols].astype(F32)
        k = k_ref[0, rows, cols].astype(F32)
        vb = v_ref[0, rows, cols]
        a = a_ref[0, rows, cols].astype(F32)
        b = b_ref[0, rows, cols].astype(F32)
        e_k = jnp.exp(rho - L)
        lhs = jnp.concatenate([(r * jnp.exp(L - rho)).astype(BF16),
                               (a * jnp.exp(Lx - rho)).astype(BF16)], axis=0)
        kbd = bd((k * e_k).astype(BF16))
        bbd = bd((b * e_k).astype(BF16))
        r0 = (r * jnp.exp(L)).astype(BF16)
        a0bd = bd((a * jnp.exp(Lx)).astype(BF16))
        e_o = jnp.exp(Lc - L)
        bk = jnp.concatenate([(b * e_o).astype(BF16), (k * e_o).astype(BF16)], axis=0)
        vbd = bd(vb)
        yield
        ak = _dot_nt(lhs, kbd)
        ab = _dot_nt(lhs, bbd)
        yield
        a_rk = jnp.where(incl, ak[:C], 0.0).astype(BF16)
        a_ak = jnp.where(strict, ak[C:], 0.0).astype(BF16)
        a_rb = jnp.where(incl, ab[:C], 0.0).astype(BF16)
        n = jnp.where(strict, ab[C:], 0.0)
        p = eye + n
        xb = n.astype(BF16)
        xbd = bd(xb)
        yield
        x = _dot(xb, xbd)
        yield
        for _ in range(4):
            xb = x.astype(BF16)
            stack = jnp.concatenate([xb, p.astype(BF16)], axis=0)
            xbd = bd(xb)
            yield
            o = _dot(stack, xbd)
            yield
            x = o[:C]
            p = p + o[C:]
        pb = p.astype(BF16)
        xbd = bd(x.astype(BF16))
        yield
        o = _dot(pb, xbd)
        yield
        tb = (p + o).astype(BF16)
        yield
        av = _dot(jnp.concatenate([a_ak, a_rk], axis=0), vbd)
        w1 = _dot(tb, a0bd)
        yield
        akvbd = bd(av[:C].astype(BF16))
        rw_lhs = jnp.concatenate([r0, w1.astype(BF16)], axis=0)
        yield
        w2 = _dot(tb, akvbd)
        yield
        done[ch] = dict(rw_lhs=rw_lhs, w2=w2, a_rb=a_rb, arkv=av[C:], vb=vb, bk=bk,
                        gc=jnp.exp(Lc), rk=r * k)

    def sequential(ck):
        d = [done[(ck, g)] for g in range(n_grp)]
        rws = [_dot_nt(d[g]["rw_lhs"], sv[g].astype(BF16)) for g in range(n_grp)]
        yield
        ub = [(rws[g][C:] + d[g]["w2"]).astype(BF16) for g in range(n_grp)]
        uv = [jnp.concatenate([ub[g], d[g]["vb"]], axis=0) for g in range(n_grp)]
        ubd = [bd(ub[g]) for g in range(n_grp)]
        yield
        upd = [_dot_tn(uv[g], d[g]["bk"]) for g in range(n_grp)]
        yield
        for g in range(n_grp):
            sv[g] = sv[g] * d[g]["gc"] + jnp.where(bdmask, upd[g], 0.0)
        done[("state", ck)] = True
        y = [rws[g][:C] + _dot(d[g]["a_rb"], ubd[g]) + d[g]["arkv"] for g in range(n_grp)]
        yield
        mean = [_split_dot(y[g], hsum) * (1.0 / HEAD_DIM) for g in range(n_grp)]
        yield
        dev = [y[g] - mean[g] for g in range(n_grp)]
        var = [_split_dot(dev[g] * dev[g], hsum) * (1.0 / HEAD_DIM) for g in range(n_grp)]
        bonus = [_split_dot(d[g]["rk"] * rk_ref[:, g * GW:(g + 1) * GW], hsum) for g in range(n_grp)]
        yield
        rows = slice(ck * C, (ck + 1) * C)
        for g in range(n_grp):
            cols = slice(g * GW, (g + 1) * GW)
            yn = dev[g] * lax.rsqrt(var[g] + GN_EPS) * gng_ref[:, cols] + gnb_ref[:, cols]
            out = (yn + bonus[g] * d[g]["vb"].astype(F32)) * g_ref[0, rows, cols].astype(F32)
            y_ref[0, rows, cols] = out.astype(BF16)

    waiting = [chain(ch) for ch in chains]
    seq_next = 0
    running = []
    while waiting or running or seq_next < nck:
        ready = (seq_next < nck and all((seq_next, g) in done for g in range(n_grp))
                 and (seq_next == 0 or ("state", seq_next - 1) in done))
        if ready:
            running.append(sequential(seq_next))
            seq_next += 1
        elif waiting:
            running.append(waiting.pop(0))
        for gen in list(running):
            try:
                next(gen)
            except StopIteration:
                running.remove(gen)
    for g in range(n_grp):
        s_refs[g][...] = sv[g]


def _split_dot_left(w_bf16, x):
    hi = x.astype(BF16)
    lo = (x - hi.astype(F32)).astype(BF16)
    return _dot(w_bf16, hi) + _dot(w_bf16, lo)


def _rwkv_call(r, k, v, a, b, g, lw, r_k, gn_gain, gn_bias, nck):
    B, S, _ = r.shape
    tb = nck * CHUNK
    tok = pl.BlockSpec((1, tb, RWKV_WIDTH), lambda bb, c: (bb, c, 0))
    par = pl.BlockSpec((1, RWKV_WIDTH), lambda bb, c: (0, 0))
    return pl.pallas_call(
        functools.partial(_rwkv_kernel, nck=nck),
        grid=(B, S // tb),
        in_specs=[tok] * 7 + [par] * 3 + [pl.BlockSpec((GW, GW), lambda bb, c: (0, 0))],
        out_specs=tok,
        out_shape=jax.ShapeDtypeStruct((B, S, RWKV_WIDTH), BF16),
        scratch_shapes=[pltpu.VMEM((GW, GW), F32)] * (RWKV_HEADS // GROUP),
        compiler_params=pltpu.CompilerParams(
            dimension_semantics=("arbitrary", "arbitrary"), vmem_limit_bytes=VMEM_LIMIT),
        name="rwkv",
    )(r, k, v, a, b, g, lw, r_k.reshape(1, -1), gn_gain.reshape(1, -1), gn_bias.reshape(1, -1),
      _head_sum_matrix(GW))


def _tree_reduce(op, x):
    while x.shape[0] > 8:
        h = x.shape[0] // 2
        x = op(x[:h], x[h:])
    return x


def _attn_kernel(slopes_ref, q_ref, k_ref, vt_ref, lq1_ref, lk1_ref, lq2_ref, lk2_ref,
                 gain_ref, o_ref, sa_ref, sb_ref, acc_ref, m_ref, *, tq, tk):
    hh = pl.program_id(1)
    i = pl.program_id(2)
    slope = slopes_ref[hh]

    q = q_ref[0].astype(F32)
    lane = lax.broadcasted_iota(I32, (tq, 256), 1)
    qrow = lax.broadcasted_iota(I32, (tq, 256), 0)
    qa = (qrow >> 5).astype(F32) * (32.0 * slope)
    qb = (qrow & 31).astype(F32) * slope
    qbias = jnp.where(lane < 130, 1.0, jnp.where(lane == 130, -qa, jnp.where(lane == 131, -qb, 0.0)))
    qpad = jnp.concatenate([q, jnp.zeros_like(q)], axis=1)
    q_aug = []
    for c in range(2):
        in_comp = (lane >= c * HEAD_DIM) & (lane < (c + 1) * HEAD_DIM)
        q_aug.append(jnp.where(in_comp, qpad, jnp.where(lane >= 128, qbias, 0.0)))
    klane = lax.broadcasted_iota(I32, (tk, 128), 1)
    krow = lax.broadcasted_iota(I32, (tk, 128), 0)
    ka = (krow >> 5).astype(F32) * (32.0 * slope)
    kb_ = (krow & 31).astype(F32) * slope
    kbias = jnp.where(klane == 0, ka, jnp.where(klane == 1, kb_, jnp.where(klane < 4, 1.0, 0.0))
                      ).astype(BF16)

    qt_both = jnp.concatenate([q_aug[0].T, q_aug[1].T], axis=1).astype(BF16)

    m_ref[...] = jnp.full_like(m_ref, -jnp.inf)
    acc_ref[...] = jnp.zeros_like(acc_ref)

    def scores(j):
        kb = k_ref[0, pl.ds(pl.multiple_of(j * tk, tk), tk), :]
        return _dot(jnp.concatenate([kb, kbias], axis=1), qt_both)

    def absorb(s_ref, j, masked):
        s = s_ref[...]
        off = j * tk - i * tq
        cj = slope * off.astype(F32)
        if masked:
            keep = (lax.broadcasted_iota(I32, (tk, tq), 0) -
                    lax.broadcasted_iota(I32, (tk, tq), 1) + off) <= 0
            s = jnp.where(jnp.concatenate([keep, keep], axis=1), s, -jnp.inf)
        m_loc = jnp.max(_tree_reduce(jnp.maximum, s), axis=0, keepdims=True)
        m_old = m_ref[...]
        m_new = jnp.maximum(m_old, m_loc + cj)
        shift = m_new - cj
        vt = vt_ref[0, 0, j]
        rows = tk // KEY_SLABS
        pv = None
        for r in range(KEY_SLABS):
            pr = jnp.exp((s[r * rows:(r + 1) * rows] - shift).astype(BF16))
            part = _dot(vt[:, r * rows:(r + 1) * rows], pr)
            pv = part if pv is None else pv + part
        alpha = jnp.exp(m_old - m_new)
        acc_ref[...] = alpha * acc_ref[...] + pv
        m_ref[...] = m_new

    n_full = (i * tq) // tk
    sa_ref[...] = scores(0)

    def pair_body(jj, carry):
        j0 = 2 * jj
        sb_ref[...] = scores(j0 + 1)
        absorb(sa_ref, j0, False)
        sa_ref[...] = scores(j0 + 2)
        absorb(sb_ref, j0 + 1, False)
        return carry

    lax.fori_loop(0, n_full // 2, pair_body, 0)

    @pl.when(n_full % 2 == 1)
    def _():
        sb_ref[...] = scores(n_full)
        absorb(sa_ref, n_full - 1, False)
        absorb(sb_ref, n_full, True)

    @pl.when(n_full % 2 == 0)
    def _():
        absorb(sa_ref, n_full, True)

    lam = (jnp.exp(jnp.sum(lq1_ref[...] * lk1_ref[...], axis=-1, keepdims=True))
           - jnp.exp(jnp.sum(lq2_ref[...] * lk2_ref[...], axis=-1, keepdims=True))
           + LAMBDA_INIT)
    o2 = acc_ref[0:128, :] * (1.0 / acc_ref[128:129, :])
    o = o2[:, :tq] - lam * o2[:, tq:]
    ot = o.T
    ms = jnp.mean(ot * ot, axis=-1, keepdims=True)
    y = ot * lax.rsqrt(ms + NORM_EPS) * gain_ref[...] * (1.0 - LAMBDA_INIT)
    o_ref[0] = y.astype(BF16)


def _attn_call(q, k, vt, lq1, lk1, lq2, lk2, subln_gain, tq):
    B, S, _ = q.shape
    ns, tk = vt.shape[2], vt.shape[4]
    slopes = jnp.asarray([2.0 ** (-8.0 * (i + 1) / DIFF_HEADS) for i in range(DIFF_HEADS)], F32)
    vec = lambda n: pl.BlockSpec((1, n), lambda b, h, i, sl: (0, 0))
    grid_spec = pltpu.PrefetchScalarGridSpec(
        num_scalar_prefetch=1,
        grid=(B, DIFF_HEADS, S // tq),
        in_specs=[pl.BlockSpec((1, tq, 128), lambda b, h, i, sl: (b, i, h)),
                  pl.BlockSpec((1, S, 128), lambda b, h, i, sl: (b, 0, h)),
                  pl.BlockSpec((1, 1, ns, VT_ROWS, tk), lambda b, h, i, sl: (b, h, 0, 0, 0)),
                  vec(HEAD_DIM), vec(HEAD_DIM), vec(HEAD_DIM), vec(HEAD_DIM), vec(128)],
        out_specs=pl.BlockSpec((1, tq, 128), lambda b, h, i, sl: (b, i, h)),
        scratch_shapes=[pltpu.VMEM((tk, 2 * tq), F32), pltpu.VMEM((tk, 2 * tq), F32),
                        pltpu.VMEM((VT_ROWS, 2 * tq), F32), pltpu.VMEM((1, 2 * tq), F32)])
    return pl.pallas_call(
        functools.partial(_attn_kernel, tq=tq, tk=tk),
        grid_spec=grid_spec,
        out_shape=jax.ShapeDtypeStruct((B, S, DIFF_WIDTH), BF16),
        compiler_params=pltpu.CompilerParams(
            dimension_semantics=("arbitrary", "arbitrary", "arbitrary"),
            vmem_limit_bytes=VMEM_LIMIT),
        name="attn",
    )(slopes, q, k, vt, lq1.reshape(1, -1), lk1.reshape(1, -1), lq2.reshape(1, -1),
      lk2.reshape(1, -1), subln_gain.reshape(1, -1))


def _pack_bf16_pairs(x):
    w = x.shape[1] // 2
    lo = pltpu.bitcast(x[:, :w].astype(BF16).astype(F32), U32)
    hi = pltpu.bitcast(x[:, w:].astype(BF16).astype(F32), U32)
    return (lo >> 16) | (hi & jnp.uint32(0xFFFF0000))


def _unpack_bf16_pairs(p):
    lo = pltpu.bitcast(p << 16, F32)
    hi = pltpu.bitcast(p & jnp.uint32(0xFFFF0000), F32)
    return lo, hi


ROW_SUB = (D_MODEL // 2) // 128


def _store_rows(ref, x2d):
    n = x2d.shape[0]
    for s in range(ROW_SUB):
        ref[pl.ds(s, n, stride=ROW_SUB), :] = x2d[:, s * 128:(s + 1) * 128]


def _load_rows(ref):
    n = ref.shape[0] // ROW_SUB
    return jnp.concatenate([ref[pl.ds(s, n, stride=ROW_SUB), :] for s in range(ROW_SUB)], axis=1)


def _outproj_kernel(yr_ref, yd_ref, x_ref, mod_ref, wo_r_ref, wo_d_ref, g2_ref, rw_hi_ref,
                    rw_lo_ref, rb_ref, x1_out, hp_out, idx_out, w_out, rank_out, cnt_out,
                    carry_ref, *, tm):
    t = pl.program_id(0)

    @pl.when(t == 0)
    def _():
        carry_ref[...] = jnp.zeros_like(carry_ref)

    gate1 = mod_ref[0, 2:3, :]
    shift2 = mod_ref[0, 3:4, :]
    scale2 = mod_ref[0, 4:5, :]
    n_part = max(1, tm // 256)
    rows_per = tm // n_part
    eidx = lax.broadcasted_iota(I32, (rows_per, N_EXPERTS), 1)
    col4 = lax.broadcasted_iota(I32, (rows_per, TOP_K), 1)
    tri = (lax.broadcasted_iota(I32, (rows_per, rows_per), 0) >
           lax.broadcasted_iota(I32, (rows_per, rows_per), 1)).astype(BF16)
    before = [carry_ref[...]]

    def part(i):
        rs = slice(i * rows_per, (i + 1) * rows_per)
        mix = _dot(yr_ref[rs, :], wo_r_ref[...]) + _dot(yd_ref[rs, :], wo_d_ref[...])
        yield
        x1 = x_ref[rs, :] + gate1 * mix
        x1_out[rs, :] = x1
        ms = jnp.mean(x1 * x1, axis=-1, keepdims=True)
        h = x1 * lax.rsqrt(ms + NORM_EPS) * g2_ref[...] * (1.0 + scale2) + shift2
        _store_rows(hp_out.at[pl.ds(i * rows_per * ROW_SUB, rows_per * ROW_SUB), :],
                    _pack_bf16_pairs(h))
        hi = h.astype(BF16)
        lo = (h - hi.astype(F32)).astype(BF16)
        yield
        logits = (_dot(hi, rw_hi_ref[...]) + _dot(hi, rw_lo_ref[...]) + _dot(lo, rw_hi_ref[...])
                  + rb_ref[...])
        yield
        lg = logits
        vals, idxs = [], []
        onehot = jnp.zeros(logits.shape, F32)
        for _ in range(TOP_K):
            m = jnp.max(lg, axis=-1, keepdims=True)
            ix = jnp.min(jnp.where(lg == m, eidx, N_EXPERTS), axis=-1, keepdims=True)
            sel = eidx == ix
            vals.append(m)
            idxs.append(ix)
            onehot = onehot + sel.astype(F32)
            lg = jnp.where(sel, -jnp.inf, lg)
        es = [jnp.exp(v - vals[0]) for v in vals]
        den = es[0] + es[1] + es[2] + es[3]
        before.append(before[i] + jnp.sum(onehot, axis=0, keepdims=True))
        yield
        prefix = _dot(tri, onehot.astype(BF16)) + before[i]
        yield
        idx4 = jnp.zeros((rows_per, TOP_K), I32)
        w4 = jnp.zeros((rows_per, TOP_K), F32)
        rank4 = jnp.zeros((rows_per, TOP_K), I32)
        for kk in range(TOP_K):
            rk = jnp.sum(jnp.where(eidx == idxs[kk], prefix, 0.0), axis=-1, keepdims=True)
            idx4 = jnp.where(col4 == kk, idxs[kk], idx4)
            w4 = jnp.where(col4 == kk, es[kk] / den, w4)
            rank4 = jnp.where(col4 == kk, rk.astype(I32), rank4)
        idx_out[rs, :] = idx4
        w_out[rs, :] = w4
        rank_out[rs, :] = rank4

    _round_robin([part(i) for i in range(n_part)])
    carry_ref[...] = before[n_part]
    cnt_out[...] = before[n_part].astype(I32)


def _outproj_call(y_rwkv, y_diff, x, mod6, w_out, norm2_gain, router_w, router_b, S, tm):
    T = x.shape[0]
    tiles_per_seq = S // tm
    rw_hi = router_w.astype(BF16)
    rw_lo = (router_w - rw_hi.astype(F32)).astype(BF16)
    tok = lambda w: pl.BlockSpec((tm, w), lambda t: (t, 0))
    full = lambda shape: pl.BlockSpec(shape, lambda t: (0,) * len(shape))
    return pl.pallas_call(
        functools.partial(_outproj_kernel, tm=tm),
        grid=(T // tm,),
        in_specs=[tok(RWKV_WIDTH), tok(DIFF_WIDTH), tok(D_MODEL),
                  pl.BlockSpec((1, 6, D_MODEL), lambda t: (t // tiles_per_seq, 0, 0)),
                  full((RWKV_WIDTH, D_MODEL)), full((DIFF_WIDTH, D_MODEL)), full((1, D_MODEL)),
                  full((D_MODEL, N_EXPERTS)), full((D_MODEL, N_EXPERTS)), full((1, N_EXPERTS))],
        out_specs=[tok(D_MODEL), pl.BlockSpec((tm * ROW_SUB, 128), lambda t: (t, 0)),
                   tok(TOP_K), tok(TOP_K), tok(TOP_K), full((1, N_EXPERTS))],
        out_shape=[jax.ShapeDtypeStruct((T, D_MODEL), F32),
                   jax.ShapeDtypeStruct((T * ROW_SUB, 128), U32),
                   jax.ShapeDtypeStruct((T, TOP_K), I32),
                   jax.ShapeDtypeStruct((T, TOP_K), F32),
                   jax.ShapeDtypeStruct((T, TOP_K), I32),
                   jax.ShapeDtypeStruct((1, N_EXPERTS), I32)],
        scratch_shapes=[pltpu.VMEM((1, N_EXPERTS), F32)],
        compiler_params=pltpu.CompilerParams(
            dimension_semantics=("arbitrary",), vmem_limit_bytes=VMEM_LIMIT),
        name="outproj",
    )(y_rwkv, y_diff, x, mod6, w_out[:RWKV_WIDTH].astype(BF16), w_out[RWKV_WIDTH:].astype(BF16),
      norm2_gain.reshape(1, -1), rw_hi, rw_lo, router_b.reshape(1, -1))


def _experts_kernel(be_ref, nu_ref, valid_ref, xs_ref, wgu_ref, bgu_ref, wd_ref, bd_ref, ys_ref,
                    wgu_bf, wd_bf):
    i = pl.program_id(0)

    @pl.when((i == 0) | (be_ref[i] != be_ref[jnp.maximum(i - 1, 0)]))
    def _():
        wgu_bf[...] = wgu_ref[0].astype(BF16)
        wd_bf[...] = wd_ref[0].astype(BF16)

    @pl.when(i < nu_ref[0])
    def _():
        packed = _load_rows(xs_ref)
        row = lax.broadcasted_iota(I32, packed.shape, 0)
        packed = jnp.where(row < valid_ref[i], packed, jnp.uint32(0))
        xa, xb = _unpack_bf16_pairs(packed)
        x = jnp.concatenate([xa.astype(BF16), xb.astype(BF16)], axis=1)
        gu = _dot(x, wgu_bf[...]) + bgu_ref[0]
        gate = jnp.minimum(gu[:, :D_EXPERT], SWIGLU_LIMIT)
        up = jnp.clip(gu[:, D_EXPERT:], -SWIGLU_LIMIT, SWIGLU_LIMIT)
        act = (up + 1.0) * (gate * jax.nn.sigmoid(SWIGLU_ALPHA * gate))
        y = _dot(act.astype(BF16), wd_bf[...]) + bd_ref[0]
        _store_rows(ys_ref, _pack_bf16_pairs(y))

    @pl.when(i >= nu_ref[0])
    def _():
        ys_ref[...] = jnp.zeros_like(ys_ref)


def _experts_call(xs, block_e, n_used, valid_rows, w_gate_up, b_gate_up, w_down, b_down, n_blocks):
    per_expert = lambda shape: pl.BlockSpec(shape, lambda i, be, nu, vr: (be[i], 0, 0))
    grid_spec = pltpu.PrefetchScalarGridSpec(
        num_scalar_prefetch=3,
        grid=(n_blocks,),
        in_specs=[pl.BlockSpec((MOE_BLOCK * ROW_SUB, 128),
                               lambda i, be, nu, vr: (jnp.minimum(i, nu[0] - 1), 0)),
                  per_expert((1, D_MODEL, 2 * D_EXPERT)), per_expert((1, 1, 2 * D_EXPERT)),
                  per_expert((1, D_EXPERT, D_MODEL)), per_expert((1, 1, D_MODEL))],
        out_specs=pl.BlockSpec((MOE_BLOCK * ROW_SUB, 128), lambda i, be, nu, vr: (i, 0)),
        scratch_shapes=[pltpu.VMEM((D_MODEL, 2 * D_EXPERT), BF16),
                        pltpu.VMEM((D_EXPERT, D_MODEL), BF16)])
    return pl.pallas_call(
        _experts_kernel,
        grid_spec=grid_spec,
        out_shape=jax.ShapeDtypeStruct((n_blocks * MOE_BLOCK * ROW_SUB, 128), U32),
        compiler_params=pltpu.CompilerParams(
            dimension_semantics=("arbitrary",), vmem_limit_bytes=VMEM_LIMIT),
        name="experts",
    )(block_e, n_used, valid_rows, xs, w_gate_up, b_gate_up.reshape(N_EXPERTS, 1, -1), w_down,
      b_down.reshape(N_EXPERTS, 1, -1))


SC_WINDOW = 128
COMBINE_PARTS = 4


def _sc_mesh():
    return plsc.VectorSubcoreMesh(core_axis_name="c", subcore_axis_name="s")


def _sc_worker_chunks(n_chunks):
    info = plsc.get_sparse_core_info()
    n_workers = info.num_cores * info.num_subcores
    assert n_chunks % n_workers == 0
    return info.num_cores, n_chunks // n_workers


def _sc_scatter_call(rows, slot_chunks, n_out_rows):
    T = rows.shape[0]
    per_pass = T // SC_WINDOW
    n_cores, per_worker = _sc_worker_chunks(per_pass)

    @functools.partial(
        pl.kernel, mesh=_sc_mesh(),
        out_type=jax.ShapeDtypeStruct((n_out_rows, ROW_SUB, 128), U32),
        scratch_types=[pltpu.VMEM((SC_WINDOW,), I32), pltpu.VMEM((SC_WINDOW, ROW_SUB, 128), U32)],
        name="sc_dispatch")
    def run(rows_hbm, idx_hbm, out_hbm, idx_v, rows_v):
        wid = lax.axis_index("s") * n_cores + lax.axis_index("c")

        @pl.loop(0, per_worker)
        def _(j):
            chunk = wid * per_worker + j
            pltpu.sync_copy(rows_hbm.at[pl.ds(chunk * SC_WINDOW, SC_WINDOW)], rows_v)
            for kk in range(TOP_K):
                pltpu.sync_copy(idx_hbm.at[kk * per_pass + chunk], idx_v)
                pltpu.sync_copy(rows_v, out_hbm.at[idx_v])

    return run(rows, slot_chunks)


def _sc_gather_call(table, slot_chunks):
    n_chunks = slot_chunks.shape[0]
    n_cores, per_worker = _sc_worker_chunks(n_chunks)

    @functools.partial(
        pl.kernel, mesh=_sc_mesh(),
        out_type=jax.ShapeDtypeStruct((n_chunks * SC_WINDOW, ROW_SUB, 128), U32),
        scratch_types=[pltpu.VMEM((SC_WINDOW,), I32), pltpu.VMEM((SC_WINDOW, ROW_SUB, 128), U32)],
        name="sc_collect")
    def run(table_hbm, idx_hbm, out_hbm, idx_v, rows_v):
        wid = lax.axis_index("s") * n_cores + lax.axis_index("c")

        @pl.loop(0, per_worker)
        def _(j):
            chunk = wid * per_worker + j
            pltpu.sync_copy(idx_hbm.at[chunk], idx_v)
            pltpu.sync_copy(table_hbm.at[idx_v], rows_v)
            pltpu.sync_copy(rows_v, out_hbm.at[pl.ds(chunk * SC_WINDOW, SC_WINDOW)])

    return run(table, slot_chunks)


def _combine_dense_kernel(g0_ref, g1_ref, g2_ref, g3_ref, x1_ref, w_ref, mod_ref, fg_ref, o_ref,
                          *, tm):
    w = w_ref[...]
    acc_lo = jnp.zeros((tm, D_MODEL // 2), F32)
    acc_hi = jnp.zeros((tm, D_MODEL // 2), F32)
    for kk, g_ref in enumerate((g0_ref, g1_ref, g2_ref, g3_ref)):
        lo, hi = _unpack_bf16_pairs(_load_rows(g_ref))
        wk = w[:, kk:kk + 1]
        acc_lo = acc_lo + wk * lo
        acc_hi = acc_hi + wk * hi
    moe = jnp.concatenate([acc_lo, acc_hi], axis=1)
    gate2 = mod_ref[0, 5:6, :]
    x2 = x1_ref[...] + gate2 * moe
    ms = jnp.mean(x2 * x2, axis=-1, keepdims=True)
    o_ref[...] = x2 * lax.rsqrt(ms + NORM_EPS) * fg_ref[...]


def _combine_dense_call(gathered, x1, top_w, mod6, final_gain, S, tm, part, n_parts):
    T = x1.shape[0]
    assert T % (tm * n_parts) == 0
    nt = T // tm // n_parts
    first = part * nt
    tiles_per_seq = S // tm
    rows = lambda kk: pl.BlockSpec((tm * ROW_SUB, 128), lambda t: (kk * nt + t, 0))
    return pl.pallas_call(
        functools.partial(_combine_dense_kernel, tm=tm),
        grid=(nt,),
        in_specs=[rows(0), rows(1), rows(2), rows(3),
                  pl.BlockSpec((tm, D_MODEL), lambda t: (first + t, 0)),
                  pl.BlockSpec((tm, TOP_K), lambda t: (first + t, 0)),
                  pl.BlockSpec((1, 6, D_MODEL), lambda t: ((first + t) // tiles_per_seq, 0, 0)),
                  pl.BlockSpec((1, D_MODEL), lambda t: (0, 0))],
        out_specs=pl.BlockSpec((tm, D_MODEL), lambda t: (first + t, 0)),
        out_shape=jax.ShapeDtypeStruct((T, D_MODEL), F32),
        input_output_aliases={4: 0},
        compiler_params=pltpu.CompilerParams(
            dimension_semantics=("arbitrary",), vmem_limit_bytes=VMEM_LIMIT),
        name="combine",
    )(gathered, gathered, gathered, gathered, x1, top_w, mod6, final_gain.reshape(1, -1))


def _forward(x, c, mod_w, mod_b, norm1_gain, w_in, rwkv_shift_mu, rwkv_w0, rwkv_w_up, rwkv_a0,
             rwkv_a_up, rwkv_g_up, rwkv_k_k, rwkv_k_a, rwkv_r_k, rwkv_gn_gain, rwkv_gn_bias,
             diff_lambda_q1, diff_lambda_k1, diff_lambda_q2, diff_lambda_k2, diff_subln_gain,
             w_out, norm2_gain, router_w, router_b, w_gate_up, b_gate_up, w_down, b_down,
             final_gain):
    B, S, D = x.shape
    T = B * S
    tm_in = min(512, S)
    tq = min(512, S)
    tm_out = min(1024, S)
    tm = min(256, S)

    mod6 = _mod_call(c, mod_w, mod_b).reshape(B, 6, D)
    (r, k, v, a, b, g, lw, q, kat, vt) = _inproj_call(
        x, mod6, norm1_gain, w_in, rwkv_shift_mu, rwkv_w0, rwkv_w_up, rwkv_a0, rwkv_a_up,
        rwkv_g_up, rwkv_k_k, rwkv_k_a, tm_in)
    y_rwkv = _rwkv_call(r, k, v, a, b, g, lw, rwkv_r_k, rwkv_gn_gain, rwkv_gn_bias,
                        min(16, S // CHUNK))
    y_diff = _attn_call(q, kat, vt, diff_lambda_q1, diff_lambda_k1, diff_lambda_q2,
                        diff_lambda_k2, diff_subln_gain, tq)

    x1, hp, top_idx, top_w, rank, counts = _outproj_call(
        y_rwkv.reshape(T, -1), y_diff.reshape(T, -1), x.reshape(T, D), mod6, w_out, norm2_gain,
        router_w, router_b, S, tm_out)

    counts = counts.reshape(N_EXPERTS)
    padded = ((counts + MOE_BLOCK - 1) // MOE_BLOCK) * MOE_BLOCK
    pad_ends = jnp.cumsum(padded)
    pad_starts = pad_ends - padded
    n_blocks = -(-(T * TOP_K + N_EXPERTS * (MOE_BLOCK - 1)) // MOE_BLOCK)
    n_used = (pad_ends[-1] // MOE_BLOCK).astype(I32).reshape(1)
    block_start = jnp.minimum(jnp.arange(n_blocks, dtype=I32), n_used[0] - 1) * MOE_BLOCK
    block_e = jnp.minimum(jnp.sum(pad_ends[None, :] <= block_start[:, None], axis=1),
                          N_EXPERTS - 1).astype(I32)
    idx_t = top_idx.T
    expert_ids = jnp.arange(N_EXPERTS, dtype=I32)[:, None, None]
    start_of = jnp.sum(jnp.where(idx_t[None] == expert_ids, pad_starts.astype(I32)[:, None, None], 0),
                       axis=0)
    slots = start_of + rank.T
    slot_chunks = slots.reshape(TOP_K * T // SC_WINDOW, SC_WINDOW)
    block_row = jnp.arange(n_blocks, dtype=I32) * MOE_BLOCK
    valid_rows = jnp.clip(counts.astype(I32)[block_e] - (block_row - pad_starts.astype(I32)[block_e]),
                          0, MOE_BLOCK).astype(I32)

    as_tokens = lambda z: z.reshape(-1, ROW_SUB, 128)
    as_lines = lambda z: z.reshape(-1, 128)
    xs = as_lines(_sc_scatter_call(as_tokens(hp), slot_chunks, n_blocks * MOE_BLOCK))
    ys = _experts_call(xs, block_e, n_used, valid_rows, w_gate_up, b_gate_up, w_down, b_down,
                       n_blocks)
    ys_tokens = as_tokens(ys)
    out = x1
    per_part = T // COMBINE_PARTS
    for part in range(COMBINE_PARTS):
        part_slots = slots[:, part * per_part:(part + 1) * per_part]
        part_chunks = part_slots.reshape(TOP_K * per_part // SC_WINDOW, SC_WINDOW)
        gathered = as_lines(_sc_gather_call(ys_tokens, part_chunks))
        out = _combine_dense_call(gathered, out, top_w, mod6, final_gain, S, tm, part, COMBINE_PARTS)
    return out.reshape(B, S, D)


def kernel(x, c, mod_w, mod_b, norm1_gain, w_in, rwkv_shift_mu, rwkv_w0, rwkv_w_up, rwkv_a0, rwkv_a_up, rwkv_g_up, rwkv_k_k, rwkv_k_a, rwkv_r_k, rwkv_gn_gain, rwkv_gn_bias, diff_lambda_q1, diff_lambda_k1, diff_lambda_q2, diff_lambda_k2, diff_subln_gain, w_out, norm2_gain, router_w, router_b, w_gate_up, b_gate_up, w_down, b_down, final_gain):
    return _forward(x, c, mod_w[0], mod_b[0], norm1_gain[0], w_in[0], rwkv_shift_mu[0], rwkv_w0[0],
                    rwkv_w_up[0], rwkv_a0[0], rwkv_a_up[0], rwkv_g_up[0], rwkv_k_k[0], rwkv_k_a[0],
                    rwkv_r_k[0].reshape(-1), rwkv_gn_gain[0], rwkv_gn_bias[0], diff_lambda_q1[0],
                    diff_lambda_k1[0], diff_lambda_q2[0], diff_lambda_k2[0], diff_subln_gain[0],
                    w_out[0], norm2_gain[0], router_w[0], router_b[0], w_gate_up[0], b_gate_up[0],
                    w_down[0], b_down[0], final_gain)
```

```python
import functools
import math

import jax
import jax.numpy as jnp
from jax import lax
from jax.experimental import pallas as pl
from jax.experimental.pallas import tpu as pltpu
from jax.experimental.pallas import tpu_sc as plsc

F32 = jnp.float32
BF16 = jnp.bfloat16
I32 = jnp.int32
U32 = jnp.uint32

D_MODEL = 1024
RWKV_WIDTH = 512
RWKV_HEADS = 8
HEAD_DIM = 64
RWKV_COLS = 3 * RWKV_WIDTH + 64 + 64 + 128
DIFF_WIDTH = 512
DIFF_HEADS = 4
DIFF_COLS = 3 * DIFF_WIDTH
N_EXPERTS = 32
TOP_K = 4
D_EXPERT = 1024
SWIGLU_LIMIT = 7.0
SWIGLU_ALPHA = 1.702
MOE_BLOCK = 512
NORM_EPS = 1e-5
GN_EPS = 64e-5
LAMBDA_INIT = 0.8 - 0.6 * math.exp(-0.3 * 0)

CHUNK = 64
GROUP = 4
GW = GROUP * HEAD_DIM
VT_ROWS = 128 + 16
KEY_SLABS = 2
VMEM_LIMIT = 56 * 1024 * 1024


def _dot(a, b):
    return jnp.dot(a, b, preferred_element_type=F32)


def _dot_nt(a, b):
    return lax.dot_general(a, b, (((1,), (1,)), ((), ())), preferred_element_type=F32)


def _dot_tn(a, b):
    return lax.dot_general(a, b, (((0,), (0,)), ((), ())), preferred_element_type=F32)


def _round_robin(stages):
    waiting = list(stages)
    running = []
    while waiting or running:
        if waiting:
            running.append(waiting.pop(0))
        for gen in list(running):
            try:
                next(gen)
            except StopIteration:
                running.remove(gen)


def _split_dot(x, w_bf16):
    hi = x.astype(BF16)
    lo = (x - hi.astype(F32)).astype(BF16)
    return _dot(hi, w_bf16) + _dot(lo, w_bf16)


def _mod_kernel(c_ref, w_ref, b_ref, o_ref):
    c = c_ref[...]
    s = c * jax.nn.sigmoid(c)
    o_ref[...] = _dot(s, w_ref[...]) + b_ref[...]


def _mod_call(c, mod_w, mod_b):
    B = c.shape[0]
    n = mod_w.shape[1]
    tn = 1536
    return pl.pallas_call(
        _mod_kernel,
        grid=(n // tn,),
        in_specs=[pl.BlockSpec((B, D_MODEL), lambda j: (0, 0)),
                  pl.BlockSpec((D_MODEL, tn), lambda j: (0, j)),
                  pl.BlockSpec((1, tn), lambda j: (0, j))],
        out_specs=pl.BlockSpec((B, tn), lambda j: (0, j)),
        out_shape=jax.ShapeDtypeStruct((B, n), F32),
        compiler_params=pltpu.CompilerParams(
            dimension_semantics=("arbitrary",), vmem_limit_bytes=VMEM_LIMIT),
        name="mod",
    )(c, mod_w, mod_b.reshape(1, n))


def _inproj_kernel(x_ref, mod_ref, g1_ref, wrw_ref, wat_ref, mu_ref, w0_ref, wup_ref,
                   a0_ref, aup_ref, gup_ref, kk_ref, ka_ref, hsum_ref,
                   r_out, k_out, v_out, a_out, b_out, g_out, lw_out,
                   q_out, kat_out, vt_out, carry_ref, *, tm):
    s = pl.program_id(1)
    shift1 = mod_ref[0, 0:1, :]
    scale1 = mod_ref[0, 1:2, :]

    @pl.when(s == 0)
    def _():
        carry_ref[...] = jnp.zeros_like(carry_ref)

    n_part = max(1, tm // 128)
    rows_per = tm // n_part
    last_row = [carry_ref[...]]

    def part(i):
        rs = slice(i * rows_per, (i + 1) * rows_per)
        x = x_ref[0, rs, :]
        ms = jnp.mean(x * x, axis=-1, keepdims=True)
        h = x * lax.rsqrt(ms + NORM_EPS) * g1_ref[...] * (1.0 + scale1) + shift1
        hb = h.astype(BF16)
        yield
        p = _dot(hb, wrw_ref[...])
        last_row.append(p[rows_per - 1:rows_per, :])
        yield
        pa = _dot(hb, wat_ref[...])
        yield
        rolled = pltpu.roll(p, shift=1, axis=0)
        row = lax.broadcasted_iota(I32, p.shape, 0)
        prev = jnp.where(row == 0, last_row[i], rolled)
        ps = p + mu_ref[...] * (prev - p)
        r = ps[:, 0:512]
        k = ps[:, 512:1024]
        v = ps[:, 1024:1536]
        lo2 = ps[:, 1536:1664]
        g_lo = ps[:, 1664:1792]
        kk = k * kk_ref[...]
        yield
        z = w0_ref[...] + _dot(jnp.tanh(lo2).astype(BF16), wup_ref[...])
        a_pre = a0_ref[...] + _dot(lo2.astype(BF16), aup_ref[...])
        g = _dot(jax.nn.sigmoid(g_lo).astype(BF16), gup_ref[...])
        ssq = _split_dot(kk * kk, hsum_ref[...])
        yield
        nz = -z
        softplus = jnp.maximum(nz, 0.0) + jnp.log(1.0 + jnp.exp(-jnp.abs(nz)))
        w = -softplus - 0.5
        lw_out[0, rs, :] = -jnp.exp(w)
        a = jax.nn.sigmoid(a_pre)
        kk = kk / jnp.maximum(jnp.sqrt(ssq), 1e-12)
        k = k * (1.0 + (a - 1.0) * ka_ref[...])
        r_out[0, rs, :] = r.astype(BF16)
        k_out[0, rs, :] = k.astype(BF16)
        v_out[0, rs, :] = v.astype(BF16)
        a_out[0, rs, :] = (-kk).astype(BF16)
        b_out[0, rs, :] = (kk * a).astype(BF16)
        g_out[0, rs, :] = g.astype(BF16)
        yield
        q_out[0, rs, :] = (pa[:, 0:512] * (1.0 / math.sqrt(HEAD_DIM))).astype(BF16)
        kat_out[0, rs, :] = pa[:, 512:1024].astype(BF16)
        for hh in range(DIFF_HEADS):
            vh = pa[:, 1024 + hh * 128:1024 + (hh + 1) * 128]
            vt_out[0, hh, 0, 0:128, rs] = vh.T.astype(BF16)

    _round_robin([part(i) for i in range(n_part)])
    carry_ref[...] = last_row[n_part]
    for hh in range(DIFF_HEADS):
        vt_out[0, hh, 0, 128:VT_ROWS, :] = jnp.ones((VT_ROWS - 128, tm), BF16)


def _head_sum_matrix(width):
    i = jnp.arange(width) // HEAD_DIM
    return (i[:, None] == i[None, :]).astype(BF16)


def _inproj_call(x, mod6, norm1_gain, w_in, mu, w0, w_up, a0, a_up, g_up, k_k, k_a, tm):
    B, S, _ = x.shape
    ns = S // tm
    w_rw = w_in[:, :RWKV_COLS].astype(BF16)
    w_at = w_in[:, RWKV_COLS:].astype(BF16)
    zeros = jnp.zeros((64, RWKV_WIDTH), F32)
    wup_p = jnp.concatenate([w_up, zeros], axis=0).astype(BF16)
    aup_p = jnp.concatenate([zeros, a_up], axis=0).astype(BF16)
    row = lambda v: v.reshape(1, -1)
    full = lambda shape: pl.BlockSpec(shape, lambda b, s: (0,) * len(shape))
    tok = lambda w: pl.BlockSpec((1, tm, w), lambda b, s: (b, s, 0))
    rw_shape = jax.ShapeDtypeStruct((B, S, RWKV_WIDTH), BF16)
    out_shape = [rw_shape] * 6 + [
        jax.ShapeDtypeStruct((B, S, RWKV_WIDTH), F32),
        jax.ShapeDtypeStruct((B, S, DIFF_WIDTH), BF16),
        jax.ShapeDtypeStruct((B, S, DIFF_WIDTH), BF16),
        jax.ShapeDtypeStruct((B, DIFF_HEADS, ns, VT_ROWS, tm), BF16)]
    out_specs = [tok(RWKV_WIDTH)] * 7 + [tok(DIFF_WIDTH)] * 2 + [
        pl.BlockSpec((1, DIFF_HEADS, 1, VT_ROWS, tm), lambda b, s: (b, 0, s, 0, 0))]
    return pl.pallas_call(
        functools.partial(_inproj_kernel, tm=tm),
        grid=(B, ns),
        in_specs=[tok(D_MODEL),
                  pl.BlockSpec((1, 6, D_MODEL), lambda b, s: (b, 0, 0)),
                  full((1, D_MODEL)),
                  full((D_MODEL, RWKV_COLS)), full((D_MODEL, DIFF_COLS)),
                  full((1, RWKV_COLS)), full((1, RWKV_WIDTH)), full((128, RWKV_WIDTH)),
                  full((1, RWKV_WIDTH)), full((128, RWKV_WIDTH)), full((128, RWKV_WIDTH)),
                  full((1, RWKV_WIDTH)), full((1, RWKV_WIDTH)),
                  full((RWKV_WIDTH, RWKV_WIDTH))],
        out_specs=out_specs,
        out_shape=out_shape,
        scratch_shapes=[pltpu.VMEM((1, RWKV_COLS), F32)],
        compiler_params=pltpu.CompilerParams(
            dimension_semantics=("arbitrary", "arbitrary"), vmem_limit_bytes=VMEM_LIMIT),
        name="inproj",
    )(x, mod6, row(norm1_gain), w_rw, w_at, row(mu), row(w0), wup_p, row(a0), aup_p,
      g_up.astype(BF16), row(k_k), row(k_a), _head_sum_matrix(RWKV_WIDTH))


def _rwkv_kernel(r_ref, k_ref, v_ref, a_ref, b_ref, g_ref, lw_ref, rk_ref, gng_ref, gnb_ref,
                 hsum_ref, y_ref, s0_ref, s1_ref, *, nck):
    C = CHUNK
    n_grp = RWKV_HEADS // GROUP
    s_refs = (s0_ref, s1_ref)

    @pl.when(pl.program_id(1) == 0)
    def _():
        s0_ref[...] = jnp.zeros_like(s0_ref)
        s1_ref[...] = jnp.zeros_like(s1_ref)

    ti = lax.broadcasted_iota(I32, (C, GW), 0)
    si = lax.broadcasted_iota(I32, (C, GW), 1) % C
    incl = si <= ti
    strict = si < ti
    eye = (si == ti).astype(F32)
    bdmask = (lax.broadcasted_iota(I32, (GW, GW), 0) // HEAD_DIM ==
              lax.broadcasted_iota(I32, (GW, GW), 1) // HEAD_DIM)
    hsum = hsum_ref[...]
    chains = [(ck, g) for ck in range(nck) for g in range(n_grp)]

    bdmask_bf = bdmask.astype(BF16)

    def bd(xb):
        return jnp.concatenate([xb] * GROUP, axis=0) * bdmask_bf

    tri =(lax.broadcasted_iota(I32, (C, C), 0) >=
           lax.broadcasted_iota(I32, (C, C), 1)).astype(BF16)
    done = {}
    sv = [s_refs[g][...] for g in range(n_grp)]

    def chain(ch):
        ck, g = ch
        rows = slice(ck * C, (ck + 1) * C)
        cols = slice(g * GW, (g + 1) * GW)
        lw = lw_ref[0, rows, cols]
        L = _split_dot_left(tri, lw)
        yield
        Lx = L - lw
        Lc = L[C - 1:C, :]
        rho = L[C // 2 - 1:C // 2, :]
        r = r_ref[0, rows, cols].astype(F32)
        k = k_ref[0, rows, cols].astype(F32)
        vb = v_ref[0, rows, cols]
        a = a_ref[0, rows, cols].astype(F32)
        b = b_ref[0, rows, cols].astype(F32)
        e_k = jnp.exp(rho - L)
        lhs = jnp.concatenate([(r * jnp.exp(L - rho)).astype(BF16),
                               (a * jnp.exp(Lx - rho)).astype(BF16)], axis=0)
        kbd = bd((k * e_k).astype(BF16))
        bbd = bd((b * e_k).astype(BF16))
        r0 = (r * jnp.exp(L)).astype(BF16)
        a0bd = bd((a * jnp.exp(Lx)).astype(BF16))
        e_o = jnp.exp(Lc - L)
        bk = jnp.concatenate([(b * e_o).astype(BF16), (k * e_o).astype(BF16)], axis=0)
        vbd = bd(vb)
        yield
        ak = _dot_nt(lhs, kbd)
        ab = _dot_nt(lhs, bbd)
        yield
        a_rk = jnp.where(incl, ak[:C], 0.0).astype(BF16)
        a_ak = jnp.where(strict, ak[C:], 0.0).astype(BF16)
        a_rb = jnp.where(incl, ab[:C], 0.0).astype(BF16)
        n = jnp.where(strict, ab[C:], 0.0)
        p = eye + n
        xb = n.astype(BF16)
        xbd = bd(xb)
        yield
        x = _dot(xb, xbd)
        yield
        for _ in range(4):
            xb = x.astype(BF16)
            stack = jnp.concatenate([xb, p.astype(BF16)], axis=0)
            xbd = bd(xb)
            yield
            o = _dot(stack, xbd)
            yield
            x = o[:C]
            p = p + o[C:]
        pb = p.astype(BF16)
        xbd = bd(x.astype(BF16))
        yield
        o = _dot(pb, xbd)
        yield
        tb = (p + o).astype(BF16)
        yield
        av = _dot(jnp.concatenate([a_ak, a_rk], axis=0), vbd)
        w1 = _dot(tb, a0bd)
        yield
        akvbd = bd(av[:C].astype(BF16))
        rw_lhs = jnp.concatenate([r0, w1.astype(BF16)], axis=0)
        yield
        w2 = _dot(tb, akvbd)
        yield
        done[ch] = dict(rw_lhs=rw_lhs, w2=w2, a_rb=a_rb, arkv=av[C:], vb=vb, bk=bk,
                        gc=jnp.exp(Lc), rk=r * k)

    def sequential(ck):
        d = [done[(ck, g)] for g in range(n_grp)]
        rws = [_dot_nt(d[g]["rw_lhs"], sv[g].astype(BF16)) for g in range(n_grp)]
        yield
        ub = [(rws[g][C:] + d[g]["w2"]).astype(BF16) for g in range(n_grp)]
        uv = [jnp.concatenate([ub[g], d[g]["vb"]], axis=0) for g in range(n_grp)]
        ubd = [bd(ub[g]) for g in range(n_grp)]
        yield
        upd = [_dot_tn(uv[g], d[g]["bk"]) for g in range(n_grp)]
        yield
        for g in range(n_grp):
            sv[g] = sv[g] * d[g]["gc"] + jnp.where(bdmask, upd[g], 0.0)
        done[("state", ck)] = True
        y = [rws[g][:C] + _dot(d[g]["a_rb"], ubd[g]) + d[g]["arkv"] for g in range(n_grp)]
        yield
        mean = [_split_dot(y[g], hsum) * (1.0 / HEAD_DIM) for g in range(n_grp)]
        yield
        dev = [y[g] - mean[g] for g in range(n_grp)]
        var = [_split_dot(dev[g] * dev[g], hsum) * (1.0 / HEAD_DIM) for g in range(n_grp)]
        bonus = [_split_dot(d[g]["rk"] * rk_ref[:, g * GW:(g + 1) * GW], hsum) for g in range(n_grp)]
        yield
        rows = slice(ck * C, (ck + 1) * C)
        for g in range(n_grp):
            cols = slice(g * GW, (g + 1) * GW)
            yn = dev[g] * lax.rsqrt(var[g] + GN_EPS) * gng_ref[:, cols] + gnb_ref[:, cols]
            out = (yn + bonus[g] * d[g]["vb"].astype(F32)) * g_ref[0, rows, cols].astype(F32)
            y_ref[0, rows, cols] = out.astype(BF16)

    waiting = [chain(ch) for ch in chains]
    seq_next = 0
    running = []
    while waiting or running or seq_next < nck:
        ready = (seq_next < nck and all((seq_next, g) in done for g in range(n_grp))
                 and (seq_next == 0 or ("state", seq_next - 1) in done))
        if ready:
            running.append(sequential(seq_next))
            seq_next += 1
        elif waiting:
            running.append(waiting.pop(0))
        for gen in list(running):
            try:
                next(gen)
            except StopIteration:
                running.remove(gen)
    for g in range(n_grp):
        s_refs[g][...] = sv[g]


def _split_dot_left(w_bf16, x):
    hi = x.astype(BF16)
    lo = (x - hi.astype(F32)).astype(BF16)
    return _dot(w_bf16, hi) + _dot(w_bf16, lo)


def _rwkv_call(r, k, v, a, b, g, lw, r_k, gn_gain, gn_bias, nck):
    B, S, _ = r.shape
    tb = nck * CHUNK
    tok = pl.BlockSpec((1, tb, RWKV_WIDTH), lambda bb, c: (bb, c, 0))
    par = pl.BlockSpec((1, RWKV_WIDTH), lambda bb, c: (0, 0))
    return pl.pallas_call(
        functools.partial(_rwkv_kernel, nck=nck),
        grid=(B, S // tb),
        in_specs=[tok] * 7 + [par] * 3 + [pl.BlockSpec((GW, GW), lambda bb, c: (0, 0))],
        out_specs=tok,
        out_shape=jax.ShapeDtypeStruct((B, S, RWKV_WIDTH), BF16),
        scratch_shapes=[pltpu.VMEM((GW, GW), F32)] * (RWKV_HEADS // GROUP),
        compiler_params=pltpu.CompilerParams(
            dimension_semantics=("arbitrary", "arbitrary"), vmem_limit_bytes=VMEM_LIMIT),
        name="rwkv",
    )(r, k, v, a, b, g, lw, r_k.reshape(1, -1), gn_gain.reshape(1, -1), gn_bias.reshape(1, -1),
      _head_sum_matrix(GW))


def _tree_reduce(op, x):
    while x.shape[0] > 8:
        h = x.shape[0] // 2
        x = op(x[:h], x[h:])
    return x


def _attn_kernel(slopes_ref, q_ref, k_ref, vt_ref, lq1_ref, lk1_ref, lq2_ref, lk2_ref,
                 gain_ref, o_ref, sa_ref, sb_ref, acc_ref, m_ref, *, tq, tk):
    hh = pl.program_id(1)
    i = pl.program_id(2)
    slope = slopes_ref[hh]

    q = q_ref[0].astype(F32)
    lane = lax.broadcasted_iota(I32, (tq, 256), 1)
    qrow = lax.broadcasted_iota(I32, (tq, 256), 0)
    qa = (qrow >> 5).astype(F32) * (32.0 * slope)
    qb = (qrow & 31).astype(F32) * slope
    qbias = jnp.where(lane < 130, 1.0, jnp.where(lane == 130, -qa, jnp.where(lane == 131, -qb, 0.0)))
    qpad = jnp.concatenate([q, jnp.zeros_like(q)], axis=1)
    q_aug = []
    for c in range(2):
        in_comp = (lane >= c * HEAD_DIM) & (lane < (c + 1) * HEAD_DIM)
        q_aug.append(jnp.where(in_comp, qpad, jnp.where(lane >= 128, qbias, 0.0)))
    klane = lax.broadcasted_iota(I32, (tk, 128), 1)
    krow = lax.broadcasted_iota(I32, (tk, 128), 0)
    ka = (krow >> 5).astype(F32) * (32.0 * slope)
    kb_ = (krow & 31).astype(F32) * slope
    kbias = jnp.where(klane == 0, ka, jnp.where(klane == 1, kb_, jnp.where(klane < 4, 1.0, 0.0))
                      ).astype(BF16)

    qt_both = jnp.concatenate([q_aug[0].T, q_aug[1].T], axis=1).astype(BF16)

    m_ref[...] = jnp.full_like(m_ref, -jnp.inf)
    acc_ref[...] = jnp.zeros_like(acc_ref)

    def scores(j):
        kb = k_ref[0, pl.ds(pl.multiple_of(j * tk, tk), tk), :]
        return _dot(jnp.concatenate([kb, kbias], axis=1), qt_both)

    def absorb(s_ref, j, masked):
        s = s_ref[...]
        off = j * tk - i * tq
        cj = slope * off.astype(F32)
        if masked:
            keep = (lax.broadcasted_iota(I32, (tk, tq), 0) -
                    lax.broadcasted_iota(I32, (tk, tq), 1) + off) <= 0
            s = jnp.where(jnp.concatenate([keep, keep], axis=1), s, -jnp.inf)
        m_loc = jnp.max(_tree_reduce(jnp.maximum, s), axis=0, keepdims=True)
        m_old = m_ref[...]
        m_new = jnp.maximum(m_old, m_loc + cj)
        shift = m_new - cj
        vt = vt_ref[0, 0, j]
        rows = tk // KEY_SLABS
        pv = None
        for r in range(KEY_SLABS):
            pr = jnp.exp((s[r * rows:(r + 1) * rows] - shift).astype(BF16))
            part = _dot(vt[:, r * rows:(r + 1) * rows], pr)
            pv = part if pv is None else pv + part
        alpha = jnp.exp(m_old - m_new)
        acc_ref[...] = alpha * acc_ref[...] + pv
        m_ref[...] = m_new

    n_full = (i * tq) // tk
    sa_ref[...] = scores(0)

    def pair_body(jj, carry):
        j0 = 2 * jj
        sb_ref[...] = scores(j0 + 1)
        absorb(sa_ref, j0, False)
        sa_ref[...] = scores(j0 + 2)
        absorb(sb_ref, j0 + 1, False)
        return carry

    lax.fori_loop(0, n_full // 2, pair_body, 0)

    @pl.when(n_full % 2 == 1)
    def _():
        sb_ref[...] = scores(n_full)
        absorb(sa_ref, n_full - 1, False)
        absorb(sb_ref, n_full, True)

    @pl.when(n_full % 2 == 0)
    def _():
        absorb(sa_ref, n_full, True)

    lam = (jnp.exp(jnp.sum(lq1_ref[...] * lk1_ref[...], axis=-1, keepdims=True))
           - jnp.exp(jnp.sum(lq2_ref[...] * lk2_ref[...], axis=-1, keepdims=True))
           + LAMBDA_INIT)
    o2 = acc_ref[0:128, :] * (1.0 / acc_ref[128:129, :])
    o = o2[:, :tq] - lam * o2[:, tq:]
    ot = o.T
    ms = jnp.mean(ot * ot, axis=-1, keepdims=True)
    y = ot * lax.rsqrt(ms + NORM_EPS) * gain_ref[...] * (1.0 - LAMBDA_INIT)
    o_ref[0] = y.astype(BF16)


def _attn_call(q, k, vt, lq1, lk1, lq2, lk2, subln_gain, tq):
    B, S, _ = q.shape
    ns, tk = vt.shape[2], vt.shape[4]
    slopes = jnp.asarray([2.0 ** (-8.0 * (i + 1) / DIFF_HEADS) for i in range(DIFF_HEADS)], F32)
    vec = lambda n: pl.BlockSpec((1, n), lambda b, h, i, sl: (0, 0))
    grid_spec = pltpu.PrefetchScalarGridSpec(
        num_scalar_prefetch=1,
        grid=(B, DIFF_HEADS, S // tq),
        in_specs=[pl.BlockSpec((1, tq, 128), lambda b, h, i, sl: (b, i, h)),
                  pl.BlockSpec((1, S, 128), lambda b, h, i, sl: (b, 0, h)),
                  pl.BlockSpec((1, 1, ns, VT_ROWS, tk), lambda b, h, i, sl: (b, h, 0, 0, 0)),
                  vec(HEAD_DIM), vec(HEAD_DIM), vec(HEAD_DIM), vec(HEAD_DIM), vec(128)],
        out_specs=pl.BlockSpec((1, tq, 128), lambda b, h, i, sl: (b, i, h)),
        scratch_shapes=[pltpu.VMEM((tk, 2 * tq), F32), pltpu.VMEM((tk, 2 * tq), F32),
                        pltpu.VMEM((VT_ROWS, 2 * tq), F32), pltpu.VMEM((1, 2 * tq), F32)])
    return pl.pallas_call(
        functools.partial(_attn_kernel, tq=tq, tk=tk),
        grid_spec=grid_spec,
        out_shape=jax.ShapeDtypeStruct((B, S, DIFF_WIDTH), BF16),
        compiler_params=pltpu.CompilerParams(
            dimension_semantics=("arbitrary", "arbitrary", "arbitrary"),
            vmem_limit_bytes=VMEM_LIMIT),
        name="attn",
    )(slopes, q, k, vt, lq1.reshape(1, -1), lk1.reshape(1, -1), lq2.reshape(1, -1),
      lk2.reshape(1, -1), subln_gain.reshape(1, -1))


def _pack_bf16_pairs(x):
    w = x.shape[1] // 2
    lo = pltpu.bitcast(x[:, :w].astype(BF16).astype(F32), U32)
    hi = pltpu.bitcast(x[:, w:].astype(BF16).astype(F32), U32)
    return (lo >> 16) | (hi & jnp.uint32(0xFFFF0000))


def _unpack_bf16_pairs(p):
    lo = pltpu.bitcast(p << 16, F32)
    hi = pltpu.bitcast(p & jnp.uint32(0xFFFF0000), F32)
    return lo, hi


ROW_SUB = (D_MODEL // 2) // 128


def _store_rows(ref, x2d):
    n = x2d.shape[0]
    for s in range(ROW_SUB):
        ref[pl.ds(s, n, stride=ROW_SUB), :] = x2d[:, s * 128:(s + 1) * 128]


def _load_rows(ref):
    n = ref.shape[0] // ROW_SUB
    return jnp.concatenate([ref[pl.ds(s, n, stride=ROW_SUB), :] for s in range(ROW_SUB)], axis=1)


def _outproj_kernel(yr_ref, yd_ref, x_ref, mod_ref, wo_r_ref, wo_d_ref, g2_ref, rw_hi_ref,
                    rw_lo_ref, rb_ref, x1_out, hp_out, idx_out, w_out, rank_out, cnt_out,
                    carry_ref, *, tm):
    t = pl.program_id(0)

    @pl.when(t == 0)
    def _():
        carry_ref[...] = jnp.zeros_like(carry_ref)

    gate1 = mod_ref[0, 2:3, :]
    shift2 = mod_ref[0, 3:4, :]
    scale2 = mod_ref[0, 4:5, :]
    n_part = max(1, tm // 256)
    rows_per = tm // n_part
    eidx = lax.broadcasted_iota(I32, (rows_per, N_EXPERTS), 1)
    col4 = lax.broadcasted_iota(I32, (rows_per, TOP_K), 1)
    tri = (lax.broadcasted_iota(I32, (rows_per, rows_per), 0) >
           lax.broadcasted_iota(I32, (rows_per, rows_per), 1)).astype(BF16)
    before = [carry_ref[...]]

    def part(i):
        rs = slice(i * rows_per, (i + 1) * rows_per)
        mix = _dot(yr_ref[rs, :], wo_r_ref[...]) + _dot(yd_ref[rs, :], wo_d_ref[...])
        yield
        x1 = x_ref[rs, :] + gate1 * mix
        x1_out[rs, :] = x1
        ms = jnp.mean(x1 * x1, axis=-1, keepdims=True)
        h = x1 * lax.rsqrt(ms + NORM_EPS) * g2_ref[...] * (1.0 + scale2) + shift2
        _store_rows(hp_out.at[pl.ds(i * rows_per * ROW_SUB, rows_per * ROW_SUB), :],
                    _pack_bf16_pairs(h))
        hi = h.astype(BF16)
        lo = (h - hi.astype(F32)).astype(BF16)
        yield
        logits = (_dot(hi, rw_hi_ref[...]) + _dot(hi, rw_lo_ref[...]) + _dot(lo, rw_hi_ref[...])
                  + rb_ref[...])
        yield
        lg = logits
        vals, idxs = [], []
        onehot = jnp.zeros(logits.shape, F32)
        for _ in range(TOP_K):
            m = jnp.max(lg, axis=-1, keepdims=True)
            ix = jnp.min(jnp.where(lg == m, eidx, N_EXPERTS), axis=-1, keepdims=True)
            sel = eidx == ix
            vals.append(m)
            idxs.append(ix)
            onehot = onehot + sel.astype(F32)
            lg = jnp.where(sel, -jnp.inf, lg)
        es = [jnp.exp(v - vals[0]) for v in vals]
        den = es[0] + es[1] + es[2] + es[3]
        before.append(before[i] + jnp.sum(onehot, axis=0, keepdims=True))
        yield
        prefix = _dot(tri, onehot.astype(BF16)) + before[i]
        yield
        idx4 = jnp.zeros((rows_per, TOP_K), I32)
        w4 = jnp.zeros((rows_per, TOP_K), F32)
        rank4 = jnp.zeros((rows_per, TOP_K), I32)
        for kk in range(TOP_K):
            rk = jnp.sum(jnp.where(eidx == idxs[kk], prefix, 0.0), axis=-1, keepdims=True)
            idx4 = jnp.where(col4 == kk, idxs[kk], idx4)
            w4 = jnp.where(col4 == kk, es[kk] / den, w4)
            rank4 = jnp.where(col4 == kk, rk.astype(I32), rank4)
        idx_out[rs, :] = idx4
        w_out[rs, :] = w4
        rank_out[rs, :] = rank4

    _round_robin([part(i) for i in range(n_part)])
    carry_ref[...] = before[n_part]
    cnt_out[...] = before[n_part].astype(I32)


def _outproj_call(y_rwkv, y_diff, x, mod6, w_out, norm2_gain, router_w, router_b, S, tm):
    T = x.shape[0]
    tiles_per_seq = S // tm
    rw_hi = router_w.astype(BF16)
    rw_lo = (router_w - rw_hi.astype(F32)).astype(BF16)
    tok = lambda w: pl.BlockSpec((tm, w), lambda t: (t, 0))
    full = lambda shape: pl.BlockSpec(shape, lambda t: (0,) * len(shape))
    return pl.pallas_call(
        functools.partial(_outproj_kernel, tm=tm),
        grid=(T // tm,),
        in_specs=[tok(RWKV_WIDTH), tok(DIFF_WIDTH), tok(D_MODEL),
                  pl.BlockSpec((1, 6, D_MODEL), lambda t: (t // tiles_per_seq, 0, 0)),
                  full((RWKV_WIDTH, D_MODEL)), full((DIFF_WIDTH, D_MODEL)), full((1, D_MODEL)),
                  full((D_MODEL, N_EXPERTS)), full((D_MODEL, N_EXPERTS)), full((1, N_EXPERTS))],
        out_specs=[tok(D_MODEL), pl.BlockSpec((tm * ROW_SUB, 128), lambda t: (t, 0)),
                   tok(TOP_K), tok(TOP_K), tok(TOP_K), full((1, N_EXPERTS))],
        out_shape=[jax.ShapeDtypeStruct((T, D_MODEL), F32),
                   jax.ShapeDtypeStruct((T * ROW_SUB, 128), U32),
                   jax.ShapeDtypeStruct((T, TOP_K), I32),
                   jax.ShapeDtypeStruct((T, TOP_K), F32),
                   jax.ShapeDtypeStruct((T, TOP_K), I32),
                   jax.ShapeDtypeStruct((1, N_EXPERTS), I32)],
        scratch_shapes=[pltpu.VMEM((1, N_EXPERTS), F32)],
        compiler_params=pltpu.CompilerParams(
            dimension_semantics=("arbitrary",), vmem_limit_bytes=VMEM_LIMIT),
        name="outproj",
    )(y_rwkv, y_diff, x, mod6, w_out[:RWKV_WIDTH].astype(BF16), w_out[RWKV_WIDTH:].astype(BF16),
      norm2_gain.reshape(1, -1), rw_hi, rw_lo, router_b.reshape(1, -1))


def _experts_kernel(be_ref, nu_ref, valid_ref, xs_ref, wgu_ref, bgu_ref, wd_ref, bd_ref, ys_ref,
                    wgu_bf, wd_bf):
    i = pl.program_id(0)

    @pl.when((i == 0) | (be_ref[i] != be_ref[jnp.maximum(i - 1, 0)]))
    def _():
        wgu_bf[...] = wgu_ref[0].astype(BF16)
        wd_bf[...] = wd_ref[0].astype(BF16)

    def mlp(n_rows):
        n_lines = n_rows * ROW_SUB
        packed = _load_rows(xs_ref.at[pl.ds(0, n_lines), :])
        row = lax.broadcasted_iota(I32, packed.shape, 0)
        packed = jnp.where(row < valid_ref[i], packed, jnp.uint32(0))
        xa, xb = _unpack_bf16_pairs(packed)
        x = jnp.concatenate([xa.astype(BF16), xb.astype(BF16)], axis=1)
        gu = _dot(x, wgu_bf[...]) + bgu_ref[0]
        gate = jnp.minimum(gu[:, :D_EXPERT], SWIGLU_LIMIT)
        up = jnp.clip(gu[:, D_EXPERT:], -SWIGLU_LIMIT, SWIGLU_LIMIT)
        act = (up + 1.0) * (gate * jax.nn.sigmoid(SWIGLU_ALPHA * gate))
        y = _dot(act.astype(BF16), wd_bf[...]) + bd_ref[0]
        _store_rows(ys_ref.at[pl.ds(0, n_lines), :], _pack_bf16_pairs(y))
        if n_rows < MOE_BLOCK:
            ys_ref[pl.ds(n_lines, MOE_BLOCK * ROW_SUB - n_lines), :] = jnp.zeros(
                (MOE_BLOCK * ROW_SUB - n_lines, 128), U32)

    used = i < nu_ref[0]
    short = valid_ref[i] <= MOE_BLOCK // 2

    @pl.when(used & jnp.logical_not(short))
    def _():
        mlp(MOE_BLOCK)

    @pl.when(used & short)
    def _():
        mlp(MOE_BLOCK // 2)

    @pl.when(i >= nu_ref[0])
    def _():
        ys_ref[...] = jnp.zeros_like(ys_ref)


def _experts_call(xs, block_e, n_used, valid_rows, w_gate_up, b_gate_up, w_down, b_down, n_blocks):
    per_expert = lambda shape: pl.BlockSpec(shape, lambda i, be, nu, vr: (be[i], 0, 0))
    grid_spec = pltpu.PrefetchScalarGridSpec(
        num_scalar_prefetch=3,
        grid=(n_blocks,),
        in_specs=[pl.BlockSpec((MOE_BLOCK * ROW_SUB, 128),
                               lambda i, be, nu, vr: (jnp.minimum(i, nu[0] - 1), 0)),
                  per_expert((1, D_MODEL, 2 * D_EXPERT)), per_expert((1, 1, 2 * D_EXPERT)),
                  per_expert((1, D_EXPERT, D_MODEL)), per_expert((1, 1, D_MODEL))],
        out_specs=pl.BlockSpec((MOE_BLOCK * ROW_SUB, 128), lambda i, be, nu, vr: (i, 0)),
        scratch_shapes=[pltpu.VMEM((D_MODEL, 2 * D_EXPERT), BF16),
                        pltpu.VMEM((D_EXPERT, D_MODEL), BF16)])
    return pl.pallas_call(
        _experts_kernel,
        grid_spec=grid_spec,
        out_shape=jax.ShapeDtypeStruct((n_blocks * MOE_BLOCK * ROW_SUB, 128), U32),
        compiler_params=pltpu.CompilerParams(
            dimension_semantics=("arbitrary",), vmem_limit_bytes=VMEM_LIMIT),
        name="experts",
    )(block_e, n_used, valid_rows, xs, w_gate_up, b_gate_up.reshape(N_EXPERTS, 1, -1), w_down,
      b_down.reshape(N_EXPERTS, 1, -1))


SC_WINDOW = 128
COMBINE_PARTS = 8


def _sc_mesh():
    return plsc.VectorSubcoreMesh(core_axis_name="c", subcore_axis_name="s")


def _sc_worker_chunks(n_chunks):
    info = plsc.get_sparse_core_info()
    n_workers = info.num_cores * info.num_subcores
    assert n_chunks % n_workers == 0
    return info.num_cores, n_chunks // n_workers


def _sc_scatter_call(rows, slot_chunks, n_out_rows):
    T = rows.shape[0]
    per_pass = T // SC_WINDOW
    n_cores, per_worker = _sc_worker_chunks(per_pass)

    @functools.partial(
        pl.kernel, mesh=_sc_mesh(),
        out_type=jax.ShapeDtypeStruct((n_out_rows, ROW_SUB, 128), U32),
        scratch_types=[pltpu.VMEM((SC_WINDOW,), I32), pltpu.VMEM((SC_WINDOW, ROW_SUB, 128), U32)],
        name="sc_dispatch")
    def run(rows_hbm, idx_hbm, out_hbm, idx_v, rows_v):
        wid = lax.axis_index("s") * n_cores + lax.axis_index("c")

        @pl.loop(0, per_worker)
        def _(j):
            chunk = wid * per_worker + j
            pltpu.sync_copy(rows_hbm.at[pl.ds(chunk * SC_WINDOW, SC_WINDOW)], rows_v)
            for kk in range(TOP_K):
                pltpu.sync_copy(idx_hbm.at[kk * per_pass + chunk], idx_v)
                pltpu.sync_copy(rows_v, out_hbm.at[idx_v])

    return run(rows, slot_chunks)


def _sc_gather_call(table, slot_chunks):
    n_chunks = slot_chunks.shape[0]
    n_cores, per_worker = _sc_worker_chunks(n_chunks)

    @functools.partial(
        pl.kernel, mesh=_sc_mesh(),
        out_type=jax.ShapeDtypeStruct((n_chunks * SC_WINDOW, ROW_SUB, 128), U32),
        scratch_types=[pltpu.VMEM((SC_WINDOW,), I32), pltpu.VMEM((SC_WINDOW, ROW_SUB, 128), U32)],
        name="sc_collect")
    def run(table_hbm, idx_hbm, out_hbm, idx_v, rows_v):
        wid = lax.axis_index("s") * n_cores + lax.axis_index("c")

        @pl.loop(0, per_worker)
        def _(j):
            chunk = wid * per_worker + j
            pltpu.sync_copy(idx_hbm.at[chunk], idx_v)
            pltpu.sync_copy(table_hbm.at[idx_v], rows_v)
            pltpu.sync_copy(rows_v, out_hbm.at[pl.ds(chunk * SC_WINDOW, SC_WINDOW)])

    return run(table, slot_chunks)


def _combine_dense_kernel(g0_ref, g1_ref, g2_ref, g3_ref, x1_ref, w_ref, mod_ref, fg_ref, o_ref,
                          *, tm):
    w = w_ref[...]
    acc_lo = jnp.zeros((tm, D_MODEL // 2), F32)
    acc_hi = jnp.zeros((tm, D_MODEL // 2), F32)
    for kk, g_ref in enumerate((g0_ref, g1_ref, g2_ref, g3_ref)):
        lo, hi = _unpack_bf16_pairs(_load_rows(g_ref))
        wk = w[:, kk:kk + 1]
        acc_lo = acc_lo + wk * lo
        acc_hi = acc_hi + wk * hi
    moe = jnp.concatenate([acc_lo, acc_hi], axis=1)
    gate2 = mod_ref[0, 5:6, :]
    x2 = x1_ref[...] + gate2 * moe
    ms = jnp.mean(x2 * x2, axis=-1, keepdims=True)
    o_ref[...] = x2 * lax.rsqrt(ms + NORM_EPS) * fg_ref[...]


def _combine_dense_call(gathered, x1, top_w, mod6, final_gain, S, tm, part, n_parts):
    T = x1.shape[0]
    assert T % (tm * n_parts) == 0
    nt = T // tm // n_parts
    first = part * nt
    tiles_per_seq = S // tm
    rows = lambda kk: pl.BlockSpec((tm * ROW_SUB, 128), lambda t: (kk * nt + t, 0))
    return pl.pallas_call(
        functools.partial(_combine_dense_kernel, tm=tm),
        grid=(nt,),
        in_specs=[rows(0), rows(1), rows(2), rows(3),
                  pl.BlockSpec((tm, D_MODEL), lambda t: (first + t, 0)),
                  pl.BlockSpec((tm, TOP_K), lambda t: (first + t, 0)),
                  pl.BlockSpec((1, 6, D_MODEL), lambda t: ((first + t) // tiles_per_seq, 0, 0)),
                  pl.BlockSpec((1, D_MODEL), lambda t: (0, 0))],
        out_specs=pl.BlockSpec((tm, D_MODEL), lambda t: (first + t, 0)),
        out_shape=jax.ShapeDtypeStruct((T, D_MODEL), F32),
        input_output_aliases={4: 0},
        compiler_params=pltpu.CompilerParams(
            dimension_semantics=("arbitrary",), vmem_limit_bytes=VMEM_LIMIT),
        name="combine",
    )(gathered, gathered, gathered, gathered, x1, top_w, mod6, final_gain.reshape(1, -1))


def _forward(x, c, mod_w, mod_b, norm1_gain, w_in, rwkv_shift_mu, rwkv_w0, rwkv_w_up, rwkv_a0,
             rwkv_a_up, rwkv_g_up, rwkv_k_k, rwkv_k_a, rwkv_r_k, rwkv_gn_gain, rwkv_gn_bias,
             diff_lambda_q1, diff_lambda_k1, diff_lambda_q2, diff_lambda_k2, diff_subln_gain,
             w_out, norm2_gain, router_w, router_b, w_gate_up, b_gate_up, w_down, b_down,
             final_gain):
    B, S, D = x.shape
    T = B * S
    tm_in = min(512, S)
    tq = min(512, S)
    tm_out = min(1024, S)
    tm = min(256, S)

    mod6 = _mod_call(c, mod_w, mod_b).reshape(B, 6, D)
    (r, k, v, a, b, g, lw, q, kat, vt) = _inproj_call(
        x, mod6, norm1_gain, w_in, rwkv_shift_mu, rwkv_w0, rwkv_w_up, rwkv_a0, rwkv_a_up,
        rwkv_g_up, rwkv_k_k, rwkv_k_a, tm_in)
    y_rwkv = _rwkv_call(r, k, v, a, b, g, lw, rwkv_r_k, rwkv_gn_gain, rwkv_gn_bias,
                        min(16, S // CHUNK))
    y_diff = _attn_call(q, kat, vt, diff_lambda_q1, diff_lambda_k1, diff_lambda_q2,
                        diff_lambda_k2, diff_subln_gain, tq)

    x1, hp, top_idx, top_w, rank, counts = _outproj_call(
        y_rwkv.reshape(T, -1), y_diff.reshape(T, -1), x.reshape(T, D), mod6, w_out, norm2_gain,
        router_w, router_b, S, tm_out)

    counts = counts.reshape(N_EXPERTS)
    padded = ((counts + MOE_BLOCK - 1) // MOE_BLOCK) * MOE_BLOCK
    pad_ends = jnp.cumsum(padded)
    pad_starts = pad_ends - padded
    n_blocks = -(-(T * TOP_K + N_EXPERTS * (MOE_BLOCK - 1)) // MOE_BLOCK)
    n_used = (pad_ends[-1] // MOE_BLOCK).astype(I32).reshape(1)
    block_start = jnp.minimum(jnp.arange(n_blocks, dtype=I32), n_used[0] - 1) * MOE_BLOCK
    block_e = jnp.minimum(jnp.sum(pad_ends[None, :] <= block_start[:, None], axis=1),
                          N_EXPERTS - 1).astype(I32)
    idx_t = top_idx.T
    expert_ids = jnp.arange(N_EXPERTS, dtype=I32)[:, None, None]
    start_of = jnp.sum(jnp.where(idx_t[None] == expert_ids, pad_starts.astype(I32)[:, None, None], 0),
                       axis=0)
    slots = start_of + rank.T
    slot_chunks = slots.reshape(TOP_K * T // SC_WINDOW, SC_WINDOW)
    block_row = jnp.arange(n_blocks, dtype=I32) * MOE_BLOCK
    valid_rows = jnp.clip(counts.astype(I32)[block_e] - (block_row - pad_starts.astype(I32)[block_e]),
                          0, MOE_BLOCK).astype(I32)

    as_tokens = lambda z: z.reshape(-1, ROW_SUB, 128)
    as_lines = lambda z: z.reshape(-1, 128)
    xs = as_lines(_sc_scatter_call(as_tokens(hp), slot_chunks, n_blocks * MOE_BLOCK))
    ys = _experts_call(xs, block_e, n_used, valid_rows, w_gate_up, b_gate_up, w_down, b_down,
                       n_blocks)
    ys_tokens = as_tokens(ys)
    out = x1
    per_part = T // COMBINE_PARTS
    for part in range(COMBINE_PARTS):
        part_slots = slots[:, part * per_part:(part + 1) * per_part]
        part_chunks = part_slots.reshape(TOP_K * per_part // SC_WINDOW, SC_WINDOW)
        gathered = as_lines(_sc_gather_call(ys_tokens, part_chunks))
        out = _combine_dense_call(gathered, out, top_w, mod6, final_gain, S, tm, part, COMBINE_PARTS)
    return out.reshape(B, S, D)


def kernel(x, c, mod_w, mod_b, norm1_gain, w_in, rwkv_shift_mu, rwkv_w0, rwkv_w_up, rwkv_a0, rwkv_a_up, rwkv_g_up, rwkv_k_k, rwkv_k_a, rwkv_r_k, rwkv_gn_gain, rwkv_gn_bias, diff_lambda_q1, diff_lambda_k1, diff_lambda_q2, diff_lambda_k2, diff_subln_gain, w_out, norm2_gain, router_w, router_b, w_gate_up, b_gate_up, w_down, b_down, final_gain):
    return _forward(x, c, mod_w[0], mod_b[0], norm1_gain[0], w_in[0], rwkv_shift_mu[0], rwkv_w0[0],
                    rwkv_w_up[0], rwkv_a0[0], rwkv_a_up[0], rwkv_g_up[0], rwkv_k_k[0], rwkv_k_a[0],
                    rwkv_r_k[0].reshape(-1), rwkv_gn_gain[0], rwkv_gn_bias[0], diff_lambda_q1[0],
                    diff_lambda_k1[0], diff_lambda_q2[0], diff_lambda_k2[0], diff_subln_gain[0],
                    w_out[0], norm2_gain[0], router_w[0], router_b[0], w_gate_up[0], b_gate_up[0],
                    w_down[0], b_down[0], final_gain)
```

```python
import functools
import math

import jax
import jax.numpy as jnp
from jax import lax
from jax.experimental import pallas as pl
from jax.experimental.pallas import tpu as pltpu
from jax.experimental.pallas import tpu_sc as plsc

F32 = jnp.float32
BF16 = jnp.bfloat16
I32 = jnp.int32
U32 = jnp.uint32

D_MODEL = 1024
RWKV_WIDTH = 512
RWKV_HEADS = 8
HEAD_DIM = 64
RWKV_COLS = 3 * RWKV_WIDTH + 64 + 64 + 128
DIFF_WIDTH = 512
DIFF_HEADS = 4
DIFF_COLS = 3 * DIFF_WIDTH
N_EXPERTS = 32
TOP_K = 4
D_EXPERT = 1024
SWIGLU_LIMIT = 7.0
SWIGLU_ALPHA = 1.702
MOE_BLOCK = 512
NORM_EPS = 1e-5
GN_EPS = 64e-5
LAMBDA_INIT = 0.8 - 0.6 * math.exp(-0.3 * 0)

CHUNK = 64
GROUP = 4
GW = GROUP * HEAD_DIM
VT_ROWS = 128 + 16
KEY_SLABS = 2
LOOP_BLOCKS = 4
VMEM_LIMIT = 56 * 1024 * 1024


def _dot(a, b):
    return jnp.dot(a, b, preferred_element_type=F32)


def _dot_nt(a, b):
    return lax.dot_general(a, b, (((1,), (1,)), ((), ())), preferred_element_type=F32)


def _dot_tn(a, b):
    return lax.dot_general(a, b, (((0,), (0,)), ((), ())), preferred_element_type=F32)


def _round_robin(stages):
    waiting = list(stages)
    running = []
    while waiting or running:
        if waiting:
            running.append(waiting.pop(0))
        for gen in list(running):
            try:
                next(gen)
            except StopIteration:
                running.remove(gen)


def _split_dot(x, w_bf16):
    hi = x.astype(BF16)
    lo = (x - hi.astype(F32)).astype(BF16)
    return _dot(hi, w_bf16) + _dot(lo, w_bf16)


def _mod_kernel(c_ref, w_ref, b_ref, o_ref):
    c = c_ref[...]
    s = c * jax.nn.sigmoid(c)
    o_ref[...] = _dot(s, w_ref[...]) + b_ref[...]


def _mod_call(c, mod_w, mod_b):
    B = c.shape[0]
    n = mod_w.shape[1]
    tn = 1536
    return pl.pallas_call(
        _mod_kernel,
        grid=(n // tn,),
        in_specs=[pl.BlockSpec((B, D_MODEL), lambda j: (0, 0)),
                  pl.BlockSpec((D_MODEL, tn), lambda j: (0, j)),
                  pl.BlockSpec((1, tn), lambda j: (0, j))],
        out_specs=pl.BlockSpec((B, tn), lambda j: (0, j)),
        out_shape=jax.ShapeDtypeStruct((B, n), F32),
        compiler_params=pltpu.CompilerParams(
            dimension_semantics=("arbitrary",), vmem_limit_bytes=VMEM_LIMIT),
        name="mod",
    )(c, mod_w, mod_b.reshape(1, n))


def _inproj_kernel(x_ref, mod_ref, g1_ref, wrw_ref, wat_ref, mu_ref, w0_ref, wup_ref,
                   a0_ref, aup_ref, gup_ref, kk_ref, ka_ref, hsum_ref,
                   r_out, k_out, v_out, a_out, b_out, g_out, lw_out,
                   q_out, kat_out, vt_out, carry_ref, *, tm):
    s = pl.program_id(1)
    shift1 = mod_ref[0, 0:1, :]
    scale1 = mod_ref[0, 1:2, :]

    @pl.when(s == 0)
    def _():
        carry_ref[...] = jnp.zeros_like(carry_ref)

    n_part = max(1, tm // 128)
    rows_per = tm // n_part
    last_row = [carry_ref[...]]

    def part(i):
        rs = slice(i * rows_per, (i + 1) * rows_per)
        x = x_ref[0, rs, :]
        ms = jnp.mean(x * x, axis=-1, keepdims=True)
        h = x * lax.rsqrt(ms + NORM_EPS) * g1_ref[...] * (1.0 + scale1) + shift1
        hb = h.astype(BF16)
        yield
        p = _dot(hb, wrw_ref[...])
        last_row.append(p[rows_per - 1:rows_per, :])
        yield
        pa = _dot(hb, wat_ref[...])
        yield
        rolled = pltpu.roll(p, shift=1, axis=0)
        row = lax.broadcasted_iota(I32, p.shape, 0)
        prev = jnp.where(row == 0, last_row[i], rolled)
        ps = p + mu_ref[...] * (prev - p)
        r = ps[:, 0:512]
        k = ps[:, 512:1024]
        v = ps[:, 1024:1536]
        lo2 = ps[:, 1536:1664]
        g_lo = ps[:, 1664:1792]
        kk = k * kk_ref[...]
        yield
        z = w0_ref[...] + _dot(jnp.tanh(lo2).astype(BF16), wup_ref[...])
        a_pre = a0_ref[...] + _dot(lo2.astype(BF16), aup_ref[...])
        g = _dot(jax.nn.sigmoid(g_lo).astype(BF16), gup_ref[...])
        ssq = _split_dot(kk * kk, hsum_ref[...])
        yield
        nz = -z
        softplus = jnp.maximum(nz, 0.0) + jnp.log(1.0 + jnp.exp(-jnp.abs(nz)))
        w = -softplus - 0.5
        lw_out[0, rs, :] = -jnp.exp(w)
        a = jax.nn.sigmoid(a_pre)
        kk = kk / jnp.maximum(jnp.sqrt(ssq), 1e-12)
        k = k * (1.0 + (a - 1.0) * ka_ref[...])
        r_out[0, rs, :] = r.astype(BF16)
        k_out[0, rs, :] = k.astype(BF16)
        v_out[0, rs, :] = v.astype(BF16)
        a_out[0, rs, :] = (-kk).astype(BF16)
        b_out[0, rs, :] = (kk * a).astype(BF16)
        g_out[0, rs, :] = g.astype(BF16)
        yield
        q_out[0, rs, :] = (pa[:, 0:512] * (1.0 / math.sqrt(HEAD_DIM))).astype(BF16)
        kat_out[0, rs, :] = pa[:, 512:1024].astype(BF16)
        for hh in range(DIFF_HEADS):
            vh = pa[:, 1024 + hh * 128:1024 + (hh + 1) * 128]
            vt_out[0, hh, 0, 0:128, rs] = vh.T.astype(BF16)

    _round_robin([part(i) for i in range(n_part)])
    carry_ref[...] = last_row[n_part]
    for hh in range(DIFF_HEADS):
        vt_out[0, hh, 0, 128:VT_ROWS, :] = jnp.ones((VT_ROWS - 128, tm), BF16)


def _head_sum_matrix(width):
    i = jnp.arange(width) // HEAD_DIM
    return (i[:, None] == i[None, :]).astype(BF16)


def _inproj_call(x, mod6, norm1_gain, w_in, mu, w0, w_up, a0, a_up, g_up, k_k, k_a, tm):
    B, S, _ = x.shape
    ns = S // tm
    w_rw = w_in[:, :RWKV_COLS].astype(BF16)
    w_at = w_in[:, RWKV_COLS:].astype(BF16)
    zeros = jnp.zeros((64, RWKV_WIDTH), F32)
    wup_p = jnp.concatenate([w_up, zeros], axis=0).astype(BF16)
    aup_p = jnp.concatenate([zeros, a_up], axis=0).astype(BF16)
    row = lambda v: v.reshape(1, -1)
    full = lambda shape: pl.BlockSpec(shape, lambda b, s: (0,) * len(shape))
    tok = lambda w: pl.BlockSpec((1, tm, w), lambda b, s: (b, s, 0))
    rw_shape = jax.ShapeDtypeStruct((B, S, RWKV_WIDTH), BF16)
    out_shape = [rw_shape] * 6 + [
        jax.ShapeDtypeStruct((B, S, RWKV_WIDTH), F32),
        jax.ShapeDtypeStruct((B, S, DIFF_WIDTH), BF16),
        jax.ShapeDtypeStruct((B, S, DIFF_WIDTH), BF16),
        jax.ShapeDtypeStruct((B, DIFF_HEADS, ns, VT_ROWS, tm), BF16)]
    out_specs = [tok(RWKV_WIDTH)] * 7 + [tok(DIFF_WIDTH)] * 2 + [
        pl.BlockSpec((1, DIFF_HEADS, 1, VT_ROWS, tm), lambda b, s: (b, 0, s, 0, 0))]
    return pl.pallas_call(
        functools.partial(_inproj_kernel, tm=tm),
        grid=(B, ns),
        in_specs=[tok(D_MODEL),
                  pl.BlockSpec((1, 6, D_MODEL), lambda b, s: (b, 0, 0)),
                  full((1, D_MODEL)),
                  full((D_MODEL, RWKV_COLS)), full((D_MODEL, DIFF_COLS)),
                  full((1, RWKV_COLS)), full((1, RWKV_WIDTH)), full((128, RWKV_WIDTH)),
                  full((1, RWKV_WIDTH)), full((128, RWKV_WIDTH)), full((128, RWKV_WIDTH)),
                  full((1, RWKV_WIDTH)), full((1, RWKV_WIDTH)),
                  full((RWKV_WIDTH, RWKV_WIDTH))],
        out_specs=out_specs,
        out_shape=out_shape,
        scratch_shapes=[pltpu.VMEM((1, RWKV_COLS), F32)],
        compiler_params=pltpu.CompilerParams(
            dimension_semantics=("arbitrary", "arbitrary"), vmem_limit_bytes=VMEM_LIMIT),
        name="inproj",
    )(x, mod6, row(norm1_gain), w_rw, w_at, row(mu), row(w0), wup_p, row(a0), aup_p,
      g_up.astype(BF16), row(k_k), row(k_a), _head_sum_matrix(RWKV_WIDTH))


def _rwkv_kernel(r_ref, k_ref, v_ref, a_ref, b_ref, g_ref, lw_ref, rk_ref, gng_ref, gnb_ref,
                 hsum_ref, y_ref, s0_ref, s1_ref, *, nck):
    C = CHUNK
    n_grp = RWKV_HEADS // GROUP
    s_refs = (s0_ref, s1_ref)

    @pl.when(pl.program_id(1) == 0)
    def _():
        s0_ref[...] = jnp.zeros_like(s0_ref)
        s1_ref[...] = jnp.zeros_like(s1_ref)

    ti = lax.broadcasted_iota(I32, (C, GW), 0)
    si = lax.broadcasted_iota(I32, (C, GW), 1) % C
    incl = si <= ti
    strict = si < ti
    eye = (si == ti).astype(F32)
    bdmask = (lax.broadcasted_iota(I32, (GW, GW), 0) // HEAD_DIM ==
              lax.broadcasted_iota(I32, (GW, GW), 1) // HEAD_DIM)
    hsum = hsum_ref[...]
    chains = [(ck, g) for ck in range(nck) for g in range(n_grp)]

    bdmask_bf = bdmask.astype(BF16)

    def bd(xb):
        return jnp.concatenate([xb] * GROUP, axis=0) * bdmask_bf

    tri =(lax.broadcasted_iota(I32, (C, C), 0) >=
           lax.broadcasted_iota(I32, (C, C), 1)).astype(BF16)
    done = {}
    sv = [s_refs[g][...] for g in range(n_grp)]

    def chain(ch):
        ck, g = ch
        rows = slice(ck * C, (ck + 1) * C)
        cols = slice(g * GW, (g + 1) * GW)
        lw = lw_ref[0, rows, cols]
        L = _split_dot_left(tri, lw)
        yield
        Lx = L - lw
        Lc = L[C - 1:C, :]
        rho = L[C // 2 - 1:C // 2, :]
        r = r_ref[0, rows, cols].astype(F32)
        k = k_ref[0, rows, cols].astype(F32)
        vb = v_ref[0, rows, cols]
        a = a_ref[0, rows, cols].astype(F32)
        b = b_ref[0, rows, cols].astype(F32)
        e_k = jnp.exp(rho - L)
        lhs = jnp.concatenate([(r * jnp.exp(L - rho)).astype(BF16),
                               (a * jnp.exp(Lx - rho)).astype(BF16)], axis=0)
        kbd = bd((k * e_k).astype(BF16))
        bbd = bd((b * e_k).astype(BF16))
        r0 = (r * jnp.exp(L)).astype(BF16)
        a0bd = bd((a * jnp.exp(Lx)).astype(BF16))
        e_o = jnp.exp(Lc - L)
        bk = jnp.concatenate([(b * e_o).astype(BF16), (k * e_o).astype(BF16)], axis=0)
        vbd = bd(vb)
        yield
        ak = _dot_nt(lhs, kbd)
        ab = _dot_nt(lhs, bbd)
        yield
        a_rk = jnp.where(incl, ak[:C], 0.0).astype(BF16)
        a_ak = jnp.where(strict, ak[C:], 0.0).astype(BF16)
        a_rb = jnp.where(incl, ab[:C], 0.0).astype(BF16)
        n = jnp.where(strict, ab[C:], 0.0)
        p = eye + n
        xb = n.astype(BF16)
        xbd = bd(xb)
        yield
        x = _dot(xb, xbd)
        yield
        for _ in range(4):
            xb = x.astype(BF16)
            stack = jnp.concatenate([xb, p.astype(BF16)], axis=0)
            xbd = bd(xb)
            yield
            o = _dot(stack, xbd)
            yield
            x = o[:C]
            p = p + o[C:]
        pb = p.astype(BF16)
        xbd = bd(x.astype(BF16))
        yield
        o = _dot(pb, xbd)
        yield
        tb = (p + o).astype(BF16)
        yield
        av = _dot(jnp.concatenate([a_ak, a_rk], axis=0), vbd)
        w1 = _dot(tb, a0bd)
        yield
        akvbd = bd(av[:C].astype(BF16))
        rw_lhs = jnp.concatenate([r0, w1.astype(BF16)], axis=0)
        yield
        w2 = _dot(tb, akvbd)
        yield
        done[ch] = dict(rw_lhs=rw_lhs, w2=w2, a_rb=a_rb, arkv=av[C:], vb=vb, bk=bk,
                        gc=jnp.exp(Lc), rk=r * k)

    def sequential(ck):
        d = [done[(ck, g)] for g in range(n_grp)]
        rws = [_dot_nt(d[g]["rw_lhs"], sv[g].astype(BF16)) for g in range(n_grp)]
        yield
        ub = [(rws[g][C:] + d[g]["w2"]).astype(BF16) for g in range(n_grp)]
        uv = [jnp.concatenate([ub[g], d[g]["vb"]], axis=0) for g in range(n_grp)]
        ubd = [bd(ub[g]) for g in range(n_grp)]
        yield
        upd = [_dot_tn(uv[g], d[g]["bk"]) for g in range(n_grp)]
        yield
        for g in range(n_grp):
            sv[g] = sv[g] * d[g]["gc"] + jnp.where(bdmask, upd[g], 0.0)
        done[("state", ck)] = True
        y = [rws[g][:C] + _dot(d[g]["a_rb"], ubd[g]) + d[g]["arkv"] for g in range(n_grp)]
        yield
        mean = [_split_dot(y[g], hsum) * (1.0 / HEAD_DIM) for g in range(n_grp)]
        yield
        dev = [y[g] - mean[g] for g in range(n_grp)]
        var = [_split_dot(dev[g] * dev[g], hsum) * (1.0 / HEAD_DIM) for g in range(n_grp)]
        bonus = [_split_dot(d[g]["rk"] * rk_ref[:, g * GW:(g + 1) * GW], hsum) for g in range(n_grp)]
        yield
        rows = slice(ck * C, (ck + 1) * C)
        for g in range(n_grp):
            cols = slice(g * GW, (g + 1) * GW)
            yn = dev[g] * lax.rsqrt(var[g] + GN_EPS) * gng_ref[:, cols] + gnb_ref[:, cols]
            out = (yn + bonus[g] * d[g]["vb"].astype(F32)) * g_ref[0, rows, cols].astype(F32)
            y_ref[0, rows, cols] = out.astype(BF16)

    waiting = [chain(ch) for ch in chains]
    seq_next = 0
    running = []
    while waiting or running or seq_next < nck:
        ready = (seq_next < nck and all((seq_next, g) in done for g in range(n_grp))
                 and (seq_next == 0 or ("state", seq_next - 1) in done))
        if ready:
            running.append(sequential(seq_next))
            seq_next += 1
        elif waiting:
            running.append(waiting.pop(0))
        for gen in list(running):
            try:
                next(gen)
            except StopIteration:
                running.remove(gen)
    for g in range(n_grp):
        s_refs[g][...] = sv[g]


def _split_dot_left(w_bf16, x):
    hi = x.astype(BF16)
    lo = (x - hi.astype(F32)).astype(BF16)
    return _dot(w_bf16, hi) + _dot(w_bf16, lo)


def _rwkv_call(r, k, v, a, b, g, lw, r_k, gn_gain, gn_bias, nck):
    B, S, _ = r.shape
    tb = nck * CHUNK
    tok = pl.BlockSpec((1, tb, RWKV_WIDTH), lambda bb, c: (bb, c, 0))
    par = pl.BlockSpec((1, RWKV_WIDTH), lambda bb, c: (0, 0))
    return pl.pallas_call(
        functools.partial(_rwkv_kernel, nck=nck),
        grid=(B, S // tb),
        in_specs=[tok] * 7 + [par] * 3 + [pl.BlockSpec((GW, GW), lambda bb, c: (0, 0))],
        out_specs=tok,
        out_shape=jax.ShapeDtypeStruct((B, S, RWKV_WIDTH), BF16),
        scratch_shapes=[pltpu.VMEM((GW, GW), F32)] * (RWKV_HEADS // GROUP),
        compiler_params=pltpu.CompilerParams(
            dimension_semantics=("arbitrary", "arbitrary"), vmem_limit_bytes=VMEM_LIMIT),
        name="rwkv",
    )(r, k, v, a, b, g, lw, r_k.reshape(1, -1), gn_gain.reshape(1, -1), gn_bias.reshape(1, -1),
      _head_sum_matrix(GW))


def _tree_reduce(op, x):
    while x.shape[0] > 8:
        h = x.shape[0] // 2
        x = op(x[:h], x[h:])
    return x


def _attn_kernel(slopes_ref, q_ref, k_ref, vt_ref, lq1_ref, lk1_ref, lq2_ref, lk2_ref,
                 gain_ref, o_ref, sa_ref, sb_ref, acc_ref, m_ref, *, tq, tk):
    hh = pl.program_id(1)
    i = pl.program_id(2)
    slope = slopes_ref[hh]

    q = q_ref[0].astype(F32)
    lane = lax.broadcasted_iota(I32, (tq, 256), 1)
    qrow = lax.broadcasted_iota(I32, (tq, 256), 0)
    qa = (qrow >> 5).astype(F32) * (32.0 * slope)
    qb = (qrow & 31).astype(F32) * slope
    qbias = jnp.where(lane < 130, 1.0, jnp.where(lane == 130, -qa, jnp.where(lane == 131, -qb, 0.0)))
    qpad = jnp.concatenate([q, jnp.zeros_like(q)], axis=1)
    q_aug = []
    for c in range(2):
        in_comp = (lane >= c * HEAD_DIM) & (lane < (c + 1) * HEAD_DIM)
        q_aug.append(jnp.where(in_comp, qpad, jnp.where(lane >= 128, qbias, 0.0)))
    klane = lax.broadcasted_iota(I32, (tk, 128), 1)
    krow = lax.broadcasted_iota(I32, (tk, 128), 0)
    ka = (krow >> 5).astype(F32) * (32.0 * slope)
    kb_ = (krow & 31).astype(F32) * slope
    kbias = jnp.where(klane == 0, ka, jnp.where(klane == 1, kb_, jnp.where(klane < 4, 1.0, 0.0))
                      ).astype(BF16)

    qt_both = jnp.concatenate([q_aug[0].T, q_aug[1].T], axis=1).astype(BF16)

    m_ref[...] = jnp.full_like(m_ref, -jnp.inf)
    acc_ref[...] = jnp.zeros_like(acc_ref)

    def scores(j):
        kb = k_ref[0, pl.ds(pl.multiple_of(j * tk, tk), tk), :]
        return _dot(jnp.concatenate([kb, kbias], axis=1), qt_both)

    def absorb(s_ref, j, masked):
        s = s_ref[...]
        off = j * tk - i * tq
        cj = slope * off.astype(F32)
        if masked:
            keep = (lax.broadcasted_iota(I32, (tk, tq), 0) -
                    lax.broadcasted_iota(I32, (tk, tq), 1) + off) <= 0
            s = jnp.where(jnp.concatenate([keep, keep], axis=1), s, -jnp.inf)
        m_loc = jnp.max(_tree_reduce(jnp.maximum, s), axis=0, keepdims=True)
        m_old = m_ref[...]
        m_new = jnp.maximum(m_old, m_loc + cj)
        shift = m_new - cj
        vt = vt_ref[0, 0, j]
        rows = tk // KEY_SLABS
        pv = None
        for r in range(KEY_SLABS):
            pr = jnp.exp((s[r * rows:(r + 1) * rows] - shift).astype(BF16))
            part = _dot(vt[:, r * rows:(r + 1) * rows], pr)
            pv = part if pv is None else pv + part
        alpha = jnp.exp(m_old - m_new)
        acc_ref[...] = alpha * acc_ref[...] + pv
        m_ref[...] = m_new

    n_full = (i * tq) // tk
    sa_ref[...] = scores(0)
    bufs = (sa_ref, sb_ref)

    def run_full(j0, count):
        for b in range(count):
            bufs[(b + 1) % 2][...] = scores(j0 + b + 1)
            absorb(bufs[b % 2], j0 + b, False)

    def unrolled_body(jj, carry):
        run_full(jj * LOOP_BLOCKS, LOOP_BLOCKS)
        return carry

    lax.fori_loop(0, n_full // LOOP_BLOCKS, unrolled_body, 0)

    rest = n_full % LOOP_BLOCKS
    for k in range(LOOP_BLOCKS):
        @pl.when(rest == k)
        def _():
            run_full(n_full - k, k)
            absorb(bufs[k % 2], n_full, True)

    lam = (jnp.exp(jnp.sum(lq1_ref[...] * lk1_ref[...], axis=-1, keepdims=True))
           - jnp.exp(jnp.sum(lq2_ref[...] * lk2_ref[...], axis=-1, keepdims=True))
           + LAMBDA_INIT)
    o2 = acc_ref[0:128, :] * (1.0 / acc_ref[128:129, :])
    o = o2[:, :tq] - lam * o2[:, tq:]
    ot = o.T
    ms = jnp.mean(ot * ot, axis=-1, keepdims=True)
    y = ot * lax.rsqrt(ms + NORM_EPS) * gain_ref[...] * (1.0 - LAMBDA_INIT)
    o_ref[0] = y.astype(BF16)


def _attn_call(q, k, vt, lq1, lk1, lq2, lk2, subln_gain, tq):
    B, S, _ = q.shape
    ns, tk = vt.shape[2], vt.shape[4]
    slopes = jnp.asarray([2.0 ** (-8.0 * (i + 1) / DIFF_HEADS) for i in range(DIFF_HEADS)], F32)
    vec = lambda n: pl.BlockSpec((1, n), lambda b, h, i, sl: (0, 0))
    grid_spec = pltpu.PrefetchScalarGridSpec(
        num_scalar_prefetch=1,
        grid=(B, DIFF_HEADS, S // tq),
        in_specs=[pl.BlockSpec((1, tq, 128), lambda b, h, i, sl: (b, i, h)),
                  pl.BlockSpec((1, S, 128), lambda b, h, i, sl: (b, 0, h)),
                  pl.BlockSpec((1, 1, ns, VT_ROWS, tk), lambda b, h, i, sl: (b, h, 0, 0, 0)),
                  vec(HEAD_DIM), vec(HEAD_DIM), vec(HEAD_DIM), vec(HEAD_DIM), vec(128)],
        out_specs=pl.BlockSpec((1, tq, 128), lambda b, h, i, sl: (b, i, h)),
        scratch_shapes=[pltpu.VMEM((tk, 2 * tq), F32), pltpu.VMEM((tk, 2 * tq), F32),
                        pltpu.VMEM((VT_ROWS, 2 * tq), F32), pltpu.VMEM((1, 2 * tq), F32)])
    return pl.pallas_call(
        functools.partial(_attn_kernel, tq=tq, tk=tk),
        grid_spec=grid_spec,
        out_shape=jax.ShapeDtypeStruct((B, S, DIFF_WIDTH), BF16),
        compiler_params=pltpu.CompilerParams(
            dimension_semantics=("arbitrary", "arbitrary", "arbitrary"),
            vmem_limit_bytes=VMEM_LIMIT),
        name="attn",
    )(slopes, q, k, vt, lq1.reshape(1, -1), lk1.reshape(1, -1), lq2.reshape(1, -1),
      lk2.reshape(1, -1), subln_gain.reshape(1, -1))


def _pack_bf16_pairs(x):
    w = x.shape[1] // 2
    lo = pltpu.bitcast(x[:, :w].astype(BF16).astype(F32), U32)
    hi = pltpu.bitcast(x[:, w:].astype(BF16).astype(F32), U32)
    return (lo >> 16) | (hi & jnp.uint32(0xFFFF0000))


def _unpack_bf16_pairs(p):
    lo = pltpu.bitcast(p << 16, F32)
    hi = pltpu.bitcast(p & jnp.uint32(0xFFFF0000), F32)
    return lo, hi


ROW_SUB = (D_MODEL // 2) // 128


def _store_rows(ref, x2d):
    n = x2d.shape[0]
    for s in range(ROW_SUB):
        ref[pl.ds(s, n, stride=ROW_SUB), :] = x2d[:, s * 128:(s + 1) * 128]


def _load_rows(ref):
    n = ref.shape[0] // ROW_SUB
    return jnp.concatenate([ref[pl.ds(s, n, stride=ROW_SUB), :] for s in range(ROW_SUB)], axis=1)


def _outproj_kernel(yr_ref, yd_ref, x_ref, mod_ref, wo_r_ref, wo_d_ref, g2_ref, rw_hi_ref,
                    rw_lo_ref, rb_ref, x1_out, hp_out, idx_out, w_out, rank_out, cnt_out,
                    carry_ref, *, tm):
    t = pl.program_id(0)

    @pl.when(t == 0)
    def _():
        carry_ref[...] = jnp.zeros_like(carry_ref)

    gate1 = mod_ref[0, 2:3, :]
    shift2 = mod_ref[0, 3:4, :]
    scale2 = mod_ref[0, 4:5, :]
    n_part = max(1, tm // 256)
    rows_per = tm // n_part
    eidx = lax.broadcasted_iota(I32, (rows_per, N_EXPERTS), 1)
    col4 = lax.broadcasted_iota(I32, (rows_per, TOP_K), 1)
    tri = (lax.broadcasted_iota(I32, (rows_per, rows_per), 0) >
           lax.broadcasted_iota(I32, (rows_per, rows_per), 1)).astype(BF16)
    before = [carry_ref[...]]

    def part(i):
        rs = slice(i * rows_per, (i + 1) * rows_per)
        mix = _dot(yr_ref[rs, :], wo_r_ref[...]) + _dot(yd_ref[rs, :], wo_d_ref[...])
        yield
        x1 = x_ref[rs, :] + gate1 * mix
        x1_out[rs, :] = x1
        ms = jnp.mean(x1 * x1, axis=-1, keepdims=True)
        h = x1 * lax.rsqrt(ms + NORM_EPS) * g2_ref[...] * (1.0 + scale2) + shift2
        _store_rows(hp_out.at[pl.ds(i * rows_per * ROW_SUB, rows_per * ROW_SUB), :],
                    _pack_bf16_pairs(h))
        hi = h.astype(BF16)
        lo = (h - hi.astype(F32)).astype(BF16)
        yield
        logits = (_dot(hi, rw_hi_ref[...]) + _dot(hi, rw_lo_ref[...]) + _dot(lo, rw_hi_ref[...])
                  + rb_ref[...])
        yield
        lg = logits
        vals, idxs = [], []
        onehot = jnp.zeros(logits.shape, F32)
        for _ in range(TOP_K):
            m = jnp.max(lg, axis=-1, keepdims=True)
            ix = jnp.min(jnp.where(lg == m, eidx, N_EXPERTS), axis=-1, keepdims=True)
            sel = eidx == ix
            vals.append(m)
            idxs.append(ix)
            onehot = onehot + sel.astype(F32)
            lg = jnp.where(sel, -jnp.inf, lg)
        es = [jnp.exp(v - vals[0]) for v in vals]
        den = es[0] + es[1] + es[2] + es[3]
        before.append(before[i] + jnp.sum(onehot, axis=0, keepdims=True))
        yield
        prefix = _dot(tri, onehot.astype(BF16)) + before[i]
        yield
        idx4 = jnp.zeros((rows_per, TOP_K), I32)
        w4 = jnp.zeros((rows_per, TOP_K), F32)
        rank4 = jnp.zeros((rows_per, TOP_K), I32)
        for kk in range(TOP_K):
            rk = jnp.sum(jnp.where(eidx == idxs[kk], prefix, 0.0), axis=-1, keepdims=True)
            idx4 = jnp.where(col4 == kk, idxs[kk], idx4)
            w4 = jnp.where(col4 == kk, es[kk] / den, w4)
            rank4 = jnp.where(col4 == kk, rk.astype(I32), rank4)
        idx_out[rs, :] = idx4
        w_out[rs, :] = w4
        rank_out[rs, :] = rank4

    _round_robin([part(i) for i in range(n_part)])
    carry_ref[...] = before[n_part]
    cnt_out[...] = before[n_part].astype(I32)


def _outproj_call(y_rwkv, y_diff, x, mod6, w_out, norm2_gain, router_w, router_b, S, tm):
    T = x.shape[0]
    tiles_per_seq = S // tm
    rw_hi = router_w.astype(BF16)
    rw_lo = (router_w - rw_hi.astype(F32)).astype(BF16)
    tok = lambda w: pl.BlockSpec((tm, w), lambda t: (t, 0))
    full = lambda shape: pl.BlockSpec(shape, lambda t: (0,) * len(shape))
    return pl.pallas_call(
        functools.partial(_outproj_kernel, tm=tm),
        grid=(T // tm,),
        in_specs=[tok(RWKV_WIDTH), tok(DIFF_WIDTH), tok(D_MODEL),
                  pl.BlockSpec((1, 6, D_MODEL), lambda t: (t // tiles_per_seq, 0, 0)),
                  full((RWKV_WIDTH, D_MODEL)), full((DIFF_WIDTH, D_MODEL)), full((1, D_MODEL)),
                  full((D_MODEL, N_EXPERTS)), full((D_MODEL, N_EXPERTS)), full((1, N_EXPERTS))],
        out_specs=[tok(D_MODEL), pl.BlockSpec((tm * ROW_SUB, 128), lambda t: (t, 0)),
                   tok(TOP_K), tok(TOP_K), tok(TOP_K), full((1, N_EXPERTS))],
        out_shape=[jax.ShapeDtypeStruct((T, D_MODEL), F32),
                   jax.ShapeDtypeStruct((T * ROW_SUB, 128), U32),
                   jax.ShapeDtypeStruct((T, TOP_K), I32),
                   jax.ShapeDtypeStruct((T, TOP_K), F32),
                   jax.ShapeDtypeStruct((T, TOP_K), I32),
                   jax.ShapeDtypeStruct((1, N_EXPERTS), I32)],
        scratch_shapes=[pltpu.VMEM((1, N_EXPERTS), F32)],
        compiler_params=pltpu.CompilerParams(
            dimension_semantics=("arbitrary",), vmem_limit_bytes=VMEM_LIMIT),
        name="outproj",
    )(y_rwkv, y_diff, x, mod6, w_out[:RWKV_WIDTH].astype(BF16), w_out[RWKV_WIDTH:].astype(BF16),
      norm2_gain.reshape(1, -1), rw_hi, rw_lo, router_b.reshape(1, -1))


def _experts_kernel(be_ref, nu_ref, valid_ref, xs_ref, wgu_ref, bgu_ref, wd_ref, bd_ref, ys_ref,
                    wgu_bf, wd_bf):
    i = pl.program_id(0)

    @pl.when((i == 0) | (be_ref[i] != be_ref[jnp.maximum(i - 1, 0)]))
    def _():
        wgu_bf[...] = wgu_ref[0].astype(BF16)
        wd_bf[...] = wd_ref[0].astype(BF16)

    @pl.when(i < nu_ref[0])
    def _():
        packed = _load_rows(xs_ref)
        row = lax.broadcasted_iota(I32, packed.shape, 0)
        packed = jnp.where(row < valid_ref[i], packed, jnp.uint32(0))
        xa, xb = _unpack_bf16_pairs(packed)
        x = jnp.concatenate([xa.astype(BF16), xb.astype(BF16)], axis=1)
        gu = _dot(x, wgu_bf[...]) + bgu_ref[0]
        gate = jnp.minimum(gu[:, :D_EXPERT], SWIGLU_LIMIT)
        up = jnp.clip(gu[:, D_EXPERT:], -SWIGLU_LIMIT, SWIGLU_LIMIT)
        act = (up + 1.0) * (gate * jax.nn.sigmoid(SWIGLU_ALPHA * gate))
        y = _dot(act.astype(BF16), wd_bf[...]) + bd_ref[0]
        _store_rows(ys_ref, _pack_bf16_pairs(y))

    @pl.when(i >= nu_ref[0])
    def _():
        ys_ref[...] = jnp.zeros_like(ys_ref)


def _experts_call(xs, block_e, n_used, valid_rows, w_gate_up, b_gate_up, w_down, b_down, n_blocks):
    per_expert = lambda shape: pl.BlockSpec(shape, lambda i, be, nu, vr: (be[i], 0, 0))
    grid_spec = pltpu.PrefetchScalarGridSpec(
        num_scalar_prefetch=3,
        grid=(n_blocks,),
        in_specs=[pl.BlockSpec((MOE_BLOCK * ROW_SUB, 128),
                               lambda i, be, nu, vr: (jnp.minimum(i, nu[0] - 1), 0)),
                  per_expert((1, D_MODEL, 2 * D_EXPERT)), per_expert((1, 1, 2 * D_EXPERT)),
                  per_expert((1, D_EXPERT, D_MODEL)), per_expert((1, 1, D_MODEL))],
        out_specs=pl.BlockSpec((MOE_BLOCK * ROW_SUB, 128), lambda i, be, nu, vr: (i, 0)),
        scratch_shapes=[pltpu.VMEM((D_MODEL, 2 * D_EXPERT), BF16),
                        pltpu.VMEM((D_EXPERT, D_MODEL), BF16)])
    return pl.pallas_call(
        _experts_kernel,
        grid_spec=grid_spec,
        out_shape=jax.ShapeDtypeStruct((n_blocks * MOE_BLOCK * ROW_SUB, 128), U32),
        compiler_params=pltpu.CompilerParams(
            dimension_semantics=("arbitrary",), vmem_limit_bytes=VMEM_LIMIT),
        name="experts",
    )(block_e, n_used, valid_rows, xs, w_gate_up, b_gate_up.reshape(N_EXPERTS, 1, -1), w_down,
      b_down.reshape(N_EXPERTS, 1, -1))


SC_WINDOW = 128
COMBINE_PARTS = 4


def _sc_mesh():
    return plsc.VectorSubcoreMesh(core_axis_name="c", subcore_axis_name="s")


def _sc_worker_chunks(n_chunks):
    info = plsc.get_sparse_core_info()
    n_workers = info.num_cores * info.num_subcores
    assert n_chunks % n_workers == 0
    return info.num_cores, n_chunks // n_workers


def _sc_scatter_call(rows, slot_chunks, n_out_rows):
    T = rows.shape[0]
    per_pass = T // SC_WINDOW
    n_cores, per_worker = _sc_worker_chunks(per_pass)

    @functools.partial(
        pl.kernel, mesh=_sc_mesh(),
        out_type=jax.ShapeDtypeStruct((n_out_rows, ROW_SUB, 128), U32),
        scratch_types=[pltpu.VMEM((SC_WINDOW,), I32), pltpu.VMEM((SC_WINDOW, ROW_SUB, 128), U32)],
        name="sc_dispatch")
    def run(rows_hbm, idx_hbm, out_hbm, idx_v, rows_v):
        wid = lax.axis_index("s") * n_cores + lax.axis_index("c")

        @pl.loop(0, per_worker)
        def _(j):
            chunk = wid * per_worker + j
            pltpu.sync_copy(rows_hbm.at[pl.ds(chunk * SC_WINDOW, SC_WINDOW)], rows_v)
            for kk in range(TOP_K):
                pltpu.sync_copy(idx_hbm.at[kk * per_pass + chunk], idx_v)
                pltpu.sync_copy(rows_v, out_hbm.at[idx_v])

    return run(rows, slot_chunks)


def _sc_gather_call(table, slot_chunks):
    n_chunks = slot_chunks.shape[0]
    n_cores, per_worker = _sc_worker_chunks(n_chunks)

    @functools.partial(
        pl.kernel, mesh=_sc_mesh(),
        out_type=jax.ShapeDtypeStruct((n_chunks * SC_WINDOW, ROW_SUB, 128), U32),
        scratch_types=[pltpu.VMEM((SC_WINDOW,), I32), pltpu.VMEM((SC_WINDOW, ROW_SUB, 128), U32)],
        name="sc_collect")
    def run(table_hbm, idx_hbm, out_hbm, idx_v, rows_v):
        wid = lax.axis_index("s") * n_cores + lax.axis_index("c")

        @pl.loop(0, per_worker)
        def _(j):
            chunk = wid * per_worker + j
            pltpu.sync_copy(idx_hbm.at[chunk], idx_v)
            pltpu.sync_copy(table_hbm.at[idx_v], rows_v)
            pltpu.sync_copy(rows_v, out_hbm.at[pl.ds(chunk * SC_WINDOW, SC_WINDOW)])

    return run(table, slot_chunks)


def _combine_dense_kernel(g0_ref, g1_ref, g2_ref, g3_ref, x1_ref, w_ref, mod_ref, fg_ref, o_ref,
                          *, tm):
    w = w_ref[...]
    acc_lo = jnp.zeros((tm, D_MODEL // 2), F32)
    acc_hi = jnp.zeros((tm, D_MODEL // 2), F32)
    for kk, g_ref in enumerate((g0_ref, g1_ref, g2_ref, g3_ref)):
        lo, hi = _unpack_bf16_pairs(_load_rows(g_ref))
        wk = w[:, kk:kk + 1]
        acc_lo = acc_lo + wk * lo
        acc_hi = acc_hi + wk * hi
    moe = jnp.concatenate([acc_lo, acc_hi], axis=1)
    gate2 = mod_ref[0, 5:6, :]
    x2 = x1_ref[...] + gate2 * moe
    ms = jnp.mean(x2 * x2, axis=-1, keepdims=True)
    o_ref[...] = x2 * lax.rsqrt(ms + NORM_EPS) * fg_ref[...]


def _combine_dense_call(gathered, x1, top_w, mod6, final_gain, S, tm, part, n_parts):
    T = x1.shape[0]
    assert T % (tm * n_parts) == 0
    nt = T // tm // n_parts
    first = part * nt
    tiles_per_seq = S // tm
    rows = lambda kk: pl.BlockSpec((tm * ROW_SUB, 128), lambda t: (kk * nt + t, 0))
    return pl.pallas_call(
        functools.partial(_combine_dense_kernel, tm=tm),
        grid=(nt,),
        in_specs=[rows(0), rows(1), rows(2), rows(3),
                  pl.BlockSpec((tm, D_MODEL), lambda t: (first + t, 0)),
                  pl.BlockSpec((tm, TOP_K), lambda t: (first + t, 0)),
                  pl.BlockSpec((1, 6, D_MODEL), lambda t: ((first + t) // tiles_per_seq, 0, 0)),
                  pl.BlockSpec((1, D_MODEL), lambda t: (0, 0))],
        out_specs=pl.BlockSpec((tm, D_MODEL), lambda t: (first + t, 0)),
        out_shape=jax.ShapeDtypeStruct((T, D_MODEL), F32),
        input_output_aliases={4: 0},
        compiler_params=pltpu.CompilerParams(
            dimension_semantics=("arbitrary",), vmem_limit_bytes=VMEM_LIMIT),
        name="combine",
    )(gathered, gathered, gathered, gathered, x1, top_w, mod6, final_gain.reshape(1, -1))


def _forward(x, c, mod_w, mod_b, norm1_gain, w_in, rwkv_shift_mu, rwkv_w0, rwkv_w_up, rwkv_a0,
             rwkv_a_up, rwkv_g_up, rwkv_k_k, rwkv_k_a, rwkv_r_k, rwkv_gn_gain, rwkv_gn_bias,
             diff_lambda_q1, diff_lambda_k1, diff_lambda_q2, diff_lambda_k2, diff_subln_gain,
             w_out, norm2_gain, router_w, router_b, w_gate_up, b_gate_up, w_down, b_down,
             final_gain):
    B, S, D = x.shape
    T = B * S
    tm_in = min(512, S)
    tq = min(512, S)
    tm_out = min(1024, S)
    tm = min(256, S)

    mod6 = _mod_call(c, mod_w, mod_b).reshape(B, 6, D)
    (r, k, v, a, b, g, lw, q, kat, vt) = _inproj_call(
        x, mod6, norm1_gain, w_in, rwkv_shift_mu, rwkv_w0, rwkv_w_up, rwkv_a0, rwkv_a_up,
        rwkv_g_up, rwkv_k_k, rwkv_k_a, tm_in)
    y_rwkv = _rwkv_call(r, k, v, a, b, g, lw, rwkv_r_k, rwkv_gn_gain, rwkv_gn_bias,
                        min(16, S // CHUNK))
    y_diff = _attn_call(q, kat, vt, diff_lambda_q1, diff_lambda_k1, diff_lambda_q2,
                        diff_lambda_k2, diff_subln_gain, tq)

    x1, hp, top_idx, top_w, rank, counts = _outproj_call(
        y_rwkv.reshape(T, -1), y_diff.reshape(T, -1), x.reshape(T, D), mod6, w_out, norm2_gain,
        router_w, router_b, S, tm_out)

    counts = counts.reshape(N_EXPERTS)
    padded = ((counts + MOE_BLOCK - 1) // MOE_BLOCK) * MOE_BLOCK
    pad_ends = jnp.cumsum(padded)
    pad_starts = pad_ends - padded
    n_blocks = -(-(T * TOP_K + N_EXPERTS * (MOE_BLOCK - 1)) // MOE_BLOCK)
    n_used = (pad_ends[-1] // MOE_BLOCK).astype(I32).reshape(1)
    block_start = jnp.minimum(jnp.arange(n_blocks, dtype=I32), n_used[0] - 1) * MOE_BLOCK
    block_e = jnp.minimum(jnp.sum(pad_ends[None, :] <= block_start[:, None], axis=1),
                          N_EXPERTS - 1).astype(I32)
    idx_t = top_idx.T
    expert_ids = jnp.arange(N_EXPERTS, dtype=I32)[:, None, None]
    start_of = jnp.sum(jnp.where(idx_t[None] == expert_ids, pad_starts.astype(I32)[:, None, None], 0),
                       axis=0)
    slots = start_of + rank.T
    slot_chunks = slots.reshape(TOP_K * T // SC_WINDOW, SC_WINDOW)
    block_row = jnp.arange(n_blocks, dtype=I32) * MOE_BLOCK
    valid_rows = jnp.clip(counts.astype(I32)[block_e] - (block_row - pad_starts.astype(I32)[block_e]),
                          0, MOE_BLOCK).astype(I32)

    as_tokens = lambda z: z.reshape(-1, ROW_SUB, 128)
    as_lines = lambda z: z.reshape(-1, 128)
    xs = as_lines(_sc_scatter_call(as_tokens(hp), slot_chunks, n_blocks * MOE_BLOCK))
    ys = _experts_call(xs, block_e, n_used, valid_rows, w_gate_up, b_gate_up, w_down, b_down,
                       n_blocks)
    ys_tokens = as_tokens(ys)
    out = x1
    per_part = T // COMBINE_PARTS
    for part in range(COMBINE_PARTS):
        part_slots = slots[:, part * per_part:(part + 1) * per_part]
        part_chunks = part_slots.reshape(TOP_K * per_part // SC_WINDOW, SC_WINDOW)
        gathered = as_lines(_sc_gather_call(ys_tokens, part_chunks))
        out = _combine_dense_call(gathered, out, top_w, mod6, final_gain, S, tm, part, COMBINE_PARTS)
    return out.reshape(B, S, D)


def kernel(x, c, mod_w, mod_b, norm1_gain, w_in, rwkv_shift_mu, rwkv_w0, rwkv_w_up, rwkv_a0, rwkv_a_up, rwkv_g_up, rwkv_k_k, rwkv_k_a, rwkv_r_k, rwkv_gn_gain, rwkv_gn_bias, diff_lambda_q1, diff_lambda_k1, diff_lambda_q2, diff_lambda_k2, diff_subln_gain, w_out, norm2_gain, router_w, router_b, w_gate_up, b_gate_up, w_down, b_down, final_gain):
    return _forward(x, c, mod_w[0], mod_b[0], norm1_gain[0], w_in[0], rwkv_shift_mu[0], rwkv_w0[0],
                    rwkv_w_up[0], rwkv_a0[0], rwkv_a_up[0], rwkv_g_up[0], rwkv_k_k[0], rwkv_k_a[0],
                    rwkv_r_k[0].reshape(-1), rwkv_gn_gain[0], rwkv_gn_bias[0], diff_lambda_q1[0],
                    diff_lambda_k1[0], diff_lambda_q2[0], diff_lambda_k2[0], diff_subln_gain[0],
                    w_out[0], norm2_gain[0], router_w[0], router_b[0], w_gate_up[0], b_gate_up[0],
                    w_down[0], b_down[0], final_gain)
```

```python
import functools
import math

import jax
import jax.numpy as jnp
from jax import lax
from jax.experimental import pallas as pl
from jax.experimental.pallas import tpu as pltpu
from jax.experimental.pallas import tpu_sc as plsc

F32 = jnp.float32
BF16 = jnp.bfloat16
I32 = jnp.int32
U32 = jnp.uint32

D_MODEL = 1024
RWKV_WIDTH = 512
RWKV_HEADS = 8
HEAD_DIM = 64
RWKV_COLS = 3 * RWKV_WIDTH + 64 + 64 + 128
DIFF_WIDTH = 512
DIFF_HEADS = 4
DIFF_COLS = 3 * DIFF_WIDTH
N_EXPERTS = 32
TOP_K = 4
D_EXPERT = 1024
SWIGLU_LIMIT = 7.0
SWIGLU_ALPHA = 1.702
MOE_BLOCK = 512
NORM_EPS = 1e-5
GN_EPS = 64e-5
LAMBDA_INIT = 0.8 - 0.6 * math.exp(-0.3 * 0)

CHUNK = 64
GROUP = 4
GW = GROUP * HEAD_DIM
VT_ROWS = 128 + 16
KEY_SLABS = 2
LOOP_BLOCKS = 8
VMEM_LIMIT = 56 * 1024 * 1024


def _dot(a, b):
    return jnp.dot(a, b, preferred_element_type=F32)


def _dot_nt(a, b):
    return lax.dot_general(a, b, (((1,), (1,)), ((), ())), preferred_element_type=F32)


def _dot_tn(a, b):
    return lax.dot_general(a, b, (((0,), (0,)), ((), ())), preferred_element_type=F32)


def _round_robin(stages):
    waiting = list(stages)
    running = []
    while waiting or running:
        if waiting:
            running.append(waiting.pop(0))
        for gen in list(running):
            try:
                next(gen)
            except StopIteration:
                running.remove(gen)


def _split_dot(x, w_bf16):
    hi = x.astype(BF16)
    lo = (x - hi.astype(F32)).astype(BF16)
    return _dot(hi, w_bf16) + _dot(lo, w_bf16)


def _mod_kernel(c_ref, w_ref, b_ref, o_ref):
    c = c_ref[...]
    s = c * jax.nn.sigmoid(c)
    o_ref[...] = _dot(s, w_ref[...]) + b_ref[...]


def _mod_call(c, mod_w, mod_b):
    B = c.shape[0]
    n = mod_w.shape[1]
    tn = 1536
    return pl.pallas_call(
        _mod_kernel,
        grid=(n // tn,),
        in_specs=[pl.BlockSpec((B, D_MODEL), lambda j: (0, 0)),
                  pl.BlockSpec((D_MODEL, tn), lambda j: (0, j)),
                  pl.BlockSpec((1, tn), lambda j: (0, j))],
        out_specs=pl.BlockSpec((B, tn), lambda j: (0, j)),
        out_shape=jax.ShapeDtypeStruct((B, n), F32),
        compiler_params=pltpu.CompilerParams(
            dimension_semantics=("arbitrary",), vmem_limit_bytes=VMEM_LIMIT),
        name="mod",
    )(c, mod_w, mod_b.reshape(1, n))


def _inproj_kernel(x_ref, mod_ref, g1_ref, wrw_ref, wat_ref, mu_ref, w0_ref, wup_ref,
                   a0_ref, aup_ref, gup_ref, kk_ref, ka_ref, hsum_ref,
                   r_out, k_out, v_out, a_out, b_out, g_out, lw_out,
                   q_out, kat_out, vt_out, carry_ref, *, tm):
    s = pl.program_id(1)
    shift1 = mod_ref[0, 0:1, :]
    scale1 = mod_ref[0, 1:2, :]

    @pl.when(s == 0)
    def _():
        carry_ref[...] = jnp.zeros_like(carry_ref)

    n_part = max(1, tm // 128)
    rows_per = tm // n_part
    last_row = [carry_ref[...]]

    def part(i):
        rs = slice(i * rows_per, (i + 1) * rows_per)
        x = x_ref[0, rs, :]
        ms = jnp.mean(x * x, axis=-1, keepdims=True)
        h = x * lax.rsqrt(ms + NORM_EPS) * g1_ref[...] * (1.0 + scale1) + shift1
        hb = h.astype(BF16)
        yield
        p = _dot(hb, wrw_ref[...])
        last_row.append(p[rows_per - 1:rows_per, :])
        yield
        pa = _dot(hb, wat_ref[...])
        yield
        rolled = pltpu.roll(p, shift=1, axis=0)
        row = lax.broadcasted_iota(I32, p.shape, 0)
        prev = jnp.where(row == 0, last_row[i], rolled)
        ps = p + mu_ref[...] * (prev - p)
        r = ps[:, 0:512]
        k = ps[:, 512:1024]
        v = ps[:, 1024:1536]
        lo2 = ps[:, 1536:1664]
        g_lo = ps[:, 1664:1792]
        kk = k * kk_ref[...]
        yield
        z = w0_ref[...] + _dot(jnp.tanh(lo2).astype(BF16), wup_ref[...])
        a_pre = a0_ref[...] + _dot(lo2.astype(BF16), aup_ref[...])
        g = _dot(jax.nn.sigmoid(g_lo).astype(BF16), gup_ref[...])
        ssq = _split_dot(kk * kk, hsum_ref[...])
        yield
        nz = -z
        softplus = jnp.maximum(nz, 0.0) + jnp.log(1.0 + jnp.exp(-jnp.abs(nz)))
        w = -softplus - 0.5
        lw_out[0, rs, :] = -jnp.exp(w)
        a = jax.nn.sigmoid(a_pre)
        kk = kk / jnp.maximum(jnp.sqrt(ssq), 1e-12)
        k = k * (1.0 + (a - 1.0) * ka_ref[...])
        r_out[0, rs, :] = r.astype(BF16)
        k_out[0, rs, :] = k.astype(BF16)
        v_out[0, rs, :] = v.astype(BF16)
        a_out[0, rs, :] = (-kk).astype(BF16)
        b_out[0, rs, :] = (kk * a).astype(BF16)
        g_out[0, rs, :] = g.astype(BF16)
        yield
        q_out[0, rs, :] = (pa[:, 0:512] * (1.0 / math.sqrt(HEAD_DIM))).astype(BF16)
        kat_out[0, rs, :] = pa[:, 512:1024].astype(BF16)
        for hh in range(DIFF_HEADS):
            vh = pa[:, 1024 + hh * 128:1024 + (hh + 1) * 128]
            vt_out[0, hh, 0, 0:128, rs] = vh.T.astype(BF16)

    _round_robin([part(i) for i in range(n_part)])
    carry_ref[...] = last_row[n_part]
    for hh in range(DIFF_HEADS):
        vt_out[0, hh, 0, 128:VT_ROWS, :] = jnp.ones((VT_ROWS - 128, tm), BF16)


def _head_sum_matrix(width):
    i = jnp.arange(width) // HEAD_DIM
    return (i[:, None] == i[None, :]).astype(BF16)


def _inproj_call(x, mod6, norm1_gain, w_in, mu, w0, w_up, a0, a_up, g_up, k_k, k_a, tm):
    B, S, _ = x.shape
    ns = S // tm
    w_rw = w_in[:, :RWKV_COLS].astype(BF16)
    w_at = w_in[:, RWKV_COLS:].astype(BF16)
    zeros = jnp.zeros((64, RWKV_WIDTH), F32)
    wup_p = jnp.concatenate([w_up, zeros], axis=0).astype(BF16)
    aup_p = jnp.concatenate([zeros, a_up], axis=0).astype(BF16)
    row = lambda v: v.reshape(1, -1)
    full = lambda shape: pl.BlockSpec(shape, lambda b, s: (0,) * len(shape))
    tok = lambda w: pl.BlockSpec((1, tm, w), lambda b, s: (b, s, 0))
    rw_shape = jax.ShapeDtypeStruct((B, S, RWKV_WIDTH), BF16)
    out_shape = [rw_shape] * 6 + [
        jax.ShapeDtypeStruct((B, S, RWKV_WIDTH), F32),
        jax.ShapeDtypeStruct((B, S, DIFF_WIDTH), BF16),
        jax.ShapeDtypeStruct((B, S, DIFF_WIDTH), BF16),
        jax.ShapeDtypeStruct((B, DIFF_HEADS, ns, VT_ROWS, tm), BF16)]
    out_specs = [tok(RWKV_WIDTH)] * 7 + [tok(DIFF_WIDTH)] * 2 + [
        pl.BlockSpec((1, DIFF_HEADS, 1, VT_ROWS, tm), lambda b, s: (b, 0, s, 0, 0))]
    return pl.pallas_call(
        functools.partial(_inproj_kernel, tm=tm),
        grid=(B, ns),
        in_specs=[tok(D_MODEL),
                  pl.BlockSpec((1, 6, D_MODEL), lambda b, s: (b, 0, 0)),
                  full((1, D_MODEL)),
                  full((D_MODEL, RWKV_COLS)), full((D_MODEL, DIFF_COLS)),
                  full((1, RWKV_COLS)), full((1, RWKV_WIDTH)), full((128, RWKV_WIDTH)),
                  full((1, RWKV_WIDTH)), full((128, RWKV_WIDTH)), full((128, RWKV_WIDTH)),
                  full((1, RWKV_WIDTH)), full((1, RWKV_WIDTH)),
                  full((RWKV_WIDTH, RWKV_WIDTH))],
        out_specs=out_specs,
        out_shape=out_shape,
        scratch_shapes=[pltpu.VMEM((1, RWKV_COLS), F32)],
        compiler_params=pltpu.CompilerParams(
            dimension_semantics=("arbitrary", "arbitrary"), vmem_limit_bytes=VMEM_LIMIT),
        name="inproj",
    )(x, mod6, row(norm1_gain), w_rw, w_at, row(mu), row(w0), wup_p, row(a0), aup_p,
      g_up.astype(BF16), row(k_k), row(k_a), _head_sum_matrix(RWKV_WIDTH))


def _rwkv_kernel(r_ref, k_ref, v_ref, a_ref, b_ref, g_ref, lw_ref, rk_ref, gng_ref, gnb_ref,
                 hsum_ref, y_ref, s0_ref, s1_ref, *, nck):
    C = CHUNK
    n_grp = RWKV_HEADS // GROUP
    s_refs = (s0_ref, s1_ref)

    @pl.when(pl.program_id(1) == 0)
    def _():
        s0_ref[...] = jnp.zeros_like(s0_ref)
        s1_ref[...] = jnp.zeros_like(s1_ref)

    ti = lax.broadcasted_iota(I32, (C, GW), 0)
    si = lax.broadcasted_iota(I32, (C, GW), 1) % C
    incl = si <= ti
    strict = si < ti
    eye = (si == ti).astype(F32)
    bdmask = (lax.broadcasted_iota(I32, (GW, GW), 0) // HEAD_DIM ==
              lax.broadcasted_iota(I32, (GW, GW), 1) // HEAD_DIM)
    hsum = hsum_ref[...]
    chains = [(ck, g) for ck in range(nck) for g in range(n_grp)]

    bdmask_bf = bdmask.astype(BF16)

    def bd(xb):
        return jnp.concatenate([xb] * GROUP, axis=0) * bdmask_bf

    tri =(lax.broadcasted_iota(I32, (C, C), 0) >=
           lax.broadcasted_iota(I32, (C, C), 1)).astype(BF16)
    done = {}
    sv = [s_refs[g][...] for g in range(n_grp)]

    def chain(ch):
        ck, g = ch
        rows = slice(ck * C, (ck + 1) * C)
        cols = slice(g * GW, (g + 1) * GW)
        lw = lw_ref[0, rows, cols]
        L = _split_dot_left(tri, lw)
        yield
        Lx = L - lw
        Lc = L[C - 1:C, :]
        rho = L[C // 2 - 1:C // 2, :]
        r = r_ref[0, rows, cols].astype(F32)
        k = k_ref[0, rows, cols].astype(F32)
        vb = v_ref[0, rows, cols]
        a = a_ref[0, rows, cols].astype(F32)
        b = b_ref[0, rows, cols].astype(F32)
        e_k = jnp.exp(rho - L)
        lhs = jnp.concatenate([(r * jnp.exp(L - rho)).astype(BF16),
                               (a * jnp.exp(Lx - rho)).astype(BF16)], axis=0)
        kbd = bd((k * e_k).astype(BF16))
        bbd = bd((b * e_k).astype(BF16))
        r0 = (r * jnp.exp(L)).astype(BF16)
        a0bd = bd((a * jnp.exp(Lx)).astype(BF16))
        e_o = jnp.exp(Lc - L)
        bk = jnp.concatenate([(b * e_o).astype(BF16), (k * e_o).astype(BF16)], axis=0)
        vbd = bd(vb)
        yield
        ak = _dot_nt(lhs, kbd)
        ab = _dot_nt(lhs, bbd)
        yield
        a_rk = jnp.where(incl, ak[:C], 0.0).astype(BF16)
        a_ak = jnp.where(strict, ak[C:], 0.0).astype(BF16)
        a_rb = jnp.where(incl, ab[:C], 0.0).astype(BF16)
        n = jnp.where(strict, ab[C:], 0.0)
        p = eye + n
        xb = n.astype(BF16)
        xbd = bd(xb)
        yield
        x = _dot(xb, xbd)
        yield
        for _ in range(4):
            xb = x.astype(BF16)
            stack = jnp.concatenate([xb, p.astype(BF16)], axis=0)
            xbd = bd(xb)
            yield
            o = _dot(stack, xbd)
            yield
            x = o[:C]
            p = p + o[C:]
        pb = p.astype(BF16)
        xbd = bd(x.astype(BF16))
        yield
        o = _dot(pb, xbd)
        yield
        tb = (p + o).astype(BF16)
        yield
        av = _dot(jnp.concatenate([a_ak, a_rk], axis=0), vbd)
        w1 = _dot(tb, a0bd)
        yield
        akvbd = bd(av[:C].astype(BF16))
        rw_lhs = jnp.concatenate([r0, w1.astype(BF16)], axis=0)
        yield
        w2 = _dot(tb, akvbd)
        yield
        done[ch] = dict(rw_lhs=rw_lhs, w2=w2, a_rb=a_rb, arkv=av[C:], vb=vb, bk=bk,
                        gc=jnp.exp(Lc), rk=r * k)

    def sequential(ck):
        d = [done[(ck, g)] for g in range(n_grp)]
        rws = [_dot_nt(d[g]["rw_lhs"], sv[g].astype(BF16)) for g in range(n_grp)]
        yield
        ub = [(rws[g][C:] + d[g]["w2"]).astype(BF16) for g in range(n_grp)]
        uv = [jnp.concatenate([ub[g], d[g]["vb"]], axis=0) for g in range(n_grp)]
        ubd = [bd(ub[g]) for g in range(n_grp)]
        yield
        upd = [_dot_tn(uv[g], d[g]["bk"]) for g in range(n_grp)]
        yield
        for g in range(n_grp):
            sv[g] = sv[g] * d[g]["gc"] + jnp.where(bdmask, upd[g], 0.0)
        done[("state", ck)] = True
        y = [rws[g][:C] + _dot(d[g]["a_rb"], ubd[g]) + d[g]["arkv"] for g in range(n_grp)]
        yield
        mean = [_split_dot(y[g], hsum) * (1.0 / HEAD_DIM) for g in range(n_grp)]
        yield
        dev = [y[g] - mean[g] for g in range(n_grp)]
        var = [_split_dot(dev[g] * dev[g], hsum) * (1.0 / HEAD_DIM) for g in range(n_grp)]
        bonus = [_split_dot(d[g]["rk"] * rk_ref[:, g * GW:(g + 1) * GW], hsum) for g in range(n_grp)]
        yield
        rows = slice(ck * C, (ck + 1) * C)
        for g in range(n_grp):
            cols = slice(g * GW, (g + 1) * GW)
            yn = dev[g] * lax.rsqrt(var[g] + GN_EPS) * gng_ref[:, cols] + gnb_ref[:, cols]
            out = (yn + bonus[g] * d[g]["vb"].astype(F32)) * g_ref[0, rows, cols].astype(F32)
            y_ref[0, rows, cols] = out.astype(BF16)

    waiting = [chain(ch) for ch in chains]
    seq_next = 0
    running = []
    while waiting or running or seq_next < nck:
        ready = (seq_next < nck and all((seq_next, g) in done for g in range(n_grp))
                 and (seq_next == 0 or ("state", seq_next - 1) in done))
        if ready:
            running.append(sequential(seq_next))
            seq_next += 1
        elif waiting:
            running.append(waiting.pop(0))
        for gen in list(running):
            try:
                next(gen)
            except StopIteration:
                running.remove(gen)
    for g in range(n_grp):
        s_refs[g][...] = sv[g]


def _split_dot_left(w_bf16, x):
    hi = x.astype(BF16)
    lo = (x - hi.astype(F32)).astype(BF16)
    return _dot(w_bf16, hi) + _dot(w_bf16, lo)


def _rwkv_call(r, k, v, a, b, g, lw, r_k, gn_gain, gn_bias, nck):
    B, S, _ = r.shape
    tb = nck * CHUNK
    tok = pl.BlockSpec((1, tb, RWKV_WIDTH), lambda bb, c: (bb, c, 0))
    par = pl.BlockSpec((1, RWKV_WIDTH), lambda bb, c: (0, 0))
    return pl.pallas_call(
        functools.partial(_rwkv_kernel, nck=nck),
        grid=(B, S // tb),
        in_specs=[tok] * 7 + [par] * 3 + [pl.BlockSpec((GW, GW), lambda bb, c: (0, 0))],
        out_specs=tok,
        out_shape=jax.ShapeDtypeStruct((B, S, RWKV_WIDTH), BF16),
        scratch_shapes=[pltpu.VMEM((GW, GW), F32)] * (RWKV_HEADS // GROUP),
        compiler_params=pltpu.CompilerParams(
            dimension_semantics=("arbitrary", "arbitrary"), vmem_limit_bytes=VMEM_LIMIT),
        name="rwkv",
    )(r, k, v, a, b, g, lw, r_k.reshape(1, -1), gn_gain.reshape(1, -1), gn_bias.reshape(1, -1),
      _head_sum_matrix(GW))


def _tree_reduce(op, x):
    while x.shape[0] > 8:
        h = x.shape[0] // 2
        x = op(x[:h], x[h:])
    return x


def _attn_kernel(slopes_ref, q_ref, k_ref, vt_ref, lq1_ref, lk1_ref, lq2_ref, lk2_ref,
                 gain_ref, o_ref, sa_ref, sb_ref, acc_ref, m_ref, *, tq, tk):
    hh = pl.program_id(1)
    i = pl.program_id(2)
    slope = slopes_ref[hh]

    q = q_ref[0].astype(F32)
    lane = lax.broadcasted_iota(I32, (tq, 256), 1)
    qrow = lax.broadcasted_iota(I32, (tq, 256), 0)
    qa = (qrow >> 5).astype(F32) * (32.0 * slope)
    qb = (qrow & 31).astype(F32) * slope
    qbias = jnp.where(lane < 130, 1.0, jnp.where(lane == 130, -qa, jnp.where(lane == 131, -qb, 0.0)))
    qpad = jnp.concatenate([q, jnp.zeros_like(q)], axis=1)
    q_aug = []
    for c in range(2):
        in_comp = (lane >= c * HEAD_DIM) & (lane < (c + 1) * HEAD_DIM)
        q_aug.append(jnp.where(in_comp, qpad, jnp.where(lane >= 128, qbias, 0.0)))
    klane = lax.broadcasted_iota(I32, (tk, 128), 1)
    krow = lax.broadcasted_iota(I32, (tk, 128), 0)
    ka = (krow >> 5).astype(F32) * (32.0 * slope)
    kb_ = (krow & 31).astype(F32) * slope
    kbias = jnp.where(klane == 0, ka, jnp.where(klane == 1, kb_, jnp.where(klane < 4, 1.0, 0.0))
                      ).astype(BF16)

    qt_both = jnp.concatenate([q_aug[0].T, q_aug[1].T], axis=1).astype(BF16)

    m_ref[...] = jnp.full_like(m_ref, -jnp.inf)
    acc_ref[...] = jnp.zeros_like(acc_ref)

    def scores(j):
        kb = k_ref[0, pl.ds(pl.multiple_of(j * tk, tk), tk), :]
        return _dot(jnp.concatenate([kb, kbias], axis=1), qt_both)

    def absorb(s_ref, j, masked):
        s = s_ref[...]
        off = j * tk - i * tq
        cj = slope * off.astype(F32)
        if masked:
            keep = (lax.broadcasted_iota(I32, (tk, tq), 0) -
                    lax.broadcasted_iota(I32, (tk, tq), 1) + off) <= 0
            s = jnp.where(jnp.concatenate([keep, keep], axis=1), s, -jnp.inf)
        m_loc = jnp.max(_tree_reduce(jnp.maximum, s), axis=0, keepdims=True)
        m_old = m_ref[...]
        m_new = jnp.maximum(m_old, m_loc + cj)
        shift = m_new - cj
        vt = vt_ref[0, 0, j]
        rows = tk // KEY_SLABS
        pv = None
        for r in range(KEY_SLABS):
            pr = jnp.exp((s[r * rows:(r + 1) * rows] - shift).astype(BF16))
            part = _dot(vt[:, r * rows:(r + 1) * rows], pr)
            pv = part if pv is None else pv + part
        alpha = jnp.exp(m_old - m_new)
        acc_ref[...] = alpha * acc_ref[...] + pv
        m_ref[...] = m_new

    n_full = (i * tq) // tk
    sa_ref[...] = scores(0)
    bufs = (sa_ref, sb_ref)

    def run_full(j0, count):
        for b in range(count):
            bufs[(b + 1) % 2][...] = scores(j0 + b + 1)
            absorb(bufs[b % 2], j0 + b, False)

    def unrolled_body(jj, carry):
        run_full(jj * LOOP_BLOCKS, LOOP_BLOCKS)
        return carry

    lax.fori_loop(0, n_full // LOOP_BLOCKS, unrolled_body, 0)

    rest = n_full % LOOP_BLOCKS
    for k in range(LOOP_BLOCKS):
        @pl.when(rest == k)
        def _():
            run_full(n_full - k, k)
            absorb(bufs[k % 2], n_full, True)

    lam = (jnp.exp(jnp.sum(lq1_ref[...] * lk1_ref[...], axis=-1, keepdims=True))
           - jnp.exp(jnp.sum(lq2_ref[...] * lk2_ref[...], axis=-1, keepdims=True))
           + LAMBDA_INIT)
    o2 = acc_ref[0:128, :] * (1.0 / acc_ref[128:129, :])
    o = o2[:, :tq] - lam * o2[:, tq:]
    ot = o.T
    ms = jnp.mean(ot * ot, axis=-1, keepdims=True)
    y = ot * lax.rsqrt(ms + NORM_EPS) * gain_ref[...] * (1.0 - LAMBDA_INIT)
    o_ref[0] = y.astype(BF16)


def _attn_call(q, k, vt, lq1, lk1, lq2, lk2, subln_gain, tq):
    B, S, _ = q.shape
    ns, tk = vt.shape[2], vt.shape[4]
    slopes = jnp.asarray([2.0 ** (-8.0 * (i + 1) / DIFF_HEADS) for i in range(DIFF_HEADS)], F32)
    vec = lambda n: pl.BlockSpec((1, n), lambda b, h, i, sl: (0, 0))
    grid_spec = pltpu.PrefetchScalarGridSpec(
        num_scalar_prefetch=1,
        grid=(B, DIFF_HEADS, S // tq),
        in_specs=[pl.BlockSpec((1, tq, 128), lambda b, h, i, sl: (b, i, h)),
                  pl.BlockSpec((1, S, 128), lambda b, h, i, sl: (b, 0, h)),
                  pl.BlockSpec((1, 1, ns, VT_ROWS, tk), lambda b, h, i, sl: (b, h, 0, 0, 0)),
                  vec(HEAD_DIM), vec(HEAD_DIM), vec(HEAD_DIM), vec(HEAD_DIM), vec(128)],
        out_specs=pl.BlockSpec((1, tq, 128), lambda b, h, i, sl: (b, i, h)),
        scratch_shapes=[pltpu.VMEM((tk, 2 * tq), F32), pltpu.VMEM((tk, 2 * tq), F32),
                        pltpu.VMEM((VT_ROWS, 2 * tq), F32), pltpu.VMEM((1, 2 * tq), F32)])
    return pl.pallas_call(
        functools.partial(_attn_kernel, tq=tq, tk=tk),
        grid_spec=grid_spec,
        out_shape=jax.ShapeDtypeStruct((B, S, DIFF_WIDTH), BF16),
        compiler_params=pltpu.CompilerParams(
            dimension_semantics=("arbitrary", "arbitrary", "arbitrary"),
            vmem_limit_bytes=VMEM_LIMIT),
        name="attn",
    )(slopes, q, k, vt, lq1.reshape(1, -1), lk1.reshape(1, -1), lq2.reshape(1, -1),
      lk2.reshape(1, -1), subln_gain.reshape(1, -1))


def _pack_bf16_pairs(x):
    w = x.shape[1] // 2
    lo = pltpu.bitcast(x[:, :w].astype(BF16).astype(F32), U32)
    hi = pltpu.bitcast(x[:, w:].astype(BF16).astype(F32), U32)
    return (lo >> 16) | (hi & jnp.uint32(0xFFFF0000))


def _unpack_bf16_pairs(p):
    lo = pltpu.bitcast(p << 16, F32)
    hi = pltpu.bitcast(p & jnp.uint32(0xFFFF0000), F32)
    return lo, hi


ROW_SUB = (D_MODEL // 2) // 128


def _store_rows(ref, x2d):
    n = x2d.shape[0]
    for s in range(ROW_SUB):
        ref[pl.ds(s, n, stride=ROW_SUB), :] = x2d[:, s * 128:(s + 1) * 128]


def _load_rows(ref):
    n = ref.shape[0] // ROW_SUB
    return jnp.concatenate([ref[pl.ds(s, n, stride=ROW_SUB), :] for s in range(ROW_SUB)], axis=1)


def _outproj_kernel(yr_ref, yd_ref, x_ref, mod_ref, wo_r_ref, wo_d_ref, g2_ref, rw_hi_ref,
                    rw_lo_ref, rb_ref, x1_out, hp_out, idx_out, w_out, rank_out, cnt_out,
                    carry_ref, *, tm):
    t = pl.program_id(0)

    @pl.when(t == 0)
    def _():
        carry_ref[...] = jnp.zeros_like(carry_ref)

    gate1 = mod_ref[0, 2:3, :]
    shift2 = mod_ref[0, 3:4, :]
    scale2 = mod_ref[0, 4:5, :]
    n_part = max(1, tm // 256)
    rows_per = tm // n_part
    eidx = lax.broadcasted_iota(I32, (rows_per, N_EXPERTS), 1)
    col4 = lax.broadcasted_iota(I32, (rows_per, TOP_K), 1)
    tri = (lax.broadcasted_iota(I32, (rows_per, rows_per), 0) >
           lax.broadcasted_iota(I32, (rows_per, rows_per), 1)).astype(BF16)
    before = [carry_ref[...]]

    def part(i):
        rs = slice(i * rows_per, (i + 1) * rows_per)
        mix = _dot(yr_ref[rs, :], wo_r_ref[...]) + _dot(yd_ref[rs, :], wo_d_ref[...])
        yield
        x1 = x_ref[rs, :] + gate1 * mix
        x1_out[rs, :] = x1
        ms = jnp.mean(x1 * x1, axis=-1, keepdims=True)
        h = x1 * lax.rsqrt(ms + NORM_EPS) * g2_ref[...] * (1.0 + scale2) + shift2
        _store_rows(hp_out.at[pl.ds(i * rows_per * ROW_SUB, rows_per * ROW_SUB), :],
                    _pack_bf16_pairs(h))
        hi = h.astype(BF16)
        lo = (h - hi.astype(F32)).astype(BF16)
        yield
        logits = (_dot(hi, rw_hi_ref[...]) + _dot(hi, rw_lo_ref[...]) + _dot(lo, rw_hi_ref[...])
                  + rb_ref[...])
        yield
        lg = logits
        vals, idxs = [], []
        onehot = jnp.zeros(logits.shape, F32)
        for _ in range(TOP_K):
            m = jnp.max(lg, axis=-1, keepdims=True)
            ix = jnp.min(jnp.where(lg == m, eidx, N_EXPERTS), axis=-1, keepdims=True)
            sel = eidx == ix
            vals.append(m)
            idxs.append(ix)
            onehot = onehot + sel.astype(F32)
            lg = jnp.where(sel, -jnp.inf, lg)
        es = [jnp.exp(v - vals[0]) for v in vals]
        den = es[0] + es[1] + es[2] + es[3]
        before.append(before[i] + jnp.sum(onehot, axis=0, keepdims=True))
        yield
        prefix = _dot(tri, onehot.astype(BF16)) + before[i]
        yield
        idx4 = jnp.zeros((rows_per, TOP_K), I32)
        w4 = jnp.zeros((rows_per, TOP_K), F32)
        rank4 = jnp.zeros((rows_per, TOP_K), I32)
        for kk in range(TOP_K):
            rk = jnp.sum(jnp.where(eidx == idxs[kk], prefix, 0.0), axis=-1, keepdims=True)
            idx4 = jnp.where(col4 == kk, idxs[kk], idx4)
            w4 = jnp.where(col4 == kk, es[kk] / den, w4)
            rank4 = jnp.where(col4 == kk, rk.astype(I32), rank4)
        idx_out[rs, :] = idx4
        w_out[rs, :] = w4
        rank_out[rs, :] = rank4

    _round_robin([part(i) for i in range(n_part)])
    carry_ref[...] = before[n_part]
    cnt_out[...] = before[n_part].astype(I32)


def _outproj_call(y_rwkv, y_diff, x, mod6, w_out, norm2_gain, router_w, router_b, S, tm):
    T = x.shape[0]
    tiles_per_seq = S // tm
    rw_hi = router_w.astype(BF16)
    rw_lo = (router_w - rw_hi.astype(F32)).astype(BF16)
    tok = lambda w: pl.BlockSpec((tm, w), lambda t: (t, 0))
    full = lambda shape: pl.BlockSpec(shape, lambda t: (0,) * len(shape))
    return pl.pallas_call(
        functools.partial(_outproj_kernel, tm=tm),
        grid=(T // tm,),
        in_specs=[tok(RWKV_WIDTH), tok(DIFF_WIDTH), tok(D_MODEL),
                  pl.BlockSpec((1, 6, D_MODEL), lambda t: (t // tiles_per_seq, 0, 0)),
                  full((RWKV_WIDTH, D_MODEL)), full((DIFF_WIDTH, D_MODEL)), full((1, D_MODEL)),
                  full((D_MODEL, N_EXPERTS)), full((D_MODEL, N_EXPERTS)), full((1, N_EXPERTS))],
        out_specs=[tok(D_MODEL), pl.BlockSpec((tm * ROW_SUB, 128), lambda t: (t, 0)),
                   tok(TOP_K), tok(TOP_K), tok(TOP_K), full((1, N_EXPERTS))],
        out_shape=[jax.ShapeDtypeStruct((T, D_MODEL), F32),
                   jax.ShapeDtypeStruct((T * ROW_SUB, 128), U32),
                   jax.ShapeDtypeStruct((T, TOP_K), I32),
                   jax.ShapeDtypeStruct((T, TOP_K), F32),
                   jax.ShapeDtypeStruct((T, TOP_K), I32),
                   jax.ShapeDtypeStruct((1, N_EXPERTS), I32)],
        scratch_shapes=[pltpu.VMEM((1, N_EXPERTS), F32)],
        compiler_params=pltpu.CompilerParams(
            dimension_semantics=("arbitrary",), vmem_limit_bytes=VMEM_LIMIT),
        name="outproj",
    )(y_rwkv, y_diff, x, mod6, w_out[:RWKV_WIDTH].astype(BF16), w_out[RWKV_WIDTH:].astype(BF16),
      norm2_gain.reshape(1, -1), rw_hi, rw_lo, router_b.reshape(1, -1))


def _experts_kernel(be_ref, nu_ref, valid_ref, xs_ref, wgu_ref, bgu_ref, wd_ref, bd_ref, ys_ref,
                    wgu_bf, wd_bf):
    i = pl.program_id(0)

    @pl.when((i == 0) | (be_ref[i] != be_ref[jnp.maximum(i - 1, 0)]))
    def _():
        wgu_bf[...] = wgu_ref[0].astype(BF16)
        wd_bf[...] = wd_ref[0].astype(BF16)

    @pl.when(i < nu_ref[0])
    def _():
        packed = _load_rows(xs_ref)
        row = lax.broadcasted_iota(I32, packed.shape, 0)
        packed = jnp.where(row < valid_ref[i], packed, jnp.uint32(0))
        xa, xb = _unpack_bf16_pairs(packed)
        x = jnp.concatenate([xa.astype(BF16), xb.astype(BF16)], axis=1)
        gu = _dot(x, wgu_bf[...]) + bgu_ref[0]
        gate = jnp.minimum(gu[:, :D_EXPERT], SWIGLU_LIMIT)
        up = jnp.clip(gu[:, D_EXPERT:], -SWIGLU_LIMIT, SWIGLU_LIMIT)
        act = (up + 1.0) * (gate * jax.nn.sigmoid(SWIGLU_ALPHA * gate))
        y = _dot(act.astype(BF16), wd_bf[...]) + bd_ref[0]
        _store_rows(ys_ref, _pack_bf16_pairs(y))

    @pl.when(i >= nu_ref[0])
    def _():
        ys_ref[...] = jnp.zeros_like(ys_ref)


def _experts_call(xs, block_e, n_used, valid_rows, w_gate_up, b_gate_up, w_down, b_down, n_blocks):
    per_expert = lambda shape: pl.BlockSpec(shape, lambda i, be, nu, vr: (be[i], 0, 0))
    grid_spec = pltpu.PrefetchScalarGridSpec(
        num_scalar_prefetch=3,
        grid=(n_blocks,),
        in_specs=[pl.BlockSpec((MOE_BLOCK * ROW_SUB, 128),
                               lambda i, be, nu, vr: (jnp.minimum(i, nu[0] - 1), 0)),
                  per_expert((1, D_MODEL, 2 * D_EXPERT)), per_expert((1, 1, 2 * D_EXPERT)),
                  per_expert((1, D_EXPERT, D_MODEL)), per_expert((1, 1, D_MODEL))],
        out_specs=pl.BlockSpec((MOE_BLOCK * ROW_SUB, 128), lambda i, be, nu, vr: (i, 0)),
        scratch_shapes=[pltpu.VMEM((D_MODEL, 2 * D_EXPERT), BF16),
                        pltpu.VMEM((D_EXPERT, D_MODEL), BF16)])
    return pl.pallas_call(
        _experts_kernel,
        grid_spec=grid_spec,
        out_shape=jax.ShapeDtypeStruct((n_blocks * MOE_BLOCK * ROW_SUB, 128), U32),
        compiler_params=pltpu.CompilerParams(
            dimension_semantics=("arbitrary",), vmem_limit_bytes=VMEM_LIMIT),
        name="experts",
    )(block_e, n_used, valid_rows, xs, w_gate_up, b_gate_up.reshape(N_EXPERTS, 1, -1), w_down,
      b_down.reshape(N_EXPERTS, 1, -1))


SC_WINDOW = 128
COMBINE_PARTS = 4


def _sc_mesh():
    return plsc.VectorSubcoreMesh(core_axis_name="c", subcore_axis_name="s")


def _sc_worker_chunks(n_chunks):
    info = plsc.get_sparse_core_info()
    n_workers = info.num_cores * info.num_subcores
    assert n_chunks % n_workers == 0
    return info.num_cores, n_chunks // n_workers


def _sc_scatter_call(rows, slot_chunks, n_out_rows):
    T = rows.shape[0]
    per_pass = T // SC_WINDOW
    n_cores, per_worker = _sc_worker_chunks(per_pass)

    @functools.partial(
        pl.kernel, mesh=_sc_mesh(),
        out_type=jax.ShapeDtypeStruct((n_out_rows, ROW_SUB, 128), U32),
        scratch_types=[pltpu.VMEM((SC_WINDOW,), I32), pltpu.VMEM((SC_WINDOW, ROW_SUB, 128), U32)],
        name="sc_dispatch")
    def run(rows_hbm, idx_hbm, out_hbm, idx_v, rows_v):
        wid = lax.axis_index("s") * n_cores + lax.axis_index("c")

        @pl.loop(0, per_worker)
        def _(j):
            chunk = wid * per_worker + j
            pltpu.sync_copy(rows_hbm.at[pl.ds(chunk * SC_WINDOW, SC_WINDOW)], rows_v)
            for kk in range(TOP_K):
                pltpu.sync_copy(idx_hbm.at[kk * per_pass + chunk], idx_v)
                pltpu.sync_copy(rows_v, out_hbm.at[idx_v])

    return run(rows, slot_chunks)


def _sc_gather_call(table, slot_chunks):
    n_chunks = slot_chunks.shape[0]
    n_cores, per_worker = _sc_worker_chunks(n_chunks)

    @functools.partial(
        pl.kernel, mesh=_sc_mesh(),
        out_type=jax.ShapeDtypeStruct((n_chunks * SC_WINDOW, ROW_SUB, 128), U32),
        scratch_types=[pltpu.VMEM((SC_WINDOW,), I32), pltpu.VMEM((SC_WINDOW, ROW_SUB, 128), U32)],
        name="sc_collect")
    def run(table_hbm, idx_hbm, out_hbm, idx_v, rows_v):
        wid = lax.axis_index("s") * n_cores + lax.axis_index("c")

        @pl.loop(0, per_worker)
        def _(j):
            chunk = wid * per_worker + j
            pltpu.sync_copy(idx_hbm.at[chunk], idx_v)
            pltpu.sync_copy(table_hbm.at[idx_v], rows_v)
            pltpu.sync_copy(rows_v, out_hbm.at[pl.ds(chunk * SC_WINDOW, SC_WINDOW)])

    return run(table, slot_chunks)


def _combine_dense_kernel(g0_ref, g1_ref, g2_ref, g3_ref, x1_ref, w_ref, mod_ref, fg_ref, o_ref,
                          *, tm):
    w = w_ref[...]
    acc_lo = jnp.zeros((tm, D_MODEL // 2), F32)
    acc_hi = jnp.zeros((tm, D_MODEL // 2), F32)
    for kk, g_ref in enumerate((g0_ref, g1_ref, g2_ref, g3_ref)):
        lo, hi = _unpack_bf16_pairs(_load_rows(g_ref))
        wk = w[:, kk:kk + 1]
        acc_lo = acc_lo + wk * lo
        acc_hi = acc_hi + wk * hi
    moe = jnp.concatenate([acc_lo, acc_hi], axis=1)
    gate2 = mod_ref[0, 5:6, :]
    x2 = x1_ref[...] + gate2 * moe
    ms = jnp.mean(x2 * x2, axis=-1, keepdims=True)
    o_ref[...] = x2 * lax.rsqrt(ms + NORM_EPS) * fg_ref[...]


def _combine_dense_call(gathered, x1, top_w, mod6, final_gain, S, tm, part, n_parts):
    T = x1.shape[0]
    assert T % (tm * n_parts) == 0
    nt = T // tm // n_parts
    first = part * nt
    tiles_per_seq = S // tm
    rows = lambda kk: pl.BlockSpec((tm * ROW_SUB, 128), lambda t: (kk * nt + t, 0))
    return pl.pallas_call(
        functools.partial(_combine_dense_kernel, tm=tm),
        grid=(nt,),
        in_specs=[rows(0), rows(1), rows(2), rows(3),
                  pl.BlockSpec((tm, D_MODEL), lambda t: (first + t, 0)),
                  pl.BlockSpec((tm, TOP_K), lambda t: (first + t, 0)),
                  pl.BlockSpec((1, 6, D_MODEL), lambda t: ((first + t) // tiles_per_seq, 0, 0)),
                  pl.BlockSpec((1, D_MODEL), lambda t: (0, 0))],
        out_specs=pl.BlockSpec((tm, D_MODEL), lambda t: (first + t, 0)),
        out_shape=jax.ShapeDtypeStruct((T, D_MODEL), F32),
        input_output_aliases={4: 0},
        compiler_params=pltpu.CompilerParams(
            dimension_semantics=("arbitrary",), vmem_limit_bytes=VMEM_LIMIT),
        name="combine",
    )(gathered, gathered, gathered, gathered, x1, top_w, mod6, final_gain.reshape(1, -1))


def _forward(x, c, mod_w, mod_b, norm1_gain, w_in, rwkv_shift_mu, rwkv_w0, rwkv_w_up, rwkv_a0,
             rwkv_a_up, rwkv_g_up, rwkv_k_k, rwkv_k_a, rwkv_r_k, rwkv_gn_gain, rwkv_gn_bias,
             diff_lambda_q1, diff_lambda_k1, diff_lambda_q2, diff_lambda_k2, diff_subln_gain,
             w_out, norm2_gain, router_w, router_b, w_gate_up, b_gate_up, w_down, b_down,
             final_gain):
    B, S, D = x.shape
    T = B * S
    tm_in = min(512, S)
    tq = min(512, S)
    tm_out = min(1024, S)
    tm = min(256, S)

    mod6 = _mod_call(c, mod_w, mod_b).reshape(B, 6, D)
    (r, k, v, a, b, g, lw, q, kat, vt) = _inproj_call(
        x, mod6, norm1_gain, w_in, rwkv_shift_mu, rwkv_w0, rwkv_w_up, rwkv_a0, rwkv_a_up,
        rwkv_g_up, rwkv_k_k, rwkv_k_a, tm_in)
    y_rwkv = _rwkv_call(r, k, v, a, b, g, lw, rwkv_r_k, rwkv_gn_gain, rwkv_gn_bias,
                        min(16, S // CHUNK))
    y_diff = _attn_call(q, kat, vt, diff_lambda_q1, diff_lambda_k1, diff_lambda_q2,
                        diff_lambda_k2, diff_subln_gain, tq)

    x1, hp, top_idx, top_w, rank, counts = _outproj_call(
        y_rwkv.reshape(T, -1), y_diff.reshape(T, -1), x.reshape(T, D), mod6, w_out, norm2_gain,
        router_w, router_b, S, tm_out)

    counts = counts.reshape(N_EXPERTS)
    padded = ((counts + MOE_BLOCK - 1) // MOE_BLOCK) * MOE_BLOCK
    pad_ends = jnp.cumsum(padded)
    pad_starts = pad_ends - padded
    n_blocks = -(-(T * TOP_K + N_EXPERTS * (MOE_BLOCK - 1)) // MOE_BLOCK)
    n_used = (pad_ends[-1] // MOE_BLOCK).astype(I32).reshape(1)
    block_start = jnp.minimum(jnp.arange(n_blocks, dtype=I32), n_used[0] - 1) * MOE_BLOCK
    block_e = jnp.minimum(jnp.sum(pad_ends[None, :] <= block_start[:, None], axis=1),
                          N_EXPERTS - 1).astype(I32)
    idx_t = top_idx.T
    expert_ids = jnp.arange(N_EXPERTS, dtype=I32)[:, None, None]
    start_of = jnp.sum(jnp.where(idx_t[None] == expert_ids, pad_starts.astype(I32)[:, None, None], 0),
                       axis=0)
    slots = start_of + rank.T
    slot_chunks = slots.reshape(TOP_K * T // SC_WINDOW, SC_WINDOW)
    block_row = jnp.arange(n_blocks, dtype=I32) * MOE_BLOCK
    valid_rows = jnp.clip(counts.astype(I32)[block_e] - (block_row - pad_starts.astype(I32)[block_e]),
                          0, MOE_BLOCK).astype(I32)

    as_tokens = lambda z: z.reshape(-1, ROW_SUB, 128)
    as_lines = lambda z: z.reshape(-1, 128)
    xs = as_lines(_sc_scatter_call(as_tokens(hp), slot_chunks, n_blocks * MOE_BLOCK))
    ys = _experts_call(xs, block_e, n_used, valid_rows, w_gate_up, b_gate_up, w_down, b_down,
                       n_blocks)
    ys_tokens = as_tokens(ys)
    out = x1
    per_part = T // COMBINE_PARTS
    for part in range(COMBINE_PARTS):
        part_slots = slots[:, part * per_part:(part + 1) * per_part]
        part_chunks = part_slots.reshape(TOP_K * per_part // SC_WINDOW, SC_WINDOW)
        gathered = as_lines(_sc_gather_call(ys_tokens, part_chunks))
        out = _combine_dense_call(gathered, out, top_w, mod6, final_gain, S, tm, part, COMBINE_PARTS)
    return out.reshape(B, S, D)


def kernel(x, c, mod_w, mod_b, norm1_gain, w_in, rwkv_shift_mu, rwkv_w0, rwkv_w_up, rwkv_a0, rwkv_a_up, rwkv_g_up, rwkv_k_k, rwkv_k_a, rwkv_r_k, rwkv_gn_gain, rwkv_gn_bias, diff_lambda_q1, diff_lambda_k1, diff_lambda_q2, diff_lambda_k2, diff_subln_gain, w_out, norm2_gain, router_w, router_b, w_gate_up, b_gate_up, w_down, b_down, final_gain):
    return _forward(x, c, mod_w[0], mod_b[0], norm1_gain[0], w_in[0], rwkv_shift_mu[0], rwkv_w0[0],
                    rwkv_w_up[0], rwkv_a0[0], rwkv_a_up[0], rwkv_g_up[0], rwkv_k_k[0], rwkv_k_a[0],
                    rwkv_r_k[0].reshape(-1), rwkv_gn_gain[0], rwkv_gn_bias[0], diff_lambda_q1[0],
                    diff_lambda_k1[0], diff_lambda_q2[0], diff_lambda_k2[0], diff_subln_gain[0],
                    w_out[0], norm2_gain[0], router_w[0], router_b[0], w_gate_up[0], b_gate_up[0],
                    w_down[0], b_down[0], final_gain)
```

```python
import functools
import math

import jax
import jax.numpy as jnp
from jax import lax
from jax.experimental import pallas as pl
from jax.experimental.pallas import tpu as pltpu
from jax.experimental.pallas import tpu_sc as plsc

F32 = jnp.float32
BF16 = jnp.bfloat16
I32 = jnp.int32
U32 = jnp.uint32

D_MODEL = 1024
RWKV_WIDTH = 512
RWKV_HEADS = 8
HEAD_DIM = 64
RWKV_COLS = 3 * RWKV_WIDTH + 64 + 64 + 128
DIFF_WIDTH = 512
DIFF_HEADS = 4
DIFF_COLS = 3 * DIFF_WIDTH
N_EXPERTS = 32
TOP_K = 4
D_EXPERT = 1024
SWIGLU_LIMIT = 7.0
SWIGLU_ALPHA = 1.702
MOE_BLOCK = 512
NORM_EPS = 1e-5
GN_EPS = 64e-5
LAMBDA_INIT = 0.8 - 0.6 * math.exp(-0.3 * 0)

CHUNK = 64
GROUP = 4
GW = GROUP * HEAD_DIM
VT_ROWS = 128 + 16
KEY_SLABS = 2
LOOP_BLOCKS = 4
VMEM_LIMIT = 56 * 1024 * 1024


def _dot(a, b):
    return jnp.dot(a, b, preferred_element_type=F32)


def _dot_nt(a, b):
    return lax.dot_general(a, b, (((1,), (1,)), ((), ())), preferred_element_type=F32)


def _dot_tn(a, b):
    return lax.dot_general(a, b, (((0,), (0,)), ((), ())), preferred_element_type=F32)


def _round_robin(stages):
    waiting = list(stages)
    running = []
    while waiting or running:
        if waiting:
            running.append(waiting.pop(0))
        for gen in list(running):
            try:
                next(gen)
            except StopIteration:
                running.remove(gen)


def _split_dot(x, w_bf16):
    hi = x.astype(BF16)
    lo = (x - hi.astype(F32)).astype(BF16)
    return _dot(hi, w_bf16) + _dot(lo, w_bf16)


def _mod_kernel(c_ref, w_ref, b_ref, o_ref):
    c = c_ref[...]
    s = c * jax.nn.sigmoid(c)
    o_ref[...] = _dot(s, w_ref[...]) + b_ref[...]


def _mod_call(c, mod_w, mod_b):
    B = c.shape[0]
    n = mod_w.shape[1]
    tn = 1536
    return pl.pallas_call(
        _mod_kernel,
        grid=(n // tn,),
        in_specs=[pl.BlockSpec((B, D_MODEL), lambda j: (0, 0)),
                  pl.BlockSpec((D_MODEL, tn), lambda j: (0, j)),
                  pl.BlockSpec((1, tn), lambda j: (0, j))],
        out_specs=pl.BlockSpec((B, tn), lambda j: (0, j)),
        out_shape=jax.ShapeDtypeStruct((B, n), F32),
        compiler_params=pltpu.CompilerParams(
            dimension_semantics=("arbitrary",), vmem_limit_bytes=VMEM_LIMIT),
        name="mod",
    )(c, mod_w, mod_b.reshape(1, n))


def _inproj_kernel(x_ref, mod_ref, g1_ref, wrw_ref, wat_ref, mu_ref, w0_ref, wup_ref,
                   a0_ref, aup_ref, gup_ref, kk_ref, ka_ref, hsum_ref,
                   r_out, k_out, v_out, a_out, b_out, g_out, lw_out,
                   q_out, kat_out, vt_out, carry_ref, *, tm):
    s = pl.program_id(1)
    shift1 = mod_ref[0, 0:1, :]
    scale1 = mod_ref[0, 1:2, :]

    @pl.when(s == 0)
    def _():
        carry_ref[...] = jnp.zeros_like(carry_ref)

    n_part = max(1, tm // 128)
    rows_per = tm // n_part
    last_row = [carry_ref[...]]

    def part(i):
        rs = slice(i * rows_per, (i + 1) * rows_per)
        x = x_ref[0, rs, :]
        ms = jnp.mean(x * x, axis=-1, keepdims=True)
        h = x * lax.rsqrt(ms + NORM_EPS) * g1_ref[...] * (1.0 + scale1) + shift1
        hb = h.astype(BF16)
        yield
        p = _dot(hb, wrw_ref[...])
        last_row.append(p[rows_per - 1:rows_per, :])
        yield
        pa = _dot(hb, wat_ref[...])
        yield
        rolled = pltpu.roll(p, shift=1, axis=0)
        row = lax.broadcasted_iota(I32, p.shape, 0)
        prev = jnp.where(row == 0, last_row[i], rolled)
        ps = p + mu_ref[...] * (prev - p)
        r = ps[:, 0:512]
        k = ps[:, 512:1024]
        v = ps[:, 1024:1536]
        lo2 = ps[:, 1536:1664]
        g_lo = ps[:, 1664:1792]
        kk = k * kk_ref[...]
        yield
        z = w0_ref[...] + _dot(jnp.tanh(lo2).astype(BF16), wup_ref[...])
        a_pre = a0_ref[...] + _dot(lo2.astype(BF16), aup_ref[...])
        g = _dot(jax.nn.sigmoid(g_lo).astype(BF16), gup_ref[...])
        ssq = _split_dot(kk * kk, hsum_ref[...])
        yield
        nz = -z
        softplus = jnp.maximum(nz, 0.0) + jnp.log(1.0 + jnp.exp(-jnp.abs(nz)))
        w = -softplus - 0.5
        lw_out[0, rs, :] = -jnp.exp(w)
        a = jax.nn.sigmoid(a_pre)
        kk = kk / jnp.maximum(jnp.sqrt(ssq), 1e-12)
        k = k * (1.0 + (a - 1.0) * ka_ref[...])
        r_out[0, rs, :] = r.astype(BF16)
        k_out[0, rs, :] = k.astype(BF16)
        v_out[0, rs, :] = v.astype(BF16)
        a_out[0, rs, :] = (-kk).astype(BF16)
        b_out[0, rs, :] = (kk * a).astype(BF16)
        g_out[0, rs, :] = g.astype(BF16)
        yield
        q_out[0, rs, :] = (pa[:, 0:512] * (1.0 / math.sqrt(HEAD_DIM))).astype(BF16)
        kat_out[0, rs, :] = pa[:, 512:1024].astype(BF16)
        for hh in range(DIFF_HEADS):
            vh = pa[:, 1024 + hh * 128:1024 + (hh + 1) * 128]
            vt_out[0, hh, 0, 0:128, rs] = vh.T.astype(BF16)

    _round_robin([part(i) for i in range(n_part)])
    carry_ref[...] = last_row[n_part]
    for hh in range(DIFF_HEADS):
        vt_out[0, hh, 0, 128:VT_ROWS, :] = jnp.ones((VT_ROWS - 128, tm), BF16)


def _head_sum_matrix(width):
    i = jnp.arange(width) // HEAD_DIM
    return (i[:, None] == i[None, :]).astype(BF16)


def _inproj_call(x, mod6, norm1_gain, w_in, mu, w0, w_up, a0, a_up, g_up, k_k, k_a, tm):
    B, S, _ = x.shape
    ns = S // tm
    w_rw = w_in[:, :RWKV_COLS].astype(BF16)
    w_at = w_in[:, RWKV_COLS:].astype(BF16)
    zeros = jnp.zeros((64, RWKV_WIDTH), F32)
    wup_p = jnp.concatenate([w_up, zeros], axis=0).astype(BF16)
    aup_p = jnp.concatenate([zeros, a_up], axis=0).astype(BF16)
    row = lambda v: v.reshape(1, -1)
    full = lambda shape: pl.BlockSpec(shape, lambda b, s: (0,) * len(shape))
    tok = lambda w: pl.BlockSpec((1, tm, w), lambda b, s: (b, s, 0))
    rw_shape = jax.ShapeDtypeStruct((B, S, RWKV_WIDTH), BF16)
    out_shape = [rw_shape] * 6 + [
        jax.ShapeDtypeStruct((B, S, RWKV_WIDTH), F32),
        jax.ShapeDtypeStruct((B, S, DIFF_WIDTH), BF16),
        jax.ShapeDtypeStruct((B, S, DIFF_WIDTH), BF16),
        jax.ShapeDtypeStruct((B, DIFF_HEADS, ns, VT_ROWS, tm), BF16)]
    out_specs = [tok(RWKV_WIDTH)] * 7 + [tok(DIFF_WIDTH)] * 2 + [
        pl.BlockSpec((1, DIFF_HEADS, 1, VT_ROWS, tm), lambda b, s: (b, 0, s, 0, 0))]
    return pl.pallas_call(
        functools.partial(_inproj_kernel, tm=tm),
        grid=(B, ns),
        in_specs=[tok(D_MODEL),
                  pl.BlockSpec((1, 6, D_MODEL), lambda b, s: (b, 0, 0)),
                  full((1, D_MODEL)),
                  full((D_MODEL, RWKV_COLS)), full((D_MODEL, DIFF_COLS)),
                  full((1, RWKV_COLS)), full((1, RWKV_WIDTH)), full((128, RWKV_WIDTH)),
                  full((1, RWKV_WIDTH)), full((128, RWKV_WIDTH)), full((128, RWKV_WIDTH)),
                  full((1, RWKV_WIDTH)), full((1, RWKV_WIDTH)),
                  full((RWKV_WIDTH, RWKV_WIDTH))],
        out_specs=out_specs,
        out_shape=out_shape,
        scratch_shapes=[pltpu.VMEM((1, RWKV_COLS), F32)],
        compiler_params=pltpu.CompilerParams(
            dimension_semantics=("arbitrary", "arbitrary"), vmem_limit_bytes=VMEM_LIMIT),
        name="inproj",
    )(x, mod6, row(norm1_gain), w_rw, w_at, row(mu), row(w0), wup_p, row(a0), aup_p,
      g_up.astype(BF16), row(k_k), row(k_a), _head_sum_matrix(RWKV_WIDTH))


def _rwkv_kernel(r_ref, k_ref, v_ref, a_ref, b_ref, g_ref, lw_ref, rk_ref, gng_ref, gnb_ref,
                 hsum_ref, y_ref, s0_ref, s1_ref, *, nck):
    C = CHUNK
    n_grp = RWKV_HEADS // GROUP
    s_refs = (s0_ref, s1_ref)

    @pl.when(pl.program_id(1) == 0)
    def _():
        s0_ref[...] = jnp.zeros_like(s0_ref)
        s1_ref[...] = jnp.zeros_like(s1_ref)

    ti = lax.broadcasted_iota(I32, (C, GW), 0)
    si = lax.broadcasted_iota(I32, (C, GW), 1) % C
    incl = si <= ti
    strict = si < ti
    eye = (si == ti).astype(F32)
    bdmask = (lax.broadcasted_iota(I32, (GW, GW), 0) // HEAD_DIM ==
              lax.broadcasted_iota(I32, (GW, GW), 1) // HEAD_DIM)
    hsum = hsum_ref[...]
    chains = [(ck, g) for ck in range(nck) for g in range(n_grp)]

    bdmask_bf = bdmask.astype(BF16)

    def bd(xb):
        return jnp.concatenate([xb] * GROUP, axis=0) * bdmask_bf

    tri =(lax.broadcasted_iota(I32, (C, C), 0) >=
           lax.broadcasted_iota(I32, (C, C), 1)).astype(BF16)
    done = {}
    sv = [s_refs[g][...] for g in range(n_grp)]

    def chain(ch):
        ck, g = ch
        rows = slice(ck * C, (ck + 1) * C)
        cols = slice(g * GW, (g + 1) * GW)
        lw = lw_ref[0, rows, cols]
        L = _split_dot_left(tri, lw)
        yield
        Lx = L - lw
        Lc = L[C - 1:C, :]
        rho = L[C // 2 - 1:C // 2, :]
        r = r_ref[0, rows, cols].astype(F32)
        k = k_ref[0, rows, cols].astype(F32)
        vb = v_ref[0, rows, cols]
        a = a_ref[0, rows, cols].astype(F32)
        b = b_ref[0, rows, cols].astype(F32)
        e_k = jnp.exp(rho - L)
        lhs = jnp.concatenate([(r * jnp.exp(L - rho)).astype(BF16),
                               (a * jnp.exp(Lx - rho)).astype(BF16)], axis=0)
        kbd = bd((k * e_k).astype(BF16))
        bbd = bd((b * e_k).astype(BF16))
        r0 = (r * jnp.exp(L)).astype(BF16)
        a0bd = bd((a * jnp.exp(Lx)).astype(BF16))
        e_o = jnp.exp(Lc - L)
        bk = jnp.concatenate([(b * e_o).astype(BF16), (k * e_o).astype(BF16)], axis=0)
        vbd = bd(vb)
        yield
        ak = _dot_nt(lhs, kbd)
        ab = _dot_nt(lhs, bbd)
        yield
        a_rk = jnp.where(incl, ak[:C], 0.0).astype(BF16)
        a_ak = jnp.where(strict, ak[C:], 0.0).astype(BF16)
        a_rb = jnp.where(incl, ab[:C], 0.0).astype(BF16)
        n = jnp.where(strict, ab[C:], 0.0)
        p = eye + n
        xb = n.astype(BF16)
        xbd = bd(xb)
        yield
        x = _dot(xb, xbd)
        yield
        for _ in range(4):
            xb = x.astype(BF16)
            stack = jnp.concatenate([xb, p.astype(BF16)], axis=0)
            xbd = bd(xb)
            yield
            o = _dot(stack, xbd)
            yield
            x = o[:C]
            p = p + o[C:]
        pb = p.astype(BF16)
        xbd = bd(x.astype(BF16))
        yield
        o = _dot(pb, xbd)
        yield
        tb = (p + o).astype(BF16)
        yield
        av = _dot(jnp.concatenate([a_ak, a_rk], axis=0), vbd)
        w1 = _dot(tb, a0bd)
        yield
        akvbd = bd(av[:C].astype(BF16))
        rw_lhs = jnp.concatenate([r0, w1.astype(BF16)], axis=0)
        yield
        w2 = _dot(tb, akvbd)
        yield
        done[ch] = dict(rw_lhs=rw_lhs, w2=w2, a_rb=a_rb, arkv=av[C:], vb=vb, bk=bk,
                        gc=jnp.exp(Lc), rk=r * k)

    def sequential(ck):
        d = [done[(ck, g)] for g in range(n_grp)]
        rws = [_dot_nt(d[g]["rw_lhs"], sv[g].astype(BF16)) for g in range(n_grp)]
        yield
        ub = [(rws[g][C:] + d[g]["w2"]).astype(BF16) for g in range(n_grp)]
        uv = [jnp.concatenate([ub[g], d[g]["vb"]], axis=0) for g in range(n_grp)]
        ubd = [bd(ub[g]) for g in range(n_grp)]
        yield
        upd = [_dot_tn(uv[g], d[g]["bk"]) for g in range(n_grp)]
        yield
        for g in range(n_grp):
            sv[g] = sv[g] * d[g]["gc"] + jnp.where(bdmask, upd[g], 0.0)
        done[("state", ck)] = True
        y = [rws[g][:C] + _dot(d[g]["a_rb"], ubd[g]) + d[g]["arkv"] for g in range(n_grp)]
        yield
        mean = [_split_dot(y[g], hsum) * (1.0 / HEAD_DIM) for g in range(n_grp)]
        yield
        dev = [y[g] - mean[g] for g in range(n_grp)]
        var = [_split_dot(dev[g] * dev[g], hsum) * (1.0 / HEAD_DIM) for g in range(n_grp)]
        bonus = [_split_dot(d[g]["rk"] * rk_ref[:, g * GW:(g + 1) * GW], hsum) for g in range(n_grp)]
        yield
        rows = slice(ck * C, (ck + 1) * C)
        for g in range(n_grp):
            cols = slice(g * GW, (g + 1) * GW)
            yn = dev[g] * lax.rsqrt(var[g] + GN_EPS) * gng_ref[:, cols] + gnb_ref[:, cols]
            out = (yn + bonus[g] * d[g]["vb"].astype(F32)) * g_ref[0, rows, cols].astype(F32)
            y_ref[0, rows, cols] = out.astype(BF16)

    waiting = [chain(ch) for ch in chains]
    seq_next = 0
    running = []
    while waiting or running or seq_next < nck:
        ready = (seq_next < nck and all((seq_next, g) in done for g in range(n_grp))
                 and (seq_next == 0 or ("state", seq_next - 1) in done))
        if ready:
            running.append(sequential(seq_next))
            seq_next += 1
        elif waiting:
            running.append(waiting.pop(0))
        for gen in list(running):
            try:
                next(gen)
            except StopIteration:
                running.remove(gen)
    for g in range(n_grp):
        s_refs[g][...] = sv[g]


def _split_dot_left(w_bf16, x):
    hi = x.astype(BF16)
    lo = (x - hi.astype(F32)).astype(BF16)
    return _dot(w_bf16, hi) + _dot(w_bf16, lo)


def _rwkv_call(r, k, v, a, b, g, lw, r_k, gn_gain, gn_bias, nck):
    B, S, _ = r.shape
    tb = nck * CHUNK
    tok = pl.BlockSpec((1, tb, RWKV_WIDTH), lambda bb, c: (bb, c, 0))
    par = pl.BlockSpec((1, RWKV_WIDTH), lambda bb, c: (0, 0))
    return pl.pallas_call(
        functools.partial(_rwkv_kernel, nck=nck),
        grid=(B, S // tb),
        in_specs=[tok] * 7 + [par] * 3 + [pl.BlockSpec((GW, GW), lambda bb, c: (0, 0))],
        out_specs=tok,
        out_shape=jax.ShapeDtypeStruct((B, S, RWKV_WIDTH), BF16),
        scratch_shapes=[pltpu.VMEM((GW, GW), F32)] * (RWKV_HEADS // GROUP),
        compiler_params=pltpu.CompilerParams(
            dimension_semantics=("arbitrary", "arbitrary"), vmem_limit_bytes=VMEM_LIMIT),
        name="rwkv",
    )(r, k, v, a, b, g, lw, r_k.reshape(1, -1), gn_gain.reshape(1, -1), gn_bias.reshape(1, -1),
      _head_sum_matrix(GW))


def _tree_reduce(op, x):
    while x.shape[0] > 8:
        h = x.shape[0] // 2
        x = op(x[:h], x[h:])
    return x


def _attn_kernel(slopes_ref, q_ref, k_ref, vt_ref, lq1_ref, lk1_ref, lq2_ref, lk2_ref,
                 gain_ref, o_ref, sa_ref, sb_ref, acc_ref, m_ref, *, tq, tk):
    hh = pl.program_id(1)
    i = pl.program_id(2)
    slope = slopes_ref[hh]

    q = q_ref[0].astype(F32)
    lane = lax.broadcasted_iota(I32, (tq, 256), 1)
    qrow = lax.broadcasted_iota(I32, (tq, 256), 0)
    qa = (qrow >> 5).astype(F32) * (32.0 * slope)
    qb = (qrow & 31).astype(F32) * slope
    qbias = jnp.where(lane < 130, 1.0, jnp.where(lane == 130, -qa, jnp.where(lane == 131, -qb, 0.0)))
    qpad = jnp.concatenate([q, jnp.zeros_like(q)], axis=1)
    q_aug = []
    for c in range(2):
        in_comp = (lane >= c * HEAD_DIM) & (lane < (c + 1) * HEAD_DIM)
        q_aug.append(jnp.where(in_comp, qpad, jnp.where(lane >= 128, qbias, 0.0)))
    klane = lax.broadcasted_iota(I32, (tk, 128), 1)
    krow = lax.broadcasted_iota(I32, (tk, 128), 0)
    ka = (krow >> 5).astype(F32) * (32.0 * slope)
    kb_ = (krow & 31).astype(F32) * slope
    kbias = jnp.where(klane == 0, ka, jnp.where(klane == 1, kb_, jnp.where(klane < 4, 1.0, 0.0))
                      ).astype(BF16)

    qt_both = jnp.concatenate([q_aug[0].T, q_aug[1].T], axis=1).astype(BF16)

    m_ref[...] = jnp.full_like(m_ref, -jnp.inf)
    acc_ref[...] = jnp.zeros_like(acc_ref)

    def scores(j):
        kb = k_ref[0, pl.ds(pl.multiple_of(j * tk, tk), tk), :]
        return _dot(jnp.concatenate([kb, kbias], axis=1), qt_both)

    def absorb(s_ref, j, masked):
        s = s_ref[...]
        off = j * tk - i * tq
        cj = slope * off.astype(F32)
        if masked:
            keep = (lax.broadcasted_iota(I32, (tk, tq), 0) -
                    lax.broadcasted_iota(I32, (tk, tq), 1) + off) <= 0
            s = jnp.where(jnp.concatenate([keep, keep], axis=1), s, -jnp.inf)
        m_loc = jnp.max(_tree_reduce(jnp.maximum, s), axis=0, keepdims=True)
        m_old = m_ref[...]
        m_new = jnp.maximum(m_old, m_loc + cj)
        shift = m_new - cj
        vt = vt_ref[0, 0, j]
        rows = tk // KEY_SLABS
        pv = None
        for r in range(KEY_SLABS):
            pr = jnp.exp((s[r * rows:(r + 1) * rows] - shift).astype(BF16))
            part = _dot(vt[:, r * rows:(r + 1) * rows], pr)
            pv = part if pv is None else pv + part
        alpha = jnp.exp(m_old - m_new)
        acc_ref[...] = alpha * acc_ref[...] + pv
        m_ref[...] = m_new

    n_full = (i * tq) // tk
    sa_ref[...] = scores(0)
    bufs = (sa_ref, sb_ref)

    def run_full(j0, count):
        for b in range(count):
            bufs[(b + 1) % 2][...] = scores(j0 + b + 1)
            absorb(bufs[b % 2], j0 + b, False)

    def unrolled_body(jj, carry):
        run_full(jj * LOOP_BLOCKS, LOOP_BLOCKS)
        return carry

    lax.fori_loop(0, n_full // LOOP_BLOCKS, unrolled_body, 0)

    rest = n_full % LOOP_BLOCKS
    for k in range(LOOP_BLOCKS):
        @pl.when(rest == k)
        def _():
            run_full(n_full - k, k)
            absorb(bufs[k % 2], n_full, True)

    lam = (jnp.exp(jnp.sum(lq1_ref[...] * lk1_ref[...], axis=-1, keepdims=True))
           - jnp.exp(jnp.sum(lq2_ref[...] * lk2_ref[...], axis=-1, keepdims=True))
           + LAMBDA_INIT)
    o2 = acc_ref[0:128, :] * (1.0 / acc_ref[128:129, :])
    o = o2[:, :tq] - lam * o2[:, tq:]
    ot = o.T
    ms = jnp.mean(ot * ot, axis=-1, keepdims=True)
    y = ot * lax.rsqrt(ms + NORM_EPS) * gain_ref[...] * (1.0 - LAMBDA_INIT)
    o_ref[0] = y.astype(BF16)


def _attn_call(q, k, vt, lq1, lk1, lq2, lk2, subln_gain, tq):
    B, S, _ = q.shape
    ns, tk = vt.shape[2], vt.shape[4]
    slopes = jnp.asarray([2.0 ** (-8.0 * (i + 1) / DIFF_HEADS) for i in range(DIFF_HEADS)], F32)
    vec = lambda n: pl.BlockSpec((1, n), lambda b, h, i, sl: (0, 0))
    grid_spec = pltpu.PrefetchScalarGridSpec(
        num_scalar_prefetch=1,
        grid=(B, DIFF_HEADS, S // tq),
        in_specs=[pl.BlockSpec((1, tq, 128), lambda b, h, i, sl: (b, i, h)),
                  pl.BlockSpec((1, S, 128), lambda b, h, i, sl: (b, 0, h)),
                  pl.BlockSpec((1, 1, ns, VT_ROWS, tk), lambda b, h, i, sl: (b, h, 0, 0, 0)),
                  vec(HEAD_DIM), vec(HEAD_DIM), vec(HEAD_DIM), vec(HEAD_DIM), vec(128)],
        out_specs=pl.BlockSpec((1, tq, 128), lambda b, h, i, sl: (b, i, h)),
        scratch_shapes=[pltpu.VMEM((tk, 2 * tq), F32), pltpu.VMEM((tk, 2 * tq), F32),
                        pltpu.VMEM((VT_ROWS, 2 * tq), F32), pltpu.VMEM((1, 2 * tq), F32)])
    return pl.pallas_call(
        functools.partial(_attn_kernel, tq=tq, tk=tk),
        grid_spec=grid_spec,
        out_shape=jax.ShapeDtypeStruct((B, S, DIFF_WIDTH), BF16),
        compiler_params=pltpu.CompilerParams(
            dimension_semantics=("arbitrary", "arbitrary", "arbitrary"),
            vmem_limit_bytes=VMEM_LIMIT),
        name="attn",
    )(slopes, q, k, vt, lq1.reshape(1, -1), lk1.reshape(1, -1), lq2.reshape(1, -1),
      lk2.reshape(1, -1), subln_gain.reshape(1, -1))


def _pack_bf16_pairs(x):
    w = x.shape[1] // 2
    lo = pltpu.bitcast(x[:, :w].astype(BF16).astype(F32), U32)
    hi = pltpu.bitcast(x[:, w:].astype(BF16).astype(F32), U32)
    return (lo >> 16) | (hi & jnp.uint32(0xFFFF0000))


def _unpack_bf16_pairs(p):
    lo = pltpu.bitcast(p << 16, F32)
    hi = pltpu.bitcast(p & jnp.uint32(0xFFFF0000), F32)
    return lo, hi


ROW_SUB = (D_MODEL // 2) // 128


def _store_rows(ref, x2d):
    n = x2d.shape[0]
    for s in range(ROW_SUB):
        ref[pl.ds(s, n, stride=ROW_SUB), :] = x2d[:, s * 128:(s + 1) * 128]


def _load_rows(ref):
    n = ref.shape[0] // ROW_SUB
    return jnp.concatenate([ref[pl.ds(s, n, stride=ROW_SUB), :] for s in range(ROW_SUB)], axis=1)


def _outproj_kernel(yr_ref, yd_ref, x_ref, mod_ref, wo_r_ref, wo_d_ref, g2_ref, rw_hi_ref,
                    rw_lo_ref, rb_ref, x1_out, hp_out, route_out, w_out, cnt_out,
                    carry_ref, *, tm):
    t = pl.program_id(0)

    @pl.when(t == 0)
    def _():
        carry_ref[...] = jnp.zeros_like(carry_ref)

    gate1 = mod_ref[0, 2:3, :]
    shift2 = mod_ref[0, 3:4, :]
    scale2 = mod_ref[0, 4:5, :]
    n_part = max(1, tm // 256)
    rows_per = tm // n_part
    eidx = lax.broadcasted_iota(I32, (rows_per, N_EXPERTS), 1)
    col4 = lax.broadcasted_iota(I32, (rows_per, TOP_K), 1)
    lane128 = lax.broadcasted_iota(I32, (rows_per, 128), 1)
    tri = (lax.broadcasted_iota(I32, (rows_per, rows_per), 0) >
           lax.broadcasted_iota(I32, (rows_per, rows_per), 1)).astype(BF16)
    before = [carry_ref[...]]

    def part(i):
        rs = slice(i * rows_per, (i + 1) * rows_per)
        mix = _dot(yr_ref[rs, :], wo_r_ref[...]) + _dot(yd_ref[rs, :], wo_d_ref[...])
        yield
        x1 = x_ref[rs, :] + gate1 * mix
        x1_out[rs, :] = x1
        ms = jnp.mean(x1 * x1, axis=-1, keepdims=True)
        h = x1 * lax.rsqrt(ms + NORM_EPS) * g2_ref[...] * (1.0 + scale2) + shift2
        _store_rows(hp_out.at[pl.ds(i * rows_per * ROW_SUB, rows_per * ROW_SUB), :],
                    _pack_bf16_pairs(h))
        hi = h.astype(BF16)
        lo = (h - hi.astype(F32)).astype(BF16)
        yield
        logits = (_dot(hi, rw_hi_ref[...]) + _dot(hi, rw_lo_ref[...]) + _dot(lo, rw_hi_ref[...])
                  + rb_ref[...])
        yield
        lg = logits
        vals, idxs = [], []
        onehot = jnp.zeros(logits.shape, F32)
        for _ in range(TOP_K):
            m = jnp.max(lg, axis=-1, keepdims=True)
            ix = jnp.min(jnp.where(lg == m, eidx, N_EXPERTS), axis=-1, keepdims=True)
            sel = eidx == ix
            vals.append(m)
            idxs.append(ix)
            onehot = onehot + sel.astype(F32)
            lg = jnp.where(sel, -jnp.inf, lg)
        es = [jnp.exp(v - vals[0]) for v in vals]
        den = es[0] + es[1] + es[2] + es[3]
        before.append(before[i] + jnp.sum(onehot, axis=0, keepdims=True))
        yield
        prefix = _dot(tri, onehot.astype(BF16)) + before[i]
        yield
        w4 = jnp.zeros((rows_per, TOP_K), F32)
        table = jnp.zeros((rows_per, 128), F32)
        for kk in range(TOP_K):
            rk = jnp.sum(jnp.where(eidx == idxs[kk], prefix, 0.0), axis=-1, keepdims=True)
            w4 = jnp.where(col4 == kk, es[kk] / den, w4)
            table = jnp.where(lane128 == kk, idxs[kk].astype(F32), table)
            table = jnp.where(lane128 == TOP_K + kk, rk, table)
        w_out[rs, :] = w4
        route_out[:, rs] = table.T[0:2 * TOP_K, :].astype(I32)

    _round_robin([part(i) for i in range(n_part)])
    carry_ref[...] = before[n_part]
    cnt_out[...] = before[n_part].astype(I32)


def _outproj_call(y_rwkv, y_diff, x, mod6, w_out, norm2_gain, router_w, router_b, S, tm):
    T = x.shape[0]
    tiles_per_seq = S // tm
    rw_hi = router_w.astype(BF16)
    rw_lo = (router_w - rw_hi.astype(F32)).astype(BF16)
    tok = lambda w: pl.BlockSpec((tm, w), lambda t: (t, 0))
    full = lambda shape: pl.BlockSpec(shape, lambda t: (0,) * len(shape))
    return pl.pallas_call(
        functools.partial(_outproj_kernel, tm=tm),
        grid=(T // tm,),
        in_specs=[tok(RWKV_WIDTH), tok(DIFF_WIDTH), tok(D_MODEL),
                  pl.BlockSpec((1, 6, D_MODEL), lambda t: (t // tiles_per_seq, 0, 0)),
                  full((RWKV_WIDTH, D_MODEL)), full((DIFF_WIDTH, D_MODEL)), full((1, D_MODEL)),
                  full((D_MODEL, N_EXPERTS)), full((D_MODEL, N_EXPERTS)), full((1, N_EXPERTS))],
        out_specs=[tok(D_MODEL), pl.BlockSpec((tm * ROW_SUB, 128), lambda t: (t, 0)),
                   pl.BlockSpec((2 * TOP_K, tm), lambda t: (0, t)), tok(TOP_K),
                   full((1, N_EXPERTS))],
        out_shape=[jax.ShapeDtypeStruct((T, D_MODEL), F32),
                   jax.ShapeDtypeStruct((T * ROW_SUB, 128), U32),
                   jax.ShapeDtypeStruct((2 * TOP_K, T), I32),
                   jax.ShapeDtypeStruct((T, TOP_K), F32),
                   jax.ShapeDtypeStruct((1, N_EXPERTS), I32)],
        scratch_shapes=[pltpu.VMEM((1, N_EXPERTS), F32)],
        compiler_params=pltpu.CompilerParams(
            dimension_semantics=("arbitrary",), vmem_limit_bytes=VMEM_LIMIT),
        name="outproj",
    )(y_rwkv, y_diff, x, mod6, w_out[:RWKV_WIDTH].astype(BF16), w_out[RWKV_WIDTH:].astype(BF16),
      norm2_gain.reshape(1, -1), rw_hi, rw_lo, router_b.reshape(1, -1))


def _experts_kernel(be_ref, nu_ref, valid_ref, xs_ref, wgu_ref, bgu_ref, wd_ref, bd_ref, ys_ref,
                    wgu_bf, wd_bf):
    i = pl.program_id(0)

    @pl.when((i == 0) | (be_ref[i] != be_ref[jnp.maximum(i - 1, 0)]))
    def _():
        wgu_bf[...] = wgu_ref[0].astype(BF16)
        wd_bf[...] = wd_ref[0].astype(BF16)

    @pl.when(i < nu_ref[0])
    def _():
        packed = _load_rows(xs_ref)
        row = lax.broadcasted_iota(I32, packed.shape, 0)
        packed = jnp.where(row < valid_ref[i], packed, jnp.uint32(0))
        xa, xb = _unpack_bf16_pairs(packed)
        x = jnp.concatenate([xa.astype(BF16), xb.astype(BF16)], axis=1)
        gu = _dot(x, wgu_bf[...]) + bgu_ref[0]
        gate = jnp.minimum(gu[:, :D_EXPERT], SWIGLU_LIMIT)
        up = jnp.clip(gu[:, D_EXPERT:], -SWIGLU_LIMIT, SWIGLU_LIMIT)
        act = (up + 1.0) * (gate * jax.nn.sigmoid(SWIGLU_ALPHA * gate))
        y = _dot(act.astype(BF16), wd_bf[...]) + bd_ref[0]
        _store_rows(ys_ref, _pack_bf16_pairs(y))

    @pl.when(i >= nu_ref[0])
    def _():
        ys_ref[...] = jnp.zeros_like(ys_ref)


def _experts_call(xs, block_e, n_used, valid_rows, w_gate_up, b_gate_up, w_down, b_down, n_blocks):
    per_expert = lambda shape: pl.BlockSpec(shape, lambda i, be, nu, vr: (be[i], 0, 0))
    grid_spec = pltpu.PrefetchScalarGridSpec(
        num_scalar_prefetch=3,
        grid=(n_blocks,),
        in_specs=[pl.BlockSpec((MOE_BLOCK * ROW_SUB, 128),
                               lambda i, be, nu, vr: (jnp.minimum(i, nu[0] - 1), 0)),
                  per_expert((1, D_MODEL, 2 * D_EXPERT)), per_expert((1, 1, 2 * D_EXPERT)),
                  per_expert((1, D_EXPERT, D_MODEL)), per_expert((1, 1, D_MODEL))],
        out_specs=pl.BlockSpec((MOE_BLOCK * ROW_SUB, 128), lambda i, be, nu, vr: (i, 0)),
        scratch_shapes=[pltpu.VMEM((D_MODEL, 2 * D_EXPERT), BF16),
                        pltpu.VMEM((D_EXPERT, D_MODEL), BF16)])
    return pl.pallas_call(
        _experts_kernel,
        grid_spec=grid_spec,
        out_shape=jax.ShapeDtypeStruct((n_blocks * MOE_BLOCK * ROW_SUB, 128), U32),
        compiler_params=pltpu.CompilerParams(
            dimension_semantics=("arbitrary",), vmem_limit_bytes=VMEM_LIMIT),
        name="experts",
    )(block_e, n_used, valid_rows, xs, w_gate_up, b_gate_up.reshape(N_EXPERTS, 1, -1), w_down,
      b_down.reshape(N_EXPERTS, 1, -1))


SC_WINDOW = 128
COMBINE_PARTS = 4


def _sc_mesh():
    return plsc.VectorSubcoreMesh(core_axis_name="c", subcore_axis_name="s")


def _sc_worker_chunks(n_chunks):
    info = plsc.get_sparse_core_info()
    n_workers = info.num_cores * info.num_subcores
    assert n_chunks % n_workers == 0
    return info.num_cores, n_chunks // n_workers


def _sc_scatter_call(rows, slot_chunks, n_out_rows):
    T = rows.shape[0]
    per_pass = T // SC_WINDOW
    n_cores, per_worker = _sc_worker_chunks(per_pass)

    @functools.partial(
        pl.kernel, mesh=_sc_mesh(),
        out_type=jax.ShapeDtypeStruct((n_out_rows, ROW_SUB, 128), U32),
        scratch_types=[pltpu.VMEM((SC_WINDOW,), I32), pltpu.VMEM((SC_WINDOW, ROW_SUB, 128), U32)],
        name="sc_dispatch")
    def run(rows_hbm, idx_hbm, out_hbm, idx_v, rows_v):
        wid = lax.axis_index("s") * n_cores + lax.axis_index("c")

        @pl.loop(0, per_worker)
        def _(j):
            chunk = wid * per_worker + j
            pltpu.sync_copy(rows_hbm.at[pl.ds(chunk * SC_WINDOW, SC_WINDOW)], rows_v)
            for kk in range(TOP_K):
                pltpu.sync_copy(idx_hbm.at[kk * per_pass + chunk], idx_v)
                pltpu.sync_copy(rows_v, out_hbm.at[idx_v])

    return run(rows, slot_chunks)


def _sc_gather_call(table, slot_chunks):
    n_chunks = slot_chunks.shape[0]
    n_cores, per_worker = _sc_worker_chunks(n_chunks)

    @functools.partial(
        pl.kernel, mesh=_sc_mesh(),
        out_type=jax.ShapeDtypeStruct((n_chunks * SC_WINDOW, ROW_SUB, 128), U32),
        scratch_types=[pltpu.VMEM((SC_WINDOW,), I32), pltpu.VMEM((SC_WINDOW, ROW_SUB, 128), U32)],
        name="sc_collect")
    def run(table_hbm, idx_hbm, out_hbm, idx_v, rows_v):
        wid = lax.axis_index("s") * n_cores + lax.axis_index("c")

        @pl.loop(0, per_worker)
        def _(j):
            chunk = wid * per_worker + j
            pltpu.sync_copy(idx_hbm.at[chunk], idx_v)
            pltpu.sync_copy(table_hbm.at[idx_v], rows_v)
            pltpu.sync_copy(rows_v, out_hbm.at[pl.ds(chunk * SC_WINDOW, SC_WINDOW)])

    return run(table, slot_chunks)


def _combine_dense_kernel(g0_ref, g1_ref, g2_ref, g3_ref, x1_ref, w_ref, mod_ref, fg_ref, o_ref,
                          *, tm):
    w = w_ref[...]
    acc_lo = jnp.zeros((tm, D_MODEL // 2), F32)
    acc_hi = jnp.zeros((tm, D_MODEL // 2), F32)
    for kk, g_ref in enumerate((g0_ref, g1_ref, g2_ref, g3_ref)):
        lo, hi = _unpack_bf16_pairs(_load_rows(g_ref))
        wk = w[:, kk:kk + 1]
        acc_lo = acc_lo + wk * lo
        acc_hi = acc_hi + wk * hi
    moe = jnp.concatenate([acc_lo, acc_hi], axis=1)
    gate2 = mod_ref[0, 5:6, :]
    x2 = x1_ref[...] + gate2 * moe
    ms = jnp.mean(x2 * x2, axis=-1, keepdims=True)
    o_ref[...] = x2 * lax.rsqrt(ms + NORM_EPS) * fg_ref[...]


def _combine_dense_call(gathered, x1, top_w, mod6, final_gain, S, tm, part, n_parts):
    T = x1.shape[0]
    assert T % (tm * n_parts) == 0
    nt = T // tm // n_parts
    first = part * nt
    tiles_per_seq = S // tm
    rows = lambda kk: pl.BlockSpec((tm * ROW_SUB, 128), lambda t: (kk * nt + t, 0))
    return pl.pallas_call(
        functools.partial(_combine_dense_kernel, tm=tm),
        grid=(nt,),
        in_specs=[rows(0), rows(1), rows(2), rows(3),
                  pl.BlockSpec((tm, D_MODEL), lambda t: (first + t, 0)),
                  pl.BlockSpec((tm, TOP_K), lambda t: (first + t, 0)),
                  pl.BlockSpec((1, 6, D_MODEL), lambda t: ((first + t) // tiles_per_seq, 0, 0)),
                  pl.BlockSpec((1, D_MODEL), lambda t: (0, 0))],
        out_specs=pl.BlockSpec((tm, D_MODEL), lambda t: (first + t, 0)),
        out_shape=jax.ShapeDtypeStruct((T, D_MODEL), F32),
        input_output_aliases={4: 0},
        compiler_params=pltpu.CompilerParams(
            dimension_semantics=("arbitrary",), vmem_limit_bytes=VMEM_LIMIT),
        name="combine",
    )(gathered, gathered, gathered, gathered, x1, top_w, mod6, final_gain.reshape(1, -1))


def _forward(x, c, mod_w, mod_b, norm1_gain, w_in, rwkv_shift_mu, rwkv_w0, rwkv_w_up, rwkv_a0,
             rwkv_a_up, rwkv_g_up, rwkv_k_k, rwkv_k_a, rwkv_r_k, rwkv_gn_gain, rwkv_gn_bias,
             diff_lambda_q1, diff_lambda_k1, diff_lambda_q2, diff_lambda_k2, diff_subln_gain,
             w_out, norm2_gain, router_w, router_b, w_gate_up, b_gate_up, w_down, b_down,
             final_gain):
    B, S, D = x.shape
    T = B * S
    tm_in = min(512, S)
    tq = min(512, S)
    tm_out = min(1024, S)
    tm = min(512, S)

    mod6 = _mod_call(c, mod_w, mod_b).reshape(B, 6, D)
    (r, k, v, a, b, g, lw, q, kat, vt) = _inproj_call(
        x, mod6, norm1_gain, w_in, rwkv_shift_mu, rwkv_w0, rwkv_w_up, rwkv_a0, rwkv_a_up,
        rwkv_g_up, rwkv_k_k, rwkv_k_a, tm_in)
    y_rwkv = _rwkv_call(r, k, v, a, b, g, lw, rwkv_r_k, rwkv_gn_gain, rwkv_gn_bias,
                        min(16, S // CHUNK))
    y_diff = _attn_call(q, kat, vt, diff_lambda_q1, diff_lambda_k1, diff_lambda_q2,
                        diff_lambda_k2, diff_subln_gain, tq)

    x1, hp, route, top_w, counts = _outproj_call(
        y_rwkv.reshape(T, -1), y_diff.reshape(T, -1), x.reshape(T, D), mod6, w_out, norm2_gain,
        router_w, router_b, S, tm_out)

    counts = counts.reshape(N_EXPERTS)
    padded = ((counts + MOE_BLOCK - 1) // MOE_BLOCK) * MOE_BLOCK
    pad_ends = jnp.cumsum(padded)
    pad_starts = pad_ends - padded
    n_blocks = -(-(T * TOP_K + N_EXPERTS * (MOE_BLOCK - 1)) // MOE_BLOCK)
    n_used = (pad_ends[-1] // MOE_BLOCK).astype(I32).reshape(1)
    block_start = jnp.minimum(jnp.arange(n_blocks, dtype=I32), n_used[0] - 1) * MOE_BLOCK
    block_e = jnp.minimum(jnp.sum(pad_ends[None, :] <= block_start[:, None], axis=1),
                          N_EXPERTS - 1).astype(I32)
    idx_t = route[:TOP_K]
    expert_ids = jnp.arange(N_EXPERTS, dtype=I32)[:, None, None]
    start_of = jnp.sum(jnp.where(idx_t[None] == expert_ids, pad_starts.astype(I32)[:, None, None], 0),
                       axis=0)
    slots = start_of + route[TOP_K:]
    slot_chunks = slots.reshape(TOP_K * T // SC_WINDOW, SC_WINDOW)
    block_row = jnp.arange(n_blocks, dtype=I32) * MOE_BLOCK
    valid_rows = jnp.clip(counts.astype(I32)[block_e] - (block_row - pad_starts.astype(I32)[block_e]),
                          0, MOE_BLOCK).astype(I32)

    as_tokens = lambda z: z.reshape(-1, ROW_SUB, 128)
    as_lines = lambda z: z.reshape(-1, 128)
    xs = as_lines(_sc_scatter_call(as_tokens(hp), slot_chunks, n_blocks * MOE_BLOCK))
    ys = _experts_call(xs, block_e, n_used, valid_rows, w_gate_up, b_gate_up, w_down, b_down,
                       n_blocks)
    ys_tokens = as_tokens(ys)
    out = x1
    per_part = T // COMBINE_PARTS
    for part in range(COMBINE_PARTS):
        part_slots = slots[:, part * per_part:(part + 1) * per_part]
        part_chunks = part_slots.reshape(TOP_K * per_part // SC_WINDOW, SC_WINDOW)
        gathered = as_lines(_sc_gather_call(ys_tokens, part_chunks))
        out = _combine_dense_call(gathered, out, top_w, mod6, final_gain, S, tm, part, COMBINE_PARTS)
    return out.reshape(B, S, D)


def kernel(x, c, mod_w, mod_b, norm1_gain, w_in, rwkv_shift_mu, rwkv_w0, rwkv_w_up, rwkv_a0, rwkv_a_up, rwkv_g_up, rwkv_k_k, rwkv_k_a, rwkv_r_k, rwkv_gn_gain, rwkv_gn_bias, diff_lambda_q1, diff_lambda_k1, diff_lambda_q2, diff_lambda_k2, diff_subln_gain, w_out, norm2_gain, router_w, router_b, w_gate_up, b_gate_up, w_down, b_down, final_gain):
    return _forward(x, c, mod_w[0], mod_b[0], norm1_gain[0], w_in[0], rwkv_shift_mu[0], rwkv_w0[0],
                    rwkv_w_up[0], rwkv_a0[0], rwkv_a_up[0], rwkv_g_up[0], rwkv_k_k[0], rwkv_k_a[0],
                    rwkv_r_k[0].reshape(-1), rwkv_gn_gain[0], rwkv_gn_bias[0], diff_lambda_q1[0],
                    diff_lambda_k1[0], diff_lambda_q2[0], diff_lambda_k2[0], diff_subln_gain[0],
                    w_out[0], norm2_gain[0], router_w[0], router_b[0], w_gate_up[0], b_gate_up[0],
                    w_down[0], b_down[0], final_gain)
```

```python
import functools
import math

import jax
import jax.numpy as jnp
from jax import lax
from jax.experimental import pallas as pl
from jax.experimental.pallas import tpu as pltpu
from jax.experimental.pallas import tpu_sc as plsc

F32 = jnp.float32
BF16 = jnp.bfloat16
I32 = jnp.int32
U32 = jnp.uint32

D_MODEL = 1024
RWKV_WIDTH = 512
RWKV_HEADS = 8
HEAD_DIM = 64
RWKV_COLS = 3 * RWKV_WIDTH + 64 + 64 + 128
DIFF_WIDTH = 512
DIFF_HEADS = 4
DIFF_COLS = 3 * DIFF_WIDTH
N_EXPERTS = 32
TOP_K = 4
D_EXPERT = 1024
SWIGLU_LIMIT = 7.0
SWIGLU_ALPHA = 1.702
MOE_BLOCK = 512
NORM_EPS = 1e-5
GN_EPS = 64e-5
LAMBDA_INIT = 0.8 - 0.6 * math.exp(-0.3 * 0)

CHUNK = 64
GROUP = 4
GW = GROUP * HEAD_DIM
VT_ROWS = 128 + 16
KEY_SLABS = 2
LOOP_BLOCKS = 4
VMEM_LIMIT = 56 * 1024 * 1024


def _dot(a, b):
    return jnp.dot(a, b, preferred_element_type=F32)


def _dot_nt(a, b):
    return lax.dot_general(a, b, (((1,), (1,)), ((), ())), preferred_element_type=F32)


def _dot_tn(a, b):
    return lax.dot_general(a, b, (((0,), (0,)), ((), ())), preferred_element_type=F32)


def _round_robin(stages):
    waiting = list(stages)
    running = []
    while waiting or running:
        if waiting:
            running.append(waiting.pop(0))
        for gen in list(running):
            try:
                next(gen)
            except StopIteration:
                running.remove(gen)


def _split_dot(x, w_bf16):
    hi = x.astype(BF16)
    lo = (x - hi.astype(F32)).astype(BF16)
    return _dot(hi, w_bf16) + _dot(lo, w_bf16)


def _mod_kernel(c_ref, w_ref, b_ref, o_ref):
    c = c_ref[...]
    s = c * jax.nn.sigmoid(c)
    o_ref[...] = _dot(s, w_ref[...]) + b_ref[...]


def _mod_call(c, mod_w, mod_b):
    B = c.shape[0]
    n = mod_w.shape[1]
    tn = 1536
    return pl.pallas_call(
        _mod_kernel,
        grid=(n // tn,),
        in_specs=[pl.BlockSpec((B, D_MODEL), lambda j: (0, 0)),
                  pl.BlockSpec((D_MODEL, tn), lambda j: (0, j)),
                  pl.BlockSpec((1, tn), lambda j: (0, j))],
        out_specs=pl.BlockSpec((B, tn), lambda j: (0, j)),
        out_shape=jax.ShapeDtypeStruct((B, n), F32),
        compiler_params=pltpu.CompilerParams(
            dimension_semantics=("arbitrary",), vmem_limit_bytes=VMEM_LIMIT),
        name="mod",
    )(c, mod_w, mod_b.reshape(1, n))


def _inproj_kernel(x_ref, mod_ref, g1_ref, wrw_ref, wat_ref, mu_ref, w0_ref, wup_ref,
                   a0_ref, aup_ref, gup_ref, kk_ref, ka_ref, hsum_ref,
                   r_out, k_out, v_out, a_out, b_out, g_out, lw_out,
                   q_out, kat_out, vt_out, carry_ref, *, tm):
    s = pl.program_id(1)
    shift1 = mod_ref[0, 0:1, :]
    scale1 = mod_ref[0, 1:2, :]

    @pl.when(s == 0)
    def _():
        carry_ref[...] = jnp.zeros_like(carry_ref)

    n_part = max(1, tm // 128)
    rows_per = tm // n_part
    last_row = [carry_ref[...]]

    def part(i):
        rs = slice(i * rows_per, (i + 1) * rows_per)
        x = x_ref[0, rs, :]
        ms = jnp.mean(x * x, axis=-1, keepdims=True)
        h = x * lax.rsqrt(ms + NORM_EPS) * g1_ref[...] * (1.0 + scale1) + shift1
        hb = h.astype(BF16)
        yield
        p = _dot(hb, wrw_ref[...])
        last_row.append(p[rows_per - 1:rows_per, :])
        yield
        pa = _dot(hb, wat_ref[...])
        yield
        rolled = pltpu.roll(p, shift=1, axis=0)
        row = lax.broadcasted_iota(I32, p.shape, 0)
        prev = jnp.where(row == 0, last_row[i], rolled)
        ps = p + mu_ref[...] * (prev - p)
        r = ps[:, 0:512]
        k = ps[:, 512:1024]
        v = ps[:, 1024:1536]
        lo2 = ps[:, 1536:1664]
        g_lo = ps[:, 1664:1792]
        kk = k * kk_ref[...]
        yield
        z = w0_ref[...] + _dot(jnp.tanh(lo2).astype(BF16), wup_ref[...])
        a_pre = a0_ref[...] + _dot(lo2.astype(BF16), aup_ref[...])
        g = _dot(jax.nn.sigmoid(g_lo).astype(BF16), gup_ref[...])
        ssq = _split_dot(kk * kk, hsum_ref[...])
        yield
        nz = -z
        softplus = jnp.maximum(nz, 0.0) + jnp.log(1.0 + jnp.exp(-jnp.abs(nz)))
        w = -softplus - 0.5
        lw_out[0, rs, :] = -jnp.exp(w)
        a = jax.nn.sigmoid(a_pre)
        kk = kk / jnp.maximum(jnp.sqrt(ssq), 1e-12)
        k = k * (1.0 + (a - 1.0) * ka_ref[...])
        r_out[0, rs, :] = r.astype(BF16)
        k_out[0, rs, :] = k.astype(BF16)
        v_out[0, rs, :] = v.astype(BF16)
        a_out[0, rs, :] = (-kk).astype(BF16)
        b_out[0, rs, :] = (kk * a).astype(BF16)
        g_out[0, rs, :] = g.astype(BF16)
        yield
        q_out[0, rs, :] = (pa[:, 0:512] * (1.0 / math.sqrt(HEAD_DIM))).astype(BF16)
        kat_out[0, rs, :] = pa[:, 512:1024].astype(BF16)
        for hh in range(DIFF_HEADS):
            vh = pa[:, 1024 + hh * 128:1024 + (hh + 1) * 128]
            vt_out[0, hh, 0, 0:128, rs] = vh.T.astype(BF16)

    _round_robin([part(i) for i in range(n_part)])
    carry_ref[...] = last_row[n_part]
    for hh in range(DIFF_HEADS):
        vt_out[0, hh, 0, 128:VT_ROWS, :] = jnp.ones((VT_ROWS - 128, tm), BF16)


def _head_sum_matrix(width):
    i = jnp.arange(width) // HEAD_DIM
    return (i[:, None] == i[None, :]).astype(BF16)


def _inproj_call(x, mod6, norm1_gain, w_in, mu, w0, w_up, a0, a_up, g_up, k_k, k_a, tm):
    B, S, _ = x.shape
    ns = S // tm
    w_rw = w_in[:, :RWKV_COLS].astype(BF16)
    w_at = w_in[:, RWKV_COLS:].astype(BF16)
    zeros = jnp.zeros((64, RWKV_WIDTH), F32)
    wup_p = jnp.concatenate([w_up, zeros], axis=0).astype(BF16)
    aup_p = jnp.concatenate([zeros, a_up], axis=0).astype(BF16)
    row = lambda v: v.reshape(1, -1)
    full = lambda shape: pl.BlockSpec(shape, lambda b, s: (0,) * len(shape))
    tok = lambda w: pl.BlockSpec((1, tm, w), lambda b, s: (b, s, 0))
    rw_shape = jax.ShapeDtypeStruct((B, S, RWKV_WIDTH), BF16)
    out_shape = [rw_shape] * 6 + [
        jax.ShapeDtypeStruct((B, S, RWKV_WIDTH), F32),
        jax.ShapeDtypeStruct((B, S, DIFF_WIDTH), BF16),
        jax.ShapeDtypeStruct((B, S, DIFF_WIDTH), BF16),
        jax.ShapeDtypeStruct((B, DIFF_HEADS, ns, VT_ROWS, tm), BF16)]
    out_specs = [tok(RWKV_WIDTH)] * 7 + [tok(DIFF_WIDTH)] * 2 + [
        pl.BlockSpec((1, DIFF_HEADS, 1, VT_ROWS, tm), lambda b, s: (b, 0, s, 0, 0))]
    return pl.pallas_call(
        functools.partial(_inproj_kernel, tm=tm),
        grid=(B, ns),
        in_specs=[tok(D_MODEL),
                  pl.BlockSpec((1, 6, D_MODEL), lambda b, s: (b, 0, 0)),
                  full((1, D_MODEL)),
                  full((D_MODEL, RWKV_COLS)), full((D_MODEL, DIFF_COLS)),
                  full((1, RWKV_COLS)), full((1, RWKV_WIDTH)), full((128, RWKV_WIDTH)),
                  full((1, RWKV_WIDTH)), full((128, RWKV_WIDTH)), full((128, RWKV_WIDTH)),
                  full((1, RWKV_WIDTH)), full((1, RWKV_WIDTH)),
                  full((RWKV_WIDTH, RWKV_WIDTH))],
        out_specs=out_specs,
        out_shape=out_shape,
        scratch_shapes=[pltpu.VMEM((1, RWKV_COLS), F32)],
        compiler_params=pltpu.CompilerParams(
            dimension_semantics=("arbitrary", "arbitrary"), vmem_limit_bytes=VMEM_LIMIT),
        name="inproj",
    )(x, mod6, row(norm1_gain), w_rw, w_at, row(mu), row(w0), wup_p, row(a0), aup_p,
      g_up.astype(BF16), row(k_k), row(k_a), _head_sum_matrix(RWKV_WIDTH))


def _rwkv_kernel(r_ref, k_ref, v_ref, a_ref, b_ref, g_ref, lw_ref, rk_ref, gng_ref, gnb_ref,
                 hsum_ref, y_ref, s0_ref, s1_ref, *, nck):
    C = CHUNK
    n_grp = RWKV_HEADS // GROUP
    s_refs = (s0_ref, s1_ref)

    @pl.when(pl.program_id(1) == 0)
    def _():
        s0_ref[...] = jnp.zeros_like(s0_ref)
        s1_ref[...] = jnp.zeros_like(s1_ref)

    ti = lax.broadcasted_iota(I32, (C, GW), 0)
    si = lax.broadcasted_iota(I32, (C, GW), 1) % C
    incl = si <= ti
    strict = si < ti
    eye = (si == ti).astype(F32)
    bdmask = (lax.broadcasted_iota(I32, (GW, GW), 0) // HEAD_DIM ==
              lax.broadcasted_iota(I32, (GW, GW), 1) // HEAD_DIM)
    hsum = hsum_ref[...]
    chains = [(ck, g) for ck in range(nck) for g in range(n_grp)]

    bdmask_bf = bdmask.astype(BF16)

    def bd(xb):
        return jnp.concatenate([xb] * GROUP, axis=0) * bdmask_bf

    tri =(lax.broadcasted_iota(I32, (C, C), 0) >=
           lax.broadcasted_iota(I32, (C, C), 1)).astype(BF16)
    done = {}
    sv = [s_refs[g][...] for g in range(n_grp)]

    def chain(ch):
        ck, g = ch
        rows = slice(ck * C, (ck + 1) * C)
        cols = slice(g * GW, (g + 1) * GW)
        lw = lw_ref[0, rows, cols]
        L = _split_dot_left(tri, lw)
        yield
        Lx = L - lw
        Lc = L[C - 1:C, :]
        rho = L[C // 2 - 1:C // 2, :]
        r = r_ref[0, rows, cols].astype(F32)
        k = k_ref[0, rows, cols].astype(F32)
        vb = v_ref[0, rows, cols]
        a = a_ref[0, rows, cols].astype(F32)
        b = b_ref[0, rows, cols].astype(F32)
        e_k = jnp.exp(rho - L)
        lhs = jnp.concatenate([(r * jnp.exp(L - rho)).astype(BF16),
                               (a * jnp.exp(Lx - rho)).astype(BF16)], axis=0)
        kbd = bd((k * e_k).astype(BF16))
        bbd = bd((b * e_k).astype(BF16))
        r0 = (r * jnp.exp(L)).astype(BF16)
        a0bd = bd((a * jnp.exp(Lx)).astype(BF16))
        e_o = jnp.exp(Lc - L)
        bk = jnp.concatenate([(b * e_o).astype(BF16), (k * e_o).astype(BF16)], axis=0)
        vbd = bd(vb)
        yield
        ak = _dot_nt(lhs, kbd)
        ab = _dot_nt(lhs, bbd)
        yield
        a_rk = jnp.where(incl, ak[:C], 0.0).astype(BF16)
        a_ak = jnp.where(strict, ak[C:], 0.0).astype(BF16)
        a_rb = jnp.where(incl, ab[:C], 0.0).astype(BF16)
        n = jnp.where(strict, ab[C:], 0.0)
        p = eye + n
        xb = n.astype(BF16)
        xbd = bd(xb)
        yield
        x = _dot(xb, xbd)
        yield
        for _ in range(4):
            xb = x.astype(BF16)
            stack = jnp.concatenate([xb, p.astype(BF16)], axis=0)
            xbd = bd(xb)
            yield
            o = _dot(stack, xbd)
            yield
            x = o[:C]
            p = p + o[C:]
        pb = p.astype(BF16)
        xbd = bd(x.astype(BF16))
        yield
        o = _dot(pb, xbd)
        yield
        tb = (p + o).astype(BF16)
        yield
        av = _dot(jnp.concatenate([a_ak, a_rk], axis=0), vbd)
        w1 = _dot(tb, a0bd)
        yield
        akvbd = bd(av[:C].astype(BF16))
        rw_lhs = jnp.concatenate([r0, w1.astype(BF16)], axis=0)
        yield
        w2 = _dot(tb, akvbd)
        yield
        done[ch] = dict(rw_lhs=rw_lhs, w2=w2, a_rb=a_rb, arkv=av[C:], vb=vb, bk=bk,
                        gc=jnp.exp(Lc), rk=r * k)

    def sequential(ck):
        d = [done[(ck, g)] for g in range(n_grp)]
        rws = [_dot_nt(d[g]["rw_lhs"], sv[g].astype(BF16)) for g in range(n_grp)]
        yield
        ub = [(rws[g][C:] + d[g]["w2"]).astype(BF16) for g in range(n_grp)]
        uv = [jnp.concatenate([ub[g], d[g]["vb"]], axis=0) for g in range(n_grp)]
        ubd = [bd(ub[g]) for g in range(n_grp)]
        yield
        upd = [_dot_tn(uv[g], d[g]["bk"]) for g in range(n_grp)]
        yield
        for g in range(n_grp):
            sv[g] = sv[g] * d[g]["gc"] + jnp.where(bdmask, upd[g], 0.0)
        done[("state", ck)] = True
        y = [rws[g][:C] + _dot(d[g]["a_rb"], ubd[g]) + d[g]["arkv"] for g in range(n_grp)]
        yield
        mean = [_split_dot(y[g], hsum) * (1.0 / HEAD_DIM) for g in range(n_grp)]
        yield
        dev = [y[g] - mean[g] for g in range(n_grp)]
        var = [_split_dot(dev[g] * dev[g], hsum) * (1.0 / HEAD_DIM) for g in range(n_grp)]
        bonus = [_split_dot(d[g]["rk"] * rk_ref[:, g * GW:(g + 1) * GW], hsum) for g in range(n_grp)]
        yield
        rows = slice(ck * C, (ck + 1) * C)
        for g in range(n_grp):
            cols = slice(g * GW, (g + 1) * GW)
            yn = dev[g] * lax.rsqrt(var[g] + GN_EPS) * gng_ref[:, cols] + gnb_ref[:, cols]
            out = (yn + bonus[g] * d[g]["vb"].astype(F32)) * g_ref[0, rows, cols].astype(F32)
            y_ref[0, rows, cols] = out.astype(BF16)

    waiting = [chain(ch) for ch in chains]
    seq_next = 0
    running = []
    while waiting or running or seq_next < nck:
        ready = (seq_next < nck and all((seq_next, g) in done for g in range(n_grp))
                 and (seq_next == 0 or ("state", seq_next - 1) in done))
        if ready:
            running.append(sequential(seq_next))
            seq_next += 1
        elif waiting:
            running.append(waiting.pop(0))
        for gen in list(running):
            try:
                next(gen)
            except StopIteration:
                running.remove(gen)
    for g in range(n_grp):
        s_refs[g][...] = sv[g]


def _split_dot_left(w_bf16, x):
    hi = x.astype(BF16)
    lo = (x - hi.astype(F32)).astype(BF16)
    return _dot(w_bf16, hi) + _dot(w_bf16, lo)


def _rwkv_call(r, k, v, a, b, g, lw, r_k, gn_gain, gn_bias, nck):
    B, S, _ = r.shape
    tb = nck * CHUNK
    tok = pl.BlockSpec((1, tb, RWKV_WIDTH), lambda bb, c: (bb, c, 0))
    par = pl.BlockSpec((1, RWKV_WIDTH), lambda bb, c: (0, 0))
    return pl.pallas_call(
        functools.partial(_rwkv_kernel, nck=nck),
        grid=(B, S // tb),
        in_specs=[tok] * 7 + [par] * 3 + [pl.BlockSpec((GW, GW), lambda bb, c: (0, 0))],
        out_specs=tok,
        out_shape=jax.ShapeDtypeStruct((B, S, RWKV_WIDTH), BF16),
        scratch_shapes=[pltpu.VMEM((GW, GW), F32)] * (RWKV_HEADS // GROUP),
        compiler_params=pltpu.CompilerParams(
            dimension_semantics=("arbitrary", "arbitrary"), vmem_limit_bytes=VMEM_LIMIT),
        name="rwkv",
    )(r, k, v, a, b, g, lw, r_k.reshape(1, -1), gn_gain.reshape(1, -1), gn_bias.reshape(1, -1),
      _head_sum_matrix(GW))


def _tree_reduce(op, x):
    while x.shape[0] > 8:
        h = x.shape[0] // 2
        x = op(x[:h], x[h:])
    return x


def _attn_kernel(slopes_ref, q_ref, k_ref, vt_ref, lq1_ref, lk1_ref, lq2_ref, lk2_ref,
                 gain_ref, o_ref, sa_ref, sb_ref, acc_ref, m_ref, *, tq, tk):
    hh = pl.program_id(1)
    i = pl.program_id(2)
    slope = slopes_ref[hh]

    q = q_ref[0].astype(F32)
    lane = lax.broadcasted_iota(I32, (tq, 256), 1)
    qrow = lax.broadcasted_iota(I32, (tq, 256), 0)
    qa = (qrow >> 5).astype(F32) * (32.0 * slope)
    qb = (qrow & 31).astype(F32) * slope
    qbias = jnp.where(lane < 130, 1.0, jnp.where(lane == 130, -qa, jnp.where(lane == 131, -qb, 0.0)))
    qpad = jnp.concatenate([q, jnp.zeros_like(q)], axis=1)
    q_aug = []
    for c in range(2):
        in_comp = (lane >= c * HEAD_DIM) & (lane < (c + 1) * HEAD_DIM)
        q_aug.append(jnp.where(in_comp, qpad, jnp.where(lane >= 128, qbias, 0.0)))
    klane = lax.broadcasted_iota(I32, (tk, 128), 1)
    krow = lax.broadcasted_iota(I32, (tk, 128), 0)
    ka = (krow >> 5).astype(F32) * (32.0 * slope)
    kb_ = (krow & 31).astype(F32) * slope
    kbias = jnp.where(klane == 0, ka, jnp.where(klane == 1, kb_, jnp.where(klane < 4, 1.0, 0.0))
                      ).astype(BF16)

    qt_both = jnp.concatenate([q_aug[0].T, q_aug[1].T], axis=1).astype(BF16)

    m_ref[...] = jnp.full_like(m_ref, -jnp.inf)
    acc_ref[...] = jnp.zeros_like(acc_ref)

    def scores(j):
        kb = k_ref[0, pl.ds(pl.multiple_of(j * tk, tk), tk), :]
        return _dot(jnp.concatenate([kb, kbias], axis=1), qt_both)

    def absorb(s_ref, j, masked):
        s = s_ref[...]
        off = j * tk - i * tq
        cj = slope * off.astype(F32)
        if masked:
            keep = (lax.broadcasted_iota(I32, (tk, tq), 0) -
                    lax.broadcasted_iota(I32, (tk, tq), 1) + off) <= 0
            s = jnp.where(jnp.concatenate([keep, keep], axis=1), s, -jnp.inf)
        m_loc = jnp.max(_tree_reduce(jnp.maximum, s), axis=0, keepdims=True)
        m_old = m_ref[...]
        m_new = jnp.maximum(m_old, m_loc + cj)
        shift = m_new - cj
        vt = vt_ref[0, 0, j]
        rows = tk // KEY_SLABS
        pv = None
        for r in range(KEY_SLABS):
            pr = jnp.exp((s[r * rows:(r + 1) * rows] - shift).astype(BF16))
            part = _dot(vt[:, r * rows:(r + 1) * rows], pr)
            pv = part if pv is None else pv + part
        alpha = jnp.exp(m_old - m_new)
        acc_ref[...] = alpha * acc_ref[...] + pv
        m_ref[...] = m_new

    n_full = (i * tq) // tk
    sa_ref[...] = scores(0)
    bufs = (sa_ref, sb_ref)

    def run_full(j0, count):
        for b in range(count):
            bufs[(b + 1) % 2][...] = scores(j0 + b + 1)
            absorb(bufs[b % 2], j0 + b, False)

    def unrolled_body(jj, carry):
        run_full(jj * LOOP_BLOCKS, LOOP_BLOCKS)
        return carry

    lax.fori_loop(0, n_full // LOOP_BLOCKS, unrolled_body, 0)

    rest = n_full % LOOP_BLOCKS
    for k in range(LOOP_BLOCKS):
        @pl.when(rest == k)
        def _():
            run_full(n_full - k, k)
            absorb(bufs[k % 2], n_full, True)

    lam = (jnp.exp(jnp.sum(lq1_ref[...] * lk1_ref[...], axis=-1, keepdims=True))
           - jnp.exp(jnp.sum(lq2_ref[...] * lk2_ref[...], axis=-1, keepdims=True))
           + LAMBDA_INIT)
    o2 = acc_ref[0:128, :] * (1.0 / acc_ref[128:129, :])
    o = o2[:, :tq] - lam * o2[:, tq:]
    ot = o.T
    ms = jnp.mean(ot * ot, axis=-1, keepdims=True)
    y = ot * lax.rsqrt(ms + NORM_EPS) * gain_ref[...] * (1.0 - LAMBDA_INIT)
    o_ref[0] = y.astype(BF16)


def _attn_call(q, k, vt, lq1, lk1, lq2, lk2, subln_gain, tq):
    B, S, _ = q.shape
    ns, tk = vt.shape[2], vt.shape[4]
    slopes = jnp.asarray([2.0 ** (-8.0 * (i + 1) / DIFF_HEADS) for i in range(DIFF_HEADS)], F32)
    vec = lambda n: pl.BlockSpec((1, n), lambda b, h, i, sl: (0, 0))
    grid_spec = pltpu.PrefetchScalarGridSpec(
        num_scalar_prefetch=1,
        grid=(B, DIFF_HEADS, S // tq),
        in_specs=[pl.BlockSpec((1, tq, 128), lambda b, h, i, sl: (b, i, h)),
                  pl.BlockSpec((1, S, 128), lambda b, h, i, sl: (b, 0, h)),
                  pl.BlockSpec((1, 1, ns, VT_ROWS, tk), lambda b, h, i, sl: (b, h, 0, 0, 0)),
                  vec(HEAD_DIM), vec(HEAD_DIM), vec(HEAD_DIM), vec(HEAD_DIM), vec(128)],
        out_specs=pl.BlockSpec((1, tq, 128), lambda b, h, i, sl: (b, i, h)),
        scratch_shapes=[pltpu.VMEM((tk, 2 * tq), F32), pltpu.VMEM((tk, 2 * tq), F32),
                        pltpu.VMEM((VT_ROWS, 2 * tq), F32), pltpu.VMEM((1, 2 * tq), F32)])
    return pl.pallas_call(
        functools.partial(_attn_kernel, tq=tq, tk=tk),
        grid_spec=grid_spec,
        out_shape=jax.ShapeDtypeStruct((B, S, DIFF_WIDTH), BF16),
        compiler_params=pltpu.CompilerParams(
            dimension_semantics=("arbitrary", "arbitrary", "arbitrary"),
            vmem_limit_bytes=VMEM_LIMIT),
        name="attn",
    )(slopes, q, k, vt, lq1.reshape(1, -1), lk1.reshape(1, -1), lq2.reshape(1, -1),
      lk2.reshape(1, -1), subln_gain.reshape(1, -1))


def _pack_bf16_pairs(x):
    w = x.shape[1] // 2
    lo = pltpu.bitcast(x[:, :w].astype(BF16).astype(F32), U32)
    hi = pltpu.bitcast(x[:, w:].astype(BF16).astype(F32), U32)
    return (lo >> 16) | (hi & jnp.uint32(0xFFFF0000))


def _unpack_bf16_pairs(p):
    lo = pltpu.bitcast(p << 16, F32)
    hi = pltpu.bitcast(p & jnp.uint32(0xFFFF0000), F32)
    return lo, hi


ROW_SUB = (D_MODEL // 2) // 128


def _store_rows(ref, x2d):
    n = x2d.shape[0]
    for s in range(ROW_SUB):
        ref[pl.ds(s, n, stride=ROW_SUB), :] = x2d[:, s * 128:(s + 1) * 128]


def _load_rows(ref):
    n = ref.shape[0] // ROW_SUB
    return jnp.concatenate([ref[pl.ds(s, n, stride=ROW_SUB), :] for s in range(ROW_SUB)], axis=1)


def _outproj_kernel(yr_ref, yd_ref, x_ref, mod_ref, wo_r_ref, wo_d_ref, g2_ref, rw_hi_ref,
                    rw_lo_ref, rb_ref, x1_out, hp_out, route_out, w_out, cnt_out,
                    carry_ref, *, tm):
    t = pl.program_id(0)

    @pl.when(t == 0)
    def _():
        carry_ref[...] = jnp.zeros_like(carry_ref)

    gate1 = mod_ref[0, 2:3, :]
    shift2 = mod_ref[0, 3:4, :]
    scale2 = mod_ref[0, 4:5, :]
    n_part = max(1, tm // 256)
    rows_per = tm // n_part
    eidx = lax.broadcasted_iota(I32, (rows_per, N_EXPERTS), 1)
    col4 = lax.broadcasted_iota(I32, (rows_per, TOP_K), 1)
    lane128 = lax.broadcasted_iota(I32, (rows_per, 128), 1)
    tri = (lax.broadcasted_iota(I32, (rows_per, rows_per), 0) >
           lax.broadcasted_iota(I32, (rows_per, rows_per), 1)).astype(BF16)
    before = [carry_ref[...]]

    def part(i):
        rs = slice(i * rows_per, (i + 1) * rows_per)
        mix = _dot(yr_ref[rs, :], wo_r_ref[...]) + _dot(yd_ref[rs, :], wo_d_ref[...])
        yield
        x1 = x_ref[rs, :] + gate1 * mix
        x1_out[rs, :] = x1
        ms = jnp.mean(x1 * x1, axis=-1, keepdims=True)
        h = x1 * lax.rsqrt(ms + NORM_EPS) * g2_ref[...] * (1.0 + scale2) + shift2
        _store_rows(hp_out.at[pl.ds(i * rows_per * ROW_SUB, rows_per * ROW_SUB), :],
                    _pack_bf16_pairs(h))
        hi = h.astype(BF16)
        lo = (h - hi.astype(F32)).astype(BF16)
        yield
        logits = (_dot(hi, rw_hi_ref[...]) + _dot(hi, rw_lo_ref[...]) + _dot(lo, rw_hi_ref[...])
                  + rb_ref[...])
        yield
        lg = logits
        vals, idxs = [], []
        onehot = jnp.zeros(logits.shape, F32)
        for _ in range(TOP_K):
            m = jnp.max(lg, axis=-1, keepdims=True)
            ix = jnp.min(jnp.where(lg == m, eidx, N_EXPERTS), axis=-1, keepdims=True)
            sel = eidx == ix
            vals.append(m)
            idxs.append(ix)
            onehot = onehot + sel.astype(F32)
            lg = jnp.where(sel, -jnp.inf, lg)
        es = [jnp.exp(v - vals[0]) for v in vals]
        den = es[0] + es[1] + es[2] + es[3]
        before.append(before[i] + jnp.sum(onehot, axis=0, keepdims=True))
        yield
        prefix = _dot(tri, onehot.astype(BF16)) + before[i]
        yield
        w4 = jnp.zeros((rows_per, TOP_K), F32)
        table = jnp.zeros((rows_per, 128), F32)
        for kk in range(TOP_K):
            rk = jnp.sum(jnp.where(eidx == idxs[kk], prefix, 0.0), axis=-1, keepdims=True)
            w4 = jnp.where(col4 == kk, es[kk] / den, w4)
            table = jnp.where(lane128 == kk, idxs[kk].astype(F32), table)
            table = jnp.where(lane128 == TOP_K + kk, rk, table)
        w_out[rs, :] = w4
        route_out[:, rs] = table.T[0:2 * TOP_K, :].astype(I32)

    _round_robin([part(i) for i in range(n_part)])
    carry_ref[...] = before[n_part]
    cnt_out[...] = before[n_part].astype(I32)


def _outproj_call(y_rwkv, y_diff, x, mod6, w_out, norm2_gain, router_w, router_b, S, tm):
    T = x.shape[0]
    tiles_per_seq = S // tm
    rw_hi = router_w.astype(BF16)
    rw_lo = (router_w - rw_hi.astype(F32)).astype(BF16)
    tok = lambda w: pl.BlockSpec((tm, w), lambda t: (t, 0))
    full = lambda shape: pl.BlockSpec(shape, lambda t: (0,) * len(shape))
    return pl.pallas_call(
        functools.partial(_outproj_kernel, tm=tm),
        grid=(T // tm,),
        in_specs=[tok(RWKV_WIDTH), tok(DIFF_WIDTH), tok(D_MODEL),
                  pl.BlockSpec((1, 6, D_MODEL), lambda t: (t // tiles_per_seq, 0, 0)),
                  full((RWKV_WIDTH, D_MODEL)), full((DIFF_WIDTH, D_MODEL)), full((1, D_MODEL)),
                  full((D_MODEL, N_EXPERTS)), full((D_MODEL, N_EXPERTS)), full((1, N_EXPERTS))],
        out_specs=[tok(D_MODEL), pl.BlockSpec((tm * ROW_SUB, 128), lambda t: (t, 0)),
                   pl.BlockSpec((2 * TOP_K, tm), lambda t: (0, t)), tok(TOP_K),
                   full((1, N_EXPERTS))],
        out_shape=[jax.ShapeDtypeStruct((T, D_MODEL), F32),
                   jax.ShapeDtypeStruct((T * ROW_SUB, 128), U32),
                   jax.ShapeDtypeStruct((2 * TOP_K, T), I32),
                   jax.ShapeDtypeStruct((T, TOP_K), F32),
                   jax.ShapeDtypeStruct((1, N_EXPERTS), I32)],
        scratch_shapes=[pltpu.VMEM((1, N_EXPERTS), F32)],
        compiler_params=pltpu.CompilerParams(
            dimension_semantics=("arbitrary",), vmem_limit_bytes=VMEM_LIMIT),
        name="outproj",
    )(y_rwkv, y_diff, x, mod6, w_out[:RWKV_WIDTH].astype(BF16), w_out[RWKV_WIDTH:].astype(BF16),
      norm2_gain.reshape(1, -1), rw_hi, rw_lo, router_b.reshape(1, -1))


def _experts_kernel(be_ref, nu_ref, valid_ref, xs_ref, wgu_ref, bgu_ref, wd_ref, bd_ref, ys_ref,
                    wgu_bf, wd_bf):
    i = pl.program_id(0)

    @pl.when((i == 0) | (be_ref[i] != be_ref[jnp.maximum(i - 1, 0)]))
    def _():
        wgu_bf[...] = wgu_ref[0].astype(BF16)
        wd_bf[...] = wd_ref[0].astype(BF16)

    @pl.when(i < nu_ref[0])
    def _():
        packed = _load_rows(xs_ref)
        row = lax.broadcasted_iota(I32, packed.shape, 0)
        packed = jnp.where(row < valid_ref[i], packed, jnp.uint32(0))
        xa, xb = _unpack_bf16_pairs(packed)
        x = jnp.concatenate([xa.astype(BF16), xb.astype(BF16)], axis=1)
        gu = _dot(x, wgu_bf[...]) + bgu_ref[0]
        gate = jnp.minimum(gu[:, :D_EXPERT], SWIGLU_LIMIT)
        up = jnp.clip(gu[:, D_EXPERT:], -SWIGLU_LIMIT, SWIGLU_LIMIT)
        act = (up + 1.0) * (gate * jax.nn.sigmoid(SWIGLU_ALPHA * gate))
        y = _dot(act.astype(BF16), wd_bf[...]) + bd_ref[0]
        _store_rows(ys_ref, _pack_bf16_pairs(y))

    @pl.when(i >= nu_ref[0])
    def _():
        ys_ref[...] = jnp.zeros_like(ys_ref)


def _experts_call(xs, block_e, n_used, valid_rows, w_gate_up, b_gate_up, w_down, b_down, n_blocks):
    per_expert = lambda shape: pl.BlockSpec(shape, lambda i, be, nu, vr: (be[i], 0, 0))
    grid_spec = pltpu.PrefetchScalarGridSpec(
        num_scalar_prefetch=3,
        grid=(n_blocks,),
        in_specs=[pl.BlockSpec((MOE_BLOCK * ROW_SUB, 128),
                               lambda i, be, nu, vr: (jnp.minimum(i, nu[0] - 1), 0)),
                  per_expert((1, D_MODEL, 2 * D_EXPERT)), per_expert((1, 1, 2 * D_EXPERT)),
                  per_expert((1, D_EXPERT, D_MODEL)), per_expert((1, 1, D_MODEL))],
        out_specs=pl.BlockSpec((MOE_BLOCK * ROW_SUB, 128), lambda i, be, nu, vr: (i, 0)),
        scratch_shapes=[pltpu.VMEM((D_MODEL, 2 * D_EXPERT), BF16),
                        pltpu.VMEM((D_EXPERT, D_MODEL), BF16)])
    return pl.pallas_call(
        _experts_kernel,
        grid_spec=grid_spec,
        out_shape=jax.ShapeDtypeStruct((n_blocks * MOE_BLOCK * ROW_SUB, 128), U32),
        compiler_params=pltpu.CompilerParams(
            dimension_semantics=("arbitrary",), vmem_limit_bytes=VMEM_LIMIT),
        name="experts",
    )(block_e, n_used, valid_rows, xs, w_gate_up, b_gate_up.reshape(N_EXPERTS, 1, -1), w_down,
      b_down.reshape(N_EXPERTS, 1, -1))


SC_WINDOW = 128
COMBINE_PARTS = 4


def _sc_mesh():
    return plsc.VectorSubcoreMesh(core_axis_name="c", subcore_axis_name="s")


def _sc_worker_chunks(n_chunks):
    info = plsc.get_sparse_core_info()
    n_workers = info.num_cores * info.num_subcores
    assert n_chunks % n_workers == 0
    return info.num_cores, n_chunks // n_workers


def _sc_scatter_call(rows, slot_chunks, n_out_rows):
    T = rows.shape[0]
    per_pass = T // SC_WINDOW
    n_cores, per_worker = _sc_worker_chunks(per_pass)

    @functools.partial(
        pl.kernel, mesh=_sc_mesh(),
        out_type=jax.ShapeDtypeStruct((n_out_rows, ROW_SUB, 128), U32),
        scratch_types=[pltpu.VMEM((SC_WINDOW,), I32), pltpu.VMEM((SC_WINDOW, ROW_SUB, 128), U32)],
        name="sc_dispatch")
    def run(rows_hbm, idx_hbm, out_hbm, idx_v, rows_v):
        wid = lax.axis_index("s") * n_cores + lax.axis_index("c")

        @pl.loop(0, per_worker)
        def _(j):
            chunk = wid * per_worker + j
            pltpu.sync_copy(rows_hbm.at[pl.ds(chunk * SC_WINDOW, SC_WINDOW)], rows_v)
            for kk in range(TOP_K):
                pltpu.sync_copy(idx_hbm.at[kk * per_pass + chunk], idx_v)
                pltpu.sync_copy(rows_v, out_hbm.at[idx_v])

    return run(rows, slot_chunks)


def _sc_gather_call(table, slot_chunks):
    n_chunks = slot_chunks.shape[0]
    n_cores, per_worker = _sc_worker_chunks(n_chunks)

    @functools.partial(
        pl.kernel, mesh=_sc_mesh(),
        out_type=jax.ShapeDtypeStruct((n_chunks * SC_WINDOW, ROW_SUB, 128), U32),
        scratch_types=[pltpu.VMEM((SC_WINDOW,), I32), pltpu.VMEM((SC_WINDOW, ROW_SUB, 128), U32)],
        name="sc_collect")
    def run(table_hbm, idx_hbm, out_hbm, idx_v, rows_v):
        wid = lax.axis_index("s") * n_cores + lax.axis_index("c")

        @pl.loop(0, per_worker)
        def _(j):
            chunk = wid * per_worker + j
            pltpu.sync_copy(idx_hbm.at[chunk], idx_v)
            pltpu.sync_copy(table_hbm.at[idx_v], rows_v)
            pltpu.sync_copy(rows_v, out_hbm.at[pl.ds(chunk * SC_WINDOW, SC_WINDOW)])

    return run(table, slot_chunks)


def _combine_dense_kernel(g0_ref, g1_ref, g2_ref, g3_ref, x1_ref, w_ref, mod_ref, fg_ref, o_ref,
                          *, tm):
    w = w_ref[...]
    acc_lo = jnp.zeros((tm, D_MODEL // 2), F32)
    acc_hi = jnp.zeros((tm, D_MODEL // 2), F32)
    for kk, g_ref in enumerate((g0_ref, g1_ref, g2_ref, g3_ref)):
        lo, hi = _unpack_bf16_pairs(_load_rows(g_ref))
        wk = w[:, kk:kk + 1]
        acc_lo = acc_lo + wk * lo
        acc_hi = acc_hi + wk * hi
    moe = jnp.concatenate([acc_lo, acc_hi], axis=1)
    gate2 = mod_ref[0, 5:6, :]
    x2 = x1_ref[...] + gate2 * moe
    ms = jnp.mean(x2 * x2, axis=-1, keepdims=True)
    o_ref[...] = x2 * lax.rsqrt(ms + NORM_EPS) * fg_ref[...]


def _combine_dense_call(gathered, x1, top_w, mod6, final_gain, S, tm, part, n_parts):
    T = x1.shape[0]
    assert T % (tm * n_parts) == 0
    nt = T // tm // n_parts
    first = part * nt
    tiles_per_seq = S // tm
    rows = lambda kk: pl.BlockSpec((tm * ROW_SUB, 128), lambda t: (kk * nt + t, 0))
    return pl.pallas_call(
        functools.partial(_combine_dense_kernel, tm=tm),
        grid=(nt,),
        in_specs=[rows(0), rows(1), rows(2), rows(3),
                  pl.BlockSpec((tm, D_MODEL), lambda t: (first + t, 0)),
                  pl.BlockSpec((tm, TOP_K), lambda t: (first + t, 0)),
                  pl.BlockSpec((1, 6, D_MODEL), lambda t: ((first + t) // tiles_per_seq, 0, 0)),
                  pl.BlockSpec((1, D_MODEL), lambda t: (0, 0))],
        out_specs=pl.BlockSpec((tm, D_MODEL), lambda t: (first + t, 0)),
        out_shape=jax.ShapeDtypeStruct((T, D_MODEL), F32),
        input_output_aliases={4: 0},
        compiler_params=pltpu.CompilerParams(
            dimension_semantics=("arbitrary",), vmem_limit_bytes=VMEM_LIMIT),
        name="combine",
    )(gathered, gathered, gathered, gathered, x1, top_w, mod6, final_gain.reshape(1, -1))


def _forward(x, c, mod_w, mod_b, norm1_gain, w_in, rwkv_shift_mu, rwkv_w0, rwkv_w_up, rwkv_a0,
             rwkv_a_up, rwkv_g_up, rwkv_k_k, rwkv_k_a, rwkv_r_k, rwkv_gn_gain, rwkv_gn_bias,
             diff_lambda_q1, diff_lambda_k1, diff_lambda_q2, diff_lambda_k2, diff_subln_gain,
             w_out, norm2_gain, router_w, router_b, w_gate_up, b_gate_up, w_down, b_down,
             final_gain):
    B, S, D = x.shape
    T = B * S
    tm_in = min(512, S)
    tq = min(512, S)
    tm_out = min(1024, S)
    tm = min(512, S)

    mod6 = _mod_call(c, mod_w, mod_b).reshape(B, 6, D)
    (r, k, v, a, b, g, lw, q, kat, vt) = _inproj_call(
        x, mod6, norm1_gain, w_in, rwkv_shift_mu, rwkv_w0, rwkv_w_up, rwkv_a0, rwkv_a_up,
        rwkv_g_up, rwkv_k_k, rwkv_k_a, tm_in)
    y_rwkv = _rwkv_call(r, k, v, a, b, g, lw, rwkv_r_k, rwkv_gn_gain, rwkv_gn_bias,
                        min(16, S // CHUNK))
    y_diff = _attn_call(q, kat, vt, diff_lambda_q1, diff_lambda_k1, diff_lambda_q2,
                        diff_lambda_k2, diff_subln_gain, tq)

    x1, hp, route, top_w, counts = _outproj_call(
        y_rwkv.reshape(T, -1), y_diff.reshape(T, -1), x.reshape(T, D), mod6, w_out, norm2_gain,
        router_w, router_b, S, tm_out)

    counts = counts.reshape(N_EXPERTS)
    padded = ((counts + MOE_BLOCK - 1) // MOE_BLOCK) * MOE_BLOCK
    pad_ends = jnp.cumsum(padded)
    pad_starts = pad_ends - padded
    n_blocks = -(-(T * TOP_K + N_EXPERTS * (MOE_BLOCK - 1)) // MOE_BLOCK)
    n_used = (pad_ends[-1] // MOE_BLOCK).astype(I32).reshape(1)
    block_start = jnp.minimum(jnp.arange(n_blocks, dtype=I32), n_used[0] - 1) * MOE_BLOCK
    block_e = jnp.minimum(jnp.sum(pad_ends[None, :] <= block_start[:, None], axis=1),
                          N_EXPERTS - 1).astype(I32)
    idx_t = route[:TOP_K]
    expert_ids = jnp.arange(N_EXPERTS, dtype=I32)[:, None, None]
    start_of = jnp.sum(jnp.where(idx_t[None] == expert_ids, pad_starts.astype(I32)[:, None, None], 0),
                       axis=0)
    slots = start_of + route[TOP_K:]
    slot_chunks = slots.reshape(TOP_K * T // SC_WINDOW, SC_WINDOW)
    block_row = (jnp.arange(n_blocks, dtype=I32) * MOE_BLOCK)[:, None]
    region_start = pad_starts.astype(I32)[None, :]
    inside = (region_start <= block_row) & (block_row < pad_ends.astype(I32)[None, :])
    tokens_end = region_start + counts.astype(I32)[None, :]
    valid_rows = jnp.sum(jnp.where(inside, jnp.clip(tokens_end - block_row, 0, MOE_BLOCK), 0),
                         axis=1).astype(I32)

    as_tokens = lambda z: z.reshape(-1, ROW_SUB, 128)
    as_lines = lambda z: z.reshape(-1, 128)
    xs = as_lines(_sc_scatter_call(as_tokens(hp), slot_chunks, n_blocks * MOE_BLOCK))
    ys = _experts_call(xs, block_e, n_used, valid_rows, w_gate_up, b_gate_up, w_down, b_down,
                       n_blocks)
    ys_tokens = as_tokens(ys)
    out = x1
    per_part = T // COMBINE_PARTS
    for part in range(COMBINE_PARTS):
        part_slots = slots[:, part * per_part:(part + 1) * per_part]
        part_chunks = part_slots.reshape(TOP_K * per_part // SC_WINDOW, SC_WINDOW)
        gathered = as_lines(_sc_gather_call(ys_tokens, part_chunks))
        out = _combine_dense_call(gathered, out, top_w, mod6, final_gain, S, tm, part, COMBINE_PARTS)
    return out.reshape(B, S, D)


def kernel(x, c, mod_w, mod_b, norm1_gain, w_in, rwkv_shift_mu, rwkv_w0, rwkv_w_up, rwkv_a0, rwkv_a_up, rwkv_g_up, rwkv_k_k, rwkv_k_a, rwkv_r_k, rwkv_gn_gain, rwkv_gn_bias, diff_lambda_q1, diff_lambda_k1, diff_lambda_q2, diff_lambda_k2, diff_subln_gain, w_out, norm2_gain, router_w, router_b, w_gate_up, b_gate_up, w_down, b_down, final_gain):
    return _forward(x, c, mod_w[0], mod_b[0], norm1_gain[0], w_in[0], rwkv_shift_mu[0], rwkv_w0[0],
                    rwkv_w_up[0], rwkv_a0[0], rwkv_a_up[0], rwkv_g_up[0], rwkv_k_k[0], rwkv_k_a[0],
                    rwkv_r_k[0].reshape(-1), rwkv_gn_gain[0], rwkv_gn_bias[0], diff_lambda_q1[0],
                    diff_lambda_k1[0], diff_lambda_q2[0], diff_lambda_k2[0], diff_subln_gain[0],
                    w_out[0], norm2_gain[0], router_w[0], router_b[0], w_gate_up[0], b_gate_up[0],
                    w_down[0], b_down[0], final_gain)
```

```python
import functools
import math

import jax
import jax.numpy as jnp
from jax import lax
from jax.experimental import pallas as pl
from jax.experimental.pallas import tpu as pltpu
from jax.experimental.pallas import tpu_sc as plsc

F32 = jnp.float32
BF16 = jnp.bfloat16
I32 = jnp.int32
U32 = jnp.uint32

D_MODEL = 1024
RWKV_WIDTH = 512
RWKV_HEADS = 8
HEAD_DIM = 64
RWKV_COLS = 3 * RWKV_WIDTH + 64 + 64 + 128
DIFF_WIDTH = 512
DIFF_HEADS = 4
DIFF_COLS = 3 * DIFF_WIDTH
N_EXPERTS = 32
TOP_K = 4
D_EXPERT = 1024
SWIGLU_LIMIT = 7.0
SWIGLU_ALPHA = 1.702
MOE_BLOCK = 512
NORM_EPS = 1e-5
GN_EPS = 64e-5
LAMBDA_INIT = 0.8 - 0.6 * math.exp(-0.3 * 0)

CHUNK = 64
GROUP = 4
GW = GROUP * HEAD_DIM
VT_ROWS = 128 + 16
KEY_SLABS = 2
LOOP_BLOCKS = 4
VMEM_LIMIT = 56 * 1024 * 1024


def _dot(a, b):
    return jnp.dot(a, b, preferred_element_type=F32)


def _dot_nt(a, b):
    return lax.dot_general(a, b, (((1,), (1,)), ((), ())), preferred_element_type=F32)


def _dot_tn(a, b):
    return lax.dot_general(a, b, (((0,), (0,)), ((), ())), preferred_element_type=F32)


def _round_robin(stages):
    waiting = list(stages)
    running = []
    while waiting or running:
        if waiting:
            running.append(waiting.pop(0))
        for gen in list(running):
            try:
                next(gen)
            except StopIteration:
                running.remove(gen)


def _split_dot(x, w_bf16):
    hi = x.astype(BF16)
    lo = (x - hi.astype(F32)).astype(BF16)
    return _dot(hi, w_bf16) + _dot(lo, w_bf16)


def _mod_kernel(c_ref, w_ref, b_ref, o_ref):
    c = c_ref[...]
    s = c * jax.nn.sigmoid(c)
    o_ref[...] = _dot(s, w_ref[...]) + b_ref[...]


def _mod_call(c, mod_w, mod_b):
    B = c.shape[0]
    n = mod_w.shape[1]
    tn = 1536
    return pl.pallas_call(
        _mod_kernel,
        grid=(n // tn,),
        in_specs=[pl.BlockSpec((B, D_MODEL), lambda j: (0, 0)),
                  pl.BlockSpec((D_MODEL, tn), lambda j: (0, j)),
                  pl.BlockSpec((1, tn), lambda j: (0, j))],
        out_specs=pl.BlockSpec((B, tn), lambda j: (0, j)),
        out_shape=jax.ShapeDtypeStruct((B, n), F32),
        compiler_params=pltpu.CompilerParams(
            dimension_semantics=("arbitrary",), vmem_limit_bytes=VMEM_LIMIT),
        name="mod",
    )(c, mod_w, mod_b.reshape(1, n))


def _inproj_kernel(x_ref, mod_ref, g1_ref, wrw_ref, wat_ref, mu_ref, w0_ref, wup_ref,
                   a0_ref, aup_ref, gup_ref, kk_ref, ka_ref, hsum_ref,
                   r_out, k_out, v_out, a_out, b_out, g_out, lw_out,
                   q_out, kat_out, vt_out, carry_ref, *, tm):
    s = pl.program_id(1)
    shift1 = mod_ref[0, 0:1, :]
    scale1 = mod_ref[0, 1:2, :]

    @pl.when(s == 0)
    def _():
        carry_ref[...] = jnp.zeros_like(carry_ref)

    n_part = max(1, tm // 128)
    rows_per = tm // n_part
    last_row = [carry_ref[...]]

    def part(i):
        rs = slice(i * rows_per, (i + 1) * rows_per)
        x = x_ref[0, rs, :]
        ms = jnp.mean(x * x, axis=-1, keepdims=True)
        h = x * lax.rsqrt(ms + NORM_EPS) * g1_ref[...] * (1.0 + scale1) + shift1
        hb = h.astype(BF16)
        yield
        p = _dot(hb, wrw_ref[...])
        last_row.append(p[rows_per - 1:rows_per, :])
        yield
        pa = _dot(hb, wat_ref[...])
        yield
        rolled = pltpu.roll(p, shift=1, axis=0)
        row = lax.broadcasted_iota(I32, p.shape, 0)
        prev = jnp.where(row == 0, last_row[i], rolled)
        ps = p + mu_ref[...] * (prev - p)
        r = ps[:, 0:512]
        k = ps[:, 512:1024]
        v = ps[:, 1024:1536]
        lo2 = ps[:, 1536:1664]
        g_lo = ps[:, 1664:1792]
        kk = k * kk_ref[...]
        yield
        z = w0_ref[...] + _dot(jnp.tanh(lo2).astype(BF16), wup_ref[...])
        a_pre = a0_ref[...] + _dot(lo2.astype(BF16), aup_ref[...])
        g = _dot(jax.nn.sigmoid(g_lo).astype(BF16), gup_ref[...])
        ssq = _split_dot(kk * kk, hsum_ref[...])
        yield
        nz = -z
        softplus = jnp.maximum(nz, 0.0) + jnp.log(1.0 + jnp.exp(-jnp.abs(nz)))
        w = -softplus - 0.5
        lw_out[0, rs, :] = -jnp.exp(w)
        a = jax.nn.sigmoid(a_pre)
        kk = kk / jnp.maximum(jnp.sqrt(ssq), 1e-12)
        k = k * (1.0 + (a - 1.0) * ka_ref[...])
        r_out[0, rs, :] = r.astype(BF16)
        k_out[0, rs, :] = k.astype(BF16)
        v_out[0, rs, :] = v.astype(BF16)
        a_out[0, rs, :] = (-kk).astype(BF16)
        b_out[0, rs, :] = (kk * a).astype(BF16)
        g_out[0, rs, :] = g.astype(BF16)
        yield
        q_out[0, rs, :] = (pa[:, 0:512] * (1.0 / math.sqrt(HEAD_DIM))).astype(BF16)
        kat_out[0, rs, :] = pa[:, 512:1024].astype(BF16)
        for hh in range(DIFF_HEADS):
            vh = pa[:, 1024 + hh * 128:1024 + (hh + 1) * 128]
            vt_out[0, hh, 0, 0:128, rs] = vh.T.astype(BF16)

    _round_robin([part(i) for i in range(n_part)])
    carry_ref[...] = last_row[n_part]
    for hh in range(DIFF_HEADS):
        vt_out[0, hh, 0, 128:VT_ROWS, :] = jnp.ones((VT_ROWS - 128, tm), BF16)


def _head_sum_matrix(width):
    i = jnp.arange(width) // HEAD_DIM
    return (i[:, None] == i[None, :]).astype(BF16)


def _inproj_call(x, mod6, norm1_gain, w_in, mu, w0, w_up, a0, a_up, g_up, k_k, k_a, tm):
    B, S, _ = x.shape
    ns = S // tm
    w_rw = w_in[:, :RWKV_COLS].astype(BF16)
    w_at = w_in[:, RWKV_COLS:].astype(BF16)
    zeros = jnp.zeros((64, RWKV_WIDTH), F32)
    wup_p = jnp.concatenate([w_up, zeros], axis=0).astype(BF16)
    aup_p = jnp.concatenate([zeros, a_up], axis=0).astype(BF16)
    row = lambda v: v.reshape(1, -1)
    full = lambda shape: pl.BlockSpec(shape, lambda b, s: (0,) * len(shape))
    tok = lambda w: pl.BlockSpec((1, tm, w), lambda b, s: (b, s, 0))
    rw_shape = jax.ShapeDtypeStruct((B, S, RWKV_WIDTH), BF16)
    out_shape = [rw_shape] * 6 + [
        jax.ShapeDtypeStruct((B, S, RWKV_WIDTH), F32),
        jax.ShapeDtypeStruct((B, S, DIFF_WIDTH), BF16),
        jax.ShapeDtypeStruct((B, S, DIFF_WIDTH), BF16),
        jax.ShapeDtypeStruct((B, DIFF_HEADS, ns, VT_ROWS, tm), BF16)]
    out_specs = [tok(RWKV_WIDTH)] * 7 + [tok(DIFF_WIDTH)] * 2 + [
        pl.BlockSpec((1, DIFF_HEADS, 1, VT_ROWS, tm), lambda b, s: (b, 0, s, 0, 0))]
    return pl.pallas_call(
        functools.partial(_inproj_kernel, tm=tm),
        grid=(B, ns),
        in_specs=[tok(D_MODEL),
                  pl.BlockSpec((1, 6, D_MODEL), lambda b, s: (b, 0, 0)),
                  full((1, D_MODEL)),
                  full((D_MODEL, RWKV_COLS)), full((D_MODEL, DIFF_COLS)),
                  full((1, RWKV_COLS)), full((1, RWKV_WIDTH)), full((128, RWKV_WIDTH)),
                  full((1, RWKV_WIDTH)), full((128, RWKV_WIDTH)), full((128, RWKV_WIDTH)),
                  full((1, RWKV_WIDTH)), full((1, RWKV_WIDTH)),
                  full((RWKV_WIDTH, RWKV_WIDTH))],
        out_specs=out_specs,
        out_shape=out_shape,
        scratch_shapes=[pltpu.VMEM((1, RWKV_COLS), F32)],
        compiler_params=pltpu.CompilerParams(
            dimension_semantics=("arbitrary", "arbitrary"), vmem_limit_bytes=VMEM_LIMIT),
        name="inproj",
    )(x, mod6, row(norm1_gain), w_rw, w_at, row(mu), row(w0), wup_p, row(a0), aup_p,
      g_up.astype(BF16), row(k_k), row(k_a), _head_sum_matrix(RWKV_WIDTH))


def _rwkv_kernel(r_ref, k_ref, v_ref, a_ref, b_ref, g_ref, lw_ref, rk_ref, gng_ref, gnb_ref,
                 hsum_ref, y_ref, s0_ref, s1_ref, *, nck):
    C = CHUNK
    n_grp = RWKV_HEADS // GROUP
    s_refs = (s0_ref, s1_ref)

    @pl.when(pl.program_id(1) == 0)
    def _():
        s0_ref[...] = jnp.zeros_like(s0_ref)
        s1_ref[...] = jnp.zeros_like(s1_ref)

    ti = lax.broadcasted_iota(I32, (C, GW), 0)
    si = lax.broadcasted_iota(I32, (C, GW), 1) % C
    incl = si <= ti
    strict = si < ti
    eye = (si == ti).astype(F32)
    bdmask = (lax.broadcasted_iota(I32, (GW, GW), 0) // HEAD_DIM ==
              lax.broadcasted_iota(I32, (GW, GW), 1) // HEAD_DIM)
    hsum = hsum_ref[...]
    chains = [(ck, g) for ck in range(nck) for g in range(n_grp)]

    bdmask_bf = bdmask.astype(BF16)

    def bd(xb):
        return jnp.concatenate([xb] * GROUP, axis=0) * bdmask_bf

    tri =(lax.broadcasted_iota(I32, (C, C), 0) >=
           lax.broadcasted_iota(I32, (C, C), 1)).astype(BF16)
    done = {}
    sv = [s_refs[g][...] for g in range(n_grp)]

    def chain(ch):
        ck, g = ch
        rows = slice(ck * C, (ck + 1) * C)
        cols = slice(g * GW, (g + 1) * GW)
        lw = lw_ref[0, rows, cols]
        L = _split_dot_left(tri, lw)
        yield
        Lx = L - lw
        Lc = L[C - 1:C, :]
        rho = L[C // 2 - 1:C // 2, :]
        r = r_ref[0, rows, cols].astype(F32)
        k = k_ref[0, rows, cols].astype(F32)
        vb = v_ref[0, rows, cols]
        a = a_ref[0, rows, cols].astype(F32)
        b = b_ref[0, rows, cols].astype(F32)
        e_k = jnp.exp(rho - L)
        lhs = jnp.concatenate([(r * jnp.exp(L - rho)).astype(BF16),
                               (a * jnp.exp(Lx - rho)).astype(BF16)], axis=0)
        kbd = bd((k * e_k).astype(BF16))
        bbd = bd((b * e_k).astype(BF16))
        r0 = (r * jnp.exp(L)).astype(BF16)
        a0bd = bd((a * jnp.exp(Lx)).astype(BF16))
        e_o = jnp.exp(Lc - L)
        bk = jnp.concatenate([(b * e_o).astype(BF16), (k * e_o).astype(BF16)], axis=0)
        vbd = bd(vb)
        yield
        ak = _dot_nt(lhs, kbd)
        ab = _dot_nt(lhs, bbd)
        yield
        a_rk = jnp.where(incl, ak[:C], 0.0).astype(BF16)
        a_ak = jnp.where(strict, ak[C:], 0.0).astype(BF16)
        a_rb = jnp.where(incl, ab[:C], 0.0).astype(BF16)
        n = jnp.where(strict, ab[C:], 0.0)
        p = eye + n
        xb = n.astype(BF16)
        xbd = bd(xb)
        yield
        x = _dot(xb, xbd)
        yield
        for _ in range(4):
            xb = x.astype(BF16)
            stack = jnp.concatenate([xb, p.astype(BF16)], axis=0)
            xbd = bd(xb)
            yield
            o = _dot(stack, xbd)
            yield
            x = o[:C]
            p = p + o[C:]
        pb = p.astype(BF16)
        xbd = bd(x.astype(BF16))
        yield
        o = _dot(pb, xbd)
        yield
        tb = (p + o).astype(BF16)
        yield
        av = _dot(jnp.concatenate([a_ak, a_rk], axis=0), vbd)
        w1 = _dot(tb, a0bd)
        yield
        akvbd = bd(av[:C].astype(BF16))
        rw_lhs = jnp.concatenate([r0, w1.astype(BF16)], axis=0)
        yield
        w2 = _dot(tb, akvbd)
        yield
        done[ch] = dict(rw_lhs=rw_lhs, w2=w2, a_rb=a_rb, arkv=av[C:], vb=vb, bk=bk,
                        gc=jnp.exp(Lc), rk=r * k)

    def sequential(ck):
        d = [done[(ck, g)] for g in range(n_grp)]
        rws = [_dot_nt(d[g]["rw_lhs"], sv[g].astype(BF16)) for g in range(n_grp)]
        yield
        ub = [(rws[g][C:] + d[g]["w2"]).astype(BF16) for g in range(n_grp)]
        uv = [jnp.concatenate([ub[g], d[g]["vb"]], axis=0) for g in range(n_grp)]
        ubd = [bd(ub[g]) for g in range(n_grp)]
        yield
        upd = [_dot_tn(uv[g], d[g]["bk"]) for g in range(n_grp)]
        yield
        for g in range(n_grp):
            sv[g] = sv[g] * d[g]["gc"] + jnp.where(bdmask, upd[g], 0.0)
        done[("state", ck)] = True
        y = [rws[g][:C] + _dot(d[g]["a_rb"], ubd[g]) + d[g]["arkv"] for g in range(n_grp)]
        yield
        mean = [_split_dot(y[g], hsum) * (1.0 / HEAD_DIM) for g in range(n_grp)]
        yield
        dev = [y[g] - mean[g] for g in range(n_grp)]
        var = [_split_dot(dev[g] * dev[g], hsum) * (1.0 / HEAD_DIM) for g in range(n_grp)]
        bonus = [_split_dot(d[g]["rk"] * rk_ref[:, g * GW:(g + 1) * GW], hsum) for g in range(n_grp)]
        yield
        rows = slice(ck * C, (ck + 1) * C)
        for g in range(n_grp):
            cols = slice(g * GW, (g + 1) * GW)
            yn = dev[g] * lax.rsqrt(var[g] + GN_EPS) * gng_ref[:, cols] + gnb_ref[:, cols]
            out = (yn + bonus[g] * d[g]["vb"].astype(F32)) * g_ref[0, rows, cols].astype(F32)
            y_ref[0, rows, cols] = out.astype(BF16)

    waiting = [chain(ch) for ch in chains]
    seq_next = 0
    running = []
    while waiting or running or seq_next < nck:
        ready = (seq_next < nck and all((seq_next, g) in done for g in range(n_grp))
                 and (seq_next == 0 or ("state", seq_next - 1) in done))
        if ready:
            running.append(sequential(seq_next))
            seq_next += 1
        elif waiting:
            running.append(waiting.pop(0))
        for gen in list(running):
            try:
                next(gen)
            except StopIteration:
                running.remove(gen)
    for g in range(n_grp):
        s_refs[g][...] = sv[g]


def _split_dot_left(w_bf16, x):
    hi = x.astype(BF16)
    lo = (x - hi.astype(F32)).astype(BF16)
    return _dot(w_bf16, hi) + _dot(w_bf16, lo)


def _rwkv_call(r, k, v, a, b, g, lw, r_k, gn_gain, gn_bias, nck):
    B, S, _ = r.shape
    tb = nck * CHUNK
    tok = pl.BlockSpec((1, tb, RWKV_WIDTH), lambda bb, c: (bb, c, 0))
    par = pl.BlockSpec((1, RWKV_WIDTH), lambda bb, c: (0, 0))
    return pl.pallas_call(
        functools.partial(_rwkv_kernel, nck=nck),
        grid=(B, S // tb),
        in_specs=[tok] * 7 + [par] * 3 + [pl.BlockSpec((GW, GW), lambda bb, c: (0, 0))],
        out_specs=tok,
        out_shape=jax.ShapeDtypeStruct((B, S, RWKV_WIDTH), BF16),
        scratch_shapes=[pltpu.VMEM((GW, GW), F32)] * (RWKV_HEADS // GROUP),
        compiler_params=pltpu.CompilerParams(
            dimension_semantics=("arbitrary", "arbitrary"), vmem_limit_bytes=VMEM_LIMIT),
        name="rwkv",
    )(r, k, v, a, b, g, lw, r_k.reshape(1, -1), gn_gain.reshape(1, -1), gn_bias.reshape(1, -1),
      _head_sum_matrix(GW))


def _tree_reduce(op, x):
    while x.shape[0] > 8:
        h = x.shape[0] // 2
        x = op(x[:h], x[h:])
    return x


def _attn_kernel(slopes_ref, q_ref, k_ref, vt_ref, lq1_ref, lk1_ref, lq2_ref, lk2_ref,
                 gain_ref, o_ref, sa_ref, sb_ref, acc_ref, m_ref, *, tq, tk):
    hh = pl.program_id(1)
    i = pl.program_id(2)
    slope = slopes_ref[hh]

    q = q_ref[0].astype(F32)
    lane = lax.broadcasted_iota(I32, (tq, 256), 1)
    qrow = lax.broadcasted_iota(I32, (tq, 256), 0)
    qa = (qrow >> 5).astype(F32) * (32.0 * slope)
    qb = (qrow & 31).astype(F32) * slope
    qbias = jnp.where(lane < 130, 1.0, jnp.where(lane == 130, -qa, jnp.where(lane == 131, -qb, 0.0)))
    qpad = jnp.concatenate([q, jnp.zeros_like(q)], axis=1)
    q_aug = []
    for c in range(2):
        in_comp = (lane >= c * HEAD_DIM) & (lane < (c + 1) * HEAD_DIM)
        q_aug.append(jnp.where(in_comp, qpad, jnp.where(lane >= 128, qbias, 0.0)))
    klane = lax.broadcasted_iota(I32, (tk, 128), 1)
    krow = lax.broadcasted_iota(I32, (tk, 128), 0)
    ka = (krow >> 5).astype(F32) * (32.0 * slope)
    kb_ = (krow & 31).astype(F32) * slope
    kbias = jnp.where(klane == 0, ka, jnp.where(klane == 1, kb_, jnp.where(klane < 4, 1.0, 0.0))
                      ).astype(BF16)

    qt_both = jnp.concatenate([q_aug[0].T, q_aug[1].T], axis=1).astype(BF16)

    m_ref[...] = jnp.full_like(m_ref, -jnp.inf)
    acc_ref[...] = jnp.zeros_like(acc_ref)

    def scores(j):
        kb = k_ref[0, pl.ds(pl.multiple_of(j * tk, tk), tk), :]
        return _dot(jnp.concatenate([kb, kbias], axis=1), qt_both)

    def absorb(s_ref, j, masked):
        s = s_ref[...]
        off = j * tk - i * tq
        cj = slope * off.astype(F32)
        if masked:
            keep = (lax.broadcasted_iota(I32, (tk, tq), 0) -
                    lax.broadcasted_iota(I32, (tk, tq), 1) + off) <= 0
            s = jnp.where(jnp.concatenate([keep, keep], axis=1), s, -jnp.inf)
        m_loc = jnp.max(_tree_reduce(jnp.maximum, s), axis=0, keepdims=True)
        m_old = m_ref[...]
        m_new = jnp.maximum(m_old, m_loc + cj)
        shift = m_new - cj
        vt = vt_ref[0, 0, j]
        rows = tk // KEY_SLABS
        pv = None
        for r in range(KEY_SLABS):
            pr = jnp.exp((s[r * rows:(r + 1) * rows] - shift).astype(BF16))
            part = _dot(vt[:, r * rows:(r + 1) * rows], pr)
            pv = part if pv is None else pv + part
        alpha = jnp.exp(m_old - m_new)
        acc_ref[...] = alpha * acc_ref[...] + pv
        m_ref[...] = m_new

    n_full = (i * tq) // tk
    sa_ref[...] = scores(0)
    bufs = (sa_ref, sb_ref)

    def run_full(j0, count):
        for b in range(count):
            bufs[(b + 1) % 2][...] = scores(j0 + b + 1)
            absorb(bufs[b % 2], j0 + b, False)

    def unrolled_body(jj, carry):
        run_full(jj * LOOP_BLOCKS, LOOP_BLOCKS)
        return carry

    lax.fori_loop(0, n_full // LOOP_BLOCKS, unrolled_body, 0)

    rest = n_full % LOOP_BLOCKS
    for k in range(LOOP_BLOCKS):
        @pl.when(rest == k)
        def _():
            run_full(n_full - k, k)
            absorb(bufs[k % 2], n_full, True)

    lam = (jnp.exp(jnp.sum(lq1_ref[...] * lk1_ref[...], axis=-1, keepdims=True))
           - jnp.exp(jnp.sum(lq2_ref[...] * lk2_ref[...], axis=-1, keepdims=True))
           + LAMBDA_INIT)
    o2 = acc_ref[0:128, :] * (1.0 / acc_ref[128:129, :])
    o = o2[:, :tq] - lam * o2[:, tq:]
    ot = o.T
    ms = jnp.mean(ot * ot, axis=-1, keepdims=True)
    y = ot * lax.rsqrt(ms + NORM_EPS) * gain_ref[...] * (1.0 - LAMBDA_INIT)
    o_ref[0] = y.astype(BF16)


def _attn_call(q, k, vt, lq1, lk1, lq2, lk2, subln_gain, tq):
    B, S, _ = q.shape
    ns, tk = vt.shape[2], vt.shape[4]
    slopes = jnp.asarray([2.0 ** (-8.0 * (i + 1) / DIFF_HEADS) for i in range(DIFF_HEADS)], F32)
    vec = lambda n: pl.BlockSpec((1, n), lambda b, h, i, sl: (0, 0))
    grid_spec = pltpu.PrefetchScalarGridSpec(
        num_scalar_prefetch=1,
        grid=(B, DIFF_HEADS, S // tq),
        in_specs=[pl.BlockSpec((1, tq, 128), lambda b, h, i, sl: (b, i, h)),
                  pl.BlockSpec((1, S, 128), lambda b, h, i, sl: (b, 0, h)),
                  pl.BlockSpec((1, 1, ns, VT_ROWS, tk), lambda b, h, i, sl: (b, h, 0, 0, 0)),
                  vec(HEAD_DIM), vec(HEAD_DIM), vec(HEAD_DIM), vec(HEAD_DIM), vec(128)],
        out_specs=pl.BlockSpec((1, tq, 128), lambda b, h, i, sl: (b, i, h)),
        scratch_shapes=[pltpu.VMEM((tk, 2 * tq), F32), pltpu.VMEM((tk, 2 * tq), F32),
                        pltpu.VMEM((VT_ROWS, 2 * tq), F32), pltpu.VMEM((1, 2 * tq), F32)])
    return pl.pallas_call(
        functools.partial(_attn_kernel, tq=tq, tk=tk),
        grid_spec=grid_spec,
        out_shape=jax.ShapeDtypeStruct((B, S, DIFF_WIDTH), BF16),
        compiler_params=pltpu.CompilerParams(
            dimension_semantics=("arbitrary", "arbitrary", "arbitrary"),
            vmem_limit_bytes=VMEM_LIMIT),
        name="attn",
    )(slopes, q, k, vt, lq1.reshape(1, -1), lk1.reshape(1, -1), lq2.reshape(1, -1),
      lk2.reshape(1, -1), subln_gain.reshape(1, -1))


def _pack_bf16_pairs(x):
    w = x.shape[1] // 2
    lo = pltpu.bitcast(x[:, :w].astype(BF16).astype(F32), U32)
    hi = pltpu.bitcast(x[:, w:].astype(BF16).astype(F32), U32)
    return (lo >> 16) | (hi & jnp.uint32(0xFFFF0000))


def _unpack_bf16_pairs(p):
    lo = pltpu.bitcast(p << 16, F32)
    hi = pltpu.bitcast(p & jnp.uint32(0xFFFF0000), F32)
    return lo, hi


ROW_SUB = (D_MODEL // 2) // 128


def _store_rows(ref, x2d):
    n = x2d.shape[0]
    for s in range(ROW_SUB):
        ref[pl.ds(s, n, stride=ROW_SUB), :] = x2d[:, s * 128:(s + 1) * 128]


def _load_rows(ref):
    n = ref.shape[0] // ROW_SUB
    return jnp.concatenate([ref[pl.ds(s, n, stride=ROW_SUB), :] for s in range(ROW_SUB)], axis=1)


def _outproj_kernel(yr_ref, yd_ref, x_ref, mod_ref, wo_r_ref, wo_d_ref, g2_ref, rw_hi_ref,
                    rw_lo_ref, rb_ref, x1_out, hp_out, route_out, w_out, cnt_out,
                    carry_ref, *, tm):
    t = pl.program_id(0)

    @pl.when(t == 0)
    def _():
        carry_ref[...] = jnp.zeros_like(carry_ref)

    gate1 = mod_ref[0, 2:3, :]
    shift2 = mod_ref[0, 3:4, :]
    scale2 = mod_ref[0, 4:5, :]
    n_part = max(1, tm // 256)
    rows_per = tm // n_part
    eidx = lax.broadcasted_iota(I32, (rows_per, N_EXPERTS), 1)
    col4 = lax.broadcasted_iota(I32, (rows_per, TOP_K), 1)
    lane128 = lax.broadcasted_iota(I32, (rows_per, 128), 1)
    tri = (lax.broadcasted_iota(I32, (rows_per, rows_per), 0) >
           lax.broadcasted_iota(I32, (rows_per, rows_per), 1)).astype(BF16)
    before = [carry_ref[...]]

    def part(i):
        rs = slice(i * rows_per, (i + 1) * rows_per)
        mix = _dot(yr_ref[rs, :], wo_r_ref[...]) + _dot(yd_ref[rs, :], wo_d_ref[...])
        yield
        x1 = x_ref[rs, :] + gate1 * mix
        x1_out[rs, :] = x1
        ms = jnp.mean(x1 * x1, axis=-1, keepdims=True)
        h = x1 * lax.rsqrt(ms + NORM_EPS) * g2_ref[...] * (1.0 + scale2) + shift2
        _store_rows(hp_out.at[pl.ds(i * rows_per * ROW_SUB, rows_per * ROW_SUB), :],
                    _pack_bf16_pairs(h))
        hi = h.astype(BF16)
        lo = (h - hi.astype(F32)).astype(BF16)
        yield
        logits = (_dot(hi, rw_hi_ref[...]) + _dot(hi, rw_lo_ref[...]) + _dot(lo, rw_hi_ref[...])
                  + rb_ref[...])
        yield
        lg = logits
        vals, idxs = [], []
        onehot = jnp.zeros(logits.shape, F32)
        for _ in range(TOP_K):
            m = jnp.max(lg, axis=-1, keepdims=True)
            ix = jnp.min(jnp.where(lg == m, eidx, N_EXPERTS), axis=-1, keepdims=True)
            sel = eidx == ix
            vals.append(m)
            idxs.append(ix)
            onehot = onehot + sel.astype(F32)
            lg = jnp.where(sel, -jnp.inf, lg)
        es = [jnp.exp(v - vals[0]) for v in vals]
        den = es[0] + es[1] + es[2] + es[3]
        before.append(before[i] + jnp.sum(onehot, axis=0, keepdims=True))
        yield
        prefix = _dot(tri, onehot.astype(BF16)) + before[i]
        yield
        w4 = jnp.zeros((rows_per, TOP_K), F32)
        table = jnp.zeros((rows_per, 128), F32)
        for kk in range(TOP_K):
            rk = jnp.sum(jnp.where(eidx == idxs[kk], prefix, 0.0), axis=-1, keepdims=True)
            w4 = jnp.where(col4 == kk, es[kk] / den, w4)
            table = jnp.where(lane128 == kk, idxs[kk].astype(F32), table)
            table = jnp.where(lane128 == TOP_K + kk, rk, table)
        w_out[rs, :] = w4
        route_out[:, rs] = table.T[0:2 * TOP_K, :].astype(I32)

    _round_robin([part(i) for i in range(n_part)])
    carry_ref[...] = before[n_part]
    cnt_out[...] = before[n_part].astype(I32)


def _outproj_call(y_rwkv, y_diff, x, mod6, w_out, norm2_gain, router_w, router_b, S, tm):
    T = x.shape[0]
    tiles_per_seq = S // tm
    rw_hi = router_w.astype(BF16)
    rw_lo = (router_w - rw_hi.astype(F32)).astype(BF16)
    tok = lambda w: pl.BlockSpec((tm, w), lambda t: (t, 0))
    full = lambda shape: pl.BlockSpec(shape, lambda t: (0,) * len(shape))
    return pl.pallas_call(
        functools.partial(_outproj_kernel, tm=tm),
        grid=(T // tm,),
        in_specs=[tok(RWKV_WIDTH), tok(DIFF_WIDTH), tok(D_MODEL),
                  pl.BlockSpec((1, 6, D_MODEL), lambda t: (t // tiles_per_seq, 0, 0)),
                  full((RWKV_WIDTH, D_MODEL)), full((DIFF_WIDTH, D_MODEL)), full((1, D_MODEL)),
                  full((D_MODEL, N_EXPERTS)), full((D_MODEL, N_EXPERTS)), full((1, N_EXPERTS))],
        out_specs=[tok(D_MODEL), pl.BlockSpec((tm * ROW_SUB, 128), lambda t: (t, 0)),
                   pl.BlockSpec((2 * TOP_K, tm), lambda t: (0, t)), tok(TOP_K),
                   full((1, N_EXPERTS))],
        out_shape=[jax.ShapeDtypeStruct((T, D_MODEL), F32),
                   jax.ShapeDtypeStruct((T * ROW_SUB, 128), U32),
                   jax.ShapeDtypeStruct((2 * TOP_K, T), I32),
                   jax.ShapeDtypeStruct((T, TOP_K), F32),
                   jax.ShapeDtypeStruct((1, N_EXPERTS), I32)],
        scratch_shapes=[pltpu.VMEM((1, N_EXPERTS), F32)],
        compiler_params=pltpu.CompilerParams(
            dimension_semantics=("arbitrary",), vmem_limit_bytes=VMEM_LIMIT),
        name="outproj",
    )(y_rwkv, y_diff, x, mod6, w_out[:RWKV_WIDTH].astype(BF16), w_out[RWKV_WIDTH:].astype(BF16),
      norm2_gain.reshape(1, -1), rw_hi, rw_lo, router_b.reshape(1, -1))


def _experts_kernel(be_ref, nu_ref, valid_ref, xs_ref, wgu_ref, bgu_ref, wd_ref, bd_ref, ys_ref,
                    wgu_bf, wd_bf):
    i = pl.program_id(0)

    @pl.when((i == 0) | (be_ref[i] != be_ref[jnp.maximum(i - 1, 0)]))
    def _():
        wgu_bf[...] = wgu_ref[0].astype(BF16)
        wd_bf[...] = wd_ref[0].astype(BF16)

    @pl.when(i < nu_ref[0])
    def _():
        packed = _load_rows(xs_ref)
        row = lax.broadcasted_iota(I32, packed.shape, 0)
        packed = jnp.where(row < valid_ref[i], packed, jnp.uint32(0))
        xa, xb = _unpack_bf16_pairs(packed)
        x = jnp.concatenate([xa.astype(BF16), xb.astype(BF16)], axis=1)
        gu = _dot(x, wgu_bf[...]) + bgu_ref[0]
        gate = jnp.minimum(gu[:, :D_EXPERT], SWIGLU_LIMIT)
        up = jnp.clip(gu[:, D_EXPERT:], -SWIGLU_LIMIT, SWIGLU_LIMIT)
        act = (up + 1.0) * (gate * jax.nn.sigmoid(SWIGLU_ALPHA * gate))
        y = _dot(act.astype(BF16), wd_bf[...]) + bd_ref[0]
        _store_rows(ys_ref, _pack_bf16_pairs(y))

    @pl.when(i >= nu_ref[0])
    def _():
        ys_ref[...] = jnp.zeros_like(ys_ref)


def _experts_call(xs, block_e, n_used, valid_rows, w_gate_up, b_gate_up, w_down, b_down, n_blocks):
    def outer(be_ref, nu_ref, valid_ref, xs_hbm, wgu_hbm, bgu_hbm, wd_hbm, bd_hbm, ys_hbm,
              wgu_bf, wd_bf):
        def body(xs_ref, wgu_ref, bgu_ref, wd_ref, bd_ref, ys_ref):
            _experts_kernel(be_ref, nu_ref, valid_ref, xs_ref, wgu_ref, bgu_ref, wd_ref, bd_ref,
                            ys_ref, wgu_bf, wd_bf)

        per_expert = lambda shape: pl.BlockSpec(shape, lambda i: (be_ref[i], 0, 0))
        ahead = lambda shape: pl.BlockSpec(shape, lambda i: (be_ref[i], 0, 0),
                                           pipeline_mode=pl.Buffered(2, use_lookahead=True))
        pltpu.emit_pipeline(
            body,
            grid=(n_blocks,),
            in_specs=[pl.BlockSpec((MOE_BLOCK * ROW_SUB, 128),
                                   lambda i: (jnp.minimum(i, nu_ref[0] - 1), 0)),
                      ahead((1, D_MODEL, 2 * D_EXPERT)), per_expert((1, 1, 2 * D_EXPERT)),
                      ahead((1, D_EXPERT, D_MODEL)), per_expert((1, 1, D_MODEL))],
            out_specs=[pl.BlockSpec((MOE_BLOCK * ROW_SUB, 128), lambda i: (i, 0))],
        )(xs_hbm, wgu_hbm, bgu_hbm, wd_hbm, bd_hbm, ys_hbm)

    smem = pl.BlockSpec(memory_space=pltpu.SMEM)
    hbm = pl.BlockSpec(memory_space=pl.ANY)
    return pl.pallas_call(
        outer,
        in_specs=[smem, smem, smem, hbm, hbm, hbm, hbm, hbm],
        out_specs=hbm,
        out_shape=jax.ShapeDtypeStruct((n_blocks * MOE_BLOCK * ROW_SUB, 128), U32),
        scratch_shapes=[pltpu.VMEM((D_MODEL, 2 * D_EXPERT), BF16),
                        pltpu.VMEM((D_EXPERT, D_MODEL), BF16)],
        compiler_params=pltpu.CompilerParams(vmem_limit_bytes=VMEM_LIMIT),
        name="experts",
    )(block_e, n_used, valid_rows, xs, w_gate_up, b_gate_up.reshape(N_EXPERTS, 1, -1), w_down,
      b_down.reshape(N_EXPERTS, 1, -1))


SC_WINDOW = 128
COMBINE_PARTS = 4


def _sc_mesh():
    return plsc.VectorSubcoreMesh(core_axis_name="c", subcore_axis_name="s")


def _sc_worker_chunks(n_chunks):
    info = plsc.get_sparse_core_info()
    n_workers = info.num_cores * info.num_subcores
    assert n_chunks % n_workers == 0
    return info.num_cores, n_chunks // n_workers


def _sc_scatter_call(rows, slot_chunks, n_out_rows):
    T = rows.shape[0]
    per_pass = T // SC_WINDOW
    n_cores, per_worker = _sc_worker_chunks(per_pass)

    @functools.partial(
        pl.kernel, mesh=_sc_mesh(),
        out_type=jax.ShapeDtypeStruct((n_out_rows, ROW_SUB, 128), U32),
        scratch_types=[pltpu.VMEM((SC_WINDOW,), I32), pltpu.VMEM((SC_WINDOW, ROW_SUB, 128), U32)],
        name="sc_dispatch")
    def run(rows_hbm, idx_hbm, out_hbm, idx_v, rows_v):
        wid = lax.axis_index("s") * n_cores + lax.axis_index("c")

        @pl.loop(0, per_worker)
        def _(j):
            chunk = wid * per_worker + j
            pltpu.sync_copy(rows_hbm.at[pl.ds(chunk * SC_WINDOW, SC_WINDOW)], rows_v)
            for kk in range(TOP_K):
                pltpu.sync_copy(idx_hbm.at[kk * per_pass + chunk], idx_v)
                pltpu.sync_copy(rows_v, out_hbm.at[idx_v])

    return run(rows, slot_chunks)


def _sc_gather_call(table, slot_chunks):
    n_chunks = slot_chunks.shape[0]
    n_cores, per_worker = _sc_worker_chunks(n_chunks)

    @functools.partial(
        pl.kernel, mesh=_sc_mesh(),
        out_type=jax.ShapeDtypeStruct((n_chunks * SC_WINDOW, ROW_SUB, 128), U32),
        scratch_types=[pltpu.VMEM((SC_WINDOW,), I32), pltpu.VMEM((SC_WINDOW, ROW_SUB, 128), U32)],
        name="sc_collect")
    def run(table_hbm, idx_hbm, out_hbm, idx_v, rows_v):
        wid = lax.axis_index("s") * n_cores + lax.axis_index("c")

        @pl.loop(0, per_worker)
        def _(j):
            chunk = wid * per_worker + j
            pltpu.sync_copy(idx_hbm.at[chunk], idx_v)
            pltpu.sync_copy(table_hbm.at[idx_v], rows_v)
            pltpu.sync_copy(rows_v, out_hbm.at[pl.ds(chunk * SC_WINDOW, SC_WINDOW)])

    return run(table, slot_chunks)


def _combine_dense_kernel(g0_ref, g1_ref, g2_ref, g3_ref, x1_ref, w_ref, mod_ref, fg_ref, o_ref,
                          *, tm):
    w = w_ref[...]
    acc_lo = jnp.zeros((tm, D_MODEL // 2), F32)
    acc_hi = jnp.zeros((tm, D_MODEL // 2), F32)
    for kk, g_ref in enumerate((g0_ref, g1_ref, g2_ref, g3_ref)):
        lo, hi = _unpack_bf16_pairs(_load_rows(g_ref))
        wk = w[:, kk:kk + 1]
        acc_lo = acc_lo + wk * lo
        acc_hi = acc_hi + wk * hi
    moe = jnp.concatenate([acc_lo, acc_hi], axis=1)
    gate2 = mod_ref[0, 5:6, :]
    x2 = x1_ref[...] + gate2 * moe
    ms = jnp.mean(x2 * x2, axis=-1, keepdims=True)
    o_ref[...] = x2 * lax.rsqrt(ms + NORM_EPS) * fg_ref[...]


def _combine_dense_call(gathered, x1, top_w, mod6, final_gain, S, tm, part, n_parts):
    T = x1.shape[0]
    assert T % (tm * n_parts) == 0
    nt = T // tm // n_parts
    first = part * nt
    tiles_per_seq = S // tm
    rows = lambda kk: pl.BlockSpec((tm * ROW_SUB, 128), lambda t: (kk * nt + t, 0))
    return pl.pallas_call(
        functools.partial(_combine_dense_kernel, tm=tm),
        grid=(nt,),
        in_specs=[rows(0), rows(1), rows(2), rows(3),
                  pl.BlockSpec((tm, D_MODEL), lambda t: (first + t, 0)),
                  pl.BlockSpec((tm, TOP_K), lambda t: (first + t, 0)),
                  pl.BlockSpec((1, 6, D_MODEL), lambda t: ((first + t) // tiles_per_seq, 0, 0)),
                  pl.BlockSpec((1, D_MODEL), lambda t: (0, 0))],
        out_specs=pl.BlockSpec((tm, D_MODEL), lambda t: (first + t, 0)),
        out_shape=jax.ShapeDtypeStruct((T, D_MODEL), F32),
        input_output_aliases={4: 0},
        compiler_params=pltpu.CompilerParams(
            dimension_semantics=("arbitrary",), vmem_limit_bytes=VMEM_LIMIT),
        name="combine",
    )(gathered, gathered, gathered, gathered, x1, top_w, mod6, final_gain.reshape(1, -1))


def _forward(x, c, mod_w, mod_b, norm1_gain, w_in, rwkv_shift_mu, rwkv_w0, rwkv_w_up, rwkv_a0,
             rwkv_a_up, rwkv_g_up, rwkv_k_k, rwkv_k_a, rwkv_r_k, rwkv_gn_gain, rwkv_gn_bias,
             diff_lambda_q1, diff_lambda_k1, diff_lambda_q2, diff_lambda_k2, diff_subln_gain,
             w_out, norm2_gain, router_w, router_b, w_gate_up, b_gate_up, w_down, b_down,
             final_gain):
    B, S, D = x.shape
    T = B * S
    tm_in = min(512, S)
    tq = min(512, S)
    tm_out = min(1024, S)
    tm = min(512, S)

    mod6 = _mod_call(c, mod_w, mod_b).reshape(B, 6, D)
    (r, k, v, a, b, g, lw, q, kat, vt) = _inproj_call(
        x, mod6, norm1_gain, w_in, rwkv_shift_mu, rwkv_w0, rwkv_w_up, rwkv_a0, rwkv_a_up,
        rwkv_g_up, rwkv_k_k, rwkv_k_a, tm_in)
    y_rwkv = _rwkv_call(r, k, v, a, b, g, lw, rwkv_r_k, rwkv_gn_gain, rwkv_gn_bias,
                        min(16, S // CHUNK))
    y_diff = _attn_call(q, kat, vt, diff_lambda_q1, diff_lambda_k1, diff_lambda_q2,
                        diff_lambda_k2, diff_subln_gain, tq)

    x1, hp, route, top_w, counts = _outproj_call(
        y_rwkv.reshape(T, -1), y_diff.reshape(T, -1), x.reshape(T, D), mod6, w_out, norm2_gain,
        router_w, router_b, S, tm_out)

    counts = counts.reshape(N_EXPERTS)
    padded = ((counts + MOE_BLOCK - 1) // MOE_BLOCK) * MOE_BLOCK
    pad_ends = jnp.cumsum(padded)
    pad_starts = pad_ends - padded
    n_blocks = -(-(T * TOP_K + N_EXPERTS * (MOE_BLOCK - 1)) // MOE_BLOCK)
    n_used = (pad_ends[-1] // MOE_BLOCK).astype(I32).reshape(1)
    block_start = jnp.minimum(jnp.arange(n_blocks, dtype=I32), n_used[0] - 1) * MOE_BLOCK
    block_e = jnp.minimum(jnp.sum(pad_ends[None, :] <= block_start[:, None], axis=1),
                          N_EXPERTS - 1).astype(I32)
    idx_t = route[:TOP_K]
    expert_ids = jnp.arange(N_EXPERTS, dtype=I32)[:, None, None]
    start_of = jnp.sum(jnp.where(idx_t[None] == expert_ids, pad_starts.astype(I32)[:, None, None], 0),
                       axis=0)
    slots = start_of + route[TOP_K:]
    slot_chunks = slots.reshape(TOP_K * T // SC_WINDOW, SC_WINDOW)
    block_row = (jnp.arange(n_blocks, dtype=I32) * MOE_BLOCK)[:, None]
    region_start = pad_starts.astype(I32)[None, :]
    inside = (region_start <= block_row) & (block_row < pad_ends.astype(I32)[None, :])
    tokens_end = region_start + counts.astype(I32)[None, :]
    valid_rows = jnp.sum(jnp.where(inside, jnp.clip(tokens_end - block_row, 0, MOE_BLOCK), 0),
                         axis=1).astype(I32)

    as_tokens = lambda z: z.reshape(-1, ROW_SUB, 128)
    as_lines = lambda z: z.reshape(-1, 128)
    xs = as_lines(_sc_scatter_call(as_tokens(hp), slot_chunks, n_blocks * MOE_BLOCK))
    ys = _experts_call(xs, block_e, n_used, valid_rows, w_gate_up, b_gate_up, w_down, b_down,
                       n_blocks)
    ys_tokens = as_tokens(ys)
    out = x1
    per_part = T // COMBINE_PARTS
    for part in range(COMBINE_PARTS):
        part_slots = slots[:, part * per_part:(part + 1) * per_part]
        part_chunks = part_slots.reshape(TOP_K * per_part // SC_WINDOW, SC_WINDOW)
        gathered = as_lines(_sc_gather_call(ys_tokens, part_chunks))
        out = _combine_dense_call(gathered, out, top_w, mod6, final_gain, S, tm, part, COMBINE_PARTS)
    return out.reshape(B, S, D)


def kernel(x, c, mod_w, mod_b, norm1_gain, w_in, rwkv_shift_mu, rwkv_w0, rwkv_w_up, rwkv_a0, rwkv_a_up, rwkv_g_up, rwkv_k_k, rwkv_k_a, rwkv_r_k, rwkv_gn_gain, rwkv_gn_bias, diff_lambda_q1, diff_lambda_k1, diff_lambda_q2, diff_lambda_k2, diff_subln_gain, w_out, norm2_gain, router_w, router_b, w_gate_up, b_gate_up, w_down, b_down, final_gain):
    return _forward(x, c, mod_w[0], mod_b[0], norm1_gain[0], w_in[0], rwkv_shift_mu[0], rwkv_w0[0],
                    rwkv_w_up[0], rwkv_a0[0], rwkv_a_up[0], rwkv_g_up[0], rwkv_k_k[0], rwkv_k_a[0],
                    rwkv_r_k[0].reshape(-1), rwkv_gn_gain[0], rwkv_gn_bias[0], diff_lambda_q1[0],
                    diff_lambda_k1[0], diff_lambda_q2[0], diff_lambda_k2[0], diff_subln_gain[0],
                    w_out[0], norm2_gain[0], router_w[0], router_b[0], w_gate_up[0], b_gate_up[0],
                    w_down[0], b_down[0], final_gain)
```

```python
import functools
import math

import jax
import jax.numpy as jnp
from jax import lax
from jax.experimental import pallas as pl
from jax.experimental.pallas import tpu as pltpu
from jax.experimental.pallas import tpu_sc as plsc

F32 = jnp.float32
BF16 = jnp.bfloat16
I32 = jnp.int32
U32 = jnp.uint32

D_MODEL = 1024
RWKV_WIDTH = 512
RWKV_HEADS = 8
HEAD_DIM = 64
RWKV_COLS = 3 * RWKV_WIDTH + 64 + 64 + 128
DIFF_WIDTH = 512
DIFF_HEADS = 4
DIFF_COLS = 3 * DIFF_WIDTH
N_EXPERTS = 32
TOP_K = 4
D_EXPERT = 1024
SWIGLU_LIMIT = 7.0
SWIGLU_ALPHA = 1.702
MOE_BLOCK = 512
NORM_EPS = 1e-5
GN_EPS = 64e-5
LAMBDA_INIT = 0.8 - 0.6 * math.exp(-0.3 * 0)

CHUNK = 64
GROUP = 4
GW = GROUP * HEAD_DIM
VT_ROWS = 128 + 16
KEY_SLABS = 2
LOOP_BLOCKS = 4
VMEM_LIMIT = 56 * 1024 * 1024


def _dot(a, b):
    return jnp.dot(a, b, preferred_element_type=F32)


def _dot_nt(a, b):
    return lax.dot_general(a, b, (((1,), (1,)), ((), ())), preferred_element_type=F32)


def _dot_tn(a, b):
    return lax.dot_general(a, b, (((0,), (0,)), ((), ())), preferred_element_type=F32)


def _round_robin(stages):
    waiting = list(stages)
    running = []
    while waiting or running:
        if waiting:
            running.append(waiting.pop(0))
        for gen in list(running):
            try:
                next(gen)
            except StopIteration:
                running.remove(gen)


def _split_dot(x, w_bf16):
    hi = x.astype(BF16)
    lo = (x - hi.astype(F32)).astype(BF16)
    return _dot(hi, w_bf16) + _dot(lo, w_bf16)


def _mod_kernel(c_ref, w_ref, b_ref, o_ref):
    c = c_ref[...]
    s = c * jax.nn.sigmoid(c)
    o_ref[...] = _dot(s, w_ref[...]) + b_ref[...]


def _mod_call(c, mod_w, mod_b):
    B = c.shape[0]
    n = mod_w.shape[1]
    tn = 1536
    return pl.pallas_call(
        _mod_kernel,
        grid=(n // tn,),
        in_specs=[pl.BlockSpec((B, D_MODEL), lambda j: (0, 0)),
                  pl.BlockSpec((D_MODEL, tn), lambda j: (0, j)),
                  pl.BlockSpec((1, tn), lambda j: (0, j))],
        out_specs=pl.BlockSpec((B, tn), lambda j: (0, j)),
        out_shape=jax.ShapeDtypeStruct((B, n), F32),
        compiler_params=pltpu.CompilerParams(
            dimension_semantics=("arbitrary",), vmem_limit_bytes=VMEM_LIMIT),
        name="mod",
    )(c, mod_w, mod_b.reshape(1, n))


def _inproj_kernel(x_ref, mod_ref, g1_ref, wrw_ref, wat_ref, mu_ref, w0_ref, wup_ref,
                   a0_ref, aup_ref, gup_ref, kk_ref, ka_ref, hsum_ref,
                   r_out, k_out, v_out, a_out, b_out, g_out, lw_out,
                   q_out, kat_out, vt_out, carry_ref, *, tm):
    s = pl.program_id(1)
    shift1 = mod_ref[0, 0:1, :]
    scale1 = mod_ref[0, 1:2, :]

    @pl.when(s == 0)
    def _():
        carry_ref[...] = jnp.zeros_like(carry_ref)

    n_part = max(1, tm // 128)
    rows_per = tm // n_part
    last_row = [carry_ref[...]]

    def part(i):
        rs = slice(i * rows_per, (i + 1) * rows_per)
        x = x_ref[0, rs, :]
        ms = jnp.mean(x * x, axis=-1, keepdims=True)
        h = x * lax.rsqrt(ms + NORM_EPS) * g1_ref[...] * (1.0 + scale1) + shift1
        hb = h.astype(BF16)
        yield
        p = _dot(hb, wrw_ref[...])
        last_row.append(p[rows_per - 1:rows_per, :])
        yield
        pa = _dot(hb, wat_ref[...])
        yield
        rolled = pltpu.roll(p, shift=1, axis=0)
        row = lax.broadcasted_iota(I32, p.shape, 0)
        prev = jnp.where(row == 0, last_row[i], rolled)
        ps = p + mu_ref[...] * (prev - p)
        r = ps[:, 0:512]
        k = ps[:, 512:1024]
        v = ps[:, 1024:1536]
        lo2 = ps[:, 1536:1664]
        g_lo = ps[:, 1664:1792]
        kk = k * kk_ref[...]
        yield
        z = w0_ref[...] + _dot(jnp.tanh(lo2).astype(BF16), wup_ref[...])
        a_pre = a0_ref[...] + _dot(lo2.astype(BF16), aup_ref[...])
        g = _dot(jax.nn.sigmoid(g_lo).astype(BF16), gup_ref[...])
        ssq = _split_dot(kk * kk, hsum_ref[...])
        yield
        nz = -z
        softplus = jnp.maximum(nz, 0.0) + jnp.log(1.0 + jnp.exp(-jnp.abs(nz)))
        w = -softplus - 0.5
        lw_out[0, rs, :] = -jnp.exp(w)
        a = jax.nn.sigmoid(a_pre)
        kk = kk / jnp.maximum(jnp.sqrt(ssq), 1e-12)
        k = k * (1.0 + (a - 1.0) * ka_ref[...])
        r_out[0, rs, :] = r.astype(BF16)
        k_out[0, rs, :] = k.astype(BF16)
        v_out[0, rs, :] = v.astype(BF16)
        a_out[0, rs, :] = (-kk).astype(BF16)
        b_out[0, rs, :] = (kk * a).astype(BF16)
        g_out[0, rs, :] = g.astype(BF16)
        yield
        q_out[0, rs, :] = (pa[:, 0:512] * (1.0 / math.sqrt(HEAD_DIM))).astype(BF16)
        kat_out[0, rs, :] = pa[:, 512:1024].astype(BF16)
        for hh in range(DIFF_HEADS):
            vh = pa[:, 1024 + hh * 128:1024 + (hh + 1) * 128]
            vt_out[0, hh, 0, 0:128, rs] = vh.T.astype(BF16)

    _round_robin([part(i) for i in range(n_part)])
    carry_ref[...] = last_row[n_part]
    for hh in range(DIFF_HEADS):
        vt_out[0, hh, 0, 128:VT_ROWS, :] = jnp.ones((VT_ROWS - 128, tm), BF16)


def _head_sum_matrix(width):
    i = jnp.arange(width) // HEAD_DIM
    return (i[:, None] == i[None, :]).astype(BF16)


def _inproj_call(x, mod6, norm1_gain, w_in, mu, w0, w_up, a0, a_up, g_up, k_k, k_a, tm):
    B, S, _ = x.shape
    ns = S // tm
    w_rw = w_in[:, :RWKV_COLS].astype(BF16)
    w_at = w_in[:, RWKV_COLS:].astype(BF16)
    zeros = jnp.zeros((64, RWKV_WIDTH), F32)
    wup_p = jnp.concatenate([w_up, zeros], axis=0).astype(BF16)
    aup_p = jnp.concatenate([zeros, a_up], axis=0).astype(BF16)
    row = lambda v: v.reshape(1, -1)
    full = lambda shape: pl.BlockSpec(shape, lambda b, s: (0,) * len(shape))
    tok = lambda w: pl.BlockSpec((1, tm, w), lambda b, s: (b, s, 0))
    rw_shape = jax.ShapeDtypeStruct((B, S, RWKV_WIDTH), BF16)
    out_shape = [rw_shape] * 6 + [
        jax.ShapeDtypeStruct((B, S, RWKV_WIDTH), F32),
        jax.ShapeDtypeStruct((B, S, DIFF_WIDTH), BF16),
        jax.ShapeDtypeStruct((B, S, DIFF_WIDTH), BF16),
        jax.ShapeDtypeStruct((B, DIFF_HEADS, ns, VT_ROWS, tm), BF16)]
    out_specs = [tok(RWKV_WIDTH)] * 7 + [tok(DIFF_WIDTH)] * 2 + [
        pl.BlockSpec((1, DIFF_HEADS, 1, VT_ROWS, tm), lambda b, s: (b, 0, s, 0, 0))]
    return pl.pallas_call(
        functools.partial(_inproj_kernel, tm=tm),
        grid=(B, ns),
        in_specs=[tok(D_MODEL),
                  pl.BlockSpec((1, 6, D_MODEL), lambda b, s: (b, 0, 0)),
                  full((1, D_MODEL)),
                  full((D_MODEL, RWKV_COLS)), full((D_MODEL, DIFF_COLS)),
                  full((1, RWKV_COLS)), full((1, RWKV_WIDTH)), full((128, RWKV_WIDTH)),
                  full((1, RWKV_WIDTH)), full((128, RWKV_WIDTH)), full((128, RWKV_WIDTH)),
                  full((1, RWKV_WIDTH)), full((1, RWKV_WIDTH)),
                  full((RWKV_WIDTH, RWKV_WIDTH))],
        out_specs=out_specs,
        out_shape=out_shape,
        scratch_shapes=[pltpu.VMEM((1, RWKV_COLS), F32)],
        compiler_params=pltpu.CompilerParams(
            dimension_semantics=("arbitrary", "arbitrary"), vmem_limit_bytes=VMEM_LIMIT),
        name="inproj",
    )(x, mod6, row(norm1_gain), w_rw, w_at, row(mu), row(w0), wup_p, row(a0), aup_p,
      g_up.astype(BF16), row(k_k), row(k_a), _head_sum_matrix(RWKV_WIDTH))


def _rwkv_kernel(r_ref, k_ref, v_ref, a_ref, b_ref, g_ref, lw_ref, rk_ref, gng_ref, gnb_ref,
                 hsum_ref, y_ref, s0_ref, s1_ref, *, nck):
    C = CHUNK
    n_grp = RWKV_HEADS // GROUP
    s_refs = (s0_ref, s1_ref)

    @pl.when(pl.program_id(1) == 0)
    def _():
        s0_ref[...] = jnp.zeros_like(s0_ref)
        s1_ref[...] = jnp.zeros_like(s1_ref)

    ti = lax.broadcasted_iota(I32, (C, GW), 0)
    si = lax.broadcasted_iota(I32, (C, GW), 1) % C
    incl = si <= ti
    strict = si < ti
    eye = (si == ti).astype(F32)
    bdmask = (lax.broadcasted_iota(I32, (GW, GW), 0) // HEAD_DIM ==
              lax.broadcasted_iota(I32, (GW, GW), 1) // HEAD_DIM)
    hsum = hsum_ref[...]
    chains = [(ck, g) for ck in range(nck) for g in range(n_grp)]

    bdmask_bf = bdmask.astype(BF16)

    def bd(xb):
        return jnp.concatenate([xb] * GROUP, axis=0) * bdmask_bf

    tri =(lax.broadcasted_iota(I32, (C, C), 0) >=
           lax.broadcasted_iota(I32, (C, C), 1)).astype(BF16)
    done = {}
    sv = [s_refs[g][...] for g in range(n_grp)]

    def chain(ch):
        ck, g = ch
        rows = slice(ck * C, (ck + 1) * C)
        cols = slice(g * GW, (g + 1) * GW)
        lw = lw_ref[0, rows, cols]
        L = _split_dot_left(tri, lw)
        yield
        Lx = L - lw
        Lc = L[C - 1:C, :]
        rho = L[C // 2 - 1:C // 2, :]
        r = r_ref[0, rows, cols].astype(F32)
        k = k_ref[0, rows, cols].astype(F32)
        vb = v_ref[0, rows, cols]
        a = a_ref[0, rows, cols].astype(F32)
        b = b_ref[0, rows, cols].astype(F32)
        e_k = jnp.exp(rho - L)
        lhs = jnp.concatenate([(r * jnp.exp(L - rho)).astype(BF16),
                               (a * jnp.exp(Lx - rho)).astype(BF16)], axis=0)
        kbd = bd((k * e_k).astype(BF16))
        bbd = bd((b * e_k).astype(BF16))
        r0 = (r * jnp.exp(L)).astype(BF16)
        a0bd = bd((a * jnp.exp(Lx)).astype(BF16))
        e_o = jnp.exp(Lc - L)
        bk = jnp.concatenate([(b * e_o).astype(BF16), (k * e_o).astype(BF16)], axis=0)
        vbd = bd(vb)
        yield
        ak = _dot_nt(lhs, kbd)
        ab = _dot_nt(lhs, bbd)
        yield
        a_rk = jnp.where(incl, ak[:C], 0.0).astype(BF16)
        a_ak = jnp.where(strict, ak[C:], 0.0).astype(BF16)
        a_rb = jnp.where(incl, ab[:C], 0.0).astype(BF16)
        n = jnp.where(strict, ab[C:], 0.0)
        p = eye + n
        xb = n.astype(BF16)
        xbd = bd(xb)
        yield
        x = _dot(xb, xbd)
        yield
        for _ in range(4):
            xb = x.astype(BF16)
            stack = jnp.concatenate([xb, p.astype(BF16)], axis=0)
            xbd = bd(xb)
            yield
            o = _dot(stack, xbd)
            yield
            x = o[:C]
            p = p + o[C:]
        pb = p.astype(BF16)
        xbd = bd(x.astype(BF16))
        yield
        o = _dot(pb, xbd)
        yield
        tb = (p + o).astype(BF16)
        yield
        av = _dot(jnp.concatenate([a_ak, a_rk], axis=0), vbd)
        w1 = _dot(tb, a0bd)
        yield
        akvbd = bd(av[:C].astype(BF16))
        rw_lhs = jnp.concatenate([r0, w1.astype(BF16)], axis=0)
        yield
        w2 = _dot(tb, akvbd)
        yield
        done[ch] = dict(rw_lhs=rw_lhs, w2=w2, a_rb=a_rb, arkv=av[C:], vb=vb, bk=bk,
                        gc=jnp.exp(Lc), rk=r * k)

    def sequential(ck):
        d = [done[(ck, g)] for g in range(n_grp)]
        rws = [_dot_nt(d[g]["rw_lhs"], sv[g].astype(BF16)) for g in range(n_grp)]
        yield
        ub = [(rws[g][C:] + d[g]["w2"]).astype(BF16) for g in range(n_grp)]
        uv = [jnp.concatenate([ub[g], d[g]["vb"]], axis=0) for g in range(n_grp)]
        ubd = [bd(ub[g]) for g in range(n_grp)]
        yield
        upd = [_dot_tn(uv[g], d[g]["bk"]) for g in range(n_grp)]
        yield
        for g in range(n_grp):
            sv[g] = sv[g] * d[g]["gc"] + jnp.where(bdmask, upd[g], 0.0)
        done[("state", ck)] = True
        y = [rws[g][:C] + _dot(d[g]["a_rb"], ubd[g]) + d[g]["arkv"] for g in range(n_grp)]
        yield
        mean = [_split_dot(y[g], hsum) * (1.0 / HEAD_DIM) for g in range(n_grp)]
        yield
        dev = [y[g] - mean[g] for g in range(n_grp)]
        var = [_split_dot(dev[g] * dev[g], hsum) * (1.0 / HEAD_DIM) for g in range(n_grp)]
        bonus = [_split_dot(d[g]["rk"] * rk_ref[:, g * GW:(g + 1) * GW], hsum) for g in range(n_grp)]
        yield
        rows = slice(ck * C, (ck + 1) * C)
        for g in range(n_grp):
            cols = slice(g * GW, (g + 1) * GW)
            yn = dev[g] * lax.rsqrt(var[g] + GN_EPS) * gng_ref[:, cols] + gnb_ref[:, cols]
            out = (yn + bonus[g] * d[g]["vb"].astype(F32)) * g_ref[0, rows, cols].astype(F32)
            y_ref[0, rows, cols] = out.astype(BF16)

    waiting = [chain(ch) for ch in chains]
    seq_next = 0
    running = []
    while waiting or running or seq_next < nck:
        ready = (seq_next < nck and all((seq_next, g) in done for g in range(n_grp))
                 and (seq_next == 0 or ("state", seq_next - 1) in done))
        if ready:
            running.append(sequential(seq_next))
            seq_next += 1
        elif waiting:
            running.append(waiting.pop(0))
        for gen in list(running):
            try:
                next(gen)
            except StopIteration:
                running.remove(gen)
    for g in range(n_grp):
        s_refs[g][...] = sv[g]


def _split_dot_left(w_bf16, x):
    hi = x.astype(BF16)
    lo = (x - hi.astype(F32)).astype(BF16)
    return _dot(w_bf16, hi) + _dot(w_bf16, lo)


def _rwkv_call(r, k, v, a, b, g, lw, r_k, gn_gain, gn_bias, nck):
    B, S, _ = r.shape
    tb = nck * CHUNK
    tok = pl.BlockSpec((1, tb, RWKV_WIDTH), lambda bb, c: (bb, c, 0))
    par = pl.BlockSpec((1, RWKV_WIDTH), lambda bb, c: (0, 0))
    return pl.pallas_call(
        functools.partial(_rwkv_kernel, nck=nck),
        grid=(B, S // tb),
        in_specs=[tok] * 7 + [par] * 3 + [pl.BlockSpec((GW, GW), lambda bb, c: (0, 0))],
        out_specs=tok,
        out_shape=jax.ShapeDtypeStruct((B, S, RWKV_WIDTH), BF16),
        scratch_shapes=[pltpu.VMEM((GW, GW), F32)] * (RWKV_HEADS // GROUP),
        compiler_params=pltpu.CompilerParams(
            dimension_semantics=("arbitrary", "arbitrary"), vmem_limit_bytes=VMEM_LIMIT),
        name="rwkv",
    )(r, k, v, a, b, g, lw, r_k.reshape(1, -1), gn_gain.reshape(1, -1), gn_bias.reshape(1, -1),
      _head_sum_matrix(GW))


def _tree_reduce(op, x):
    while x.shape[0] > 8:
        h = x.shape[0] // 2
        x = op(x[:h], x[h:])
    return x


def _attn_kernel(slopes_ref, q_ref, k_ref, vt_ref, lq1_ref, lk1_ref, lq2_ref, lk2_ref,
                 gain_ref, o_ref, sa_ref, sb_ref, acc_ref, m_ref, *, tq, tk):
    hh = pl.program_id(1)
    i = pl.program_id(2)
    slope = slopes_ref[hh]

    q = q_ref[0].astype(F32)
    lane = lax.broadcasted_iota(I32, (tq, 256), 1)
    qrow = lax.broadcasted_iota(I32, (tq, 256), 0)
    qa = (qrow >> 5).astype(F32) * (32.0 * slope)
    qb = (qrow & 31).astype(F32) * slope
    qbias = jnp.where(lane < 130, 1.0, jnp.where(lane == 130, -qa, jnp.where(lane == 131, -qb, 0.0)))
    qpad = jnp.concatenate([q, jnp.zeros_like(q)], axis=1)
    q_aug = []
    for c in range(2):
        in_comp = (lane >= c * HEAD_DIM) & (lane < (c + 1) * HEAD_DIM)
        q_aug.append(jnp.where(in_comp, qpad, jnp.where(lane >= 128, qbias, 0.0)))
    klane = lax.broadcasted_iota(I32, (tk, 128), 1)
    krow = lax.broadcasted_iota(I32, (tk, 128), 0)
    ka = (krow >> 5).astype(F32) * (32.0 * slope)
    kb_ = (krow & 31).astype(F32) * slope
    kbias = jnp.where(klane == 0, ka, jnp.where(klane == 1, kb_, jnp.where(klane < 4, 1.0, 0.0))
                      ).astype(BF16)

    qt_both = jnp.concatenate([q_aug[0].T, q_aug[1].T], axis=1).astype(BF16)

    m_ref[...] = jnp.full_like(m_ref, -jnp.inf)
    acc_ref[...] = jnp.zeros_like(acc_ref)

    def scores(j):
        kb = k_ref[0, pl.ds(pl.multiple_of(j * tk, tk), tk), :]
        return _dot(jnp.concatenate([kb, kbias], axis=1), qt_both)

    def absorb(s_ref, j, masked):
        s = s_ref[...]
        off = j * tk - i * tq
        cj = slope * off.astype(F32)
        if masked:
            keep = (lax.broadcasted_iota(I32, (tk, tq), 0) -
                    lax.broadcasted_iota(I32, (tk, tq), 1) + off) <= 0
            s = jnp.where(jnp.concatenate([keep, keep], axis=1), s, -jnp.inf)
        m_loc = jnp.max(_tree_reduce(jnp.maximum, s), axis=0, keepdims=True)
        m_old = m_ref[...]
        m_new = jnp.maximum(m_old, m_loc + cj)
        shift = m_new - cj
        vt = vt_ref[0, 0, j]
        rows = tk // KEY_SLABS
        pv = None
        for r in range(KEY_SLABS):
            pr = jnp.exp((s[r * rows:(r + 1) * rows] - shift).astype(BF16))
            part = _dot(vt[:, r * rows:(r + 1) * rows], pr)
            pv = part if pv is None else pv + part
        alpha = jnp.exp(m_old - m_new)
        acc_ref[...] = alpha * acc_ref[...] + pv
        m_ref[...] = m_new

    n_full = (i * tq) // tk
    sa_ref[...] = scores(0)
    bufs = (sa_ref, sb_ref)

    def run_full(j0, count):
        for b in range(count):
            bufs[(b + 1) % 2][...] = scores(j0 + b + 1)
            absorb(bufs[b % 2], j0 + b, False)

    def unrolled_body(jj, carry):
        run_full(jj * LOOP_BLOCKS, LOOP_BLOCKS)
        return carry

    lax.fori_loop(0, n_full // LOOP_BLOCKS, unrolled_body, 0)

    rest = n_full % LOOP_BLOCKS
    for k in range(LOOP_BLOCKS):
        @pl.when(rest == k)
        def _():
            run_full(n_full - k, k)
            absorb(bufs[k % 2], n_full, True)

    lam = (jnp.exp(jnp.sum(lq1_ref[...] * lk1_ref[...], axis=-1, keepdims=True))
           - jnp.exp(jnp.sum(lq2_ref[...] * lk2_ref[...], axis=-1, keepdims=True))
           + LAMBDA_INIT)
    o2 = acc_ref[0:128, :] * (1.0 / acc_ref[128:129, :])
    o = o2[:, :tq] - lam * o2[:, tq:]
    ot = o.T
    ms = jnp.mean(ot * ot, axis=-1, keepdims=True)
    y = ot * lax.rsqrt(ms + NORM_EPS) * gain_ref[...] * (1.0 - LAMBDA_INIT)
    o_ref[0] = y.astype(BF16)


def _attn_call(q, k, vt, lq1, lk1, lq2, lk2, subln_gain, tq):
    B, S, _ = q.shape
    ns, tk = vt.shape[2], vt.shape[4]
    slopes = jnp.asarray([2.0 ** (-8.0 * (i + 1) / DIFF_HEADS) for i in range(DIFF_HEADS)], F32)
    vec = lambda n: pl.BlockSpec((1, n), lambda b, h, i, sl: (0, 0))
    grid_spec = pltpu.PrefetchScalarGridSpec(
        num_scalar_prefetch=1,
        grid=(B, DIFF_HEADS, S // tq),
        in_specs=[pl.BlockSpec((1, tq, 128), lambda b, h, i, sl: (b, i, h)),
                  pl.BlockSpec((1, S, 128), lambda b, h, i, sl: (b, 0, h)),
                  pl.BlockSpec((1, 1, ns, VT_ROWS, tk), lambda b, h, i, sl: (b, h, 0, 0, 0)),
                  vec(HEAD_DIM), vec(HEAD_DIM), vec(HEAD_DIM), vec(HEAD_DIM), vec(128)],
        out_specs=pl.BlockSpec((1, tq, 128), lambda b, h, i, sl: (b, i, h)),
        scratch_shapes=[pltpu.VMEM((tk, 2 * tq), F32), pltpu.VMEM((tk, 2 * tq), F32),
                        pltpu.VMEM((VT_ROWS, 2 * tq), F32), pltpu.VMEM((1, 2 * tq), F32)])
    return pl.pallas_call(
        functools.partial(_attn_kernel, tq=tq, tk=tk),
        grid_spec=grid_spec,
        out_shape=jax.ShapeDtypeStruct((B, S, DIFF_WIDTH), BF16),
        compiler_params=pltpu.CompilerParams(
            dimension_semantics=("arbitrary", "arbitrary", "arbitrary"),
            vmem_limit_bytes=VMEM_LIMIT),
        name="attn",
    )(slopes, q, k, vt, lq1.reshape(1, -1), lk1.reshape(1, -1), lq2.reshape(1, -1),
      lk2.reshape(1, -1), subln_gain.reshape(1, -1))


def _pack_bf16_pairs(x):
    w = x.shape[1] // 2
    lo = pltpu.bitcast(x[:, :w].astype(BF16).astype(F32), U32)
    hi = pltpu.bitcast(x[:, w:].astype(BF16).astype(F32), U32)
    return (lo >> 16) | (hi & jnp.uint32(0xFFFF0000))


def _unpack_bf16_pairs(p):
    lo = pltpu.bitcast(p << 16, F32)
    hi = pltpu.bitcast(p & jnp.uint32(0xFFFF0000), F32)
    return lo, hi


ROW_SUB = (D_MODEL // 2) // 128


def _store_rows(ref, x2d):
    n = x2d.shape[0]
    for s in range(ROW_SUB):
        ref[pl.ds(s, n, stride=ROW_SUB), :] = x2d[:, s * 128:(s + 1) * 128]


def _load_rows(ref):
    n = ref.shape[0] // ROW_SUB
    return jnp.concatenate([ref[pl.ds(s, n, stride=ROW_SUB), :] for s in range(ROW_SUB)], axis=1)


def _outproj_kernel(yr_ref, yd_ref, x_ref, mod_ref, wo_r_ref, wo_d_ref, g2_ref, rw_hi_ref,
                    rw_lo_ref, rb_ref, x1_out, hp_out, route_out, w_out, cnt_out,
                    carry_ref, *, tm):
    t = pl.program_id(0)

    @pl.when(t == 0)
    def _():
        carry_ref[...] = jnp.zeros_like(carry_ref)

    gate1 = mod_ref[0, 2:3, :]
    shift2 = mod_ref[0, 3:4, :]
    scale2 = mod_ref[0, 4:5, :]
    n_part = max(1, tm // 256)
    rows_per = tm // n_part
    eidx = lax.broadcasted_iota(I32, (rows_per, N_EXPERTS), 1)
    col4 = lax.broadcasted_iota(I32, (rows_per, TOP_K), 1)
    lane128 = lax.broadcasted_iota(I32, (rows_per, 128), 1)
    tri = (lax.broadcasted_iota(I32, (rows_per, rows_per), 0) >
           lax.broadcasted_iota(I32, (rows_per, rows_per), 1)).astype(BF16)
    before = [carry_ref[...]]

    def part(i):
        rs = slice(i * rows_per, (i + 1) * rows_per)
        mix = _dot(yr_ref[rs, :], wo_r_ref[...]) + _dot(yd_ref[rs, :], wo_d_ref[...])
        yield
        x1 = x_ref[rs, :] + gate1 * mix
        x1_out[rs, :] = x1
        ms = jnp.mean(x1 * x1, axis=-1, keepdims=True)
        h = x1 * lax.rsqrt(ms + NORM_EPS) * g2_ref[...] * (1.0 + scale2) + shift2
        _store_rows(hp_out.at[pl.ds(i * rows_per * ROW_SUB, rows_per * ROW_SUB), :],
                    _pack_bf16_pairs(h))
        hi = h.astype(BF16)
        lo = (h - hi.astype(F32)).astype(BF16)
        yield
        logits = (_dot(hi, rw_hi_ref[...]) + _dot(hi, rw_lo_ref[...]) + _dot(lo, rw_hi_ref[...])
                  + rb_ref[...])
        yield
        lg = logits
        vals, idxs = [], []
        onehot = jnp.zeros(logits.shape, F32)
        for _ in range(TOP_K):
            m = jnp.max(lg, axis=-1, keepdims=True)
            ix = jnp.min(jnp.where(lg == m, eidx, N_EXPERTS), axis=-1, keepdims=True)
            sel = eidx == ix
            vals.append(m)
            idxs.append(ix)
            onehot = onehot + sel.astype(F32)
            lg = jnp.where(sel, -jnp.inf, lg)
        es = [jnp.exp(v - vals[0]) for v in vals]
        den = es[0] + es[1] + es[2] + es[3]
        before.append(before[i] + jnp.sum(onehot, axis=0, keepdims=True))
        yield
        prefix = _dot(tri, onehot.astype(BF16)) + before[i]
        yield
        w4 = jnp.zeros((rows_per, TOP_K), F32)
        table = jnp.zeros((rows_per, 128), F32)
        for kk in range(TOP_K):
            rk = jnp.sum(jnp.where(eidx == idxs[kk], prefix, 0.0), axis=-1, keepdims=True)
            w4 = jnp.where(col4 == kk, es[kk] / den, w4)
            table = jnp.where(lane128 == kk, idxs[kk].astype(F32), table)
            table = jnp.where(lane128 == TOP_K + kk, rk, table)
        w_out[rs, :] = w4
        route_out[:, rs] = table.T[0:2 * TOP_K, :].astype(I32)

    _round_robin([part(i) for i in range(n_part)])
    carry_ref[...] = before[n_part]
    cnt_out[...] = before[n_part].astype(I32)


def _outproj_call(y_rwkv, y_diff, x, mod6, w_out, norm2_gain, router_w, router_b, S, tm):
    T = x.shape[0]
    tiles_per_seq = S // tm
    rw_hi = router_w.astype(BF16)
    rw_lo = (router_w - rw_hi.astype(F32)).astype(BF16)
    tok = lambda w: pl.BlockSpec((tm, w), lambda t: (t, 0))
    full = lambda shape: pl.BlockSpec(shape, lambda t: (0,) * len(shape))
    return pl.pallas_call(
        functools.partial(_outproj_kernel, tm=tm),
        grid=(T // tm,),
        in_specs=[tok(RWKV_WIDTH), tok(DIFF_WIDTH), tok(D_MODEL),
                  pl.BlockSpec((1, 6, D_MODEL), lambda t: (t // tiles_per_seq, 0, 0)),
                  full((RWKV_WIDTH, D_MODEL)), full((DIFF_WIDTH, D_MODEL)), full((1, D_MODEL)),
                  full((D_MODEL, N_EXPERTS)), full((D_MODEL, N_EXPERTS)), full((1, N_EXPERTS))],
        out_specs=[tok(D_MODEL), pl.BlockSpec((tm * ROW_SUB, 128), lambda t: (t, 0)),
                   pl.BlockSpec((2 * TOP_K, tm), lambda t: (0, t)), tok(TOP_K),
                   full((1, N_EXPERTS))],
        out_shape=[jax.ShapeDtypeStruct((T, D_MODEL), F32),
                   jax.ShapeDtypeStruct((T * ROW_SUB, 128), U32),
                   jax.ShapeDtypeStruct((2 * TOP_K, T), I32),
                   jax.ShapeDtypeStruct((T, TOP_K), F32),
                   jax.ShapeDtypeStruct((1, N_EXPERTS), I32)],
        scratch_shapes=[pltpu.VMEM((1, N_EXPERTS), F32)],
        compiler_params=pltpu.CompilerParams(
            dimension_semantics=("arbitrary",), vmem_limit_bytes=VMEM_LIMIT),
        name="outproj",
    )(y_rwkv, y_diff, x, mod6, w_out[:RWKV_WIDTH].astype(BF16), w_out[RWKV_WIDTH:].astype(BF16),
      norm2_gain.reshape(1, -1), rw_hi, rw_lo, router_b.reshape(1, -1))


def _experts_kernel(be_ref, nu_ref, valid_ref, xs_ref, wgu_ref, bgu_ref, wd_ref, bd_ref, ys_ref,
                    wgu_bf, wd_bf):
    i = pl.program_id(0)

    @pl.when((i == 0) | (be_ref[i] != be_ref[jnp.maximum(i - 1, 0)]))
    def _():
        wgu_bf[...] = wgu_ref[0].astype(BF16)
        wd_bf[...] = wd_ref[0].astype(BF16)

    @pl.when(i < nu_ref[0])
    def _():
        packed = _load_rows(xs_ref)
        row = lax.broadcasted_iota(I32, packed.shape, 0)
        packed = jnp.where(row < valid_ref[i], packed, jnp.uint32(0))
        xa, xb = _unpack_bf16_pairs(packed)
        x = jnp.concatenate([xa.astype(BF16), xb.astype(BF16)], axis=1)
        gu = _dot(x, wgu_bf[...]) + bgu_ref[0]
        gate = jnp.minimum(gu[:, :D_EXPERT], SWIGLU_LIMIT)
        up = jnp.clip(gu[:, D_EXPERT:], -SWIGLU_LIMIT, SWIGLU_LIMIT)
        act = (up + 1.0) * (gate * jax.nn.sigmoid(SWIGLU_ALPHA * gate))
        y = _dot(act.astype(BF16), wd_bf[...]) + bd_ref[0]
        _store_rows(ys_ref, _pack_bf16_pairs(y))

    @pl.when(i >= nu_ref[0])
    def _():
        ys_ref[...] = jnp.zeros_like(ys_ref)


def _experts_call(xs, block_e, n_used, valid_rows, w_gate_up, b_gate_up, w_down, b_down, n_blocks):
    def outer(be_ref, nu_ref, valid_ref, xs_hbm, wgu_hbm, bgu_hbm, wd_hbm, bd_hbm, ys_hbm,
              wgu_bf, wd_bf):
        def body(xs_ref, wgu_ref, bgu_ref, wd_ref, bd_ref, ys_ref):
            _experts_kernel(be_ref, nu_ref, valid_ref, xs_ref, wgu_ref, bgu_ref, wd_ref, bd_ref,
                            ys_ref, wgu_bf, wd_bf)

        per_expert = lambda shape: pl.BlockSpec(shape, lambda i: (be_ref[i], 0, 0))
        ahead = lambda shape: pl.BlockSpec(shape, lambda i: (be_ref[i], 0, 0),
                                           pipeline_mode=pl.Buffered(2, use_lookahead=True))
        pltpu.emit_pipeline(
            body,
            grid=(n_blocks,),
            in_specs=[pl.BlockSpec((MOE_BLOCK * ROW_SUB, 128),
                                   lambda i: (jnp.minimum(i, nu_ref[0] - 1), 0),
                                   pipeline_mode=pl.Buffered(3)),
                      ahead((1, D_MODEL, 2 * D_EXPERT)), per_expert((1, 1, 2 * D_EXPERT)),
                      ahead((1, D_EXPERT, D_MODEL)), per_expert((1, 1, D_MODEL))],
            out_specs=[pl.BlockSpec((MOE_BLOCK * ROW_SUB, 128), lambda i: (i, 0))],
        )(xs_hbm, wgu_hbm, bgu_hbm, wd_hbm, bd_hbm, ys_hbm)

    smem = pl.BlockSpec(memory_space=pltpu.SMEM)
    hbm = pl.BlockSpec(memory_space=pl.ANY)
    return pl.pallas_call(
        outer,
        in_specs=[smem, smem, smem, hbm, hbm, hbm, hbm, hbm],
        out_specs=hbm,
        out_shape=jax.ShapeDtypeStruct((n_blocks * MOE_BLOCK * ROW_SUB, 128), U32),
        scratch_shapes=[pltpu.VMEM((D_MODEL, 2 * D_EXPERT), BF16),
                        pltpu.VMEM((D_EXPERT, D_MODEL), BF16)],
        compiler_params=pltpu.CompilerParams(vmem_limit_bytes=VMEM_LIMIT),
        name="experts",
    )(block_e, n_used, valid_rows, xs, w_gate_up, b_gate_up.reshape(N_EXPERTS, 1, -1), w_down,
      b_down.reshape(N_EXPERTS, 1, -1))


SC_WINDOW = 128
COMBINE_PARTS = 4


def _sc_mesh():
    return plsc.VectorSubcoreMesh(core_axis_name="c", subcore_axis_name="s")


def _sc_worker_chunks(n_chunks):
    info = plsc.get_sparse_core_info()
    n_workers = info.num_cores * info.num_subcores
    assert n_chunks % n_workers == 0
    return info.num_cores, n_chunks // n_workers


def _sc_scatter_call(rows, slot_chunks, n_out_rows):
    T = rows.shape[0]
    per_pass = T // SC_WINDOW
    n_cores, per_worker = _sc_worker_chunks(per_pass)

    @functools.partial(
        pl.kernel, mesh=_sc_mesh(),
        out_type=jax.ShapeDtypeStruct((n_out_rows, ROW_SUB, 128), U32),
        scratch_types=[pltpu.VMEM((SC_WINDOW,), I32), pltpu.VMEM((SC_WINDOW, ROW_SUB, 128), U32)],
        name="sc_dispatch")
    def run(rows_hbm, idx_hbm, out_hbm, idx_v, rows_v):
        wid = lax.axis_index("s") * n_cores + lax.axis_index("c")

        @pl.loop(0, per_worker)
        def _(j):
            chunk = wid * per_worker + j
            pltpu.sync_copy(rows_hbm.at[pl.ds(chunk * SC_WINDOW, SC_WINDOW)], rows_v)
            for kk in range(TOP_K):
                pltpu.sync_copy(idx_hbm.at[kk * per_pass + chunk], idx_v)
                pltpu.sync_copy(rows_v, out_hbm.at[idx_v])

    return run(rows, slot_chunks)


def _sc_gather_call(table, slot_chunks):
    n_chunks = slot_chunks.shape[0]
    n_cores, per_worker = _sc_worker_chunks(n_chunks)

    @functools.partial(
        pl.kernel, mesh=_sc_mesh(),
        out_type=jax.ShapeDtypeStruct((n_chunks * SC_WINDOW, ROW_SUB, 128), U32),
        scratch_types=[pltpu.VMEM((SC_WINDOW,), I32), pltpu.VMEM((SC_WINDOW, ROW_SUB, 128), U32)],
        name="sc_collect")
    def run(table_hbm, idx_hbm, out_hbm, idx_v, rows_v):
        wid = lax.axis_index("s") * n_cores + lax.axis_index("c")

        @pl.loop(0, per_worker)
        def _(j):
            chunk = wid * per_worker + j
            pltpu.sync_copy(idx_hbm.at[chunk], idx_v)
            pltpu.sync_copy(table_hbm.at[idx_v], rows_v)
            pltpu.sync_copy(rows_v, out_hbm.at[pl.ds(chunk * SC_WINDOW, SC_WINDOW)])

    return run(table, slot_chunks)


def _combine_dense_kernel(g0_ref, g1_ref, g2_ref, g3_ref, x1_ref, w_ref, mod_ref, fg_ref, o_ref,
                          *, tm):
    w = w_ref[...]
    acc_lo = jnp.zeros((tm, D_MODEL // 2), F32)
    acc_hi = jnp.zeros((tm, D_MODEL // 2), F32)
    for kk, g_ref in enumerate((g0_ref, g1_ref, g2_ref, g3_ref)):
        lo, hi = _unpack_bf16_pairs(_load_rows(g_ref))
        wk = w[:, kk:kk + 1]
        acc_lo = acc_lo + wk * lo
        acc_hi = acc_hi + wk * hi
    moe = jnp.concatenate([acc_lo, acc_hi], axis=1)
    gate2 = mod_ref[0, 5:6, :]
    x2 = x1_ref[...] + gate2 * moe
    ms = jnp.mean(x2 * x2, axis=-1, keepdims=True)
    o_ref[...] = x2 * lax.rsqrt(ms + NORM_EPS) * fg_ref[...]


def _combine_dense_call(gathered, x1, top_w, mod6, final_gain, S, tm, part, n_parts):
    T = x1.shape[0]
    assert T % (tm * n_parts) == 0
    nt = T // tm // n_parts
    first = part * nt
    tiles_per_seq = S // tm
    rows = lambda kk: pl.BlockSpec((tm * ROW_SUB, 128), lambda t: (kk * nt + t, 0))
    return pl.pallas_call(
        functools.partial(_combine_dense_kernel, tm=tm),
        grid=(nt,),
        in_specs=[rows(0), rows(1), rows(2), rows(3),
                  pl.BlockSpec((tm, D_MODEL), lambda t: (first + t, 0)),
                  pl.BlockSpec((tm, TOP_K), lambda t: (first + t, 0)),
                  pl.BlockSpec((1, 6, D_MODEL), lambda t: ((first + t) // tiles_per_seq, 0, 0)),
                  pl.BlockSpec((1, D_MODEL), lambda t: (0, 0))],
        out_specs=pl.BlockSpec((tm, D_MODEL), lambda t: (first + t, 0)),
        out_shape=jax.ShapeDtypeStruct((T, D_MODEL), F32),
        input_output_aliases={4: 0},
        compiler_params=pltpu.CompilerParams(
            dimension_semantics=("arbitrary",), vmem_limit_bytes=VMEM_LIMIT),
        name="combine",
    )(gathered, gathered, gathered, gathered, x1, top_w, mod6, final_gain.reshape(1, -1))


def _forward(x, c, mod_w, mod_b, norm1_gain, w_in, rwkv_shift_mu, rwkv_w0, rwkv_w_up, rwkv_a0,
             rwkv_a_up, rwkv_g_up, rwkv_k_k, rwkv_k_a, rwkv_r_k, rwkv_gn_gain, rwkv_gn_bias,
             diff_lambda_q1, diff_lambda_k1, diff_lambda_q2, diff_lambda_k2, diff_subln_gain,
             w_out, norm2_gain, router_w, router_b, w_gate_up, b_gate_up, w_down, b_down,
             final_gain):
    B, S, D = x.shape
    T = B * S
    tm_in = min(512, S)
    tq = min(512, S)
    tm_out = min(1024, S)
    tm = min(512, S)

    mod6 = _mod_call(c, mod_w, mod_b).reshape(B, 6, D)
    (r, k, v, a, b, g, lw, q, kat, vt) = _inproj_call(
        x, mod6, norm1_gain, w_in, rwkv_shift_mu, rwkv_w0, rwkv_w_up, rwkv_a0, rwkv_a_up,
        rwkv_g_up, rwkv_k_k, rwkv_k_a, tm_in)
    y_rwkv = _rwkv_call(r, k, v, a, b, g, lw, rwkv_r_k, rwkv_gn_gain, rwkv_gn_bias,
                        min(16, S // CHUNK))
    y_diff = _attn_call(q, kat, vt, diff_lambda_q1, diff_lambda_k1, diff_lambda_q2,
                        diff_lambda_k2, diff_subln_gain, tq)

    x1, hp, route, top_w, counts = _outproj_call(
        y_rwkv.reshape(T, -1), y_diff.reshape(T, -1), x.reshape(T, D), mod6, w_out, norm2_gain,
        router_w, router_b, S, tm_out)

    counts = counts.reshape(N_EXPERTS)
    padded = ((counts + MOE_BLOCK - 1) // MOE_BLOCK) * MOE_BLOCK
    pad_ends = jnp.cumsum(padded)
    pad_starts = pad_ends - padded
    n_blocks = -(-(T * TOP_K + N_EXPERTS * (MOE_BLOCK - 1)) // MOE_BLOCK)
    n_used = (pad_ends[-1] // MOE_BLOCK).astype(I32).reshape(1)
    block_start = jnp.minimum(jnp.arange(n_blocks, dtype=I32), n_used[0] - 1) * MOE_BLOCK
    block_e = jnp.minimum(jnp.sum(pad_ends[None, :] <= block_start[:, None], axis=1),
                          N_EXPERTS - 1).astype(I32)
    idx_t = route[:TOP_K]
    expert_ids = jnp.arange(N_EXPERTS, dtype=I32)[:, None, None]
    start_of = jnp.sum(jnp.where(idx_t[None] == expert_ids, pad_starts.astype(I32)[:, None, None], 0),
                       axis=0)
    slots = start_of + route[TOP_K:]
    slot_chunks = slots.reshape(TOP_K * T // SC_WINDOW, SC_WINDOW)
    block_row = (jnp.arange(n_blocks, dtype=I32) * MOE_BLOCK)[:, None]
    region_start = pad_starts.astype(I32)[None, :]
    inside = (region_start <= block_row) & (block_row < pad_ends.astype(I32)[None, :])
    tokens_end = region_start + counts.astype(I32)[None, :]
    valid_rows = jnp.sum(jnp.where(inside, jnp.clip(tokens_end - block_row, 0, MOE_BLOCK), 0),
                         axis=1).astype(I32)

    as_tokens = lambda z: z.reshape(-1, ROW_SUB, 128)
    as_lines = lambda z: z.reshape(-1, 128)
    xs = as_lines(_sc_scatter_call(as_tokens(hp), slot_chunks, n_blocks * MOE_BLOCK))
    ys = _experts_call(xs, block_e, n_used, valid_rows, w_gate_up, b_gate_up, w_down, b_down,
                       n_blocks)
    ys_tokens = as_tokens(ys)
    out = x1
    per_part = T // COMBINE_PARTS
    for part in range(COMBINE_PARTS):
        part_slots = slots[:, part * per_part:(part + 1) * per_part]
        part_chunks = part_slots.reshape(TOP_K * per_part // SC_WINDOW, SC_WINDOW)
        gathered = as_lines(_sc_gather_call(ys_tokens, part_chunks))
        out = _combine_dense_call(gathered, out, top_w, mod6, final_gain, S, tm, part, COMBINE_PARTS)
    return out.reshape(B, S, D)


def kernel(x, c, mod_w, mod_b, norm1_gain, w_in, rwkv_shift_mu, rwkv_w0, rwkv_w_up, rwkv_a0, rwkv_a_up, rwkv_g_up, rwkv_k_k, rwkv_k_a, rwkv_r_k, rwkv_gn_gain, rwkv_gn_bias, diff_lambda_q1, diff_lambda_k1, diff_lambda_q2, diff_lambda_k2, diff_subln_gain, w_out, norm2_gain, router_w, router_b, w_gate_up, b_gate_up, w_down, b_down, final_gain):
    return _forward(x, c, mod_w[0], mod_b[0], norm1_gain[0], w_in[0], rwkv_shift_mu[0], rwkv_w0[0],
                    rwkv_w_up[0], rwkv_a0[0], rwkv_a_up[0], rwkv_g_up[0], rwkv_k_k[0], rwkv_k_a[0],
                    rwkv_r_k[0].reshape(-1), rwkv_gn_gain[0], rwkv_gn_bias[0], diff_lambda_q1[0],
                    diff_lambda_k1[0], diff_lambda_q2[0], diff_lambda_k2[0], diff_subln_gain[0],
                    w_out[0], norm2_gain[0], router_w[0], router_b[0], w_gate_up[0], b_gate_up[0],
                    w_down[0], b_down[0], final_gain)
```
